```python
import math
import jax, jax.numpy as jnp
from jax import lax
import numpy as np

D_MODEL = 1024
BATCH = 8
SEQ = 4096
DEPTH = 1

MIX_WIDTH = 2 * D_MODEL
SSD_WIDTH = MIX_WIDTH // 2
SSD_HEAD_DIM = 64
SSD_HEADS = SSD_WIDTH // SSD_HEAD_DIM
SSD_GROUPS = 2
SSD_STATE = 128
SSD_CONV = 4
CHUNK = 128
CONF_WIDTH = MIX_WIDTH - SSD_WIDTH
CONF_KERNEL = 31
D_FF = 4 * D_MODEL
PLE_DIM = 256
EPS = 1e-6
XBC_WIDTH = SSD_WIDTH + 2 * SSD_GROUPS * SSD_STATE
IN_WIDTH = SSD_WIDTH + XBC_WIDTH + SSD_HEADS + 2 * CONF_WIDTH

kernel_name = "hybrid_ssd_conformer_block"


def rmsnorm(x, g):
    xf = x.astype(jnp.float32)
    y = xf * lax.rsqrt(jnp.mean(xf * xf, axis=-1, keepdims=True) + EPS)
    return (y * g.astype(jnp.float32)).astype(x.dtype)


def gated_group_rmsnorm(y, z, g):
    v = (y * jax.nn.silu(z)).astype(jnp.float32)
    shp = v.shape
    v = v.reshape(shp[:-1] + (SSD_GROUPS, shp[-1] // SSD_GROUPS))
    v = v * lax.rsqrt(jnp.mean(v * v, axis=-1, keepdims=True) + EPS)
    return (v.reshape(shp) * g.astype(jnp.float32)).astype(y.dtype)


def layernorm(x, g, b):
    xf = x.astype(jnp.float32)
    mu = jnp.mean(xf, axis=-1, keepdims=True)
    xc = xf - mu
    y = xc * lax.rsqrt(jnp.mean(xc * xc, axis=-1, keepdims=True) + EPS)
    return (y * g.astype(jnp.float32) + b.astype(jnp.float32)).astype(x.dtype)


def causal_depthwise_conv(x, w, b):
    k = w.shape[0]
    y = lax.conv_general_dilated(
        x, w[:, None, :].astype(x.dtype), window_strides=(1,), padding=[(k - 1, 0)],
        dimension_numbers=('NWC', 'WIO', 'NWC'), feature_group_count=x.shape[-1])
    return y + b.astype(x.dtype)


def ssd_chunked(x, dt, A, Bm, Cm):
    b, l, h, p = x.shape
    g, n = Bm.shape[-2:]
    e = h // g
    c = l // CHUNK
    dtype = x.dtype
    Xc = (x * dt[..., None].astype(dtype)).reshape(b, c, CHUNK, g, e, p)
    Bc = Bm.reshape(b, c, CHUNK, g, n)
    Cc = Cm.reshape(b, c, CHUNK, g, n)
    a = jnp.transpose((dt * A).reshape(b, c, CHUNK, g, e), (0, 3, 4, 1, 2))
    a_cs = jnp.cumsum(a, axis=-1)
    causal = jnp.tril(jnp.ones((CHUNK, CHUNK), dtype=bool))
    seg = a_cs[..., :, None] - a_cs[..., None, :]
    Lmat = jnp.exp(jnp.where(causal, seg, -jnp.inf)).astype(dtype)
    scores = jnp.einsum('bclgn,bcsgn->bgcls', Cc, Bc)
    y_diag = jnp.einsum('bgecls,bcsgep->bclgep', scores[:, :, None] * Lmat, Xc)
    decay_to_end = jnp.exp(a_cs[..., -1:] - a_cs).astype(dtype)
    chunk_states = jnp.einsum('bclgn,bgecl,bclgep->cbgepn', Bc, decay_to_end, Xc)
    chunk_decay = jnp.moveaxis(jnp.exp(a_cs[..., -1]).astype(dtype), -1, 0)

    def step(state, inp):
        dec, new = inp
        return state * dec[..., None, None] + new, state

    init = jnp.zeros(chunk_states.shape[1:], dtype)
    _, prev_states = lax.scan(step, init, (chunk_decay, chunk_states))
    decay_from_start = jnp.exp(a_cs).astype(dtype)
    y_off = jnp.einsum('bclgn,cbgepn,bgecl->bclgep', Cc, prev_states, decay_from_start)
    return (y_diag + y_off).reshape(b, l, h, p)


def _fwd_setup_inputs(seed: int = 0) -> dict:
    key = jax.random.key(seed)
    ks = jax.random.split(key, 32)
    L = DEPTH
    f32 = jnp.float32

    def nrm(k, shape, scale):
        return jax.random.normal(k, shape, f32) * scale

    def gain(k, shape):
        return 1.0 + 0.01 * jax.random.normal(k, shape, f32)

    dt0 = jnp.exp(jax.random.uniform(ks[6], (L, SSD_HEADS), f32,
                                     minval=math.log(1e-3), maxval=math.log(1e-1)))
    dt_bias = dt0 + jnp.log(-jnp.expm1(-dt0))
    return {
        "x": nrm(ks[0], (BATCH, SEQ, D_MODEL), 1.0),
        "p": nrm(ks[1], (DEPTH, BATCH, SEQ, PLE_DIM), 1.0),
        "mix_norm_g": gain(ks[2], (L, D_MODEL)),
        "w_in": nrm(ks[3], (L, D_MODEL, IN_WIDTH), D_MODEL ** -0.5),
        "ssd_conv_w": nrm(ks[4], (L, SSD_CONV, XBC_WIDTH), SSD_CONV ** -0.5),
        "ssd_conv_b": nrm(ks[5], (L, XBC_WIDTH), 0.01),
        "dt_bias": dt_bias,
        "A_log": jnp.log(jax.random.uniform(ks[7], (L, SSD_HEADS), f32, minval=1.0, maxval=16.0)),
        "D_skip": gain(ks[8], (L, SSD_HEADS)),
        "ssd_norm_g": gain(ks[9], (L, SSD_WIDTH)),
        "conf_dw_w": nrm(ks[10], (L, CONF_KERNEL, CONF_WIDTH), CONF_KERNEL ** -0.5),
        "conf_dw_b": nrm(ks[11], (L, CONF_WIDTH), 0.01),
        "conf_ln_g": gain(ks[12], (L, CONF_WIDTH)),
        "conf_ln_b": nrm(ks[13], (L, CONF_WIDTH), 0.01),
        "w_out": nrm(ks[14], (L, MIX_WIDTH, D_MODEL), MIX_WIDTH ** -0.5),
        "mlp_norm_g": gain(ks[15], (L, D_MODEL)),
        "w_up": nrm(ks[16], (L, D_MODEL, D_FF), D_MODEL ** -0.5),
        "w_down": nrm(ks[17], (L, D_FF, D_MODEL), D_FF ** -0.5),
        "ple_gate_norm_g": gain(ks[18], (L, D_MODEL)),
        "w_ple_gate": nrm(ks[19], (L, D_MODEL, D_MODEL), D_MODEL ** -0.5),
        "b_ple_gate": nrm(ks[20], (L, D_MODEL), 0.01),
        "w_ple": nrm(ks[21], (L, PLE_DIM, D_MODEL), PLE_DIM ** -0.5),
        "ple_norm_g": gain(ks[22], (L, D_MODEL)),
        "final_norm_g": gain(ks[23], (D_MODEL,)),
    }


def _fwd_reference(x, p, mix_norm_g, w_in, ssd_conv_w, ssd_conv_b, dt_bias, A_log, D_skip,
              ssd_norm_g, conf_dw_w, conf_dw_b, conf_ln_g, conf_ln_b, w_out, mlp_norm_g,
              w_up, w_down, ple_gate_norm_g, w_ple_gate, b_ple_gate, w_ple, ple_norm_g,
              final_norm_g):
    b, l, _ = x.shape
    split_at = np.cumsum([SSD_WIDTH, XBC_WIDTH, SSD_HEADS, CONF_WIDTH]).tolist()
    xbc_split = [SSD_WIDTH, SSD_WIDTH + SSD_GROUPS * SSD_STATE]
    h = x
    for i in range(DEPTH):
        u = rmsnorm(h, mix_norm_g[i])
        proj = u @ w_in[i].astype(u.dtype)
        z, xbc, dt_raw, conf_val, conf_gate = jnp.split(proj, split_at, axis=-1)

        xbc = jax.nn.silu(causal_depthwise_conv(xbc, ssd_conv_w[i], ssd_conv_b[i]))
        xs, Bm, Cm = jnp.split(xbc, xbc_split, axis=-1)
        dt = jax.nn.softplus(dt_raw.astype(jnp.float32) + dt_bias[i].astype(jnp.float32))
        A = -jnp.exp(A_log[i].astype(jnp.float32))
        xh = xs.reshape(b, l, SSD_HEADS, SSD_HEAD_DIM)
        y = ssd_chunked(xh, dt, A,
                        Bm.reshape(b, l, SSD_GROUPS, SSD_STATE),
                        Cm.reshape(b, l, SSD_GROUPS, SSD_STATE))
        y = (y + xh * D_skip[i].astype(xh.dtype)[:, None]).reshape(b, l, SSD_WIDTH)
        y_ssd = gated_group_rmsnorm(y, z, ssd_norm_g[i])

        v = conf_val * jax.nn.sigmoid(conf_gate)
        v = causal_depthwise_conv(v, conf_dw_w[i], conf_dw_b[i])
        y_conf = jax.nn.silu(layernorm(v, conf_ln_g[i], conf_ln_b[i]))

        mixed = jnp.concatenate([y_ssd, y_conf], axis=-1)
        h = h + mixed @ w_out[i].astype(mixed.dtype)

        u = rmsnorm(h, mlp_norm_g[i])
        hid = jax.nn.relu(u @ w_up[i].astype(u.dtype))
        h = h + (hid * hid) @ w_down[i].astype(hid.dtype)

        gate = jax.nn.sigmoid(rmsnorm(h, ple_gate_norm_g[i]) @ w_ple_gate[i].astype(h.dtype)
                              + b_ple_gate[i].astype(h.dtype))
        emb = rmsnorm(p[i].astype(h.dtype) @ w_ple[i].astype(h.dtype), ple_norm_g[i])
        h = h + gate * emb
    return rmsnorm(h, final_norm_g)


import jax as _jax
import jax.numpy as _jnp

TWIN_FORMAT = 'train_step'
FWD_PARAMS = ['x', 'p', 'mix_norm_g', 'w_in', 'ssd_conv_w', 'ssd_conv_b', 'dt_bias', 'A_log', 'D_skip', 'ssd_norm_g', 'conf_dw_w', 'conf_dw_b', 'conf_ln_g', 'conf_ln_b', 'w_out', 'mlp_norm_g', 'w_up', 'w_down', 'ple_gate_norm_g', 'w_ple_gate', 'b_ple_gate', 'w_ple', 'ple_norm_g', 'final_norm_g']
TWIN_WEIGHTS = ['mix_norm_g', 'w_in', 'ssd_conv_w', 'ssd_conv_b', 'dt_bias', 'A_log', 'D_skip', 'ssd_norm_g', 'conf_dw_w', 'conf_dw_b', 'conf_ln_g', 'conf_ln_b', 'w_out', 'mlp_norm_g', 'w_up', 'w_down', 'ple_gate_norm_g', 'w_ple_gate', 'b_ple_gate', 'w_ple', 'ple_norm_g', 'final_norm_g']
TWIN_DIFF_INPUT = 'x'
TWIN_INPUTS = ['x', 'p', 'mix_norm_g', 'w_in', 'ssd_conv_w', 'ssd_conv_b', 'dt_bias', 'A_log', 'D_skip', 'ssd_norm_g', 'conf_dw_w', 'conf_dw_b', 'conf_ln_g', 'conf_ln_b', 'w_out', 'mlp_norm_g', 'w_up', 'w_down', 'ple_gate_norm_g', 'w_ple_gate', 'b_ple_gate', 'w_ple', 'ple_norm_g', 'final_norm_g', 'loss_target', 'm_mix_norm_g', 'm_w_in', 'm_ssd_conv_w', 'm_ssd_conv_b', 'm_dt_bias', 'm_A_log', 'm_D_skip', 'm_ssd_norm_g', 'm_conf_dw_w', 'm_conf_dw_b', 'm_conf_ln_g', 'm_conf_ln_b', 'm_w_out', 'm_mlp_norm_g', 'm_w_up', 'm_w_down', 'm_ple_gate_norm_g', 'm_w_ple_gate', 'm_b_ple_gate', 'm_w_ple', 'm_ple_norm_g', 'm_final_norm_g', 'v_mix_norm_g', 'v_w_in', 'v_ssd_conv_w', 'v_ssd_conv_b', 'v_dt_bias', 'v_A_log', 'v_D_skip', 'v_ssd_norm_g', 'v_conf_dw_w', 'v_conf_dw_b', 'v_conf_ln_g', 'v_conf_ln_b', 'v_w_out', 'v_mlp_norm_g', 'v_w_up', 'v_w_down', 'v_ple_gate_norm_g', 'v_w_ple_gate', 'v_b_ple_gate', 'v_w_ple', 'v_ple_norm_g', 'v_final_norm_g']
TWIN_OUTPUTS = ['loss', 'grad_x', 'grad_mix_norm_g', 'grad_w_in', 'grad_ssd_conv_w', 'grad_ssd_conv_b', 'grad_dt_bias', 'grad_A_log', 'grad_D_skip', 'grad_ssd_norm_g', 'grad_conf_dw_w', 'grad_conf_dw_b', 'grad_conf_ln_g', 'grad_conf_ln_b', 'grad_w_out', 'grad_mlp_norm_g', 'grad_w_up', 'grad_w_down', 'grad_ple_gate_norm_g', 'grad_w_ple_gate', 'grad_b_ple_gate', 'grad_w_ple', 'grad_ple_norm_g', 'grad_final_norm_g', 'delta_mix_norm_g', 'delta_w_in', 'delta_ssd_conv_w', 'delta_ssd_conv_b', 'delta_dt_bias', 'delta_A_log', 'delta_D_skip', 'delta_ssd_norm_g', 'delta_conf_dw_w', 'delta_conf_dw_b', 'delta_conf_ln_g', 'delta_conf_ln_b', 'delta_w_out', 'delta_mlp_norm_g', 'delta_w_up', 'delta_w_down', 'delta_ple_gate_norm_g', 'delta_w_ple_gate', 'delta_b_ple_gate', 'delta_w_ple', 'delta_ple_norm_g', 'delta_final_norm_g', 'new_m_mix_norm_g', 'new_m_w_in', 'new_m_ssd_conv_w', 'new_m_ssd_conv_b', 'new_m_dt_bias', 'new_m_A_log', 'new_m_D_skip', 'new_m_ssd_norm_g', 'new_m_conf_dw_w', 'new_m_conf_dw_b', 'new_m_conf_ln_g', 'new_m_conf_ln_b', 'new_m_w_out', 'new_m_mlp_norm_g', 'new_m_w_up', 'new_m_w_down', 'new_m_ple_gate_norm_g', 'new_m_w_ple_gate', 'new_m_b_ple_gate', 'new_m_w_ple', 'new_m_ple_norm_g', 'new_m_final_norm_g', 'new_v_mix_norm_g', 'new_v_w_in', 'new_v_ssd_conv_w', 'new_v_ssd_conv_b', 'new_v_dt_bias', 'new_v_A_log', 'new_v_D_skip', 'new_v_ssd_norm_g', 'new_v_conf_dw_w', 'new_v_conf_dw_b', 'new_v_conf_ln_g', 'new_v_conf_ln_b', 'new_v_w_out', 'new_v_mlp_norm_g', 'new_v_w_up', 'new_v_w_down', 'new_v_ple_gate_norm_g', 'new_v_w_ple_gate', 'new_v_b_ple_gate', 'new_v_w_ple', 'new_v_ple_norm_g', 'new_v_final_norm_g']
TWIN_LEAF_KINDS = {'loss': 'loss', 'grad_x': 'grad_x', 'grad_mix_norm_g': 'grad_w', 'grad_w_in': 'grad_w', 'grad_ssd_conv_w': 'grad_w', 'grad_ssd_conv_b': 'grad_w', 'grad_dt_bias': 'grad_w', 'grad_A_log': 'grad_w', 'grad_D_skip': 'grad_w', 'grad_ssd_norm_g': 'grad_w', 'grad_conf_dw_w': 'grad_w', 'grad_conf_dw_b': 'grad_w', 'grad_conf_ln_g': 'grad_w', 'grad_conf_ln_b': 'grad_w', 'grad_w_out': 'grad_w', 'grad_mlp_norm_g': 'grad_w', 'grad_w_up': 'grad_w', 'grad_w_down': 'grad_w', 'grad_ple_gate_norm_g': 'grad_w', 'grad_w_ple_gate': 'grad_w', 'grad_b_ple_gate': 'grad_w', 'grad_w_ple': 'grad_w', 'grad_ple_norm_g': 'grad_w', 'grad_final_norm_g': 'grad_w', 'delta_mix_norm_g': 'delta_w', 'delta_w_in': 'delta_w', 'delta_ssd_conv_w': 'delta_w', 'delta_ssd_conv_b': 'delta_w', 'delta_dt_bias': 'delta_w', 'delta_A_log': 'delta_w', 'delta_D_skip': 'delta_w', 'delta_ssd_norm_g': 'delta_w', 'delta_conf_dw_w': 'delta_w', 'delta_conf_dw_b': 'delta_w', 'delta_conf_ln_g': 'delta_w', 'delta_conf_ln_b': 'delta_w', 'delta_w_out': 'delta_w', 'delta_mlp_norm_g': 'delta_w', 'delta_w_up': 'delta_w', 'delta_w_down': 'delta_w', 'delta_ple_gate_norm_g': 'delta_w', 'delta_w_ple_gate': 'delta_w', 'delta_b_ple_gate': 'delta_w', 'delta_w_ple': 'delta_w', 'delta_ple_norm_g': 'delta_w', 'delta_final_norm_g': 'delta_w', 'new_m_mix_norm_g': 'new_m', 'new_m_w_in': 'new_m', 'new_m_ssd_conv_w': 'new_m', 'new_m_ssd_conv_b': 'new_m', 'new_m_dt_bias': 'new_m', 'new_m_A_log': 'new_m', 'new_m_D_skip': 'new_m', 'new_m_ssd_norm_g': 'new_m', 'new_m_conf_dw_w': 'new_m', 'new_m_conf_dw_b': 'new_m', 'new_m_conf_ln_g': 'new_m', 'new_m_conf_ln_b': 'new_m', 'new_m_w_out': 'new_m', 'new_m_mlp_norm_g': 'new_m', 'new_m_w_up': 'new_m', 'new_m_w_down': 'new_m', 'new_m_ple_gate_norm_g': 'new_m', 'new_m_w_ple_gate': 'new_m', 'new_m_b_ple_gate': 'new_m', 'new_m_w_ple': 'new_m', 'new_m_ple_norm_g': 'new_m', 'new_m_final_norm_g': 'new_m', 'new_v_mix_norm_g': 'new_v', 'new_v_w_in': 'new_v', 'new_v_ssd_conv_w': 'new_v', 'new_v_ssd_conv_b': 'new_v', 'new_v_dt_bias': 'new_v', 'new_v_A_log': 'new_v', 'new_v_D_skip': 'new_v', 'new_v_ssd_norm_g': 'new_v', 'new_v_conf_dw_w': 'new_v', 'new_v_conf_dw_b': 'new_v', 'new_v_conf_ln_g': 'new_v', 'new_v_conf_ln_b': 'new_v', 'new_v_w_out': 'new_v', 'new_v_mlp_norm_g': 'new_v', 'new_v_w_up': 'new_v', 'new_v_w_down': 'new_v', 'new_v_ple_gate_norm_g': 'new_v', 'new_v_w_ple_gate': 'new_v', 'new_v_b_ple_gate': 'new_v', 'new_v_w_ple': 'new_v', 'new_v_ple_norm_g': 'new_v', 'new_v_final_norm_g': 'new_v'}


def _forward(args):
    return _fwd_reference(*[args[k] for k in FWD_PARAMS])


def _output_shape():
    out = _jax.eval_shape(lambda: _forward(_fwd_setup_inputs(0)))
    return out.shape, out.dtype

N_MICROBATCH = 1
ADAM_LR = 0.001
ADAM_B1 = 0.9
ADAM_B2 = 0.999
ADAM_EPS = 1e-08
ADAM_WD = 0.01
ADAM_STEP = 10
PER_EXAMPLE_BATCH_AXIS = {'x': 0, 'p': 1, 'loss_target': 0}
SHARED_INPUTS = []
_WEIGHT_DTYPES = {'mix_norm_g': _jnp.float32, 'w_in': _jnp.float32, 'ssd_conv_w': _jnp.float32, 'ssd_conv_b': _jnp.float32, 'dt_bias': _jnp.float32, 'A_log': _jnp.float32, 'D_skip': _jnp.float32, 'ssd_norm_g': _jnp.float32, 'conf_dw_w': _jnp.float32, 'conf_dw_b': _jnp.float32, 'conf_ln_g': _jnp.float32, 'conf_ln_b': _jnp.float32, 'w_out': _jnp.float32, 'mlp_norm_g': _jnp.float32, 'w_up': _jnp.float32, 'w_down': _jnp.float32, 'ple_gate_norm_g': _jnp.float32, 'w_ple_gate': _jnp.float32, 'b_ple_gate': _jnp.float32, 'w_ple': _jnp.float32, 'ple_norm_g': _jnp.float32, 'final_norm_g': _jnp.float32}
MOMENT_SCALE = {'mix_norm_g': 1.717430e-01, 'w_in': 7.889346e-02, 'ssd_conv_w': 8.916496e-02, 'ssd_conv_b': 1.201040e-01, 'dt_bias': 3.470893e-01, 'A_log': 3.482812e-01, 'D_skip': 6.439458e-01, 'ssd_norm_g': 1.019275e-01, 'conf_dw_w': 6.197526e-02, 'conf_dw_b': 1.299054e-01, 'conf_ln_g': 7.342091e-02, 'conf_ln_b': 7.451497e-02, 'w_out': 1.170216e-01, 'mlp_norm_g': 1.295265e-01, 'w_up': 6.769941e-02, 'w_down': 1.187138e-01, 'ple_gate_norm_g': 1.985967e-02, 'w_ple_gate': 2.026585e-02, 'b_ple_gate': 2.219288e-02, 'w_ple': 5.177734e-02, 'ple_norm_g': 5.752518e-02, 'final_norm_g': 3.219725e+01}


def _to_microbatches(a, axis):
    t = _jnp.moveaxis(a, axis, 0)
    t = t.reshape((N_MICROBATCH, t.shape[0] // N_MICROBATCH) + t.shape[1:])
    return _jnp.moveaxis(t, 1, axis + 1)


def setup_inputs(seed: int = 0) -> dict:
    inp = _fwd_setup_inputs(seed)
    key = _jax.random.fold_in(_jax.random.key(seed), 7919)
    shape, _ = _output_shape()
    out = dict(inp)
    out["loss_target"] = _jax.random.normal(_jax.random.fold_in(key, 0), shape, _jnp.float32)
    for i, name in enumerate(TWIN_WEIGHTS):
        w = inp[name].astype(_jnp.float32)
        if MOMENT_SCALE is None:
            s = _jnp.sqrt(_jnp.mean(_jnp.square(w)) + 1e-30)
        else:
            s = MOMENT_SCALE[name]
        km, kv = _jax.random.split(_jax.random.fold_in(key, i + 1))
        out[name] = w
        out["m_" + name] = s * _jax.random.normal(km, w.shape, _jnp.float32)
        out["v_" + name] = (s * s) * _jax.random.uniform(kv, w.shape, _jnp.float32, 0.5, 1.5)
    if N_MICROBATCH > 1:
        for name, axis in PER_EXAMPLE_BATCH_AXIS.items():
            out[name] = _to_microbatches(out[name], axis)
    return {'x': out['x'], 'p': out['p'], 'mix_norm_g': out['mix_norm_g'], 'w_in': out['w_in'], 'ssd_conv_w': out['ssd_conv_w'], 'ssd_conv_b': out['ssd_conv_b'], 'dt_bias': out['dt_bias'], 'A_log': out['A_log'], 'D_skip': out['D_skip'], 'ssd_norm_g': out['ssd_norm_g'], 'conf_dw_w': out['conf_dw_w'], 'conf_dw_b': out['conf_dw_b'], 'conf_ln_g': out['conf_ln_g'], 'conf_ln_b': out['conf_ln_b'], 'w_out': out['w_out'], 'mlp_norm_g': out['mlp_norm_g'], 'w_up': out['w_up'], 'w_down': out['w_down'], 'ple_gate_norm_g': out['ple_gate_norm_g'], 'w_ple_gate': out['w_ple_gate'], 'b_ple_gate': out['b_ple_gate'], 'w_ple': out['w_ple'], 'ple_norm_g': out['ple_norm_g'], 'final_norm_g': out['final_norm_g'], 'loss_target': out['loss_target'], 'm_mix_norm_g': out['m_mix_norm_g'], 'm_w_in': out['m_w_in'], 'm_ssd_conv_w': out['m_ssd_conv_w'], 'm_ssd_conv_b': out['m_ssd_conv_b'], 'm_dt_bias': out['m_dt_bias'], 'm_A_log': out['m_A_log'], 'm_D_skip': out['m_D_skip'], 'm_ssd_norm_g': out['m_ssd_norm_g'], 'm_conf_dw_w': out['m_conf_dw_w'], 'm_conf_dw_b': out['m_conf_dw_b'], 'm_conf_ln_g': out['m_conf_ln_g'], 'm_conf_ln_b': out['m_conf_ln_b'], 'm_w_out': out['m_w_out'], 'm_mlp_norm_g': out['m_mlp_norm_g'], 'm_w_up': out['m_w_up'], 'm_w_down': out['m_w_down'], 'm_ple_gate_norm_g': out['m_ple_gate_norm_g'], 'm_w_ple_gate': out['m_w_ple_gate'], 'm_b_ple_gate': out['m_b_ple_gate'], 'm_w_ple': out['m_w_ple'], 'm_ple_norm_g': out['m_ple_norm_g'], 'm_final_norm_g': out['m_final_norm_g'], 'v_mix_norm_g': out['v_mix_norm_g'], 'v_w_in': out['v_w_in'], 'v_ssd_conv_w': out['v_ssd_conv_w'], 'v_ssd_conv_b': out['v_ssd_conv_b'], 'v_dt_bias': out['v_dt_bias'], 'v_A_log': out['v_A_log'], 'v_D_skip': out['v_D_skip'], 'v_ssd_norm_g': out['v_ssd_norm_g'], 'v_conf_dw_w': out['v_conf_dw_w'], 'v_conf_dw_b': out['v_conf_dw_b'], 'v_conf_ln_g': out['v_conf_ln_g'], 'v_conf_ln_b': out['v_conf_ln_b'], 'v_w_out': out['v_w_out'], 'v_mlp_norm_g': out['v_mlp_norm_g'], 'v_w_up': out['v_w_up'], 'v_w_down': out['v_w_down'], 'v_ple_gate_norm_g': out['v_ple_gate_norm_g'], 'v_w_ple_gate': out['v_w_ple_gate'], 'v_b_ple_gate': out['v_b_ple_gate'], 'v_w_ple': out['v_w_ple'], 'v_ple_norm_g': out['v_ple_norm_g'], 'v_final_norm_g': out['v_final_norm_g']}


def _loss(weights, diff, rest, loss_target):
    with _jax.named_scope("forward"):
        args = {**rest, TWIN_DIFF_INPUT: diff, **{k: w.astype(_WEIGHT_DTYPES[k]) for k, w in weights.items()}}
        y = _forward(args)
    with _jax.named_scope("loss_head"):
        err = _jnp.square(y.astype(_jnp.float32) - loss_target)
        return 0.5 * _jnp.sum(_jnp.mean(err, axis=-1)) if err.ndim else 0.5 * err


def _adamw(w, g, m, v):
    m = ADAM_B1 * m + (1.0 - ADAM_B1) * g
    v = ADAM_B2 * v + (1.0 - ADAM_B2) * _jnp.square(g)
    m_hat = m / (1.0 - ADAM_B1 ** ADAM_STEP)
    v_hat = v / (1.0 - ADAM_B2 ** ADAM_STEP)
    delta = -ADAM_LR * (m_hat / (_jnp.sqrt(v_hat) + ADAM_EPS) + ADAM_WD * w)
    return delta, m, v


def reference(x, p, mix_norm_g, w_in, ssd_conv_w, ssd_conv_b, dt_bias, A_log, D_skip, ssd_norm_g, conf_dw_w, conf_dw_b, conf_ln_g, conf_ln_b, w_out, mlp_norm_g, w_up, w_down, ple_gate_norm_g, w_ple_gate, b_ple_gate, w_ple, ple_norm_g, final_norm_g, loss_target, m_mix_norm_g, m_w_in, m_ssd_conv_w, m_ssd_conv_b, m_dt_bias, m_A_log, m_D_skip, m_ssd_norm_g, m_conf_dw_w, m_conf_dw_b, m_conf_ln_g, m_conf_ln_b, m_w_out, m_mlp_norm_g, m_w_up, m_w_down, m_ple_gate_norm_g, m_w_ple_gate, m_b_ple_gate, m_w_ple, m_ple_norm_g, m_final_norm_g, v_mix_norm_g, v_w_in, v_ssd_conv_w, v_ssd_conv_b, v_dt_bias, v_A_log, v_D_skip, v_ssd_norm_g, v_conf_dw_w, v_conf_dw_b, v_conf_ln_g, v_conf_ln_b, v_w_out, v_mlp_norm_g, v_w_up, v_w_down, v_ple_gate_norm_g, v_w_ple_gate, v_b_ple_gate, v_w_ple, v_ple_norm_g, v_final_norm_g):
    given = dict(x=x, p=p, mix_norm_g=mix_norm_g, w_in=w_in, ssd_conv_w=ssd_conv_w, ssd_conv_b=ssd_conv_b, dt_bias=dt_bias, A_log=A_log, D_skip=D_skip, ssd_norm_g=ssd_norm_g, conf_dw_w=conf_dw_w, conf_dw_b=conf_dw_b, conf_ln_g=conf_ln_g, conf_ln_b=conf_ln_b, w_out=w_out, mlp_norm_g=mlp_norm_g, w_up=w_up, w_down=w_down, ple_gate_norm_g=ple_gate_norm_g, w_ple_gate=w_ple_gate, b_ple_gate=b_ple_gate, w_ple=w_ple, ple_norm_g=ple_norm_g, final_norm_g=final_norm_g, loss_target=loss_target, m_mix_norm_g=m_mix_norm_g, m_w_in=m_w_in, m_ssd_conv_w=m_ssd_conv_w, m_ssd_conv_b=m_ssd_conv_b, m_dt_bias=m_dt_bias, m_A_log=m_A_log, m_D_skip=m_D_skip, m_ssd_norm_g=m_ssd_norm_g, m_conf_dw_w=m_conf_dw_w, m_conf_dw_b=m_conf_dw_b, m_conf_ln_g=m_conf_ln_g, m_conf_ln_b=m_conf_ln_b, m_w_out=m_w_out, m_mlp_norm_g=m_mlp_norm_g, m_w_up=m_w_up, m_w_down=m_w_down, m_ple_gate_norm_g=m_ple_gate_norm_g, m_w_ple_gate=m_w_ple_gate, m_b_ple_gate=m_b_ple_gate, m_w_ple=m_w_ple, m_ple_norm_g=m_ple_norm_g, m_final_norm_g=m_final_norm_g, v_mix_norm_g=v_mix_norm_g, v_w_in=v_w_in, v_ssd_conv_w=v_ssd_conv_w, v_ssd_conv_b=v_ssd_conv_b, v_dt_bias=v_dt_bias, v_A_log=v_A_log, v_D_skip=v_D_skip, v_ssd_norm_g=v_ssd_norm_g, v_conf_dw_w=v_conf_dw_w, v_conf_dw_b=v_conf_dw_b, v_conf_ln_g=v_conf_ln_g, v_conf_ln_b=v_conf_ln_b, v_w_out=v_w_out, v_mlp_norm_g=v_mlp_norm_g, v_w_up=v_w_up, v_w_down=v_w_down, v_ple_gate_norm_g=v_ple_gate_norm_g, v_w_ple_gate=v_w_ple_gate, v_b_ple_gate=v_b_ple_gate, v_w_ple=v_w_ple, v_ple_norm_g=v_ple_norm_g, v_final_norm_g=v_final_norm_g)
    weights = {n: given[n] for n in TWIN_WEIGHTS}
    shared = {n: given[n] for n in SHARED_INPUTS}
    per_example = {n: given[n] for n in ['x', 'p']}
    grad_fn = _jax.value_and_grad(_loss, argnums=(0, 1))

    def one_microbatch(ex, loss_target):
        ex = dict(ex)
        diff = ex.pop(TWIN_DIFF_INPUT)
        return grad_fn(weights, diff, {**shared, **ex}, loss_target)

    if N_MICROBATCH == 1:
        loss, (grad_w, grad_x) = one_microbatch(per_example, given["loss_target"])
    else:
        def body(carry, xs):
            loss_sum, grad_sum = carry
            l_k, (gw_k, gx_k) = one_microbatch(xs[0], xs[1])
            with _jax.named_scope("update"):
                return (loss_sum + l_k, _jax.tree.map(_jnp.add, grad_sum, gw_k)), gx_k

        init = (_jnp.zeros((), _jnp.float32), _jax.tree.map(_jnp.zeros_like, weights))
        (loss, grad_w), grad_x = _jax.lax.scan(body, init, (per_example, given["loss_target"]))
    with _jax.named_scope("update"):
        delta_w, new_m, new_v = {}, {}, {}
        for n in TWIN_WEIGHTS:
            delta_w[n], new_m[n], new_v[n] = _adamw(weights[n], grad_w[n], given["m_" + n], given["v_" + n])
    return (loss, grad_x, *[grad_w[n] for n in TWIN_WEIGHTS], *[delta_w[n] for n in TWIN_WEIGHTS],
            *[new_m[n] for n in TWIN_WEIGHTS], *[new_v[n] for n in TWIN_WEIGHTS])
```

```python
import jax
import jax.numpy as jnp
from jax import lax
from jax.experimental import pallas as pl
from jax.experimental.pallas import tpu as pltpu

f32 = jnp.float32
bf16 = jnp.bfloat16

D_MODEL = 1024
SSD_WIDTH = 1024
SSD_HEADS = 16
HEAD_DIM = 64
SSD_STATE = 128
XBC_WIDTH = 1536
SSD_CONV = 4
CHUNK = 128
CONF_WIDTH = 1024
CONF_KERNEL = 31
D_FF = 4096
PLE_DIM = 256
IN_WIDTH = 4624
EPS = 1e-6
N_CHIPS = 4
N_DEV = 8

ADAM_LR = 0.001
ADAM_B1 = 0.9
ADAM_B2 = 0.999
ADAM_EPS = 1e-08
ADAM_WD = 0.01
ADAM_STEP = 10

LANES = 128
VMEM_BIG = 56 * 1024 * 1024
VMEM_MID = 40 * 1024 * 1024

SLAB_ROWS = (("w_in", 1156), ("w_out", 512), ("w_up", 1024), ("w_down", 1024), ("w_ple_gate", 256), ("w_ple", 64))
SLAB_USED = sum(r for _, r in SLAB_ROWS)
SLAB_R = 4096
SLAB_H = SLAB_R // 2
CONVW_ROWS = 16
SMALL_ROWS = 56

MESH = pl.DeviceIdType.MESH
ANY = pl.BlockSpec(memory_space=pl.ANY)


def _cparams(sem=None, vmem=None):
    return pltpu.CompilerParams(dimension_semantics=sem, vmem_limit_bytes=vmem)


def _full(shape):
    n = len(shape)
    return pl.BlockSpec(shape, lambda *_: (0,) * n)


def _dot(a, b):
    return jnp.dot(a, b, preferred_element_type=f32)


def _dot_nt(a, b):
    return lax.dot_general(a, b, (((1,), (1,)), ((), ())), preferred_element_type=f32)


def _dot_tn(a, b):
    return lax.dot_general(a, b, (((0,), (0,)), ((), ())), preferred_element_type=f32)


def _sigmoid(x):
    return jax.nn.sigmoid(x)


def _rms(x, g):
    r = lax.rsqrt(jnp.mean(x * x, axis=-1, keepdims=True) + EPS)
    return x * r * g


def _rms_bwd(dy, x, g):
    r = lax.rsqrt(jnp.mean(x * x, axis=-1, keepdims=True) + EPS)
    xh = x * r
    dg = jnp.sum(dy * xh, axis=0, keepdims=True)
    dxh = dy * g
    dx = r * (dxh - xh * jnp.mean(dxh * xh, axis=-1, keepdims=True))
    return dx, dg


def _dsilu(x):
    s = _sigmoid(x)
    return s * (1.0 + x * (1.0 - s))


def _split3(x):
    hi = x.astype(bf16)
    r1 = x - hi.astype(f32)
    mid = r1.astype(bf16)
    lo = (r1 - mid.astype(f32)).astype(bf16)
    return hi, mid, lo


def _head_matrix():
    row = lax.broadcasted_iota(jnp.int32, (LANES, SSD_WIDTH), 0)
    col = lax.broadcasted_iota(jnp.int32, (LANES, SSD_WIDTH), 1)
    lo = row * HEAD_DIM
    return ((col >= lo) & (col < lo + HEAD_DIM)).astype(bf16)


def _expand(x, e):
    hi, mid, lo = _split3(x)
    return _dot(hi, e) + _dot(mid, e) + _dot(lo, e)


def _contract(x, e):
    hi, mid, lo = _split3(x)
    return _dot_nt(hi, e) + _dot_nt(mid, e) + _dot_nt(lo, e)


def _in_proj_fwd(x, g, wz, wxbc, wcv, wcg, wdt):
    T = x.shape[0]
    tm = min(256, T)

    def body(x_ref, g_ref, wz_ref, wx_ref, wcv_ref, wcg_ref, wdt_ref,
             u_ref, z_ref, xbc_ref, cv_ref, cg_ref, dt_ref, v_ref):
        ub = _rms(x_ref[...], g_ref[...]).astype(bf16)
        u_ref[...] = ub
        z_ref[...] = _dot(ub, wz_ref[...])
        xbc_ref[...] = _dot(ub, wx_ref[...])
        cv = _dot(ub, wcv_ref[...])
        cg = _dot(ub, wcg_ref[...])
        cv_ref[...] = cv
        cg_ref[...] = cg
        v_ref[...] = cv * _sigmoid(cg)
        dt_ref[...] = _dot(ub, wdt_ref[...])

    row = lambda n: pl.BlockSpec((tm, n), lambda i: (i, 0))
    return pl.pallas_call(
        body, name="in_proj_fwd", grid=(T // tm,),
        in_specs=[row(D_MODEL), _full((1, D_MODEL)), _full(wz.shape), _full(wxbc.shape), _full(wcv.shape),
                  _full(wcg.shape), _full(wdt.shape)],
        out_specs=[row(D_MODEL), row(SSD_WIDTH), row(XBC_WIDTH), row(CONF_WIDTH), row(CONF_WIDTH), row(LANES),
                   row(CONF_WIDTH)],
        out_shape=[jax.ShapeDtypeStruct((T, D_MODEL), bf16), jax.ShapeDtypeStruct((T, SSD_WIDTH), f32),
                   jax.ShapeDtypeStruct((T, XBC_WIDTH), f32), jax.ShapeDtypeStruct((T, CONF_WIDTH), f32),
                   jax.ShapeDtypeStruct((T, CONF_WIDTH), f32), jax.ShapeDtypeStruct((T, LANES), f32),
                   jax.ShapeDtypeStruct((T, CONF_WIDTH), f32)],
        compiler_params=_cparams(("parallel",), VMEM_BIG),
    )(x, g, wz, wxbc, wcv, wcg, wdt)


def _conv_taps(cur, halo, w_ref, buf_ref, K, hb, tm):
    buf_ref[0:hb, :] = halo
    buf_ref[hb:hb + tm, :] = cur
    acc = None
    for k in range(K):
        term = w_ref[k:k + 1, :] * buf_ref[pl.ds(hb - (K - 1) + k, tm), :]
        acc = term if acc is None else acc + term
    return acc


def _prev_halo_spec(hb, tm, C):
    return pl.BlockSpec((hb, C), lambda i: (jnp.maximum(i * (tm // hb) - 1, 0), 0))


def _ssd_conv_fwd(xbc, w, b):
    T, C = xbc.shape
    K, hb = SSD_CONV, 8
    tm = min(256, T)

    def body(cur_ref, halo_ref, w_ref, b_ref, pre_ref, buf_ref):
        keep = jnp.where(pl.program_id(0) > 0, 1.0, 0.0)
        acc = _conv_taps(cur_ref[...], halo_ref[...] * keep, w_ref, buf_ref, K, hb, tm)
        pre_ref[...] = acc + b_ref[...]

    return pl.pallas_call(
        body, name="ssd_conv_fwd", grid=(T // tm,),
        in_specs=[pl.BlockSpec((tm, C), lambda i: (i, 0)), _prev_halo_spec(hb, tm, C), _full(w.shape), _full((1, C))],
        out_specs=pl.BlockSpec((tm, C), lambda i: (i, 0)),
        out_shape=jax.ShapeDtypeStruct((T, C), f32),
        scratch_shapes=[pltpu.VMEM((hb + tm, C), f32)],
        compiler_params=_cparams(("parallel",), VMEM_MID),
    )(xbc, xbc, w, b)


def _conf_fwd(v, w, b, ln_g, ln_b):
    T, C = v.shape
    K, hb = CONF_KERNEL, 32
    tm = min(256, T)

    def body(cur_ref, halo_ref, w_ref, b_ref, g_ref, bb_ref, co_ref, y_ref, buf_ref):
        keep = jnp.where(pl.program_id(0) > 0, 1.0, 0.0)
        co = _conv_taps(cur_ref[...], halo_ref[...] * keep, w_ref, buf_ref, K, hb, tm) + b_ref[...]
        co_ref[...] = co
        mu = jnp.mean(co, axis=-1, keepdims=True)
        xc = co - mu
        yn = xc * lax.rsqrt(jnp.mean(xc * xc, axis=-1, keepdims=True) + EPS) * g_ref[...] + bb_ref[...]
        y_ref[...] = (yn * _sigmoid(yn)).astype(bf16)

    return pl.pallas_call(
        body, name="conf_fwd", grid=(T // tm,),
        in_specs=[pl.BlockSpec((tm, C), lambda i: (i, 0)), _prev_halo_spec(hb, tm, C), _full(w.shape), _full((1, C)),
                  _full((1, C)), _full((1, C))],
        out_specs=[pl.BlockSpec((tm, C), lambda i: (i, 0)), pl.BlockSpec((tm, C), lambda i: (i, 0))],
        out_shape=[jax.ShapeDtypeStruct((T, C), f32), jax.ShapeDtypeStruct((T, C), bf16)],
        scratch_shapes=[pltpu.VMEM((hb + tm, C), f32)],
        compiler_params=_cparams(("parallel",), VMEM_MID),
    )(v, v, w, b, ln_g, ln_b)


def _ssd_chunk_common(pre, dtr, dtb, alog, e):
    act = pre * _sigmoid(pre)
    xs = act[:, :SSD_WIDTH]
    bm = act[:, SSD_WIDTH:SSD_WIDTH + 2 * SSD_STATE]
    cm = act[:, SSD_WIDTH + 2 * SSD_STATE:]
    row = lax.broadcasted_iota(jnp.int32, (CHUNK, CHUNK), 0)
    col = lax.broadcasted_iota(jnp.int32, (CHUNK, CHUNK), 1)
    tri = row >= col
    dt = jax.nn.softplus(dtr + dtb)
    a_neg = -jnp.exp(alog)
    a = dt * a_neg
    cs = jnp.dot(tri.astype(f32), a, precision=lax.Precision.HIGHEST, preferred_element_type=f32)
    cs_e = _expand(cs, e)
    dt_e = _expand(dt, e)
    csl_e = cs_e[CHUNK - 1:CHUNK, :]
    ecs_e = jnp.exp(cs_e)
    dte_e = jnp.exp(csl_e - cs_e)
    cd_e = jnp.exp(csl_e)
    xc = xs * dt_e
    xd = xc * dte_e
    return dict(xs=xs, bm=bm, cm=cm, tri=tri, dt=dt, a_neg=a_neg, cs=cs, ecs_e=ecs_e, dte_e=dte_e, cd_e=cd_e,
                dt_e=dt_e, xc=xc, xd=xd)


def _group(v, g, width):
    return v[:, g * width:(g + 1) * width]


def _ssd_fwd(pre, dtr, z, dtb, alog, dskip_e, gn):
    T = pre.shape[0]
    nc = T // CHUNK
    GW = SSD_WIDTH // 2

    def body(pre_ref, dtr_ref, z_ref, dtb_ref, alog_ref, de_ref, gn_ref, y_ref, ys_ref, sp_ref, st_ref):
        @pl.when(pl.program_id(0) == 0)
        def _():
            st_ref[...] = jnp.zeros_like(st_ref)

        e = _head_matrix()
        q = _ssd_chunk_common(pre_ref[...], dtr_ref[...], dtb_ref[...], alog_ref[...], e)
        cs, tri, xc, xd = q["cs"], q["tri"], q["xc"], q["xd"]
        cs_t = cs.T
        st = st_ref[...]
        sp_ref[0] = st
        lane = lax.broadcasted_iota(jnp.int32, (1, LANES), 1)
        halves = (lane < HEAD_DIM, lane >= HEAD_DIM)

        g_mat, y_off, s_new = [], [], []
        for g in range(2):
            bg = _group(q["bm"], g, SSD_STATE)
            cg = _group(q["cm"], g, SSD_STATE)
            bgb, cgb = bg.astype(bf16), cg.astype(bf16)
            g_mat.append(_dot_nt(cgb, bgb))
            y_off.append(_dot(cgb, _group(st, g, GW).astype(bf16)))
            s_new.append(_dot(bg.T.astype(bf16), _group(xd, g, GW).astype(bf16)))
        y_off = jnp.concatenate(y_off, axis=1) * q["ecs_e"]
        st_ref[...] = st * q["cd_e"] + jnp.concatenate(s_new, axis=1)

        pairs = []
        for j in range(SSD_HEADS // 2):
            xp = xc[:, j * LANES:(j + 1) * LANES]
            acc = jnp.zeros((CHUNK, LANES), f32)
            for hh in range(2):
                h = 2 * j + hh
                seg = cs[:, h:h + 1] - cs_t[h:h + 1, :]
                lm = jnp.exp(jnp.where(tri, seg, -1e30))
                m = (g_mat[h // 8] * lm).astype(bf16)
                acc = acc + _dot(m, jnp.where(halves[hh], xp, 0.0).astype(bf16))
            pairs.append(acc)
        y = jnp.concatenate(pairs, axis=1) + y_off + q["xs"] * de_ref[...]
        y_ref[...] = y

        zz = z_ref[...]
        v = y * (zz * _sigmoid(zz))
        outs = []
        for g in range(2):
            vg = _group(v, g, GW)
            outs.append(vg * lax.rsqrt(jnp.mean(vg * vg, axis=-1, keepdims=True) + EPS))
        ys_ref[...] = (jnp.concatenate(outs, axis=1) * gn_ref[...]).astype(bf16)

    ch = lambda n: pl.BlockSpec((CHUNK, n), lambda c: (c, 0))
    return pl.pallas_call(
        body, name="ssd_fwd", grid=(nc,),
        in_specs=[ch(XBC_WIDTH), ch(LANES), ch(SSD_WIDTH), _full((1, LANES)), _full((1, LANES)), _full((1, SSD_WIDTH)),
                  _full((1, SSD_WIDTH))],
        out_specs=[ch(SSD_WIDTH), ch(SSD_WIDTH), pl.BlockSpec((1, SSD_STATE, SSD_WIDTH), lambda c: (c, 0, 0))],
        out_shape=[jax.ShapeDtypeStruct((T, SSD_WIDTH), f32), jax.ShapeDtypeStruct((T, SSD_WIDTH), bf16),
                   jax.ShapeDtypeStruct((nc, SSD_STATE, SSD_WIDTH), f32)],
        scratch_shapes=[pltpu.VMEM((SSD_STATE, SSD_WIDTH), f32)],
        compiler_params=_cparams(("arbitrary",), VMEM_MID),
    )(pre, dtr, z, dtb, alog, dskip_e, gn)


def _out_proj_fwd(x, ys, yc, w_out, g):
    T = x.shape[0]
    tm = min(512, T)

    def body(x_ref, ys_ref, yc_ref, w_ref, g_ref, h_ref, u_ref):
        h = x_ref[...] + _dot(ys_ref[...], w_ref[0:SSD_WIDTH, :]) + _dot(yc_ref[...], w_ref[SSD_WIDTH:, :])
        h_ref[...] = h
        u_ref[...] = _rms(h, g_ref[...]).astype(bf16)

    row = pl.BlockSpec((tm, D_MODEL), lambda i: (i, 0))
    return pl.pallas_call(
        body, name="out_proj_fwd", grid=(T // tm,),
        in_specs=[row, row, row, _full(w_out.shape), _full((1, D_MODEL))],
        out_specs=[row, row],
        out_shape=[jax.ShapeDtypeStruct((T, D_MODEL), f32), jax.ShapeDtypeStruct((T, D_MODEL), bf16)],
        compiler_params=_cparams(("parallel",), VMEM_MID),
    )(x, ys, yc, w_out, g)


def _mlp_fwd(h1, u1, w_up_g, w_down, g_next):
    T = h1.shape[0]
    tm = min(512, T)
    nb = D_FF // D_MODEL

    def body(h_ref, u_ref, wu_ref, wd_ref, g_ref, r_ref, h2_ref, u2_ref, acc_ref):
        b = pl.program_id(1)

        @pl.when(b == 0)
        def _():
            acc_ref[...] = jnp.zeros_like(acc_ref)

        r = jnp.maximum(_dot(u_ref[...], wu_ref[0]), 0.0)
        r_ref[...] = r.astype(bf16)
        acc_ref[...] += _dot((r * r).astype(bf16), wd_ref[...])

        @pl.when(b == nb - 1)
        def _():
            h2 = h_ref[...] + acc_ref[...]
            h2_ref[...] = h2
            u2_ref[...] = _rms(h2, g_ref[...]).astype(bf16)

    row = pl.BlockSpec((tm, D_MODEL), lambda i, b: (i, 0))
    return pl.pallas_call(
        body, name="mlp_fwd", grid=(T // tm, nb),
        in_specs=[row, row, pl.BlockSpec((1, D_MODEL, D_MODEL), lambda i, b: (b, 0, 0)),
                  pl.BlockSpec((D_MODEL, D_MODEL), lambda i, b: (b, 0)), _full((1, D_MODEL))],
        out_specs=[pl.BlockSpec((tm, D_MODEL), lambda i, b: (i, b)), row, row],
        out_shape=[jax.ShapeDtypeStruct((T, D_FF), bf16), jax.ShapeDtypeStruct((T, D_MODEL), f32),
                   jax.ShapeDtypeStruct((T, D_MODEL), bf16)],
        scratch_shapes=[pltpu.VMEM((tm, D_MODEL), f32)],
        compiler_params=_cparams(("parallel", "arbitrary"), VMEM_MID),
    )(h1, u1, w_up_g, w_down, g_next)


def _ple_loss(h2, u2, p, tgt, w_pg, b_pg, w_ple, g_ple, g_fin, g_pg):
    T = h2.shape[0]
    tm = min(256, T)

    def body(h2_ref, u2_ref, p_ref, t_ref, wpg_ref, bpg_ref, wple_ref, gple_ref, gfin_ref, gpg_ref,
             loss_ref, dh2_ref, dgp_ref, dep_ref, dgfin_ref, dgple_ref, dbpg_ref, dgpg_ref):
        @pl.when(pl.program_id(0) == 0)
        def _():
            loss_ref[...] = jnp.zeros_like(loss_ref)
            dgfin_ref[...] = jnp.zeros_like(dgfin_ref)
            dgple_ref[...] = jnp.zeros_like(dgple_ref)
            dbpg_ref[...] = jnp.zeros_like(dbpg_ref)
            dgpg_ref[...] = jnp.zeros_like(dgpg_ref)

        h2 = h2_ref[...]
        gate = _sigmoid(_dot(u2_ref[...], wpg_ref[...]) + bpg_ref[...])
        e_pre = _dot(p_ref[...].astype(bf16), wple_ref[...])
        emb = _rms(e_pre, gple_ref[...])
        h3 = h2 + gate * emb
        diff = _rms(h3, gfin_ref[...]) - t_ref[...]
        sq = jnp.sum(jnp.sum(diff * diff, axis=1, keepdims=True), axis=0, keepdims=True)
        loss_ref[...] += (0.5 / D_MODEL) * sq
        dh3, dgfin = _rms_bwd(diff * (1.0 / D_MODEL), h3, gfin_ref[...])
        dgfin_ref[...] += dgfin
        dgp = dh3 * emb * gate * (1.0 - gate)
        dbpg_ref[...] += jnp.sum(dgp, axis=0, keepdims=True)
        dep, dgple = _rms_bwd(dh3 * gate, e_pre, gple_ref[...])
        dgple_ref[...] += dgple
        dgpb = dgp.astype(bf16)
        dgp_ref[...] = dgpb
        dep_ref[...] = dep.astype(bf16)
        dx, dgpg = _rms_bwd(_dot_nt(dgpb, wpg_ref[...]), h2, gpg_ref[...])
        dgpg_ref[...] += dgpg
        dh2_ref[...] = dh3 + dx

    row = pl.BlockSpec((tm, D_MODEL), lambda i: (i, 0))
    vec = _full((1, D_MODEL))
    vshape = jax.ShapeDtypeStruct((1, D_MODEL), f32)
    return pl.pallas_call(
        body, name="ple_loss", grid=(T // tm,),
        in_specs=[row, row, pl.BlockSpec((tm, PLE_DIM), lambda i: (i, 0)), row, _full(w_pg.shape), vec, _full(w_ple.shape),
                  vec, vec, vec],
        out_specs=[_full((8, LANES)), row, row, row, vec, vec, vec, vec],
        out_shape=[jax.ShapeDtypeStruct((8, LANES), f32), jax.ShapeDtypeStruct((T, D_MODEL), f32),
                   jax.ShapeDtypeStruct((T, D_MODEL), bf16), jax.ShapeDtypeStruct((T, D_MODEL), bf16),
                   vshape, vshape, vshape, vshape],
        compiler_params=_cparams(("arbitrary",), VMEM_MID),
    )(h2, u2, p, tgt, w_pg, b_pg, w_ple, g_ple, g_fin, g_pg)


def _mlp_bwd(dh2, r, w_down, w_up_g, h1, g):
    T = dh2.shape[0]
    tm = min(512, T)
    nb = D_FF // D_MODEL

    def body(dh2_ref, r_ref, wd_ref, wu_ref, h1_ref, g_ref, dhp_ref, dh1_ref, dg_ref, acc_ref):
        i, b = pl.program_id(0), pl.program_id(1)

        @pl.when(b == 0)
        def _():
            acc_ref[...] = jnp.zeros_like(acc_ref)

        @pl.when((b == 0) & (i == 0))
        def _():
            dg_ref[...] = jnp.zeros_like(dg_ref)

        dact = _dot_nt(dh2_ref[...].astype(bf16), wd_ref[...])
        dhp = (dact * 2.0 * r_ref[...].astype(f32)).astype(bf16)
        dhp_ref[...] = dhp
        acc_ref[...] += _dot_nt(dhp, wu_ref[0])

        @pl.when(b == nb - 1)
        def _():
            dx, dg = _rms_bwd(acc_ref[...], h1_ref[...], g_ref[...])
            dg_ref[...] += dg
            dh1_ref[...] = dh2_ref[...] + dx

    row = pl.BlockSpec((tm, D_MODEL), lambda i, b: (i, 0))
    return pl.pallas_call(
        body, name="mlp_bwd", grid=(T // tm, nb),
        in_specs=[row, pl.BlockSpec((tm, D_MODEL), lambda i, b: (i, b)),
                  pl.BlockSpec((D_MODEL, D_MODEL), lambda i, b: (b, 0)),
                  pl.BlockSpec((1, D_MODEL, D_MODEL), lambda i, b: (b, 0, 0)), row, _full((1, D_MODEL))],
        out_specs=[pl.BlockSpec((tm, D_MODEL), lambda i, b: (i, b)), row, _full((1, D_MODEL))],
        out_shape=[jax.ShapeDtypeStruct((T, D_FF), bf16), jax.ShapeDtypeStruct((T, D_MODEL), f32),
                   jax.ShapeDtypeStruct((1, D_MODEL), f32)],
        scratch_shapes=[pltpu.VMEM((tm, D_MODEL), f32)],
        compiler_params=_cparams(("arbitrary", "arbitrary"), VMEM_MID),
    )(dh2, r, w_down, w_up_g, h1, g)


def _out_proj_bwd(dh1, w_out, co, ln_g, ln_b):
    T = dh1.shape[0]
    tm = min(512, T)

    def body(dh_ref, w_ref, co_ref, g_ref, b_ref, dys_ref, dco_ref, dg_ref, db_ref):
        @pl.when(pl.program_id(0) == 0)
        def _():
            dg_ref[...] = jnp.zeros_like(dg_ref)
            db_ref[...] = jnp.zeros_like(db_ref)

        dhb = dh_ref[...].astype(bf16)
        dys_ref[...] = _dot_nt(dhb, w_ref[0:SSD_WIDTH, :])
        dyc = _dot_nt(dhb, w_ref[SSD_WIDTH:, :])
        co = co_ref[...]
        mu = jnp.mean(co, axis=-1, keepdims=True)
        xc = co - mu
        rstd = lax.rsqrt(jnp.mean(xc * xc, axis=-1, keepdims=True) + EPS)
        xh = xc * rstd
        yn = xh * g_ref[...] + b_ref[...]
        dyn = dyc * _dsilu(yn)
        dg_ref[...] += jnp.sum(dyn * xh, axis=0, keepdims=True)
        db_ref[...] += jnp.sum(dyn, axis=0, keepdims=True)
        dxh = dyn * g_ref[...]
        dco_ref[...] = rstd * (dxh - jnp.mean(dxh, axis=-1, keepdims=True)
                               - xh * jnp.mean(dxh * xh, axis=-1, keepdims=True))

    row = pl.BlockSpec((tm, D_MODEL), lambda i: (i, 0))
    vec = _full((1, CONF_WIDTH))
    vshape = jax.ShapeDtypeStruct((1, CONF_WIDTH), f32)
    return pl.pallas_call(
        body, name="out_proj_bwd", grid=(T // tm,),
        in_specs=[row, _full(w_out.shape), row, vec, vec],
        out_specs=[row, row, vec, vec],
        out_shape=[jax.ShapeDtypeStruct((T, SSD_WIDTH), f32), jax.ShapeDtypeStruct((T, CONF_WIDTH), f32), vshape, vshape],
        compiler_params=_cparams(("arbitrary",), VMEM_MID),
    )(dh1, w_out, co, ln_g, ln_b)


def _conv_bwd_taps(dcur, dnext, xcur, xprev, w_ref, dw_ref, bufd_ref, bufx_ref, K, hb, tm):
    bufd_ref[0:tm, :] = dcur
    bufd_ref[tm:tm + hb, :] = dnext
    bufx_ref[0:hb, :] = xprev
    bufx_ref[hb:hb + tm, :] = xcur
    dx = None
    for k in range(K):
        term = w_ref[k:k + 1, :] * bufd_ref[pl.ds(K - 1 - k, tm), :]
        dx = term if dx is None else dx + term
        dw_ref[k:k + 1, :] += jnp.sum(bufx_ref[pl.ds(hb - (K - 1) + k, tm), :] * dcur, axis=0, keepdims=True)
    return dx


def _next_halo_spec(hb, tm, C, T):
    return pl.BlockSpec((hb, C), lambda i: (jnp.minimum((i + 1) * (tm // hb), T // hb - 1), 0))


def _ssd_conv_bwd(dpre, xbc, w):
    T, C = xbc.shape
    K, hb = SSD_CONV, 8
    tm = min(256, T)
    nt = T // tm

    def body(dcur_ref, dnext_ref, xcur_ref, xprev_ref, w_ref, dx_ref, dw_ref, db_ref, bufd_ref, bufx_ref):
        i = pl.program_id(0)

        @pl.when(i == 0)
        def _():
            dw_ref[...] = jnp.zeros_like(dw_ref)
            db_ref[...] = jnp.zeros_like(db_ref)

        dcur = dcur_ref[...]
        dx = _conv_bwd_taps(dcur, dnext_ref[...] * jnp.where(i < nt - 1, 1.0, 0.0), xcur_ref[...],
                            xprev_ref[...] * jnp.where(i > 0, 1.0, 0.0), w_ref, dw_ref, bufd_ref, bufx_ref, K, hb, tm)
        dx_ref[...] = dx.astype(bf16)
        db_ref[...] += jnp.sum(dcur, axis=0, keepdims=True)

    row = pl.BlockSpec((tm, C), lambda i: (i, 0))
    return pl.pallas_call(
        body, name="ssd_conv_bwd", grid=(nt,),
        in_specs=[row, _next_halo_spec(hb, tm, C, T), row, _prev_halo_spec(hb, tm, C), _full(w.shape)],
        out_specs=[row, _full((8, C)), _full((1, C))],
        out_shape=[jax.ShapeDtypeStruct((T, C), bf16), jax.ShapeDtypeStruct((8, C), f32), jax.ShapeDtypeStruct((1, C), f32)],
        scratch_shapes=[pltpu.VMEM((tm + hb, C), f32), pltpu.VMEM((hb + tm, C), f32)],
        compiler_params=_cparams(("arbitrary",), VMEM_MID),
    )(dpre, dpre, xbc, xbc, w)


def _conf_conv_bwd(dco, v, w, cv, cg):
    T, C = v.shape
    K, hb = CONF_KERNEL, 32
    tm = min(256, T)
    nt = T // tm

    def body(dcur_ref, dnext_ref, vcur_ref, vprev_ref, w_ref, cv_ref, cg_ref, dcv_ref, dcg_ref, dw_ref, db_ref,
             bufd_ref, bufx_ref):
        i = pl.program_id(0)

        @pl.when(i == 0)
        def _():
            dw_ref[...] = jnp.zeros_like(dw_ref)
            db_ref[...] = jnp.zeros_like(db_ref)

        dcur = dcur_ref[...]
        dv = _conv_bwd_taps(dcur, dnext_ref[...] * jnp.where(i < nt - 1, 1.0, 0.0), vcur_ref[...],
                            vprev_ref[...] * jnp.where(i > 0, 1.0, 0.0), w_ref, dw_ref, bufd_ref, bufx_ref, K, hb, tm)
        db_ref[...] += jnp.sum(dcur, axis=0, keepdims=True)
        s = _sigmoid(cg_ref[...])
        dcv_ref[...] = (dv * s).astype(bf16)
        dcg_ref[...] = (dv * cv_ref[...] * s * (1.0 - s)).astype(bf16)

    row = pl.BlockSpec((tm, C), lambda i: (i, 0))
    return pl.pallas_call(
        body, name="conf_conv_bwd", grid=(nt,),
        in_specs=[row, _next_halo_spec(hb, tm, C, T), row, _prev_halo_spec(hb, tm, C), _full(w.shape), row, row],
        out_specs=[row, row, _full((32, C)), _full((1, C))],
        out_shape=[jax.ShapeDtypeStruct((T, C), bf16), jax.ShapeDtypeStruct((T, C), bf16),
                   jax.ShapeDtypeStruct((32, C), f32), jax.ShapeDtypeStruct((1, C), f32)],
        scratch_shapes=[pltpu.VMEM((tm + hb, C), f32), pltpu.VMEM((hb + tm, C), f32)],
        compiler_params=_cparams(("arbitrary",), VMEM_MID),
    )(dco, dco, v, v, w, cv, cg)


def _ssd_bwd(dys, y, z, pre, dtr, sprev, dtb, alog, dskip_e, gn):
    T = pre.shape[0]
    nc = T // CHUNK
    GW = SSD_WIDTH // 2

    def body(dys_ref, y_ref, z_ref, pre_ref, dtr_ref, sp_ref, dtb_ref, alog_ref, de_ref, gn_ref,
             dz_ref, dpre_ref, ddtr_ref, dgn_ref, dd_ref, dal_ref, ddtb_ref, ds_ref):
        @pl.when(pl.program_id(0) == 0)
        def _():
            ds_ref[...] = jnp.zeros_like(ds_ref)
            dgn_ref[...] = jnp.zeros_like(dgn_ref)
            dd_ref[...] = jnp.zeros_like(dd_ref)
            dal_ref[...] = jnp.zeros_like(dal_ref)
            ddtb_ref[...] = jnp.zeros_like(ddtb_ref)

        e = _head_matrix()
        pre = pre_ref[...]
        dtr_b = dtr_ref[...] + dtb_ref[...]
        q = _ssd_chunk_common(pre, dtr_ref[...], dtb_ref[...], alog_ref[...], e)
        cs, tri, xc, xd, xs, dt = q["cs"], q["tri"], q["xc"], q["xd"], q["xs"], q["dt"]
        cs_t = cs.T
        st = sp_ref[0]
        dsn = ds_ref[...]
        lane = lax.broadcasted_iota(jnp.int32, (1, LANES), 1)
        halves = (lane < HEAD_DIM, lane >= HEAD_DIM)
        row_i = lax.broadcasted_iota(jnp.int32, (CHUNK, CHUNK), 0)
        col_i = lax.broadcasted_iota(jnp.int32, (CHUNK, CHUNK), 1)
        tri_t = col_i >= row_i

        y = y_ref[...]
        zz = z_ref[...]
        sz = _sigmoid(zz)
        silu_z = zz * sz
        v = y * silu_z
        dout = dys_ref[...]
        gn_v = gn_ref[...]
        dv, vh = [], []
        for g in range(2):
            vg = _group(v, g, GW)
            rstd = lax.rsqrt(jnp.mean(vg * vg, axis=-1, keepdims=True) + EPS)
            vhg = vg * rstd
            dvh = _group(dout, g, GW) * _group(gn_v, g, GW)
            dv.append(rstd * (dvh - vhg * jnp.mean(dvh * vhg, axis=-1, keepdims=True)))
            vh.append(vhg)
        dv = jnp.concatenate(dv, axis=1)
        dgn_ref[...] += jnp.sum(dout * jnp.concatenate(vh, axis=1), axis=0, keepdims=True)
        dy = dv * silu_z
        dz_ref[...] = (dv * y * (sz * (1.0 + zz * (1.0 - sz)))).astype(bf16)

        dd_row = jnp.sum(dy * xs, axis=0, keepdims=True)
        dd_ref[...] += _contract(jnp.broadcast_to(dd_row, (8, SSD_WIDTH)), e)[0:1, :]
        dxs = dy * de_ref[...]

        dz_in = dy * q["ecs_e"]
        g_mat, gt_mat, dcm, dbm, dsp, dxd, y_off = [], [], [], [], [], [], []
        bgs, cgs = [], []
        for g in range(2):
            bg = _group(q["bm"], g, SSD_STATE)
            cg = _group(q["cm"], g, SSD_STATE)
            bgb, cgb = bg.astype(bf16), cg.astype(bf16)
            bgs.append(bgb)
            cgs.append(cgb)
            stg = _group(st, g, GW).astype(bf16)
            dsng = _group(dsn, g, GW).astype(bf16)
            dzg = _group(dz_in, g, GW).astype(bf16)
            g_mat.append(_dot_nt(cgb, bgb))
            gt_mat.append(_dot_nt(bgb, cgb))
            y_off.append(_dot(cgb, stg))
            dcm.append(_dot_nt(dzg, stg))
            dsp.append(_dot(cg.T.astype(bf16), dzg))
            dbm.append(_dot_nt(_group(xd, g, GW).astype(bf16), dsng))
            dxd.append(_dot(bgb, dsng))
        y_off = jnp.concatenate(y_off, axis=1) * q["ecs_e"]
        dxd = jnp.concatenate(dxd, axis=1)
        ds_ref[...] = dsn * q["cd_e"] + jnp.concatenate(dsp, axis=1)
        dcd_row = jnp.sum(dsn * st, axis=0, keepdims=True) * q["cd_e"]
        t_e = dxd * xd
        dcs = _contract(dy * y_off - t_e, e)
        last_row = _contract(jnp.broadcast_to(dcd_row + jnp.sum(t_e, axis=0, keepdims=True), (8, SSD_WIDTH)), e)[0:1, :]
        dxc_state = dxd * q["dte_e"]

        dg_acc = [jnp.zeros((CHUNK, CHUNK), f32), jnp.zeros((CHUNK, CHUNK), f32)]
        dgt_acc = [jnp.zeros((CHUNK, CHUNK), f32), jnp.zeros((CHUNK, CHUNK), f32)]
        dxc_pairs = []
        for j in range(SSD_HEADS // 2):
            dyp_f = dy[:, j * LANES:(j + 1) * LANES]
            xcp_f = xc[:, j * LANES:(j + 1) * LANES]
            acc = jnp.zeros((CHUNK, LANES), f32)
            for hh in range(2):
                h = 2 * j + hh
                g = h // 8
                dyp = jnp.where(halves[hh], dyp_f, 0.0).astype(bf16)
                xcp = jnp.where(halves[hh], xcp_f, 0.0).astype(bf16)
                lm = jnp.exp(jnp.where(tri, cs[:, h:h + 1] - cs_t[h:h + 1, :], -1e30))
                lm_t = jnp.exp(jnp.where(tri_t, cs_t[h:h + 1, :] - cs[:, h:h + 1], -1e30))
                dm = _dot_nt(dyp, xcp) * lm
                dm_t = _dot_nt(xcp, dyp) * lm_t
                acc = acc + _dot((gt_mat[g] * lm_t).astype(bf16), dyp)
                dg_acc[g] = dg_acc[g] + dm
                dgt_acc[g] = dgt_acc[g] + dm_t
                qd = jnp.sum(dm * g_mat[g] - dm_t * gt_mat[g], axis=1, keepdims=True)
                dcs = dcs + qd * (lane == h).astype(f32)
            dxc_pairs.append(acc)
        dxc = jnp.concatenate(dxc_pairs, axis=1) + dxc_state
        for g in range(2):
            dcm[g] = dcm[g] + _dot(dg_acc[g].astype(bf16), bgs[g])
            dbm[g] = dbm[g] + _dot(dgt_acc[g].astype(bf16), cgs[g])

        dxs = dxs + dxc * q["dt_e"]
        ddt = _contract(dxc * xs, e)
        dcs = dcs + jnp.where(row_i == CHUNK - 1, jnp.broadcast_to(last_row, (CHUNK, LANES)), 0.0)
        da = jnp.dot(tri_t.astype(f32), dcs, precision=lax.Precision.HIGHEST, preferred_element_type=f32)
        ddt = ddt + da * q["a_neg"]
        dal_ref[...] += jnp.sum(da * dt, axis=0, keepdims=True) * q["a_neg"]
        ddtr = ddt * _sigmoid(dtr_b) * (lane < SSD_HEADS).astype(f32)
        ddtb_ref[...] += jnp.sum(ddtr, axis=0, keepdims=True)
        ddtr_ref[...] = ddtr.astype(bf16)

        dact = jnp.concatenate([dxs, dbm[0], dbm[1], dcm[0], dcm[1]], axis=1)
        dpre_ref[...] = dact * _dsilu(pre)

    rev = lambda n: pl.BlockSpec((CHUNK, n), lambda c: (nc - 1 - c, 0))
    vec = _full((1, LANES))
    vshape = jax.ShapeDtypeStruct((1, LANES), f32)
    return pl.pallas_call(
        body, name="ssd_bwd", grid=(nc,),
        in_specs=[rev(SSD_WIDTH), rev(SSD_WIDTH), rev(SSD_WIDTH), rev(XBC_WIDTH), rev(LANES),
                  pl.BlockSpec((1, SSD_STATE, SSD_WIDTH), lambda c: (nc - 1 - c, 0, 0)),
                  vec, vec, _full((1, SSD_WIDTH)), _full((1, SSD_WIDTH))],
        out_specs=[rev(SSD_WIDTH), rev(XBC_WIDTH), rev(LANES), _full((1, SSD_WIDTH)), vec, vec, vec],
        out_shape=[jax.ShapeDtypeStruct((T, SSD_WIDTH), bf16), jax.ShapeDtypeStruct((T, XBC_WIDTH), f32),
                   jax.ShapeDtypeStruct((T, LANES), bf16), jax.ShapeDtypeStruct((1, SSD_WIDTH), f32),
                   vshape, vshape, vshape],
        scratch_shapes=[pltpu.VMEM((SSD_STATE, SSD_WIDTH), f32)],
        compiler_params=_cparams(("arbitrary",), VMEM_MID),
    )(dys, y, z, pre, dtr, sprev, dtb, alog, dskip_e, gn)


def _in_proj_bwd(dz, dxbc, dcv, dcg, ddt, wz, wxbc, wcv, wcg, wdt, x, dh1, g):
    T = x.shape[0]
    tm = min(256, T)

    def body(dz_ref, dx_ref, dcv_ref, dcg_ref, ddt_ref, wz_ref, wx_ref, wcv_ref, wcg_ref, wdt_ref, x_ref, dh_ref, g_ref,
             gx_ref, dg_ref):
        @pl.when(pl.program_id(0) == 0)
        def _():
            dg_ref[...] = jnp.zeros_like(dg_ref)

        du = (_dot_nt(dz_ref[...], wz_ref[...]) + _dot_nt(dx_ref[...], wx_ref[...]) + _dot_nt(dcv_ref[...], wcv_ref[...])
              + _dot_nt(dcg_ref[...], wcg_ref[...]) + _dot_nt(ddt_ref[...], wdt_ref[...]))
        dx, dg = _rms_bwd(du, x_ref[...], g_ref[...])
        dg_ref[...] += dg
        gx_ref[...] = dh_ref[...] + dx

    row = lambda n: pl.BlockSpec((tm, n), lambda i: (i, 0))
    return pl.pallas_call(
        body, name="in_proj_bwd", grid=(T // tm,),
        in_specs=[row(SSD_WIDTH), row(XBC_WIDTH), row(CONF_WIDTH), row(CONF_WIDTH), row(LANES), _full(wz.shape),
                  _full(wxbc.shape), _full(wcv.shape), _full(wcg.shape), _full(wdt.shape), row(D_MODEL), row(D_MODEL),
                  _full((1, D_MODEL))],
        out_specs=[row(D_MODEL), _full((1, D_MODEL))],
        out_shape=[jax.ShapeDtypeStruct((T, D_MODEL), f32), jax.ShapeDtypeStruct((1, D_MODEL), f32)],
        compiler_params=_cparams(("arbitrary",), VMEM_BIG),
    )(dz, dxbc, dcv, dcg, ddt, wz, wxbc, wcv, wcg, wdt, x, dh1, g)


def _weight_grad(a, g, name, square=False):
    T, K = a.shape
    N = g.shape[1]
    tk = min(512, K)
    tn = min(512, N)
    tt = min(2048, T)

    def body(a_ref, g_ref, o_ref):
        @pl.when(pl.program_id(2) == 0)
        def _():
            o_ref[...] = jnp.zeros_like(o_ref)

        av = a_ref[...]
        if square:
            av = av.astype(f32)
            av = av * av
        o_ref[...] += _dot_tn(av.astype(bf16), g_ref[...].astype(bf16))

    return pl.pallas_call(
        body, name=name, grid=(K // tk, N // tn, T // tt),
        in_specs=[pl.BlockSpec((tt, tk), lambda i, j, t: (t, i)), pl.BlockSpec((tt, tn), lambda i, j, t: (t, j))],
        out_specs=pl.BlockSpec((tk, tn), lambda i, j, t: (i, j)),
        out_shape=jax.ShapeDtypeStruct((K, N), f32),
        compiler_params=_cparams(("parallel", "parallel", "arbitrary"), VMEM_MID),
    )(a, g)


def _place():
    return lax.axis_index("x"), lax.axis_index("y"), lax.axis_index("c")


def _other_chips(x, y):
    return [(1 - x, y), (x, 1 - y), (1 - x, 1 - y)]


def _remote(src, dst, ssem, rsem, dev):
    return pltpu.make_async_remote_copy(src_ref=src, dst_ref=dst, send_sem=ssem, recv_sem=rsem, device_id=dev,
                                        device_id_type=MESH)


def _gather_weights(slab, convw):
    def body(slab_ref, cw_ref, out_ref, cwo_ref, ssem, rsem, lsem):
        x, y, c = _place()
        me_b = 2 * x + y
        sib = (x, y, 1 - c)
        chips = _other_chips(x, y)
        mine = pl.ds(c * SLAB_H, SLAB_H)
        theirs = pl.ds((1 - c) * SLAB_H, SLAB_H)
        l1 = pltpu.make_async_copy(slab_ref, out_ref.at[me_b], lsem.at[0])
        l2 = pltpu.make_async_copy(cw_ref, cwo_ref.at[me_b], lsem.at[1])
        l1.start()
        l2.start()
        sends = []
        for k, (px, py) in enumerate(chips):
            sends.append(_remote(slab_ref.at[mine], out_ref.at[me_b, mine], ssem.at[k], rsem.at[k], (px, py, c)))
            sends.append(_remote(cw_ref, cwo_ref.at[me_b], ssem.at[6 + k], rsem.at[6 + k], (px, py, c)))
        for cp in sends:
            cp.start()
        for k, (px, py) in enumerate(chips):
            b = 2 * px + py
            _remote(slab_ref.at[mine], out_ref.at[b, mine], ssem.at[k], rsem.at[k], (px, py, c)).wait_recv()
            fw = _remote(out_ref.at[b, mine], out_ref.at[b, mine], ssem.at[3 + k], rsem.at[3 + k], sib)
            fw.start()
            sends.append(fw)
        for k, (px, py) in enumerate(chips):
            b = 2 * px + py
            _remote(cw_ref, cwo_ref.at[b], ssem.at[6 + k], rsem.at[6 + k], (px, py, c)).wait_recv()
            _remote(out_ref.at[b, theirs], out_ref.at[b, theirs], ssem.at[3 + k], rsem.at[3 + k], sib).wait_recv()
        for cp in sends:
            cp.wait_send()
        l1.wait()
        l2.wait()

    return pl.pallas_call(
        body, name="gather_weights", in_specs=[ANY, ANY], out_specs=[ANY, ANY],
        out_shape=[jax.ShapeDtypeStruct((N_CHIPS, SLAB_R, D_MODEL), bf16),
                   jax.ShapeDtypeStruct((N_CHIPS, CONVW_ROWS, D_MODEL), f32)],
        scratch_shapes=[pltpu.SemaphoreType.DMA((9,)), pltpu.SemaphoreType.DMA((9,)), pltpu.SemaphoreType.DMA((2,))],
    )(slab, convw)


def _swap_halves(gslab):
    def body(g_ref, recv_ref, ssem, rsem):
        x, y, c = _place()
        cp = _remote(g_ref.at[:, pl.ds((1 - c) * SLAB_H, SLAB_H), :], recv_ref, ssem, rsem, (x, y, 1 - c))
        cp.start()
        cp.wait()

    return pl.pallas_call(
        body, name="swap_halves", in_specs=[ANY], out_specs=ANY,
        out_shape=jax.ShapeDtypeStruct((N_CHIPS, SLAB_H, D_MODEL), f32),
        scratch_shapes=[pltpu.SemaphoreType.DMA(()), pltpu.SemaphoreType.DMA(())],
    )(gslab)


def _chip_sum(cidx, gslab, recv):
    tr = SLAB_H // 8

    def body(c_ref, g_ref, r_ref, o_ref):
        o_ref[...] = (g_ref[...] + r_ref[...]).astype(bf16)

    return pl.pallas_call(
        body, name="chip_sum",
        grid_spec=pltpu.PrefetchScalarGridSpec(
            num_scalar_prefetch=1, grid=(N_CHIPS, SLAB_H // tr),
            in_specs=[pl.BlockSpec((1, tr, D_MODEL), lambda b, i, c_ref: (b, c_ref[0] * (SLAB_H // tr) + i, 0)),
                      pl.BlockSpec((1, tr, D_MODEL), lambda b, i, c_ref: (b, i, 0))],
            out_specs=pl.BlockSpec((1, tr, D_MODEL), lambda b, i, c_ref: (b, i, 0))),
        out_shape=jax.ShapeDtypeStruct((N_CHIPS, SLAB_H, D_MODEL), bf16),
        compiler_params=_cparams(("parallel", "parallel"), VMEM_MID),
    )(cidx, gslab, recv)


def _exchange(hb, small):
    def body(hb_ref, sm_ref, recv_ref, all_ref, ssem, rsem, lsem):
        x, y, c = _place()
        me = 4 * x + 2 * y + c
        chips = _other_chips(x, y)
        loc = pltpu.make_async_copy(sm_ref, all_ref.at[me], lsem)
        loc.start()
        sends = []
        for k, (px, py) in enumerate(chips):
            sends.append(_remote(hb_ref.at[2 * px + py], recv_ref.at[k], ssem.at[k], rsem.at[k], (px, py, c)))
        peers = []
        for r in range(1, N_DEV):
            peer = ((1 - x) if r & 4 else x, (1 - y) if r & 2 else y, (1 - c) if r & 1 else c)
            peers.append(peer)
            sends.append(_remote(sm_ref, all_ref.at[me], ssem.at[2 + r], rsem.at[2 + r], peer))
        for cp in sends:
            cp.start()
        for k, (px, py) in enumerate(chips):
            _remote(hb_ref.at[0], recv_ref.at[k], ssem.at[k], rsem.at[k], (px, py, c)).wait_recv()
        for r, peer in zip(range(1, N_DEV), peers):
            pid = 4 * peer[0] + 2 * peer[1] + peer[2]
            _remote(sm_ref, all_ref.at[pid], ssem.at[2 + r], rsem.at[2 + r], peer).wait_recv()
        for cp in sends:
            cp.wait_send()
        loc.wait()

    return pl.pallas_call(
        body, name="exchange", in_specs=[ANY, ANY], out_specs=[ANY, ANY],
        out_shape=[jax.ShapeDtypeStruct((3, SLAB_H, D_MODEL), bf16), jax.ShapeDtypeStruct((N_DEV, SMALL_ROWS, D_MODEL), f32)],
        scratch_shapes=[pltpu.SemaphoreType.DMA((10,)), pltpu.SemaphoreType.DMA((10,)), pltpu.SemaphoreType.DMA(())],
    )(hb, small)


def _final_sum(idx, gslab, recv_sib, recv_ici):
    tr = SLAB_H // 8

    def body(i_ref, g_ref, r_ref, p_ref, o_ref):
        acc = g_ref[0] + r_ref[0]
        for k in range(3):
            acc = acc + p_ref[k].astype(f32)
        o_ref[...] = acc

    return pl.pallas_call(
        body, name="final_sum",
        grid_spec=pltpu.PrefetchScalarGridSpec(
            num_scalar_prefetch=1, grid=(SLAB_H // tr,),
            in_specs=[pl.BlockSpec((1, tr, D_MODEL), lambda i, s: (s[1], s[0] * (SLAB_H // tr) + i, 0)),
                      pl.BlockSpec((1, tr, D_MODEL), lambda i, s: (s[1], i, 0)),
                      pl.BlockSpec((3, tr, D_MODEL), lambda i, s: (0, i, 0))],
            out_specs=pl.BlockSpec((tr, D_MODEL), lambda i, s: (i, 0))),
        out_shape=jax.ShapeDtypeStruct((SLAB_H, D_MODEL), f32),
        compiler_params=_cparams(("parallel",), VMEM_MID),
    )(idx, gslab, recv_sib, recv_ici)


def _join_halves(rh):
    def body(rh_ref, out_ref, ssem, rsem, lsem):
        x, y, c = _place()
        mine = pl.ds(c * SLAB_H, SLAB_H)
        theirs = pl.ds((1 - c) * SLAB_H, SLAB_H)
        loc = pltpu.make_async_copy(rh_ref, out_ref.at[mine], lsem)
        loc.start()
        cp = _remote(rh_ref, out_ref.at[mine], ssem, rsem, (x, y, 1 - c))
        cp.start()
        _remote(rh_ref, out_ref.at[theirs], ssem, rsem, (x, y, 1 - c)).wait_recv()
        cp.wait_send()
        loc.wait()

    return pl.pallas_call(
        body, name="join_halves", in_specs=[ANY], out_specs=ANY,
        out_shape=jax.ShapeDtypeStruct((SLAB_R, D_MODEL), f32),
        scratch_shapes=[pltpu.SemaphoreType.DMA(()), pltpu.SemaphoreType.DMA(()), pltpu.SemaphoreType.DMA(())],
    )(rh)


def _sum_small(all_small):
    def body(a_ref, o_ref):
        acc = a_ref[0]
        for d in range(1, N_DEV):
            acc = acc + a_ref[d]
        o_ref[...] = acc

    return pl.pallas_call(
        body, name="sum_small", out_shape=jax.ShapeDtypeStruct((SMALL_ROWS, D_MODEL), f32),
    )(all_small)


def _adamw(w, g, m, v, name):
    R, C = w.shape
    tr = 256 if R % 256 == 0 else R
    c1 = 1.0 - ADAM_B1 ** ADAM_STEP
    c2 = 1.0 - ADAM_B2 ** ADAM_STEP

    def body(w_ref, g_ref, m_ref, v_ref, d_ref, mo_ref, vo_ref):
        gg = g_ref[...]
        m2 = ADAM_B1 * m_ref[...] + (1.0 - ADAM_B1) * gg
        v2 = ADAM_B2 * v_ref[...] + (1.0 - ADAM_B2) * (gg * gg)
        mo_ref[...] = m2
        vo_ref[...] = v2
        d_ref[...] = -ADAM_LR * ((m2 / c1) / (jnp.sqrt(v2 / c2) + ADAM_EPS) + ADAM_WD * w_ref[...])

    blk = pl.BlockSpec((tr, C), lambda i: (i, 0))
    shp = jax.ShapeDtypeStruct((R, C), f32)
    return pl.pallas_call(
        body, name=name, grid=(R // tr,), in_specs=[blk] * 4, out_specs=[blk] * 3, out_shape=[shp] * 3,
        compiler_params=_cparams(("parallel",), VMEM_MID),
    )(w, g, m, v)


def _pad_lanes(v):
    return jnp.pad(v, ((0, 0), (0, LANES - v.shape[1])))


def _local_step(x, p, tgt, W, S):
    wz, wxbc, wcv, wcg, wdt = W["wz"], W["wxbc"], W["wcv"], W["wcg"], W["wdt"]
    dtb = _pad_lanes(S["dt_bias"])
    alog = _pad_lanes(S["A_log"])
    dskip_e = jnp.repeat(S["D_skip"], HEAD_DIM, axis=1)

    u0, z, xbc, cv, cg, dtr, v = _in_proj_fwd(x, S["mix_norm_g"], wz, wxbc, wcv, wcg, wdt)
    pre = _ssd_conv_fwd(xbc, S["ssd_conv_w"], S["ssd_conv_b"])
    y, ys, sprev = _ssd_fwd(pre, dtr, z, dtb, alog, dskip_e, S["ssd_norm_g"])
    co, yc = _conf_fwd(v, S["conf_dw_w"], S["conf_dw_b"], S["conf_ln_g"], S["conf_ln_b"])
    h1, u1 = _out_proj_fwd(x, ys, yc, W["w_out"], S["mlp_norm_g"])
    r, h2, u2 = _mlp_fwd(h1, u1, W["w_up_g"], W["w_down"], S["ple_gate_norm_g"])
    loss, dh2, dgp, dep, dg_fin, dg_ple, db_pg, dg_pg = _ple_loss(
        h2, u2, p, tgt, W["w_pg"], S["b_ple_gate"], W["w_ple"], S["ple_norm_g"], S["final_norm_g"], S["ple_gate_norm_g"])

    dhp, dh1, dg_mlp = _mlp_bwd(dh2, r, W["w_down"], W["w_up_g"], h1, S["mlp_norm_g"])
    dys, dco, dg_ln, db_ln = _out_proj_bwd(dh1, W["w_out"], co, S["conf_ln_g"], S["conf_ln_b"])
    dcv, dcg, dw_conf, db_conf = _conf_conv_bwd(dco, v, S["conf_dw_w"], cv, cg)
    dz, dpre, ddtr, dg_ssdn, dd, dal, ddtb = _ssd_bwd(dys, y, z, pre, dtr, sprev, dtb, alog, dskip_e, S["ssd_norm_g"])
    dxbc, dw_sconv, db_sconv = _ssd_conv_bwd(dpre, xbc, S["ssd_conv_w"])
    gx, dg_mix = _in_proj_bwd(dz, dxbc, dcv, dcg, ddtr, wz, wxbc, wcv, wcg, wdt, x, dh1, S["mix_norm_g"])

    gw_in = jnp.concatenate([
        _weight_grad(u0, dz, "dw_in_z"), _weight_grad(u0, dxbc, "dw_in_xbc"),
        _weight_grad(u0, ddtr, "dw_in_dt")[:, :SSD_HEADS],
        _weight_grad(u0, dcv, "dw_in_cv"), _weight_grad(u0, dcg, "dw_in_cg")], axis=1)
    big = {
        "w_in": gw_in,
        "w_out": jnp.concatenate([_weight_grad(ys, dh1, "dw_out_ssd"), _weight_grad(yc, dh1, "dw_out_conf")], axis=0),
        "w_up": _weight_grad(u1, dhp, "dw_up"),
        "w_down": _weight_grad(r, dh2, "dw_down", square=True),
        "w_ple_gate": _weight_grad(u2, dgp, "dw_ple_gate"),
        "w_ple": _weight_grad(p, dep, "dw_ple"),
    }
    small = {
        "mix_norm_g": dg_mix, "ssd_conv_w": dw_sconv[:SSD_CONV], "ssd_conv_b": db_sconv,
        "dt_bias": ddtb[:, :SSD_HEADS], "A_log": dal[:, :SSD_HEADS], "D_skip": dd[:, :SSD_HEADS],
        "ssd_norm_g": dg_ssdn, "conf_dw_w": dw_conf[:CONF_KERNEL], "conf_dw_b": db_conf,
        "conf_ln_g": dg_ln, "conf_ln_b": db_ln, "mlp_norm_g": dg_mlp, "ple_gate_norm_g": dg_pg,
        "b_ple_gate": db_pg, "ple_norm_g": dg_ple, "final_norm_g": dg_fin,
    }
    return loss, gx, big, small


def _rows(a):
    return a.reshape(-1, D_MODEL)


def _pad_rows(a, n):
    flat = a.reshape(-1)
    return jnp.pad(flat, (0, n * D_MODEL - flat.shape[0])).reshape(n, D_MODEL)


def _slab_of_shards(sh):
    parts = [_rows(sh[n]) for n, _ in SLAB_ROWS]
    parts.append(jnp.zeros((SLAB_R - SLAB_USED, D_MODEL), parts[0].dtype))
    return jnp.concatenate(parts, axis=0)


SHARD_SHAPES = {"w_in": (1024, 1156), "w_out": (512, 1024), "w_up": (1024, 1024), "w_down": (1024, 1024),
                "w_ple_gate": (256, 1024), "w_ple": (256, 256)}


def _shards_of_slab(slab):
    out, o = {}, 0
    for n, r in SLAB_ROWS:
        out[n] = slab[o:o + r].reshape(SHARD_SHAPES[n])
        o += r
    return out


SMALL_LAYOUT = (("mix_norm_g", 1), ("ssd_norm_g", 1), ("conf_dw_b", 1), ("conf_ln_g", 1), ("conf_ln_b", 1),
                ("mlp_norm_g", 1), ("ple_gate_norm_g", 1), ("b_ple_gate", 1), ("ple_norm_g", 1), ("final_norm_g", 1),
                ("ssd_conv_b", 2), ("dt_bias", 1), ("A_log", 1), ("D_skip", 1), ("loss", 1),
                ("ssd_conv_w", 6), ("conf_dw_w", 31))


def _pack_small(d):
    parts = [_pad_rows(d[n], r) for n, r in SMALL_LAYOUT]
    used = sum(r for _, r in SMALL_LAYOUT)
    parts.append(jnp.zeros((SMALL_ROWS - used, D_MODEL), f32))
    return jnp.concatenate(parts, axis=0)


def _unpack_small(a, shapes):
    out, o = {}, 0
    for n, r in SMALL_LAYOUT:
        shp = shapes[n]
        size = 1
        for s in shp:
            size *= s
        out[n] = a[o:o + r].reshape(-1)[:size].reshape(shp)
        o += r
    return out


BIG = ("w_in", "w_out", "w_up", "w_down", "w_ple_gate", "w_ple")
WEIGHTS = ("mix_norm_g", "w_in", "ssd_conv_w", "ssd_conv_b", "dt_bias", "A_log", "D_skip", "ssd_norm_g", "conf_dw_w",
           "conf_dw_b", "conf_ln_g", "conf_ln_b", "w_out", "mlp_norm_g", "w_up", "w_down", "ple_gate_norm_g",
           "w_ple_gate", "b_ple_gate", "w_ple", "ple_norm_g", "final_norm_g")


def kernel(x, p, mix_norm_g, w_in, ssd_conv_w, ssd_conv_b, dt_bias, A_log, D_skip, ssd_norm_g, conf_dw_w, conf_dw_b, conf_ln_g, conf_ln_b, w_out, mlp_norm_g, w_up, w_down, ple_gate_norm_g, w_ple_gate, b_ple_gate, w_ple, ple_norm_g, final_norm_g, loss_target, m_mix_norm_g, m_w_in, m_ssd_conv_w, m_ssd_conv_b, m_dt_bias, m_A_log, m_D_skip, m_ssd_norm_g, m_conf_dw_w, m_conf_dw_b, m_conf_ln_g, m_conf_ln_b, m_w_out, m_mlp_norm_g, m_w_up, m_w_down, m_ple_gate_norm_g, m_w_ple_gate, m_b_ple_gate, m_w_ple, m_ple_norm_g, m_final_norm_g, v_mix_norm_g, v_w_in, v_ssd_conv_w, v_ssd_conv_b, v_dt_bias, v_A_log, v_D_skip, v_ssd_norm_g, v_conf_dw_w, v_conf_dw_b, v_conf_ln_g, v_conf_ln_b, v_w_out, v_mlp_norm_g, v_w_up, v_w_down, v_ple_gate_norm_g, v_w_ple_gate, v_b_ple_gate, v_w_ple, v_ple_norm_g, v_final_norm_g):
    w = dict(mix_norm_g=mix_norm_g, w_in=w_in, ssd_conv_w=ssd_conv_w, ssd_conv_b=ssd_conv_b, dt_bias=dt_bias, A_log=A_log,
             D_skip=D_skip, ssd_norm_g=ssd_norm_g, conf_dw_w=conf_dw_w, conf_dw_b=conf_dw_b, conf_ln_g=conf_ln_g,
             conf_ln_b=conf_ln_b, w_out=w_out, mlp_norm_g=mlp_norm_g, w_up=w_up, w_down=w_down,
             ple_gate_norm_g=ple_gate_norm_g, w_ple_gate=w_ple_gate, b_ple_gate=b_ple_gate, w_ple=w_ple,
             ple_norm_g=ple_norm_g, final_norm_g=final_norm_g)
    m = dict(mix_norm_g=m_mix_norm_g, w_in=m_w_in, ssd_conv_w=m_ssd_conv_w, ssd_conv_b=m_ssd_conv_b, dt_bias=m_dt_bias,
             A_log=m_A_log, D_skip=m_D_skip, ssd_norm_g=m_ssd_norm_g, conf_dw_w=m_conf_dw_w, conf_dw_b=m_conf_dw_b,
             conf_ln_g=m_conf_ln_g, conf_ln_b=m_conf_ln_b, w_out=m_w_out, mlp_norm_g=m_mlp_norm_g, w_up=m_w_up,
             w_down=m_w_down, ple_gate_norm_g=m_ple_gate_norm_g, w_ple_gate=m_w_ple_gate, b_ple_gate=m_b_ple_gate,
             w_ple=m_w_ple, ple_norm_g=m_ple_norm_g, final_norm_g=m_final_norm_g)
    v = dict(mix_norm_g=v_mix_norm_g, w_in=v_w_in, ssd_conv_w=v_ssd_conv_w, ssd_conv_b=v_ssd_conv_b, dt_bias=v_dt_bias,
             A_log=v_A_log, D_skip=v_D_skip, ssd_norm_g=v_ssd_norm_g, conf_dw_w=v_conf_dw_w, conf_dw_b=v_conf_dw_b,
             conf_ln_g=v_conf_ln_g, conf_ln_b=v_conf_ln_b, w_out=v_w_out, mlp_norm_g=v_mlp_norm_g, w_up=v_w_up,
             w_down=v_w_down, ple_gate_norm_g=v_ple_gate_norm_g, w_ple_gate=v_w_ple_gate, b_ple_gate=v_b_ple_gate,
             w_ple=v_w_ple, ple_norm_g=v_ple_norm_g, final_norm_g=v_final_norm_g)
    xi, yi, ci = lax.axis_index("x"), lax.axis_index("y"), lax.axis_index("c")
    chip = 2 * xi + yi

    slab = _slab_of_shards({n: w[n][0] for n in BIG}).astype(bf16)
    convw = _pad_rows(jnp.concatenate([ssd_conv_w[0].reshape(-1), conf_dw_w[0].reshape(-1)]), CONVW_ROWS)
    gath, cwg = _gather_weights(slab, convw)
    per_chip = [_shards_of_slab(gath[b]) for b in range(N_CHIPS)]
    w_in_full = jnp.concatenate([per_chip[b]["w_in"] for b in range(N_CHIPS)], axis=1)
    o_dt = SSD_WIDTH + XBC_WIDTH
    o_cv = o_dt + SSD_HEADS
    W = {
        "wz": w_in_full[:, :SSD_WIDTH], "wxbc": w_in_full[:, SSD_WIDTH:o_dt],
        "wdt": jnp.pad(w_in_full[:, o_dt:o_cv], ((0, 0), (0, LANES - SSD_HEADS))),
        "wcv": w_in_full[:, o_cv:o_cv + CONF_WIDTH], "wcg": w_in_full[:, o_cv + CONF_WIDTH:],
        "w_out": jnp.concatenate([per_chip[b]["w_out"] for b in range(N_CHIPS)], axis=0),
        "w_up_g": jnp.stack([per_chip[b]["w_up"] for b in range(N_CHIPS)], axis=0),
        "w_down": jnp.concatenate([per_chip[b]["w_down"] for b in range(N_CHIPS)], axis=0),
        "w_pg": jnp.concatenate([per_chip[b]["w_ple_gate"] for b in range(N_CHIPS)], axis=0),
        "w_ple": jnp.concatenate([per_chip[b]["w_ple"] for b in range(N_CHIPS)], axis=1),
    }
    n_sc = SSD_CONV * (XBC_WIDTH // N_CHIPS)
    n_cf = CONF_KERNEL * (CONF_WIDTH // N_CHIPS)
    S = {n: w[n][0] for n in ("mix_norm_g", "ssd_conv_b", "dt_bias", "A_log", "D_skip", "ssd_norm_g", "conf_dw_b",
                              "conf_ln_g", "conf_ln_b", "mlp_norm_g", "ple_gate_norm_g", "b_ple_gate", "ple_norm_g")}
    S = {n: a.reshape(1, -1) for n, a in S.items()}
    S["final_norm_g"] = final_norm_g.reshape(1, -1)
    S["ssd_conv_w"] = jnp.concatenate(
        [cwg[b].reshape(-1)[:n_sc].reshape(SSD_CONV, XBC_WIDTH // N_CHIPS) for b in range(N_CHIPS)], axis=1)
    S["conf_dw_w"] = jnp.concatenate(
        [cwg[b].reshape(-1)[n_sc:n_sc + n_cf].reshape(CONF_KERNEL, CONF_WIDTH // N_CHIPS) for b in range(N_CHIPS)], axis=1)

    loss8, grad_x, gbig, gsmall = _local_step(x[0], p[0, 0], loss_target[0], W, S)

    def shard_of(name, g, b):
        if name in ("w_in", "w_up", "w_ple"):
            n = g.shape[1] // N_CHIPS
            return g[:, b * n:(b + 1) * n]
        n = g.shape[0] // N_CHIPS
        return g[b * n:(b + 1) * n]

    gslab = jnp.stack([_slab_of_shards({n: shard_of(n, gbig[n], b) for n in BIG}) for b in range(N_CHIPS)], axis=0)
    gsmall = dict(gsmall)
    gsmall["loss"] = loss8[0:1, 0:1]
    small = _pack_small(gsmall)

    recv_sib = _swap_halves(gslab)
    cidx = jnp.stack([ci, chip]).astype(jnp.int32)
    hb = _chip_sum(cidx, gslab, recv_sib)
    recv_ici, all_small = _exchange(hb, small)
    rh = _final_sum(cidx, gslab, recv_sib, recv_ici)
    rslab = _join_halves(rh)
    tot_small = _sum_small(all_small)

    shapes = {n: (tuple(w[n].shape[1:]) if n != "final_norm_g" else (D_MODEL,)) for n in WEIGHTS if n not in BIG}
    shapes["ssd_conv_w"] = (SSD_CONV, XBC_WIDTH)
    shapes["conf_dw_w"] = (CONF_KERNEL, CONF_WIDTH)
    shapes["loss"] = (1,)
    tot = _unpack_small(tot_small, shapes)
    loss = tot["loss"].reshape(())
    n1, n2 = XBC_WIDTH // N_CHIPS, CONF_WIDTH // N_CHIPS
    tot["ssd_conv_w"] = lax.dynamic_slice(tot["ssd_conv_w"], (0, chip * n1), (SSD_CONV, n1))
    tot["conf_dw_w"] = lax.dynamic_slice(tot["conf_dw_w"], (0, chip * n2), (CONF_KERNEL, n2))

    grads = dict(_shards_of_slab(rslab))
    for n in WEIGHTS:
        if n not in BIG:
            grads[n] = tot[n].reshape(w[n].shape)
    for n in BIG:
        grads[n] = grads[n].reshape(w[n].shape)

    delta, new_m, new_v = {}, {}, {}
    for n in BIG:
        d_, m_, v_ = _adamw(w[n][0], grads[n][0], m[n][0], v[n][0], "adamw_" + n)
        delta[n], new_m[n], new_v[n] = d_[None], m_[None], v_[None]
    small_names = [n for n in WEIGHTS if n not in BIG]
    sizes = {n: int(w[n].size) for n in small_names}
    rows_needed = sum(-(-sizes[n] // D_MODEL) for n in small_names)
    rows_pad = -(-rows_needed // 8) * 8

    def pack(d):
        parts = [_pad_rows(d[n], -(-sizes[n] // D_MODEL)) for n in small_names]
        parts.append(jnp.zeros((rows_pad - rows_needed, D_MODEL), f32))
        return jnp.concatenate(parts, axis=0)

    sd, sm, sv = _adamw(pack(w), pack(grads), pack(m), pack(v), "adamw_small")

    def unpack(a, n, o):
        r = -(-sizes[n] // D_MODEL)
        return a[o:o + r].reshape(-1)[:sizes[n]].reshape(w[n].shape), o + r

    o = 0
    for n in small_names:
        delta[n], _ = unpack(sd, n, o)
        new_m[n], _ = unpack(sm, n, o)
        new_v[n], o = unpack(sv, n, o)

    return (loss, grad_x[None], *[grads[n] for n in WEIGHTS], *[delta[n] for n in WEIGHTS],
            *[new_m[n] for n in WEIGHTS], *[new_v[n] for n in WEIGHTS])
```

```python
import jax
import jax.numpy as jnp
from jax import lax
from jax.experimental import pallas as pl
from jax.experimental.pallas import tpu as pltpu

f32 = jnp.float32
bf16 = jnp.bfloat16

D_MODEL = 1024
SSD_WIDTH = 1024
SSD_HEADS = 16
HEAD_DIM = 64
SSD_STATE = 128
XBC_WIDTH = 1536
SSD_CONV = 4
CHUNK = 128
CONF_WIDTH = 1024
CONF_KERNEL = 31
D_FF = 4096
PLE_DIM = 256
IN_WIDTH = 4624
EPS = 1e-6
N_CHIPS = 4
N_DEV = 8

ADAM_LR = 0.001
ADAM_B1 = 0.9
ADAM_B2 = 0.999
ADAM_EPS = 1e-08
ADAM_WD = 0.01
ADAM_STEP = 10

LANES = 128
VMEM_BIG = 56 * 1024 * 1024
VMEM_MID = 40 * 1024 * 1024

UP_OFF, DOWN_OFF, OUT_OFF, PG_OFF = 0, 1024, 2048, 2560
SLAB_A = 2816
SLAB_B_ROWS = (("w_ple", 64), ("w_in", 1156))
SLAB_B = 1280
SLAB_R = SLAB_A + SLAB_B
SLAB_H = SLAB_R // 2
CONVW_ROWS = 16
SMALL_ROWS = 56

MESH = pl.DeviceIdType.MESH
ANY = pl.BlockSpec(memory_space=pl.ANY)


def _cparams(sem=None, vmem=None):
    return pltpu.CompilerParams(dimension_semantics=sem, vmem_limit_bytes=vmem)


def _full(shape):
    n = len(shape)
    return pl.BlockSpec(shape, lambda *_: (0,) * n)


def _dot(a, b):
    return jnp.dot(a, b, preferred_element_type=f32)


def _dot_nt(a, b):
    return lax.dot_general(a, b, (((1,), (1,)), ((), ())), preferred_element_type=f32)


def _dot_tn(a, b):
    return lax.dot_general(a, b, (((0,), (0,)), ((), ())), preferred_element_type=f32)


def _sigmoid(x):
    return jax.nn.sigmoid(x)


def _rms(x, g):
    r = lax.rsqrt(jnp.mean(x * x, axis=-1, keepdims=True) + EPS)
    return x * r * g


def _rms_bwd(dy, x, g):
    r = lax.rsqrt(jnp.mean(x * x, axis=-1, keepdims=True) + EPS)
    xh = x * r
    dg = jnp.sum(dy * xh, axis=0, keepdims=True)
    dxh = dy * g
    dx = r * (dxh - xh * jnp.mean(dxh * xh, axis=-1, keepdims=True))
    return dx, dg


def _dsilu(x):
    s = _sigmoid(x)
    return s * (1.0 + x * (1.0 - s))


def _split3(x):
    hi = x.astype(bf16)
    r1 = x - hi.astype(f32)
    mid = r1.astype(bf16)
    lo = (r1 - mid.astype(f32)).astype(bf16)
    return hi, mid, lo


def _head_matrix():
    row = lax.broadcasted_iota(jnp.int32, (LANES, SSD_WIDTH), 0)
    col = lax.broadcasted_iota(jnp.int32, (LANES, SSD_WIDTH), 1)
    lo = row * HEAD_DIM
    return ((col >= lo) & (col < lo + HEAD_DIM)).astype(bf16)


def _expand(x, e):
    hi, mid, lo = _split3(x)
    return _dot(hi, e) + _dot(mid, e) + _dot(lo, e)


def _contract(x, e):
    hi, mid, lo = _split3(x)
    return _dot_nt(hi, e) + _dot_nt(mid, e) + _dot_nt(lo, e)


def _in_proj_fwd(x, g, wz, wxbc, wcv, wcg, wdt):
    T = x.shape[0]
    tm = min(256, T)

    def body(x_ref, g_ref, wz_ref, wx_ref, wcv_ref, wcg_ref, wdt_ref,
             u_ref, z_ref, xbc_ref, cv_ref, cg_ref, dt_ref, v_ref):
        ub = _rms(x_ref[...], g_ref[...]).astype(bf16)
        u_ref[...] = ub
        z_ref[...] = _dot(ub, wz_ref[...])
        xbc_ref[...] = _dot(ub, wx_ref[...])
        cv = _dot(ub, wcv_ref[...])
        cg = _dot(ub, wcg_ref[...])
        cv_ref[...] = cv
        cg_ref[...] = cg
        v_ref[...] = cv * _sigmoid(cg)
        dt_ref[...] = _dot(ub, wdt_ref[...])

    row = lambda n: pl.BlockSpec((tm, n), lambda i: (i, 0))
    return pl.pallas_call(
        body, name="in_proj_fwd", grid=(T // tm,),
        in_specs=[row(D_MODEL), _full((1, D_MODEL)), _full(wz.shape), _full(wxbc.shape), _full(wcv.shape),
                  _full(wcg.shape), _full(wdt.shape)],
        out_specs=[row(D_MODEL), row(SSD_WIDTH), row(XBC_WIDTH), row(CONF_WIDTH), row(CONF_WIDTH), row(LANES),
                   row(CONF_WIDTH)],
        out_shape=[jax.ShapeDtypeStruct((T, D_MODEL), bf16), jax.ShapeDtypeStruct((T, SSD_WIDTH), f32),
                   jax.ShapeDtypeStruct((T, XBC_WIDTH), f32), jax.ShapeDtypeStruct((T, CONF_WIDTH), f32),
                   jax.ShapeDtypeStruct((T, CONF_WIDTH), f32), jax.ShapeDtypeStruct((T, LANES), f32),
                   jax.ShapeDtypeStruct((T, CONF_WIDTH), f32)],
        compiler_params=_cparams(("parallel",), VMEM_BIG),
    )(x, g, wz, wxbc, wcv, wcg, wdt)


SUBLANES = 8


def _phases(offsets):
    return sorted({o % SUBLANES for o in offsets} - {0})


def _phase_shape(offsets, tm, C):
    a_max = max([o // SUBLANES for o in offsets if o % SUBLANES] or [0])
    return (max(len(_phases(offsets)), 1), tm + SUBLANES * a_max, C)


def _make_phases(buf_ref, ph_ref, offsets, tm):
    for idx, b in enumerate(_phases(offsets)):
        n = tm + SUBLANES * max(o // SUBLANES for o in offsets if o % SUBLANES == b)
        ph_ref[idx, 0:n, :] = buf_ref[pl.ds(b, n), :]


def _window(buf_ref, ph_ref, offsets, o, r0, rb):
    a, b = divmod(o, SUBLANES)
    if b == 0:
        return buf_ref[pl.ds(r0 + SUBLANES * a, rb), :]
    return ph_ref[_phases(offsets).index(b), pl.ds(r0 + SUBLANES * a, rb), :]


def _conv_rows(w_ref, buf_ref, ph_ref, offsets, r0, rb):
    acc = None
    for k, o in enumerate(offsets):
        term = w_ref[k:k + 1, :] * _window(buf_ref, ph_ref, offsets, o, r0, rb)
        acc = term if acc is None else acc + term
    return acc


def _fwd_offsets(K, hb):
    return [hb - (K - 1) + k for k in range(K)]


def _prev_halo_spec(hb, tm, C):
    return pl.BlockSpec((hb, C), lambda i: (jnp.maximum(i * (tm // hb) - 1, 0), 0))


CONV_RB = 16


def _ssd_conv_fwd(xbc, w, b):
    T, C = xbc.shape
    K, hb = SSD_CONV, 8
    tm = min(256, T)
    offs = _fwd_offsets(K, hb)

    def body(cur_ref, halo_ref, w_ref, b_ref, pre_ref, buf_ref, ph_ref):
        keep = jnp.where(pl.program_id(0) > 0, 1.0, 0.0)
        buf_ref[0:hb, :] = halo_ref[...] * keep
        buf_ref[hb:hb + tm, :] = cur_ref[...]
        _make_phases(buf_ref, ph_ref, offs, tm)

        def chunk(i, carry):
            r0 = pl.multiple_of(i * CONV_RB, CONV_RB)
            pre_ref[pl.ds(r0, CONV_RB), :] = _conv_rows(w_ref, buf_ref, ph_ref, offs, r0, CONV_RB) + b_ref[...]
            return carry

        lax.fori_loop(0, tm // CONV_RB, chunk, 0)

    return pl.pallas_call(
        body, name="ssd_conv_fwd", grid=(T // tm,),
        in_specs=[pl.BlockSpec((tm, C), lambda i: (i, 0)), _prev_halo_spec(hb, tm, C), _full(w.shape), _full((1, C))],
        out_specs=pl.BlockSpec((tm, C), lambda i: (i, 0)),
        out_shape=jax.ShapeDtypeStruct((T, C), f32),
        scratch_shapes=[pltpu.VMEM((hb + tm, C), f32), pltpu.VMEM(_phase_shape(offs, tm, C), f32)],
        compiler_params=_cparams(("parallel",), VMEM_MID),
    )(xbc, xbc, w, b)


def _conf_fwd(v, w, b, ln_g, ln_b):
    T, C = v.shape
    K, hb = CONF_KERNEL, 32
    tm = min(256, T)
    offs = _fwd_offsets(K, hb)

    def body(cur_ref, halo_ref, w_ref, b_ref, g_ref, bb_ref, co_ref, y_ref, buf_ref, ph_ref):
        keep = jnp.where(pl.program_id(0) > 0, 1.0, 0.0)
        buf_ref[0:hb, :] = halo_ref[...] * keep
        buf_ref[hb:hb + tm, :] = cur_ref[...]
        _make_phases(buf_ref, ph_ref, offs, tm)

        def chunk(i, carry):
            r0 = pl.multiple_of(i * CONV_RB, CONV_RB)
            co = _conv_rows(w_ref, buf_ref, ph_ref, offs, r0, CONV_RB) + b_ref[...]
            co_ref[pl.ds(r0, CONV_RB), :] = co
            mu = jnp.mean(co, axis=-1, keepdims=True)
            xc = co - mu
            yn = xc * lax.rsqrt(jnp.mean(xc * xc, axis=-1, keepdims=True) + EPS) * g_ref[...] + bb_ref[...]
            y_ref[pl.ds(r0, CONV_RB), :] = (yn * _sigmoid(yn)).astype(bf16)
            return carry

        lax.fori_loop(0, tm // CONV_RB, chunk, 0)

    return pl.pallas_call(
        body, name="conf_fwd", grid=(T // tm,),
        in_specs=[pl.BlockSpec((tm, C), lambda i: (i, 0)), _prev_halo_spec(hb, tm, C), _full(w.shape), _full((1, C)),
                  _full((1, C)), _full((1, C))],
        out_specs=[pl.BlockSpec((tm, C), lambda i: (i, 0)), pl.BlockSpec((tm, C), lambda i: (i, 0))],
        out_shape=[jax.ShapeDtypeStruct((T, C), f32), jax.ShapeDtypeStruct((T, C), bf16)],
        scratch_shapes=[pltpu.VMEM((hb + tm, C), f32), pltpu.VMEM(_phase_shape(offs, tm, C), f32)],
        compiler_params=_cparams(("parallel",), VMEM_MID),
    )(v, v, w, b, ln_g, ln_b)


def _ssd_chunk_common(pre, dtr, dtb, alog, e):
    act = pre * _sigmoid(pre)
    xs = act[:, :SSD_WIDTH]
    bm = act[:, SSD_WIDTH:SSD_WIDTH + 2 * SSD_STATE]
    cm = act[:, SSD_WIDTH + 2 * SSD_STATE:]
    row = lax.broadcasted_iota(jnp.int32, (CHUNK, CHUNK), 0)
    col = lax.broadcasted_iota(jnp.int32, (CHUNK, CHUNK), 1)
    tri = row >= col
    dt = jax.nn.softplus(dtr + dtb)
    a_neg = -jnp.exp(alog)
    a = dt * a_neg
    cs = jnp.dot(tri.astype(f32), a, precision=lax.Precision.HIGHEST, preferred_element_type=f32)
    cs_e = _expand(cs, e)
    dt_e = _expand(dt, e)
    csl_e = cs_e[CHUNK - 1:CHUNK, :]
    ecs_e = jnp.exp(cs_e)
    dte_e = jnp.exp(csl_e - cs_e)
    cd_e = jnp.exp(csl_e)
    xc = xs * dt_e
    xd = xc * dte_e
    return dict(xs=xs, bm=bm, cm=cm, tri=tri, dt=dt, a_neg=a_neg, cs=cs, ecs_e=ecs_e, dte_e=dte_e, cd_e=cd_e,
                dt_e=dt_e, xc=xc, xd=xd)


def _group(v, g, width):
    return v[:, g * width:(g + 1) * width]


def _ssd_fwd(pre, dtr, z, dtb, alog, dskip_e, gn):
    T = pre.shape[0]
    nc = T // CHUNK
    GW = SSD_WIDTH // 2

    def body(pre_ref, dtr_ref, z_ref, dtb_ref, alog_ref, de_ref, gn_ref, y_ref, ys_ref, sp_ref, st_ref):
        @pl.when(pl.program_id(0) == 0)
        def _():
            st_ref[...] = jnp.zeros_like(st_ref)

        e = _head_matrix()
        q = _ssd_chunk_common(pre_ref[...], dtr_ref[...], dtb_ref[...], alog_ref[...], e)
        cs, tri, xc, xd = q["cs"], q["tri"], q["xc"], q["xd"]
        cs_t = cs.T
        st = st_ref[...]
        sp_ref[0] = st
        lane = lax.broadcasted_iota(jnp.int32, (1, LANES), 1)
        halves = (lane < HEAD_DIM, lane >= HEAD_DIM)

        g_mat, y_off, s_new = [], [], []
        for g in range(2):
            bg = _group(q["bm"], g, SSD_STATE)
            cg = _group(q["cm"], g, SSD_STATE)
            bgb, cgb = bg.astype(bf16), cg.astype(bf16)
            g_mat.append(_dot_nt(cgb, bgb))
            y_off.append(_dot(cgb, _group(st, g, GW).astype(bf16)))
            s_new.append(_dot(bg.T.astype(bf16), _group(xd, g, GW).astype(bf16)))
        y_off = jnp.concatenate(y_off, axis=1) * q["ecs_e"]
        st_ref[...] = st * q["cd_e"] + jnp.concatenate(s_new, axis=1)

        pairs = []
        for j in range(SSD_HEADS // 2):
            xp = xc[:, j * LANES:(j + 1) * LANES]
            acc = jnp.zeros((CHUNK, LANES), f32)
            for hh in range(2):
                h = 2 * j + hh
                seg = cs[:, h:h + 1] - cs_t[h:h + 1, :]
                lm = jnp.exp(jnp.where(tri, seg, -1e30))
                m = (g_mat[h // 8] * lm).astype(bf16)
                acc = acc + _dot(m, jnp.where(halves[hh], xp, 0.0).astype(bf16))
            pairs.append(acc)
        y = jnp.concatenate(pairs, axis=1) + y_off + q["xs"] * de_ref[...]
        y_ref[...] = y

        zz = z_ref[...]
        v = y * (zz * _sigmoid(zz))
        outs = []
        for g in range(2):
            vg = _group(v, g, GW)
            outs.append(vg * lax.rsqrt(jnp.mean(vg * vg, axis=-1, keepdims=True) + EPS))
        ys_ref[...] = (jnp.concatenate(outs, axis=1) * gn_ref[...]).astype(bf16)

    ch = lambda n: pl.BlockSpec((CHUNK, n), lambda c: (c, 0))
    return pl.pallas_call(
        body, name="ssd_fwd", grid=(nc,),
        in_specs=[ch(XBC_WIDTH), ch(LANES), ch(SSD_WIDTH), _full((1, LANES)), _full((1, LANES)), _full((1, SSD_WIDTH)),
                  _full((1, SSD_WIDTH))],
        out_specs=[ch(SSD_WIDTH), ch(SSD_WIDTH), pl.BlockSpec((1, SSD_STATE, SSD_WIDTH), lambda c: (c, 0, 0))],
        out_shape=[jax.ShapeDtypeStruct((T, SSD_WIDTH), f32), jax.ShapeDtypeStruct((T, SSD_WIDTH), bf16),
                   jax.ShapeDtypeStruct((nc, SSD_STATE, SSD_WIDTH), f32)],
        scratch_shapes=[pltpu.VMEM((SSD_STATE, SSD_WIDTH), f32)],
        compiler_params=_cparams(("arbitrary",), VMEM_MID),
    )(pre, dtr, z, dtb, alog, dskip_e, gn)


def _w_out_spec():
    n = 2 * SSD_WIDTH // N_CHIPS
    return pl.BlockSpec((N_CHIPS, n, D_MODEL), lambda *_: (0, OUT_OFF // n, 0))


def _out_proj_fwd(x, ys, yc, gath, g):
    T = x.shape[0]
    tm = min(512, T)
    n = 2 * SSD_WIDTH // N_CHIPS

    def body(x_ref, ys_ref, yc_ref, w_ref, g_ref, h_ref, u_ref):
        h = (x_ref[...] + _dot(ys_ref[:, 0:n], w_ref[0]) + _dot(ys_ref[:, n:], w_ref[1])
             + _dot(yc_ref[:, 0:n], w_ref[2]) + _dot(yc_ref[:, n:], w_ref[3]))
        h_ref[...] = h
        u_ref[...] = _rms(h, g_ref[...]).astype(bf16)

    row = pl.BlockSpec((tm, D_MODEL), lambda i: (i, 0))
    return pl.pallas_call(
        body, name="out_proj_fwd", grid=(T // tm,),
        in_specs=[row, row, row, _w_out_spec(), _full((1, D_MODEL))],
        out_specs=[row, row],
        out_shape=[jax.ShapeDtypeStruct((T, D_MODEL), f32), jax.ShapeDtypeStruct((T, D_MODEL), bf16)],
        compiler_params=_cparams(("parallel",), VMEM_MID),
    )(x, ys, yc, gath, g)


def _w_up_spec():
    return pl.BlockSpec((1, D_MODEL, D_MODEL), lambda i, b: (b, UP_OFF // D_MODEL, 0))


def _w_down_spec():
    return pl.BlockSpec((1, D_MODEL, D_MODEL), lambda i, b: (b, DOWN_OFF // D_MODEL, 0))


def _mlp_fwd(h1, u1, gath, g_next):
    T = h1.shape[0]
    tm = min(512, T)
    nb = D_FF // D_MODEL

    def body(h_ref, u_ref, wu_ref, wd_ref, g_ref, r_ref, h2_ref, u2_ref, acc_ref):
        b = pl.program_id(1)

        @pl.when(b == 0)
        def _():
            acc_ref[...] = jnp.zeros_like(acc_ref)

        r = jnp.maximum(_dot(u_ref[...], wu_ref[0]), 0.0)
        r_ref[...] = r.astype(bf16)
        acc_ref[...] += _dot((r * r).astype(bf16), wd_ref[0])

        @pl.when(b == nb - 1)
        def _():
            h2 = h_ref[...] + acc_ref[...]
            h2_ref[...] = h2
            u2_ref[...] = _rms(h2, g_ref[...]).astype(bf16)

    row = pl.BlockSpec((tm, D_MODEL), lambda i, b: (i, 0))
    return pl.pallas_call(
        body, name="mlp_fwd", grid=(T // tm, nb),
        in_specs=[row, row, _w_up_spec(), _w_down_spec(), _full((1, D_MODEL))],
        out_specs=[pl.BlockSpec((tm, D_MODEL), lambda i, b: (i, b)), row, row],
        out_shape=[jax.ShapeDtypeStruct((T, D_FF), bf16), jax.ShapeDtypeStruct((T, D_MODEL), f32),
                   jax.ShapeDtypeStruct((T, D_MODEL), bf16)],
        scratch_shapes=[pltpu.VMEM((tm, D_MODEL), f32)],
        compiler_params=_cparams(("parallel", "arbitrary"), VMEM_MID),
    )(h1, u1, gath, gath, g_next)


def _ple_loss(h2, u2, p, tgt, gath, b_pg, w_ple, g_ple, g_fin, g_pg):
    T = h2.shape[0]
    tm = min(256, T)
    npg = D_MODEL // N_CHIPS

    def body(h2_ref, u2_ref, p_ref, t_ref, wpg_ref, bpg_ref, wple_ref, gple_ref, gfin_ref, gpg_ref,
             loss_ref, dh2_ref, dgp_ref, dep_ref, dgfin_ref, dgple_ref, dbpg_ref, dgpg_ref):
        @pl.when(pl.program_id(0) == 0)
        def _():
            loss_ref[...] = jnp.zeros_like(loss_ref)
            dgfin_ref[...] = jnp.zeros_like(dgfin_ref)
            dgple_ref[...] = jnp.zeros_like(dgple_ref)
            dbpg_ref[...] = jnp.zeros_like(dbpg_ref)
            dgpg_ref[...] = jnp.zeros_like(dgpg_ref)

        h2 = h2_ref[...]
        gate_pre = bpg_ref[...]
        for b in range(N_CHIPS):
            gate_pre = gate_pre + _dot(u2_ref[:, b * npg:(b + 1) * npg], wpg_ref[b])
        gate = _sigmoid(gate_pre)
        e_pre = _dot(p_ref[...].astype(bf16), wple_ref[...])
        emb = _rms(e_pre, gple_ref[...])
        h3 = h2 + gate * emb
        diff = _rms(h3, gfin_ref[...]) - t_ref[...]
        sq = jnp.sum(jnp.sum(diff * diff, axis=1, keepdims=True), axis=0, keepdims=True)
        loss_ref[...] += (0.5 / D_MODEL) * sq
        dh3, dgfin = _rms_bwd(diff * (1.0 / D_MODEL), h3, gfin_ref[...])
        dgfin_ref[...] += dgfin
        dgp = dh3 * emb * gate * (1.0 - gate)
        dbpg_ref[...] += jnp.sum(dgp, axis=0, keepdims=True)
        dep, dgple = _rms_bwd(dh3 * gate, e_pre, gple_ref[...])
        dgple_ref[...] += dgple
        dgpb = dgp.astype(bf16)
        dgp_ref[...] = dgpb
        dep_ref[...] = dep.astype(bf16)
        du2 = jnp.concatenate([_dot_nt(dgpb, wpg_ref[b]) for b in range(N_CHIPS)], axis=1)
        dx, dgpg = _rms_bwd(du2, h2, gpg_ref[...])
        dgpg_ref[...] += dgpg
        dh2_ref[...] = dh3 + dx

    row = pl.BlockSpec((tm, D_MODEL), lambda i: (i, 0))
    vec = _full((1, D_MODEL))
    vshape = jax.ShapeDtypeStruct((1, D_MODEL), f32)
    return pl.pallas_call(
        body, name="ple_loss", grid=(T // tm,),
        in_specs=[row, row, pl.BlockSpec((tm, PLE_DIM), lambda i: (i, 0)), row,
                  pl.BlockSpec((N_CHIPS, npg, D_MODEL), lambda i: (0, PG_OFF // npg, 0)), vec, _full(w_ple.shape),
                  vec, vec, vec],
        out_specs=[_full((8, LANES)), row, row, row, vec, vec, vec, vec],
        out_shape=[jax.ShapeDtypeStruct((8, LANES), f32), jax.ShapeDtypeStruct((T, D_MODEL), f32),
                   jax.ShapeDtypeStruct((T, D_MODEL), bf16), jax.ShapeDtypeStruct((T, D_MODEL), bf16),
                   vshape, vshape, vshape, vshape],
        compiler_params=_cparams(("arbitrary",), VMEM_MID),
    )(h2, u2, p, tgt, gath, b_pg, w_ple, g_ple, g_fin, g_pg)


def _mlp_bwd(dh2, r, gath, h1, g):
    T = dh2.shape[0]
    tm = min(512, T)
    nb = D_FF // D_MODEL

    def body(dh2_ref, r_ref, wd_ref, wu_ref, h1_ref, g_ref, dhp_ref, dh1_ref, dg_ref, acc_ref):
        i, b = pl.program_id(0), pl.program_id(1)

        @pl.when(b == 0)
        def _():
            acc_ref[...] = jnp.zeros_like(acc_ref)

        @pl.when((b == 0) & (i == 0))
        def _():
            dg_ref[...] = jnp.zeros_like(dg_ref)

        dact = _dot_nt(dh2_ref[...].astype(bf16), wd_ref[0])
        dhp = (dact * 2.0 * r_ref[...].astype(f32)).astype(bf16)
        dhp_ref[...] = dhp
        acc_ref[...] += _dot_nt(dhp, wu_ref[0])

        @pl.when(b == nb - 1)
        def _():
            dx, dg = _rms_bwd(acc_ref[...], h1_ref[...], g_ref[...])
            dg_ref[...] += dg
            dh1_ref[...] = dh2_ref[...] + dx

    row = pl.BlockSpec((tm, D_MODEL), lambda i, b: (i, 0))
    return pl.pallas_call(
        body, name="mlp_bwd", grid=(T // tm, nb),
        in_specs=[row, pl.BlockSpec((tm, D_MODEL), lambda i, b: (i, b)), _w_down_spec(), _w_up_spec(), row,
                  _full((1, D_MODEL))],
        out_specs=[pl.BlockSpec((tm, D_MODEL), lambda i, b: (i, b)), row, _full((1, D_MODEL))],
        out_shape=[jax.ShapeDtypeStruct((T, D_FF), bf16), jax.ShapeDtypeStruct((T, D_MODEL), f32),
                   jax.ShapeDtypeStruct((1, D_MODEL), f32)],
        scratch_shapes=[pltpu.VMEM((tm, D_MODEL), f32)],
        compiler_params=_cparams(("arbitrary", "arbitrary"), VMEM_MID),
    )(dh2, r, gath, gath, h1, g)


def _out_proj_bwd(dh1, gath, co, ln_g, ln_b):
    T = dh1.shape[0]
    tm = min(512, T)

    def body(dh_ref, w_ref, co_ref, g_ref, b_ref, dys_ref, dco_ref, dg_ref, db_ref):
        @pl.when(pl.program_id(0) == 0)
        def _():
            dg_ref[...] = jnp.zeros_like(dg_ref)
            db_ref[...] = jnp.zeros_like(db_ref)

        dhb = dh_ref[...].astype(bf16)
        dys_ref[...] = jnp.concatenate([_dot_nt(dhb, w_ref[0]), _dot_nt(dhb, w_ref[1])], axis=1)
        dyc = jnp.concatenate([_dot_nt(dhb, w_ref[2]), _dot_nt(dhb, w_ref[3])], axis=1)
        co = co_ref[...]
        mu = jnp.mean(co, axis=-1, keepdims=True)
        xc = co - mu
        rstd = lax.rsqrt(jnp.mean(xc * xc, axis=-1, keepdims=True) + EPS)
        xh = xc * rstd
        yn = xh * g_ref[...] + b_ref[...]
        dyn = dyc * _dsilu(yn)
        dg_ref[...] += jnp.sum(dyn * xh, axis=0, keepdims=True)
        db_ref[...] += jnp.sum(dyn, axis=0, keepdims=True)
        dxh = dyn * g_ref[...]
        dco_ref[...] = rstd * (dxh - jnp.mean(dxh, axis=-1, keepdims=True)
                               - xh * jnp.mean(dxh * xh, axis=-1, keepdims=True))

    row = pl.BlockSpec((tm, D_MODEL), lambda i: (i, 0))
    vec = _full((1, CONF_WIDTH))
    vshape = jax.ShapeDtypeStruct((1, CONF_WIDTH), f32)
    return pl.pallas_call(
        body, name="out_proj_bwd", grid=(T // tm,),
        in_specs=[row, _w_out_spec(), row, vec, vec],
        out_specs=[row, row, vec, vec],
        out_shape=[jax.ShapeDtypeStruct((T, SSD_WIDTH), f32), jax.ShapeDtypeStruct((T, CONF_WIDTH), f32), vshape, vshape],
        compiler_params=_cparams(("arbitrary",), VMEM_MID),
    )(dh1, gath, co, ln_g, ln_b)


def _bwd_offsets(K):
    return [K - 1 - k for k in range(K)]


def _next_halo_spec(hb, tm, C, T):
    return pl.BlockSpec((hb, C), lambda i: (jnp.minimum((i + 1) * (tm // hb), T // hb - 1), 0))


DW_RB = 8
DW_ACC_VREGS = 32


def _conv_dw(dw_ref, bufd_ref, bufx_ref, phx_ref, offs_x, tm, C):
    K = len(offs_x)
    group = max(1, DW_ACC_VREGS // (C // LANES))
    for k0 in range(0, K, group):
        ks = list(range(k0, min(k0 + group, K)))

        def step(i, accs, ks=ks):
            r0 = pl.multiple_of(i * DW_RB, DW_RB)
            d = bufd_ref[pl.ds(r0, DW_RB), :]
            return tuple(acc + _window(bufx_ref, phx_ref, offs_x, offs_x[k], r0, DW_RB) * d for k, acc in zip(ks, accs))

        accs = lax.fori_loop(0, tm // DW_RB, step, tuple(jnp.zeros((DW_RB, C), f32) for _ in ks))
        for k, acc in zip(ks, accs):
            dw_ref[k:k + 1, :] += jnp.sum(acc, axis=0, keepdims=True)


def _fill_bwd_buffers(dcur_ref, dnext_ref, xcur_ref, xprev_ref, bufd_ref, bufx_ref, phd_ref, phx_ref, offs_d, offs_x,
                      hb, tm, first, last):
    bufd_ref[0:tm, :] = dcur_ref[...]
    bufd_ref[tm:tm + hb, :] = dnext_ref[...] * jnp.where(last, 0.0, 1.0)
    bufx_ref[0:hb, :] = xprev_ref[...] * jnp.where(first, 0.0, 1.0)
    bufx_ref[hb:hb + tm, :] = xcur_ref[...]
    _make_phases(bufd_ref, phd_ref, offs_d, tm)
    _make_phases(bufx_ref, phx_ref, offs_x, tm)


def _ssd_conv_bwd(dpre, xbc, w):
    T, C = xbc.shape
    K, hb = SSD_CONV, 8
    tm = min(256, T)
    nt = T // tm
    offs_d, offs_x = _bwd_offsets(K), _fwd_offsets(K, hb)

    def body(dcur_ref, dnext_ref, xcur_ref, xprev_ref, w_ref, dx_ref, dw_ref, db_ref, bufd_ref, bufx_ref, phd_ref, phx_ref):
        i = pl.program_id(0)

        @pl.when(i == 0)
        def _():
            dw_ref[...] = jnp.zeros_like(dw_ref)
            db_ref[...] = jnp.zeros_like(db_ref)

        _fill_bwd_buffers(dcur_ref, dnext_ref, xcur_ref, xprev_ref, bufd_ref, bufx_ref, phd_ref, phx_ref, offs_d, offs_x,
                          hb, tm, i == 0, i == nt - 1)

        def chunk(j, carry):
            r0 = pl.multiple_of(j * CONV_RB, CONV_RB)
            dx_ref[pl.ds(r0, CONV_RB), :] = _conv_rows(w_ref, bufd_ref, phd_ref, offs_d, r0, CONV_RB).astype(bf16)
            return carry

        lax.fori_loop(0, tm // CONV_RB, chunk, 0)
        _conv_dw(dw_ref, bufd_ref, bufx_ref, phx_ref, offs_x, tm, C)
        db_ref[...] += jnp.sum(dcur_ref[...], axis=0, keepdims=True)

    row = pl.BlockSpec((tm, C), lambda i: (i, 0))
    return pl.pallas_call(
        body, name="ssd_conv_bwd", grid=(nt,),
        in_specs=[row, _next_halo_spec(hb, tm, C, T), row, _prev_halo_spec(hb, tm, C), _full(w.shape)],
        out_specs=[row, _full((8, C)), _full((1, C))],
        out_shape=[jax.ShapeDtypeStruct((T, C), bf16), jax.ShapeDtypeStruct((8, C), f32), jax.ShapeDtypeStruct((1, C), f32)],
        scratch_shapes=[pltpu.VMEM((tm + hb, C), f32), pltpu.VMEM((hb + tm, C), f32),
                        pltpu.VMEM(_phase_shape(offs_d, tm, C), f32),
                        pltpu.VMEM(_phase_shape(offs_x, tm, C), f32)],
        compiler_params=_cparams(("arbitrary",), VMEM_BIG),
    )(dpre, dpre, xbc, xbc, w)


def _conf_conv_bwd(dco, v, w, cv, cg):
    T, C = v.shape
    K, hb = CONF_KERNEL, 32
    tm = min(256, T)
    nt = T // tm
    offs_d, offs_x = _bwd_offsets(K), _fwd_offsets(K, hb)

    def body(dcur_ref, dnext_ref, vcur_ref, vprev_ref, w_ref, cv_ref, cg_ref, dcv_ref, dcg_ref, dw_ref, db_ref,
             bufd_ref, bufx_ref, phd_ref, phx_ref):
        i = pl.program_id(0)

        @pl.when(i == 0)
        def _():
            dw_ref[...] = jnp.zeros_like(dw_ref)
            db_ref[...] = jnp.zeros_like(db_ref)

        _fill_bwd_buffers(dcur_ref, dnext_ref, vcur_ref, vprev_ref, bufd_ref, bufx_ref, phd_ref, phx_ref, offs_d, offs_x,
                          hb, tm, i == 0, i == nt - 1)

        def chunk(j, carry):
            r0 = pl.multiple_of(j * CONV_RB, CONV_RB)
            rows = pl.ds(r0, CONV_RB)
            dv = _conv_rows(w_ref, bufd_ref, phd_ref, offs_d, r0, CONV_RB)
            s = _sigmoid(cg_ref[rows, :])
            dcv_ref[rows, :] = (dv * s).astype(bf16)
            dcg_ref[rows, :] = (dv * cv_ref[rows, :] * s * (1.0 - s)).astype(bf16)
            return carry

        lax.fori_loop(0, tm // CONV_RB, chunk, 0)
        _conv_dw(dw_ref, bufd_ref, bufx_ref, phx_ref, offs_x, tm, C)
        db_ref[...] += jnp.sum(dcur_ref[...], axis=0, keepdims=True)

    row = pl.BlockSpec((tm, C), lambda i: (i, 0))
    return pl.pallas_call(
        body, name="conf_conv_bwd", grid=(nt,),
        in_specs=[row, _next_halo_spec(hb, tm, C, T), row, _prev_halo_spec(hb, tm, C), _full(w.shape), row, row],
        out_specs=[row, row, _full((32, C)), _full((1, C))],
        out_shape=[jax.ShapeDtypeStruct((T, C), bf16), jax.ShapeDtypeStruct((T, C), bf16),
                   jax.ShapeDtypeStruct((32, C), f32), jax.ShapeDtypeStruct((1, C), f32)],
        scratch_shapes=[pltpu.VMEM((tm + hb, C), f32), pltpu.VMEM((hb + tm, C), f32),
                        pltpu.VMEM(_phase_shape(offs_d, tm, C), f32),
                        pltpu.VMEM(_phase_shape(offs_x, tm, C), f32)],
        compiler_params=_cparams(("arbitrary",), VMEM_BIG),
    )(dco, dco, v, v, w, cv, cg)


def _ssd_bwd(dys, y, z, pre, dtr, sprev, dtb, alog, dskip_e, gn):
    T = pre.shape[0]
    nc = T // CHUNK
    GW = SSD_WIDTH // 2

    def body(dys_ref, y_ref, z_ref, pre_ref, dtr_ref, sp_ref, dtb_ref, alog_ref, de_ref, gn_ref,
             dz_ref, dpre_ref, ddtr_ref, dgn_ref, dd_ref, dal_ref, ddtb_ref, ds_ref):
        @pl.when(pl.program_id(0) == 0)
        def _():
            ds_ref[...] = jnp.zeros_like(ds_ref)
            dgn_ref[...] = jnp.zeros_like(dgn_ref)
            dd_ref[...] = jnp.zeros_like(dd_ref)
            dal_ref[...] = jnp.zeros_like(dal_ref)
            ddtb_ref[...] = jnp.zeros_like(ddtb_ref)

        e = _head_matrix()
        pre = pre_ref[...]
        dtr_b = dtr_ref[...] + dtb_ref[...]
        q = _ssd_chunk_common(pre, dtr_ref[...], dtb_ref[...], alog_ref[...], e)
        cs, tri, xc, xd, xs, dt = q["cs"], q["tri"], q["xc"], q["xd"], q["xs"], q["dt"]
        cs_t = cs.T
        st = sp_ref[0]
        dsn = ds_ref[...]
        lane = lax.broadcasted_iota(jnp.int32, (1, LANES), 1)
        halves = (lane < HEAD_DIM, lane >= HEAD_DIM)
        row_i = lax.broadcasted_iota(jnp.int32, (CHUNK, CHUNK), 0)
        col_i = lax.broadcasted_iota(jnp.int32, (CHUNK, CHUNK), 1)
        tri_t = col_i >= row_i

        y = y_ref[...]
        zz = z_ref[...]
        sz = _sigmoid(zz)
        silu_z = zz * sz
        v = y * silu_z
        dout = dys_ref[...]
        gn_v = gn_ref[...]
        dv, vh = [], []
        for g in range(2):
            vg = _group(v, g, GW)
            rstd = lax.rsqrt(jnp.mean(vg * vg, axis=-1, keepdims=True) + EPS)
            vhg = vg * rstd
            dvh = _group(dout, g, GW) * _group(gn_v, g, GW)
            dv.append(rstd * (dvh - vhg * jnp.mean(dvh * vhg, axis=-1, keepdims=True)))
            vh.append(vhg)
        dv = jnp.concatenate(dv, axis=1)
        dgn_ref[...] += jnp.sum(dout * jnp.concatenate(vh, axis=1), axis=0, keepdims=True)
        dy = dv * silu_z
        dz_ref[...] = (dv * y * (sz * (1.0 + zz * (1.0 - sz)))).astype(bf16)

        dd_row = jnp.sum(dy * xs, axis=0, keepdims=True)
        dd_ref[...] += _contract(jnp.broadcast_to(dd_row, (8, SSD_WIDTH)), e)[0:1, :]
        dxs = dy * de_ref[...]

        dz_in = dy * q["ecs_e"]
        g_mat, gt_mat, dcm, dbm, dsp, dxd, y_off = [], [], [], [], [], [], []
        bgs, cgs = [], []
        for g in range(2):
            bg = _group(q["bm"], g, SSD_STATE)
            cg = _group(q["cm"], g, SSD_STATE)
            bgb, cgb = bg.astype(bf16), cg.astype(bf16)
            bgs.append(bgb)
            cgs.append(cgb)
            stg = _group(st, g, GW).astype(bf16)
            dsng = _group(dsn, g, GW).astype(bf16)
            dzg = _group(dz_in, g, GW).astype(bf16)
            g_mat.append(_dot_nt(cgb, bgb))
            gt_mat.append(_dot_nt(bgb, cgb))
            y_off.append(_dot(cgb, stg))
            dcm.append(_dot_nt(dzg, stg))
            dsp.append(_dot(cg.T.astype(bf16), dzg))
            dbm.append(_dot_nt(_group(xd, g, GW).astype(bf16), dsng))
            dxd.append(_dot(bgb, dsng))
        y_off = jnp.concatenate(y_off, axis=1) * q["ecs_e"]
        dxd = jnp.concatenate(dxd, axis=1)
        ds_ref[...] = dsn * q["cd_e"] + jnp.concatenate(dsp, axis=1)
        dcd_row = jnp.sum(dsn * st, axis=0, keepdims=True) * q["cd_e"]
        t_e = dxd * xd
        dcs = _contract(dy * y_off - t_e, e)
        last_row = _contract(jnp.broadcast_to(dcd_row + jnp.sum(t_e, axis=0, keepdims=True), (8, SSD_WIDTH)), e)[0:1, :]
        dxc_state = dxd * q["dte_e"]

        dg_acc = [jnp.zeros((CHUNK, CHUNK), f32), jnp.zeros((CHUNK, CHUNK), f32)]
        dgt_acc = [jnp.zeros((CHUNK, CHUNK), f32), jnp.zeros((CHUNK, CHUNK), f32)]
        dxc_pairs = []
        for j in range(SSD_HEADS // 2):
            dyp_f = dy[:, j * LANES:(j + 1) * LANES]
            xcp_f = xc[:, j * LANES:(j + 1) * LANES]
            acc = jnp.zeros((CHUNK, LANES), f32)
            for hh in range(2):
                h = 2 * j + hh
                g = h // 8
                dyp = jnp.where(halves[hh], dyp_f, 0.0).astype(bf16)
                xcp = jnp.where(halves[hh], xcp_f, 0.0).astype(bf16)
                lm = jnp.exp(jnp.where(tri, cs[:, h:h + 1] - cs_t[h:h + 1, :], -1e30))
                lm_t = jnp.exp(jnp.where(tri_t, cs_t[h:h + 1, :] - cs[:, h:h + 1], -1e30))
                dm = _dot_nt(dyp, xcp) * lm
                dm_t = _dot_nt(xcp, dyp) * lm_t
                acc = acc + _dot((gt_mat[g] * lm_t).astype(bf16), dyp)
                dg_acc[g] = dg_acc[g] + dm
                dgt_acc[g] = dgt_acc[g] + dm_t
                qd = jnp.sum(dm * g_mat[g] - dm_t * gt_mat[g], axis=1, keepdims=True)
                dcs = dcs + qd * (lane == h).astype(f32)
            dxc_pairs.append(acc)
        dxc = jnp.concatenate(dxc_pairs, axis=1) + dxc_state
        for g in range(2):
            dcm[g] = dcm[g] + _dot(dg_acc[g].astype(bf16), bgs[g])
            dbm[g] = dbm[g] + _dot(dgt_acc[g].astype(bf16), cgs[g])

        dxs = dxs + dxc * q["dt_e"]
        ddt = _contract(dxc * xs, e)
        dcs = dcs + jnp.where(row_i == CHUNK - 1, jnp.broadcast_to(last_row, (CHUNK, LANES)), 0.0)
        da = jnp.dot(tri_t.astype(f32), dcs, precision=lax.Precision.HIGHEST, preferred_element_type=f32)
        ddt = ddt + da * q["a_neg"]
        dal_ref[...] += jnp.sum(da * dt, axis=0, keepdims=True) * q["a_neg"]
        ddtr = ddt * _sigmoid(dtr_b) * (lane < SSD_HEADS).astype(f32)
        ddtb_ref[...] += jnp.sum(ddtr, axis=0, keepdims=True)
        ddtr_ref[...] = ddtr.astype(bf16)

        dact = jnp.concatenate([dxs, dbm[0], dbm[1], dcm[0], dcm[1]], axis=1)
        dpre_ref[...] = dact * _dsilu(pre)

    rev = lambda n: pl.BlockSpec((CHUNK, n), lambda c: (nc - 1 - c, 0))
    vec = _full((1, LANES))
    vshape = jax.ShapeDtypeStruct((1, LANES), f32)
    return pl.pallas_call(
        body, name="ssd_bwd", grid=(nc,),
        in_specs=[rev(SSD_WIDTH), rev(SSD_WIDTH), rev(SSD_WIDTH), rev(XBC_WIDTH), rev(LANES),
                  pl.BlockSpec((1, SSD_STATE, SSD_WIDTH), lambda c: (nc - 1 - c, 0, 0)),
                  vec, vec, _full((1, SSD_WIDTH)), _full((1, SSD_WIDTH))],
        out_specs=[rev(SSD_WIDTH), rev(XBC_WIDTH), rev(LANES), _full((1, SSD_WIDTH)), vec, vec, vec],
        out_shape=[jax.ShapeDtypeStruct((T, SSD_WIDTH), bf16), jax.ShapeDtypeStruct((T, XBC_WIDTH), f32),
                   jax.ShapeDtypeStruct((T, LANES), bf16), jax.ShapeDtypeStruct((1, SSD_WIDTH), f32),
                   vshape, vshape, vshape],
        scratch_shapes=[pltpu.VMEM((SSD_STATE, SSD_WIDTH), f32)],
        compiler_params=_cparams(("arbitrary",), VMEM_MID),
    )(dys, y, z, pre, dtr, sprev, dtb, alog, dskip_e, gn)


def _in_proj_bwd(dz, dxbc, dcv, dcg, ddt, wz, wxbc, wcv, wcg, wdt, x, dh1, g):
    T = x.shape[0]
    tm = min(256, T)

    def body(dz_ref, dx_ref, dcv_ref, dcg_ref, ddt_ref, wz_ref, wx_ref, wcv_ref, wcg_ref, wdt_ref, x_ref, dh_ref, g_ref,
             gx_ref, dg_ref):
        @pl.when(pl.program_id(0) == 0)
        def _():
            dg_ref[...] = jnp.zeros_like(dg_ref)

        du = (_dot_nt(dz_ref[...], wz_ref[...]) + _dot_nt(dx_ref[...], wx_ref[...]) + _dot_nt(dcv_ref[...], wcv_ref[...])
              + _dot_nt(dcg_ref[...], wcg_ref[...]) + _dot_nt(ddt_ref[...], wdt_ref[...]))
        dx, dg = _rms_bwd(du, x_ref[...], g_ref[...])
        dg_ref[...] += dg
        gx_ref[...] = dh_ref[...] + dx

    row = lambda n: pl.BlockSpec((tm, n), lambda i: (i, 0))
    return pl.pallas_call(
        body, name="in_proj_bwd", grid=(T // tm,),
        in_specs=[row(SSD_WIDTH), row(XBC_WIDTH), row(CONF_WIDTH), row(CONF_WIDTH), row(LANES), _full(wz.shape),
                  _full(wxbc.shape), _full(wcv.shape), _full(wcg.shape), _full(wdt.shape), row(D_MODEL), row(D_MODEL),
                  _full((1, D_MODEL))],
        out_specs=[row(D_MODEL), _full((1, D_MODEL))],
        out_shape=[jax.ShapeDtypeStruct((T, D_MODEL), f32), jax.ShapeDtypeStruct((1, D_MODEL), f32)],
        compiler_params=_cparams(("arbitrary",), VMEM_BIG),
    )(dz, dxbc, dcv, dcg, ddt, wz, wxbc, wcv, wcg, wdt, x, dh1, g)


def _weight_grad(a, g, name, square=False, slab=None, place=None, tk=512):
    T, K = a.shape
    N = g.shape[1]
    tk = min(tk, K)
    tn = min(512, N)
    tt = min(2048, T)

    def body(a_ref, g_ref, *rest):
        o_ref = rest[-1]
        acc = _dot_tn(_operand(a_ref[...]), g_ref[...].astype(bf16))
        t = pl.program_id(2)
        shaped = acc if slab is None else acc[None]

        @pl.when(t == 0)
        def _():
            o_ref[...] = shaped

        @pl.when(t > 0)
        def _():
            o_ref[...] += shaped

    def _operand(av):
        if square:
            av = av.astype(f32)
            av = av * av
        return av.astype(bf16)

    in_specs = [pl.BlockSpec((tt, tk), lambda i, j, t: (t, i)), pl.BlockSpec((tt, tn), lambda i, j, t: (t, j))]
    grid = (K // tk, N // tn, T // tt)
    params = _cparams(("parallel", "parallel", "arbitrary"), VMEM_MID)
    if slab is None:
        return pl.pallas_call(
            body, name=name, grid=grid, in_specs=in_specs,
            out_specs=pl.BlockSpec((tk, tn), lambda i, j, t: (i, j)),
            out_shape=jax.ShapeDtypeStruct((K, N), f32), compiler_params=params,
        )(a, g)
    return pl.pallas_call(
        body, name=name, grid=grid, in_specs=in_specs + [ANY],
        out_specs=pl.BlockSpec((1, tk, tn), lambda i, j, t: place(i, j)),
        out_shape=jax.ShapeDtypeStruct(slab.shape, f32), input_output_aliases={2: 0}, compiler_params=params,
    )(a, g, slab)


def _place():
    return lax.axis_index("x"), lax.axis_index("y"), lax.axis_index("c")


def _other_chips(x, y):
    return [(1 - x, y), (x, 1 - y), (1 - x, 1 - y)]


def _remote(src, dst, ssem, rsem, dev):
    return pltpu.make_async_remote_copy(src_ref=src, dst_ref=dst, send_sem=ssem, recv_sem=rsem, device_id=dev,
                                        device_id_type=MESH)


def _gather_weights(gath0, convw):
    def body(g_ref, cw_ref, out_ref, cwo_ref, ssem, rsem, lsem):
        x, y, c = _place()
        me_b = 2 * x + y
        sib = (x, y, 1 - c)
        chips = _other_chips(x, y)
        mine = pl.ds(c * SLAB_H, SLAB_H)
        theirs = pl.ds((1 - c) * SLAB_H, SLAB_H)
        loc = pltpu.make_async_copy(cw_ref, cwo_ref.at[me_b], lsem)
        loc.start()
        sends = []
        for k, (px, py) in enumerate(chips):
            sends.append(_remote(g_ref.at[me_b, mine], out_ref.at[me_b, mine], ssem.at[k], rsem.at[k], (px, py, c)))
            sends.append(_remote(cw_ref, cwo_ref.at[me_b], ssem.at[6 + k], rsem.at[6 + k], (px, py, c)))
        for cp in sends:
            cp.start()
        for k, (px, py) in enumerate(chips):
            b = 2 * px + py
            _remote(g_ref.at[b, mine], out_ref.at[b, mine], ssem.at[k], rsem.at[k], (px, py, c)).wait_recv()
            fw = _remote(out_ref.at[b, mine], out_ref.at[b, mine], ssem.at[3 + k], rsem.at[3 + k], sib)
            fw.start()
            sends.append(fw)
        for k, (px, py) in enumerate(chips):
            b = 2 * px + py
            _remote(cw_ref, cwo_ref.at[b], ssem.at[6 + k], rsem.at[6 + k], (px, py, c)).wait_recv()
            _remote(g_ref.at[b, theirs], out_ref.at[b, theirs], ssem.at[3 + k], rsem.at[3 + k], sib).wait_recv()
        for cp in sends:
            cp.wait_send()
        loc.wait()

    return pl.pallas_call(
        body, name="gather_weights", in_specs=[ANY, ANY], out_specs=[ANY, ANY],
        out_shape=[jax.ShapeDtypeStruct((N_CHIPS, SLAB_R, D_MODEL), bf16),
                   jax.ShapeDtypeStruct((N_CHIPS, CONVW_ROWS, D_MODEL), f32)],
        input_output_aliases={0: 0},
        scratch_shapes=[pltpu.SemaphoreType.DMA((9,)), pltpu.SemaphoreType.DMA((9,)), pltpu.SemaphoreType.DMA(())],
    )(gath0, convw)


def _swap_halves(ga, gb):
    ha, hb = SLAB_A // 2, SLAB_B // 2

    def body(a_ref, b_ref, ra_ref, rb_ref, ssem, rsem):
        x, y, c = _place()
        sib = (x, y, 1 - c)
        ca = _remote(a_ref.at[:, pl.ds((1 - c) * ha, ha), :], ra_ref, ssem.at[0], rsem.at[0], sib)
        cb = _remote(b_ref.at[:, pl.ds((1 - c) * hb, hb), :], rb_ref, ssem.at[1], rsem.at[1], sib)
        ca.start()
        cb.start()
        ca.wait()
        cb.wait()

    return pl.pallas_call(
        body, name="swap_halves", in_specs=[ANY, ANY], out_specs=[ANY, ANY],
        out_shape=[jax.ShapeDtypeStruct((N_CHIPS, ha, D_MODEL), f32), jax.ShapeDtypeStruct((N_CHIPS, hb, D_MODEL), f32)],
        scratch_shapes=[pltpu.SemaphoreType.DMA((2,)), pltpu.SemaphoreType.DMA((2,))],
    )(ga, gb)


def _chip_sum(cidx, gslab, recv, name):
    half = recv.shape[1]
    tr = half // 2

    def body(c_ref, g_ref, r_ref, o_ref):
        o_ref[...] = (g_ref[...] + r_ref[...]).astype(bf16)

    return pl.pallas_call(
        body, name=name,
        grid_spec=pltpu.PrefetchScalarGridSpec(
            num_scalar_prefetch=1, grid=(N_CHIPS, half // tr),
            in_specs=[pl.BlockSpec((1, tr, D_MODEL), lambda b, i, c_ref: (b, c_ref[0] * (half // tr) + i, 0)),
                      pl.BlockSpec((1, tr, D_MODEL), lambda b, i, c_ref: (b, i, 0))],
            out_specs=pl.BlockSpec((1, tr, D_MODEL), lambda b, i, c_ref: (b, i, 0))),
        out_shape=jax.ShapeDtypeStruct((N_CHIPS, half, D_MODEL), bf16),
        compiler_params=_cparams(("parallel", "parallel"), VMEM_MID),
    )(cidx, gslab, recv)


def _exchange(ha, hb, small):
    def body(ha_ref, hb_ref, sm_ref, ra_ref, rb_ref, all_ref, ssem, rsem, lsem):
        x, y, c = _place()
        me = 4 * x + 2 * y + c
        chips = _other_chips(x, y)
        loc = pltpu.make_async_copy(sm_ref, all_ref.at[me], lsem)
        loc.start()
        sends = []
        for k, (px, py) in enumerate(chips):
            sends.append(_remote(ha_ref.at[2 * px + py], ra_ref.at[k], ssem.at[k], rsem.at[k], (px, py, c)))
            sends.append(_remote(hb_ref.at[2 * px + py], rb_ref.at[k], ssem.at[3 + k], rsem.at[3 + k], (px, py, c)))
        peers = []
        for r in range(1, N_DEV):
            peer = ((1 - x) if r & 4 else x, (1 - y) if r & 2 else y, (1 - c) if r & 1 else c)
            peers.append(peer)
            sends.append(_remote(sm_ref, all_ref.at[me], ssem.at[5 + r], rsem.at[5 + r], peer))
        for cp in sends:
            cp.start()
        for k, (px, py) in enumerate(chips):
            _remote(ha_ref.at[0], ra_ref.at[k], ssem.at[k], rsem.at[k], (px, py, c)).wait_recv()
            _remote(hb_ref.at[0], rb_ref.at[k], ssem.at[3 + k], rsem.at[3 + k], (px, py, c)).wait_recv()
        for r, peer in zip(range(1, N_DEV), peers):
            pid = 4 * peer[0] + 2 * peer[1] + peer[2]
            _remote(sm_ref, all_ref.at[pid], ssem.at[5 + r], rsem.at[5 + r], peer).wait_recv()
        for cp in sends:
            cp.wait_send()
        loc.wait()

    return pl.pallas_call(
        body, name="exchange", in_specs=[ANY, ANY, ANY], out_specs=[ANY, ANY, ANY],
        out_shape=[jax.ShapeDtypeStruct((3,) + ha.shape[1:], bf16), jax.ShapeDtypeStruct((3,) + hb.shape[1:], bf16),
                   jax.ShapeDtypeStruct((N_DEV, SMALL_ROWS, D_MODEL), f32)],
        scratch_shapes=[pltpu.SemaphoreType.DMA((13,)), pltpu.SemaphoreType.DMA((13,)), pltpu.SemaphoreType.DMA(())],
    )(ha, hb, small)


def _final_sum(idx, gslab, recv_sib, recv_ici, name):
    half = recv_sib.shape[1]
    tr = half // 2

    def body(i_ref, g_ref, r_ref, p_ref, o_ref):
        acc = g_ref[0] + r_ref[0]
        for k in range(3):
            acc = acc + p_ref[k].astype(f32)
        o_ref[...] = acc

    return pl.pallas_call(
        body, name=name,
        grid_spec=pltpu.PrefetchScalarGridSpec(
            num_scalar_prefetch=1, grid=(half // tr,),
            in_specs=[pl.BlockSpec((1, tr, D_MODEL), lambda i, s: (s[1], s[0] * (half // tr) + i, 0)),
                      pl.BlockSpec((1, tr, D_MODEL), lambda i, s: (s[1], i, 0)),
                      pl.BlockSpec((3, tr, D_MODEL), lambda i, s: (0, i, 0))],
            out_specs=pl.BlockSpec((tr, D_MODEL), lambda i, s: (s[0] * (half // tr) + i, 0))),
        out_shape=jax.ShapeDtypeStruct((2 * half, D_MODEL), f32),
        compiler_params=_cparams(("parallel",), VMEM_MID),
    )(idx, gslab, recv_sib, recv_ici)


def _join_halves(ra, rb):
    ha, hb = SLAB_A // 2, SLAB_B // 2

    def body(a_ref, b_ref, ao_ref, bo_ref, ssem, rsem):
        x, y, c = _place()
        sib = (x, y, 1 - c)
        mine_a, theirs_a = pl.ds(c * ha, ha), pl.ds((1 - c) * ha, ha)
        mine_b, theirs_b = pl.ds(c * hb, hb), pl.ds((1 - c) * hb, hb)
        ca = _remote(a_ref.at[mine_a], ao_ref.at[mine_a], ssem.at[0], rsem.at[0], sib)
        cb = _remote(b_ref.at[mine_b], bo_ref.at[mine_b], ssem.at[1], rsem.at[1], sib)
        ca.start()
        cb.start()
        _remote(a_ref.at[theirs_a], ao_ref.at[theirs_a], ssem.at[0], rsem.at[0], sib).wait_recv()
        _remote(b_ref.at[theirs_b], bo_ref.at[theirs_b], ssem.at[1], rsem.at[1], sib).wait_recv()
        ca.wait_send()
        cb.wait_send()

    return pl.pallas_call(
        body, name="join_halves", in_specs=[ANY, ANY], out_specs=[ANY, ANY],
        out_shape=[jax.ShapeDtypeStruct((SLAB_A, D_MODEL), f32), jax.ShapeDtypeStruct((SLAB_B, D_MODEL), f32)],
        input_output_aliases={0: 0, 1: 1},
        scratch_shapes=[pltpu.SemaphoreType.DMA((2,)), pltpu.SemaphoreType.DMA((2,))],
    )(ra, rb)


def _sum_small(all_small):
    def body(a_ref, o_ref):
        acc = a_ref[0]
        for d in range(1, N_DEV):
            acc = acc + a_ref[d]
        o_ref[...] = acc

    return pl.pallas_call(
        body, name="sum_small", out_shape=jax.ShapeDtypeStruct((SMALL_ROWS, D_MODEL), f32),
    )(all_small)


def _adamw(w, g, m, v, name, g_off=0):
    R, C = w.shape
    tr = 256 if R % 256 == 0 else R
    assert g_off % tr == 0
    c1 = 1.0 - ADAM_B1 ** ADAM_STEP
    c2 = 1.0 - ADAM_B2 ** ADAM_STEP

    def body(w_ref, g_ref, m_ref, v_ref, d_ref, mo_ref, vo_ref):
        gg = g_ref[...]
        m2 = ADAM_B1 * m_ref[...] + (1.0 - ADAM_B1) * gg
        v2 = ADAM_B2 * v_ref[...] + (1.0 - ADAM_B2) * (gg * gg)
        mo_ref[...] = m2
        vo_ref[...] = v2
        d_ref[...] = -ADAM_LR * ((m2 / c1) / (jnp.sqrt(v2 / c2) + ADAM_EPS) + ADAM_WD * w_ref[...])

    blk = pl.BlockSpec((tr, C), lambda i: (i, 0))
    gblk = pl.BlockSpec((tr, C), lambda i: (g_off // tr + i, 0))
    shp = jax.ShapeDtypeStruct((R, C), f32)
    return pl.pallas_call(
        body, name=name, grid=(R // tr,), in_specs=[blk, gblk, blk, blk], out_specs=[blk] * 3, out_shape=[shp] * 3,
        compiler_params=_cparams(("parallel",), VMEM_MID),
    )(w, g, m, v)


def _pad_lanes(v):
    return jnp.pad(v, ((0, 0), (0, LANES - v.shape[1])))


def _local_step(x, p, tgt, gath, W, S):
    wz, wxbc, wcv, wcg, wdt = W["wz"], W["wxbc"], W["wcv"], W["wcg"], W["wdt"]
    dtb = _pad_lanes(S["dt_bias"])
    alog = _pad_lanes(S["A_log"])
    dskip_e = jnp.repeat(S["D_skip"], HEAD_DIM, axis=1)

    u0, z, xbc, cv, cg, dtr, v = _in_proj_fwd(x, S["mix_norm_g"], wz, wxbc, wcv, wcg, wdt)
    pre = _ssd_conv_fwd(xbc, S["ssd_conv_w"], S["ssd_conv_b"])
    y, ys, sprev = _ssd_fwd(pre, dtr, z, dtb, alog, dskip_e, S["ssd_norm_g"])
    co, yc = _conf_fwd(v, S["conf_dw_w"], S["conf_dw_b"], S["conf_ln_g"], S["conf_ln_b"])
    h1, u1 = _out_proj_fwd(x, ys, yc, gath, S["mlp_norm_g"])
    r, h2, u2 = _mlp_fwd(h1, u1, gath, S["ple_gate_norm_g"])
    loss, dh2, dgp, dep, dg_fin, dg_ple, db_pg, dg_pg = _ple_loss(
        h2, u2, p, tgt, gath, S["b_ple_gate"], W["w_ple"], S["ple_norm_g"], S["final_norm_g"], S["ple_gate_norm_g"])

    npg = D_MODEL // N_CHIPS
    ga = lax.empty((N_CHIPS, SLAB_A, D_MODEL), f32)
    ga = _weight_grad(u2, dgp, "dw_ple_gate", slab=ga, tk=npg, place=lambda i, j: (i, PG_OFF // npg, j))
    ga = _weight_grad(r, dh2, "dw_down", square=True, slab=ga, place=lambda i, j: (i // 2, DOWN_OFF // 512 + i % 2, j))
    gw_ple = _weight_grad(p, dep, "dw_ple")
    dhp, dh1, dg_mlp = _mlp_bwd(dh2, r, gath, h1, S["mlp_norm_g"])
    ga = _weight_grad(u1, dhp, "dw_up", slab=ga, place=lambda i, j: (j // 2, UP_OFF // 512 + i, j % 2))
    ga = _weight_grad(ys, dh1, "dw_out_ssd", slab=ga, place=lambda i, j: (i, OUT_OFF // 512, j))
    ga = _weight_grad(yc, dh1, "dw_out_conf", slab=ga, place=lambda i, j: (2 + i, OUT_OFF // 512, j))
    dys, dco, dg_ln, db_ln = _out_proj_bwd(dh1, gath, co, S["conf_ln_g"], S["conf_ln_b"])
    dcv, dcg, dw_conf, db_conf = _conf_conv_bwd(dco, v, S["conf_dw_w"], cv, cg)
    dz, dpre, ddtr, dg_ssdn, dd, dal, ddtb = _ssd_bwd(dys, y, z, pre, dtr, sprev, dtb, alog, dskip_e, S["ssd_norm_g"])
    dxbc, dw_sconv, db_sconv = _ssd_conv_bwd(dpre, xbc, S["ssd_conv_w"])
    gx, dg_mix = _in_proj_bwd(dz, dxbc, dcv, dcg, ddtr, wz, wxbc, wcv, wcg, wdt, x, dh1, S["mix_norm_g"])

    gw_in = jnp.concatenate([
        _weight_grad(u0, dz, "dw_in_z"), _weight_grad(u0, dxbc, "dw_in_xbc"),
        _weight_grad(u0, ddtr, "dw_in_dt")[:, :SSD_HEADS],
        _weight_grad(u0, dcv, "dw_in_cv"), _weight_grad(u0, dcg, "dw_in_cg")], axis=1)
    small = {
        "mix_norm_g": dg_mix, "ssd_conv_w": dw_sconv[:SSD_CONV], "ssd_conv_b": db_sconv,
        "dt_bias": ddtb[:, :SSD_HEADS], "A_log": dal[:, :SSD_HEADS], "D_skip": dd[:, :SSD_HEADS],
        "ssd_norm_g": dg_ssdn, "conf_dw_w": dw_conf[:CONF_KERNEL], "conf_dw_b": db_conf,
        "conf_ln_g": dg_ln, "conf_ln_b": db_ln, "mlp_norm_g": dg_mlp, "ple_gate_norm_g": dg_pg,
        "b_ple_gate": db_pg, "ple_norm_g": dg_ple, "final_norm_g": dg_fin,
    }
    return loss, gx, ga, gw_in, gw_ple, small


def _rows(a):
    return a.reshape(-1, D_MODEL)


def _pad_rows(a, n):
    flat = a.reshape(-1)
    return jnp.pad(flat, (0, n * D_MODEL - flat.shape[0])).reshape(n, D_MODEL)


def _slab_b(w_ple_shard, w_in_shard):
    parts = [_rows(w_ple_shard), _rows(w_in_shard)]
    used = sum(r for _, r in SLAB_B_ROWS)
    parts.append(jnp.zeros((SLAB_B - used, D_MODEL), parts[0].dtype))
    return jnp.concatenate(parts, axis=0)


def _unslab_b(sb):
    n_ple = SLAB_B_ROWS[0][1]
    n_in = SLAB_B_ROWS[1][1]
    return (sb[:n_ple].reshape(PLE_DIM, D_MODEL // N_CHIPS),
            sb[n_ple:n_ple + n_in].reshape(D_MODEL, IN_WIDTH // N_CHIPS))


SMALL_LAYOUT = (("mix_norm_g", 1), ("ssd_norm_g", 1), ("conf_dw_b", 1), ("conf_ln_g", 1), ("conf_ln_b", 1),
                ("mlp_norm_g", 1), ("ple_gate_norm_g", 1), ("b_ple_gate", 1), ("ple_norm_g", 1), ("final_norm_g", 1),
                ("ssd_conv_b", 2), ("dt_bias", 1), ("A_log", 1), ("D_skip", 1), ("loss", 1),
                ("ssd_conv_w", 6), ("conf_dw_w", 31))


def _pack_small(d):
    parts = [_pad_rows(d[n], r) for n, r in SMALL_LAYOUT]
    used = sum(r for _, r in SMALL_LAYOUT)
    parts.append(jnp.zeros((SMALL_ROWS - used, D_MODEL), f32))
    return jnp.concatenate(parts, axis=0)


def _unpack_small(a, shapes):
    out, o = {}, 0
    for n, r in SMALL_LAYOUT:
        shp = shapes[n]
        size = 1
        for s in shp:
            size *= s
        out[n] = a[o:o + r].reshape(-1)[:size].reshape(shp)
        o += r
    return out


BIG = ("w_in", "w_out", "w_up", "w_down", "w_ple_gate", "w_ple")
BIG_A = (("w_up", UP_OFF), ("w_down", DOWN_OFF), ("w_out", OUT_OFF), ("w_ple_gate", PG_OFF))
WEIGHTS = ("mix_norm_g", "w_in", "ssd_conv_w", "ssd_conv_b", "dt_bias", "A_log", "D_skip", "ssd_norm_g", "conf_dw_w",
           "conf_dw_b", "conf_ln_g", "conf_ln_b", "w_out", "mlp_norm_g", "w_up", "w_down", "ple_gate_norm_g",
           "w_ple_gate", "b_ple_gate", "w_ple", "ple_norm_g", "final_norm_g")


def kernel(x, p, mix_norm_g, w_in, ssd_conv_w, ssd_conv_b, dt_bias, A_log, D_skip, ssd_norm_g, conf_dw_w, conf_dw_b, conf_ln_g, conf_ln_b, w_out, mlp_norm_g, w_up, w_down, ple_gate_norm_g, w_ple_gate, b_ple_gate, w_ple, ple_norm_g, final_norm_g, loss_target, m_mix_norm_g, m_w_in, m_ssd_conv_w, m_ssd_conv_b, m_dt_bias, m_A_log, m_D_skip, m_ssd_norm_g, m_conf_dw_w, m_conf_dw_b, m_conf_ln_g, m_conf_ln_b, m_w_out, m_mlp_norm_g, m_w_up, m_w_down, m_ple_gate_norm_g, m_w_ple_gate, m_b_ple_gate, m_w_ple, m_ple_norm_g, m_final_norm_g, v_mix_norm_g, v_w_in, v_ssd_conv_w, v_ssd_conv_b, v_dt_bias, v_A_log, v_D_skip, v_ssd_norm_g, v_conf_dw_w, v_conf_dw_b, v_conf_ln_g, v_conf_ln_b, v_w_out, v_mlp_norm_g, v_w_up, v_w_down, v_ple_gate_norm_g, v_w_ple_gate, v_b_ple_gate, v_w_ple, v_ple_norm_g, v_final_norm_g):
    w = dict(mix_norm_g=mix_norm_g, w_in=w_in, ssd_conv_w=ssd_conv_w, ssd_conv_b=ssd_conv_b, dt_bias=dt_bias, A_log=A_log,
             D_skip=D_skip, ssd_norm_g=ssd_norm_g, conf_dw_w=conf_dw_w, conf_dw_b=conf_dw_b, conf_ln_g=conf_ln_g,
             conf_ln_b=conf_ln_b, w_out=w_out, mlp_norm_g=mlp_norm_g, w_up=w_up, w_down=w_down,
             ple_gate_norm_g=ple_gate_norm_g, w_ple_gate=w_ple_gate, b_ple_gate=b_ple_gate, w_ple=w_ple,
             ple_norm_g=ple_norm_g, final_norm_g=final_norm_g)
    m = dict(mix_norm_g=m_mix_norm_g, w_in=m_w_in, ssd_conv_w=m_ssd_conv_w, ssd_conv_b=m_ssd_conv_b, dt_bias=m_dt_bias,
             A_log=m_A_log, D_skip=m_D_skip, ssd_norm_g=m_ssd_norm_g, conf_dw_w=m_conf_dw_w, conf_dw_b=m_conf_dw_b,
             conf_ln_g=m_conf_ln_g, conf_ln_b=m_conf_ln_b, w_out=m_w_out, mlp_norm_g=m_mlp_norm_g, w_up=m_w_up,
             w_down=m_w_down, ple_gate_norm_g=m_ple_gate_norm_g, w_ple_gate=m_w_ple_gate, b_ple_gate=m_b_ple_gate,
             w_ple=m_w_ple, ple_norm_g=m_ple_norm_g, final_norm_g=m_final_norm_g)
    v = dict(mix_norm_g=v_mix_norm_g, w_in=v_w_in, ssd_conv_w=v_ssd_conv_w, ssd_conv_b=v_ssd_conv_b, dt_bias=v_dt_bias,
             A_log=v_A_log, D_skip=v_D_skip, ssd_norm_g=v_ssd_norm_g, conf_dw_w=v_conf_dw_w, conf_dw_b=v_conf_dw_b,
             conf_ln_g=v_conf_ln_g, conf_ln_b=v_conf_ln_b, w_out=v_w_out, mlp_norm_g=v_mlp_norm_g, w_up=v_w_up,
             w_down=v_w_down, ple_gate_norm_g=v_ple_gate_norm_g, w_ple_gate=v_w_ple_gate, b_ple_gate=v_b_ple_gate,
             w_ple=v_w_ple, ple_norm_g=v_ple_norm_g, final_norm_g=v_final_norm_g)
    xi, yi, ci = lax.axis_index("x"), lax.axis_index("y"), lax.axis_index("c")
    chip = 2 * xi + yi

    slab = jnp.concatenate([w_up[0], w_down[0], w_out[0], w_ple_gate[0], _slab_b(w_ple[0], w_in[0])], axis=0).astype(bf16)
    gath0 = lax.dynamic_update_slice(jnp.zeros((N_CHIPS, SLAB_R, D_MODEL), bf16), slab[None], (chip, 0, 0))
    convw = _pad_rows(jnp.concatenate([ssd_conv_w[0].reshape(-1), conf_dw_w[0].reshape(-1)]), CONVW_ROWS)
    gath, cwg = _gather_weights(gath0, convw)
    parts_b = [_unslab_b(gath[b, SLAB_A:]) for b in range(N_CHIPS)]
    w_in_full = jnp.concatenate([pb[1] for pb in parts_b], axis=1)
    o_dt = SSD_WIDTH + XBC_WIDTH
    o_cv = o_dt + SSD_HEADS
    W = {
        "wz": w_in_full[:, :SSD_WIDTH], "wxbc": w_in_full[:, SSD_WIDTH:o_dt],
        "wdt": jnp.pad(w_in_full[:, o_dt:o_cv], ((0, 0), (0, LANES - SSD_HEADS))),
        "wcv": w_in_full[:, o_cv:o_cv + CONF_WIDTH], "wcg": w_in_full[:, o_cv + CONF_WIDTH:],
        "w_ple": jnp.concatenate([pb[0] for pb in parts_b], axis=1),
    }
    n_sc = SSD_CONV * (XBC_WIDTH // N_CHIPS)
    n_cf = CONF_KERNEL * (CONF_WIDTH // N_CHIPS)
    S = {n: w[n][0] for n in ("mix_norm_g", "ssd_conv_b", "dt_bias", "A_log", "D_skip", "ssd_norm_g", "conf_dw_b",
                              "conf_ln_g", "conf_ln_b", "mlp_norm_g", "ple_gate_norm_g", "b_ple_gate", "ple_norm_g")}
    S = {n: a.reshape(1, -1) for n, a in S.items()}
    S["final_norm_g"] = final_norm_g.reshape(1, -1)
    S["ssd_conv_w"] = jnp.concatenate(
        [cwg[b].reshape(-1)[:n_sc].reshape(SSD_CONV, XBC_WIDTH // N_CHIPS) for b in range(N_CHIPS)], axis=1)
    S["conf_dw_w"] = jnp.concatenate(
        [cwg[b].reshape(-1)[n_sc:n_sc + n_cf].reshape(CONF_KERNEL, CONF_WIDTH // N_CHIPS) for b in range(N_CHIPS)], axis=1)

    loss8, grad_x, ga, gw_in, gw_ple, gsmall = _local_step(x[0], p[0, 0], loss_target[0], gath, W, S)

    n_in, n_ple = IN_WIDTH // N_CHIPS, D_MODEL // N_CHIPS
    gb = jnp.stack([_slab_b(gw_ple[:, b * n_ple:(b + 1) * n_ple], gw_in[:, b * n_in:(b + 1) * n_in])
                    for b in range(N_CHIPS)], axis=0)
    gsmall = dict(gsmall)
    gsmall["loss"] = loss8[0:1, 0:1]
    small = _pack_small(gsmall)

    recv_a, recv_b = _swap_halves(ga, gb)
    cidx = jnp.stack([ci, chip]).astype(jnp.int32)
    ha = _chip_sum(cidx, ga, recv_a, "chip_sum_a")
    hb = _chip_sum(cidx, gb, recv_b, "chip_sum_b")
    ici_a, ici_b, all_small = _exchange(ha, hb, small)
    ra = _final_sum(cidx, ga, recv_a, ici_a, "final_sum_a")
    rb = _final_sum(cidx, gb, recv_b, ici_b, "final_sum_b")
    ra, rb = _join_halves(ra, rb)
    tot_small = _sum_small(all_small)

    shapes = {n: (tuple(w[n].shape[1:]) if n != "final_norm_g" else (D_MODEL,)) for n in WEIGHTS if n not in BIG}
    shapes["ssd_conv_w"] = (SSD_CONV, XBC_WIDTH)
    shapes["conf_dw_w"] = (CONF_KERNEL, CONF_WIDTH)
    shapes["loss"] = (1,)
    tot = _unpack_small(tot_small, shapes)
    loss = tot["loss"].reshape(())
    n1, n2 = XBC_WIDTH // N_CHIPS, CONF_WIDTH // N_CHIPS
    tot["ssd_conv_w"] = lax.dynamic_slice(tot["ssd_conv_w"], (0, chip * n1), (SSD_CONV, n1))
    tot["conf_dw_w"] = lax.dynamic_slice(tot["conf_dw_w"], (0, chip * n2), (CONF_KERNEL, n2))

    grads = {}
    grads["w_ple"], grads["w_in"] = _unslab_b(rb)
    for n, off in BIG_A:
        grads[n] = ra[off:off + w[n].shape[1]]
    for n in WEIGHTS:
        if n not in BIG:
            grads[n] = tot[n]
    grads = {n: g.reshape(w[n].shape) for n, g in grads.items()}

    delta, new_m, new_v = {}, {}, {}
    for n, off in BIG_A:
        d_, m_, v_ = _adamw(w[n][0], ra, m[n][0], v[n][0], "adamw_" + n, g_off=off)
        delta[n], new_m[n], new_v[n] = d_[None], m_[None], v_[None]
    for n in ("w_in", "w_ple"):
        d_, m_, v_ = _adamw(w[n][0], grads[n][0], m[n][0], v[n][0], "adamw_" + n)
        delta[n], new_m[n], new_v[n] = d_[None], m_[None], v_[None]
    small_names = [n for n in WEIGHTS if n not in BIG]
    sizes = {n: int(w[n].size) for n in small_names}
    rows_needed = sum(-(-sizes[n] // D_MODEL) for n in small_names)
    rows_pad = -(-rows_needed // 8) * 8

    def pack(d):
        parts = [_pad_rows(d[n], -(-sizes[n] // D_MODEL)) for n in small_names]
        parts.append(jnp.zeros((rows_pad - rows_needed, D_MODEL), f32))
        return jnp.concatenate(parts, axis=0)

    sd, sm, sv = _adamw(pack(w), pack(grads), pack(m), pack(v), "adamw_small")

    def unpack(a, n, o):
        r = -(-sizes[n] // D_MODEL)
        return a[o:o + r].reshape(-1)[:sizes[n]].reshape(w[n].shape), o + r

    o = 0
    for n in small_names:
        delta[n], _ = unpack(sd, n, o)
        new_m[n], _ = unpack(sm, n, o)
        new_v[n], o = unpack(sv, n, o)

    return (loss, grad_x[None], *[grads[n] for n in WEIGHTS], *[delta[n] for n in WEIGHTS],
            *[new_m[n] for n in WEIGHTS], *[new_v[n] for n in WEIGHTS])
```

```python
import jax
import jax.numpy as jnp
from jax import lax
from jax.experimental import pallas as pl
from jax.experimental.pallas import tpu as pltpu

f32 = jnp.float32
bf16 = jnp.bfloat16

D_MODEL = 1024
SSD_WIDTH = 1024
SSD_HEADS = 16
HEAD_DIM = 64
SSD_STATE = 128
XBC_WIDTH = 1536
SSD_CONV = 4
CHUNK = 128
CONF_WIDTH = 1024
CONF_KERNEL = 31
D_FF = 4096
PLE_DIM = 256
IN_WIDTH = 4624
EPS = 1e-6
N_CHIPS = 4
N_DEV = 8

ADAM_LR = 0.001
ADAM_B1 = 0.9
ADAM_B2 = 0.999
ADAM_EPS = 1e-08
ADAM_WD = 0.01
ADAM_STEP = 10

LANES = 128
VMEM_BIG = 56 * 1024 * 1024
VMEM_MID = 40 * 1024 * 1024

UP_OFF, DOWN_OFF, OUT_OFF, PG_OFF, PLE_OFF = 0, 1024, 2048, 2560, 2816
PLE_ROWS = 64
SLAB_A = PLE_OFF + PLE_ROWS
W_IN_SHARD = (1024, 1156)
CONVW_ROWS = 16
SMALL_ROWS = 56

MESH = pl.DeviceIdType.MESH
ANY = pl.BlockSpec(memory_space=pl.ANY)


def _cparams(sem=None, vmem=None):
    return pltpu.CompilerParams(dimension_semantics=sem, vmem_limit_bytes=vmem)


def _full(shape):
    n = len(shape)
    return pl.BlockSpec(shape, lambda *_: (0,) * n)


def _dot(a, b):
    return jnp.dot(a, b, preferred_element_type=f32)


def _dot_nt(a, b):
    return lax.dot_general(a, b, (((1,), (1,)), ((), ())), preferred_element_type=f32)


def _dot_tn(a, b):
    return lax.dot_general(a, b, (((0,), (0,)), ((), ())), preferred_element_type=f32)


def _sigmoid(x):
    return jax.nn.sigmoid(x)


def _rms(x, g):
    r = lax.rsqrt(jnp.mean(x * x, axis=-1, keepdims=True) + EPS)
    return x * r * g


def _rms_bwd(dy, x, g):
    r = lax.rsqrt(jnp.mean(x * x, axis=-1, keepdims=True) + EPS)
    xh = x * r
    dg = jnp.sum(dy * xh, axis=0, keepdims=True)
    dxh = dy * g
    dx = r * (dxh - xh * jnp.mean(dxh * xh, axis=-1, keepdims=True))
    return dx, dg


def _dsilu(x):
    s = _sigmoid(x)
    return s * (1.0 + x * (1.0 - s))


def _split3(x):
    hi = x.astype(bf16)
    r1 = x - hi.astype(f32)
    mid = r1.astype(bf16)
    lo = (r1 - mid.astype(f32)).astype(bf16)
    return hi, mid, lo


def _head_matrix():
    row = lax.broadcasted_iota(jnp.int32, (LANES, SSD_WIDTH), 0)
    col = lax.broadcasted_iota(jnp.int32, (LANES, SSD_WIDTH), 1)
    lo = row * HEAD_DIM
    return ((col >= lo) & (col < lo + HEAD_DIM)).astype(bf16)


def _expand(x, e):
    hi, mid, lo = _split3(x)
    return _dot(hi, e) + _dot(mid, e) + _dot(lo, e)


def _contract(x, e):
    hi, mid, lo = _split3(x)
    return _dot_nt(hi, e) + _dot_nt(mid, e) + _dot_nt(lo, e)


def _in_proj_fwd(x, g, wz, wxbc, wcv, wcg, wdt):
    T = x.shape[0]
    tm = min(256, T)

    def body(x_ref, g_ref, wz_ref, wx_ref, wcv_ref, wcg_ref, wdt_ref,
             u_ref, z_ref, xbc_ref, cv_ref, cg_ref, dt_ref, v_ref):
        ub = _rms(x_ref[...], g_ref[...]).astype(bf16)
        u_ref[...] = ub
        z_ref[...] = _dot(ub, wz_ref[...])
        xbc_ref[...] = _dot(ub, wx_ref[...])
        cv = _dot(ub, wcv_ref[...])
        cg = _dot(ub, wcg_ref[...])
        cv_ref[...] = cv
        cg_ref[...] = cg
        v_ref[...] = cv * _sigmoid(cg)
        dt_ref[...] = _dot(ub, wdt_ref[...])

    row = lambda n: pl.BlockSpec((tm, n), lambda i: (i, 0))
    return pl.pallas_call(
        body, name="in_proj_fwd", grid=(T // tm,),
        in_specs=[row(D_MODEL), _full((1, D_MODEL)), _full(wz.shape), _full(wxbc.shape), _full(wcv.shape),
                  _full(wcg.shape), _full(wdt.shape)],
        out_specs=[row(D_MODEL), row(SSD_WIDTH), row(XBC_WIDTH), row(CONF_WIDTH), row(CONF_WIDTH), row(LANES),
                   row(CONF_WIDTH)],
        out_shape=[jax.ShapeDtypeStruct((T, D_MODEL), bf16), jax.ShapeDtypeStruct((T, SSD_WIDTH), f32),
                   jax.ShapeDtypeStruct((T, XBC_WIDTH), f32), jax.ShapeDtypeStruct((T, CONF_WIDTH), f32),
                   jax.ShapeDtypeStruct((T, CONF_WIDTH), f32), jax.ShapeDtypeStruct((T, LANES), f32),
                   jax.ShapeDtypeStruct((T, CONF_WIDTH), f32)],
        compiler_params=_cparams(("parallel",), VMEM_BIG),
    )(x, g, wz, wxbc, wcv, wcg, wdt)


SUBLANES = 8


def _phases(offsets):
    return sorted({o % SUBLANES for o in offsets} - {0})


def _phase_shape(offsets, tm, C):
    a_max = max([o // SUBLANES for o in offsets if o % SUBLANES] or [0])
    return (max(len(_phases(offsets)), 1), tm + SUBLANES * a_max, C)


def _make_phases(buf_ref, ph_ref, offsets, tm):
    for idx, b in enumerate(_phases(offsets)):
        n = tm + SUBLANES * max(o // SUBLANES for o in offsets if o % SUBLANES == b)
        ph_ref[idx, 0:n, :] = buf_ref[pl.ds(b, n), :]


def _window(buf_ref, ph_ref, offsets, o, r0, rb):
    a, b = divmod(o, SUBLANES)
    if b == 0:
        return buf_ref[pl.ds(r0 + SUBLANES * a, rb), :]
    return ph_ref[_phases(offsets).index(b), pl.ds(r0 + SUBLANES * a, rb), :]


def _conv_rows(wb_ref, buf_ref, ph_ref, offsets, r0, rb):
    accs = [None] * (rb // SUBLANES)
    for k, o in enumerate(offsets):
        wk = wb_ref[pl.ds(SUBLANES * k, SUBLANES), :]
        for s in range(rb // SUBLANES):
            term = wk * _window(buf_ref, ph_ref, offsets, o, r0 + SUBLANES * s, SUBLANES)
            accs[s] = term if accs[s] is None else accs[s] + term
    return jnp.concatenate(accs, axis=0)


def _sublane_rows(w):
    return jnp.repeat(w, SUBLANES, axis=0)


def _fwd_offsets(K, hb):
    return [hb - (K - 1) + k for k in range(K)]


def _prev_halo_spec(hb, tm, C):
    return pl.BlockSpec((hb, C), lambda i: (jnp.maximum(i * (tm // hb) - 1, 0), 0))


CONV_RB = 16


def _ssd_conv_fwd(xbc, w, b):
    T, C = xbc.shape
    K, hb = SSD_CONV, 8
    tm = min(256, T)
    offs = _fwd_offsets(K, hb)

    def body(cur_ref, halo_ref, w_ref, b_ref, pre_ref, buf_ref, ph_ref):
        keep = jnp.where(pl.program_id(0) > 0, 1.0, 0.0)
        buf_ref[0:hb, :] = halo_ref[...] * keep
        buf_ref[hb:hb + tm, :] = cur_ref[...]
        _make_phases(buf_ref, ph_ref, offs, tm)

        def chunk(i, carry):
            r0 = pl.multiple_of(i * CONV_RB, CONV_RB)
            pre_ref[pl.ds(r0, CONV_RB), :] = _conv_rows(w_ref, buf_ref, ph_ref, offs, r0, CONV_RB) + b_ref[...]
            return carry

        lax.fori_loop(0, tm // CONV_RB, chunk, 0)

    return pl.pallas_call(
        body, name="ssd_conv_fwd", grid=(T // tm,),
        in_specs=[pl.BlockSpec((tm, C), lambda i: (i, 0)), _prev_halo_spec(hb, tm, C), _full((SUBLANES * K, C)),
                  _full((1, C))],
        out_specs=pl.BlockSpec((tm, C), lambda i: (i, 0)),
        out_shape=jax.ShapeDtypeStruct((T, C), f32),
        scratch_shapes=[pltpu.VMEM((hb + tm, C), f32), pltpu.VMEM(_phase_shape(offs, tm, C), f32)],
        compiler_params=_cparams(("parallel",), VMEM_MID),
    )(xbc, xbc, _sublane_rows(w), b)


def _conf_fwd(v, w, b, ln_g, ln_b):
    T, C = v.shape
    K, hb = CONF_KERNEL, 32
    tm = min(256, T)
    offs = _fwd_offsets(K, hb)

    def body(cur_ref, halo_ref, w_ref, b_ref, g_ref, bb_ref, co_ref, y_ref, buf_ref, ph_ref):
        keep = jnp.where(pl.program_id(0) > 0, 1.0, 0.0)
        buf_ref[0:hb, :] = halo_ref[...] * keep
        buf_ref[hb:hb + tm, :] = cur_ref[...]
        _make_phases(buf_ref, ph_ref, offs, tm)

        def chunk(i, carry):
            r0 = pl.multiple_of(i * CONV_RB, CONV_RB)
            co = _conv_rows(w_ref, buf_ref, ph_ref, offs, r0, CONV_RB) + b_ref[...]
            co_ref[pl.ds(r0, CONV_RB), :] = co
            mu = jnp.mean(co, axis=-1, keepdims=True)
            xc = co - mu
            yn = xc * lax.rsqrt(jnp.mean(xc * xc, axis=-1, keepdims=True) + EPS) * g_ref[...] + bb_ref[...]
            y_ref[pl.ds(r0, CONV_RB), :] = (yn * _sigmoid(yn)).astype(bf16)
            return carry

        lax.fori_loop(0, tm // CONV_RB, chunk, 0)

    return pl.pallas_call(
        body, name="conf_fwd", grid=(T // tm,),
        in_specs=[pl.BlockSpec((tm, C), lambda i: (i, 0)), _prev_halo_spec(hb, tm, C), _full((SUBLANES * K, C)),
                  _full((1, C)), _full((1, C)), _full((1, C))],
        out_specs=[pl.BlockSpec((tm, C), lambda i: (i, 0)), pl.BlockSpec((tm, C), lambda i: (i, 0))],
        out_shape=[jax.ShapeDtypeStruct((T, C), f32), jax.ShapeDtypeStruct((T, C), bf16)],
        scratch_shapes=[pltpu.VMEM((hb + tm, C), f32), pltpu.VMEM(_phase_shape(offs, tm, C), f32)],
        compiler_params=_cparams(("parallel",), VMEM_MID),
    )(v, v, _sublane_rows(w), b, ln_g, ln_b)


def _ssd_chunk_common(pre, dtr, dtb, alog, e):
    act = pre * _sigmoid(pre)
    xs = act[:, :SSD_WIDTH]
    bm = act[:, SSD_WIDTH:SSD_WIDTH + 2 * SSD_STATE]
    cm = act[:, SSD_WIDTH + 2 * SSD_STATE:]
    row = lax.broadcasted_iota(jnp.int32, (CHUNK, CHUNK), 0)
    col = lax.broadcasted_iota(jnp.int32, (CHUNK, CHUNK), 1)
    tri = row >= col
    dt = jax.nn.softplus(dtr + dtb)
    a_neg = -jnp.exp(alog)
    a = dt * a_neg
    cs = jnp.dot(tri.astype(f32), a, precision=lax.Precision.HIGHEST, preferred_element_type=f32)
    cs_e = _expand(cs, e)
    dt_e = _expand(dt, e)
    csl_e = cs_e[CHUNK - 1:CHUNK, :]
    ecs_e = jnp.exp(cs_e)
    dte_e = jnp.exp(csl_e - cs_e)
    cd_e = jnp.exp(csl_e)
    xc = xs * dt_e
    xd = xc * dte_e
    return dict(xs=xs, bm=bm, cm=cm, tri=tri, dt=dt, a_neg=a_neg, cs=cs, ecs_e=ecs_e, dte_e=dte_e, cd_e=cd_e,
                dt_e=dt_e, xc=xc, xd=xd)


def _group(v, g, width):
    return v[:, g * width:(g + 1) * width]


def _ssd_fwd(pre, dtr, z, dtb, alog, dskip_e, gn):
    T = pre.shape[0]
    nc = T // CHUNK
    GW = SSD_WIDTH // 2

    def body(pre_ref, dtr_ref, z_ref, dtb_ref, alog_ref, de_ref, gn_ref, y_ref, ys_ref, sp_ref, st_ref):
        @pl.when(pl.program_id(0) == 0)
        def _():
            st_ref[...] = jnp.zeros_like(st_ref)

        e = _head_matrix()
        q = _ssd_chunk_common(pre_ref[...], dtr_ref[...], dtb_ref[...], alog_ref[...], e)
        cs, tri, xc, xd = q["cs"], q["tri"], q["xc"], q["xd"]
        cs_t = cs.T
        st = st_ref[...]
        sp_ref[0] = st
        lane = lax.broadcasted_iota(jnp.int32, (1, LANES), 1)
        halves = (lane < HEAD_DIM, lane >= HEAD_DIM)

        g_mat, y_off, s_new = [], [], []
        for g in range(2):
            bg = _group(q["bm"], g, SSD_STATE)
            cg = _group(q["cm"], g, SSD_STATE)
            bgb, cgb = bg.astype(bf16), cg.astype(bf16)
            g_mat.append(_dot_nt(cgb, bgb))
            y_off.append(_dot(cgb, _group(st, g, GW).astype(bf16)))
            s_new.append(_dot(bg.T.astype(bf16), _group(xd, g, GW).astype(bf16)))
        y_off = jnp.concatenate(y_off, axis=1) * q["ecs_e"]
        st_ref[...] = st * q["cd_e"] + jnp.concatenate(s_new, axis=1)

        pairs = []
        for j in range(SSD_HEADS // 2):
            xp = xc[:, j * LANES:(j + 1) * LANES]
            acc = jnp.zeros((CHUNK, LANES), f32)
            for hh in range(2):
                h = 2 * j + hh
                seg = cs[:, h:h + 1] - cs_t[h:h + 1, :]
                lm = jnp.exp(jnp.where(tri, seg, -1e30))
                m = (g_mat[h // 8] * lm).astype(bf16)
                acc = acc + _dot(m, jnp.where(halves[hh], xp, 0.0).astype(bf16))
            pairs.append(acc)
        y = jnp.concatenate(pairs, axis=1) + y_off + q["xs"] * de_ref[...]
        y_ref[...] = y

        zz = z_ref[...]
        v = y * (zz * _sigmoid(zz))
        outs = []
        for g in range(2):
            vg = _group(v, g, GW)
            outs.append(vg * lax.rsqrt(jnp.mean(vg * vg, axis=-1, keepdims=True) + EPS))
        ys_ref[...] = (jnp.concatenate(outs, axis=1) * gn_ref[...]).astype(bf16)

    ch = lambda n: pl.BlockSpec((CHUNK, n), lambda c: (c, 0))
    return pl.pallas_call(
        body, name="ssd_fwd", grid=(nc,),
        in_specs=[ch(XBC_WIDTH), ch(LANES), ch(SSD_WIDTH), _full((1, LANES)), _full((1, LANES)), _full((1, SSD_WIDTH)),
                  _full((1, SSD_WIDTH))],
        out_specs=[ch(SSD_WIDTH), ch(SSD_WIDTH), pl.BlockSpec((1, SSD_STATE, SSD_WIDTH), lambda c: (c, 0, 0))],
        out_shape=[jax.ShapeDtypeStruct((T, SSD_WIDTH), f32), jax.ShapeDtypeStruct((T, SSD_WIDTH), bf16),
                   jax.ShapeDtypeStruct((nc, SSD_STATE, SSD_WIDTH), f32)],
        scratch_shapes=[pltpu.VMEM((SSD_STATE, SSD_WIDTH), f32)],
        compiler_params=_cparams(("arbitrary",), VMEM_MID),
    )(pre, dtr, z, dtb, alog, dskip_e, gn)


def _w_out_spec():
    n = 2 * SSD_WIDTH // N_CHIPS
    return pl.BlockSpec((N_CHIPS, n, D_MODEL), lambda *_: (0, OUT_OFF // n, 0))


def _out_proj_fwd(x, ys, yc, gath, g):
    T = x.shape[0]
    tm = min(512, T)
    n = 2 * SSD_WIDTH // N_CHIPS

    def body(x_ref, ys_ref, yc_ref, w_ref, g_ref, h_ref, u_ref):
        h = (x_ref[...] + _dot(ys_ref[:, 0:n], w_ref[0]) + _dot(ys_ref[:, n:], w_ref[1])
             + _dot(yc_ref[:, 0:n], w_ref[2]) + _dot(yc_ref[:, n:], w_ref[3]))
        h_ref[...] = h
        u_ref[...] = _rms(h, g_ref[...]).astype(bf16)

    row = pl.BlockSpec((tm, D_MODEL), lambda i: (i, 0))
    return pl.pallas_call(
        body, name="out_proj_fwd", grid=(T // tm,),
        in_specs=[row, row, row, _w_out_spec(), _full((1, D_MODEL))],
        out_specs=[row, row],
        out_shape=[jax.ShapeDtypeStruct((T, D_MODEL), f32), jax.ShapeDtypeStruct((T, D_MODEL), bf16)],
        compiler_params=_cparams(("parallel",), VMEM_MID),
    )(x, ys, yc, gath, g)


def _w_up_spec():
    return pl.BlockSpec((1, D_MODEL, D_MODEL), lambda i, b: (b, UP_OFF // D_MODEL, 0))


def _w_down_spec():
    return pl.BlockSpec((1, D_MODEL, D_MODEL), lambda i, b: (b, DOWN_OFF // D_MODEL, 0))


def _mlp_fwd(h1, u1, gath, g_next):
    T = h1.shape[0]
    tm = min(512, T)
    nb = D_FF // D_MODEL

    def body(h_ref, u_ref, wu_ref, wd_ref, g_ref, r_ref, h2_ref, u2_ref, acc_ref):
        b = pl.program_id(1)

        @pl.when(b == 0)
        def _():
            acc_ref[...] = jnp.zeros_like(acc_ref)

        r = jnp.maximum(_dot(u_ref[...], wu_ref[0]), 0.0)
        r_ref[...] = r.astype(bf16)
        acc_ref[...] += _dot((r * r).astype(bf16), wd_ref[0])

        @pl.when(b == nb - 1)
        def _():
            h2 = h_ref[...] + acc_ref[...]
            h2_ref[...] = h2
            u2_ref[...] = _rms(h2, g_ref[...]).astype(bf16)

    row = pl.BlockSpec((tm, D_MODEL), lambda i, b: (i, 0))
    return pl.pallas_call(
        body, name="mlp_fwd", grid=(T // tm, nb),
        in_specs=[row, row, _w_up_spec(), _w_down_spec(), _full((1, D_MODEL))],
        out_specs=[pl.BlockSpec((tm, D_MODEL), lambda i, b: (i, b)), row, row],
        out_shape=[jax.ShapeDtypeStruct((T, D_FF), bf16), jax.ShapeDtypeStruct((T, D_MODEL), f32),
                   jax.ShapeDtypeStruct((T, D_MODEL), bf16)],
        scratch_shapes=[pltpu.VMEM((tm, D_MODEL), f32)],
        compiler_params=_cparams(("parallel", "arbitrary"), VMEM_MID),
    )(h1, u1, gath, gath, g_next)


def _ple_loss(h2, u2, p, tgt, gath, b_pg, w_ple, g_ple, g_fin, g_pg):
    T = h2.shape[0]
    tm = min(256, T)
    npg = D_MODEL // N_CHIPS

    def body(h2_ref, u2_ref, p_ref, t_ref, wpg_ref, bpg_ref, wple_ref, gple_ref, gfin_ref, gpg_ref,
             loss_ref, dh2_ref, dgp_ref, dep_ref, dgfin_ref, dgple_ref, dbpg_ref, dgpg_ref):
        @pl.when(pl.program_id(0) == 0)
        def _():
            loss_ref[...] = jnp.zeros_like(loss_ref)
            dgfin_ref[...] = jnp.zeros_like(dgfin_ref)
            dgple_ref[...] = jnp.zeros_like(dgple_ref)
            dbpg_ref[...] = jnp.zeros_like(dbpg_ref)
            dgpg_ref[...] = jnp.zeros_like(dgpg_ref)

        h2 = h2_ref[...]
        gate_pre = bpg_ref[...]
        for b in range(N_CHIPS):
            gate_pre = gate_pre + _dot(u2_ref[:, b * npg:(b + 1) * npg], wpg_ref[b])
        gate = _sigmoid(gate_pre)
        e_pre = _dot(p_ref[...].astype(bf16), wple_ref[...])
        emb = _rms(e_pre, gple_ref[...])
        h3 = h2 + gate * emb
        diff = _rms(h3, gfin_ref[...]) - t_ref[...]
        sq = jnp.sum(jnp.sum(diff * diff, axis=1, keepdims=True), axis=0, keepdims=True)
        loss_ref[...] += (0.5 / D_MODEL) * sq
        dh3, dgfin = _rms_bwd(diff * (1.0 / D_MODEL), h3, gfin_ref[...])
        dgfin_ref[...] += dgfin
        dgp = dh3 * emb * gate * (1.0 - gate)
        dbpg_ref[...] += jnp.sum(dgp, axis=0, keepdims=True)
        dep, dgple = _rms_bwd(dh3 * gate, e_pre, gple_ref[...])
        dgple_ref[...] += dgple
        dgpb = dgp.astype(bf16)
        dgp_ref[...] = dgpb
        dep_ref[...] = dep.astype(bf16)
        du2 = jnp.concatenate([_dot_nt(dgpb, wpg_ref[b]) for b in range(N_CHIPS)], axis=1)
        dx, dgpg = _rms_bwd(du2, h2, gpg_ref[...])
        dgpg_ref[...] += dgpg
        dh2_ref[...] = dh3 + dx

    row = pl.BlockSpec((tm, D_MODEL), lambda i: (i, 0))
    vec = _full((1, D_MODEL))
    vshape = jax.ShapeDtypeStruct((1, D_MODEL), f32)
    return pl.pallas_call(
        body, name="ple_loss", grid=(T // tm,),
        in_specs=[row, row, pl.BlockSpec((tm, PLE_DIM), lambda i: (i, 0)), row,
                  pl.BlockSpec((N_CHIPS, npg, D_MODEL), lambda i: (0, PG_OFF // npg, 0)), vec, _full(w_ple.shape),
                  vec, vec, vec],
        out_specs=[_full((8, LANES)), row, row, row, vec, vec, vec, vec],
        out_shape=[jax.ShapeDtypeStruct((8, LANES), f32), jax.ShapeDtypeStruct((T, D_MODEL), f32),
                   jax.ShapeDtypeStruct((T, D_MODEL), bf16), jax.ShapeDtypeStruct((T, D_MODEL), bf16),
                   vshape, vshape, vshape, vshape],
        compiler_params=_cparams(("arbitrary",), VMEM_MID),
    )(h2, u2, p, tgt, gath, b_pg, w_ple, g_ple, g_fin, g_pg)


def _mlp_bwd(dh2, r, gath, h1, g):
    T = dh2.shape[0]
    tm = min(512, T)
    nb = D_FF // D_MODEL

    def body(dh2_ref, r_ref, wd_ref, wu_ref, h1_ref, g_ref, dhp_ref, dh1_ref, dg_ref, acc_ref):
        i, b = pl.program_id(0), pl.program_id(1)

        @pl.when(b == 0)
        def _():
            acc_ref[...] = jnp.zeros_like(acc_ref)

        @pl.when((b == 0) & (i == 0))
        def _():
            dg_ref[...] = jnp.zeros_like(dg_ref)

        dact = _dot_nt(dh2_ref[...].astype(bf16), wd_ref[0])
        dhp = (dact * 2.0 * r_ref[...].astype(f32)).astype(bf16)
        dhp_ref[...] = dhp
        acc_ref[...] += _dot_nt(dhp, wu_ref[0])

        @pl.when(b == nb - 1)
        def _():
            dx, dg = _rms_bwd(acc_ref[...], h1_ref[...], g_ref[...])
            dg_ref[...] += dg
            dh1_ref[...] = dh2_ref[...] + dx

    row = pl.BlockSpec((tm, D_MODEL), lambda i, b: (i, 0))
    return pl.pallas_call(
        body, name="mlp_bwd", grid=(T // tm, nb),
        in_specs=[row, pl.BlockSpec((tm, D_MODEL), lambda i, b: (i, b)), _w_down_spec(), _w_up_spec(), row,
                  _full((1, D_MODEL))],
        out_specs=[pl.BlockSpec((tm, D_MODEL), lambda i, b: (i, b)), row, _full((1, D_MODEL))],
        out_shape=[jax.ShapeDtypeStruct((T, D_FF), bf16), jax.ShapeDtypeStruct((T, D_MODEL), f32),
                   jax.ShapeDtypeStruct((1, D_MODEL), f32)],
        scratch_shapes=[pltpu.VMEM((tm, D_MODEL), f32)],
        compiler_params=_cparams(("arbitrary", "arbitrary"), VMEM_MID),
    )(dh2, r, gath, gath, h1, g)


def _out_proj_bwd(dh1, gath, co, ln_g, ln_b):
    T = dh1.shape[0]
    tm = min(512, T)

    def body(dh_ref, w_ref, co_ref, g_ref, b_ref, dys_ref, dco_ref, dg_ref, db_ref):
        @pl.when(pl.program_id(0) == 0)
        def _():
            dg_ref[...] = jnp.zeros_like(dg_ref)
            db_ref[...] = jnp.zeros_like(db_ref)

        dhb = dh_ref[...].astype(bf16)
        dys_ref[...] = jnp.concatenate([_dot_nt(dhb, w_ref[0]), _dot_nt(dhb, w_ref[1])], axis=1)
        dyc = jnp.concatenate([_dot_nt(dhb, w_ref[2]), _dot_nt(dhb, w_ref[3])], axis=1)
        co = co_ref[...]
        mu = jnp.mean(co, axis=-1, keepdims=True)
        xc = co - mu
        rstd = lax.rsqrt(jnp.mean(xc * xc, axis=-1, keepdims=True) + EPS)
        xh = xc * rstd
        yn = xh * g_ref[...] + b_ref[...]
        dyn = dyc * _dsilu(yn)
        dg_ref[...] += jnp.sum(dyn * xh, axis=0, keepdims=True)
        db_ref[...] += jnp.sum(dyn, axis=0, keepdims=True)
        dxh = dyn * g_ref[...]
        dco_ref[...] = rstd * (dxh - jnp.mean(dxh, axis=-1, keepdims=True)
                               - xh * jnp.mean(dxh * xh, axis=-1, keepdims=True))

    row = pl.BlockSpec((tm, D_MODEL), lambda i: (i, 0))
    vec = _full((1, CONF_WIDTH))
    vshape = jax.ShapeDtypeStruct((1, CONF_WIDTH), f32)
    return pl.pallas_call(
        body, name="out_proj_bwd", grid=(T // tm,),
        in_specs=[row, _w_out_spec(), row, vec, vec],
        out_specs=[row, row, vec, vec],
        out_shape=[jax.ShapeDtypeStruct((T, SSD_WIDTH), f32), jax.ShapeDtypeStruct((T, CONF_WIDTH), f32), vshape, vshape],
        compiler_params=_cparams(("arbitrary",), VMEM_MID),
    )(dh1, gath, co, ln_g, ln_b)


def _bwd_offsets(K):
    return [K - 1 - k for k in range(K)]


def _next_halo_spec(hb, tm, C, T):
    return pl.BlockSpec((hb, C), lambda i: (jnp.minimum((i + 1) * (tm // hb), T // hb - 1), 0))


DW_RB = 8
DW_ACC_VREGS = 32


def _conv_dw(dw_ref, bufd_ref, bufx_ref, phx_ref, offs_x, tm, C):
    K = len(offs_x)
    group = max(1, DW_ACC_VREGS // (C // LANES))
    for k0 in range(0, K, group):
        ks = list(range(k0, min(k0 + group, K)))

        def step(i, accs, ks=ks):
            r0 = pl.multiple_of(i * DW_RB, DW_RB)
            d = bufd_ref[pl.ds(r0, DW_RB), :]
            return tuple(acc + _window(bufx_ref, phx_ref, offs_x, offs_x[k], r0, DW_RB) * d for k, acc in zip(ks, accs))

        accs = lax.fori_loop(0, tm // DW_RB, step, tuple(jnp.zeros((DW_RB, C), f32) for _ in ks), unroll=4)
        for k, acc in zip(ks, accs):
            dw_ref[k:k + 1, :] += jnp.sum(acc, axis=0, keepdims=True)


def _fill_bwd_buffers(dcur_ref, dnext_ref, xcur_ref, xprev_ref, bufd_ref, bufx_ref, phd_ref, phx_ref, offs_d, offs_x,
                      hb, tm, first, last):
    bufd_ref[0:tm, :] = dcur_ref[...]
    bufd_ref[tm:tm + hb, :] = dnext_ref[...] * jnp.where(last, 0.0, 1.0)
    bufx_ref[0:hb, :] = xprev_ref[...] * jnp.where(first, 0.0, 1.0)
    bufx_ref[hb:hb + tm, :] = xcur_ref[...]
    _make_phases(bufd_ref, phd_ref, offs_d, tm)
    _make_phases(bufx_ref, phx_ref, offs_x, tm)


def _ssd_conv_bwd(dpre, xbc, w):
    T, C = xbc.shape
    K, hb = SSD_CONV, 8
    tm = min(256, T)
    nt = T // tm
    offs_d, offs_x = _bwd_offsets(K), _fwd_offsets(K, hb)

    def body(dcur_ref, dnext_ref, xcur_ref, xprev_ref, w_ref, dx_ref, dw_ref, db_ref, bufd_ref, bufx_ref, phd_ref, phx_ref):
        i = pl.program_id(0)

        @pl.when(i == 0)
        def _():
            dw_ref[...] = jnp.zeros_like(dw_ref)
            db_ref[...] = jnp.zeros_like(db_ref)

        _fill_bwd_buffers(dcur_ref, dnext_ref, xcur_ref, xprev_ref, bufd_ref, bufx_ref, phd_ref, phx_ref, offs_d, offs_x,
                          hb, tm, i == 0, i == nt - 1)

        def chunk(j, carry):
            r0 = pl.multiple_of(j * CONV_RB, CONV_RB)
            dx_ref[pl.ds(r0, CONV_RB), :] = _conv_rows(w_ref, bufd_ref, phd_ref, offs_d, r0, CONV_RB).astype(bf16)
            return carry

        lax.fori_loop(0, tm // CONV_RB, chunk, 0)
        _conv_dw(dw_ref, bufd_ref, bufx_ref, phx_ref, offs_x, tm, C)
        db_ref[...] += jnp.sum(dcur_ref[...], axis=0, keepdims=True)

    row = pl.BlockSpec((tm, C), lambda i: (i, 0))
    return pl.pallas_call(
        body, name="ssd_conv_bwd", grid=(nt,),
        in_specs=[row, _next_halo_spec(hb, tm, C, T), row, _prev_halo_spec(hb, tm, C), _full((SUBLANES * K, C))],
        out_specs=[row, _full((8, C)), _full((1, C))],
        out_shape=[jax.ShapeDtypeStruct((T, C), bf16), jax.ShapeDtypeStruct((8, C), f32), jax.ShapeDtypeStruct((1, C), f32)],
        scratch_shapes=[pltpu.VMEM((tm + hb, C), f32), pltpu.VMEM((hb + tm, C), f32),
                        pltpu.VMEM(_phase_shape(offs_d, tm, C), f32),
                        pltpu.VMEM(_phase_shape(offs_x, tm, C), f32)],
        compiler_params=_cparams(("arbitrary",), VMEM_BIG),
    )(dpre, dpre, xbc, xbc, _sublane_rows(w))


def _conf_conv_bwd(dco, v, w, cv, cg):
    T, C = v.shape
    K, hb = CONF_KERNEL, 32
    tm = min(256, T)
    nt = T // tm
    offs_d, offs_x = _bwd_offsets(K), _fwd_offsets(K, hb)

    def body(dcur_ref, dnext_ref, vcur_ref, vprev_ref, w_ref, cv_ref, cg_ref, dcv_ref, dcg_ref, dw_ref, db_ref,
             bufd_ref, bufx_ref, phd_ref, phx_ref):
        i = pl.program_id(0)

        @pl.when(i == 0)
        def _():
            dw_ref[...] = jnp.zeros_like(dw_ref)
            db_ref[...] = jnp.zeros_like(db_ref)

        _fill_bwd_buffers(dcur_ref, dnext_ref, vcur_ref, vprev_ref, bufd_ref, bufx_ref, phd_ref, phx_ref, offs_d, offs_x,
                          hb, tm, i == 0, i == nt - 1)

        def chunk(j, carry):
            r0 = pl.multiple_of(j * CONV_RB, CONV_RB)
            rows = pl.ds(r0, CONV_RB)
            dv = _conv_rows(w_ref, bufd_ref, phd_ref, offs_d, r0, CONV_RB)
            s = _sigmoid(cg_ref[rows, :])
            dcv_ref[rows, :] = (dv * s).astype(bf16)
            dcg_ref[rows, :] = (dv * cv_ref[rows, :] * s * (1.0 - s)).astype(bf16)
            return carry

        lax.fori_loop(0, tm // CONV_RB, chunk, 0)
        _conv_dw(dw_ref, bufd_ref, bufx_ref, phx_ref, offs_x, tm, C)
        db_ref[...] += jnp.sum(dcur_ref[...], axis=0, keepdims=True)

    row = pl.BlockSpec((tm, C), lambda i: (i, 0))
    return pl.pallas_call(
        body, name="conf_conv_bwd", grid=(nt,),
        in_specs=[row, _next_halo_spec(hb, tm, C, T), row, _prev_halo_spec(hb, tm, C), _full((SUBLANES * K, C)), row, row],
        out_specs=[row, row, _full((32, C)), _full((1, C))],
        out_shape=[jax.ShapeDtypeStruct((T, C), bf16), jax.ShapeDtypeStruct((T, C), bf16),
                   jax.ShapeDtypeStruct((32, C), f32), jax.ShapeDtypeStruct((1, C), f32)],
        scratch_shapes=[pltpu.VMEM((tm + hb, C), f32), pltpu.VMEM((hb + tm, C), f32),
                        pltpu.VMEM(_phase_shape(offs_d, tm, C), f32),
                        pltpu.VMEM(_phase_shape(offs_x, tm, C), f32)],
        compiler_params=_cparams(("arbitrary",), VMEM_BIG),
    )(dco, dco, v, v, _sublane_rows(w), cv, cg)


def _ssd_bwd(dys, y, z, pre, dtr, sprev, dtb, alog, dskip_e, gn):
    T = pre.shape[0]
    nc = T // CHUNK
    GW = SSD_WIDTH // 2

    def body(dys_ref, y_ref, z_ref, pre_ref, dtr_ref, sp_ref, dtb_ref, alog_ref, de_ref, gn_ref,
             dz_ref, dpre_ref, ddtr_ref, dgn_ref, dd_ref, dal_ref, ddtb_ref, ds_ref):
        @pl.when(pl.program_id(0) == 0)
        def _():
            ds_ref[...] = jnp.zeros_like(ds_ref)
            dgn_ref[...] = jnp.zeros_like(dgn_ref)
            dd_ref[...] = jnp.zeros_like(dd_ref)
            dal_ref[...] = jnp.zeros_like(dal_ref)
            ddtb_ref[...] = jnp.zeros_like(ddtb_ref)

        e = _head_matrix()
        pre = pre_ref[...]
        dtr_b = dtr_ref[...] + dtb_ref[...]
        q = _ssd_chunk_common(pre, dtr_ref[...], dtb_ref[...], alog_ref[...], e)
        cs, tri, xc, xd, xs, dt = q["cs"], q["tri"], q["xc"], q["xd"], q["xs"], q["dt"]
        cs_t = cs.T
        st = sp_ref[0]
        dsn = ds_ref[...]
        lane = lax.broadcasted_iota(jnp.int32, (1, LANES), 1)
        halves = (lane < HEAD_DIM, lane >= HEAD_DIM)
        row_i = lax.broadcasted_iota(jnp.int32, (CHUNK, CHUNK), 0)
        col_i = lax.broadcasted_iota(jnp.int32, (CHUNK, CHUNK), 1)
        tri_t = col_i >= row_i

        y = y_ref[...]
        zz = z_ref[...]
        sz = _sigmoid(zz)
        silu_z = zz * sz
        v = y * silu_z
        dout = dys_ref[...]
        gn_v = gn_ref[...]
        dv, vh = [], []
        for g in range(2):
            vg = _group(v, g, GW)
            rstd = lax.rsqrt(jnp.mean(vg * vg, axis=-1, keepdims=True) + EPS)
            vhg = vg * rstd
            dvh = _group(dout, g, GW) * _group(gn_v, g, GW)
            dv.append(rstd * (dvh - vhg * jnp.mean(dvh * vhg, axis=-1, keepdims=True)))
            vh.append(vhg)
        dv = jnp.concatenate(dv, axis=1)
        dgn_ref[...] += jnp.sum(dout * jnp.concatenate(vh, axis=1), axis=0, keepdims=True)
        dy = dv * silu_z
        dz_ref[...] = (dv * y * (sz * (1.0 + zz * (1.0 - sz)))).astype(bf16)

        dd_row = jnp.sum(dy * xs, axis=0, keepdims=True)
        dd_ref[...] += _contract(jnp.broadcast_to(dd_row, (8, SSD_WIDTH)), e)[0:1, :]
        dxs = dy * de_ref[...]

        dz_in = dy * q["ecs_e"]
        g_mat, gt_mat, dcm, dbm, dsp, dxd, y_off = [], [], [], [], [], [], []
        bgs, cgs = [], []
        for g in range(2):
            bg = _group(q["bm"], g, SSD_STATE)
            cg = _group(q["cm"], g, SSD_STATE)
            bgb, cgb = bg.astype(bf16), cg.astype(bf16)
            bgs.append(bgb)
            cgs.append(cgb)
            stg = _group(st, g, GW).astype(bf16)
            dsng = _group(dsn, g, GW).astype(bf16)
            dzg = _group(dz_in, g, GW).astype(bf16)
            g_mat.append(_dot_nt(cgb, bgb))
            gt_mat.append(_dot_nt(bgb, cgb))
            y_off.append(_dot(cgb, stg))
            dcm.append(_dot_nt(dzg, stg))
            dsp.append(_dot(cg.T.astype(bf16), dzg))
            dbm.append(_dot_nt(_group(xd, g, GW).astype(bf16), dsng))
            dxd.append(_dot(bgb, dsng))
        y_off = jnp.concatenate(y_off, axis=1) * q["ecs_e"]
        dxd = jnp.concatenate(dxd, axis=1)
        ds_ref[...] = dsn * q["cd_e"] + jnp.concatenate(dsp, axis=1)
        dcd_row = jnp.sum(dsn * st, axis=0, keepdims=True) * q["cd_e"]
        t_e = dxd * xd
        dcs = _contract(dy * y_off - t_e, e)
        last_row = _contract(jnp.broadcast_to(dcd_row + jnp.sum(t_e, axis=0, keepdims=True), (8, SSD_WIDTH)), e)[0:1, :]
        dxc_state = dxd * q["dte_e"]

        dg_acc = [jnp.zeros((CHUNK, CHUNK), f32), jnp.zeros((CHUNK, CHUNK), f32)]
        dgt_acc = [jnp.zeros((CHUNK, CHUNK), f32), jnp.zeros((CHUNK, CHUNK), f32)]
        dxc_pairs = []
        for j in range(SSD_HEADS // 2):
            dyp_f = dy[:, j * LANES:(j + 1) * LANES]
            xcp_f = xc[:, j * LANES:(j + 1) * LANES]
            acc = jnp.zeros((CHUNK, LANES), f32)
            for hh in range(2):
                h = 2 * j + hh
                g = h // 8
                dyp = jnp.where(halves[hh], dyp_f, 0.0).astype(bf16)
                xcp = jnp.where(halves[hh], xcp_f, 0.0).astype(bf16)
                lm = jnp.exp(jnp.where(tri, cs[:, h:h + 1] - cs_t[h:h + 1, :], -1e30))
                lm_t = jnp.exp(jnp.where(tri_t, cs_t[h:h + 1, :] - cs[:, h:h + 1], -1e30))
                dm = _dot_nt(dyp, xcp) * lm
                dm_t = _dot_nt(xcp, dyp) * lm_t
                acc = acc + _dot((gt_mat[g] * lm_t).astype(bf16), dyp)
                dg_acc[g] = dg_acc[g] + dm
                dgt_acc[g] = dgt_acc[g] + dm_t
                qd = jnp.sum(dm * g_mat[g] - dm_t * gt_mat[g], axis=1, keepdims=True)
                dcs = dcs + qd * (lane == h).astype(f32)
            dxc_pairs.append(acc)
        dxc = jnp.concatenate(dxc_pairs, axis=1) + dxc_state
        for g in range(2):
            dcm[g] = dcm[g] + _dot(dg_acc[g].astype(bf16), bgs[g])
            dbm[g] = dbm[g] + _dot(dgt_acc[g].astype(bf16), cgs[g])

        dxs = dxs + dxc * q["dt_e"]
        ddt = _contract(dxc * xs, e)
        dcs = dcs + jnp.where(row_i == CHUNK - 1, jnp.broadcast_to(last_row, (CHUNK, LANES)), 0.0)
        da = jnp.dot(tri_t.astype(f32), dcs, precision=lax.Precision.HIGHEST, preferred_element_type=f32)
        ddt = ddt + da * q["a_neg"]
        dal_ref[...] += jnp.sum(da * dt, axis=0, keepdims=True) * q["a_neg"]
        ddtr = ddt * _sigmoid(dtr_b) * (lane < SSD_HEADS).astype(f32)
        ddtb_ref[...] += jnp.sum(ddtr, axis=0, keepdims=True)
        ddtr_ref[...] = ddtr.astype(bf16)

        dact = jnp.concatenate([dxs, dbm[0], dbm[1], dcm[0], dcm[1]], axis=1)
        dpre_ref[...] = dact * _dsilu(pre)

    rev = lambda n: pl.BlockSpec((CHUNK, n), lambda c: (nc - 1 - c, 0))
    vec = _full((1, LANES))
    vshape = jax.ShapeDtypeStruct((1, LANES), f32)
    return pl.pallas_call(
        body, name="ssd_bwd", grid=(nc,),
        in_specs=[rev(SSD_WIDTH), rev(SSD_WIDTH), rev(SSD_WIDTH), rev(XBC_WIDTH), rev(LANES),
                  pl.BlockSpec((1, SSD_STATE, SSD_WIDTH), lambda c: (nc - 1 - c, 0, 0)),
                  vec, vec, _full((1, SSD_WIDTH)), _full((1, SSD_WIDTH))],
        out_specs=[rev(SSD_WIDTH), rev(XBC_WIDTH), rev(LANES), _full((1, SSD_WIDTH)), vec, vec, vec],
        out_shape=[jax.ShapeDtypeStruct((T, SSD_WIDTH), bf16), jax.ShapeDtypeStruct((T, XBC_WIDTH), f32),
                   jax.ShapeDtypeStruct((T, LANES), bf16), jax.ShapeDtypeStruct((1, SSD_WIDTH), f32),
                   vshape, vshape, vshape],
        scratch_shapes=[pltpu.VMEM((SSD_STATE, SSD_WIDTH), f32)],
        compiler_params=_cparams(("arbitrary",), VMEM_MID),
    )(dys, y, z, pre, dtr, sprev, dtb, alog, dskip_e, gn)


def _in_proj_bwd(dz, dxbc, dcv, dcg, ddt, wz, wxbc, wcv, wcg, wdt, x, dh1, g):
    T = x.shape[0]
    tm = min(256, T)

    def body(dz_ref, dx_ref, dcv_ref, dcg_ref, ddt_ref, wz_ref, wx_ref, wcv_ref, wcg_ref, wdt_ref, x_ref, dh_ref, g_ref,
             gx_ref, dg_ref):
        @pl.when(pl.program_id(0) == 0)
        def _():
            dg_ref[...] = jnp.zeros_like(dg_ref)

        du = (_dot_nt(dz_ref[...], wz_ref[...]) + _dot_nt(dx_ref[...], wx_ref[...]) + _dot_nt(dcv_ref[...], wcv_ref[...])
              + _dot_nt(dcg_ref[...], wcg_ref[...]) + _dot_nt(ddt_ref[...], wdt_ref[...]))
        dx, dg = _rms_bwd(du, x_ref[...], g_ref[...])
        dg_ref[...] += dg
        gx_ref[...] = dh_ref[...] + dx

    row = lambda n: pl.BlockSpec((tm, n), lambda i: (i, 0))
    return pl.pallas_call(
        body, name="in_proj_bwd", grid=(T // tm,),
        in_specs=[row(SSD_WIDTH), row(XBC_WIDTH), row(CONF_WIDTH), row(CONF_WIDTH), row(LANES), _full(wz.shape),
                  _full(wxbc.shape), _full(wcv.shape), _full(wcg.shape), _full(wdt.shape), row(D_MODEL), row(D_MODEL),
                  _full((1, D_MODEL))],
        out_specs=[row(D_MODEL), _full((1, D_MODEL))],
        out_shape=[jax.ShapeDtypeStruct((T, D_MODEL), f32), jax.ShapeDtypeStruct((1, D_MODEL), f32)],
        compiler_params=_cparams(("arbitrary",), VMEM_BIG),
    )(dz, dxbc, dcv, dcg, ddt, wz, wxbc, wcv, wcg, wdt, x, dh1, g)


def _weight_grad(a, g, name, square=False, slab=None, place=None, tk=512):
    T, K = a.shape
    N = g.shape[1]
    tk = min(tk, K)
    tn = min(512, N)
    tt = min(2048, T)

    def body(a_ref, g_ref, *rest):
        o_ref = rest[-1]
        acc = _dot_tn(_operand(a_ref[...]), g_ref[...].astype(bf16))
        t = pl.program_id(2)
        shaped = acc if slab is None else acc[None]

        @pl.when(t == 0)
        def _():
            o_ref[...] = shaped

        @pl.when(t > 0)
        def _():
            o_ref[...] += shaped

    def _operand(av):
        if square:
            av = av.astype(f32)
            av = av * av
        return av.astype(bf16)

    in_specs = [pl.BlockSpec((tt, tk), lambda i, j, t: (t, i)), pl.BlockSpec((tt, tn), lambda i, j, t: (t, j))]
    grid = (K // tk, N // tn, T // tt)
    params = _cparams(("parallel", "parallel", "arbitrary"), VMEM_MID)
    if slab is None:
        return pl.pallas_call(
            body, name=name, grid=grid, in_specs=in_specs,
            out_specs=pl.BlockSpec((tk, tn), lambda i, j, t: (i, j)),
            out_shape=jax.ShapeDtypeStruct((K, N), f32), compiler_params=params,
        )(a, g)
    return pl.pallas_call(
        body, name=name, grid=grid, in_specs=in_specs + [ANY],
        out_specs=pl.BlockSpec((1, tk, tn), lambda i, j, t: place(i, j)),
        out_shape=jax.ShapeDtypeStruct(slab.shape, f32), input_output_aliases={2: 0}, compiler_params=params,
    )(a, g, slab)


def _place():
    return lax.axis_index("x"), lax.axis_index("y"), lax.axis_index("c")


def _other_chips(x, y):
    return [(1 - x, y), (x, 1 - y), (1 - x, 1 - y)]


def _remote(src, dst, ssem, rsem, dev):
    return pltpu.make_async_remote_copy(src_ref=src, dst_ref=dst, send_sem=ssem, recv_sem=rsem, device_id=dev,
                                        device_id_type=MESH)


def _gather_weights(gath0, gin0, convw):
    halves = (gath0.shape[1] // 2, gin0.shape[1] // 2)

    def body(a_ref, n_ref, cw_ref, ao_ref, no_ref, cwo_ref, ssem, rsem, lsem):
        x, y, c = _place()
        me_b = 2 * x + y
        sib = (x, y, 1 - c)
        chips = _other_chips(x, y)
        loc = pltpu.make_async_copy(cw_ref, cwo_ref.at[me_b], lsem)
        loc.start()
        sends = []
        for j, (src, dst, h) in enumerate(((a_ref, ao_ref, halves[0]), (n_ref, no_ref, halves[1]))):
            mine = pl.ds(c * h, h)
            for k, (px, py) in enumerate(chips):
                s = 6 * j + k
                sends.append(_remote(src.at[me_b, mine], dst.at[me_b, mine], ssem.at[s], rsem.at[s], (px, py, c)))
        for k, (px, py) in enumerate(chips):
            sends.append(_remote(cw_ref, cwo_ref.at[me_b], ssem.at[12 + k], rsem.at[12 + k], (px, py, c)))
        for cp in sends:
            cp.start()
        for j, (src, dst, h) in enumerate(((a_ref, ao_ref, halves[0]), (n_ref, no_ref, halves[1]))):
            mine = pl.ds(c * h, h)
            for k, (px, py) in enumerate(chips):
                b = 2 * px + py
                s = 6 * j + k
                _remote(src.at[b, mine], dst.at[b, mine], ssem.at[s], rsem.at[s], (px, py, c)).wait_recv()
                fw = _remote(dst.at[b, mine], dst.at[b, mine], ssem.at[s + 3], rsem.at[s + 3], sib)
                fw.start()
                sends.append(fw)
        for k, (px, py) in enumerate(chips):
            b = 2 * px + py
            _remote(cw_ref, cwo_ref.at[b], ssem.at[12 + k], rsem.at[12 + k], (px, py, c)).wait_recv()
        for j, (src, dst, h) in enumerate(((a_ref, ao_ref, halves[0]), (n_ref, no_ref, halves[1]))):
            theirs = pl.ds((1 - c) * h, h)
            for k, (px, py) in enumerate(chips):
                b = 2 * px + py
                s = 6 * j + k + 3
                _remote(src.at[b, theirs], dst.at[b, theirs], ssem.at[s], rsem.at[s], sib).wait_recv()
        for cp in sends:
            cp.wait_send()
        loc.wait()

    return pl.pallas_call(
        body, name="gather_weights", in_specs=[ANY, ANY, ANY], out_specs=[ANY, ANY, ANY],
        out_shape=[jax.ShapeDtypeStruct(gath0.shape, bf16), jax.ShapeDtypeStruct(gin0.shape, bf16),
                   jax.ShapeDtypeStruct((N_CHIPS, CONVW_ROWS, D_MODEL), f32)],
        input_output_aliases={0: 0, 1: 1},
        scratch_shapes=[pltpu.SemaphoreType.DMA((15,)), pltpu.SemaphoreType.DMA((15,)), pltpu.SemaphoreType.DMA(())],
    )(gath0, gin0, convw)


def _swap_halves(ga, gb):
    ha, hb = ga.shape[1] // 2, gb.shape[1] // 2

    def body(a_ref, b_ref, ra_ref, rb_ref, ssem, rsem):
        x, y, c = _place()
        sib = (x, y, 1 - c)
        ca = _remote(a_ref.at[:, pl.ds((1 - c) * ha, ha), :], ra_ref, ssem.at[0], rsem.at[0], sib)
        cb = _remote(b_ref.at[:, pl.ds((1 - c) * hb, hb), :], rb_ref, ssem.at[1], rsem.at[1], sib)
        ca.start()
        cb.start()
        ca.wait()
        cb.wait()

    return pl.pallas_call(
        body, name="swap_halves", in_specs=[ANY, ANY], out_specs=[ANY, ANY],
        out_shape=[jax.ShapeDtypeStruct((N_CHIPS, ha, ga.shape[2]), f32),
                   jax.ShapeDtypeStruct((N_CHIPS, hb, gb.shape[2]), f32)],
        scratch_shapes=[pltpu.SemaphoreType.DMA((2,)), pltpu.SemaphoreType.DMA((2,))],
    )(ga, gb)


def _chip_sum(cidx, gslab, recv, name):
    half, C = recv.shape[1:]
    tr = half // 2

    def body(c_ref, g_ref, r_ref, o_ref):
        o_ref[...] = (g_ref[...] + r_ref[...]).astype(bf16)

    return pl.pallas_call(
        body, name=name,
        grid_spec=pltpu.PrefetchScalarGridSpec(
            num_scalar_prefetch=1, grid=(N_CHIPS, half // tr),
            in_specs=[pl.BlockSpec((1, tr, C), lambda b, i, c_ref: (b, c_ref[0] * (half // tr) + i, 0)),
                      pl.BlockSpec((1, tr, C), lambda b, i, c_ref: (b, i, 0))],
            out_specs=pl.BlockSpec((1, tr, C), lambda b, i, c_ref: (b, i, 0))),
        out_shape=jax.ShapeDtypeStruct((N_CHIPS, half, C), bf16),
        compiler_params=_cparams(("parallel", "parallel"), VMEM_MID),
    )(cidx, gslab, recv)


def _exchange(ha, hb, small):
    def body(ha_ref, hb_ref, sm_ref, ra_ref, rb_ref, all_ref, ssem, rsem, lsem):
        x, y, c = _place()
        me = 4 * x + 2 * y + c
        chips = _other_chips(x, y)
        loc = pltpu.make_async_copy(sm_ref, all_ref.at[me], lsem)
        loc.start()
        sends = []
        for k, (px, py) in enumerate(chips):
            sends.append(_remote(ha_ref.at[2 * px + py], ra_ref.at[k], ssem.at[k], rsem.at[k], (px, py, c)))
            sends.append(_remote(hb_ref.at[2 * px + py], rb_ref.at[k], ssem.at[3 + k], rsem.at[3 + k], (px, py, c)))
        peers = []
        for r in range(1, N_DEV):
            peer = ((1 - x) if r & 4 else x, (1 - y) if r & 2 else y, (1 - c) if r & 1 else c)
            peers.append(peer)
            sends.append(_remote(sm_ref, all_ref.at[me], ssem.at[5 + r], rsem.at[5 + r], peer))
        for cp in sends:
            cp.start()
        for k, (px, py) in enumerate(chips):
            _remote(ha_ref.at[0], ra_ref.at[k], ssem.at[k], rsem.at[k], (px, py, c)).wait_recv()
            _remote(hb_ref.at[0], rb_ref.at[k], ssem.at[3 + k], rsem.at[3 + k], (px, py, c)).wait_recv()
        for r, peer in zip(range(1, N_DEV), peers):
            pid = 4 * peer[0] + 2 * peer[1] + peer[2]
            _remote(sm_ref, all_ref.at[pid], ssem.at[5 + r], rsem.at[5 + r], peer).wait_recv()
        for cp in sends:
            cp.wait_send()
        loc.wait()

    return pl.pallas_call(
        body, name="exchange", in_specs=[ANY, ANY, ANY], out_specs=[ANY, ANY, ANY],
        out_shape=[jax.ShapeDtypeStruct((3,) + ha.shape[1:], bf16), jax.ShapeDtypeStruct((3,) + hb.shape[1:], bf16),
                   jax.ShapeDtypeStruct((N_DEV, SMALL_ROWS, D_MODEL), f32)],
        scratch_shapes=[pltpu.SemaphoreType.DMA((13,)), pltpu.SemaphoreType.DMA((13,)), pltpu.SemaphoreType.DMA(())],
    )(ha, hb, small)


def _final_sum(idx, gslab, recv_sib, recv_ici, name):
    half, C = recv_sib.shape[1:]
    tr = half // 2

    def body(i_ref, g_ref, r_ref, p_ref, o_ref):
        acc = g_ref[0] + r_ref[0]
        for k in range(3):
            acc = acc + p_ref[k].astype(f32)
        o_ref[...] = acc

    return pl.pallas_call(
        body, name=name,
        grid_spec=pltpu.PrefetchScalarGridSpec(
            num_scalar_prefetch=1, grid=(half // tr,),
            in_specs=[pl.BlockSpec((1, tr, C), lambda i, s: (s[1], s[0] * (half // tr) + i, 0)),
                      pl.BlockSpec((1, tr, C), lambda i, s: (s[1], i, 0)),
                      pl.BlockSpec((3, tr, C), lambda i, s: (0, i, 0))],
            out_specs=pl.BlockSpec((tr, C), lambda i, s: (s[0] * (half // tr) + i, 0))),
        out_shape=jax.ShapeDtypeStruct((2 * half, C), f32),
        compiler_params=_cparams(("parallel",), VMEM_MID),
    )(idx, gslab, recv_sib, recv_ici)


def _join_halves(ra, rb):
    ha, hb = ra.shape[0] // 2, rb.shape[0] // 2

    def body(a_ref, b_ref, ao_ref, bo_ref, ssem, rsem):
        x, y, c = _place()
        sib = (x, y, 1 - c)
        mine_a, theirs_a = pl.ds(c * ha, ha), pl.ds((1 - c) * ha, ha)
        mine_b, theirs_b = pl.ds(c * hb, hb), pl.ds((1 - c) * hb, hb)
        ca = _remote(a_ref.at[mine_a], ao_ref.at[mine_a], ssem.at[0], rsem.at[0], sib)
        cb = _remote(b_ref.at[mine_b], bo_ref.at[mine_b], ssem.at[1], rsem.at[1], sib)
        ca.start()
        cb.start()
        _remote(a_ref.at[theirs_a], ao_ref.at[theirs_a], ssem.at[0], rsem.at[0], sib).wait_recv()
        _remote(b_ref.at[theirs_b], bo_ref.at[theirs_b], ssem.at[1], rsem.at[1], sib).wait_recv()
        ca.wait_send()
        cb.wait_send()

    return pl.pallas_call(
        body, name="join_halves", in_specs=[ANY, ANY], out_specs=[ANY, ANY],
        out_shape=[jax.ShapeDtypeStruct(ra.shape, f32), jax.ShapeDtypeStruct(rb.shape, f32)],
        input_output_aliases={0: 0, 1: 1},
        scratch_shapes=[pltpu.SemaphoreType.DMA((2,)), pltpu.SemaphoreType.DMA((2,))],
    )(ra, rb)


def _sum_small(all_small):
    def body(a_ref, o_ref):
        acc = a_ref[0]
        for d in range(1, N_DEV):
            acc = acc + a_ref[d]
        o_ref[...] = acc

    return pl.pallas_call(
        body, name="sum_small", out_shape=jax.ShapeDtypeStruct((SMALL_ROWS, D_MODEL), f32),
    )(all_small)


def _adamw(w, g, m, v, name, g_off=0):
    R, C = w.shape
    tr = 256 if R % 256 == 0 else R
    assert g_off % tr == 0
    c1 = 1.0 - ADAM_B1 ** ADAM_STEP
    c2 = 1.0 - ADAM_B2 ** ADAM_STEP

    def body(w_ref, g_ref, m_ref, v_ref, d_ref, mo_ref, vo_ref):
        gg = g_ref[...]
        m2 = ADAM_B1 * m_ref[...] + (1.0 - ADAM_B1) * gg
        v2 = ADAM_B2 * v_ref[...] + (1.0 - ADAM_B2) * (gg * gg)
        mo_ref[...] = m2
        vo_ref[...] = v2
        d_ref[...] = -ADAM_LR * ((m2 / c1) / (jnp.sqrt(v2 / c2) + ADAM_EPS) + ADAM_WD * w_ref[...])

    blk = pl.BlockSpec((tr, C), lambda i: (i, 0))
    gblk = pl.BlockSpec((tr, C), lambda i: (g_off // tr + i, 0))
    shp = jax.ShapeDtypeStruct((R, C), f32)
    return pl.pallas_call(
        body, name=name, grid=(R // tr,), in_specs=[blk, gblk, blk, blk], out_specs=[blk] * 3, out_shape=[shp] * 3,
        compiler_params=_cparams(("parallel",), VMEM_MID),
    )(w, g, m, v)


def _pad_lanes(v):
    return jnp.pad(v, ((0, 0), (0, LANES - v.shape[1])))


def _local_step(x, p, tgt, gath, W, S):
    wz, wxbc, wcv, wcg, wdt = W["wz"], W["wxbc"], W["wcv"], W["wcg"], W["wdt"]
    dtb = _pad_lanes(S["dt_bias"])
    alog = _pad_lanes(S["A_log"])
    dskip_e = jnp.repeat(S["D_skip"], HEAD_DIM, axis=1)

    u0, z, xbc, cv, cg, dtr, v = _in_proj_fwd(x, S["mix_norm_g"], wz, wxbc, wcv, wcg, wdt)
    pre = _ssd_conv_fwd(xbc, S["ssd_conv_w"], S["ssd_conv_b"])
    y, ys, sprev = _ssd_fwd(pre, dtr, z, dtb, alog, dskip_e, S["ssd_norm_g"])
    co, yc = _conf_fwd(v, S["conf_dw_w"], S["conf_dw_b"], S["conf_ln_g"], S["conf_ln_b"])
    h1, u1 = _out_proj_fwd(x, ys, yc, gath, S["mlp_norm_g"])
    r, h2, u2 = _mlp_fwd(h1, u1, gath, S["ple_gate_norm_g"])
    loss, dh2, dgp, dep, dg_fin, dg_ple, db_pg, dg_pg = _ple_loss(
        h2, u2, p, tgt, gath, S["b_ple_gate"], W["w_ple"], S["ple_norm_g"], S["final_norm_g"], S["ple_gate_norm_g"])

    npg = D_MODEL // N_CHIPS
    ga = lax.empty((N_CHIPS, SLAB_A, D_MODEL), f32)
    ga = _weight_grad(u2, dgp, "dw_ple_gate", slab=ga, tk=npg, place=lambda i, j: (i, PG_OFF // npg, j))
    ga = _weight_grad(r, dh2, "dw_down", square=True, slab=ga, place=lambda i, j: (i // 2, DOWN_OFF // 512 + i % 2, j))
    gw_ple = _weight_grad(p, dep, "dw_ple")
    dhp, dh1, dg_mlp = _mlp_bwd(dh2, r, gath, h1, S["mlp_norm_g"])
    ga = _weight_grad(u1, dhp, "dw_up", slab=ga, place=lambda i, j: (j // 2, UP_OFF // 512 + i, j % 2))
    ga = _weight_grad(ys, dh1, "dw_out_ssd", slab=ga, place=lambda i, j: (i, OUT_OFF // 512, j))
    ga = _weight_grad(yc, dh1, "dw_out_conf", slab=ga, place=lambda i, j: (2 + i, OUT_OFF // 512, j))
    dys, dco, dg_ln, db_ln = _out_proj_bwd(dh1, gath, co, S["conf_ln_g"], S["conf_ln_b"])
    dcv, dcg, dw_conf, db_conf = _conf_conv_bwd(dco, v, S["conf_dw_w"], cv, cg)
    dz, dpre, ddtr, dg_ssdn, dd, dal, ddtb = _ssd_bwd(dys, y, z, pre, dtr, sprev, dtb, alog, dskip_e, S["ssd_norm_g"])
    dxbc, dw_sconv, db_sconv = _ssd_conv_bwd(dpre, xbc, S["ssd_conv_w"])
    gx, dg_mix = _in_proj_bwd(dz, dxbc, dcv, dcg, ddtr, wz, wxbc, wcv, wcg, wdt, x, dh1, S["mix_norm_g"])

    gw_in = jnp.concatenate([
        _weight_grad(u0, dz, "dw_in_z"), _weight_grad(u0, dxbc, "dw_in_xbc"),
        _weight_grad(u0, ddtr, "dw_in_dt")[:, :SSD_HEADS],
        _weight_grad(u0, dcv, "dw_in_cv"), _weight_grad(u0, dcg, "dw_in_cg")], axis=1)
    small = {
        "mix_norm_g": dg_mix, "ssd_conv_w": dw_sconv[:SSD_CONV], "ssd_conv_b": db_sconv,
        "dt_bias": ddtb[:, :SSD_HEADS], "A_log": dal[:, :SSD_HEADS], "D_skip": dd[:, :SSD_HEADS],
        "ssd_norm_g": dg_ssdn, "conf_dw_w": dw_conf[:CONF_KERNEL], "conf_dw_b": db_conf,
        "conf_ln_g": dg_ln, "conf_ln_b": db_ln, "mlp_norm_g": dg_mlp, "ple_gate_norm_g": dg_pg,
        "b_ple_gate": db_pg, "ple_norm_g": dg_ple, "final_norm_g": dg_fin,
    }
    return loss, gx, ga, gw_in, gw_ple, small


def _rows(a):
    return a.reshape(-1, D_MODEL)


def _pad_rows(a, n):
    flat = a.reshape(-1)
    return jnp.pad(flat, (0, n * D_MODEL - flat.shape[0])).reshape(n, D_MODEL)


def _ple_of_slab(slab):
    return slab[PLE_OFF:PLE_OFF + PLE_ROWS].reshape(PLE_DIM, D_MODEL // N_CHIPS)


SMALL_LAYOUT = (("mix_norm_g", 1), ("ssd_norm_g", 1), ("conf_dw_b", 1), ("conf_ln_g", 1), ("conf_ln_b", 1),
                ("mlp_norm_g", 1), ("ple_gate_norm_g", 1), ("b_ple_gate", 1), ("ple_norm_g", 1), ("final_norm_g", 1),
                ("ssd_conv_b", 2), ("dt_bias", 1), ("A_log", 1), ("D_skip", 1), ("loss", 1),
                ("ssd_conv_w", 6), ("conf_dw_w", 31))


def _pack_small(d):
    parts = [_pad_rows(d[n], r) for n, r in SMALL_LAYOUT]
    used = sum(r for _, r in SMALL_LAYOUT)
    parts.append(jnp.zeros((SMALL_ROWS - used, D_MODEL), f32))
    return jnp.concatenate(parts, axis=0)


def _unpack_small(a, shapes):
    out, o = {}, 0
    for n, r in SMALL_LAYOUT:
        shp = shapes[n]
        size = 1
        for s in shp:
            size *= s
        out[n] = a[o:o + r].reshape(-1)[:size].reshape(shp)
        o += r
    return out


BIG = ("w_in", "w_out", "w_up", "w_down", "w_ple_gate", "w_ple")
BIG_A = (("w_up", UP_OFF), ("w_down", DOWN_OFF), ("w_out", OUT_OFF), ("w_ple_gate", PG_OFF))
WEIGHTS = ("mix_norm_g", "w_in", "ssd_conv_w", "ssd_conv_b", "dt_bias", "A_log", "D_skip", "ssd_norm_g", "conf_dw_w",
           "conf_dw_b", "conf_ln_g", "conf_ln_b", "w_out", "mlp_norm_g", "w_up", "w_down", "ple_gate_norm_g",
           "w_ple_gate", "b_ple_gate", "w_ple", "ple_norm_g", "final_norm_g")


def kernel(x, p, mix_norm_g, w_in, ssd_conv_w, ssd_conv_b, dt_bias, A_log, D_skip, ssd_norm_g, conf_dw_w, conf_dw_b, conf_ln_g, conf_ln_b, w_out, mlp_norm_g, w_up, w_down, ple_gate_norm_g, w_ple_gate, b_ple_gate, w_ple, ple_norm_g, final_norm_g, loss_target, m_mix_norm_g, m_w_in, m_ssd_conv_w, m_ssd_conv_b, m_dt_bias, m_A_log, m_D_skip, m_ssd_norm_g, m_conf_dw_w, m_conf_dw_b, m_conf_ln_g, m_conf_ln_b, m_w_out, m_mlp_norm_g, m_w_up, m_w_down, m_ple_gate_norm_g, m_w_ple_gate, m_b_ple_gate, m_w_ple, m_ple_norm_g, m_final_norm_g, v_mix_norm_g, v_w_in, v_ssd_conv_w, v_ssd_conv_b, v_dt_bias, v_A_log, v_D_skip, v_ssd_norm_g, v_conf_dw_w, v_conf_dw_b, v_conf_ln_g, v_conf_ln_b, v_w_out, v_mlp_norm_g, v_w_up, v_w_down, v_ple_gate_norm_g, v_w_ple_gate, v_b_ple_gate, v_w_ple, v_ple_norm_g, v_final_norm_g):
    w = dict(mix_norm_g=mix_norm_g, w_in=w_in, ssd_conv_w=ssd_conv_w, ssd_conv_b=ssd_conv_b, dt_bias=dt_bias, A_log=A_log,
             D_skip=D_skip, ssd_norm_g=ssd_norm_g, conf_dw_w=conf_dw_w, conf_dw_b=conf_dw_b, conf_ln_g=conf_ln_g,
             conf_ln_b=conf_ln_b, w_out=w_out, mlp_norm_g=mlp_norm_g, w_up=w_up, w_down=w_down,
             ple_gate_norm_g=ple_gate_norm_g, w_ple_gate=w_ple_gate, b_ple_gate=b_ple_gate, w_ple=w_ple,
             ple_norm_g=ple_norm_g, final_norm_g=final_norm_g)
    m = dict(mix_norm_g=m_mix_norm_g, w_in=m_w_in, ssd_conv_w=m_ssd_conv_w, ssd_conv_b=m_ssd_conv_b, dt_bias=m_dt_bias,
             A_log=m_A_log, D_skip=m_D_skip, ssd_norm_g=m_ssd_norm_g, conf_dw_w=m_conf_dw_w, conf_dw_b=m_conf_dw_b,
             conf_ln_g=m_conf_ln_g, conf_ln_b=m_conf_ln_b, w_out=m_w_out, mlp_norm_g=m_mlp_norm_g, w_up=m_w_up,
             w_down=m_w_down, ple_gate_norm_g=m_ple_gate_norm_g, w_ple_gate=m_w_ple_gate, b_ple_gate=m_b_ple_gate,
             w_ple=m_w_ple, ple_norm_g=m_ple_norm_g, final_norm_g=m_final_norm_g)
    v = dict(mix_norm_g=v_mix_norm_g, w_in=v_w_in, ssd_conv_w=v_ssd_conv_w, ssd_conv_b=v_ssd_conv_b, dt_bias=v_dt_bias,
             A_log=v_A_log, D_skip=v_D_skip, ssd_norm_g=v_ssd_norm_g, conf_dw_w=v_conf_dw_w, conf_dw_b=v_conf_dw_b,
             conf_ln_g=v_conf_ln_g, conf_ln_b=v_conf_ln_b, w_out=v_w_out, mlp_norm_g=v_mlp_norm_g, w_up=v_w_up,
             w_down=v_w_down, ple_gate_norm_g=v_ple_gate_norm_g, w_ple_gate=v_w_ple_gate, b_ple_gate=v_b_ple_gate,
             w_ple=v_w_ple, ple_norm_g=v_ple_norm_g, final_norm_g=v_final_norm_g)
    xi, yi, ci = lax.axis_index("x"), lax.axis_index("y"), lax.axis_index("c")
    chip = 2 * xi + yi

    slab = jnp.concatenate([w_up[0], w_down[0], w_out[0], w_ple_gate[0], _rows(w_ple[0])], axis=0).astype(bf16)
    gath0 = lax.dynamic_update_slice(jnp.zeros((N_CHIPS, SLAB_A, D_MODEL), bf16), slab[None], (chip, 0, 0))
    gin0 = lax.dynamic_update_slice(jnp.zeros((N_CHIPS,) + W_IN_SHARD, bf16), w_in.astype(bf16), (chip, 0, 0))
    convw = _pad_rows(jnp.concatenate([ssd_conv_w[0].reshape(-1), conf_dw_w[0].reshape(-1)]), CONVW_ROWS)
    gath, gin, cwg = _gather_weights(gath0, gin0, convw)
    w_in_full = jnp.concatenate([gin[b] for b in range(N_CHIPS)], axis=1)
    o_dt = SSD_WIDTH + XBC_WIDTH
    o_cv = o_dt + SSD_HEADS
    W = {
        "wz": w_in_full[:, :SSD_WIDTH], "wxbc": w_in_full[:, SSD_WIDTH:o_dt],
        "wdt": jnp.pad(w_in_full[:, o_dt:o_cv], ((0, 0), (0, LANES - SSD_HEADS))),
        "wcv": w_in_full[:, o_cv:o_cv + CONF_WIDTH], "wcg": w_in_full[:, o_cv + CONF_WIDTH:],
        "w_ple": jnp.concatenate([_ple_of_slab(gath[b]) for b in range(N_CHIPS)], axis=1),
    }
    n_sc = SSD_CONV * (XBC_WIDTH // N_CHIPS)
    n_cf = CONF_KERNEL * (CONF_WIDTH // N_CHIPS)
    S = {n: w[n][0] for n in ("mix_norm_g", "ssd_conv_b", "dt_bias", "A_log", "D_skip", "ssd_norm_g", "conf_dw_b",
                              "conf_ln_g", "conf_ln_b", "mlp_norm_g", "ple_gate_norm_g", "b_ple_gate", "ple_norm_g")}
    S = {n: a.reshape(1, -1) for n, a in S.items()}
    S["final_norm_g"] = final_norm_g.reshape(1, -1)
    S["ssd_conv_w"] = jnp.concatenate(
        [cwg[b].reshape(-1)[:n_sc].reshape(SSD_CONV, XBC_WIDTH // N_CHIPS) for b in range(N_CHIPS)], axis=1)
    S["conf_dw_w"] = jnp.concatenate(
        [cwg[b].reshape(-1)[n_sc:n_sc + n_cf].reshape(CONF_KERNEL, CONF_WIDTH // N_CHIPS) for b in range(N_CHIPS)], axis=1)

    loss8, grad_x, ga, gw_in, gw_ple, gsmall = _local_step(x[0], p[0, 0], loss_target[0], gath, W, S)

    n_in, n_ple = IN_WIDTH // N_CHIPS, D_MODEL // N_CHIPS
    gb = jnp.stack([gw_in[:, b * n_in:(b + 1) * n_in] for b in range(N_CHIPS)], axis=0)
    ple_rows = jnp.stack([_rows(gw_ple[:, b * n_ple:(b + 1) * n_ple]) for b in range(N_CHIPS)], axis=0)
    ga = lax.dynamic_update_slice(ga, ple_rows, (0, PLE_OFF, 0))
    gsmall = dict(gsmall)
    gsmall["loss"] = loss8[0:1, 0:1]
    small = _pack_small(gsmall)

    recv_a, recv_b = _swap_halves(ga, gb)
    cidx = jnp.stack([ci, chip]).astype(jnp.int32)
    ha = _chip_sum(cidx, ga, recv_a, "chip_sum_a")
    hb = _chip_sum(cidx, gb, recv_b, "chip_sum_b")
    ici_a, ici_b, all_small = _exchange(ha, hb, small)
    ra = _final_sum(cidx, ga, recv_a, ici_a, "final_sum_a")
    rb = _final_sum(cidx, gb, recv_b, ici_b, "final_sum_b")
    ra, rb = _join_halves(ra, rb)
    tot_small = _sum_small(all_small)

    shapes = {n: (tuple(w[n].shape[1:]) if n != "final_norm_g" else (D_MODEL,)) for n in WEIGHTS if n not in BIG}
    shapes["ssd_conv_w"] = (SSD_CONV, XBC_WIDTH)
    shapes["conf_dw_w"] = (CONF_KERNEL, CONF_WIDTH)
    shapes["loss"] = (1,)
    tot = _unpack_small(tot_small, shapes)
    loss = tot["loss"].reshape(())
    n1, n2 = XBC_WIDTH // N_CHIPS, CONF_WIDTH // N_CHIPS
    tot["ssd_conv_w"] = lax.dynamic_slice(tot["ssd_conv_w"], (0, chip * n1), (SSD_CONV, n1))
    tot["conf_dw_w"] = lax.dynamic_slice(tot["conf_dw_w"], (0, chip * n2), (CONF_KERNEL, n2))

    grads = {"w_ple": _ple_of_slab(ra), "w_in": rb}
    for n, off in BIG_A:
        grads[n] = ra[off:off + w[n].shape[1]]
    for n in WEIGHTS:
        if n not in BIG:
            grads[n] = tot[n]
    grads = {n: g.reshape(w[n].shape) for n, g in grads.items()}

    delta, new_m, new_v = {}, {}, {}
    for n, off in BIG_A:
        d_, m_, v_ = _adamw(w[n][0], ra, m[n][0], v[n][0], "adamw_" + n, g_off=off)
        delta[n], new_m[n], new_v[n] = d_[None], m_[None], v_[None]
    for n in ("w_in", "w_ple"):
        d_, m_, v_ = _adamw(w[n][0], grads[n][0], m[n][0], v[n][0], "adamw_" + n)
        delta[n], new_m[n], new_v[n] = d_[None], m_[None], v_[None]
    small_names = [n for n in WEIGHTS if n not in BIG]
    sizes = {n: int(w[n].size) for n in small_names}
    rows_needed = sum(-(-sizes[n] // D_MODEL) for n in small_names)
    rows_pad = -(-rows_needed // 8) * 8

    def pack(d):
        parts = [_pad_rows(d[n], -(-sizes[n] // D_MODEL)) for n in small_names]
        parts.append(jnp.zeros((rows_pad - rows_needed, D_MODEL), f32))
        return jnp.concatenate(parts, axis=0)

    sd, sm, sv = _adamw(pack(w), pack(grads), pack(m), pack(v), "adamw_small")

    def unpack(a, n, o):
        r = -(-sizes[n] // D_MODEL)
        return a[o:o + r].reshape(-1)[:sizes[n]].reshape(w[n].shape), o + r

    o = 0
    for n in small_names:
        delta[n], _ = unpack(sd, n, o)
        new_m[n], _ = unpack(sm, n, o)
        new_v[n], o = unpack(sv, n, o)

    return (loss, grad_x[None], *[grads[n] for n in WEIGHTS], *[delta[n] for n in WEIGHTS],
            *[new_m[n] for n in WEIGHTS], *[new_v[n] for n in WEIGHTS])
```

```python
import jax
import jax.numpy as jnp
from jax import lax
from jax.experimental import pallas as pl
from jax.experimental.pallas import tpu as pltpu

f32 = jnp.float32
bf16 = jnp.bfloat16

D_MODEL = 1024
SSD_WIDTH = 1024
SSD_HEADS = 16
HEAD_DIM = 64
SSD_STATE = 128
XBC_WIDTH = 1536
SSD_CONV = 4
CHUNK = 128
CONF_WIDTH = 1024
CONF_KERNEL = 31
D_FF = 4096
PLE_DIM = 256
IN_WIDTH = 4624
EPS = 1e-6
N_CHIPS = 4
N_DEV = 8

ADAM_LR = 0.001
ADAM_B1 = 0.9
ADAM_B2 = 0.999
ADAM_EPS = 1e-08
ADAM_WD = 0.01
ADAM_STEP = 10

LANES = 128
VMEM_BIG = 56 * 1024 * 1024
VMEM_MID = 40 * 1024 * 1024

UP_OFF, DOWN_OFF, OUT_OFF, PG_OFF, PLE_OFF = 0, 1024, 2048, 2560, 2816
PLE_ROWS = 64
SLAB_A = PLE_OFF + PLE_ROWS
W_IN_SHARD = (1024, 1156)
CONVW_ROWS = 16
SMALL_ROWS = 56

MESH = pl.DeviceIdType.MESH
ANY = pl.BlockSpec(memory_space=pl.ANY)


def _cparams(sem=None, vmem=None):
    return pltpu.CompilerParams(dimension_semantics=sem, vmem_limit_bytes=vmem)


def _full(shape):
    n = len(shape)
    return pl.BlockSpec(shape, lambda *_: (0,) * n)


def _dot(a, b):
    return jnp.dot(a, b, preferred_element_type=f32)


def _dot_nt(a, b):
    return lax.dot_general(a, b, (((1,), (1,)), ((), ())), preferred_element_type=f32)


def _dot_tn(a, b):
    return lax.dot_general(a, b, (((0,), (0,)), ((), ())), preferred_element_type=f32)


def _sigmoid(x):
    return jax.nn.sigmoid(x)


def _rms(x, g):
    r = lax.rsqrt(jnp.mean(x * x, axis=-1, keepdims=True) + EPS)
    return x * r * g


def _rms_bwd(dy, x, g):
    r = lax.rsqrt(jnp.mean(x * x, axis=-1, keepdims=True) + EPS)
    xh = x * r
    dg = jnp.sum(dy * xh, axis=0, keepdims=True)
    dxh = dy * g
    dx = r * (dxh - xh * jnp.mean(dxh * xh, axis=-1, keepdims=True))
    return dx, dg


def _dsilu(x):
    s = _sigmoid(x)
    return s * (1.0 + x * (1.0 - s))


def _split3(x):
    hi = x.astype(bf16)
    r1 = x - hi.astype(f32)
    mid = r1.astype(bf16)
    lo = (r1 - mid.astype(f32)).astype(bf16)
    return hi, mid, lo


def _head_matrix():
    row = lax.broadcasted_iota(jnp.int32, (LANES, SSD_WIDTH), 0)
    col = lax.broadcasted_iota(jnp.int32, (LANES, SSD_WIDTH), 1)
    lo = row * HEAD_DIM
    return ((col >= lo) & (col < lo + HEAD_DIM)).astype(bf16)


def _expand(x, e):
    hi, mid, lo = _split3(x)
    return _dot(hi, e) + _dot(mid, e) + _dot(lo, e)


def _contract(x, e):
    hi, mid, lo = _split3(x)
    return _dot_nt(hi, e) + _dot_nt(mid, e) + _dot_nt(lo, e)


def _in_proj_fwd(x, g, wz, wxbc, wcv, wcg, wdt):
    T = x.shape[0]
    tm = min(256, T)

    def body(x_ref, g_ref, wz_ref, wx_ref, wcv_ref, wcg_ref, wdt_ref,
             u_ref, z_ref, xbc_ref, cv_ref, cg_ref, dt_ref, v_ref):
        ub = _rms(x_ref[...], g_ref[...]).astype(bf16)
        u_ref[...] = ub
        z_ref[...] = _dot(ub, wz_ref[...])
        xbc_ref[...] = _dot(ub, wx_ref[...])
        cv = _dot(ub, wcv_ref[...])
        cg = _dot(ub, wcg_ref[...])
        cv_ref[...] = cv
        cg_ref[...] = cg
        v_ref[...] = cv * _sigmoid(cg)
        dt_ref[...] = _dot(ub, wdt_ref[...])

    row = lambda n: pl.BlockSpec((tm, n), lambda i: (i, 0))
    return pl.pallas_call(
        body, name="in_proj_fwd", grid=(T // tm,),
        in_specs=[row(D_MODEL), _full((1, D_MODEL)), _full(wz.shape), _full(wxbc.shape), _full(wcv.shape),
                  _full(wcg.shape), _full(wdt.shape)],
        out_specs=[row(D_MODEL), row(SSD_WIDTH), row(XBC_WIDTH), row(CONF_WIDTH), row(CONF_WIDTH), row(LANES),
                   row(CONF_WIDTH)],
        out_shape=[jax.ShapeDtypeStruct((T, D_MODEL), bf16), jax.ShapeDtypeStruct((T, SSD_WIDTH), f32),
                   jax.ShapeDtypeStruct((T, XBC_WIDTH), f32), jax.ShapeDtypeStruct((T, CONF_WIDTH), f32),
                   jax.ShapeDtypeStruct((T, CONF_WIDTH), f32), jax.ShapeDtypeStruct((T, LANES), f32),
                   jax.ShapeDtypeStruct((T, CONF_WIDTH), f32)],
        compiler_params=_cparams(("parallel",), VMEM_BIG),
    )(x, g, wz, wxbc, wcv, wcg, wdt)


SUBLANES = 8


def _phases(offsets):
    return sorted({o % SUBLANES for o in offsets} - {0})


def _phase_shape(offsets, tm, C):
    a_max = max([o // SUBLANES for o in offsets if o % SUBLANES] or [0])
    return (max(len(_phases(offsets)), 1), tm + SUBLANES * a_max, C)


def _make_phases(buf_ref, ph_ref, offsets, tm):
    for idx, b in enumerate(_phases(offsets)):
        n = tm + SUBLANES * max(o // SUBLANES for o in offsets if o % SUBLANES == b)
        ph_ref[idx, 0:n, :] = buf_ref[pl.ds(b, n), :]


def _window(buf_ref, ph_ref, offsets, o, r0, rb):
    a, b = divmod(o, SUBLANES)
    if b == 0:
        return buf_ref[pl.ds(r0 + SUBLANES * a, rb), :]
    return ph_ref[_phases(offsets).index(b), pl.ds(r0 + SUBLANES * a, rb), :]


def _conv_rows(wb_ref, buf_ref, ph_ref, offsets, r0, rb):
    nsub = rb // SUBLANES
    C = wb_ref.shape[1]
    acc = None
    for k, o in enumerate(offsets):
        wk = wb_ref[pl.ds(SUBLANES * k, SUBLANES), :]
        win = _window(buf_ref, ph_ref, offsets, o, r0, rb).reshape(nsub, SUBLANES, C)
        term = wk[None] * win
        acc = term if acc is None else acc + term
    return acc.reshape(rb, C)


def _sublane_rows(w):
    return jnp.repeat(w, SUBLANES, axis=0)


def _fwd_offsets(K, hb):
    return [hb - (K - 1) + k for k in range(K)]


def _prev_halo_spec(hb, tm, C):
    return pl.BlockSpec((hb, C), lambda i: (jnp.maximum(i * (tm // hb) - 1, 0), 0))


CONV_RB = 16


def _ssd_conv_fwd(xbc, w, b):
    T, C = xbc.shape
    K, hb = SSD_CONV, 8
    tm = min(256, T)
    offs = _fwd_offsets(K, hb)

    def body(cur_ref, halo_ref, w_ref, b_ref, pre_ref, buf_ref, ph_ref):
        keep = jnp.where(pl.program_id(0) > 0, 1.0, 0.0)
        buf_ref[0:hb, :] = halo_ref[...] * keep
        buf_ref[hb:hb + tm, :] = cur_ref[...]
        _make_phases(buf_ref, ph_ref, offs, tm)

        def chunk(i, carry):
            r0 = pl.multiple_of(i * CONV_RB, CONV_RB)
            pre_ref[pl.ds(r0, CONV_RB), :] = _conv_rows(w_ref, buf_ref, ph_ref, offs, r0, CONV_RB) + b_ref[...]
            return carry

        lax.fori_loop(0, tm // CONV_RB, chunk, 0)

    return pl.pallas_call(
        body, name="ssd_conv_fwd", grid=(T // tm,),
        in_specs=[pl.BlockSpec((tm, C), lambda i: (i, 0)), _prev_halo_spec(hb, tm, C), _full((SUBLANES * K, C)),
                  _full((1, C))],
        out_specs=pl.BlockSpec((tm, C), lambda i: (i, 0)),
        out_shape=jax.ShapeDtypeStruct((T, C), f32),
        scratch_shapes=[pltpu.VMEM((hb + tm, C), f32), pltpu.VMEM(_phase_shape(offs, tm, C), f32)],
        compiler_params=_cparams(("parallel",), VMEM_MID),
    )(xbc, xbc, _sublane_rows(w), b)


def _conf_fwd(v, w, b, ln_g, ln_b):
    T, C = v.shape
    K, hb = CONF_KERNEL, 32
    tm = min(256, T)
    offs = _fwd_offsets(K, hb)
    rb = 2 * CONV_RB

    def body(cur_ref, halo_ref, w_ref, b_ref, g_ref, bb_ref, co_ref, y_ref, buf_ref, ph_ref):
        keep = jnp.where(pl.program_id(0) > 0, 1.0, 0.0)
        buf_ref[0:hb, :] = halo_ref[...] * keep
        buf_ref[hb:hb + tm, :] = cur_ref[...]
        _make_phases(buf_ref, ph_ref, offs, tm)

        def chunk(i, carry):
            r0 = pl.multiple_of(i * rb, rb)
            co = _conv_rows(w_ref, buf_ref, ph_ref, offs, r0, rb) + b_ref[...]
            co_ref[pl.ds(r0, rb), :] = co
            mu = jnp.mean(co, axis=-1, keepdims=True)
            xc = co - mu
            yn = xc * lax.rsqrt(jnp.mean(xc * xc, axis=-1, keepdims=True) + EPS) * g_ref[...] + bb_ref[...]
            y_ref[pl.ds(r0, rb), :] = (yn * _sigmoid(yn)).astype(bf16)
            return carry

        lax.fori_loop(0, tm // rb, chunk, 0)

    return pl.pallas_call(
        body, name="conf_fwd", grid=(T // tm,),
        in_specs=[pl.BlockSpec((tm, C), lambda i: (i, 0)), _prev_halo_spec(hb, tm, C), _full((SUBLANES * K, C)),
                  _full((1, C)), _full((1, C)), _full((1, C))],
        out_specs=[pl.BlockSpec((tm, C), lambda i: (i, 0)), pl.BlockSpec((tm, C), lambda i: (i, 0))],
        out_shape=[jax.ShapeDtypeStruct((T, C), f32), jax.ShapeDtypeStruct((T, C), bf16)],
        scratch_shapes=[pltpu.VMEM((hb + tm, C), f32), pltpu.VMEM(_phase_shape(offs, tm, C), f32)],
        compiler_params=_cparams(("parallel",), VMEM_MID),
    )(v, v, _sublane_rows(w), b, ln_g, ln_b)


def _ssd_chunk_common(pre, dtr, dtb, alog, e):
    act = pre * _sigmoid(pre)
    xs = act[:, :SSD_WIDTH]
    bm = act[:, SSD_WIDTH:SSD_WIDTH + 2 * SSD_STATE]
    cm = act[:, SSD_WIDTH + 2 * SSD_STATE:]
    row = lax.broadcasted_iota(jnp.int32, (CHUNK, CHUNK), 0)
    col = lax.broadcasted_iota(jnp.int32, (CHUNK, CHUNK), 1)
    tri = row >= col
    dt = jax.nn.softplus(dtr + dtb)
    a_neg = -jnp.exp(alog)
    a = dt * a_neg
    cs = jnp.dot(tri.astype(f32), a, precision=lax.Precision.HIGHEST, preferred_element_type=f32)
    cs_e = _expand(cs, e)
    dt_e = _expand(dt, e)
    csl_e = cs_e[CHUNK - 1:CHUNK, :]
    ecs_e = jnp.exp(cs_e)
    dte_e = jnp.exp(csl_e - cs_e)
    cd_e = jnp.exp(csl_e)
    xc = xs * dt_e
    xd = xc * dte_e
    return dict(xs=xs, bm=bm, cm=cm, tri=tri, dt=dt, a_neg=a_neg, cs=cs, ecs_e=ecs_e, dte_e=dte_e, cd_e=cd_e,
                dt_e=dt_e, xc=xc, xd=xd)


def _group(v, g, width):
    return v[:, g * width:(g + 1) * width]


def _ssd_fwd(pre, dtr, z, dtb, alog, dskip_e, gn):
    T = pre.shape[0]
    nc = T // CHUNK
    GW = SSD_WIDTH // 2

    def body(pre_ref, dtr_ref, z_ref, dtb_ref, alog_ref, de_ref, gn_ref, y_ref, ys_ref, sp_ref, st_ref):
        @pl.when(pl.program_id(0) == 0)
        def _():
            st_ref[...] = jnp.zeros_like(st_ref)

        e = _head_matrix()
        q = _ssd_chunk_common(pre_ref[...], dtr_ref[...], dtb_ref[...], alog_ref[...], e)
        cs, tri, xc, xd = q["cs"], q["tri"], q["xc"], q["xd"]
        cs_t = cs.T
        st = st_ref[...]
        sp_ref[0] = st
        lane = lax.broadcasted_iota(jnp.int32, (1, LANES), 1)
        halves = (lane < HEAD_DIM, lane >= HEAD_DIM)

        g_mat, y_off, s_new = [], [], []
        for g in range(2):
            bg = _group(q["bm"], g, SSD_STATE)
            cg = _group(q["cm"], g, SSD_STATE)
            bgb, cgb = bg.astype(bf16), cg.astype(bf16)
            g_mat.append(_dot_nt(cgb, bgb))
            y_off.append(_dot(cgb, _group(st, g, GW).astype(bf16)))
            s_new.append(_dot(bg.T.astype(bf16), _group(xd, g, GW).astype(bf16)))
        y_off = jnp.concatenate(y_off, axis=1) * q["ecs_e"]
        st_ref[...] = st * q["cd_e"] + jnp.concatenate(s_new, axis=1)

        pairs = []
        for j in range(SSD_HEADS // 2):
            xp = xc[:, j * LANES:(j + 1) * LANES]
            acc = jnp.zeros((CHUNK, LANES), f32)
            for hh in range(2):
                h = 2 * j + hh
                seg = cs[:, h:h + 1] - cs_t[h:h + 1, :]
                lm = jnp.exp(jnp.where(tri, seg, -1e30))
                m = (g_mat[h // 8] * lm).astype(bf16)
                acc = acc + _dot(m, jnp.where(halves[hh], xp, 0.0).astype(bf16))
            pairs.append(acc)
        y = jnp.concatenate(pairs, axis=1) + y_off + q["xs"] * de_ref[...]
        y_ref[...] = y

        zz = z_ref[...]
        v = y * (zz * _sigmoid(zz))
        outs = []
        for g in range(2):
            vg = _group(v, g, GW)
            outs.append(vg * lax.rsqrt(jnp.mean(vg * vg, axis=-1, keepdims=True) + EPS))
        ys_ref[...] = (jnp.concatenate(outs, axis=1) * gn_ref[...]).astype(bf16)

    ch = lambda n: pl.BlockSpec((CHUNK, n), lambda c: (c, 0))
    return pl.pallas_call(
        body, name="ssd_fwd", grid=(nc,),
        in_specs=[ch(XBC_WIDTH), ch(LANES), ch(SSD_WIDTH), _full((1, LANES)), _full((1, LANES)), _full((1, SSD_WIDTH)),
                  _full((1, SSD_WIDTH))],
        out_specs=[ch(SSD_WIDTH), ch(SSD_WIDTH), pl.BlockSpec((1, SSD_STATE, SSD_WIDTH), lambda c: (c, 0, 0))],
        out_shape=[jax.ShapeDtypeStruct((T, SSD_WIDTH), f32), jax.ShapeDtypeStruct((T, SSD_WIDTH), bf16),
                   jax.ShapeDtypeStruct((nc, SSD_STATE, SSD_WIDTH), f32)],
        scratch_shapes=[pltpu.VMEM((SSD_STATE, SSD_WIDTH), f32)],
        compiler_params=_cparams(("arbitrary",), VMEM_MID),
    )(pre, dtr, z, dtb, alog, dskip_e, gn)


def _w_out_spec():
    n = 2 * SSD_WIDTH // N_CHIPS
    return pl.BlockSpec((N_CHIPS, n, D_MODEL), lambda *_: (0, OUT_OFF // n, 0))


def _out_proj_fwd(x, ys, yc, gath, g):
    T = x.shape[0]
    tm = min(512, T)
    n = 2 * SSD_WIDTH // N_CHIPS

    def body(x_ref, ys_ref, yc_ref, w_ref, g_ref, h_ref, u_ref):
        h = (x_ref[...] + _dot(ys_ref[:, 0:n], w_ref[0]) + _dot(ys_ref[:, n:], w_ref[1])
             + _dot(yc_ref[:, 0:n], w_ref[2]) + _dot(yc_ref[:, n:], w_ref[3]))
        h_ref[...] = h
        u_ref[...] = _rms(h, g_ref[...]).astype(bf16)

    row = pl.BlockSpec((tm, D_MODEL), lambda i: (i, 0))
    return pl.pallas_call(
        body, name="out_proj_fwd", grid=(T // tm,),
        in_specs=[row, row, row, _w_out_spec(), _full((1, D_MODEL))],
        out_specs=[row, row],
        out_shape=[jax.ShapeDtypeStruct((T, D_MODEL), f32), jax.ShapeDtypeStruct((T, D_MODEL), bf16)],
        compiler_params=_cparams(("parallel",), VMEM_MID),
    )(x, ys, yc, gath, g)


def _w_up_spec():
    return pl.BlockSpec((1, D_MODEL, D_MODEL), lambda i, b: (b, UP_OFF // D_MODEL, 0))


def _w_down_spec():
    return pl.BlockSpec((1, D_MODEL, D_MODEL), lambda i, b: (b, DOWN_OFF // D_MODEL, 0))


def _mlp_fwd(h1, u1, gath, g_next):
    T = h1.shape[0]
    tm = min(512, T)
    nb = D_FF // D_MODEL

    def body(h_ref, u_ref, wu_ref, wd_ref, g_ref, r_ref, h2_ref, u2_ref, acc_ref):
        b = pl.program_id(1)

        @pl.when(b == 0)
        def _():
            acc_ref[...] = jnp.zeros_like(acc_ref)

        r = jnp.maximum(_dot(u_ref[...], wu_ref[0]), 0.0)
        r_ref[...] = r.astype(bf16)
        acc_ref[...] += _dot((r * r).astype(bf16), wd_ref[0])

        @pl.when(b == nb - 1)
        def _():
            h2 = h_ref[...] + acc_ref[...]
            h2_ref[...] = h2
            u2_ref[...] = _rms(h2, g_ref[...]).astype(bf16)

    row = pl.BlockSpec((tm, D_MODEL), lambda i, b: (i, 0))
    return pl.pallas_call(
        body, name="mlp_fwd", grid=(T // tm, nb),
        in_specs=[row, row, _w_up_spec(), _w_down_spec(), _full((1, D_MODEL))],
        out_specs=[pl.BlockSpec((tm, D_MODEL), lambda i, b: (i, b)), row, row],
        out_shape=[jax.ShapeDtypeStruct((T, D_FF), bf16), jax.ShapeDtypeStruct((T, D_MODEL), f32),
                   jax.ShapeDtypeStruct((T, D_MODEL), bf16)],
        scratch_shapes=[pltpu.VMEM((tm, D_MODEL), f32)],
        compiler_params=_cparams(("parallel", "arbitrary"), VMEM_MID),
    )(h1, u1, gath, gath, g_next)


def _ple_loss(h2, u2, p, tgt, gath, b_pg, w_ple, g_ple, g_fin, g_pg):
    T = h2.shape[0]
    tm = min(256, T)
    npg = D_MODEL // N_CHIPS

    def body(h2_ref, u2_ref, p_ref, t_ref, wpg_ref, bpg_ref, wple_ref, gple_ref, gfin_ref, gpg_ref,
             loss_ref, dh2_ref, dh2b_ref, dgp_ref, dep_ref, dgfin_ref, dgple_ref, dbpg_ref, dgpg_ref):
        @pl.when(pl.program_id(0) == 0)
        def _():
            loss_ref[...] = jnp.zeros_like(loss_ref)
            dgfin_ref[...] = jnp.zeros_like(dgfin_ref)
            dgple_ref[...] = jnp.zeros_like(dgple_ref)
            dbpg_ref[...] = jnp.zeros_like(dbpg_ref)
            dgpg_ref[...] = jnp.zeros_like(dgpg_ref)

        h2 = h2_ref[...]
        gate_pre = bpg_ref[...]
        for b in range(N_CHIPS):
            gate_pre = gate_pre + _dot(u2_ref[:, b * npg:(b + 1) * npg], wpg_ref[b])
        gate = _sigmoid(gate_pre)
        e_pre = _dot(p_ref[...].astype(bf16), wple_ref[...])
        emb = _rms(e_pre, gple_ref[...])
        h3 = h2 + gate * emb
        diff = _rms(h3, gfin_ref[...]) - t_ref[...]
        sq = jnp.sum(jnp.sum(diff * diff, axis=1, keepdims=True), axis=0, keepdims=True)
        loss_ref[...] += (0.5 / D_MODEL) * sq
        dh3, dgfin = _rms_bwd(diff * (1.0 / D_MODEL), h3, gfin_ref[...])
        dgfin_ref[...] += dgfin
        dgp = dh3 * emb * gate * (1.0 - gate)
        dbpg_ref[...] += jnp.sum(dgp, axis=0, keepdims=True)
        dep, dgple = _rms_bwd(dh3 * gate, e_pre, gple_ref[...])
        dgple_ref[...] += dgple
        dgpb = dgp.astype(bf16)
        dgp_ref[...] = dgpb
        dep_ref[...] = dep.astype(bf16)
        du2 = jnp.concatenate([_dot_nt(dgpb, wpg_ref[b]) for b in range(N_CHIPS)], axis=1)
        dx, dgpg = _rms_bwd(du2, h2, gpg_ref[...])
        dgpg_ref[...] += dgpg
        dh2 = dh3 + dx
        dh2_ref[...] = dh2
        dh2b_ref[...] = dh2.astype(bf16)

    row = pl.BlockSpec((tm, D_MODEL), lambda i: (i, 0))
    vec = _full((1, D_MODEL))
    vshape = jax.ShapeDtypeStruct((1, D_MODEL), f32)
    return pl.pallas_call(
        body, name="ple_loss", grid=(T // tm,),
        in_specs=[row, row, pl.BlockSpec((tm, PLE_DIM), lambda i: (i, 0)), row,
                  pl.BlockSpec((N_CHIPS, npg, D_MODEL), lambda i: (0, PG_OFF // npg, 0)), vec, _full(w_ple.shape),
                  vec, vec, vec],
        out_specs=[_full((8, LANES)), row, row, row, row, vec, vec, vec, vec],
        out_shape=[jax.ShapeDtypeStruct((8, LANES), f32), jax.ShapeDtypeStruct((T, D_MODEL), f32),
                   jax.ShapeDtypeStruct((T, D_MODEL), bf16), jax.ShapeDtypeStruct((T, D_MODEL), bf16),
                   jax.ShapeDtypeStruct((T, D_MODEL), bf16), vshape, vshape, vshape, vshape],
        compiler_params=_cparams(("arbitrary",), VMEM_MID),
    )(h2, u2, p, tgt, gath, b_pg, w_ple, g_ple, g_fin, g_pg)


def _mlp_bwd(dh2, r, gath, h1, g):
    T = dh2.shape[0]
    tm = min(512, T)
    nb = D_FF // D_MODEL

    def body(dh2_ref, r_ref, wd_ref, wu_ref, h1_ref, g_ref, dhp_ref, dh1_ref, dh1b_ref, dg_ref, acc_ref):
        i, b = pl.program_id(0), pl.program_id(1)

        @pl.when(b == 0)
        def _():
            acc_ref[...] = jnp.zeros_like(acc_ref)

        @pl.when((b == 0) & (i == 0))
        def _():
            dg_ref[...] = jnp.zeros_like(dg_ref)

        dact = _dot_nt(dh2_ref[...].astype(bf16), wd_ref[0])
        dhp = (dact * 2.0 * r_ref[...].astype(f32)).astype(bf16)
        dhp_ref[...] = dhp
        acc_ref[...] += _dot_nt(dhp, wu_ref[0])

        @pl.when(b == nb - 1)
        def _():
            dx, dg = _rms_bwd(acc_ref[...], h1_ref[...], g_ref[...])
            dg_ref[...] += dg
            dh1 = dh2_ref[...] + dx
            dh1_ref[...] = dh1
            dh1b_ref[...] = dh1.astype(bf16)

    row = pl.BlockSpec((tm, D_MODEL), lambda i, b: (i, 0))
    return pl.pallas_call(
        body, name="mlp_bwd", grid=(T // tm, nb),
        in_specs=[row, pl.BlockSpec((tm, D_MODEL), lambda i, b: (i, b)), _w_down_spec(), _w_up_spec(), row,
                  _full((1, D_MODEL))],
        out_specs=[pl.BlockSpec((tm, D_MODEL), lambda i, b: (i, b)), row, row, _full((1, D_MODEL))],
        out_shape=[jax.ShapeDtypeStruct((T, D_FF), bf16), jax.ShapeDtypeStruct((T, D_MODEL), f32),
                   jax.ShapeDtypeStruct((T, D_MODEL), bf16), jax.ShapeDtypeStruct((1, D_MODEL), f32)],
        scratch_shapes=[pltpu.VMEM((tm, D_MODEL), f32)],
        compiler_params=_cparams(("arbitrary", "arbitrary"), VMEM_MID),
    )(dh2, r, gath, gath, h1, g)


def _out_proj_bwd(dh1, gath, co, ln_g, ln_b):
    T = dh1.shape[0]
    tm = min(512, T)

    def body(dh_ref, w_ref, co_ref, g_ref, b_ref, dys_ref, dco_ref, dg_ref, db_ref):
        @pl.when(pl.program_id(0) == 0)
        def _():
            dg_ref[...] = jnp.zeros_like(dg_ref)
            db_ref[...] = jnp.zeros_like(db_ref)

        dhb = dh_ref[...].astype(bf16)
        dys_ref[...] = jnp.concatenate([_dot_nt(dhb, w_ref[0]), _dot_nt(dhb, w_ref[1])], axis=1)
        dyc = jnp.concatenate([_dot_nt(dhb, w_ref[2]), _dot_nt(dhb, w_ref[3])], axis=1)
        co = co_ref[...]
        mu = jnp.mean(co, axis=-1, keepdims=True)
        xc = co - mu
        rstd = lax.rsqrt(jnp.mean(xc * xc, axis=-1, keepdims=True) + EPS)
        xh = xc * rstd
        yn = xh * g_ref[...] + b_ref[...]
        dyn = dyc * _dsilu(yn)
        dg_ref[...] += jnp.sum(dyn * xh, axis=0, keepdims=True)
        db_ref[...] += jnp.sum(dyn, axis=0, keepdims=True)
        dxh = dyn * g_ref[...]
        dco_ref[...] = rstd * (dxh - jnp.mean(dxh, axis=-1, keepdims=True)
                               - xh * jnp.mean(dxh * xh, axis=-1, keepdims=True))

    row = pl.BlockSpec((tm, D_MODEL), lambda i: (i, 0))
    vec = _full((1, CONF_WIDTH))
    vshape = jax.ShapeDtypeStruct((1, CONF_WIDTH), f32)
    return pl.pallas_call(
        body, name="out_proj_bwd", grid=(T // tm,),
        in_specs=[row, _w_out_spec(), row, vec, vec],
        out_specs=[row, row, vec, vec],
        out_shape=[jax.ShapeDtypeStruct((T, SSD_WIDTH), f32), jax.ShapeDtypeStruct((T, CONF_WIDTH), f32), vshape, vshape],
        compiler_params=_cparams(("arbitrary",), VMEM_MID),
    )(dh1, gath, co, ln_g, ln_b)


def _bwd_offsets(K):
    return [K - 1 - k for k in range(K)]


def _next_halo_spec(hb, tm, C, T):
    return pl.BlockSpec((hb, C), lambda i: (jnp.minimum((i + 1) * (tm // hb), T // hb - 1), 0))


DW_RB = 8
DW_UNROLL = 4
DW_ACC_VREGS = 32


def _conv_dw(dw_ref, bufd_ref, bufx_ref, phx_ref, offs_x, tm, C):
    K = len(offs_x)
    group = max(1, DW_ACC_VREGS // (C // LANES))
    for k0 in range(0, K, group):
        ks = list(range(k0, min(k0 + group, K)))

        def step(i, accs, ks=ks):
            for u in range(DW_UNROLL):
                r0 = pl.multiple_of((i * DW_UNROLL + u) * DW_RB, DW_RB)
                d = bufd_ref[pl.ds(r0, DW_RB), :]
                accs = tuple(acc + _window(bufx_ref, phx_ref, offs_x, offs_x[k], r0, DW_RB) * d
                             for k, acc in zip(ks, accs))
            return accs

        accs = lax.fori_loop(0, tm // (DW_RB * DW_UNROLL), step, tuple(jnp.zeros((DW_RB, C), f32) for _ in ks))
        for k, acc in zip(ks, accs):
            dw_ref[k:k + 1, :] += jnp.sum(acc, axis=0, keepdims=True)


def _fill_bwd_buffers(dcur_ref, dnext_ref, xcur_ref, xprev_ref, bufd_ref, bufx_ref, phd_ref, phx_ref, offs_d, offs_x,
                      hb, tm, first, last):
    bufd_ref[0:tm, :] = dcur_ref[...]
    bufd_ref[tm:tm + hb, :] = dnext_ref[...] * jnp.where(last, 0.0, 1.0)
    bufx_ref[0:hb, :] = xprev_ref[...] * jnp.where(first, 0.0, 1.0)
    bufx_ref[hb:hb + tm, :] = xcur_ref[...]
    _make_phases(bufd_ref, phd_ref, offs_d, tm)
    _make_phases(bufx_ref, phx_ref, offs_x, tm)


def _ssd_conv_bwd(dpre, xbc, w):
    T, C = xbc.shape
    K, hb = SSD_CONV, 8
    tm = min(256, T)
    nt = T // tm
    offs_d, offs_x = _bwd_offsets(K), _fwd_offsets(K, hb)

    def body(dcur_ref, dnext_ref, xcur_ref, xprev_ref, w_ref, dx_ref, dw_ref, db_ref, bufd_ref, bufx_ref, phd_ref, phx_ref):
        i = pl.program_id(0)

        @pl.when(i == 0)
        def _():
            dw_ref[...] = jnp.zeros_like(dw_ref)
            db_ref[...] = jnp.zeros_like(db_ref)

        _fill_bwd_buffers(dcur_ref, dnext_ref, xcur_ref, xprev_ref, bufd_ref, bufx_ref, phd_ref, phx_ref, offs_d, offs_x,
                          hb, tm, i == 0, i == nt - 1)

        def chunk(j, carry):
            r0 = pl.multiple_of(j * CONV_RB, CONV_RB)
            dx_ref[pl.ds(r0, CONV_RB), :] = _conv_rows(w_ref, bufd_ref, phd_ref, offs_d, r0, CONV_RB).astype(bf16)
            return carry

        lax.fori_loop(0, tm // CONV_RB, chunk, 0)
        _conv_dw(dw_ref, bufd_ref, bufx_ref, phx_ref, offs_x, tm, C)
        db_ref[...] += jnp.sum(dcur_ref[...], axis=0, keepdims=True)

    row = pl.BlockSpec((tm, C), lambda i: (i, 0))
    return pl.pallas_call(
        body, name="ssd_conv_bwd", grid=(nt,),
        in_specs=[row, _next_halo_spec(hb, tm, C, T), row, _prev_halo_spec(hb, tm, C), _full((SUBLANES * K, C))],
        out_specs=[row, _full((8, C)), _full((1, C))],
        out_shape=[jax.ShapeDtypeStruct((T, C), bf16), jax.ShapeDtypeStruct((8, C), f32), jax.ShapeDtypeStruct((1, C), f32)],
        scratch_shapes=[pltpu.VMEM((tm + hb, C), f32), pltpu.VMEM((hb + tm, C), f32),
                        pltpu.VMEM(_phase_shape(offs_d, tm, C), f32),
                        pltpu.VMEM(_phase_shape(offs_x, tm, C), f32)],
        compiler_params=_cparams(("arbitrary",), VMEM_BIG),
    )(dpre, dpre, xbc, xbc, _sublane_rows(w))


def _conf_conv_bwd(dco, v, w, cv, cg):
    T, C = v.shape
    K, hb = CONF_KERNEL, 32
    tm = min(256, T)
    nt = T // tm
    offs_d, offs_x = _bwd_offsets(K), _fwd_offsets(K, hb)

    def body(dcur_ref, dnext_ref, vcur_ref, vprev_ref, w_ref, cv_ref, cg_ref, dcv_ref, dcg_ref, dw_ref, db_ref,
             bufd_ref, bufx_ref, phd_ref, phx_ref):
        i = pl.program_id(0)

        @pl.when(i == 0)
        def _():
            dw_ref[...] = jnp.zeros_like(dw_ref)
            db_ref[...] = jnp.zeros_like(db_ref)

        _fill_bwd_buffers(dcur_ref, dnext_ref, vcur_ref, vprev_ref, bufd_ref, bufx_ref, phd_ref, phx_ref, offs_d, offs_x,
                          hb, tm, i == 0, i == nt - 1)

        def chunk(j, carry):
            r0 = pl.multiple_of(j * CONV_RB, CONV_RB)
            rows = pl.ds(r0, CONV_RB)
            dv = _conv_rows(w_ref, bufd_ref, phd_ref, offs_d, r0, CONV_RB)
            s = _sigmoid(cg_ref[rows, :])
            dcv_ref[rows, :] = (dv * s).astype(bf16)
            dcg_ref[rows, :] = (dv * cv_ref[rows, :] * s * (1.0 - s)).astype(bf16)
            return carry

        lax.fori_loop(0, tm // CONV_RB, chunk, 0)
        _conv_dw(dw_ref, bufd_ref, bufx_ref, phx_ref, offs_x, tm, C)
        db_ref[...] += jnp.sum(dcur_ref[...], axis=0, keepdims=True)

    row = pl.BlockSpec((tm, C), lambda i: (i, 0))
    return pl.pallas_call(
        body, name="conf_conv_bwd", grid=(nt,),
        in_specs=[row, _next_halo_spec(hb, tm, C, T), row, _prev_halo_spec(hb, tm, C), _full((SUBLANES * K, C)), row, row],
        out_specs=[row, row, _full((32, C)), _full((1, C))],
        out_shape=[jax.ShapeDtypeStruct((T, C), bf16), jax.ShapeDtypeStruct((T, C), bf16),
                   jax.ShapeDtypeStruct((32, C), f32), jax.ShapeDtypeStruct((1, C), f32)],
        scratch_shapes=[pltpu.VMEM((tm + hb, C), f32), pltpu.VMEM((hb + tm, C), f32),
                        pltpu.VMEM(_phase_shape(offs_d, tm, C), f32),
                        pltpu.VMEM(_phase_shape(offs_x, tm, C), f32)],
        compiler_params=_cparams(("arbitrary",), VMEM_BIG),
    )(dco, dco, v, v, _sublane_rows(w), cv, cg)


def _ssd_bwd(dys, y, z, pre, dtr, sprev, dtb, alog, dskip_e, gn):
    T = pre.shape[0]
    nc = T // CHUNK
    GW = SSD_WIDTH // 2

    def body(dys_ref, y_ref, z_ref, pre_ref, dtr_ref, sp_ref, dtb_ref, alog_ref, de_ref, gn_ref,
             dz_ref, dpre_ref, ddtr_ref, dgn_ref, dd_ref, dal_ref, ddtb_ref, ds_ref):
        @pl.when(pl.program_id(0) == 0)
        def _():
            ds_ref[...] = jnp.zeros_like(ds_ref)
            dgn_ref[...] = jnp.zeros_like(dgn_ref)
            dd_ref[...] = jnp.zeros_like(dd_ref)
            dal_ref[...] = jnp.zeros_like(dal_ref)
            ddtb_ref[...] = jnp.zeros_like(ddtb_ref)

        e = _head_matrix()
        pre = pre_ref[...]
        dtr_b = dtr_ref[...] + dtb_ref[...]
        q = _ssd_chunk_common(pre, dtr_ref[...], dtb_ref[...], alog_ref[...], e)
        cs, tri, xc, xd, xs, dt = q["cs"], q["tri"], q["xc"], q["xd"], q["xs"], q["dt"]
        cs_t = cs.T
        st = sp_ref[0]
        dsn = ds_ref[...]
        lane = lax.broadcasted_iota(jnp.int32, (1, LANES), 1)
        halves = (lane < HEAD_DIM, lane >= HEAD_DIM)
        row_i = lax.broadcasted_iota(jnp.int32, (CHUNK, CHUNK), 0)
        col_i = lax.broadcasted_iota(jnp.int32, (CHUNK, CHUNK), 1)
        tri_t = col_i >= row_i

        y = y_ref[...]
        zz = z_ref[...]
        sz = _sigmoid(zz)
        silu_z = zz * sz
        v = y * silu_z
        dout = dys_ref[...]
        gn_v = gn_ref[...]
        dv, vh = [], []
        for g in range(2):
            vg = _group(v, g, GW)
            rstd = lax.rsqrt(jnp.mean(vg * vg, axis=-1, keepdims=True) + EPS)
            vhg = vg * rstd
            dvh = _group(dout, g, GW) * _group(gn_v, g, GW)
            dv.append(rstd * (dvh - vhg * jnp.mean(dvh * vhg, axis=-1, keepdims=True)))
            vh.append(vhg)
        dv = jnp.concatenate(dv, axis=1)
        dgn_ref[...] += jnp.sum(dout * jnp.concatenate(vh, axis=1), axis=0, keepdims=True)
        dy = dv * silu_z
        dz_ref[...] = (dv * y * (sz * (1.0 + zz * (1.0 - sz)))).astype(bf16)

        dd_row = jnp.sum(dy * xs, axis=0, keepdims=True)
        dd_ref[...] += _contract(jnp.broadcast_to(dd_row, (8, SSD_WIDTH)), e)[0:1, :]
        dxs = dy * de_ref[...]

        dz_in = dy * q["ecs_e"]
        g_mat, gt_mat, dcm, dbm, dsp, dxd, y_off = [], [], [], [], [], [], []
        bgs, cgs = [], []
        for g in range(2):
            bg = _group(q["bm"], g, SSD_STATE)
            cg = _group(q["cm"], g, SSD_STATE)
            bgb, cgb = bg.astype(bf16), cg.astype(bf16)
            bgs.append(bgb)
            cgs.append(cgb)
            stg = _group(st, g, GW).astype(bf16)
            dsng = _group(dsn, g, GW).astype(bf16)
            dzg = _group(dz_in, g, GW).astype(bf16)
            g_mat.append(_dot_nt(cgb, bgb))
            gt_mat.append(_dot_nt(bgb, cgb))
            y_off.append(_dot(cgb, stg))
            dcm.append(_dot_nt(dzg, stg))
            dsp.append(_dot(cg.T.astype(bf16), dzg))
            dbm.append(_dot_nt(_group(xd, g, GW).astype(bf16), dsng))
            dxd.append(_dot(bgb, dsng))
        y_off = jnp.concatenate(y_off, axis=1) * q["ecs_e"]
        dxd = jnp.concatenate(dxd, axis=1)
        ds_ref[...] = dsn * q["cd_e"] + jnp.concatenate(dsp, axis=1)
        dcd_row = jnp.sum(dsn * st, axis=0, keepdims=True) * q["cd_e"]
        t_e = dxd * xd
        dcs = _contract(dy * y_off - t_e, e)
        last_row = _contract(jnp.broadcast_to(dcd_row + jnp.sum(t_e, axis=0, keepdims=True), (8, SSD_WIDTH)), e)[0:1, :]
        dxc_state = dxd * q["dte_e"]

        dg_acc = [jnp.zeros((CHUNK, CHUNK), f32), jnp.zeros((CHUNK, CHUNK), f32)]
        dgt_acc = [jnp.zeros((CHUNK, CHUNK), f32), jnp.zeros((CHUNK, CHUNK), f32)]
        dxc_pairs = []
        for j in range(SSD_HEADS // 2):
            dyp_f = dy[:, j * LANES:(j + 1) * LANES]
            xcp_f = xc[:, j * LANES:(j + 1) * LANES]
            acc = jnp.zeros((CHUNK, LANES), f32)
            for hh in range(2):
                h = 2 * j + hh
                g = h // 8
                dyp = jnp.where(halves[hh], dyp_f, 0.0).astype(bf16)
                xcp = jnp.where(halves[hh], xcp_f, 0.0).astype(bf16)
                lm = jnp.exp(jnp.where(tri, cs[:, h:h + 1] - cs_t[h:h + 1, :], -1e30))
                lm_t = jnp.exp(jnp.where(tri_t, cs_t[h:h + 1, :] - cs[:, h:h + 1], -1e30))
                dm = _dot_nt(dyp, xcp) * lm
                dm_t = _dot_nt(xcp, dyp) * lm_t
                acc = acc + _dot((gt_mat[g] * lm_t).astype(bf16), dyp)
                dg_acc[g] = dg_acc[g] + dm
                dgt_acc[g] = dgt_acc[g] + dm_t
                qd = jnp.sum(dm * g_mat[g] - dm_t * gt_mat[g], axis=1, keepdims=True)
                dcs = dcs + qd * (lane == h).astype(f32)
            dxc_pairs.append(acc)
        dxc = jnp.concatenate(dxc_pairs, axis=1) + dxc_state
        for g in range(2):
            dcm[g] = dcm[g] + _dot(dg_acc[g].astype(bf16), bgs[g])
            dbm[g] = dbm[g] + _dot(dgt_acc[g].astype(bf16), cgs[g])

        dxs = dxs + dxc * q["dt_e"]
        ddt = _contract(dxc * xs, e)
        dcs = dcs + jnp.where(row_i == CHUNK - 1, jnp.broadcast_to(last_row, (CHUNK, LANES)), 0.0)
        da = jnp.dot(tri_t.astype(f32), dcs, precision=lax.Precision.HIGHEST, preferred_element_type=f32)
        ddt = ddt + da * q["a_neg"]
        dal_ref[...] += jnp.sum(da * dt, axis=0, keepdims=True) * q["a_neg"]
        ddtr = ddt * _sigmoid(dtr_b) * (lane < SSD_HEADS).astype(f32)
        ddtb_ref[...] += jnp.sum(ddtr, axis=0, keepdims=True)
        ddtr_ref[...] = ddtr.astype(bf16)

        dact = jnp.concatenate([dxs, dbm[0], dbm[1], dcm[0], dcm[1]], axis=1)
        dpre_ref[...] = dact * _dsilu(pre)

    rev = lambda n: pl.BlockSpec((CHUNK, n), lambda c: (nc - 1 - c, 0))
    vec = _full((1, LANES))
    vshape = jax.ShapeDtypeStruct((1, LANES), f32)
    return pl.pallas_call(
        body, name="ssd_bwd", grid=(nc,),
        in_specs=[rev(SSD_WIDTH), rev(SSD_WIDTH), rev(SSD_WIDTH), rev(XBC_WIDTH), rev(LANES),
                  pl.BlockSpec((1, SSD_STATE, SSD_WIDTH), lambda c: (nc - 1 - c, 0, 0)),
                  vec, vec, _full((1, SSD_WIDTH)), _full((1, SSD_WIDTH))],
        out_specs=[rev(SSD_WIDTH), rev(XBC_WIDTH), rev(LANES), _full((1, SSD_WIDTH)), vec, vec, vec],
        out_shape=[jax.ShapeDtypeStruct((T, SSD_WIDTH), bf16), jax.ShapeDtypeStruct((T, XBC_WIDTH), f32),
                   jax.ShapeDtypeStruct((T, LANES), bf16), jax.ShapeDtypeStruct((1, SSD_WIDTH), f32),
                   vshape, vshape, vshape],
        scratch_shapes=[pltpu.VMEM((SSD_STATE, SSD_WIDTH), f32)],
        compiler_params=_cparams(("arbitrary",), VMEM_MID),
    )(dys, y, z, pre, dtr, sprev, dtb, alog, dskip_e, gn)


def _in_proj_bwd(dz, dxbc, dcv, dcg, ddt, wz, wxbc, wcv, wcg, wdt, x, dh1, g):
    T = x.shape[0]
    tm = min(256, T)

    def body(dz_ref, dx_ref, dcv_ref, dcg_ref, ddt_ref, wz_ref, wx_ref, wcv_ref, wcg_ref, wdt_ref, x_ref, dh_ref, g_ref,
             gx_ref, dg_ref):
        @pl.when(pl.program_id(0) == 0)
        def _():
            dg_ref[...] = jnp.zeros_like(dg_ref)

        du = (_dot_nt(dz_ref[...], wz_ref[...]) + _dot_nt(dx_ref[...], wx_ref[...]) + _dot_nt(dcv_ref[...], wcv_ref[...])
              + _dot_nt(dcg_ref[...], wcg_ref[...]) + _dot_nt(ddt_ref[...], wdt_ref[...]))
        dx, dg = _rms_bwd(du, x_ref[...], g_ref[...])
        dg_ref[...] += dg
        gx_ref[...] = dh_ref[...] + dx

    row = lambda n: pl.BlockSpec((tm, n), lambda i: (i, 0))
    return pl.pallas_call(
        body, name="in_proj_bwd", grid=(T // tm,),
        in_specs=[row(SSD_WIDTH), row(XBC_WIDTH), row(CONF_WIDTH), row(CONF_WIDTH), row(LANES), _full(wz.shape),
                  _full(wxbc.shape), _full(wcv.shape), _full(wcg.shape), _full(wdt.shape), row(D_MODEL), row(D_MODEL),
                  _full((1, D_MODEL))],
        out_specs=[row(D_MODEL), _full((1, D_MODEL))],
        out_shape=[jax.ShapeDtypeStruct((T, D_MODEL), f32), jax.ShapeDtypeStruct((1, D_MODEL), f32)],
        compiler_params=_cparams(("arbitrary",), VMEM_BIG),
    )(dz, dxbc, dcv, dcg, ddt, wz, wxbc, wcv, wcg, wdt, x, dh1, g)


def _weight_grad(a, g, name, square=False, slab=None, place=None, tk=512):
    T, K = a.shape
    N = g.shape[1]
    tk = min(tk, K)
    tn = 1024 if N % 1024 == 0 else min(512, N)
    tt = min(2048, T)

    def body(a_ref, g_ref, *rest):
        o_ref = rest[-1]
        acc = _dot_tn(_operand(a_ref[...]), g_ref[...].astype(bf16))
        t = pl.program_id(2)
        shaped = acc if slab is None else acc[None]

        @pl.when(t == 0)
        def _():
            o_ref[...] = shaped

        @pl.when(t > 0)
        def _():
            o_ref[...] += shaped

    def _operand(av):
        if square:
            av = av.astype(f32)
            av = av * av
        return av.astype(bf16)

    in_specs = [pl.BlockSpec((tt, tk), lambda i, j, t: (t, i)), pl.BlockSpec((tt, tn), lambda i, j, t: (t, j))]
    grid = (K // tk, N // tn, T // tt)
    params = _cparams(("parallel", "parallel", "arbitrary"), VMEM_MID)
    if slab is None:
        return pl.pallas_call(
            body, name=name, grid=grid, in_specs=in_specs,
            out_specs=pl.BlockSpec((tk, tn), lambda i, j, t: (i, j)),
            out_shape=jax.ShapeDtypeStruct((K, N), f32), compiler_params=params,
        )(a, g)
    return pl.pallas_call(
        body, name=name, grid=grid, in_specs=in_specs + [ANY],
        out_specs=pl.BlockSpec((1, tk, tn), lambda i, j, t: place(i, j)),
        out_shape=jax.ShapeDtypeStruct(slab.shape, f32), input_output_aliases={2: 0}, compiler_params=params,
    )(a, g, slab)


def _place():
    return lax.axis_index("x"), lax.axis_index("y"), lax.axis_index("c")


def _other_chips(x, y):
    return [(1 - x, y), (x, 1 - y), (1 - x, 1 - y)]


def _remote(src, dst, ssem, rsem, dev):
    return pltpu.make_async_remote_copy(src_ref=src, dst_ref=dst, send_sem=ssem, recv_sem=rsem, device_id=dev,
                                        device_id_type=MESH)


def _gather_weights(gath0, gin0, convw):
    halves = (gath0.shape[1] // 2, gin0.shape[1] // 2)

    def body(a_ref, n_ref, cw_ref, ao_ref, no_ref, cwo_ref, ssem, rsem, lsem):
        x, y, c = _place()
        me_b = 2 * x + y
        sib = (x, y, 1 - c)
        chips = _other_chips(x, y)
        loc = pltpu.make_async_copy(cw_ref, cwo_ref.at[me_b], lsem)
        loc.start()
        sends = []
        for j, (src, dst, h) in enumerate(((a_ref, ao_ref, halves[0]), (n_ref, no_ref, halves[1]))):
            mine = pl.ds(c * h, h)
            for k, (px, py) in enumerate(chips):
                s = 6 * j + k
                sends.append(_remote(src.at[me_b, mine], dst.at[me_b, mine], ssem.at[s], rsem.at[s], (px, py, c)))
        for k, (px, py) in enumerate(chips):
            sends.append(_remote(cw_ref, cwo_ref.at[me_b], ssem.at[12 + k], rsem.at[12 + k], (px, py, c)))
        for cp in sends:
            cp.start()
        for j, (src, dst, h) in enumerate(((a_ref, ao_ref, halves[0]), (n_ref, no_ref, halves[1]))):
            mine = pl.ds(c * h, h)
            for k, (px, py) in enumerate(chips):
                b = 2 * px + py
                s = 6 * j + k
                _remote(src.at[b, mine], dst.at[b, mine], ssem.at[s], rsem.at[s], (px, py, c)).wait_recv()
                fw = _remote(dst.at[b, mine], dst.at[b, mine], ssem.at[s + 3], rsem.at[s + 3], sib)
                fw.start()
                sends.append(fw)
        for k, (px, py) in enumerate(chips):
            b = 2 * px + py
            _remote(cw_ref, cwo_ref.at[b], ssem.at[12 + k], rsem.at[12 + k], (px, py, c)).wait_recv()
        for j, (src, dst, h) in enumerate(((a_ref, ao_ref, halves[0]), (n_ref, no_ref, halves[1]))):
            theirs = pl.ds((1 - c) * h, h)
            for k, (px, py) in enumerate(chips):
                b = 2 * px + py
                s = 6 * j + k + 3
                _remote(src.at[b, theirs], dst.at[b, theirs], ssem.at[s], rsem.at[s], sib).wait_recv()
        for cp in sends:
            cp.wait_send()
        loc.wait()

    return pl.pallas_call(
        body, name="gather_weights", in_specs=[ANY, ANY, ANY], out_specs=[ANY, ANY, ANY],
        out_shape=[jax.ShapeDtypeStruct(gath0.shape, bf16), jax.ShapeDtypeStruct(gin0.shape, bf16),
                   jax.ShapeDtypeStruct((N_CHIPS, CONVW_ROWS, D_MODEL), f32)],
        input_output_aliases={0: 0, 1: 1},
        scratch_shapes=[pltpu.SemaphoreType.DMA((15,)), pltpu.SemaphoreType.DMA((15,)), pltpu.SemaphoreType.DMA(())],
    )(gath0, gin0, convw)


def _swap_halves(ga, gb):
    ha, hb = ga.shape[1] // 2, gb.shape[1] // 2

    def body(a_ref, b_ref, ra_ref, rb_ref, ssem, rsem):
        x, y, c = _place()
        sib = (x, y, 1 - c)
        ca = _remote(a_ref.at[:, pl.ds((1 - c) * ha, ha), :], ra_ref, ssem.at[0], rsem.at[0], sib)
        cb = _remote(b_ref.at[:, pl.ds((1 - c) * hb, hb), :], rb_ref, ssem.at[1], rsem.at[1], sib)
        ca.start()
        cb.start()
        ca.wait()
        cb.wait()

    return pl.pallas_call(
        body, name="swap_halves", in_specs=[ANY, ANY], out_specs=[ANY, ANY],
        out_shape=[jax.ShapeDtypeStruct((N_CHIPS, ha, ga.shape[2]), f32),
                   jax.ShapeDtypeStruct((N_CHIPS, hb, gb.shape[2]), f32)],
        scratch_shapes=[pltpu.SemaphoreType.DMA((2,)), pltpu.SemaphoreType.DMA((2,))],
    )(ga, gb)


def _chip_sum(cidx, gslab, recv, name):
    half, C = recv.shape[1:]
    tr = half // 2

    def body(c_ref, g_ref, r_ref, o_ref):
        o_ref[...] = (g_ref[...] + r_ref[...]).astype(bf16)

    return pl.pallas_call(
        body, name=name,
        grid_spec=pltpu.PrefetchScalarGridSpec(
            num_scalar_prefetch=1, grid=(N_CHIPS, half // tr),
            in_specs=[pl.BlockSpec((1, tr, C), lambda b, i, c_ref: (b, c_ref[0] * (half // tr) + i, 0)),
                      pl.BlockSpec((1, tr, C), lambda b, i, c_ref: (b, i, 0))],
            out_specs=pl.BlockSpec((1, tr, C), lambda b, i, c_ref: (b, i, 0))),
        out_shape=jax.ShapeDtypeStruct((N_CHIPS, half, C), bf16),
        compiler_params=_cparams(("parallel", "parallel"), VMEM_MID),
    )(cidx, gslab, recv)


def _exchange(ha, hb, small):
    def body(ha_ref, hb_ref, sm_ref, ra_ref, rb_ref, all_ref, ssem, rsem, lsem):
        x, y, c = _place()
        me = 4 * x + 2 * y + c
        chips = _other_chips(x, y)
        loc = pltpu.make_async_copy(sm_ref, all_ref.at[me], lsem)
        loc.start()
        sends = []
        for k, (px, py) in enumerate(chips):
            sends.append(_remote(ha_ref.at[2 * px + py], ra_ref.at[k], ssem.at[k], rsem.at[k], (px, py, c)))
            sends.append(_remote(hb_ref.at[2 * px + py], rb_ref.at[k], ssem.at[3 + k], rsem.at[3 + k], (px, py, c)))
        peers = []
        for r in range(1, N_DEV):
            peer = ((1 - x) if r & 4 else x, (1 - y) if r & 2 else y, (1 - c) if r & 1 else c)
            peers.append(peer)
            sends.append(_remote(sm_ref, all_ref.at[me], ssem.at[5 + r], rsem.at[5 + r], peer))
        for cp in sends:
            cp.start()
        for k, (px, py) in enumerate(chips):
            _remote(ha_ref.at[0], ra_ref.at[k], ssem.at[k], rsem.at[k], (px, py, c)).wait_recv()
            _remote(hb_ref.at[0], rb_ref.at[k], ssem.at[3 + k], rsem.at[3 + k], (px, py, c)).wait_recv()
        for r, peer in zip(range(1, N_DEV), peers):
            pid = 4 * peer[0] + 2 * peer[1] + peer[2]
            _remote(sm_ref, all_ref.at[pid], ssem.at[5 + r], rsem.at[5 + r], peer).wait_recv()
        for cp in sends:
            cp.wait_send()
        loc.wait()

    return pl.pallas_call(
        body, name="exchange", in_specs=[ANY, ANY, ANY], out_specs=[ANY, ANY, ANY],
        out_shape=[jax.ShapeDtypeStruct((3,) + ha.shape[1:], bf16), jax.ShapeDtypeStruct((3,) + hb.shape[1:], bf16),
                   jax.ShapeDtypeStruct((N_DEV, SMALL_ROWS, D_MODEL), f32)],
        scratch_shapes=[pltpu.SemaphoreType.DMA((13,)), pltpu.SemaphoreType.DMA((13,)), pltpu.SemaphoreType.DMA(())],
    )(ha, hb, small)


def _final_sum(idx, gslab, recv_sib, recv_ici, name):
    half, C = recv_sib.shape[1:]
    tr = half // 2

    def body(i_ref, g_ref, r_ref, p_ref, o_ref):
        acc = g_ref[0] + r_ref[0]
        for k in range(3):
            acc = acc + p_ref[k].astype(f32)
        o_ref[...] = acc

    return pl.pallas_call(
        body, name=name,
        grid_spec=pltpu.PrefetchScalarGridSpec(
            num_scalar_prefetch=1, grid=(half // tr,),
            in_specs=[pl.BlockSpec((1, tr, C), lambda i, s: (s[1], s[0] * (half // tr) + i, 0)),
                      pl.BlockSpec((1, tr, C), lambda i, s: (s[1], i, 0)),
                      pl.BlockSpec((3, tr, C), lambda i, s: (0, i, 0))],
            out_specs=pl.BlockSpec((tr, C), lambda i, s: (s[0] * (half // tr) + i, 0))),
        out_shape=jax.ShapeDtypeStruct((2 * half, C), f32),
        compiler_params=_cparams(("parallel",), VMEM_MID),
    )(idx, gslab, recv_sib, recv_ici)


def _join_halves(ra, rb):
    ha, hb = ra.shape[0] // 2, rb.shape[0] // 2

    def body(a_ref, b_ref, ao_ref, bo_ref, ssem, rsem):
        x, y, c = _place()
        sib = (x, y, 1 - c)
        mine_a, theirs_a = pl.ds(c * ha, ha), pl.ds((1 - c) * ha, ha)
        mine_b, theirs_b = pl.ds(c * hb, hb), pl.ds((1 - c) * hb, hb)
        ca = _remote(a_ref.at[mine_a], ao_ref.at[mine_a], ssem.at[0], rsem.at[0], sib)
        cb = _remote(b_ref.at[mine_b], bo_ref.at[mine_b], ssem.at[1], rsem.at[1], sib)
        ca.start()
        cb.start()
        _remote(a_ref.at[theirs_a], ao_ref.at[theirs_a], ssem.at[0], rsem.at[0], sib).wait_recv()
        _remote(b_ref.at[theirs_b], bo_ref.at[theirs_b], ssem.at[1], rsem.at[1], sib).wait_recv()
        ca.wait_send()
        cb.wait_send()

    return pl.pallas_call(
        body, name="join_halves", in_specs=[ANY, ANY], out_specs=[ANY, ANY],
        out_shape=[jax.ShapeDtypeStruct(ra.shape, f32), jax.ShapeDtypeStruct(rb.shape, f32)],
        input_output_aliases={0: 0, 1: 1},
        scratch_shapes=[pltpu.SemaphoreType.DMA((2,)), pltpu.SemaphoreType.DMA((2,))],
    )(ra, rb)


def _sum_small(all_small):
    def body(a_ref, o_ref):
        acc = a_ref[0]
        for d in range(1, N_DEV):
            acc = acc + a_ref[d]
        o_ref[...] = acc

    return pl.pallas_call(
        body, name="sum_small", out_shape=jax.ShapeDtypeStruct((SMALL_ROWS, D_MODEL), f32),
    )(all_small)


def _adamw(w, g, m, v, name, g_off=0):
    R, C = w.shape
    tr = 256 if R % 256 == 0 else R
    assert g_off % tr == 0
    c1 = 1.0 - ADAM_B1 ** ADAM_STEP
    c2 = 1.0 - ADAM_B2 ** ADAM_STEP

    def body(w_ref, g_ref, m_ref, v_ref, d_ref, mo_ref, vo_ref):
        gg = g_ref[...]
        m2 = ADAM_B1 * m_ref[...] + (1.0 - ADAM_B1) * gg
        v2 = ADAM_B2 * v_ref[...] + (1.0 - ADAM_B2) * (gg * gg)
        mo_ref[...] = m2
        vo_ref[...] = v2
        d_ref[...] = -ADAM_LR * ((m2 / c1) / (jnp.sqrt(v2 / c2) + ADAM_EPS) + ADAM_WD * w_ref[...])

    blk = pl.BlockSpec((tr, C), lambda i: (i, 0))
    gblk = pl.BlockSpec((tr, C), lambda i: (g_off // tr + i, 0))
    shp = jax.ShapeDtypeStruct((R, C), f32)
    return pl.pallas_call(
        body, name=name, grid=(R // tr,), in_specs=[blk, gblk, blk, blk], out_specs=[blk] * 3, out_shape=[shp] * 3,
        compiler_params=_cparams(("parallel",), VMEM_MID),
    )(w, g, m, v)


def _pad_lanes(v):
    return jnp.pad(v, ((0, 0), (0, LANES - v.shape[1])))


def _local_step(x, p, tgt, gath, W, S):
    wz, wxbc, wcv, wcg, wdt = W["wz"], W["wxbc"], W["wcv"], W["wcg"], W["wdt"]
    dtb = _pad_lanes(S["dt_bias"])
    alog = _pad_lanes(S["A_log"])
    dskip_e = jnp.repeat(S["D_skip"], HEAD_DIM, axis=1)

    u0, z, xbc, cv, cg, dtr, v = _in_proj_fwd(x, S["mix_norm_g"], wz, wxbc, wcv, wcg, wdt)
    pre = _ssd_conv_fwd(xbc, S["ssd_conv_w"], S["ssd_conv_b"])
    y, ys, sprev = _ssd_fwd(pre, dtr, z, dtb, alog, dskip_e, S["ssd_norm_g"])
    co, yc = _conf_fwd(v, S["conf_dw_w"], S["conf_dw_b"], S["conf_ln_g"], S["conf_ln_b"])
    h1, u1 = _out_proj_fwd(x, ys, yc, gath, S["mlp_norm_g"])
    r, h2, u2 = _mlp_fwd(h1, u1, gath, S["ple_gate_norm_g"])
    loss, dh2, dh2b, dgp, dep, dg_fin, dg_ple, db_pg, dg_pg = _ple_loss(
        h2, u2, p, tgt, gath, S["b_ple_gate"], W["w_ple"], S["ple_norm_g"], S["final_norm_g"], S["ple_gate_norm_g"])

    npg = D_MODEL // N_CHIPS
    ga = lax.empty((N_CHIPS, SLAB_A, D_MODEL), f32)
    ga = _weight_grad(u2, dgp, "dw_ple_gate", slab=ga, tk=npg, place=lambda i, j: (i, PG_OFF // npg, j))
    ga = _weight_grad(r, dh2b, "dw_down", square=True, slab=ga, place=lambda i, j: (i // 2, DOWN_OFF // 512 + i % 2, j))
    gw_ple = _weight_grad(p, dep, "dw_ple")
    dhp, dh1, dh1b, dg_mlp = _mlp_bwd(dh2, r, gath, h1, S["mlp_norm_g"])
    ga = _weight_grad(u1, dhp, "dw_up", slab=ga, place=lambda i, j: (j, UP_OFF // 512 + i, 0))
    ga = _weight_grad(ys, dh1b, "dw_out_ssd", slab=ga, place=lambda i, j: (i, OUT_OFF // 512, j))
    ga = _weight_grad(yc, dh1b, "dw_out_conf", slab=ga, place=lambda i, j: (2 + i, OUT_OFF // 512, j))
    dys, dco, dg_ln, db_ln = _out_proj_bwd(dh1, gath, co, S["conf_ln_g"], S["conf_ln_b"])
    dcv, dcg, dw_conf, db_conf = _conf_conv_bwd(dco, v, S["conf_dw_w"], cv, cg)
    dz, dpre, ddtr, dg_ssdn, dd, dal, ddtb = _ssd_bwd(dys, y, z, pre, dtr, sprev, dtb, alog, dskip_e, S["ssd_norm_g"])
    dxbc, dw_sconv, db_sconv = _ssd_conv_bwd(dpre, xbc, S["ssd_conv_w"])
    gx, dg_mix = _in_proj_bwd(dz, dxbc, dcv, dcg, ddtr, wz, wxbc, wcv, wcg, wdt, x, dh1, S["mix_norm_g"])

    gw_in = jnp.concatenate([
        _weight_grad(u0, dz, "dw_in_z"), _weight_grad(u0, dxbc, "dw_in_xbc"),
        _weight_grad(u0, ddtr, "dw_in_dt")[:, :SSD_HEADS],
        _weight_grad(u0, dcv, "dw_in_cv"), _weight_grad(u0, dcg, "dw_in_cg")], axis=1)
    small = {
        "mix_norm_g": dg_mix, "ssd_conv_w": dw_sconv[:SSD_CONV], "ssd_conv_b": db_sconv,
        "dt_bias": ddtb[:, :SSD_HEADS], "A_log": dal[:, :SSD_HEADS], "D_skip": dd[:, :SSD_HEADS],
        "ssd_norm_g": dg_ssdn, "conf_dw_w": dw_conf[:CONF_KERNEL], "conf_dw_b": db_conf,
        "conf_ln_g": dg_ln, "conf_ln_b": db_ln, "mlp_norm_g": dg_mlp, "ple_gate_norm_g": dg_pg,
        "b_ple_gate": db_pg, "ple_norm_g": dg_ple, "final_norm_g": dg_fin,
    }
    return loss, gx, ga, gw_in, gw_ple, small


def _rows(a):
    return a.reshape(-1, D_MODEL)


def _pad_rows(a, n):
    flat = a.reshape(-1)
    return jnp.pad(flat, (0, n * D_MODEL - flat.shape[0])).reshape(n, D_MODEL)


def _ple_of_slab(slab):
    return slab[PLE_OFF:PLE_OFF + PLE_ROWS].reshape(PLE_DIM, D_MODEL // N_CHIPS)


SMALL_LAYOUT = (("mix_norm_g", 1), ("ssd_norm_g", 1), ("conf_dw_b", 1), ("conf_ln_g", 1), ("conf_ln_b", 1),
                ("mlp_norm_g", 1), ("ple_gate_norm_g", 1), ("b_ple_gate", 1), ("ple_norm_g", 1), ("final_norm_g", 1),
                ("ssd_conv_b", 2), ("dt_bias", 1), ("A_log", 1), ("D_skip", 1), ("loss", 1),
                ("ssd_conv_w", 6), ("conf_dw_w", 31))


def _pack_small(d):
    parts = [_pad_rows(d[n], r) for n, r in SMALL_LAYOUT]
    used = sum(r for _, r in SMALL_LAYOUT)
    parts.append(jnp.zeros((SMALL_ROWS - used, D_MODEL), f32))
    return jnp.concatenate(parts, axis=0)


def _unpack_small(a, shapes):
    out, o = {}, 0
    for n, r in SMALL_LAYOUT:
        shp = shapes[n]
        size = 1
        for s in shp:
            size *= s
        out[n] = a[o:o + r].reshape(-1)[:size].reshape(shp)
        o += r
    return out


BIG = ("w_in", "w_out", "w_up", "w_down", "w_ple_gate", "w_ple")
BIG_A = (("w_up", UP_OFF), ("w_down", DOWN_OFF), ("w_out", OUT_OFF), ("w_ple_gate", PG_OFF))
WEIGHTS = ("mix_norm_g", "w_in", "ssd_conv_w", "ssd_conv_b", "dt_bias", "A_log", "D_skip", "ssd_norm_g", "conf_dw_w",
           "conf_dw_b", "conf_ln_g", "conf_ln_b", "w_out", "mlp_norm_g", "w_up", "w_down", "ple_gate_norm_g",
           "w_ple_gate", "b_ple_gate", "w_ple", "ple_norm_g", "final_norm_g")


def kernel(x, p, mix_norm_g, w_in, ssd_conv_w, ssd_conv_b, dt_bias, A_log, D_skip, ssd_norm_g, conf_dw_w, conf_dw_b, conf_ln_g, conf_ln_b, w_out, mlp_norm_g, w_up, w_down, ple_gate_norm_g, w_ple_gate, b_ple_gate, w_ple, ple_norm_g, final_norm_g, loss_target, m_mix_norm_g, m_w_in, m_ssd_conv_w, m_ssd_conv_b, m_dt_bias, m_A_log, m_D_skip, m_ssd_norm_g, m_conf_dw_w, m_conf_dw_b, m_conf_ln_g, m_conf_ln_b, m_w_out, m_mlp_norm_g, m_w_up, m_w_down, m_ple_gate_norm_g, m_w_ple_gate, m_b_ple_gate, m_w_ple, m_ple_norm_g, m_final_norm_g, v_mix_norm_g, v_w_in, v_ssd_conv_w, v_ssd_conv_b, v_dt_bias, v_A_log, v_D_skip, v_ssd_norm_g, v_conf_dw_w, v_conf_dw_b, v_conf_ln_g, v_conf_ln_b, v_w_out, v_mlp_norm_g, v_w_up, v_w_down, v_ple_gate_norm_g, v_w_ple_gate, v_b_ple_gate, v_w_ple, v_ple_norm_g, v_final_norm_g):
    w = dict(mix_norm_g=mix_norm_g, w_in=w_in, ssd_conv_w=ssd_conv_w, ssd_conv_b=ssd_conv_b, dt_bias=dt_bias, A_log=A_log,
             D_skip=D_skip, ssd_norm_g=ssd_norm_g, conf_dw_w=conf_dw_w, conf_dw_b=conf_dw_b, conf_ln_g=conf_ln_g,
             conf_ln_b=conf_ln_b, w_out=w_out, mlp_norm_g=mlp_norm_g, w_up=w_up, w_down=w_down,
             ple_gate_norm_g=ple_gate_norm_g, w_ple_gate=w_ple_gate, b_ple_gate=b_ple_gate, w_ple=w_ple,
             ple_norm_g=ple_norm_g, final_norm_g=final_norm_g)
    m = dict(mix_norm_g=m_mix_norm_g, w_in=m_w_in, ssd_conv_w=m_ssd_conv_w, ssd_conv_b=m_ssd_conv_b, dt_bias=m_dt_bias,
             A_log=m_A_log, D_skip=m_D_skip, ssd_norm_g=m_ssd_norm_g, conf_dw_w=m_conf_dw_w, conf_dw_b=m_conf_dw_b,
             conf_ln_g=m_conf_ln_g, conf_ln_b=m_conf_ln_b, w_out=m_w_out, mlp_norm_g=m_mlp_norm_g, w_up=m_w_up,
             w_down=m_w_down, ple_gate_norm_g=m_ple_gate_norm_g, w_ple_gate=m_w_ple_gate, b_ple_gate=m_b_ple_gate,
             w_ple=m_w_ple, ple_norm_g=m_ple_norm_g, final_norm_g=m_final_norm_g)
    v = dict(mix_norm_g=v_mix_norm_g, w_in=v_w_in, ssd_conv_w=v_ssd_conv_w, ssd_conv_b=v_ssd_conv_b, dt_bias=v_dt_bias,
             A_log=v_A_log, D_skip=v_D_skip, ssd_norm_g=v_ssd_norm_g, conf_dw_w=v_conf_dw_w, conf_dw_b=v_conf_dw_b,
             conf_ln_g=v_conf_ln_g, conf_ln_b=v_conf_ln_b, w_out=v_w_out, mlp_norm_g=v_mlp_norm_g, w_up=v_w_up,
             w_down=v_w_down, ple_gate_norm_g=v_ple_gate_norm_g, w_ple_gate=v_w_ple_gate, b_ple_gate=v_b_ple_gate,
             w_ple=v_w_ple, ple_norm_g=v_ple_norm_g, final_norm_g=v_final_norm_g)
    xi, yi, ci = lax.axis_index("x"), lax.axis_index("y"), lax.axis_index("c")
    chip = 2 * xi + yi

    slab = jnp.concatenate([w_up[0], w_down[0], w_out[0], w_ple_gate[0], _rows(w_ple[0])], axis=0).astype(bf16)
    gath0 = lax.dynamic_update_slice(jnp.zeros((N_CHIPS, SLAB_A, D_MODEL), bf16), slab[None], (chip, 0, 0))
    gin0 = lax.dynamic_update_slice(jnp.zeros((N_CHIPS,) + W_IN_SHARD, bf16), w_in.astype(bf16), (chip, 0, 0))
    convw = _pad_rows(jnp.concatenate([ssd_conv_w[0].reshape(-1), conf_dw_w[0].reshape(-1)]), CONVW_ROWS)
    gath, gin, cwg = _gather_weights(gath0, gin0, convw)
    w_in_full = jnp.concatenate([gin[b] for b in range(N_CHIPS)], axis=1)
    o_dt = SSD_WIDTH + XBC_WIDTH
    o_cv = o_dt + SSD_HEADS
    W = {
        "wz": w_in_full[:, :SSD_WIDTH], "wxbc": w_in_full[:, SSD_WIDTH:o_dt],
        "wdt": jnp.pad(w_in_full[:, o_dt:o_cv], ((0, 0), (0, LANES - SSD_HEADS))),
        "wcv": w_in_full[:, o_cv:o_cv + CONF_WIDTH], "wcg": w_in_full[:, o_cv + CONF_WIDTH:],
        "w_ple": jnp.concatenate([_ple_of_slab(gath[b]) for b in range(N_CHIPS)], axis=1),
    }
    n_sc = SSD_CONV * (XBC_WIDTH // N_CHIPS)
    n_cf = CONF_KERNEL * (CONF_WIDTH // N_CHIPS)
    S = {n: w[n][0] for n in ("mix_norm_g", "ssd_conv_b", "dt_bias", "A_log", "D_skip", "ssd_norm_g", "conf_dw_b",
                              "conf_ln_g", "conf_ln_b", "mlp_norm_g", "ple_gate_norm_g", "b_ple_gate", "ple_norm_g")}
    S = {n: a.reshape(1, -1) for n, a in S.items()}
    S["final_norm_g"] = final_norm_g.reshape(1, -1)
    S["ssd_conv_w"] = jnp.concatenate(
        [cwg[b].reshape(-1)[:n_sc].reshape(SSD_CONV, XBC_WIDTH // N_CHIPS) for b in range(N_CHIPS)], axis=1)
    S["conf_dw_w"] = jnp.concatenate(
        [cwg[b].reshape(-1)[n_sc:n_sc + n_cf].reshape(CONF_KERNEL, CONF_WIDTH // N_CHIPS) for b in range(N_CHIPS)], axis=1)

    loss8, grad_x, ga, gw_in, gw_ple, gsmall = _local_step(x[0], p[0, 0], loss_target[0], gath, W, S)

    n_in, n_ple = IN_WIDTH // N_CHIPS, D_MODEL // N_CHIPS
    gb = jnp.stack([gw_in[:, b * n_in:(b + 1) * n_in] for b in range(N_CHIPS)], axis=0)
    ple_rows = jnp.stack([_rows(gw_ple[:, b * n_ple:(b + 1) * n_ple]) for b in range(N_CHIPS)], axis=0)
    ga = lax.dynamic_update_slice(ga, ple_rows, (0, PLE_OFF, 0))
    gsmall = dict(gsmall)
    gsmall["loss"] = loss8[0:1, 0:1]
    small = _pack_small(gsmall)

    recv_a, recv_b = _swap_halves(ga, gb)
    cidx = jnp.stack([ci, chip]).astype(jnp.int32)
    ha = _chip_sum(cidx, ga, recv_a, "chip_sum_a")
    hb = _chip_sum(cidx, gb, recv_b, "chip_sum_b")
    ici_a, ici_b, all_small = _exchange(ha, hb, small)
    ra = _final_sum(cidx, ga, recv_a, ici_a, "final_sum_a")
    rb = _final_sum(cidx, gb, recv_b, ici_b, "final_sum_b")
    ra, rb = _join_halves(ra, rb)
    tot_small = _sum_small(all_small)

    shapes = {n: (tuple(w[n].shape[1:]) if n != "final_norm_g" else (D_MODEL,)) for n in WEIGHTS if n not in BIG}
    shapes["ssd_conv_w"] = (SSD_CONV, XBC_WIDTH)
    shapes["conf_dw_w"] = (CONF_KERNEL, CONF_WIDTH)
    shapes["loss"] = (1,)
    tot = _unpack_small(tot_small, shapes)
    loss = tot["loss"].reshape(())
    n1, n2 = XBC_WIDTH // N_CHIPS, CONF_WIDTH // N_CHIPS
    tot["ssd_conv_w"] = lax.dynamic_slice(tot["ssd_conv_w"], (0, chip * n1), (SSD_CONV, n1))
    tot["conf_dw_w"] = lax.dynamic_slice(tot["conf_dw_w"], (0, chip * n2), (CONF_KERNEL, n2))

    grads = {"w_ple": _ple_of_slab(ra), "w_in": rb}
    for n, off in BIG_A:
        grads[n] = ra[off:off + w[n].shape[1]]
    for n in WEIGHTS:
        if n not in BIG:
            grads[n] = tot[n]
    grads = {n: g.reshape(w[n].shape) for n, g in grads.items()}

    delta, new_m, new_v = {}, {}, {}
    for n, off in BIG_A:
        d_, m_, v_ = _adamw(w[n][0], ra, m[n][0], v[n][0], "adamw_" + n, g_off=off)
        delta[n], new_m[n], new_v[n] = d_[None], m_[None], v_[None]
    for n in ("w_in", "w_ple"):
        d_, m_, v_ = _adamw(w[n][0], grads[n][0], m[n][0], v[n][0], "adamw_" + n)
        delta[n], new_m[n], new_v[n] = d_[None], m_[None], v_[None]
    small_names = [n for n in WEIGHTS if n not in BIG]
    sizes = {n: int(w[n].size) for n in small_names}
    rows_needed = sum(-(-sizes[n] // D_MODEL) for n in small_names)
    rows_pad = -(-rows_needed // 8) * 8

    def pack(d):
        parts = [_pad_rows(d[n], -(-sizes[n] // D_MODEL)) for n in small_names]
        parts.append(jnp.zeros((rows_pad - rows_needed, D_MODEL), f32))
        return jnp.concatenate(parts, axis=0)

    sd, sm, sv = _adamw(pack(w), pack(grads), pack(m), pack(v), "adamw_small")

    def unpack(a, n, o):
        r = -(-sizes[n] // D_MODEL)
        return a[o:o + r].reshape(-1)[:sizes[n]].reshape(w[n].shape), o + r

    o = 0
    for n in small_names:
        delta[n], _ = unpack(sd, n, o)
        new_m[n], _ = unpack(sm, n, o)
        new_v[n], o = unpack(sv, n, o)

    return (loss, grad_x[None], *[grads[n] for n in WEIGHTS], *[delta[n] for n in WEIGHTS],
            *[new_m[n] for n in WEIGHTS], *[new_v[n] for n in WEIGHTS])
```

```python
import jax
import jax.numpy as jnp
from jax import lax
from jax.experimental import pallas as pl
from jax.experimental.pallas import tpu as pltpu

f32 = jnp.float32
bf16 = jnp.bfloat16

D_MODEL = 1024
SSD_WIDTH = 1024
SSD_HEADS = 16
HEAD_DIM = 64
SSD_STATE = 128
XBC_WIDTH = 1536
SSD_CONV = 4
CHUNK = 128
CONF_WIDTH = 1024
CONF_KERNEL = 31
D_FF = 4096
PLE_DIM = 256
IN_WIDTH = 4624
EPS = 1e-6
N_CHIPS = 4
N_DEV = 8

ADAM_LR = 0.001
ADAM_B1 = 0.9
ADAM_B2 = 0.999
ADAM_EPS = 1e-08
ADAM_WD = 0.01
ADAM_STEP = 10

LANES = 128
VMEM_BIG = 56 * 1024 * 1024
VMEM_MID = 40 * 1024 * 1024

UP_OFF, DOWN_OFF, OUT_OFF, PG_OFF, PLE_OFF = 0, 1024, 2048, 2560, 2816
PLE_ROWS = 64
SLAB_A = PLE_OFF + PLE_ROWS
W_IN_SHARD = (1024, 1156)
CONVW_ROWS = 16
SMALL_ROWS = 56

MESH = pl.DeviceIdType.MESH
ANY = pl.BlockSpec(memory_space=pl.ANY)


def _cparams(sem=None, vmem=None):
    return pltpu.CompilerParams(dimension_semantics=sem, vmem_limit_bytes=vmem)


def _full(shape):
    n = len(shape)
    return pl.BlockSpec(shape, lambda *_: (0,) * n)


class _Rider:
    def __init__(self, inputs, out_shapes, aliases, n_sems, start, finish):
        self.inputs, self.out_shapes, self.aliases = list(inputs), list(out_shapes), dict(aliases)
        self.n_sems, self.start, self.finish = n_sems, start, finish


def _call(body, args, *, name, grid, in_specs, out_specs, out_shape, scratch_shapes=(), params=None, rider=None):
    if rider is None:
        return pl.pallas_call(body, name=name, grid=grid, in_specs=in_specs, out_specs=out_specs, out_shape=out_shape,
                              scratch_shapes=list(scratch_shapes), compiler_params=params)(*args)
    ni, no, ns = len(in_specs), len(out_specs), len(scratch_shapes)
    ri, ro = len(rider.inputs), len(rider.out_shapes)
    (steps,) = grid

    def with_rider(*refs):
        ins, refs = refs[:ni], refs[ni:]
        rins, refs = refs[:ri], refs[ri:]
        outs, refs = refs[:no], refs[no:]
        routs, refs = refs[:ro], refs[ro:]
        scratch, (ssem, rsem) = refs[:ns], refs[ns:]
        step = pl.program_id(0)

        @pl.when(step == 0)
        def _():
            rider.start(rins, routs, ssem, rsem)

        body(*ins, *outs, *scratch)

        @pl.when(step == steps - 1)
        def _():
            rider.finish(rins, routs, ssem, rsem)

    sems = [pltpu.SemaphoreType.DMA((rider.n_sems,)), pltpu.SemaphoreType.DMA((rider.n_sems,))]
    return pl.pallas_call(
        with_rider, name=name, grid=grid, in_specs=list(in_specs) + [ANY] * ri, out_specs=list(out_specs) + [ANY] * ro,
        out_shape=list(out_shape) + rider.out_shapes, scratch_shapes=list(scratch_shapes) + sems,
        input_output_aliases={ni + a: no + b for a, b in rider.aliases.items()}, compiler_params=params,
    )(*args, *rider.inputs)


def _dot(a, b):
    return jnp.dot(a, b, preferred_element_type=f32)


def _dot_nt(a, b):
    return lax.dot_general(a, b, (((1,), (1,)), ((), ())), preferred_element_type=f32)


def _dot_tn(a, b):
    return lax.dot_general(a, b, (((0,), (0,)), ((), ())), preferred_element_type=f32)


def _sigmoid(x):
    return jax.nn.sigmoid(x)


def _rms(x, g):
    r = lax.rsqrt(jnp.mean(x * x, axis=-1, keepdims=True) + EPS)
    return x * r * g


def _rms_bwd(dy, x, g):
    r = lax.rsqrt(jnp.mean(x * x, axis=-1, keepdims=True) + EPS)
    xh = x * r
    dg = jnp.sum(dy * xh, axis=0, keepdims=True)
    dxh = dy * g
    dx = r * (dxh - xh * jnp.mean(dxh * xh, axis=-1, keepdims=True))
    return dx, dg


def _dsilu(x):
    s = _sigmoid(x)
    return s * (1.0 + x * (1.0 - s))


def _split3(x):
    hi = x.astype(bf16)
    r1 = x - hi.astype(f32)
    mid = r1.astype(bf16)
    lo = (r1 - mid.astype(f32)).astype(bf16)
    return hi, mid, lo


def _head_matrix():
    row = lax.broadcasted_iota(jnp.int32, (LANES, SSD_WIDTH), 0)
    col = lax.broadcasted_iota(jnp.int32, (LANES, SSD_WIDTH), 1)
    lo = row * HEAD_DIM
    return ((col >= lo) & (col < lo + HEAD_DIM)).astype(bf16)


def _expand(x, e):
    hi, mid, lo = _split3(x)
    return _dot(hi, e) + _dot(mid, e) + _dot(lo, e)


def _contract(x, e):
    hi, mid, lo = _split3(x)
    return _dot_nt(hi, e) + _dot_nt(mid, e) + _dot_nt(lo, e)


def _in_proj_fwd(x, g, wz, wxbc, wcv, wcg, wdt):
    T = x.shape[0]
    tm = min(256, T)

    def body(x_ref, g_ref, wz_ref, wx_ref, wcv_ref, wcg_ref, wdt_ref,
             u_ref, z_ref, xbc_ref, cv_ref, cg_ref, dt_ref, v_ref):
        ub = _rms(x_ref[...], g_ref[...]).astype(bf16)
        u_ref[...] = ub
        z_ref[...] = _dot(ub, wz_ref[...])
        xbc_ref[...] = _dot(ub, wx_ref[...])
        cv = _dot(ub, wcv_ref[...])
        cg = _dot(ub, wcg_ref[...])
        cv_ref[...] = cv
        cg_ref[...] = cg
        v_ref[...] = cv * _sigmoid(cg)
        dt_ref[...] = _dot(ub, wdt_ref[...])

    row = lambda n: pl.BlockSpec((tm, n), lambda i: (i, 0))
    return pl.pallas_call(
        body, name="in_proj_fwd", grid=(T // tm,),
        in_specs=[row(D_MODEL), _full((1, D_MODEL)), _full(wz.shape), _full(wxbc.shape), _full(wcv.shape),
                  _full(wcg.shape), _full(wdt.shape)],
        out_specs=[row(D_MODEL), row(SSD_WIDTH), row(XBC_WIDTH), row(CONF_WIDTH), row(CONF_WIDTH), row(LANES),
                   row(CONF_WIDTH)],
        out_shape=[jax.ShapeDtypeStruct((T, D_MODEL), bf16), jax.ShapeDtypeStruct((T, SSD_WIDTH), f32),
                   jax.ShapeDtypeStruct((T, XBC_WIDTH), f32), jax.ShapeDtypeStruct((T, CONF_WIDTH), f32),
                   jax.ShapeDtypeStruct((T, CONF_WIDTH), f32), jax.ShapeDtypeStruct((T, LANES), f32),
                   jax.ShapeDtypeStruct((T, CONF_WIDTH), f32)],
        compiler_params=_cparams(("parallel",), VMEM_BIG),
    )(x, g, wz, wxbc, wcv, wcg, wdt)


SUBLANES = 8


def _phases(offsets):
    return sorted({o % SUBLANES for o in offsets} - {0})


def _phase_shape(offsets, tm, C):
    a_max = max([o // SUBLANES for o in offsets if o % SUBLANES] or [0])
    return (max(len(_phases(offsets)), 1), tm + SUBLANES * a_max, C)


def _make_phases(buf_ref, ph_ref, offsets, tm):
    for idx, b in enumerate(_phases(offsets)):
        n = tm + SUBLANES * max(o // SUBLANES for o in offsets if o % SUBLANES == b)
        ph_ref[idx, 0:n, :] = buf_ref[pl.ds(b, n), :]


def _window(buf_ref, ph_ref, offsets, o, r0, rb):
    a, b = divmod(o, SUBLANES)
    if b == 0:
        return buf_ref[pl.ds(r0 + SUBLANES * a, rb), :]
    return ph_ref[_phases(offsets).index(b), pl.ds(r0 + SUBLANES * a, rb), :]


def _conv_rows(wb_ref, buf_ref, ph_ref, offsets, r0, rb):
    nsub = rb // SUBLANES
    C = wb_ref.shape[1]
    acc = None
    for k, o in enumerate(offsets):
        wk = wb_ref[pl.ds(SUBLANES * k, SUBLANES), :]
        win = _window(buf_ref, ph_ref, offsets, o, r0, rb).reshape(nsub, SUBLANES, C)
        term = wk[None] * win
        acc = term if acc is None else acc + term
    return acc.reshape(rb, C)


def _sublane_rows(w):
    return jnp.repeat(w, SUBLANES, axis=0)


def _fwd_offsets(K, hb):
    return [hb - (K - 1) + k for k in range(K)]


def _prev_halo_spec(hb, tm, C):
    return pl.BlockSpec((hb, C), lambda i: (jnp.maximum(i * (tm // hb) - 1, 0), 0))


CONV_RB = 16


def _ssd_conv_fwd(xbc, w, b):
    T, C = xbc.shape
    K, hb = SSD_CONV, 8
    tm = min(256, T)
    offs = _fwd_offsets(K, hb)

    def body(cur_ref, halo_ref, w_ref, b_ref, pre_ref, buf_ref, ph_ref):
        keep = jnp.where(pl.program_id(0) > 0, 1.0, 0.0)
        buf_ref[0:hb, :] = halo_ref[...] * keep
        buf_ref[hb:hb + tm, :] = cur_ref[...]
        _make_phases(buf_ref, ph_ref, offs, tm)

        def chunk(i, carry):
            r0 = pl.multiple_of(i * CONV_RB, CONV_RB)
            pre_ref[pl.ds(r0, CONV_RB), :] = _conv_rows(w_ref, buf_ref, ph_ref, offs, r0, CONV_RB) + b_ref[...]
            return carry

        lax.fori_loop(0, tm // CONV_RB, chunk, 0)

    return pl.pallas_call(
        body, name="ssd_conv_fwd", grid=(T // tm,),
        in_specs=[pl.BlockSpec((tm, C), lambda i: (i, 0)), _prev_halo_spec(hb, tm, C), _full((SUBLANES * K, C)),
                  _full((1, C))],
        out_specs=pl.BlockSpec((tm, C), lambda i: (i, 0)),
        out_shape=jax.ShapeDtypeStruct((T, C), f32),
        scratch_shapes=[pltpu.VMEM((hb + tm, C), f32), pltpu.VMEM(_phase_shape(offs, tm, C), f32)],
        compiler_params=_cparams(("parallel",), VMEM_MID),
    )(xbc, xbc, _sublane_rows(w), b)


def _conf_fwd(v, w, b, ln_g, ln_b, rider=None):
    T, C = v.shape
    K, hb = CONF_KERNEL, 32
    tm = min(256, T)
    offs = _fwd_offsets(K, hb)
    rb = 2 * CONV_RB

    def body(cur_ref, halo_ref, w_ref, b_ref, g_ref, bb_ref, co_ref, y_ref, buf_ref, ph_ref):
        keep = jnp.where(pl.program_id(0) > 0, 1.0, 0.0)
        buf_ref[0:hb, :] = halo_ref[...] * keep
        buf_ref[hb:hb + tm, :] = cur_ref[...]
        _make_phases(buf_ref, ph_ref, offs, tm)

        def chunk(i, carry):
            r0 = pl.multiple_of(i * rb, rb)
            co = _conv_rows(w_ref, buf_ref, ph_ref, offs, r0, rb) + b_ref[...]
            co_ref[pl.ds(r0, rb), :] = co
            mu = jnp.mean(co, axis=-1, keepdims=True)
            xc = co - mu
            yn = xc * lax.rsqrt(jnp.mean(xc * xc, axis=-1, keepdims=True) + EPS) * g_ref[...] + bb_ref[...]
            y_ref[pl.ds(r0, rb), :] = (yn * _sigmoid(yn)).astype(bf16)
            return carry

        lax.fori_loop(0, tm // rb, chunk, 0)

    return _call(
        body, (v, v, _sublane_rows(w), b, ln_g, ln_b), name="conf_fwd", grid=(T // tm,),
        in_specs=[pl.BlockSpec((tm, C), lambda i: (i, 0)), _prev_halo_spec(hb, tm, C), _full((SUBLANES * K, C)),
                  _full((1, C)), _full((1, C)), _full((1, C))],
        out_specs=[pl.BlockSpec((tm, C), lambda i: (i, 0)), pl.BlockSpec((tm, C), lambda i: (i, 0))],
        out_shape=[jax.ShapeDtypeStruct((T, C), f32), jax.ShapeDtypeStruct((T, C), bf16)],
        scratch_shapes=[pltpu.VMEM((hb + tm, C), f32), pltpu.VMEM(_phase_shape(offs, tm, C), f32)],
        params=_cparams(("arbitrary",), VMEM_MID), rider=rider)


def _ssd_chunk_common(pre, dtr, dtb, alog, e):
    act = pre * _sigmoid(pre)
    xs = act[:, :SSD_WIDTH]
    bm = act[:, SSD_WIDTH:SSD_WIDTH + 2 * SSD_STATE]
    cm = act[:, SSD_WIDTH + 2 * SSD_STATE:]
    row = lax.broadcasted_iota(jnp.int32, (CHUNK, CHUNK), 0)
    col = lax.broadcasted_iota(jnp.int32, (CHUNK, CHUNK), 1)
    tri = row >= col
    dt = jax.nn.softplus(dtr + dtb)
    a_neg = -jnp.exp(alog)
    a = dt * a_neg
    cs = jnp.dot(tri.astype(f32), a, precision=lax.Precision.HIGHEST, preferred_element_type=f32)
    cs_e = _expand(cs, e)
    dt_e = _expand(dt, e)
    csl_e = cs_e[CHUNK - 1:CHUNK, :]
    ecs_e = jnp.exp(cs_e)
    dte_e = jnp.exp(csl_e - cs_e)
    cd_e = jnp.exp(csl_e)
    xc = xs * dt_e
    xd = xc * dte_e
    return dict(xs=xs, bm=bm, cm=cm, tri=tri, dt=dt, a_neg=a_neg, cs=cs, ecs_e=ecs_e, dte_e=dte_e, cd_e=cd_e,
                dt_e=dt_e, xc=xc, xd=xd)


def _group(v, g, width):
    return v[:, g * width:(g + 1) * width]


def _ssd_fwd(pre, dtr, z, dtb, alog, dskip_e, gn):
    T = pre.shape[0]
    nc = T // CHUNK
    GW = SSD_WIDTH // 2

    def body(pre_ref, dtr_ref, z_ref, dtb_ref, alog_ref, de_ref, gn_ref, y_ref, ys_ref, sp_ref, st_ref):
        @pl.when(pl.program_id(0) == 0)
        def _():
            st_ref[...] = jnp.zeros_like(st_ref)

        e = _head_matrix()
        q = _ssd_chunk_common(pre_ref[...], dtr_ref[...], dtb_ref[...], alog_ref[...], e)
        cs, tri, xc, xd = q["cs"], q["tri"], q["xc"], q["xd"]
        cs_t = cs.T
        st = st_ref[...]
        sp_ref[0] = st
        lane = lax.broadcasted_iota(jnp.int32, (1, LANES), 1)
        halves = (lane < HEAD_DIM, lane >= HEAD_DIM)

        g_mat, y_off, s_new = [], [], []
        for g in range(2):
            bg = _group(q["bm"], g, SSD_STATE)
            cg = _group(q["cm"], g, SSD_STATE)
            bgb, cgb = bg.astype(bf16), cg.astype(bf16)
            g_mat.append(_dot_nt(cgb, bgb))
            y_off.append(_dot(cgb, _group(st, g, GW).astype(bf16)))
            s_new.append(_dot(bg.T.astype(bf16), _group(xd, g, GW).astype(bf16)))
        y_off = jnp.concatenate(y_off, axis=1) * q["ecs_e"]
        st_ref[...] = st * q["cd_e"] + jnp.concatenate(s_new, axis=1)

        pairs = []
        for j in range(SSD_HEADS // 2):
            xp = xc[:, j * LANES:(j + 1) * LANES]
            acc = jnp.zeros((CHUNK, LANES), f32)
            for hh in range(2):
                h = 2 * j + hh
                seg = cs[:, h:h + 1] - cs_t[h:h + 1, :]
                lm = jnp.exp(jnp.where(tri, seg, -1e30))
                m = (g_mat[h // 8] * lm).astype(bf16)
                acc = acc + _dot(m, jnp.where(halves[hh], xp, 0.0).astype(bf16))
            pairs.append(acc)
        y = jnp.concatenate(pairs, axis=1) + y_off + q["xs"] * de_ref[...]
        y_ref[...] = y

        zz = z_ref[...]
        v = y * (zz * _sigmoid(zz))
        outs = []
        for g in range(2):
            vg = _group(v, g, GW)
            outs.append(vg * lax.rsqrt(jnp.mean(vg * vg, axis=-1, keepdims=True) + EPS))
        ys_ref[...] = (jnp.concatenate(outs, axis=1) * gn_ref[...]).astype(bf16)

    ch = lambda n: pl.BlockSpec((CHUNK, n), lambda c: (c, 0))
    return pl.pallas_call(
        body, name="ssd_fwd", grid=(nc,),
        in_specs=[ch(XBC_WIDTH), ch(LANES), ch(SSD_WIDTH), _full((1, LANES)), _full((1, LANES)), _full((1, SSD_WIDTH)),
                  _full((1, SSD_WIDTH))],
        out_specs=[ch(SSD_WIDTH), ch(SSD_WIDTH), pl.BlockSpec((1, SSD_STATE, SSD_WIDTH), lambda c: (c, 0, 0))],
        out_shape=[jax.ShapeDtypeStruct((T, SSD_WIDTH), f32), jax.ShapeDtypeStruct((T, SSD_WIDTH), bf16),
                   jax.ShapeDtypeStruct((nc, SSD_STATE, SSD_WIDTH), f32)],
        scratch_shapes=[pltpu.VMEM((SSD_STATE, SSD_WIDTH), f32)],
        compiler_params=_cparams(("arbitrary",), VMEM_MID),
    )(pre, dtr, z, dtb, alog, dskip_e, gn)


def _w_out_spec():
    n = 2 * SSD_WIDTH // N_CHIPS
    return pl.BlockSpec((N_CHIPS, n, D_MODEL), lambda *_: (0, OUT_OFF // n, 0))


def _out_proj_fwd(x, ys, yc, gath, g):
    T = x.shape[0]
    tm = min(512, T)
    n = 2 * SSD_WIDTH // N_CHIPS

    def body(x_ref, ys_ref, yc_ref, w_ref, g_ref, h_ref, u_ref):
        h = (x_ref[...] + _dot(ys_ref[:, 0:n], w_ref[0]) + _dot(ys_ref[:, n:], w_ref[1])
             + _dot(yc_ref[:, 0:n], w_ref[2]) + _dot(yc_ref[:, n:], w_ref[3]))
        h_ref[...] = h
        u_ref[...] = _rms(h, g_ref[...]).astype(bf16)

    row = pl.BlockSpec((tm, D_MODEL), lambda i: (i, 0))
    return pl.pallas_call(
        body, name="out_proj_fwd", grid=(T // tm,),
        in_specs=[row, row, row, _w_out_spec(), _full((1, D_MODEL))],
        out_specs=[row, row],
        out_shape=[jax.ShapeDtypeStruct((T, D_MODEL), f32), jax.ShapeDtypeStruct((T, D_MODEL), bf16)],
        compiler_params=_cparams(("parallel",), VMEM_MID),
    )(x, ys, yc, gath, g)


def _w_up_spec():
    return pl.BlockSpec((1, D_MODEL, D_MODEL), lambda i, b: (b, UP_OFF // D_MODEL, 0))


def _w_down_spec():
    return pl.BlockSpec((1, D_MODEL, D_MODEL), lambda i, b: (b, DOWN_OFF // D_MODEL, 0))


def _mlp_fwd(h1, u1, gath, g_next):
    T = h1.shape[0]
    tm = min(512, T)
    nb = D_FF // D_MODEL

    def body(h_ref, u_ref, wu_ref, wd_ref, g_ref, r_ref, h2_ref, u2_ref, acc_ref):
        b = pl.program_id(1)

        @pl.when(b == 0)
        def _():
            acc_ref[...] = jnp.zeros_like(acc_ref)

        r = jnp.maximum(_dot(u_ref[...], wu_ref[0]), 0.0)
        r_ref[...] = r.astype(bf16)
        acc_ref[...] += _dot((r * r).astype(bf16), wd_ref[0])

        @pl.when(b == nb - 1)
        def _():
            h2 = h_ref[...] + acc_ref[...]
            h2_ref[...] = h2
            u2_ref[...] = _rms(h2, g_ref[...]).astype(bf16)

    row = pl.BlockSpec((tm, D_MODEL), lambda i, b: (i, 0))
    return pl.pallas_call(
        body, name="mlp_fwd", grid=(T // tm, nb),
        in_specs=[row, row, _w_up_spec(), _w_down_spec(), _full((1, D_MODEL))],
        out_specs=[pl.BlockSpec((tm, D_MODEL), lambda i, b: (i, b)), row, row],
        out_shape=[jax.ShapeDtypeStruct((T, D_FF), bf16), jax.ShapeDtypeStruct((T, D_MODEL), f32),
                   jax.ShapeDtypeStruct((T, D_MODEL), bf16)],
        scratch_shapes=[pltpu.VMEM((tm, D_MODEL), f32)],
        compiler_params=_cparams(("parallel", "arbitrary"), VMEM_MID),
    )(h1, u1, gath, gath, g_next)


def _ple_loss(h2, u2, p, tgt, gath, b_pg, w_ple, g_ple, g_fin, g_pg):
    T = h2.shape[0]
    tm = min(256, T)
    npg = D_MODEL // N_CHIPS

    def body(h2_ref, u2_ref, p_ref, t_ref, wpg_ref, bpg_ref, wple_ref, gple_ref, gfin_ref, gpg_ref,
             loss_ref, dh2_ref, dh2b_ref, dgp_ref, dep_ref, dgfin_ref, dgple_ref, dbpg_ref, dgpg_ref):
        @pl.when(pl.program_id(0) == 0)
        def _():
            loss_ref[...] = jnp.zeros_like(loss_ref)
            dgfin_ref[...] = jnp.zeros_like(dgfin_ref)
            dgple_ref[...] = jnp.zeros_like(dgple_ref)
            dbpg_ref[...] = jnp.zeros_like(dbpg_ref)
            dgpg_ref[...] = jnp.zeros_like(dgpg_ref)

        h2 = h2_ref[...]
        gate_pre = bpg_ref[...]
        for b in range(N_CHIPS):
            gate_pre = gate_pre + _dot(u2_ref[:, b * npg:(b + 1) * npg], wpg_ref[b])
        gate = _sigmoid(gate_pre)
        e_pre = _dot(p_ref[...].astype(bf16), wple_ref[...])
        emb = _rms(e_pre, gple_ref[...])
        h3 = h2 + gate * emb
        diff = _rms(h3, gfin_ref[...]) - t_ref[...]
        sq = jnp.sum(jnp.sum(diff * diff, axis=1, keepdims=True), axis=0, keepdims=True)
        loss_ref[...] += (0.5 / D_MODEL) * sq
        dh3, dgfin = _rms_bwd(diff * (1.0 / D_MODEL), h3, gfin_ref[...])
        dgfin_ref[...] += dgfin
        dgp = dh3 * emb * gate * (1.0 - gate)
        dbpg_ref[...] += jnp.sum(dgp, axis=0, keepdims=True)
        dep, dgple = _rms_bwd(dh3 * gate, e_pre, gple_ref[...])
        dgple_ref[...] += dgple
        dgpb = dgp.astype(bf16)
        dgp_ref[...] = dgpb
        dep_ref[...] = dep.astype(bf16)
        du2 = jnp.concatenate([_dot_nt(dgpb, wpg_ref[b]) for b in range(N_CHIPS)], axis=1)
        dx, dgpg = _rms_bwd(du2, h2, gpg_ref[...])
        dgpg_ref[...] += dgpg
        dh2 = dh3 + dx
        dh2_ref[...] = dh2
        dh2b_ref[...] = dh2.astype(bf16)

    row = pl.BlockSpec((tm, D_MODEL), lambda i: (i, 0))
    vec = _full((1, D_MODEL))
    vshape = jax.ShapeDtypeStruct((1, D_MODEL), f32)
    return pl.pallas_call(
        body, name="ple_loss", grid=(T // tm,),
        in_specs=[row, row, pl.BlockSpec((tm, PLE_DIM), lambda i: (i, 0)), row,
                  pl.BlockSpec((N_CHIPS, npg, D_MODEL), lambda i: (0, PG_OFF // npg, 0)), vec, _full(w_ple.shape),
                  vec, vec, vec],
        out_specs=[_full((8, LANES)), row, row, row, row, vec, vec, vec, vec],
        out_shape=[jax.ShapeDtypeStruct((8, LANES), f32), jax.ShapeDtypeStruct((T, D_MODEL), f32),
                   jax.ShapeDtypeStruct((T, D_MODEL), bf16), jax.ShapeDtypeStruct((T, D_MODEL), bf16),
                   jax.ShapeDtypeStruct((T, D_MODEL), bf16), vshape, vshape, vshape, vshape],
        compiler_params=_cparams(("arbitrary",), VMEM_MID),
    )(h2, u2, p, tgt, gath, b_pg, w_ple, g_ple, g_fin, g_pg)


def _mlp_bwd(dh2, r, gath, h1, g):
    T = dh2.shape[0]
    tm = min(512, T)
    nb = D_FF // D_MODEL

    def body(dh2_ref, r_ref, wd_ref, wu_ref, h1_ref, g_ref, dhp_ref, dh1_ref, dh1b_ref, dg_ref, acc_ref):
        i, b = pl.program_id(0), pl.program_id(1)

        @pl.when(b == 0)
        def _():
            acc_ref[...] = jnp.zeros_like(acc_ref)

        @pl.when((b == 0) & (i == 0))
        def _():
            dg_ref[...] = jnp.zeros_like(dg_ref)

        dact = _dot_nt(dh2_ref[...].astype(bf16), wd_ref[0])
        dhp = (dact * 2.0 * r_ref[...].astype(f32)).astype(bf16)
        dhp_ref[...] = dhp
        acc_ref[...] += _dot_nt(dhp, wu_ref[0])

        @pl.when(b == nb - 1)
        def _():
            dx, dg = _rms_bwd(acc_ref[...], h1_ref[...], g_ref[...])
            dg_ref[...] += dg
            dh1 = dh2_ref[...] + dx
            dh1_ref[...] = dh1
            dh1b_ref[...] = dh1.astype(bf16)

    row = pl.BlockSpec((tm, D_MODEL), lambda i, b: (i, 0))
    return pl.pallas_call(
        body, name="mlp_bwd", grid=(T // tm, nb),
        in_specs=[row, pl.BlockSpec((tm, D_MODEL), lambda i, b: (i, b)), _w_down_spec(), _w_up_spec(), row,
                  _full((1, D_MODEL))],
        out_specs=[pl.BlockSpec((tm, D_MODEL), lambda i, b: (i, b)), row, row, _full((1, D_MODEL))],
        out_shape=[jax.ShapeDtypeStruct((T, D_FF), bf16), jax.ShapeDtypeStruct((T, D_MODEL), f32),
                   jax.ShapeDtypeStruct((T, D_MODEL), bf16), jax.ShapeDtypeStruct((1, D_MODEL), f32)],
        scratch_shapes=[pltpu.VMEM((tm, D_MODEL), f32)],
        compiler_params=_cparams(("arbitrary", "arbitrary"), VMEM_MID),
    )(dh2, r, gath, gath, h1, g)


def _out_proj_bwd(dh1, gath, co, ln_g, ln_b, rider=None):
    T = dh1.shape[0]
    tm = min(512, T)

    def body(dh_ref, w_ref, co_ref, g_ref, b_ref, dys_ref, dco_ref, dg_ref, db_ref):
        @pl.when(pl.program_id(0) == 0)
        def _():
            dg_ref[...] = jnp.zeros_like(dg_ref)
            db_ref[...] = jnp.zeros_like(db_ref)

        dhb = dh_ref[...].astype(bf16)
        dys_ref[...] = jnp.concatenate([_dot_nt(dhb, w_ref[0]), _dot_nt(dhb, w_ref[1])], axis=1)
        dyc = jnp.concatenate([_dot_nt(dhb, w_ref[2]), _dot_nt(dhb, w_ref[3])], axis=1)
        co = co_ref[...]
        mu = jnp.mean(co, axis=-1, keepdims=True)
        xc = co - mu
        rstd = lax.rsqrt(jnp.mean(xc * xc, axis=-1, keepdims=True) + EPS)
        xh = xc * rstd
        yn = xh * g_ref[...] + b_ref[...]
        dyn = dyc * _dsilu(yn)
        dg_ref[...] += jnp.sum(dyn * xh, axis=0, keepdims=True)
        db_ref[...] += jnp.sum(dyn, axis=0, keepdims=True)
        dxh = dyn * g_ref[...]
        dco_ref[...] = rstd * (dxh - jnp.mean(dxh, axis=-1, keepdims=True)
                               - xh * jnp.mean(dxh * xh, axis=-1, keepdims=True))

    row = pl.BlockSpec((tm, D_MODEL), lambda i: (i, 0))
    vec = _full((1, CONF_WIDTH))
    vshape = jax.ShapeDtypeStruct((1, CONF_WIDTH), f32)
    return _call(
        body, (dh1, gath, co, ln_g, ln_b), name="out_proj_bwd", grid=(T // tm,),
        in_specs=[row, _w_out_spec(), row, vec, vec],
        out_specs=[row, row, vec, vec],
        out_shape=[jax.ShapeDtypeStruct((T, SSD_WIDTH), f32), jax.ShapeDtypeStruct((T, CONF_WIDTH), f32), vshape, vshape],
        params=_cparams(("arbitrary",), VMEM_MID), rider=rider)


def _bwd_offsets(K):
    return [K - 1 - k for k in range(K)]


def _next_halo_spec(hb, tm, C, T):
    return pl.BlockSpec((hb, C), lambda i: (jnp.minimum((i + 1) * (tm // hb), T // hb - 1), 0))


DW_RB = 8
DW_UNROLL = 4
DW_ACC_VREGS = 32


def _conv_dw(dw_ref, bufd_ref, bufx_ref, phx_ref, offs_x, tm, C):
    K = len(offs_x)
    group = max(1, DW_ACC_VREGS // (C // LANES))
    for k0 in range(0, K, group):
        ks = list(range(k0, min(k0 + group, K)))

        def step(i, accs, ks=ks):
            for u in range(DW_UNROLL):
                r0 = pl.multiple_of((i * DW_UNROLL + u) * DW_RB, DW_RB)
                d = bufd_ref[pl.ds(r0, DW_RB), :]
                accs = tuple(acc + _window(bufx_ref, phx_ref, offs_x, offs_x[k], r0, DW_RB) * d
                             for k, acc in zip(ks, accs))
            return accs

        accs = lax.fori_loop(0, tm // (DW_RB * DW_UNROLL), step, tuple(jnp.zeros((DW_RB, C), f32) for _ in ks))
        for k, acc in zip(ks, accs):
            dw_ref[k:k + 1, :] += jnp.sum(acc, axis=0, keepdims=True)


def _fill_bwd_buffers(dcur_ref, dnext_ref, xcur_ref, xprev_ref, bufd_ref, bufx_ref, phd_ref, phx_ref, offs_d, offs_x,
                      hb, tm, first, last):
    bufd_ref[0:tm, :] = dcur_ref[...]
    bufd_ref[tm:tm + hb, :] = dnext_ref[...] * jnp.where(last, 0.0, 1.0)
    bufx_ref[0:hb, :] = xprev_ref[...] * jnp.where(first, 0.0, 1.0)
    bufx_ref[hb:hb + tm, :] = xcur_ref[...]
    _make_phases(bufd_ref, phd_ref, offs_d, tm)
    _make_phases(bufx_ref, phx_ref, offs_x, tm)


def _ssd_conv_bwd(dpre, xbc, w):
    T, C = xbc.shape
    K, hb = SSD_CONV, 8
    tm = min(256, T)
    nt = T // tm
    offs_d, offs_x = _bwd_offsets(K), _fwd_offsets(K, hb)

    def body(dcur_ref, dnext_ref, xcur_ref, xprev_ref, w_ref, dx_ref, dw_ref, db_ref, bufd_ref, bufx_ref, phd_ref, phx_ref):
        i = pl.program_id(0)

        @pl.when(i == 0)
        def _():
            dw_ref[...] = jnp.zeros_like(dw_ref)
            db_ref[...] = jnp.zeros_like(db_ref)

        _fill_bwd_buffers(dcur_ref, dnext_ref, xcur_ref, xprev_ref, bufd_ref, bufx_ref, phd_ref, phx_ref, offs_d, offs_x,
                          hb, tm, i == 0, i == nt - 1)

        def chunk(j, carry):
            r0 = pl.multiple_of(j * CONV_RB, CONV_RB)
            dx_ref[pl.ds(r0, CONV_RB), :] = _conv_rows(w_ref, bufd_ref, phd_ref, offs_d, r0, CONV_RB).astype(bf16)
            return carry

        lax.fori_loop(0, tm // CONV_RB, chunk, 0)
        _conv_dw(dw_ref, bufd_ref, bufx_ref, phx_ref, offs_x, tm, C)
        db_ref[...] += jnp.sum(dcur_ref[...], axis=0, keepdims=True)

    row = pl.BlockSpec((tm, C), lambda i: (i, 0))
    return pl.pallas_call(
        body, name="ssd_conv_bwd", grid=(nt,),
        in_specs=[row, _next_halo_spec(hb, tm, C, T), row, _prev_halo_spec(hb, tm, C), _full((SUBLANES * K, C))],
        out_specs=[row, _full((8, C)), _full((1, C))],
        out_shape=[jax.ShapeDtypeStruct((T, C), bf16), jax.ShapeDtypeStruct((8, C), f32), jax.ShapeDtypeStruct((1, C), f32)],
        scratch_shapes=[pltpu.VMEM((tm + hb, C), f32), pltpu.VMEM((hb + tm, C), f32),
                        pltpu.VMEM(_phase_shape(offs_d, tm, C), f32),
                        pltpu.VMEM(_phase_shape(offs_x, tm, C), f32)],
        compiler_params=_cparams(("arbitrary",), VMEM_BIG),
    )(dpre, dpre, xbc, xbc, _sublane_rows(w))


def _conf_conv_bwd(dco, v, w, cv, cg, rider=None):
    T, C = v.shape
    K, hb = CONF_KERNEL, 32
    tm = min(256, T)
    nt = T // tm
    offs_d, offs_x = _bwd_offsets(K), _fwd_offsets(K, hb)

    def body(dcur_ref, dnext_ref, vcur_ref, vprev_ref, w_ref, cv_ref, cg_ref, dcv_ref, dcg_ref, dw_ref, db_ref,
             bufd_ref, bufx_ref, phd_ref, phx_ref):
        i = pl.program_id(0)

        @pl.when(i == 0)
        def _():
            dw_ref[...] = jnp.zeros_like(dw_ref)
            db_ref[...] = jnp.zeros_like(db_ref)

        _fill_bwd_buffers(dcur_ref, dnext_ref, vcur_ref, vprev_ref, bufd_ref, bufx_ref, phd_ref, phx_ref, offs_d, offs_x,
                          hb, tm, i == 0, i == nt - 1)

        def chunk(j, carry):
            r0 = pl.multiple_of(j * CONV_RB, CONV_RB)
            rows = pl.ds(r0, CONV_RB)
            dv = _conv_rows(w_ref, bufd_ref, phd_ref, offs_d, r0, CONV_RB)
            s = _sigmoid(cg_ref[rows, :])
            dcv_ref[rows, :] = (dv * s).astype(bf16)
            dcg_ref[rows, :] = (dv * cv_ref[rows, :] * s * (1.0 - s)).astype(bf16)
            return carry

        lax.fori_loop(0, tm // CONV_RB, chunk, 0)
        _conv_dw(dw_ref, bufd_ref, bufx_ref, phx_ref, offs_x, tm, C)
        db_ref[...] += jnp.sum(dcur_ref[...], axis=0, keepdims=True)

    row = pl.BlockSpec((tm, C), lambda i: (i, 0))
    return _call(
        body, (dco, dco, v, v, _sublane_rows(w), cv, cg), name="conf_conv_bwd", grid=(nt,),
        in_specs=[row, _next_halo_spec(hb, tm, C, T), row, _prev_halo_spec(hb, tm, C), _full((SUBLANES * K, C)), row, row],
        out_specs=[row, row, _full((32, C)), _full((1, C))],
        out_shape=[jax.ShapeDtypeStruct((T, C), bf16), jax.ShapeDtypeStruct((T, C), bf16),
                   jax.ShapeDtypeStruct((32, C), f32), jax.ShapeDtypeStruct((1, C), f32)],
        scratch_shapes=[pltpu.VMEM((tm + hb, C), f32), pltpu.VMEM((hb + tm, C), f32),
                        pltpu.VMEM(_phase_shape(offs_d, tm, C), f32),
                        pltpu.VMEM(_phase_shape(offs_x, tm, C), f32)],
        params=_cparams(("arbitrary",), VMEM_BIG), rider=rider)


def _ssd_bwd(dys, y, z, pre, dtr, sprev, dtb, alog, dskip_e, gn):
    T = pre.shape[0]
    nc = T // CHUNK
    GW = SSD_WIDTH // 2

    def body(dys_ref, y_ref, z_ref, pre_ref, dtr_ref, sp_ref, dtb_ref, alog_ref, de_ref, gn_ref,
             dz_ref, dpre_ref, ddtr_ref, dgn_ref, dd_ref, dal_ref, ddtb_ref, ds_ref):
        @pl.when(pl.program_id(0) == 0)
        def _():
            ds_ref[...] = jnp.zeros_like(ds_ref)
            dgn_ref[...] = jnp.zeros_like(dgn_ref)
            dd_ref[...] = jnp.zeros_like(dd_ref)
            dal_ref[...] = jnp.zeros_like(dal_ref)
            ddtb_ref[...] = jnp.zeros_like(ddtb_ref)

        e = _head_matrix()
        pre = pre_ref[...]
        dtr_b = dtr_ref[...] + dtb_ref[...]
        q = _ssd_chunk_common(pre, dtr_ref[...], dtb_ref[...], alog_ref[...], e)
        cs, tri, xc, xd, xs, dt = q["cs"], q["tri"], q["xc"], q["xd"], q["xs"], q["dt"]
        cs_t = cs.T
        st = sp_ref[0]
        dsn = ds_ref[...]
        lane = lax.broadcasted_iota(jnp.int32, (1, LANES), 1)
        halves = (lane < HEAD_DIM, lane >= HEAD_DIM)
        row_i = lax.broadcasted_iota(jnp.int32, (CHUNK, CHUNK), 0)
        col_i = lax.broadcasted_iota(jnp.int32, (CHUNK, CHUNK), 1)
        tri_t = col_i >= row_i

        y = y_ref[...]
        zz = z_ref[...]
        sz = _sigmoid(zz)
        silu_z = zz * sz
        v = y * silu_z
        dout = dys_ref[...]
        gn_v = gn_ref[...]
        dv, vh = [], []
        for g in range(2):
            vg = _group(v, g, GW)
            rstd = lax.rsqrt(jnp.mean(vg * vg, axis=-1, keepdims=True) + EPS)
            vhg = vg * rstd
            dvh = _group(dout, g, GW) * _group(gn_v, g, GW)
            dv.append(rstd * (dvh - vhg * jnp.mean(dvh * vhg, axis=-1, keepdims=True)))
            vh.append(vhg)
        dv = jnp.concatenate(dv, axis=1)
        dgn_ref[...] += jnp.sum(dout * jnp.concatenate(vh, axis=1), axis=0, keepdims=True)
        dy = dv * silu_z
        dz_ref[...] = (dv * y * (sz * (1.0 + zz * (1.0 - sz)))).astype(bf16)

        dd_row = jnp.sum(dy * xs, axis=0, keepdims=True)
        dd_ref[...] += _contract(jnp.broadcast_to(dd_row, (8, SSD_WIDTH)), e)[0:1, :]
        dxs = dy * de_ref[...]

        dz_in = dy * q["ecs_e"]
        g_mat, gt_mat, dcm, dbm, dsp, dxd, y_off = [], [], [], [], [], [], []
        bgs, cgs = [], []
        for g in range(2):
            bg = _group(q["bm"], g, SSD_STATE)
            cg = _group(q["cm"], g, SSD_STATE)
            bgb, cgb = bg.astype(bf16), cg.astype(bf16)
            bgs.append(bgb)
            cgs.append(cgb)
            stg = _group(st, g, GW).astype(bf16)
            dsng = _group(dsn, g, GW).astype(bf16)
            dzg = _group(dz_in, g, GW).astype(bf16)
            g_mat.append(_dot_nt(cgb, bgb))
            gt_mat.append(_dot_nt(bgb, cgb))
            y_off.append(_dot(cgb, stg))
            dcm.append(_dot_nt(dzg, stg))
            dsp.append(_dot(cg.T.astype(bf16), dzg))
            dbm.append(_dot_nt(_group(xd, g, GW).astype(bf16), dsng))
            dxd.append(_dot(bgb, dsng))
        y_off = jnp.concatenate(y_off, axis=1) * q["ecs_e"]
        dxd = jnp.concatenate(dxd, axis=1)
        ds_ref[...] = dsn * q["cd_e"] + jnp.concatenate(dsp, axis=1)
        dcd_row = jnp.sum(dsn * st, axis=0, keepdims=True) * q["cd_e"]
        t_e = dxd * xd
        dcs = _contract(dy * y_off - t_e, e)
        last_row = _contract(jnp.broadcast_to(dcd_row + jnp.sum(t_e, axis=0, keepdims=True), (8, SSD_WIDTH)), e)[0:1, :]
        dxc_state = dxd * q["dte_e"]

        dg_acc = [jnp.zeros((CHUNK, CHUNK), f32), jnp.zeros((CHUNK, CHUNK), f32)]
        dgt_acc = [jnp.zeros((CHUNK, CHUNK), f32), jnp.zeros((CHUNK, CHUNK), f32)]
        dxc_pairs = []
        for j in range(SSD_HEADS // 2):
            dyp_f = dy[:, j * LANES:(j + 1) * LANES]
            xcp_f = xc[:, j * LANES:(j + 1) * LANES]
            acc = jnp.zeros((CHUNK, LANES), f32)
            for hh in range(2):
                h = 2 * j + hh
                g = h // 8
                dyp = jnp.where(halves[hh], dyp_f, 0.0).astype(bf16)
                xcp = jnp.where(halves[hh], xcp_f, 0.0).astype(bf16)
                lm = jnp.exp(jnp.where(tri, cs[:, h:h + 1] - cs_t[h:h + 1, :], -1e30))
                lm_t = jnp.exp(jnp.where(tri_t, cs_t[h:h + 1, :] - cs[:, h:h + 1], -1e30))
                dm = _dot_nt(dyp, xcp) * lm
                dm_t = _dot_nt(xcp, dyp) * lm_t
                acc = acc + _dot((gt_mat[g] * lm_t).astype(bf16), dyp)
                dg_acc[g] = dg_acc[g] + dm
                dgt_acc[g] = dgt_acc[g] + dm_t
                qd = jnp.sum(dm * g_mat[g] - dm_t * gt_mat[g], axis=1, keepdims=True)
                dcs = dcs + qd * (lane == h).astype(f32)
            dxc_pairs.append(acc)
        dxc = jnp.concatenate(dxc_pairs, axis=1) + dxc_state
        for g in range(2):
            dcm[g] = dcm[g] + _dot(dg_acc[g].astype(bf16), bgs[g])
            dbm[g] = dbm[g] + _dot(dgt_acc[g].astype(bf16), cgs[g])

        dxs = dxs + dxc * q["dt_e"]
        ddt = _contract(dxc * xs, e)
        dcs = dcs + jnp.where(row_i == CHUNK - 1, jnp.broadcast_to(last_row, (CHUNK, LANES)), 0.0)
        da = jnp.dot(tri_t.astype(f32), dcs, precision=lax.Precision.HIGHEST, preferred_element_type=f32)
        ddt = ddt + da * q["a_neg"]
        dal_ref[...] += jnp.sum(da * dt, axis=0, keepdims=True) * q["a_neg"]
        ddtr = ddt * _sigmoid(dtr_b) * (lane < SSD_HEADS).astype(f32)
        ddtb_ref[...] += jnp.sum(ddtr, axis=0, keepdims=True)
        ddtr_ref[...] = ddtr.astype(bf16)

        dact = jnp.concatenate([dxs, dbm[0], dbm[1], dcm[0], dcm[1]], axis=1)
        dpre_ref[...] = dact * _dsilu(pre)

    rev = lambda n: pl.BlockSpec((CHUNK, n), lambda c: (nc - 1 - c, 0))
    vec = _full((1, LANES))
    vshape = jax.ShapeDtypeStruct((1, LANES), f32)
    return pl.pallas_call(
        body, name="ssd_bwd", grid=(nc,),
        in_specs=[rev(SSD_WIDTH), rev(SSD_WIDTH), rev(SSD_WIDTH), rev(XBC_WIDTH), rev(LANES),
                  pl.BlockSpec((1, SSD_STATE, SSD_WIDTH), lambda c: (nc - 1 - c, 0, 0)),
                  vec, vec, _full((1, SSD_WIDTH)), _full((1, SSD_WIDTH))],
        out_specs=[rev(SSD_WIDTH), rev(XBC_WIDTH), rev(LANES), _full((1, SSD_WIDTH)), vec, vec, vec],
        out_shape=[jax.ShapeDtypeStruct((T, SSD_WIDTH), bf16), jax.ShapeDtypeStruct((T, XBC_WIDTH), f32),
                   jax.ShapeDtypeStruct((T, LANES), bf16), jax.ShapeDtypeStruct((1, SSD_WIDTH), f32),
                   vshape, vshape, vshape],
        scratch_shapes=[pltpu.VMEM((SSD_STATE, SSD_WIDTH), f32)],
        compiler_params=_cparams(("arbitrary",), VMEM_MID),
    )(dys, y, z, pre, dtr, sprev, dtb, alog, dskip_e, gn)


def _in_proj_bwd(dz, dxbc, dcv, dcg, ddt, wz, wxbc, wcv, wcg, wdt, x, dh1, g):
    T = x.shape[0]
    tm = min(256, T)

    def body(dz_ref, dx_ref, dcv_ref, dcg_ref, ddt_ref, wz_ref, wx_ref, wcv_ref, wcg_ref, wdt_ref, x_ref, dh_ref, g_ref,
             gx_ref, dg_ref):
        @pl.when(pl.program_id(0) == 0)
        def _():
            dg_ref[...] = jnp.zeros_like(dg_ref)

        du = (_dot_nt(dz_ref[...], wz_ref[...]) + _dot_nt(dx_ref[...], wx_ref[...]) + _dot_nt(dcv_ref[...], wcv_ref[...])
              + _dot_nt(dcg_ref[...], wcg_ref[...]) + _dot_nt(ddt_ref[...], wdt_ref[...]))
        dx, dg = _rms_bwd(du, x_ref[...], g_ref[...])
        dg_ref[...] += dg
        gx_ref[...] = dh_ref[...] + dx

    row = lambda n: pl.BlockSpec((tm, n), lambda i: (i, 0))
    return pl.pallas_call(
        body, name="in_proj_bwd", grid=(T // tm,),
        in_specs=[row(SSD_WIDTH), row(XBC_WIDTH), row(CONF_WIDTH), row(CONF_WIDTH), row(LANES), _full(wz.shape),
                  _full(wxbc.shape), _full(wcv.shape), _full(wcg.shape), _full(wdt.shape), row(D_MODEL), row(D_MODEL),
                  _full((1, D_MODEL))],
        out_specs=[row(D_MODEL), _full((1, D_MODEL))],
        out_shape=[jax.ShapeDtypeStruct((T, D_MODEL), f32), jax.ShapeDtypeStruct((1, D_MODEL), f32)],
        compiler_params=_cparams(("arbitrary",), VMEM_BIG),
    )(dz, dxbc, dcv, dcg, ddt, wz, wxbc, wcv, wcg, wdt, x, dh1, g)


def _weight_grad(a, g, name, square=False, slab=None, place=None, tk=512):
    T, K = a.shape
    N = g.shape[1]
    tk = min(tk, K)
    tn = 1024 if N % 1024 == 0 else min(512, N)
    tt = min(2048, T)

    def body(a_ref, g_ref, *rest):
        o_ref = rest[-1]
        acc = _dot_tn(_operand(a_ref[...]), g_ref[...].astype(bf16))
        t = pl.program_id(2)
        shaped = acc if slab is None else acc[None]

        @pl.when(t == 0)
        def _():
            o_ref[...] = shaped

        @pl.when(t > 0)
        def _():
            o_ref[...] += shaped

    def _operand(av):
        if square:
            av = av.astype(f32)
            av = av * av
        return av.astype(bf16)

    in_specs = [pl.BlockSpec((tt, tk), lambda i, j, t: (t, i)), pl.BlockSpec((tt, tn), lambda i, j, t: (t, j))]
    grid = (K // tk, N // tn, T // tt)
    params = _cparams(("parallel", "parallel", "arbitrary"), VMEM_MID)
    if slab is None:
        return pl.pallas_call(
            body, name=name, grid=grid, in_specs=in_specs,
            out_specs=pl.BlockSpec((tk, tn), lambda i, j, t: (i, j)),
            out_shape=jax.ShapeDtypeStruct((K, N), f32), compiler_params=params,
        )(a, g)
    return pl.pallas_call(
        body, name=name, grid=grid, in_specs=in_specs + [ANY],
        out_specs=pl.BlockSpec((1, tk, tn), lambda i, j, t: place(i, j)),
        out_shape=jax.ShapeDtypeStruct(slab.shape, f32), input_output_aliases={2: 0}, compiler_params=params,
    )(a, g, slab)


def _place():
    return lax.axis_index("x"), lax.axis_index("y"), lax.axis_index("c")


def _other_chips(x, y):
    return [(1 - x, y), (x, 1 - y), (1 - x, 1 - y)]


def _remote(src, dst, ssem, rsem, dev):
    return pltpu.make_async_remote_copy(src_ref=src, dst_ref=dst, send_sem=ssem, recv_sem=rsem, device_id=dev,
                                        device_id_type=MESH)


def _gather_weights(arrays, convw):
    n = len(arrays)
    halves = tuple(a.shape[1] // 2 for a in arrays)

    def body(*refs):
        cw_ref, cwo_ref = refs[n], refs[2 * n + 1]
        ssem, rsem, lsem = refs[2 * n + 2:]
        triples = tuple(zip(refs[:n], refs[n + 1:2 * n + 1], halves))
        x, y, c = _place()
        me_b = 2 * x + y
        sib = (x, y, 1 - c)
        chips = _other_chips(x, y)
        loc = pltpu.make_async_copy(cw_ref, cwo_ref.at[me_b], lsem)
        loc.start()
        sends = []
        for j, (src, dst, h) in enumerate(triples):
            mine = pl.ds(c * h, h)
            for k, (px, py) in enumerate(chips):
                s = 6 * j + k
                sends.append(_remote(src.at[me_b, mine], dst.at[me_b, mine], ssem.at[s], rsem.at[s], (px, py, c)))
        for k, (px, py) in enumerate(chips):
            sends.append(_remote(cw_ref, cwo_ref.at[me_b], ssem.at[6 * n + k], rsem.at[6 * n + k], (px, py, c)))
        for cp in sends:
            cp.start()
        for j, (src, dst, h) in enumerate(triples):
            mine = pl.ds(c * h, h)
            for k, (px, py) in enumerate(chips):
                b = 2 * px + py
                s = 6 * j + k
                _remote(src.at[b, mine], dst.at[b, mine], ssem.at[s], rsem.at[s], (px, py, c)).wait_recv()
                fw = _remote(dst.at[b, mine], dst.at[b, mine], ssem.at[s + 3], rsem.at[s + 3], sib)
                fw.start()
                sends.append(fw)
        for k, (px, py) in enumerate(chips):
            b = 2 * px + py
            _remote(cw_ref, cwo_ref.at[b], ssem.at[6 * n + k], rsem.at[6 * n + k], (px, py, c)).wait_recv()
        for j, (src, dst, h) in enumerate(triples):
            theirs = pl.ds((1 - c) * h, h)
            for k, (px, py) in enumerate(chips):
                b = 2 * px + py
                s = 6 * j + k + 3
                _remote(src.at[b, theirs], dst.at[b, theirs], ssem.at[s], rsem.at[s], sib).wait_recv()
        for cp in sends:
            cp.wait_send()
        loc.wait()

    return pl.pallas_call(
        body, name="gather_weights", in_specs=[ANY] * (n + 1), out_specs=[ANY] * (n + 1),
        out_shape=[jax.ShapeDtypeStruct(a.shape, bf16) for a in arrays]
        + [jax.ShapeDtypeStruct((N_CHIPS, CONVW_ROWS, D_MODEL), f32)],
        input_output_aliases={j: j for j in range(n)},
        scratch_shapes=[pltpu.SemaphoreType.DMA((6 * n + 3,)), pltpu.SemaphoreType.DMA((6 * n + 3,)),
                        pltpu.SemaphoreType.DMA(())],
    )(*arrays, convw)


def _gather_rider(gath0):
    h = gath0.shape[1] // 2

    def copies(rins, routs, ssem, rsem, sending):
        (g_ref,), (o_ref,) = rins, routs
        x, y, c = _place()
        mine = pl.ds(c * h, h)
        for k, (px, py) in enumerate(_other_chips(x, y)):
            b = 2 * x + y if sending else 2 * px + py
            yield _remote(g_ref.at[b, mine], o_ref.at[b, mine], ssem.at[k], rsem.at[k], (px, py, c))

    def start(*refs):
        for cp in copies(*refs, sending=True):
            cp.start()

    def finish(*refs):
        for cp in copies(*refs, sending=False):
            cp.wait()

    return _Rider([gath0], [jax.ShapeDtypeStruct(gath0.shape, gath0.dtype)], {0: 0}, 3, start, finish)


def _forward_to_sibling(gath):
    h = gath.shape[1] // 2

    def body(g_ref, o_ref, ssem, rsem):
        x, y, c = _place()
        sib = (x, y, 1 - c)
        mine, theirs = pl.ds(c * h, h), pl.ds((1 - c) * h, h)
        blocks = [2 * px + py for px, py in _other_chips(x, y)]
        sends = [_remote(g_ref.at[b, mine], o_ref.at[b, mine], ssem.at[k], rsem.at[k], sib) for k, b in enumerate(blocks)]
        for cp in sends:
            cp.start()
        for k, b in enumerate(blocks):
            _remote(g_ref.at[b, theirs], o_ref.at[b, theirs], ssem.at[k], rsem.at[k], sib).wait_recv()
        for cp in sends:
            cp.wait_send()

    return pl.pallas_call(
        body, name="forward_to_sibling", in_specs=[ANY], out_specs=ANY,
        out_shape=jax.ShapeDtypeStruct(gath.shape, gath.dtype), input_output_aliases={0: 0},
        scratch_shapes=[pltpu.SemaphoreType.DMA((3,)), pltpu.SemaphoreType.DMA((3,))],
    )(gath)


def _swap_copy(g_ref, r_ref, ssem, rsem):
    x, y, c = _place()
    h = r_ref.shape[1]
    return _remote(g_ref.at[:, pl.ds((1 - c) * h, h), :], r_ref, ssem.at[0], rsem.at[0], (x, y, 1 - c))


def _swap_rider(g):
    def start(rins, routs, ssem, rsem):
        _swap_copy(rins[0], routs[0], ssem, rsem).start()

    def finish(rins, routs, ssem, rsem):
        _swap_copy(rins[0], routs[0], ssem, rsem).wait()

    return _Rider([g], [jax.ShapeDtypeStruct((N_CHIPS, g.shape[1] // 2, g.shape[2]), g.dtype)], {}, 1, start, finish)


def _swap_halves(g):
    def body(g_ref, r_ref, ssem, rsem):
        cp = _swap_copy(g_ref, r_ref, ssem, rsem)
        cp.start()
        cp.wait()

    return pl.pallas_call(
        body, name="swap_halves", in_specs=[ANY], out_specs=ANY,
        out_shape=jax.ShapeDtypeStruct((N_CHIPS, g.shape[1] // 2, g.shape[2]), g.dtype),
        scratch_shapes=[pltpu.SemaphoreType.DMA((1,)), pltpu.SemaphoreType.DMA((1,))],
    )(g)


def _chip_sum(cidx, gslab, recv, name):
    half, C = recv.shape[1:]
    tr = half // 2

    def body(c_ref, g_ref, r_ref, o_ref):
        o_ref[...] = (g_ref[...] + r_ref[...]).astype(bf16)

    return pl.pallas_call(
        body, name=name,
        grid_spec=pltpu.PrefetchScalarGridSpec(
            num_scalar_prefetch=1, grid=(N_CHIPS, half // tr),
            in_specs=[pl.BlockSpec((1, tr, C), lambda b, i, c_ref: (b, c_ref[0] * (half // tr) + i, 0)),
                      pl.BlockSpec((1, tr, C), lambda b, i, c_ref: (b, i, 0))],
            out_specs=pl.BlockSpec((1, tr, C), lambda b, i, c_ref: (b, i, 0))),
        out_shape=jax.ShapeDtypeStruct((N_CHIPS, half, C), bf16),
        compiler_params=_cparams(("parallel", "parallel"), VMEM_MID),
    )(cidx, gslab, recv)


def _exchange_rider(h):
    def copies(rins, routs, ssem, rsem):
        x, y, c = _place()
        for k, (px, py) in enumerate(_other_chips(x, y)):
            yield _remote(rins[0].at[2 * px + py], routs[0].at[k], ssem.at[k], rsem.at[k], (px, py, c))

    def start(*refs):
        for cp in copies(*refs):
            cp.start()

    def finish(*refs):
        for cp in copies(*refs):
            cp.wait()

    return _Rider([h], [jax.ShapeDtypeStruct((3,) + h.shape[1:], h.dtype)], {}, 3, start, finish)


def _exchange(hb, small):
    def body(hb_ref, sm_ref, rb_ref, all_ref, ssem, rsem, lsem):
        x, y, c = _place()
        me = 4 * x + 2 * y + c
        chips = _other_chips(x, y)
        loc = pltpu.make_async_copy(sm_ref, all_ref.at[me], lsem)
        loc.start()
        sends = []
        for k, (px, py) in enumerate(chips):
            sends.append(_remote(hb_ref.at[2 * px + py], rb_ref.at[k], ssem.at[3 + k], rsem.at[3 + k], (px, py, c)))
        peers = []
        for r in range(1, N_DEV):
            peer = ((1 - x) if r & 4 else x, (1 - y) if r & 2 else y, (1 - c) if r & 1 else c)
            peers.append(peer)
            sends.append(_remote(sm_ref, all_ref.at[me], ssem.at[5 + r], rsem.at[5 + r], peer))
        for cp in sends:
            cp.start()
        for k, (px, py) in enumerate(chips):
            _remote(hb_ref.at[0], rb_ref.at[k], ssem.at[3 + k], rsem.at[3 + k], (px, py, c)).wait_recv()
        for r, peer in zip(range(1, N_DEV), peers):
            pid = 4 * peer[0] + 2 * peer[1] + peer[2]
            _remote(sm_ref, all_ref.at[pid], ssem.at[5 + r], rsem.at[5 + r], peer).wait_recv()
        for cp in sends:
            cp.wait_send()
        loc.wait()

    return pl.pallas_call(
        body, name="exchange", in_specs=[ANY, ANY], out_specs=[ANY, ANY],
        out_shape=[jax.ShapeDtypeStruct((3,) + hb.shape[1:], bf16),
                   jax.ShapeDtypeStruct((N_DEV, SMALL_ROWS, D_MODEL), f32)],
        scratch_shapes=[pltpu.SemaphoreType.DMA((13,)), pltpu.SemaphoreType.DMA((13,)), pltpu.SemaphoreType.DMA(())],
    )(hb, small)


def _final_sum(idx, gslab, recv_sib, recv_ici, name):
    half, C = recv_sib.shape[1:]
    tr = half // 2

    def body(i_ref, g_ref, r_ref, p_ref, o_ref):
        acc = g_ref[0] + r_ref[0]
        for k in range(3):
            acc = acc + p_ref[k].astype(f32)
        o_ref[...] = acc

    return pl.pallas_call(
        body, name=name,
        grid_spec=pltpu.PrefetchScalarGridSpec(
            num_scalar_prefetch=1, grid=(half // tr,),
            in_specs=[pl.BlockSpec((1, tr, C), lambda i, s: (s[1], s[0] * (half // tr) + i, 0)),
                      pl.BlockSpec((1, tr, C), lambda i, s: (s[1], i, 0)),
                      pl.BlockSpec((3, tr, C), lambda i, s: (0, i, 0))],
            out_specs=pl.BlockSpec((tr, C), lambda i, s: (s[0] * (half // tr) + i, 0))),
        out_shape=jax.ShapeDtypeStruct((2 * half, C), f32),
        compiler_params=_cparams(("parallel",), VMEM_MID),
    )(idx, gslab, recv_sib, recv_ici)


def _join_halves(ra, rb):
    ha, hb = ra.shape[0] // 2, rb.shape[0] // 2

    def body(a_ref, b_ref, ao_ref, bo_ref, ssem, rsem):
        x, y, c = _place()
        sib = (x, y, 1 - c)
        mine_a, theirs_a = pl.ds(c * ha, ha), pl.ds((1 - c) * ha, ha)
        mine_b, theirs_b = pl.ds(c * hb, hb), pl.ds((1 - c) * hb, hb)
        ca = _remote(a_ref.at[mine_a], ao_ref.at[mine_a], ssem.at[0], rsem.at[0], sib)
        cb = _remote(b_ref.at[mine_b], bo_ref.at[mine_b], ssem.at[1], rsem.at[1], sib)
        ca.start()
        cb.start()
        _remote(a_ref.at[theirs_a], ao_ref.at[theirs_a], ssem.at[0], rsem.at[0], sib).wait_recv()
        _remote(b_ref.at[theirs_b], bo_ref.at[theirs_b], ssem.at[1], rsem.at[1], sib).wait_recv()
        ca.wait_send()
        cb.wait_send()

    return pl.pallas_call(
        body, name="join_halves", in_specs=[ANY, ANY], out_specs=[ANY, ANY],
        out_shape=[jax.ShapeDtypeStruct(ra.shape, f32), jax.ShapeDtypeStruct(rb.shape, f32)],
        input_output_aliases={0: 0, 1: 1},
        scratch_shapes=[pltpu.SemaphoreType.DMA((2,)), pltpu.SemaphoreType.DMA((2,))],
    )(ra, rb)


def _sum_small(all_small):
    def body(a_ref, o_ref):
        acc = a_ref[0]
        for d in range(1, N_DEV):
            acc = acc + a_ref[d]
        o_ref[...] = acc

    return pl.pallas_call(
        body, name="sum_small", out_shape=jax.ShapeDtypeStruct((SMALL_ROWS, D_MODEL), f32),
    )(all_small)


def _adamw(w, g, m, v, name, g_off=0):
    R, C = w.shape
    tr = 256 if R % 256 == 0 else R
    assert g_off % tr == 0
    c1 = 1.0 - ADAM_B1 ** ADAM_STEP
    c2 = 1.0 - ADAM_B2 ** ADAM_STEP

    def body(w_ref, g_ref, m_ref, v_ref, d_ref, mo_ref, vo_ref):
        gg = g_ref[...]
        m2 = ADAM_B1 * m_ref[...] + (1.0 - ADAM_B1) * gg
        v2 = ADAM_B2 * v_ref[...] + (1.0 - ADAM_B2) * (gg * gg)
        mo_ref[...] = m2
        vo_ref[...] = v2
        d_ref[...] = -ADAM_LR * ((m2 / c1) / (jnp.sqrt(v2 / c2) + ADAM_EPS) + ADAM_WD * w_ref[...])

    blk = pl.BlockSpec((tr, C), lambda i: (i, 0))
    gblk = pl.BlockSpec((tr, C), lambda i: (g_off // tr + i, 0))
    shp = jax.ShapeDtypeStruct((R, C), f32)
    return pl.pallas_call(
        body, name=name, grid=(R // tr,), in_specs=[blk, gblk, blk, blk], out_specs=[blk] * 3, out_shape=[shp] * 3,
        compiler_params=_cparams(("parallel",), VMEM_MID),
    )(w, g, m, v)


def _pad_lanes(v):
    return jnp.pad(v, ((0, 0), (0, LANES - v.shape[1])))


def _local_step(x, p, tgt, gath0, cidx, W, S):
    wz, wxbc, wcv, wcg, wdt = W["wz"], W["wxbc"], W["wcv"], W["wcg"], W["wdt"]
    dtb = _pad_lanes(S["dt_bias"])
    alog = _pad_lanes(S["A_log"])
    dskip_e = jnp.repeat(S["D_skip"], HEAD_DIM, axis=1)

    u0, z, xbc, cv, cg, dtr, v = _in_proj_fwd(x, S["mix_norm_g"], wz, wxbc, wcv, wcg, wdt)
    co, yc, gath = _conf_fwd(v, S["conf_dw_w"], S["conf_dw_b"], S["conf_ln_g"], S["conf_ln_b"], rider=_gather_rider(gath0))
    gath = _forward_to_sibling(gath)
    w_ple = jnp.concatenate([_ple_of_slab(gath[b]) for b in range(N_CHIPS)], axis=1)
    pre = _ssd_conv_fwd(xbc, S["ssd_conv_w"], S["ssd_conv_b"])
    y, ys, sprev = _ssd_fwd(pre, dtr, z, dtb, alog, dskip_e, S["ssd_norm_g"])
    h1, u1 = _out_proj_fwd(x, ys, yc, gath, S["mlp_norm_g"])
    r, h2, u2 = _mlp_fwd(h1, u1, gath, S["ple_gate_norm_g"])
    loss, dh2, dh2b, dgp, dep, dg_fin, dg_ple, db_pg, dg_pg = _ple_loss(
        h2, u2, p, tgt, gath, S["b_ple_gate"], w_ple, S["ple_norm_g"], S["final_norm_g"], S["ple_gate_norm_g"])

    npg = D_MODEL // N_CHIPS
    ga = lax.empty((N_CHIPS, SLAB_A, D_MODEL), f32)
    ga = _weight_grad(u2, dgp, "dw_ple_gate", slab=ga, tk=npg, place=lambda i, j: (i, PG_OFF // npg, j))
    ga = _weight_grad(r, dh2b, "dw_down", square=True, slab=ga, place=lambda i, j: (i // 2, DOWN_OFF // 512 + i % 2, j))
    gw_ple = _weight_grad(p, dep, "dw_ple")
    dhp, dh1, dh1b, dg_mlp = _mlp_bwd(dh2, r, gath, h1, S["mlp_norm_g"])
    ga = _weight_grad(u1, dhp, "dw_up", slab=ga, place=lambda i, j: (j, UP_OFF // 512 + i, 0))
    ga = _weight_grad(ys, dh1b, "dw_out_ssd", slab=ga, place=lambda i, j: (i, OUT_OFF // 512, j))
    ga = _weight_grad(yc, dh1b, "dw_out_conf", slab=ga, place=lambda i, j: (2 + i, OUT_OFF // 512, j))
    n_ple = D_MODEL // N_CHIPS
    ple_rows = jnp.stack([_rows(gw_ple[:, b * n_ple:(b + 1) * n_ple]) for b in range(N_CHIPS)], axis=0)
    ga = lax.dynamic_update_slice(ga, ple_rows, (0, PLE_OFF, 0))
    dys, dco, dg_ln, db_ln, recv_a = _out_proj_bwd(dh1, gath, co, S["conf_ln_g"], S["conf_ln_b"], rider=_swap_rider(ga))
    ha = _chip_sum(cidx, ga, recv_a, "chip_sum_a")
    dcv, dcg, dw_conf, db_conf, ici_a = _conf_conv_bwd(dco, v, S["conf_dw_w"], cv, cg, rider=_exchange_rider(ha))
    dz, dpre, ddtr, dg_ssdn, dd, dal, ddtb = _ssd_bwd(dys, y, z, pre, dtr, sprev, dtb, alog, dskip_e, S["ssd_norm_g"])
    dxbc, dw_sconv, db_sconv = _ssd_conv_bwd(dpre, xbc, S["ssd_conv_w"])
    gx, dg_mix = _in_proj_bwd(dz, dxbc, dcv, dcg, ddtr, wz, wxbc, wcv, wcg, wdt, x, dh1, S["mix_norm_g"])

    gw_in = jnp.concatenate([
        _weight_grad(u0, dz, "dw_in_z"), _weight_grad(u0, dxbc, "dw_in_xbc"),
        _weight_grad(u0, ddtr, "dw_in_dt")[:, :SSD_HEADS],
        _weight_grad(u0, dcv, "dw_in_cv"), _weight_grad(u0, dcg, "dw_in_cg")], axis=1)
    small = {
        "mix_norm_g": dg_mix, "ssd_conv_w": dw_sconv[:SSD_CONV], "ssd_conv_b": db_sconv,
        "dt_bias": ddtb[:, :SSD_HEADS], "A_log": dal[:, :SSD_HEADS], "D_skip": dd[:, :SSD_HEADS],
        "ssd_norm_g": dg_ssdn, "conf_dw_w": dw_conf[:CONF_KERNEL], "conf_dw_b": db_conf,
        "conf_ln_g": dg_ln, "conf_ln_b": db_ln, "mlp_norm_g": dg_mlp, "ple_gate_norm_g": dg_pg,
        "b_ple_gate": db_pg, "ple_norm_g": dg_ple, "final_norm_g": dg_fin,
    }
    return loss, gx, ga, recv_a, ici_a, gw_in, small


def _rows(a):
    return a.reshape(-1, D_MODEL)


def _pad_rows(a, n):
    flat = a.reshape(-1)
    return jnp.pad(flat, (0, n * D_MODEL - flat.shape[0])).reshape(n, D_MODEL)


def _ple_of_slab(slab):
    return slab[PLE_OFF:PLE_OFF + PLE_ROWS].reshape(PLE_DIM, D_MODEL // N_CHIPS)


SMALL_LAYOUT = (("mix_norm_g", 1), ("ssd_norm_g", 1), ("conf_dw_b", 1), ("conf_ln_g", 1), ("conf_ln_b", 1),
                ("mlp_norm_g", 1), ("ple_gate_norm_g", 1), ("b_ple_gate", 1), ("ple_norm_g", 1), ("final_norm_g", 1),
                ("ssd_conv_b", 2), ("dt_bias", 1), ("A_log", 1), ("D_skip", 1), ("loss", 1),
                ("ssd_conv_w", 6), ("conf_dw_w", 31))


def _pack_small(d):
    parts = [_pad_rows(d[n], r) for n, r in SMALL_LAYOUT]
    used = sum(r for _, r in SMALL_LAYOUT)
    parts.append(jnp.zeros((SMALL_ROWS - used, D_MODEL), f32))
    return jnp.concatenate(parts, axis=0)


def _unpack_small(a, shapes):
    out, o = {}, 0
    for n, r in SMALL_LAYOUT:
        shp = shapes[n]
        size = 1
        for s in shp:
            size *= s
        out[n] = a[o:o + r].reshape(-1)[:size].reshape(shp)
        o += r
    return out


BIG = ("w_in", "w_out", "w_up", "w_down", "w_ple_gate", "w_ple")
BIG_A = (("w_up", UP_OFF), ("w_down", DOWN_OFF), ("w_out", OUT_OFF), ("w_ple_gate", PG_OFF))
WEIGHTS = ("mix_norm_g", "w_in", "ssd_conv_w", "ssd_conv_b", "dt_bias", "A_log", "D_skip", "ssd_norm_g", "conf_dw_w",
           "conf_dw_b", "conf_ln_g", "conf_ln_b", "w_out", "mlp_norm_g", "w_up", "w_down", "ple_gate_norm_g",
           "w_ple_gate", "b_ple_gate", "w_ple", "ple_norm_g", "final_norm_g")


def kernel(x, p, mix_norm_g, w_in, ssd_conv_w, ssd_conv_b, dt_bias, A_log, D_skip, ssd_norm_g, conf_dw_w, conf_dw_b, conf_ln_g, conf_ln_b, w_out, mlp_norm_g, w_up, w_down, ple_gate_norm_g, w_ple_gate, b_ple_gate, w_ple, ple_norm_g, final_norm_g, loss_target, m_mix_norm_g, m_w_in, m_ssd_conv_w, m_ssd_conv_b, m_dt_bias, m_A_log, m_D_skip, m_ssd_norm_g, m_conf_dw_w, m_conf_dw_b, m_conf_ln_g, m_conf_ln_b, m_w_out, m_mlp_norm_g, m_w_up, m_w_down, m_ple_gate_norm_g, m_w_ple_gate, m_b_ple_gate, m_w_ple, m_ple_norm_g, m_final_norm_g, v_mix_norm_g, v_w_in, v_ssd_conv_w, v_ssd_conv_b, v_dt_bias, v_A_log, v_D_skip, v_ssd_norm_g, v_conf_dw_w, v_conf_dw_b, v_conf_ln_g, v_conf_ln_b, v_w_out, v_mlp_norm_g, v_w_up, v_w_down, v_ple_gate_norm_g, v_w_ple_gate, v_b_ple_gate, v_w_ple, v_ple_norm_g, v_final_norm_g):
    w = dict(mix_norm_g=mix_norm_g, w_in=w_in, ssd_conv_w=ssd_conv_w, ssd_conv_b=ssd_conv_b, dt_bias=dt_bias, A_log=A_log,
             D_skip=D_skip, ssd_norm_g=ssd_norm_g, conf_dw_w=conf_dw_w, conf_dw_b=conf_dw_b, conf_ln_g=conf_ln_g,
             conf_ln_b=conf_ln_b, w_out=w_out, mlp_norm_g=mlp_norm_g, w_up=w_up, w_down=w_down,
             ple_gate_norm_g=ple_gate_norm_g, w_ple_gate=w_ple_gate, b_ple_gate=b_ple_gate, w_ple=w_ple,
             ple_norm_g=ple_norm_g, final_norm_g=final_norm_g)
    m = dict(mix_norm_g=m_mix_norm_g, w_in=m_w_in, ssd_conv_w=m_ssd_conv_w, ssd_conv_b=m_ssd_conv_b, dt_bias=m_dt_bias,
             A_log=m_A_log, D_skip=m_D_skip, ssd_norm_g=m_ssd_norm_g, conf_dw_w=m_conf_dw_w, conf_dw_b=m_conf_dw_b,
             conf_ln_g=m_conf_ln_g, conf_ln_b=m_conf_ln_b, w_out=m_w_out, mlp_norm_g=m_mlp_norm_g, w_up=m_w_up,
             w_down=m_w_down, ple_gate_norm_g=m_ple_gate_norm_g, w_ple_gate=m_w_ple_gate, b_ple_gate=m_b_ple_gate,
             w_ple=m_w_ple, ple_norm_g=m_ple_norm_g, final_norm_g=m_final_norm_g)
    v = dict(mix_norm_g=v_mix_norm_g, w_in=v_w_in, ssd_conv_w=v_ssd_conv_w, ssd_conv_b=v_ssd_conv_b, dt_bias=v_dt_bias,
             A_log=v_A_log, D_skip=v_D_skip, ssd_norm_g=v_ssd_norm_g, conf_dw_w=v_conf_dw_w, conf_dw_b=v_conf_dw_b,
             conf_ln_g=v_conf_ln_g, conf_ln_b=v_conf_ln_b, w_out=v_w_out, mlp_norm_g=v_mlp_norm_g, w_up=v_w_up,
             w_down=v_w_down, ple_gate_norm_g=v_ple_gate_norm_g, w_ple_gate=v_w_ple_gate, b_ple_gate=v_b_ple_gate,
             w_ple=v_w_ple, ple_norm_g=v_ple_norm_g, final_norm_g=v_final_norm_g)
    xi, yi, ci = lax.axis_index("x"), lax.axis_index("y"), lax.axis_index("c")
    chip = 2 * xi + yi

    slab = jnp.concatenate([w_up[0], w_down[0], w_out[0], w_ple_gate[0], _rows(w_ple[0])], axis=0).astype(bf16)
    gath0 = lax.dynamic_update_slice(jnp.zeros((N_CHIPS, SLAB_A, D_MODEL), bf16), slab[None], (chip, 0, 0))
    gin0 = lax.dynamic_update_slice(jnp.zeros((N_CHIPS,) + W_IN_SHARD, bf16), w_in.astype(bf16), (chip, 0, 0))
    convw = _pad_rows(jnp.concatenate([ssd_conv_w[0].reshape(-1), conf_dw_w[0].reshape(-1)]), CONVW_ROWS)
    gin, cwg = _gather_weights([gin0], convw)
    w_in_full = jnp.concatenate([gin[b] for b in range(N_CHIPS)], axis=1)
    o_dt = SSD_WIDTH + XBC_WIDTH
    o_cv = o_dt + SSD_HEADS
    W = {
        "wz": w_in_full[:, :SSD_WIDTH], "wxbc": w_in_full[:, SSD_WIDTH:o_dt],
        "wdt": jnp.pad(w_in_full[:, o_dt:o_cv], ((0, 0), (0, LANES - SSD_HEADS))),
        "wcv": w_in_full[:, o_cv:o_cv + CONF_WIDTH], "wcg": w_in_full[:, o_cv + CONF_WIDTH:],
    }
    n_sc = SSD_CONV * (XBC_WIDTH // N_CHIPS)
    n_cf = CONF_KERNEL * (CONF_WIDTH // N_CHIPS)
    S = {n: w[n][0] for n in ("mix_norm_g", "ssd_conv_b", "dt_bias", "A_log", "D_skip", "ssd_norm_g", "conf_dw_b",
                              "conf_ln_g", "conf_ln_b", "mlp_norm_g", "ple_gate_norm_g", "b_ple_gate", "ple_norm_g")}
    S = {n: a.reshape(1, -1) for n, a in S.items()}
    S["final_norm_g"] = final_norm_g.reshape(1, -1)
    S["ssd_conv_w"] = jnp.concatenate(
        [cwg[b].reshape(-1)[:n_sc].reshape(SSD_CONV, XBC_WIDTH // N_CHIPS) for b in range(N_CHIPS)], axis=1)
    S["conf_dw_w"] = jnp.concatenate(
        [cwg[b].reshape(-1)[n_sc:n_sc + n_cf].reshape(CONF_KERNEL, CONF_WIDTH // N_CHIPS) for b in range(N_CHIPS)], axis=1)

    cidx = jnp.stack([ci, chip]).astype(jnp.int32)
    loss8, grad_x, ga, recv_a, ici_a, gw_in, gsmall = _local_step(x[0], p[0, 0], loss_target[0], gath0, cidx, W, S)

    n_in = IN_WIDTH // N_CHIPS
    gb = jnp.stack([gw_in[:, b * n_in:(b + 1) * n_in] for b in range(N_CHIPS)], axis=0)
    gsmall = dict(gsmall)
    gsmall["loss"] = loss8[0:1, 0:1]
    small = _pack_small(gsmall)

    recv_b = _swap_halves(gb)
    hb = _chip_sum(cidx, gb, recv_b, "chip_sum_b")
    ici_b, all_small = _exchange(hb, small)
    ra = _final_sum(cidx, ga, recv_a, ici_a, "final_sum_a")
    rb = _final_sum(cidx, gb, recv_b, ici_b, "final_sum_b")
    ra, rb = _join_halves(ra, rb)
    tot_small = _sum_small(all_small)

    shapes = {n: (tuple(w[n].shape[1:]) if n != "final_norm_g" else (D_MODEL,)) for n in WEIGHTS if n not in BIG}
    shapes["ssd_conv_w"] = (SSD_CONV, XBC_WIDTH)
    shapes["conf_dw_w"] = (CONF_KERNEL, CONF_WIDTH)
    shapes["loss"] = (1,)
    tot = _unpack_small(tot_small, shapes)
    loss = tot["loss"].reshape(())
    n1, n2 = XBC_WIDTH // N_CHIPS, CONF_WIDTH // N_CHIPS
    tot["ssd_conv_w"] = lax.dynamic_slice(tot["ssd_conv_w"], (0, chip * n1), (SSD_CONV, n1))
    tot["conf_dw_w"] = lax.dynamic_slice(tot["conf_dw_w"], (0, chip * n2), (CONF_KERNEL, n2))

    grads = {"w_ple": _ple_of_slab(ra), "w_in": rb}
    for n, off in BIG_A:
        grads[n] = ra[off:off + w[n].shape[1]]
    for n in WEIGHTS:
        if n not in BIG:
            grads[n] = tot[n]
    grads = {n: g.reshape(w[n].shape) for n, g in grads.items()}

    delta, new_m, new_v = {}, {}, {}
    for n, off in BIG_A:
        d_, m_, v_ = _adamw(w[n][0], ra, m[n][0], v[n][0], "adamw_" + n, g_off=off)
        delta[n], new_m[n], new_v[n] = d_[None], m_[None], v_[None]
    for n in ("w_in", "w_ple"):
        d_, m_, v_ = _adamw(w[n][0], grads[n][0], m[n][0], v[n][0], "adamw_" + n)
        delta[n], new_m[n], new_v[n] = d_[None], m_[None], v_[None]
    small_names = [n for n in WEIGHTS if n not in BIG]
    sizes = {n: int(w[n].size) for n in small_names}
    rows_needed = sum(-(-sizes[n] // D_MODEL) for n in small_names)
    rows_pad = -(-rows_needed // 8) * 8

    def pack(d):
        parts = [_pad_rows(d[n], -(-sizes[n] // D_MODEL)) for n in small_names]
        parts.append(jnp.zeros((rows_pad - rows_needed, D_MODEL), f32))
        return jnp.concatenate(parts, axis=0)

    sd, sm, sv = _adamw(pack(w), pack(grads), pack(m), pack(v), "adamw_small")

    def unpack(a, n, o):
        r = -(-sizes[n] // D_MODEL)
        return a[o:o + r].reshape(-1)[:sizes[n]].reshape(w[n].shape), o + r

    o = 0
    for n in small_names:
        delta[n], _ = unpack(sd, n, o)
        new_m[n], _ = unpack(sm, n, o)
        new_v[n], o = unpack(sv, n, o)

    return (loss, grad_x[None], *[grads[n] for n in WEIGHTS], *[delta[n] for n in WEIGHTS],
            *[new_m[n] for n in WEIGHTS], *[new_v[n] for n in WEIGHTS])
```

```python
import jax
import jax.numpy as jnp
from jax import lax
from jax.experimental import pallas as pl
from jax.experimental.pallas import tpu as pltpu

f32 = jnp.float32
bf16 = jnp.bfloat16

D_MODEL = 1024
SSD_WIDTH = 1024
SSD_HEADS = 16
HEAD_DIM = 64
SSD_STATE = 128
XBC_WIDTH = 1536
SSD_CONV = 4
CHUNK = 128
CONF_WIDTH = 1024
CONF_KERNEL = 31
D_FF = 4096
PLE_DIM = 256
IN_WIDTH = 4624
EPS = 1e-6
N_CHIPS = 4
N_DEV = 8

ADAM_LR = 0.001
ADAM_B1 = 0.9
ADAM_B2 = 0.999
ADAM_EPS = 1e-08
ADAM_WD = 0.01
ADAM_STEP = 10

LANES = 128
VMEM_BIG = 56 * 1024 * 1024
VMEM_MID = 40 * 1024 * 1024

UP_OFF, DOWN_OFF, OUT_OFF, PG_OFF, PLE_OFF = 0, 1024, 2048, 2560, 2816
PLE_ROWS = 64
SLAB_A = PLE_OFF + PLE_ROWS
GATHER_EARLY_ROWS = 480
W_IN_ROWS = 1156
W_IN_ROWS_PAD = 1184
CONVW_ROWS = 16
SMALL_ROWS = 56

MESH = pl.DeviceIdType.MESH
ANY = pl.BlockSpec(memory_space=pl.ANY)


def _cparams(sem=None, vmem=None):
    return pltpu.CompilerParams(dimension_semantics=sem, vmem_limit_bytes=vmem)


def _full(shape):
    n = len(shape)
    return pl.BlockSpec(shape, lambda *_: (0,) * n)


class _Rider:
    def __init__(self, inputs, out_shapes, aliases, n_sems, start, finish):
        self.inputs, self.out_shapes, self.aliases = list(inputs), list(out_shapes), dict(aliases)
        self.n_sems, self.start, self.finish = n_sems, start, finish


def _call(body, args, *, name, grid, in_specs, out_specs, out_shape, scratch_shapes=(), params=None, rider=None):
    if rider is None:
        return pl.pallas_call(body, name=name, grid=grid, in_specs=in_specs, out_specs=out_specs, out_shape=out_shape,
                              scratch_shapes=list(scratch_shapes), compiler_params=params)(*args)
    ni, no, ns = len(in_specs), len(out_specs), len(scratch_shapes)
    ri, ro = len(rider.inputs), len(rider.out_shapes)
    (steps,) = grid

    def with_rider(*refs):
        ins, refs = refs[:ni], refs[ni:]
        rins, refs = refs[:ri], refs[ri:]
        outs, refs = refs[:no], refs[no:]
        routs, refs = refs[:ro], refs[ro:]
        scratch, (ssem, rsem) = refs[:ns], refs[ns:]
        step = pl.program_id(0)

        @pl.when(step == 0)
        def _():
            rider.start(rins, routs, ssem, rsem)

        body(*ins, *outs, *scratch)

        @pl.when(step == steps - 1)
        def _():
            rider.finish(rins, routs, ssem, rsem)

    sems = [pltpu.SemaphoreType.DMA((rider.n_sems,)), pltpu.SemaphoreType.DMA((rider.n_sems,))]
    return pl.pallas_call(
        with_rider, name=name, grid=grid, in_specs=list(in_specs) + [ANY] * ri, out_specs=list(out_specs) + [ANY] * ro,
        out_shape=list(out_shape) + rider.out_shapes, scratch_shapes=list(scratch_shapes) + sems,
        input_output_aliases={ni + a: no + b for a, b in rider.aliases.items()}, compiler_params=params,
    )(*args, *rider.inputs)


def _dot(a, b):
    return jnp.dot(a, b, preferred_element_type=f32)


def _dot_nt(a, b):
    return lax.dot_general(a, b, (((1,), (1,)), ((), ())), preferred_element_type=f32)


def _dot_tn(a, b):
    return lax.dot_general(a, b, (((0,), (0,)), ((), ())), preferred_element_type=f32)


def _sigmoid(x):
    return jax.nn.sigmoid(x)


def _rms(x, g):
    r = lax.rsqrt(jnp.mean(x * x, axis=-1, keepdims=True) + EPS)
    return x * r * g


def _rms_bwd(dy, x, g):
    r = lax.rsqrt(jnp.mean(x * x, axis=-1, keepdims=True) + EPS)
    xh = x * r
    dg = jnp.sum(dy * xh, axis=0, keepdims=True)
    dxh = dy * g
    dx = r * (dxh - xh * jnp.mean(dxh * xh, axis=-1, keepdims=True))
    return dx, dg


def _dsilu(x):
    s = _sigmoid(x)
    return s * (1.0 + x * (1.0 - s))


def _split3(x):
    hi = x.astype(bf16)
    r1 = x - hi.astype(f32)
    mid = r1.astype(bf16)
    lo = (r1 - mid.astype(f32)).astype(bf16)
    return hi, mid, lo


def _head_matrix():
    row = lax.broadcasted_iota(jnp.int32, (LANES, SSD_WIDTH), 0)
    col = lax.broadcasted_iota(jnp.int32, (LANES, SSD_WIDTH), 1)
    lo = row * HEAD_DIM
    return ((col >= lo) & (col < lo + HEAD_DIM)).astype(bf16)


def _expand(x, e):
    hi, mid, lo = _split3(x)
    return _dot(hi, e) + _dot(mid, e) + _dot(lo, e)


def _contract(x, e):
    hi, mid, lo = _split3(x)
    return _dot_nt(hi, e) + _dot_nt(mid, e) + _dot_nt(lo, e)


O_XBC = SSD_WIDTH
O_DT = O_XBC + XBC_WIDTH
O_CV = O_DT + SSD_HEADS
O_CG = O_CV + CONF_WIDTH


def _in_proj_fwd(x, g, wt, rider=None):
    T = x.shape[0]
    tm = min(256, T)

    def body(x_ref, g_ref, wt_ref, u_ref, z_ref, xbc_ref, cv_ref, cg_ref, dt_ref, v_ref):
        ub = _rms(x_ref[...], g_ref[...]).astype(bf16)
        u_ref[...] = ub
        z_ref[...] = _dot_nt(ub, wt_ref[0:O_XBC, :])
        xbc_ref[...] = _dot_nt(ub, wt_ref[O_XBC:O_DT, :])
        cv = _dot_nt(ub, wt_ref[O_CV:O_CG, :])
        cg = _dot_nt(ub, wt_ref[O_CG:IN_WIDTH, :])
        cv_ref[...] = cv
        cg_ref[...] = cg
        v_ref[...] = cv * _sigmoid(cg)
        dt_ref[...] = _dot_nt(ub, wt_ref[O_DT:O_DT + LANES, :])

    row = lambda n: pl.BlockSpec((tm, n), lambda i: (i, 0))
    return _call(
        body, (x, g, wt), name="in_proj_fwd", grid=(T // tm,),
        in_specs=[row(D_MODEL), _full((1, D_MODEL)), _full(wt.shape)],
        out_specs=[row(D_MODEL), row(SSD_WIDTH), row(XBC_WIDTH), row(CONF_WIDTH), row(CONF_WIDTH), row(LANES),
                   row(CONF_WIDTH)],
        out_shape=[jax.ShapeDtypeStruct((T, D_MODEL), bf16), jax.ShapeDtypeStruct((T, SSD_WIDTH), f32),
                   jax.ShapeDtypeStruct((T, XBC_WIDTH), f32), jax.ShapeDtypeStruct((T, CONF_WIDTH), f32),
                   jax.ShapeDtypeStruct((T, CONF_WIDTH), f32), jax.ShapeDtypeStruct((T, LANES), f32),
                   jax.ShapeDtypeStruct((T, CONF_WIDTH), f32)],
        params=_cparams(("arbitrary",), VMEM_BIG), rider=rider)


SUBLANES = 8


def _phases(offsets):
    return sorted({o % SUBLANES for o in offsets} - {0})


def _phase_shape(offsets, tm, C):
    a_max = max([o // SUBLANES for o in offsets if o % SUBLANES] or [0])
    return (max(len(_phases(offsets)), 1), tm + SUBLANES * a_max, C)


def _make_phases(buf_ref, ph_ref, offsets, tm):
    for idx, b in enumerate(_phases(offsets)):
        n = tm + SUBLANES * max(o // SUBLANES for o in offsets if o % SUBLANES == b)
        ph_ref[idx, 0:n, :] = buf_ref[pl.ds(b, n), :]


def _window(buf_ref, ph_ref, offsets, o, r0, rb):
    a, b = divmod(o, SUBLANES)
    if b == 0:
        return buf_ref[pl.ds(r0 + SUBLANES * a, rb), :]
    return ph_ref[_phases(offsets).index(b), pl.ds(r0 + SUBLANES * a, rb), :]


def _conv_rows(wb_ref, buf_ref, ph_ref, offsets, r0, rb):
    nsub = rb // SUBLANES
    accs = [None] * nsub
    for k, o in enumerate(offsets):
        wk = wb_ref[pl.ds(SUBLANES * k, SUBLANES), :]
        for s in range(nsub):
            term = wk * _window(buf_ref, ph_ref, offsets, o, r0 + SUBLANES * s, SUBLANES)
            accs[s] = term if accs[s] is None else accs[s] + term
    return accs[0] if nsub == 1 else jnp.concatenate(accs, axis=0)


def _sublane_rows(w):
    return jnp.repeat(w, SUBLANES, axis=0)


def _fwd_offsets(K, hb):
    return [hb - (K - 1) + k for k in range(K)]


def _prev_halo_spec(hb, tm, C):
    return pl.BlockSpec((hb, C), lambda i: (jnp.maximum(i * (tm // hb) - 1, 0), 0))


CONV_RB = 16


def _ssd_conv_fwd(xbc, w, b):
    T, C = xbc.shape
    K, hb = SSD_CONV, 8
    tm = min(256, T)
    offs = _fwd_offsets(K, hb)

    def body(cur_ref, halo_ref, w_ref, b_ref, pre_ref, buf_ref, ph_ref):
        keep = jnp.where(pl.program_id(0) > 0, 1.0, 0.0)
        buf_ref[0:hb, :] = halo_ref[...] * keep
        buf_ref[hb:hb + tm, :] = cur_ref[...]
        _make_phases(buf_ref, ph_ref, offs, tm)

        def chunk(i, carry):
            r0 = pl.multiple_of(i * CONV_RB, CONV_RB)
            pre_ref[pl.ds(r0, CONV_RB), :] = _conv_rows(w_ref, buf_ref, ph_ref, offs, r0, CONV_RB) + b_ref[...]
            return carry

        lax.fori_loop(0, tm // CONV_RB, chunk, 0)

    return pl.pallas_call(
        body, name="ssd_conv_fwd", grid=(T // tm,),
        in_specs=[pl.BlockSpec((tm, C), lambda i: (i, 0)), _prev_halo_spec(hb, tm, C), _full((SUBLANES * K, C)),
                  _full((1, C))],
        out_specs=pl.BlockSpec((tm, C), lambda i: (i, 0)),
        out_shape=jax.ShapeDtypeStruct((T, C), f32),
        scratch_shapes=[pltpu.VMEM((hb + tm, C), f32), pltpu.VMEM(_phase_shape(offs, tm, C), f32)],
        compiler_params=_cparams(("parallel",), VMEM_MID),
    )(xbc, xbc, _sublane_rows(w), b)


def _conf_fwd(v, w, b, ln_g, ln_b, rider=None):
    T, C = v.shape
    K, hb = CONF_KERNEL, 32
    tm = min(256, T)
    offs = _fwd_offsets(K, hb)
    rb = 2 * CONV_RB

    def body(cur_ref, halo_ref, w_ref, b_ref, g_ref, bb_ref, co_ref, y_ref, buf_ref, ph_ref):
        keep = jnp.where(pl.program_id(0) > 0, 1.0, 0.0)
        buf_ref[0:hb, :] = halo_ref[...] * keep
        buf_ref[hb:hb + tm, :] = cur_ref[...]
        _make_phases(buf_ref, ph_ref, offs, tm)

        def chunk(i, carry):
            r0 = pl.multiple_of(i * rb, rb)
            co = _conv_rows(w_ref, buf_ref, ph_ref, offs, r0, rb) + b_ref[...]
            co_ref[pl.ds(r0, rb), :] = co
            mu = jnp.mean(co, axis=-1, keepdims=True)
            xc = co - mu
            yn = xc * lax.rsqrt(jnp.mean(xc * xc, axis=-1, keepdims=True) + EPS) * g_ref[...] + bb_ref[...]
            y_ref[pl.ds(r0, rb), :] = (yn * _sigmoid(yn)).astype(bf16)
            return carry

        lax.fori_loop(0, tm // rb, chunk, 0)

    return _call(
        body, (v, v, _sublane_rows(w), b, ln_g, ln_b), name="conf_fwd", grid=(T // tm,),
        in_specs=[pl.BlockSpec((tm, C), lambda i: (i, 0)), _prev_halo_spec(hb, tm, C), _full((SUBLANES * K, C)),
                  _full((1, C)), _full((1, C)), _full((1, C))],
        out_specs=[pl.BlockSpec((tm, C), lambda i: (i, 0)), pl.BlockSpec((tm, C), lambda i: (i, 0))],
        out_shape=[jax.ShapeDtypeStruct((T, C), f32), jax.ShapeDtypeStruct((T, C), bf16)],
        scratch_shapes=[pltpu.VMEM((hb + tm, C), f32), pltpu.VMEM(_phase_shape(offs, tm, C), f32)],
        params=_cparams(("arbitrary",), VMEM_MID), rider=rider)


def _ssd_chunk_common(pre, dtr, dtb, alog, e):
    act = pre * _sigmoid(pre)
    xs = act[:, :SSD_WIDTH]
    bm = act[:, SSD_WIDTH:SSD_WIDTH + 2 * SSD_STATE]
    cm = act[:, SSD_WIDTH + 2 * SSD_STATE:]
    row = lax.broadcasted_iota(jnp.int32, (CHUNK, CHUNK), 0)
    col = lax.broadcasted_iota(jnp.int32, (CHUNK, CHUNK), 1)
    tri = row >= col
    dt = jax.nn.softplus(dtr + dtb)
    a_neg = -jnp.exp(alog)
    a = dt * a_neg
    cs = jnp.dot(tri.astype(f32), a, precision=lax.Precision.HIGHEST, preferred_element_type=f32)
    cs_e = _expand(cs, e)
    dt_e = _expand(dt, e)
    csl_e = cs_e[CHUNK - 1:CHUNK, :]
    ecs_e = jnp.exp(cs_e)
    dte_e = jnp.exp(csl_e - cs_e)
    cd_e = jnp.exp(csl_e)
    xc = xs * dt_e
    xd = xc * dte_e
    return dict(xs=xs, bm=bm, cm=cm, tri=tri, dt=dt, a_neg=a_neg, cs=cs, ecs_e=ecs_e, dte_e=dte_e, cd_e=cd_e,
                dt_e=dt_e, xc=xc, xd=xd)


def _group(v, g, width):
    return v[:, g * width:(g + 1) * width]


def _ssd_fwd(pre, dtr, z, dtb, alog, dskip_e, gn):
    T = pre.shape[0]
    nc = T // CHUNK
    GW = SSD_WIDTH // 2

    def body(pre_ref, dtr_ref, z_ref, dtb_ref, alog_ref, de_ref, gn_ref, y_ref, ys_ref, sp_ref, st_ref):
        @pl.when(pl.program_id(0) == 0)
        def _():
            st_ref[...] = jnp.zeros_like(st_ref)

        e = _head_matrix()
        q = _ssd_chunk_common(pre_ref[...], dtr_ref[...], dtb_ref[...], alog_ref[...], e)
        cs, tri, xc, xd = q["cs"], q["tri"], q["xc"], q["xd"]
        cs_t = cs.T
        st = st_ref[...]
        sp_ref[0] = st
        lane = lax.broadcasted_iota(jnp.int32, (1, LANES), 1)
        halves = (lane < HEAD_DIM, lane >= HEAD_DIM)

        g_mat, y_off, s_new = [], [], []
        for g in range(2):
            bg = _group(q["bm"], g, SSD_STATE)
            cg = _group(q["cm"], g, SSD_STATE)
            bgb, cgb = bg.astype(bf16), cg.astype(bf16)
            g_mat.append(_dot_nt(cgb, bgb))
            y_off.append(_dot(cgb, _group(st, g, GW).astype(bf16)))
            s_new.append(_dot(bg.T.astype(bf16), _group(xd, g, GW).astype(bf16)))
        y_off = jnp.concatenate(y_off, axis=1) * q["ecs_e"]
        st_ref[...] = st * q["cd_e"] + jnp.concatenate(s_new, axis=1)

        pairs = []
        for j in range(SSD_HEADS // 2):
            xp = xc[:, j * LANES:(j + 1) * LANES]
            acc = jnp.zeros((CHUNK, LANES), f32)
            for hh in range(2):
                h = 2 * j + hh
                seg = cs[:, h:h + 1] - cs_t[h:h + 1, :]
                lm = jnp.exp(jnp.where(tri, seg, -1e30))
                m = (g_mat[h // 8] * lm).astype(bf16)
                acc = acc + _dot(m, jnp.where(halves[hh], xp, 0.0).astype(bf16))
            pairs.append(acc)
        y = jnp.concatenate(pairs, axis=1) + y_off + q["xs"] * de_ref[...]
        y_ref[...] = y

        zz = z_ref[...]
        v = y * (zz * _sigmoid(zz))
        outs = []
        for g in range(2):
            vg = _group(v, g, GW)
            outs.append(vg * lax.rsqrt(jnp.mean(vg * vg, axis=-1, keepdims=True) + EPS))
        ys_ref[...] = (jnp.concatenate(outs, axis=1) * gn_ref[...]).astype(bf16)

    ch = lambda n: pl.BlockSpec((CHUNK, n), lambda c: (c, 0))
    return pl.pallas_call(
        body, name="ssd_fwd", grid=(nc,),
        in_specs=[ch(XBC_WIDTH), ch(LANES), ch(SSD_WIDTH), _full((1, LANES)), _full((1, LANES)), _full((1, SSD_WIDTH)),
                  _full((1, SSD_WIDTH))],
        out_specs=[ch(SSD_WIDTH), ch(SSD_WIDTH), pl.BlockSpec((1, SSD_STATE, SSD_WIDTH), lambda c: (c, 0, 0))],
        out_shape=[jax.ShapeDtypeStruct((T, SSD_WIDTH), f32), jax.ShapeDtypeStruct((T, SSD_WIDTH), bf16),
                   jax.ShapeDtypeStruct((nc, SSD_STATE, SSD_WIDTH), f32)],
        scratch_shapes=[pltpu.VMEM((SSD_STATE, SSD_WIDTH), f32)],
        compiler_params=_cparams(("arbitrary",), VMEM_MID),
    )(pre, dtr, z, dtb, alog, dskip_e, gn)


def _w_out_spec():
    n = 2 * SSD_WIDTH // N_CHIPS
    return pl.BlockSpec((N_CHIPS, n, D_MODEL), lambda *_: (0, OUT_OFF // n, 0))


def _out_proj_fwd(x, ys, yc, gath, g):
    T = x.shape[0]
    tm = min(512, T)
    n = 2 * SSD_WIDTH // N_CHIPS

    def body(x_ref, ys_ref, yc_ref, w_ref, g_ref, h_ref, u_ref):
        h = (x_ref[...] + _dot(ys_ref[:, 0:n], w_ref[0]) + _dot(ys_ref[:, n:], w_ref[1])
             + _dot(yc_ref[:, 0:n], w_ref[2]) + _dot(yc_ref[:, n:], w_ref[3]))
        h_ref[...] = h
        u_ref[...] = _rms(h, g_ref[...]).astype(bf16)

    row = pl.BlockSpec((tm, D_MODEL), lambda i: (i, 0))
    return pl.pallas_call(
        body, name="out_proj_fwd", grid=(T // tm,),
        in_specs=[row, row, row, _w_out_spec(), _full((1, D_MODEL))],
        out_specs=[row, row],
        out_shape=[jax.ShapeDtypeStruct((T, D_MODEL), f32), jax.ShapeDtypeStruct((T, D_MODEL), bf16)],
        compiler_params=_cparams(("parallel",), VMEM_MID),
    )(x, ys, yc, gath, g)


def _w_up_spec():
    return pl.BlockSpec((1, D_MODEL, D_MODEL), lambda i, b: (b, UP_OFF // D_MODEL, 0))


def _w_down_spec():
    return pl.BlockSpec((1, D_MODEL, D_MODEL), lambda i, b: (b, DOWN_OFF // D_MODEL, 0))


def _mlp_fwd(h1, u1, gath, g_next):
    T = h1.shape[0]
    tm = min(512, T)
    nb = D_FF // D_MODEL

    def body(h_ref, u_ref, wu_ref, wd_ref, g_ref, r_ref, h2_ref, u2_ref, acc_ref):
        b = pl.program_id(1)

        @pl.when(b == 0)
        def _():
            acc_ref[...] = jnp.zeros_like(acc_ref)

        r = jnp.maximum(_dot(u_ref[...], wu_ref[0]), 0.0)
        r_ref[...] = r.astype(bf16)
        acc_ref[...] += _dot((r * r).astype(bf16), wd_ref[0])

        @pl.when(b == nb - 1)
        def _():
            h2 = h_ref[...] + acc_ref[...]
            h2_ref[...] = h2
            u2_ref[...] = _rms(h2, g_ref[...]).astype(bf16)

    row = pl.BlockSpec((tm, D_MODEL), lambda i, b: (i, 0))
    return pl.pallas_call(
        body, name="mlp_fwd", grid=(T // tm, nb),
        in_specs=[row, row, _w_up_spec(), _w_down_spec(), _full((1, D_MODEL))],
        out_specs=[pl.BlockSpec((tm, D_MODEL), lambda i, b: (i, b)), row, row],
        out_shape=[jax.ShapeDtypeStruct((T, D_FF), bf16), jax.ShapeDtypeStruct((T, D_MODEL), f32),
                   jax.ShapeDtypeStruct((T, D_MODEL), bf16)],
        scratch_shapes=[pltpu.VMEM((tm, D_MODEL), f32)],
        compiler_params=_cparams(("parallel", "arbitrary"), VMEM_MID),
    )(h1, u1, gath, gath, g_next)


def _ple_loss(h2, u2, p, tgt, gath, b_pg, w_ple, g_ple, g_fin, g_pg):
    T = h2.shape[0]
    tm = min(256, T)
    npg = D_MODEL // N_CHIPS

    def body(h2_ref, u2_ref, p_ref, t_ref, wpg_ref, bpg_ref, wple_ref, gple_ref, gfin_ref, gpg_ref,
             loss_ref, dh2_ref, dh2b_ref, dgp_ref, dep_ref, dgfin_ref, dgple_ref, dbpg_ref, dgpg_ref):
        @pl.when(pl.program_id(0) == 0)
        def _():
            loss_ref[...] = jnp.zeros_like(loss_ref)
            dgfin_ref[...] = jnp.zeros_like(dgfin_ref)
            dgple_ref[...] = jnp.zeros_like(dgple_ref)
            dbpg_ref[...] = jnp.zeros_like(dbpg_ref)
            dgpg_ref[...] = jnp.zeros_like(dgpg_ref)

        h2 = h2_ref[...]
        gate_pre = bpg_ref[...]
        for b in range(N_CHIPS):
            gate_pre = gate_pre + _dot(u2_ref[:, b * npg:(b + 1) * npg], wpg_ref[b])
        gate = _sigmoid(gate_pre)
        e_pre = _dot(p_ref[...].astype(bf16), wple_ref[...])
        emb = _rms(e_pre, gple_ref[...])
        h3 = h2 + gate * emb
        diff = _rms(h3, gfin_ref[...]) - t_ref[...]
        sq = jnp.sum(jnp.sum(diff * diff, axis=1, keepdims=True), axis=0, keepdims=True)
        loss_ref[...] += (0.5 / D_MODEL) * sq
        dh3, dgfin = _rms_bwd(diff * (1.0 / D_MODEL), h3, gfin_ref[...])
        dgfin_ref[...] += dgfin
        dgp = dh3 * emb * gate * (1.0 - gate)
        dbpg_ref[...] += jnp.sum(dgp, axis=0, keepdims=True)
        dep, dgple = _rms_bwd(dh3 * gate, e_pre, gple_ref[...])
        dgple_ref[...] += dgple
        dgpb = dgp.astype(bf16)
        dgp_ref[...] = dgpb
        dep_ref[...] = dep.astype(bf16)
        du2 = jnp.concatenate([_dot_nt(dgpb, wpg_ref[b]) for b in range(N_CHIPS)], axis=1)
        dx, dgpg = _rms_bwd(du2, h2, gpg_ref[...])
        dgpg_ref[...] += dgpg
        dh2 = dh3 + dx
        dh2_ref[...] = dh2
        dh2b_ref[...] = dh2.astype(bf16)

    row = pl.BlockSpec((tm, D_MODEL), lambda i: (i, 0))
    vec = _full((1, D_MODEL))
    vshape = jax.ShapeDtypeStruct((1, D_MODEL), f32)
    return pl.pallas_call(
        body, name="ple_loss", grid=(T // tm,),
        in_specs=[row, row, pl.BlockSpec((tm, PLE_DIM), lambda i: (i, 0)), row,
                  pl.BlockSpec((N_CHIPS, npg, D_MODEL), lambda i: (0, PG_OFF // npg, 0)), vec, _full(w_ple.shape),
                  vec, vec, vec],
        out_specs=[_full((8, LANES)), row, row, row, row, vec, vec, vec, vec],
        out_shape=[jax.ShapeDtypeStruct((8, LANES), f32), jax.ShapeDtypeStruct((T, D_MODEL), f32),
                   jax.ShapeDtypeStruct((T, D_MODEL), bf16), jax.ShapeDtypeStruct((T, D_MODEL), bf16),
                   jax.ShapeDtypeStruct((T, D_MODEL), bf16), vshape, vshape, vshape, vshape],
        compiler_params=_cparams(("arbitrary",), VMEM_MID),
    )(h2, u2, p, tgt, gath, b_pg, w_ple, g_ple, g_fin, g_pg)


def _mlp_bwd(dh2, r, gath, h1, g):
    T = dh2.shape[0]
    tm = min(512, T)
    nb = D_FF // D_MODEL

    def body(dh2_ref, r_ref, wd_ref, wu_ref, h1_ref, g_ref, dhp_ref, dh1_ref, dh1b_ref, dg_ref, acc_ref):
        i, b = pl.program_id(0), pl.program_id(1)

        @pl.when(b == 0)
        def _():
            acc_ref[...] = jnp.zeros_like(acc_ref)

        @pl.when((b == 0) & (i == 0))
        def _():
            dg_ref[...] = jnp.zeros_like(dg_ref)

        dact = _dot_nt(dh2_ref[...].astype(bf16), wd_ref[0])
        dhp = (dact * 2.0 * r_ref[...].astype(f32)).astype(bf16)
        dhp_ref[...] = dhp
        acc_ref[...] += _dot_nt(dhp, wu_ref[0])

        @pl.when(b == nb - 1)
        def _():
            dx, dg = _rms_bwd(acc_ref[...], h1_ref[...], g_ref[...])
            dg_ref[...] += dg
            dh1 = dh2_ref[...] + dx
            dh1_ref[...] = dh1
            dh1b_ref[...] = dh1.astype(bf16)

    row = pl.BlockSpec((tm, D_MODEL), lambda i, b: (i, 0))
    return pl.pallas_call(
        body, name="mlp_bwd", grid=(T // tm, nb),
        in_specs=[row, pl.BlockSpec((tm, D_MODEL), lambda i, b: (i, b)), _w_down_spec(), _w_up_spec(), row,
                  _full((1, D_MODEL))],
        out_specs=[pl.BlockSpec((tm, D_MODEL), lambda i, b: (i, b)), row, row, _full((1, D_MODEL))],
        out_shape=[jax.ShapeDtypeStruct((T, D_FF), bf16), jax.ShapeDtypeStruct((T, D_MODEL), f32),
                   jax.ShapeDtypeStruct((T, D_MODEL), bf16), jax.ShapeDtypeStruct((1, D_MODEL), f32)],
        scratch_shapes=[pltpu.VMEM((tm, D_MODEL), f32)],
        compiler_params=_cparams(("arbitrary", "arbitrary"), VMEM_MID),
    )(dh2, r, gath, gath, h1, g)


def _out_proj_bwd(dh1, gath, co, ln_g, ln_b, rider=None):
    T = dh1.shape[0]
    tm = min(512, T)

    def body(dh_ref, w_ref, co_ref, g_ref, b_ref, dys_ref, dco_ref, dg_ref, db_ref):
        @pl.when(pl.program_id(0) == 0)
        def _():
            dg_ref[...] = jnp.zeros_like(dg_ref)
            db_ref[...] = jnp.zeros_like(db_ref)

        dhb = dh_ref[...].astype(bf16)
        dys_ref[...] = jnp.concatenate([_dot_nt(dhb, w_ref[0]), _dot_nt(dhb, w_ref[1])], axis=1)
        dyc = jnp.concatenate([_dot_nt(dhb, w_ref[2]), _dot_nt(dhb, w_ref[3])], axis=1)
        co = co_ref[...]
        mu = jnp.mean(co, axis=-1, keepdims=True)
        xc = co - mu
        rstd = lax.rsqrt(jnp.mean(xc * xc, axis=-1, keepdims=True) + EPS)
        xh = xc * rstd
        yn = xh * g_ref[...] + b_ref[...]
        dyn = dyc * _dsilu(yn)
        dg_ref[...] += jnp.sum(dyn * xh, axis=0, keepdims=True)
        db_ref[...] += jnp.sum(dyn, axis=0, keepdims=True)
        dxh = dyn * g_ref[...]
        dco_ref[...] = rstd * (dxh - jnp.mean(dxh, axis=-1, keepdims=True)
                               - xh * jnp.mean(dxh * xh, axis=-1, keepdims=True))

    row = pl.BlockSpec((tm, D_MODEL), lambda i: (i, 0))
    vec = _full((1, CONF_WIDTH))
    vshape = jax.ShapeDtypeStruct((1, CONF_WIDTH), f32)
    return _call(
        body, (dh1, gath, co, ln_g, ln_b), name="out_proj_bwd", grid=(T // tm,),
        in_specs=[row, _w_out_spec(), row, vec, vec],
        out_specs=[row, row, vec, vec],
        out_shape=[jax.ShapeDtypeStruct((T, SSD_WIDTH), f32), jax.ShapeDtypeStruct((T, CONF_WIDTH), f32), vshape, vshape],
        params=_cparams(("arbitrary",), VMEM_MID), rider=rider)


def _bwd_offsets(K):
    return [K - 1 - k for k in range(K)]


def _next_halo_spec(hb, tm, C, T):
    return pl.BlockSpec((hb, C), lambda i: (jnp.minimum((i + 1) * (tm // hb), T // hb - 1), 0))


DW_RB = 8
DW_UNROLL = 4
DW_ACC_VREGS = 32


def _conv_dw(dw_ref, bufd_ref, bufx_ref, phx_ref, offs_x, tm, C):
    K = len(offs_x)
    group = max(1, DW_ACC_VREGS // (C // LANES))
    for k0 in range(0, K, group):
        ks = list(range(k0, min(k0 + group, K)))

        def step(i, accs, ks=ks):
            for u in range(DW_UNROLL):
                r0 = pl.multiple_of((i * DW_UNROLL + u) * DW_RB, DW_RB)
                d = bufd_ref[pl.ds(r0, DW_RB), :]
                accs = tuple(acc + _window(bufx_ref, phx_ref, offs_x, offs_x[k], r0, DW_RB) * d
                             for k, acc in zip(ks, accs))
            return accs

        accs = lax.fori_loop(0, tm // (DW_RB * DW_UNROLL), step, tuple(jnp.zeros((DW_RB, C), f32) for _ in ks))
        for k, acc in zip(ks, accs):
            dw_ref[k:k + 1, :] += jnp.sum(acc, axis=0, keepdims=True)


def _fill_bwd_buffers(dcur_ref, dnext_ref, xcur_ref, xprev_ref, bufd_ref, bufx_ref, phd_ref, phx_ref, offs_d, offs_x,
                      hb, tm, first, last):
    bufd_ref[0:tm, :] = dcur_ref[...]
    bufd_ref[tm:tm + hb, :] = dnext_ref[...] * jnp.where(last, 0.0, 1.0)
    bufx_ref[0:hb, :] = xprev_ref[...] * jnp.where(first, 0.0, 1.0)
    bufx_ref[hb:hb + tm, :] = xcur_ref[...]
    _make_phases(bufd_ref, phd_ref, offs_d, tm)
    _make_phases(bufx_ref, phx_ref, offs_x, tm)


def _ssd_conv_bwd(dpre, xbc, w):
    T, C = xbc.shape
    K, hb = SSD_CONV, 8
    tm = min(256, T)
    nt = T // tm
    offs_d, offs_x = _bwd_offsets(K), _fwd_offsets(K, hb)

    def body(dcur_ref, dnext_ref, xcur_ref, xprev_ref, w_ref, dx_ref, dw_ref, db_ref, bufd_ref, bufx_ref, phd_ref, phx_ref):
        i = pl.program_id(0)

        @pl.when(i == 0)
        def _():
            dw_ref[...] = jnp.zeros_like(dw_ref)
            db_ref[...] = jnp.zeros_like(db_ref)

        _fill_bwd_buffers(dcur_ref, dnext_ref, xcur_ref, xprev_ref, bufd_ref, bufx_ref, phd_ref, phx_ref, offs_d, offs_x,
                          hb, tm, i == 0, i == nt - 1)

        def chunk(j, carry):
            r0 = pl.multiple_of(j * CONV_RB, CONV_RB)
            dx_ref[pl.ds(r0, CONV_RB), :] = _conv_rows(w_ref, bufd_ref, phd_ref, offs_d, r0, CONV_RB).astype(bf16)
            return carry

        lax.fori_loop(0, tm // CONV_RB, chunk, 0)
        _conv_dw(dw_ref, bufd_ref, bufx_ref, phx_ref, offs_x, tm, C)
        db_ref[...] += jnp.sum(dcur_ref[...], axis=0, keepdims=True)

    row = pl.BlockSpec((tm, C), lambda i: (i, 0))
    return pl.pallas_call(
        body, name="ssd_conv_bwd", grid=(nt,),
        in_specs=[row, _next_halo_spec(hb, tm, C, T), row, _prev_halo_spec(hb, tm, C), _full((SUBLANES * K, C))],
        out_specs=[row, _full((8, C)), _full((1, C))],
        out_shape=[jax.ShapeDtypeStruct((T, C), bf16), jax.ShapeDtypeStruct((8, C), f32), jax.ShapeDtypeStruct((1, C), f32)],
        scratch_shapes=[pltpu.VMEM((tm + hb, C), f32), pltpu.VMEM((hb + tm, C), f32),
                        pltpu.VMEM(_phase_shape(offs_d, tm, C), f32),
                        pltpu.VMEM(_phase_shape(offs_x, tm, C), f32)],
        compiler_params=_cparams(("arbitrary",), VMEM_BIG),
    )(dpre, dpre, xbc, xbc, _sublane_rows(w))


def _conf_conv_bwd(dco, v, w, cv, cg, rider=None):
    T, C = v.shape
    K, hb = CONF_KERNEL, 32
    tm = min(256, T)
    nt = T // tm
    offs_d, offs_x = _bwd_offsets(K), _fwd_offsets(K, hb)

    def body(dcur_ref, dnext_ref, vcur_ref, vprev_ref, w_ref, cv_ref, cg_ref, dcv_ref, dcg_ref, dw_ref, db_ref,
             bufd_ref, bufx_ref, phd_ref, phx_ref):
        i = pl.program_id(0)

        @pl.when(i == 0)
        def _():
            dw_ref[...] = jnp.zeros_like(dw_ref)
            db_ref[...] = jnp.zeros_like(db_ref)

        _fill_bwd_buffers(dcur_ref, dnext_ref, vcur_ref, vprev_ref, bufd_ref, bufx_ref, phd_ref, phx_ref, offs_d, offs_x,
                          hb, tm, i == 0, i == nt - 1)

        def chunk(j, carry):
            r0 = pl.multiple_of(j * CONV_RB, CONV_RB)
            rows = pl.ds(r0, CONV_RB)
            dv = _conv_rows(w_ref, bufd_ref, phd_ref, offs_d, r0, CONV_RB)
            s = _sigmoid(cg_ref[rows, :])
            dcv_ref[rows, :] = (dv * s).astype(bf16)
            dcg_ref[rows, :] = (dv * cv_ref[rows, :] * s * (1.0 - s)).astype(bf16)
            return carry

        lax.fori_loop(0, tm // CONV_RB, chunk, 0)
        _conv_dw(dw_ref, bufd_ref, bufx_ref, phx_ref, offs_x, tm, C)
        db_ref[...] += jnp.sum(dcur_ref[...], axis=0, keepdims=True)

    row = pl.BlockSpec((tm, C), lambda i: (i, 0))
    return _call(
        body, (dco, dco, v, v, _sublane_rows(w), cv, cg), name="conf_conv_bwd", grid=(nt,),
        in_specs=[row, _next_halo_spec(hb, tm, C, T), row, _prev_halo_spec(hb, tm, C), _full((SUBLANES * K, C)), row, row],
        out_specs=[row, row, _full((32, C)), _full((1, C))],
        out_shape=[jax.ShapeDtypeStruct((T, C), bf16), jax.ShapeDtypeStruct((T, C), bf16),
                   jax.ShapeDtypeStruct((32, C), f32), jax.ShapeDtypeStruct((1, C), f32)],
        scratch_shapes=[pltpu.VMEM((tm + hb, C), f32), pltpu.VMEM((hb + tm, C), f32),
                        pltpu.VMEM(_phase_shape(offs_d, tm, C), f32),
                        pltpu.VMEM(_phase_shape(offs_x, tm, C), f32)],
        params=_cparams(("arbitrary",), VMEM_BIG), rider=rider)


def _ssd_bwd(dys, y, z, pre, dtr, sprev, dtb, alog, dskip_e, gn):
    T = pre.shape[0]
    nc = T // CHUNK
    GW = SSD_WIDTH // 2

    def body(dys_ref, y_ref, z_ref, pre_ref, dtr_ref, sp_ref, dtb_ref, alog_ref, de_ref, gn_ref,
             dz_ref, dpre_ref, ddtr_ref, dgn_ref, dd_ref, dal_ref, ddtb_ref, ds_ref):
        @pl.when(pl.program_id(0) == 0)
        def _():
            ds_ref[...] = jnp.zeros_like(ds_ref)
            dgn_ref[...] = jnp.zeros_like(dgn_ref)
            dd_ref[...] = jnp.zeros_like(dd_ref)
            dal_ref[...] = jnp.zeros_like(dal_ref)
            ddtb_ref[...] = jnp.zeros_like(ddtb_ref)

        e = _head_matrix()
        pre = pre_ref[...]
        dtr_b = dtr_ref[...] + dtb_ref[...]
        q = _ssd_chunk_common(pre, dtr_ref[...], dtb_ref[...], alog_ref[...], e)
        cs, tri, xc, xd, xs, dt = q["cs"], q["tri"], q["xc"], q["xd"], q["xs"], q["dt"]
        cs_t = cs.T
        st = sp_ref[0]
        dsn = ds_ref[...]
        lane = lax.broadcasted_iota(jnp.int32, (1, LANES), 1)
        halves = (lane < HEAD_DIM, lane >= HEAD_DIM)
        row_i = lax.broadcasted_iota(jnp.int32, (CHUNK, CHUNK), 0)
        col_i = lax.broadcasted_iota(jnp.int32, (CHUNK, CHUNK), 1)
        tri_t = col_i >= row_i

        y = y_ref[...]
        zz = z_ref[...]
        sz = _sigmoid(zz)
        silu_z = zz * sz
        v = y * silu_z
        dout = dys_ref[...]
        gn_v = gn_ref[...]
        dv, vh = [], []
        for g in range(2):
            vg = _group(v, g, GW)
            rstd = lax.rsqrt(jnp.mean(vg * vg, axis=-1, keepdims=True) + EPS)
            vhg = vg * rstd
            dvh = _group(dout, g, GW) * _group(gn_v, g, GW)
            dv.append(rstd * (dvh - vhg * jnp.mean(dvh * vhg, axis=-1, keepdims=True)))
            vh.append(vhg)
        dv = jnp.concatenate(dv, axis=1)
        dgn_ref[...] += jnp.sum(dout * jnp.concatenate(vh, axis=1), axis=0, keepdims=True)
        dy = dv * silu_z
        dz_ref[...] = (dv * y * (sz * (1.0 + zz * (1.0 - sz)))).astype(bf16)

        dd_row = jnp.sum(dy * xs, axis=0, keepdims=True)
        dd_ref[...] += _contract(jnp.broadcast_to(dd_row, (8, SSD_WIDTH)), e)[0:1, :]
        dxs = dy * de_ref[...]

        dz_in = dy * q["ecs_e"]
        g_mat, gt_mat, dcm, dbm, dsp, dxd, y_off = [], [], [], [], [], [], []
        bgs, cgs = [], []
        for g in range(2):
            bg = _group(q["bm"], g, SSD_STATE)
            cg = _group(q["cm"], g, SSD_STATE)
            bgb, cgb = bg.astype(bf16), cg.astype(bf16)
            bgs.append(bgb)
            cgs.append(cgb)
            stg = _group(st, g, GW).astype(bf16)
            dsng = _group(dsn, g, GW).astype(bf16)
            dzg = _group(dz_in, g, GW).astype(bf16)
            g_mat.append(_dot_nt(cgb, bgb))
            gt_mat.append(_dot_nt(bgb, cgb))
            y_off.append(_dot(cgb, stg))
            dcm.append(_dot_nt(dzg, stg))
            dsp.append(_dot(cg.T.astype(bf16), dzg))
            dbm.append(_dot_nt(_group(xd, g, GW).astype(bf16), dsng))
            dxd.append(_dot(bgb, dsng))
        y_off = jnp.concatenate(y_off, axis=1) * q["ecs_e"]
        dxd = jnp.concatenate(dxd, axis=1)
        ds_ref[...] = dsn * q["cd_e"] + jnp.concatenate(dsp, axis=1)
        dcd_row = jnp.sum(dsn * st, axis=0, keepdims=True) * q["cd_e"]
        t_e = dxd * xd
        dcs = _contract(dy * y_off - t_e, e)
        last_row = _contract(jnp.broadcast_to(dcd_row + jnp.sum(t_e, axis=0, keepdims=True), (8, SSD_WIDTH)), e)[0:1, :]
        dxc_state = dxd * q["dte_e"]

        dg_acc = [jnp.zeros((CHUNK, CHUNK), f32), jnp.zeros((CHUNK, CHUNK), f32)]
        dgt_acc = [jnp.zeros((CHUNK, CHUNK), f32), jnp.zeros((CHUNK, CHUNK), f32)]
        dxc_pairs = []
        for j in range(SSD_HEADS // 2):
            dyp_f = dy[:, j * LANES:(j + 1) * LANES]
            xcp_f = xc[:, j * LANES:(j + 1) * LANES]
            acc = jnp.zeros((CHUNK, LANES), f32)
            for hh in range(2):
                h = 2 * j + hh
                g = h // 8
                dyp = jnp.where(halves[hh], dyp_f, 0.0).astype(bf16)
                xcp = jnp.where(halves[hh], xcp_f, 0.0).astype(bf16)
                lm = jnp.exp(jnp.where(tri, cs[:, h:h + 1] - cs_t[h:h + 1, :], -1e30))
                lm_t = jnp.exp(jnp.where(tri_t, cs_t[h:h + 1, :] - cs[:, h:h + 1], -1e30))
                dm = _dot_nt(dyp, xcp) * lm
                dm_t = _dot_nt(xcp, dyp) * lm_t
                acc = acc + _dot((gt_mat[g] * lm_t).astype(bf16), dyp)
                dg_acc[g] = dg_acc[g] + dm
                dgt_acc[g] = dgt_acc[g] + dm_t
                qd = jnp.sum(dm * g_mat[g] - dm_t * gt_mat[g], axis=1, keepdims=True)
                dcs = dcs + qd * (lane == h).astype(f32)
            dxc_pairs.append(acc)
        dxc = jnp.concatenate(dxc_pairs, axis=1) + dxc_state
        for g in range(2):
            dcm[g] = dcm[g] + _dot(dg_acc[g].astype(bf16), bgs[g])
            dbm[g] = dbm[g] + _dot(dgt_acc[g].astype(bf16), cgs[g])

        dxs = dxs + dxc * q["dt_e"]
        ddt = _contract(dxc * xs, e)
        dcs = dcs + jnp.where(row_i == CHUNK - 1, jnp.broadcast_to(last_row, (CHUNK, LANES)), 0.0)
        da = jnp.dot(tri_t.astype(f32), dcs, precision=lax.Precision.HIGHEST, preferred_element_type=f32)
        ddt = ddt + da * q["a_neg"]
        dal_ref[...] += jnp.sum(da * dt, axis=0, keepdims=True) * q["a_neg"]
        ddtr = ddt * _sigmoid(dtr_b) * (lane < SSD_HEADS).astype(f32)
        ddtb_ref[...] += jnp.sum(ddtr, axis=0, keepdims=True)
        ddtr_ref[...] = ddtr.astype(bf16)

        dact = jnp.concatenate([dxs, dbm[0], dbm[1], dcm[0], dcm[1]], axis=1)
        dpre_ref[...] = dact * _dsilu(pre)

    rev = lambda n: pl.BlockSpec((CHUNK, n), lambda c: (nc - 1 - c, 0))
    vec = _full((1, LANES))
    vshape = jax.ShapeDtypeStruct((1, LANES), f32)
    return pl.pallas_call(
        body, name="ssd_bwd", grid=(nc,),
        in_specs=[rev(SSD_WIDTH), rev(SSD_WIDTH), rev(SSD_WIDTH), rev(XBC_WIDTH), rev(LANES),
                  pl.BlockSpec((1, SSD_STATE, SSD_WIDTH), lambda c: (nc - 1 - c, 0, 0)),
                  vec, vec, _full((1, SSD_WIDTH)), _full((1, SSD_WIDTH))],
        out_specs=[rev(SSD_WIDTH), rev(XBC_WIDTH), rev(LANES), _full((1, SSD_WIDTH)), vec, vec, vec],
        out_shape=[jax.ShapeDtypeStruct((T, SSD_WIDTH), bf16), jax.ShapeDtypeStruct((T, XBC_WIDTH), f32),
                   jax.ShapeDtypeStruct((T, LANES), bf16), jax.ShapeDtypeStruct((1, SSD_WIDTH), f32),
                   vshape, vshape, vshape],
        scratch_shapes=[pltpu.VMEM((SSD_STATE, SSD_WIDTH), f32)],
        compiler_params=_cparams(("arbitrary",), VMEM_MID),
    )(dys, y, z, pre, dtr, sprev, dtb, alog, dskip_e, gn)


def _in_proj_bwd(dz, dxbc, dcv, dcg, ddt, wt, x, dh1, g):
    T = x.shape[0]
    tm = min(256, T)

    def body(dz_ref, dx_ref, dcv_ref, dcg_ref, ddt_ref, wt_ref, x_ref, dh_ref, g_ref, gx_ref, dg_ref):
        @pl.when(pl.program_id(0) == 0)
        def _():
            dg_ref[...] = jnp.zeros_like(dg_ref)

        du = (_dot(dz_ref[...], wt_ref[0:O_XBC, :]) + _dot(dx_ref[...], wt_ref[O_XBC:O_DT, :])
              + _dot(dcv_ref[...], wt_ref[O_CV:O_CG, :]) + _dot(dcg_ref[...], wt_ref[O_CG:IN_WIDTH, :])
              + _dot(ddt_ref[...], wt_ref[O_DT:O_DT + LANES, :]))
        dx, dg = _rms_bwd(du, x_ref[...], g_ref[...])
        dg_ref[...] += dg
        gx_ref[...] = dh_ref[...] + dx

    row = lambda n: pl.BlockSpec((tm, n), lambda i: (i, 0))
    return pl.pallas_call(
        body, name="in_proj_bwd", grid=(T // tm,),
        in_specs=[row(SSD_WIDTH), row(XBC_WIDTH), row(CONF_WIDTH), row(CONF_WIDTH), row(LANES), _full(wt.shape),
                  row(D_MODEL), row(D_MODEL), _full((1, D_MODEL))],
        out_specs=[row(D_MODEL), _full((1, D_MODEL))],
        out_shape=[jax.ShapeDtypeStruct((T, D_MODEL), f32), jax.ShapeDtypeStruct((1, D_MODEL), f32)],
        compiler_params=_cparams(("arbitrary",), VMEM_BIG),
    )(dz, dxbc, dcv, dcg, ddt, wt, x, dh1, g)


def _weight_grad(a, g, name, square=False, slab=None, place=None, tk=512):
    T, K = a.shape
    N = g.shape[1]
    tk = min(tk, K)
    tn = 1024 if N % 1024 == 0 else min(512, N)
    tt = min(2048, T)

    def body(a_ref, g_ref, *rest):
        o_ref = rest[-1]
        acc = _dot_tn(_operand(a_ref[...]), g_ref[...].astype(bf16))
        t = pl.program_id(2)
        shaped = acc if slab is None else acc[None]

        @pl.when(t == 0)
        def _():
            o_ref[...] = shaped

        @pl.when(t > 0)
        def _():
            o_ref[...] += shaped

    def _operand(av):
        if square:
            av = av.astype(f32)
            av = av * av
        return av.astype(bf16)

    in_specs = [pl.BlockSpec((tt, tk), lambda i, j, t: (t, i)), pl.BlockSpec((tt, tn), lambda i, j, t: (t, j))]
    grid = (K // tk, N // tn, T // tt)
    params = _cparams(("parallel", "parallel", "arbitrary"), VMEM_MID)
    if slab is None:
        return pl.pallas_call(
            body, name=name, grid=grid, in_specs=in_specs,
            out_specs=pl.BlockSpec((tk, tn), lambda i, j, t: (i, j)),
            out_shape=jax.ShapeDtypeStruct((K, N), f32), compiler_params=params,
        )(a, g)
    return pl.pallas_call(
        body, name=name, grid=grid, in_specs=in_specs + [ANY],
        out_specs=pl.BlockSpec((1, tk, tn), lambda i, j, t: place(i, j)),
        out_shape=jax.ShapeDtypeStruct(slab.shape, f32), input_output_aliases={2: 0}, compiler_params=params,
    )(a, g, slab)


def _place():
    return lax.axis_index("x"), lax.axis_index("y"), lax.axis_index("c")


def _other_chips(x, y):
    return [(1 - x, y), (x, 1 - y), (1 - x, 1 - y)]


def _remote(src, dst, ssem, rsem, dev):
    return pltpu.make_async_remote_copy(src_ref=src, dst_ref=dst, send_sem=ssem, recv_sem=rsem, device_id=dev,
                                        device_id_type=MESH)


def _gather_weights(arrays, convw):
    n = len(arrays)
    halves = tuple(a.shape[1] // 2 for a in arrays)

    def body(*refs):
        cw_ref, cwo_ref = refs[n], refs[2 * n + 1]
        ssem, rsem, lsem = refs[2 * n + 2:]
        triples = tuple(zip(refs[:n], refs[n + 1:2 * n + 1], halves))
        x, y, c = _place()
        me_b = 2 * x + y
        sib = (x, y, 1 - c)
        chips = _other_chips(x, y)
        loc = pltpu.make_async_copy(cw_ref, cwo_ref.at[me_b], lsem)
        loc.start()
        sends = []
        for j, (src, dst, h) in enumerate(triples):
            mine = pl.ds(c * h, h)
            for k, (px, py) in enumerate(chips):
                s = 6 * j + k
                sends.append(_remote(src.at[me_b, mine], dst.at[me_b, mine], ssem.at[s], rsem.at[s], (px, py, c)))
        for k, (px, py) in enumerate(chips):
            sends.append(_remote(cw_ref, cwo_ref.at[me_b], ssem.at[6 * n + k], rsem.at[6 * n + k], (px, py, c)))
        for cp in sends:
            cp.start()
        for j, (src, dst, h) in enumerate(triples):
            mine = pl.ds(c * h, h)
            for k, (px, py) in enumerate(chips):
                b = 2 * px + py
                s = 6 * j + k
                _remote(src.at[b, mine], dst.at[b, mine], ssem.at[s], rsem.at[s], (px, py, c)).wait_recv()
                fw = _remote(dst.at[b, mine], dst.at[b, mine], ssem.at[s + 3], rsem.at[s + 3], sib)
                fw.start()
                sends.append(fw)
        for k, (px, py) in enumerate(chips):
            b = 2 * px + py
            _remote(cw_ref, cwo_ref.at[b], ssem.at[6 * n + k], rsem.at[6 * n + k], (px, py, c)).wait_recv()
        for j, (src, dst, h) in enumerate(triples):
            theirs = pl.ds((1 - c) * h, h)
            for k, (px, py) in enumerate(chips):
                b = 2 * px + py
                s = 6 * j + k + 3
                _remote(src.at[b, theirs], dst.at[b, theirs], ssem.at[s], rsem.at[s], sib).wait_recv()
        for cp in sends:
            cp.wait_send()
        loc.wait()

    return pl.pallas_call(
        body, name="gather_weights", in_specs=[ANY] * (n + 1), out_specs=[ANY] * (n + 1),
        out_shape=[jax.ShapeDtypeStruct(a.shape, bf16) for a in arrays]
        + [jax.ShapeDtypeStruct((N_CHIPS, CONVW_ROWS, D_MODEL), f32)],
        input_output_aliases={j: j for j in range(n)},
        scratch_shapes=[pltpu.SemaphoreType.DMA((6 * n + 3,)), pltpu.SemaphoreType.DMA((6 * n + 3,)),
                        pltpu.SemaphoreType.DMA(())],
    )(*arrays, convw)


def _gather_rider(gath0, lo, n):
    h = gath0.shape[1] // 2

    def copies(rins, routs, ssem, rsem, sending):
        (g_ref,), (o_ref,) = rins, routs
        x, y, c = _place()
        mine = pl.ds(c * h + lo, n)
        for k, (px, py) in enumerate(_other_chips(x, y)):
            b = 2 * x + y if sending else 2 * px + py
            yield _remote(g_ref.at[b, mine], o_ref.at[b, mine], ssem.at[k], rsem.at[k], (px, py, c))

    def start(*refs):
        for cp in copies(*refs, sending=True):
            cp.start()

    def finish(*refs):
        for cp in copies(*refs, sending=False):
            cp.wait()

    return _Rider([gath0], [jax.ShapeDtypeStruct(gath0.shape, gath0.dtype)], {0: 0}, 3, start, finish)


def _forward_to_sibling(gath):
    h = gath.shape[1] // 2

    def body(g_ref, o_ref, ssem, rsem):
        x, y, c = _place()
        sib = (x, y, 1 - c)
        mine, theirs = pl.ds(c * h, h), pl.ds((1 - c) * h, h)
        blocks = [2 * px + py for px, py in _other_chips(x, y)]
        sends = [_remote(g_ref.at[b, mine], o_ref.at[b, mine], ssem.at[k], rsem.at[k], sib) for k, b in enumerate(blocks)]
        for cp in sends:
            cp.start()
        for k, b in enumerate(blocks):
            _remote(g_ref.at[b, theirs], o_ref.at[b, theirs], ssem.at[k], rsem.at[k], sib).wait_recv()
        for cp in sends:
            cp.wait_send()

    return pl.pallas_call(
        body, name="forward_to_sibling", in_specs=[ANY], out_specs=ANY,
        out_shape=jax.ShapeDtypeStruct(gath.shape, gath.dtype), input_output_aliases={0: 0},
        scratch_shapes=[pltpu.SemaphoreType.DMA((3,)), pltpu.SemaphoreType.DMA((3,))],
    )(gath)


def _swap_copy(g_ref, r_ref, ssem, rsem):
    x, y, c = _place()
    h = r_ref.shape[1]
    return _remote(g_ref.at[:, pl.ds((1 - c) * h, h), :], r_ref, ssem.at[0], rsem.at[0], (x, y, 1 - c))


def _swap_rider(g):
    def start(rins, routs, ssem, rsem):
        _swap_copy(rins[0], routs[0], ssem, rsem).start()

    def finish(rins, routs, ssem, rsem):
        _swap_copy(rins[0], routs[0], ssem, rsem).wait()

    return _Rider([g], [jax.ShapeDtypeStruct((N_CHIPS, g.shape[1] // 2, g.shape[2]), g.dtype)], {}, 1, start, finish)


def _swap_halves(g):
    def body(g_ref, r_ref, ssem, rsem):
        cp = _swap_copy(g_ref, r_ref, ssem, rsem)
        cp.start()
        cp.wait()

    return pl.pallas_call(
        body, name="swap_halves", in_specs=[ANY], out_specs=ANY,
        out_shape=jax.ShapeDtypeStruct((N_CHIPS, g.shape[1] // 2, g.shape[2]), g.dtype),
        scratch_shapes=[pltpu.SemaphoreType.DMA((1,)), pltpu.SemaphoreType.DMA((1,))],
    )(g)


def _chip_sum(cidx, gslab, recv, name):
    half, C = recv.shape[1:]
    tr = half // 2 if (half // 2) % 16 == 0 else half

    def body(c_ref, g_ref, r_ref, o_ref):
        o_ref[...] = (g_ref[...] + r_ref[...]).astype(bf16)

    return pl.pallas_call(
        body, name=name,
        grid_spec=pltpu.PrefetchScalarGridSpec(
            num_scalar_prefetch=1, grid=(N_CHIPS, half // tr),
            in_specs=[pl.BlockSpec((1, tr, C), lambda b, i, c_ref: (b, c_ref[0] * (half // tr) + i, 0)),
                      pl.BlockSpec((1, tr, C), lambda b, i, c_ref: (b, i, 0))],
            out_specs=pl.BlockSpec((1, tr, C), lambda b, i, c_ref: (b, i, 0))),
        out_shape=jax.ShapeDtypeStruct((N_CHIPS, half, C), bf16),
        compiler_params=_cparams(("parallel", "parallel"), VMEM_MID),
    )(cidx, gslab, recv)


def _exchange_rider(h):
    def copies(rins, routs, ssem, rsem):
        x, y, c = _place()
        for k, (px, py) in enumerate(_other_chips(x, y)):
            yield _remote(rins[0].at[2 * px + py], routs[0].at[k], ssem.at[k], rsem.at[k], (px, py, c))

    def start(*refs):
        for cp in copies(*refs):
            cp.start()

    def finish(*refs):
        for cp in copies(*refs):
            cp.wait()

    return _Rider([h], [jax.ShapeDtypeStruct((3,) + h.shape[1:], h.dtype)], {}, 3, start, finish)


def _exchange(hb, small):
    def body(hb_ref, sm_ref, rb_ref, all_ref, ssem, rsem, lsem):
        x, y, c = _place()
        me = 4 * x + 2 * y + c
        chips = _other_chips(x, y)
        loc = pltpu.make_async_copy(sm_ref, all_ref.at[me], lsem)
        loc.start()
        sends = []
        for k, (px, py) in enumerate(chips):
            sends.append(_remote(hb_ref.at[2 * px + py], rb_ref.at[k], ssem.at[3 + k], rsem.at[3 + k], (px, py, c)))
        peers = []
        for r in range(1, N_DEV):
            peer = ((1 - x) if r & 4 else x, (1 - y) if r & 2 else y, (1 - c) if r & 1 else c)
            peers.append(peer)
            sends.append(_remote(sm_ref, all_ref.at[me], ssem.at[5 + r], rsem.at[5 + r], peer))
        for cp in sends:
            cp.start()
        for k, (px, py) in enumerate(chips):
            _remote(hb_ref.at[0], rb_ref.at[k], ssem.at[3 + k], rsem.at[3 + k], (px, py, c)).wait_recv()
        for r, peer in zip(range(1, N_DEV), peers):
            pid = 4 * peer[0] + 2 * peer[1] + peer[2]
            _remote(sm_ref, all_ref.at[pid], ssem.at[5 + r], rsem.at[5 + r], peer).wait_recv()
        for cp in sends:
            cp.wait_send()
        loc.wait()

    return pl.pallas_call(
        body, name="exchange", in_specs=[ANY, ANY], out_specs=[ANY, ANY],
        out_shape=[jax.ShapeDtypeStruct((3,) + hb.shape[1:], bf16),
                   jax.ShapeDtypeStruct((N_DEV, SMALL_ROWS, D_MODEL), f32)],
        scratch_shapes=[pltpu.SemaphoreType.DMA((13,)), pltpu.SemaphoreType.DMA((13,)), pltpu.SemaphoreType.DMA(())],
    )(hb, small)


def _final_sum(idx, gslab, recv_sib, recv_ici, name):
    half, C = recv_sib.shape[1:]
    tr = half // 2 if (half // 2) % 16 == 0 else half

    def body(i_ref, g_ref, r_ref, p_ref, o_ref):
        acc = g_ref[0] + r_ref[0]
        for k in range(3):
            acc = acc + p_ref[k].astype(f32)
        o_ref[...] = acc

    return pl.pallas_call(
        body, name=name,
        grid_spec=pltpu.PrefetchScalarGridSpec(
            num_scalar_prefetch=1, grid=(half // tr,),
            in_specs=[pl.BlockSpec((1, tr, C), lambda i, s: (s[1], s[0] * (half // tr) + i, 0)),
                      pl.BlockSpec((1, tr, C), lambda i, s: (s[1], i, 0)),
                      pl.BlockSpec((3, tr, C), lambda i, s: (0, i, 0))],
            out_specs=pl.BlockSpec((tr, C), lambda i, s: (s[0] * (half // tr) + i, 0))),
        out_shape=jax.ShapeDtypeStruct((2 * half, C), f32),
        compiler_params=_cparams(("parallel",), VMEM_MID),
    )(idx, gslab, recv_sib, recv_ici)


def _join_halves(ra, rb):
    ha, hb = ra.shape[0] // 2, rb.shape[0] // 2

    def body(a_ref, b_ref, ao_ref, bo_ref, ssem, rsem):
        x, y, c = _place()
        sib = (x, y, 1 - c)
        mine_a, theirs_a = pl.ds(c * ha, ha), pl.ds((1 - c) * ha, ha)
        mine_b, theirs_b = pl.ds(c * hb, hb), pl.ds((1 - c) * hb, hb)
        ca = _remote(a_ref.at[mine_a], ao_ref.at[mine_a], ssem.at[0], rsem.at[0], sib)
        cb = _remote(b_ref.at[mine_b], bo_ref.at[mine_b], ssem.at[1], rsem.at[1], sib)
        ca.start()
        cb.start()
        _remote(a_ref.at[theirs_a], ao_ref.at[theirs_a], ssem.at[0], rsem.at[0], sib).wait_recv()
        _remote(b_ref.at[theirs_b], bo_ref.at[theirs_b], ssem.at[1], rsem.at[1], sib).wait_recv()
        ca.wait_send()
        cb.wait_send()

    return pl.pallas_call(
        body, name="join_halves", in_specs=[ANY, ANY], out_specs=[ANY, ANY],
        out_shape=[jax.ShapeDtypeStruct(ra.shape, f32), jax.ShapeDtypeStruct(rb.shape, f32)],
        input_output_aliases={0: 0, 1: 1},
        scratch_shapes=[pltpu.SemaphoreType.DMA((2,)), pltpu.SemaphoreType.DMA((2,))],
    )(ra, rb)


def _sum_small(all_small):
    def body(a_ref, o_ref):
        acc = a_ref[0]
        for d in range(1, N_DEV):
            acc = acc + a_ref[d]
        o_ref[...] = acc

    return pl.pallas_call(
        body, name="sum_small", out_shape=jax.ShapeDtypeStruct((SMALL_ROWS, D_MODEL), f32),
    )(all_small)


def _adamw(w, g, m, v, name, g_off=0, by_columns=False):
    R, C = w.shape
    tr = 256 if R % 256 == 0 else R
    assert g_off % tr == 0 and not (by_columns and g_off)
    c1 = 1.0 - ADAM_B1 ** ADAM_STEP
    c2 = 1.0 - ADAM_B2 ** ADAM_STEP

    def body(w_ref, g_ref, m_ref, v_ref, d_ref, mo_ref, vo_ref):
        gg = g_ref[...]
        m2 = ADAM_B1 * m_ref[...] + (1.0 - ADAM_B1) * gg
        v2 = ADAM_B2 * v_ref[...] + (1.0 - ADAM_B2) * (gg * gg)
        mo_ref[...] = m2
        vo_ref[...] = v2
        d_ref[...] = -ADAM_LR * ((m2 / c1) / (jnp.sqrt(v2 / c2) + ADAM_EPS) + ADAM_WD * w_ref[...])

    if by_columns:
        blk = gblk = pl.BlockSpec((R, LANES), lambda i: (0, i))
        grid = (C // LANES,)
    else:
        blk = pl.BlockSpec((tr, C), lambda i: (i, 0))
        gblk = pl.BlockSpec((tr, C), lambda i: (g_off // tr + i, 0))
        grid = (R // tr,)
    shp = jax.ShapeDtypeStruct((R, C), f32)
    return pl.pallas_call(
        body, name=name, grid=grid, in_specs=[blk, gblk, blk, blk], out_specs=[blk] * 3, out_shape=[shp] * 3,
        compiler_params=_cparams(("parallel",), VMEM_MID),
    )(w, g, m, v)


def _pad_lanes(v):
    return jnp.pad(v, ((0, 0), (0, LANES - v.shape[1])))


def _local_step(x, p, tgt, gath0, cidx, wt, S):
    dtb = _pad_lanes(S["dt_bias"])
    alog = _pad_lanes(S["A_log"])
    dskip_e = jnp.repeat(S["D_skip"], HEAD_DIM, axis=1)

    early = GATHER_EARLY_ROWS
    u0, z, xbc, cv, cg, dtr, v, gath1 = _in_proj_fwd(x, S["mix_norm_g"], wt, rider=_gather_rider(gath0, 0, early))
    co, yc, gath = _conf_fwd(v, S["conf_dw_w"], S["conf_dw_b"], S["conf_ln_g"], S["conf_ln_b"],
                             rider=_gather_rider(gath1, early, SLAB_A // 2 - early))
    gath = _forward_to_sibling(gath)
    w_ple = jnp.concatenate([_ple_of_slab(gath[b]) for b in range(N_CHIPS)], axis=1)
    pre = _ssd_conv_fwd(xbc, S["ssd_conv_w"], S["ssd_conv_b"])
    y, ys, sprev = _ssd_fwd(pre, dtr, z, dtb, alog, dskip_e, S["ssd_norm_g"])
    h1, u1 = _out_proj_fwd(x, ys, yc, gath, S["mlp_norm_g"])
    r, h2, u2 = _mlp_fwd(h1, u1, gath, S["ple_gate_norm_g"])
    loss, dh2, dh2b, dgp, dep, dg_fin, dg_ple, db_pg, dg_pg = _ple_loss(
        h2, u2, p, tgt, gath, S["b_ple_gate"], w_ple, S["ple_norm_g"], S["final_norm_g"], S["ple_gate_norm_g"])

    npg = D_MODEL // N_CHIPS
    ga = lax.empty((N_CHIPS, SLAB_A, D_MODEL), f32)
    ga = _weight_grad(u2, dgp, "dw_ple_gate", slab=ga, tk=npg, place=lambda i, j: (i, PG_OFF // npg, j))
    ga = _weight_grad(r, dh2b, "dw_down", square=True, slab=ga, place=lambda i, j: (i // 2, DOWN_OFF // 512 + i % 2, j))
    gw_ple = _weight_grad(p, dep, "dw_ple")
    dhp, dh1, dh1b, dg_mlp = _mlp_bwd(dh2, r, gath, h1, S["mlp_norm_g"])
    ga = _weight_grad(u1, dhp, "dw_up", slab=ga, place=lambda i, j: (j, UP_OFF // 512 + i, 0))
    ga = _weight_grad(ys, dh1b, "dw_out_ssd", slab=ga, place=lambda i, j: (i, OUT_OFF // 512, j))
    ga = _weight_grad(yc, dh1b, "dw_out_conf", slab=ga, place=lambda i, j: (2 + i, OUT_OFF // 512, j))
    n_ple = D_MODEL // N_CHIPS
    ple_rows = jnp.stack([_rows(gw_ple[:, b * n_ple:(b + 1) * n_ple]) for b in range(N_CHIPS)], axis=0)
    ga = lax.dynamic_update_slice(ga, ple_rows, (0, PLE_OFF, 0))
    dys, dco, dg_ln, db_ln, recv_a = _out_proj_bwd(dh1, gath, co, S["conf_ln_g"], S["conf_ln_b"], rider=_swap_rider(ga))
    ha = _chip_sum(cidx, ga, recv_a, "chip_sum_a")
    dcv, dcg, dw_conf, db_conf, ici_a = _conf_conv_bwd(dco, v, S["conf_dw_w"], cv, cg, rider=_exchange_rider(ha))
    dz, dpre, ddtr, dg_ssdn, dd, dal, ddtb = _ssd_bwd(dys, y, z, pre, dtr, sprev, dtb, alog, dskip_e, S["ssd_norm_g"])
    dxbc, dw_sconv, db_sconv = _ssd_conv_bwd(dpre, xbc, S["ssd_conv_w"])
    gx, dg_mix = _in_proj_bwd(dz, dxbc, dcv, dcg, ddtr, wt, x, dh1, S["mix_norm_g"])

    gw_in = jnp.concatenate([
        _weight_grad(dz, u0, "dw_in_z"), _weight_grad(dxbc, u0, "dw_in_xbc"),
        _weight_grad(ddtr, u0, "dw_in_dt")[:SSD_HEADS],
        _weight_grad(dcv, u0, "dw_in_cv"), _weight_grad(dcg, u0, "dw_in_cg")], axis=0)
    small = {
        "mix_norm_g": dg_mix, "ssd_conv_w": dw_sconv[:SSD_CONV], "ssd_conv_b": db_sconv,
        "dt_bias": ddtb[:, :SSD_HEADS], "A_log": dal[:, :SSD_HEADS], "D_skip": dd[:, :SSD_HEADS],
        "ssd_norm_g": dg_ssdn, "conf_dw_w": dw_conf[:CONF_KERNEL], "conf_dw_b": db_conf,
        "conf_ln_g": dg_ln, "conf_ln_b": db_ln, "mlp_norm_g": dg_mlp, "ple_gate_norm_g": dg_pg,
        "b_ple_gate": db_pg, "ple_norm_g": dg_ple, "final_norm_g": dg_fin,
    }
    return loss, gx, ga, recv_a, ici_a, gw_in, small


def _rows(a):
    return a.reshape(-1, D_MODEL)


def _pad_rows(a, n):
    flat = a.reshape(-1)
    return jnp.pad(flat, (0, n * D_MODEL - flat.shape[0])).reshape(n, D_MODEL)


def _ple_of_slab(slab):
    return slab[PLE_OFF:PLE_OFF + PLE_ROWS].reshape(PLE_DIM, D_MODEL // N_CHIPS)


SMALL_LAYOUT = (("mix_norm_g", 1), ("ssd_norm_g", 1), ("conf_dw_b", 1), ("conf_ln_g", 1), ("conf_ln_b", 1),
                ("mlp_norm_g", 1), ("ple_gate_norm_g", 1), ("b_ple_gate", 1), ("ple_norm_g", 1), ("final_norm_g", 1),
                ("ssd_conv_b", 2), ("dt_bias", 1), ("A_log", 1), ("D_skip", 1), ("loss", 1),
                ("ssd_conv_w", 6), ("conf_dw_w", 31))


def _pack_small(d):
    parts = [_pad_rows(d[n], r) for n, r in SMALL_LAYOUT]
    used = sum(r for _, r in SMALL_LAYOUT)
    parts.append(jnp.zeros((SMALL_ROWS - used, D_MODEL), f32))
    return jnp.concatenate(parts, axis=0)


def _unpack_small(a, shapes):
    out, o = {}, 0
    for n, r in SMALL_LAYOUT:
        shp = shapes[n]
        size = 1
        for s in shp:
            size *= s
        out[n] = a[o:o + r].reshape(-1)[:size].reshape(shp)
        o += r
    return out


BIG = ("w_in", "w_out", "w_up", "w_down", "w_ple_gate", "w_ple")
BIG_A = (("w_up", UP_OFF), ("w_down", DOWN_OFF), ("w_out", OUT_OFF), ("w_ple_gate", PG_OFF))
WEIGHTS = ("mix_norm_g", "w_in", "ssd_conv_w", "ssd_conv_b", "dt_bias", "A_log", "D_skip", "ssd_norm_g", "conf_dw_w",
           "conf_dw_b", "conf_ln_g", "conf_ln_b", "w_out", "mlp_norm_g", "w_up", "w_down", "ple_gate_norm_g",
           "w_ple_gate", "b_ple_gate", "w_ple", "ple_norm_g", "final_norm_g")


def kernel(x, p, mix_norm_g, w_in, ssd_conv_w, ssd_conv_b, dt_bias, A_log, D_skip, ssd_norm_g, conf_dw_w, conf_dw_b, conf_ln_g, conf_ln_b, w_out, mlp_norm_g, w_up, w_down, ple_gate_norm_g, w_ple_gate, b_ple_gate, w_ple, ple_norm_g, final_norm_g, loss_target, m_mix_norm_g, m_w_in, m_ssd_conv_w, m_ssd_conv_b, m_dt_bias, m_A_log, m_D_skip, m_ssd_norm_g, m_conf_dw_w, m_conf_dw_b, m_conf_ln_g, m_conf_ln_b, m_w_out, m_mlp_norm_g, m_w_up, m_w_down, m_ple_gate_norm_g, m_w_ple_gate, m_b_ple_gate, m_w_ple, m_ple_norm_g, m_final_norm_g, v_mix_norm_g, v_w_in, v_ssd_conv_w, v_ssd_conv_b, v_dt_bias, v_A_log, v_D_skip, v_ssd_norm_g, v_conf_dw_w, v_conf_dw_b, v_conf_ln_g, v_conf_ln_b, v_w_out, v_mlp_norm_g, v_w_up, v_w_down, v_ple_gate_norm_g, v_w_ple_gate, v_b_ple_gate, v_w_ple, v_ple_norm_g, v_final_norm_g):
    w = dict(mix_norm_g=mix_norm_g, w_in=w_in, ssd_conv_w=ssd_conv_w, ssd_conv_b=ssd_conv_b, dt_bias=dt_bias, A_log=A_log,
             D_skip=D_skip, ssd_norm_g=ssd_norm_g, conf_dw_w=conf_dw_w, conf_dw_b=conf_dw_b, conf_ln_g=conf_ln_g,
             conf_ln_b=conf_ln_b, w_out=w_out, mlp_norm_g=mlp_norm_g, w_up=w_up, w_down=w_down,
             ple_gate_norm_g=ple_gate_norm_g, w_ple_gate=w_ple_gate, b_ple_gate=b_ple_gate, w_ple=w_ple,
             ple_norm_g=ple_norm_g, final_norm_g=final_norm_g)
    m = dict(mix_norm_g=m_mix_norm_g, w_in=m_w_in, ssd_conv_w=m_ssd_conv_w, ssd_conv_b=m_ssd_conv_b, dt_bias=m_dt_bias,
             A_log=m_A_log, D_skip=m_D_skip, ssd_norm_g=m_ssd_norm_g, conf_dw_w=m_conf_dw_w, conf_dw_b=m_conf_dw_b,
             conf_ln_g=m_conf_ln_g, conf_ln_b=m_conf_ln_b, w_out=m_w_out, mlp_norm_g=m_mlp_norm_g, w_up=m_w_up,
             w_down=m_w_down, ple_gate_norm_g=m_ple_gate_norm_g, w_ple_gate=m_w_ple_gate, b_ple_gate=m_b_ple_gate,
             w_ple=m_w_ple, ple_norm_g=m_ple_norm_g, final_norm_g=m_final_norm_g)
    v = dict(mix_norm_g=v_mix_norm_g, w_in=v_w_in, ssd_conv_w=v_ssd_conv_w, ssd_conv_b=v_ssd_conv_b, dt_bias=v_dt_bias,
             A_log=v_A_log, D_skip=v_D_skip, ssd_norm_g=v_ssd_norm_g, conf_dw_w=v_conf_dw_w, conf_dw_b=v_conf_dw_b,
             conf_ln_g=v_conf_ln_g, conf_ln_b=v_conf_ln_b, w_out=v_w_out, mlp_norm_g=v_mlp_norm_g, w_up=v_w_up,
             w_down=v_w_down, ple_gate_norm_g=v_ple_gate_norm_g, w_ple_gate=v_w_ple_gate, b_ple_gate=v_b_ple_gate,
             w_ple=v_w_ple, ple_norm_g=v_ple_norm_g, final_norm_g=v_final_norm_g)
    xi, yi, ci = lax.axis_index("x"), lax.axis_index("y"), lax.axis_index("c")
    chip = 2 * xi + yi

    slab = jnp.concatenate([w_up[0], w_down[0], w_out[0], w_ple_gate[0], _rows(w_ple[0])], axis=0).astype(bf16)
    gath0 = lax.dynamic_update_slice(jnp.zeros((N_CHIPS, SLAB_A, D_MODEL), bf16), slab[None], (chip, 0, 0))
    wt_shard = jnp.swapaxes(w_in, 1, 2).astype(bf16)
    gin0 = lax.dynamic_update_slice(jnp.zeros((N_CHIPS, W_IN_ROWS_PAD, D_MODEL), bf16), wt_shard, (chip, 0, 0))
    convw = _pad_rows(jnp.concatenate([ssd_conv_w[0].reshape(-1), conf_dw_w[0].reshape(-1)]), CONVW_ROWS)
    gin, cwg = _gather_weights([gin0], convw)
    wt = jnp.concatenate([gin[b, :W_IN_ROWS] for b in range(N_CHIPS)], axis=0)
    n_sc = SSD_CONV * (XBC_WIDTH // N_CHIPS)
    n_cf = CONF_KERNEL * (CONF_WIDTH // N_CHIPS)
    S = {n: w[n][0] for n in ("mix_norm_g", "ssd_conv_b", "dt_bias", "A_log", "D_skip", "ssd_norm_g", "conf_dw_b",
                              "conf_ln_g", "conf_ln_b", "mlp_norm_g", "ple_gate_norm_g", "b_ple_gate", "ple_norm_g")}
    S = {n: a.reshape(1, -1) for n, a in S.items()}
    S["final_norm_g"] = final_norm_g.reshape(1, -1)
    S["ssd_conv_w"] = jnp.concatenate(
        [cwg[b].reshape(-1)[:n_sc].reshape(SSD_CONV, XBC_WIDTH // N_CHIPS) for b in range(N_CHIPS)], axis=1)
    S["conf_dw_w"] = jnp.concatenate(
        [cwg[b].reshape(-1)[n_sc:n_sc + n_cf].reshape(CONF_KERNEL, CONF_WIDTH // N_CHIPS) for b in range(N_CHIPS)], axis=1)

    cidx = jnp.stack([ci, chip]).astype(jnp.int32)
    loss8, grad_x, ga, recv_a, ici_a, gw_in, gsmall = _local_step(x[0], p[0, 0], loss_target[0], gath0, cidx, wt, S)

    n_in = IN_WIDTH // N_CHIPS
    gb = jnp.pad(gw_in.reshape(N_CHIPS, n_in, D_MODEL), ((0, 0), (0, W_IN_ROWS_PAD - n_in), (0, 0)))
    gsmall = dict(gsmall)
    gsmall["loss"] = loss8[0:1, 0:1]
    small = _pack_small(gsmall)

    recv_b = _swap_halves(gb)
    hb = _chip_sum(cidx, gb, recv_b, "chip_sum_b")
    ici_b, all_small = _exchange(hb, small)
    ra = _final_sum(cidx, ga, recv_a, ici_a, "final_sum_a")
    rb = _final_sum(cidx, gb, recv_b, ici_b, "final_sum_b")
    ra, rb = _join_halves(ra, rb)
    tot_small = _sum_small(all_small)

    shapes = {n: (tuple(w[n].shape[1:]) if n != "final_norm_g" else (D_MODEL,)) for n in WEIGHTS if n not in BIG}
    shapes["ssd_conv_w"] = (SSD_CONV, XBC_WIDTH)
    shapes["conf_dw_w"] = (CONF_KERNEL, CONF_WIDTH)
    shapes["loss"] = (1,)
    tot = _unpack_small(tot_small, shapes)
    loss = tot["loss"].reshape(())
    n1, n2 = XBC_WIDTH // N_CHIPS, CONF_WIDTH // N_CHIPS
    tot["ssd_conv_w"] = lax.dynamic_slice(tot["ssd_conv_w"], (0, chip * n1), (SSD_CONV, n1))
    tot["conf_dw_w"] = lax.dynamic_slice(tot["conf_dw_w"], (0, chip * n2), (CONF_KERNEL, n2))

    g_in_t = rb[:W_IN_ROWS]
    grads = {"w_ple": _ple_of_slab(ra), "w_in": jnp.swapaxes(g_in_t, 0, 1)}
    for n, off in BIG_A:
        grads[n] = ra[off:off + w[n].shape[1]]
    for n in WEIGHTS:
        if n not in BIG:
            grads[n] = tot[n]
    grads = {n: g.reshape(w[n].shape) for n, g in grads.items()}

    delta, new_m, new_v = {}, {}, {}
    for n, off in BIG_A:
        d_, m_, v_ = _adamw(w[n][0], ra, m[n][0], v[n][0], "adamw_" + n, g_off=off)
        delta[n], new_m[n], new_v[n] = d_[None], m_[None], v_[None]
    d_, m_, v_ = _adamw(w_ple[0], grads["w_ple"][0], m_w_ple[0], v_w_ple[0], "adamw_w_ple")
    delta["w_ple"], new_m["w_ple"], new_v["w_ple"] = d_[None], m_[None], v_[None]
    tr_ = lambda a: jnp.swapaxes(a[0], 0, 1)
    d_, m_, v_ = _adamw(tr_(w_in), g_in_t, tr_(m_w_in), tr_(v_w_in), "adamw_w_in", by_columns=True)
    delta["w_in"], new_m["w_in"], new_v["w_in"] = (jnp.swapaxes(a, 0, 1)[None] for a in (d_, m_, v_))
    small_names = [n for n in WEIGHTS if n not in BIG]
    sizes = {n: int(w[n].size) for n in small_names}
    rows_needed = sum(-(-sizes[n] // D_MODEL) for n in small_names)
    rows_pad = -(-rows_needed // 8) * 8

    def pack(d):
        parts = [_pad_rows(d[n], -(-sizes[n] // D_MODEL)) for n in small_names]
        parts.append(jnp.zeros((rows_pad - rows_needed, D_MODEL), f32))
        return jnp.concatenate(parts, axis=0)

    sd, sm, sv = _adamw(pack(w), pack(grads), pack(m), pack(v), "adamw_small")

    def unpack(a, n, o):
        r = -(-sizes[n] // D_MODEL)
        return a[o:o + r].reshape(-1)[:sizes[n]].reshape(w[n].shape), o + r

    o = 0
    for n in small_names:
        delta[n], _ = unpack(sd, n, o)
        new_m[n], _ = unpack(sm, n, o)
        new_v[n], o = unpack(sv, n, o)

    return (loss, grad_x[None], *[grads[n] for n in WEIGHTS], *[delta[n] for n in WEIGHTS],
            *[new_m[n] for n in WEIGHTS], *[new_v[n] for n in WEIGHTS])
```

```python
import jax
import jax.numpy as jnp
from jax import lax
from jax.experimental import pallas as pl
from jax.experimental.pallas import tpu as pltpu

f32 = jnp.float32
bf16 = jnp.bfloat16

D_MODEL = 1024
SSD_WIDTH = 1024
SSD_HEADS = 16
HEAD_DIM = 64
SSD_STATE = 128
XBC_WIDTH = 1536
SSD_CONV = 4
CHUNK = 128
CONF_WIDTH = 1024
CONF_KERNEL = 31
D_FF = 4096
PLE_DIM = 256
IN_WIDTH = 4624
EPS = 1e-6
N_CHIPS = 4
N_DEV = 8

ADAM_LR = 0.001
ADAM_B1 = 0.9
ADAM_B2 = 0.999
ADAM_EPS = 1e-08
ADAM_WD = 0.01
ADAM_STEP = 10

LANES = 128
VMEM_BIG = 56 * 1024 * 1024
VMEM_MID = 40 * 1024 * 1024

UP_OFF, DOWN_OFF, OUT_OFF, PG_OFF, PLE_OFF = 0, 1024, 2048, 2560, 2816
PLE_ROWS = 64
SLAB_A = PLE_OFF + PLE_ROWS
GATHER_EARLY_ROWS = 480
W_IN_ROWS = 1156
W_IN_ROWS_PAD = 1184
CONVW_ROWS = 16
SMALL_ROWS = 56

MESH = pl.DeviceIdType.MESH
ANY = pl.BlockSpec(memory_space=pl.ANY)


def _cparams(sem=None, vmem=None):
    return pltpu.CompilerParams(dimension_semantics=sem, vmem_limit_bytes=vmem)


def _full(shape):
    n = len(shape)
    return pl.BlockSpec(shape, lambda *_: (0,) * n)


class _Rider:
    def __init__(self, inputs, out_shapes, aliases, n_sems, start, finish):
        self.inputs, self.out_shapes, self.aliases = list(inputs), list(out_shapes), dict(aliases)
        self.n_sems, self.start, self.finish = n_sems, start, finish


def _call(body, args, *, name, grid, in_specs, out_specs, out_shape, scratch_shapes=(), params=None, rider=None):
    if rider is None:
        return pl.pallas_call(body, name=name, grid=grid, in_specs=in_specs, out_specs=out_specs, out_shape=out_shape,
                              scratch_shapes=list(scratch_shapes), compiler_params=params)(*args)
    ni, no, ns = len(in_specs), len(out_specs), len(scratch_shapes)
    ri, ro = len(rider.inputs), len(rider.out_shapes)
    (steps,) = grid

    def with_rider(*refs):
        ins, refs = refs[:ni], refs[ni:]
        rins, refs = refs[:ri], refs[ri:]
        outs, refs = refs[:no], refs[no:]
        routs, refs = refs[:ro], refs[ro:]
        scratch, (ssem, rsem) = refs[:ns], refs[ns:]
        step = pl.program_id(0)

        @pl.when(step == 0)
        def _():
            rider.start(rins, routs, ssem, rsem)

        body(*ins, *outs, *scratch)

        @pl.when(step == steps - 1)
        def _():
            rider.finish(rins, routs, ssem, rsem)

    sems = [pltpu.SemaphoreType.DMA((rider.n_sems,)), pltpu.SemaphoreType.DMA((rider.n_sems,))]
    return pl.pallas_call(
        with_rider, name=name, grid=grid, in_specs=list(in_specs) + [ANY] * ri, out_specs=list(out_specs) + [ANY] * ro,
        out_shape=list(out_shape) + rider.out_shapes, scratch_shapes=list(scratch_shapes) + sems,
        input_output_aliases={ni + a: no + b for a, b in rider.aliases.items()}, compiler_params=params,
    )(*args, *rider.inputs)


def _dot(a, b):
    return jnp.dot(a, b, preferred_element_type=f32)


def _dot_nt(a, b):
    return lax.dot_general(a, b, (((1,), (1,)), ((), ())), preferred_element_type=f32)


def _dot_tn(a, b):
    return lax.dot_general(a, b, (((0,), (0,)), ((), ())), preferred_element_type=f32)


def _sigmoid(x):
    return jax.nn.sigmoid(x)


def _rms(x, g):
    r = lax.rsqrt(jnp.mean(x * x, axis=-1, keepdims=True) + EPS)
    return x * r * g


def _rms_bwd(dy, x, g):
    r = lax.rsqrt(jnp.mean(x * x, axis=-1, keepdims=True) + EPS)
    xh = x * r
    dg = jnp.sum(dy * xh, axis=0, keepdims=True)
    dxh = dy * g
    dx = r * (dxh - xh * jnp.mean(dxh * xh, axis=-1, keepdims=True))
    return dx, dg


def _dsilu(x):
    s = _sigmoid(x)
    return s * (1.0 + x * (1.0 - s))


def _split3(x):
    hi = x.astype(bf16)
    r1 = x - hi.astype(f32)
    mid = r1.astype(bf16)
    lo = (r1 - mid.astype(f32)).astype(bf16)
    return hi, mid, lo


def _head_matrix():
    row = lax.broadcasted_iota(jnp.int32, (LANES, SSD_WIDTH), 0)
    col = lax.broadcasted_iota(jnp.int32, (LANES, SSD_WIDTH), 1)
    lo = row * HEAD_DIM
    return ((col >= lo) & (col < lo + HEAD_DIM)).astype(bf16)


def _expand(x, e):
    hi, mid, lo = _split3(x)
    return _dot(hi, e) + _dot(mid, e) + _dot(lo, e)


def _contract(x, e):
    hi, mid, lo = _split3(x)
    return _dot_nt(hi, e) + _dot_nt(mid, e) + _dot_nt(lo, e)


O_XBC = SSD_WIDTH
O_DT = O_XBC + XBC_WIDTH
O_CV = O_DT + SSD_HEADS
O_CG = O_CV + CONF_WIDTH


def _assemble_w_in_t(gin_ref, wt_ref):
    for b in range(N_CHIPS):
        wt_ref[b * W_IN_ROWS:(b + 1) * W_IN_ROWS, :] = gin_ref[b, 0:W_IN_ROWS, :]


def _in_proj_fwd(x, g, gin, rider=None):
    T = x.shape[0]
    tm = min(256, T)

    def body(x_ref, g_ref, gin_ref, u_ref, z_ref, xbc_ref, cv_ref, cg_ref, dt_ref, v_ref, wt_ref):
        @pl.when(pl.program_id(0) == 0)
        def _():
            _assemble_w_in_t(gin_ref, wt_ref)

        ub = _rms(x_ref[...], g_ref[...]).astype(bf16)
        u_ref[...] = ub
        z_ref[...] = _dot_nt(ub, wt_ref[0:O_XBC, :])
        xbc_ref[...] = _dot_nt(ub, wt_ref[O_XBC:O_DT, :])
        cv = _dot_nt(ub, wt_ref[O_CV:O_CG, :])
        cg = _dot_nt(ub, wt_ref[O_CG:IN_WIDTH, :])
        cv_ref[...] = cv
        cg_ref[...] = cg
        v_ref[...] = cv * _sigmoid(cg)
        dt_ref[...] = _dot_nt(ub, wt_ref[O_DT:O_DT + LANES, :])

    row = lambda n: pl.BlockSpec((tm, n), lambda i: (i, 0))
    return _call(
        body, (x, g, gin), name="in_proj_fwd", grid=(T // tm,),
        in_specs=[row(D_MODEL), _full((1, D_MODEL)), _full(gin.shape)],
        out_specs=[row(D_MODEL), row(SSD_WIDTH), row(XBC_WIDTH), row(CONF_WIDTH), row(CONF_WIDTH), row(LANES),
                   row(CONF_WIDTH)],
        out_shape=[jax.ShapeDtypeStruct((T, D_MODEL), bf16), jax.ShapeDtypeStruct((T, SSD_WIDTH), f32),
                   jax.ShapeDtypeStruct((T, XBC_WIDTH), f32), jax.ShapeDtypeStruct((T, CONF_WIDTH), f32),
                   jax.ShapeDtypeStruct((T, CONF_WIDTH), f32), jax.ShapeDtypeStruct((T, LANES), f32),
                   jax.ShapeDtypeStruct((T, CONF_WIDTH), f32)],
        scratch_shapes=[pltpu.VMEM((IN_WIDTH, D_MODEL), bf16)],
        params=_cparams(("arbitrary",), VMEM_BIG), rider=rider)


SUBLANES = 8


def _phases(offsets):
    return sorted({o % SUBLANES for o in offsets} - {0})


def _phase_shape(offsets, tm, C):
    a_max = max([o // SUBLANES for o in offsets if o % SUBLANES] or [0])
    return (max(len(_phases(offsets)), 1), tm + SUBLANES * a_max, C)


def _make_phases(buf_ref, ph_ref, offsets, tm):
    for idx, b in enumerate(_phases(offsets)):
        n = tm + SUBLANES * max(o // SUBLANES for o in offsets if o % SUBLANES == b)
        ph_ref[idx, 0:n, :] = buf_ref[pl.ds(b, n), :]


def _window(buf_ref, ph_ref, offsets, o, r0, rb):
    a, b = divmod(o, SUBLANES)
    if b == 0:
        return buf_ref[pl.ds(r0 + SUBLANES * a, rb), :]
    return ph_ref[_phases(offsets).index(b), pl.ds(r0 + SUBLANES * a, rb), :]


def _conv_rows(wb_ref, buf_ref, ph_ref, offsets, r0, rb):
    nsub = rb // SUBLANES
    accs = [None] * nsub
    for k, o in enumerate(offsets):
        wk = wb_ref[pl.ds(SUBLANES * k, SUBLANES), :]
        for s in range(nsub):
            term = wk * _window(buf_ref, ph_ref, offsets, o, r0 + SUBLANES * s, SUBLANES)
            accs[s] = term if accs[s] is None else accs[s] + term
    return accs[0] if nsub == 1 else jnp.concatenate(accs, axis=0)


def _sublane_rows(w):
    return jnp.repeat(w, SUBLANES, axis=0)


def _fwd_offsets(K, hb):
    return [hb - (K - 1) + k for k in range(K)]


def _prev_halo_spec(hb, tm, C):
    return pl.BlockSpec((hb, C), lambda i: (jnp.maximum(i * (tm // hb) - 1, 0), 0))


CONV_RB = 16


def _ssd_conv_fwd(xbc, w, b):
    T, C = xbc.shape
    K, hb = SSD_CONV, 8
    tm = min(256, T)
    offs = _fwd_offsets(K, hb)

    def body(cur_ref, halo_ref, w_ref, b_ref, pre_ref, buf_ref, ph_ref):
        keep = jnp.where(pl.program_id(0) > 0, 1.0, 0.0)
        buf_ref[0:hb, :] = halo_ref[...] * keep
        buf_ref[hb:hb + tm, :] = cur_ref[...]
        _make_phases(buf_ref, ph_ref, offs, tm)

        def chunk(i, carry):
            r0 = pl.multiple_of(i * CONV_RB, CONV_RB)
            pre_ref[pl.ds(r0, CONV_RB), :] = _conv_rows(w_ref, buf_ref, ph_ref, offs, r0, CONV_RB) + b_ref[...]
            return carry

        lax.fori_loop(0, tm // CONV_RB, chunk, 0)

    return pl.pallas_call(
        body, name="ssd_conv_fwd", grid=(T // tm,),
        in_specs=[pl.BlockSpec((tm, C), lambda i: (i, 0)), _prev_halo_spec(hb, tm, C), _full((SUBLANES * K, C)),
                  _full((1, C))],
        out_specs=pl.BlockSpec((tm, C), lambda i: (i, 0)),
        out_shape=jax.ShapeDtypeStruct((T, C), f32),
        scratch_shapes=[pltpu.VMEM((hb + tm, C), f32), pltpu.VMEM(_phase_shape(offs, tm, C), f32)],
        compiler_params=_cparams(("parallel",), VMEM_MID),
    )(xbc, xbc, _sublane_rows(w), b)


def _conf_fwd(v, w, b, ln_g, ln_b, rider=None):
    T, C = v.shape
    K, hb = CONF_KERNEL, 32
    tm = min(256, T)
    offs = _fwd_offsets(K, hb)
    rb = 2 * CONV_RB

    def body(cur_ref, halo_ref, w_ref, b_ref, g_ref, bb_ref, co_ref, y_ref, buf_ref, ph_ref):
        keep = jnp.where(pl.program_id(0) > 0, 1.0, 0.0)
        buf_ref[0:hb, :] = halo_ref[...] * keep
        buf_ref[hb:hb + tm, :] = cur_ref[...]
        _make_phases(buf_ref, ph_ref, offs, tm)

        def chunk(i, carry):
            r0 = pl.multiple_of(i * rb, rb)
            co = _conv_rows(w_ref, buf_ref, ph_ref, offs, r0, rb) + b_ref[...]
            co_ref[pl.ds(r0, rb), :] = co
            mu = jnp.mean(co, axis=-1, keepdims=True)
            xc = co - mu
            yn = xc * lax.rsqrt(jnp.mean(xc * xc, axis=-1, keepdims=True) + EPS) * g_ref[...] + bb_ref[...]
            y_ref[pl.ds(r0, rb), :] = (yn * _sigmoid(yn)).astype(bf16)
            return carry

        lax.fori_loop(0, tm // rb, chunk, 0)

    return _call(
        body, (v, v, _sublane_rows(w), b, ln_g, ln_b), name="conf_fwd", grid=(T // tm,),
        in_specs=[pl.BlockSpec((tm, C), lambda i: (i, 0)), _prev_halo_spec(hb, tm, C), _full((SUBLANES * K, C)),
                  _full((1, C)), _full((1, C)), _full((1, C))],
        out_specs=[pl.BlockSpec((tm, C), lambda i: (i, 0)), pl.BlockSpec((tm, C), lambda i: (i, 0))],
        out_shape=[jax.ShapeDtypeStruct((T, C), f32), jax.ShapeDtypeStruct((T, C), bf16)],
        scratch_shapes=[pltpu.VMEM((hb + tm, C), f32), pltpu.VMEM(_phase_shape(offs, tm, C), f32)],
        params=_cparams(("arbitrary",), VMEM_MID), rider=rider)


def _ssd_chunk_common(pre, dtr, dtb, alog, e):
    act = pre * _sigmoid(pre)
    xs = act[:, :SSD_WIDTH]
    bm = act[:, SSD_WIDTH:SSD_WIDTH + 2 * SSD_STATE]
    cm = act[:, SSD_WIDTH + 2 * SSD_STATE:]
    row = lax.broadcasted_iota(jnp.int32, (CHUNK, CHUNK), 0)
    col = lax.broadcasted_iota(jnp.int32, (CHUNK, CHUNK), 1)
    tri = row >= col
    dt = jax.nn.softplus(dtr + dtb)
    a_neg = -jnp.exp(alog)
    a = dt * a_neg
    cs = jnp.dot(tri.astype(f32), a, precision=lax.Precision.HIGHEST, preferred_element_type=f32)
    cs_e = _expand(cs, e)
    dt_e = _expand(dt, e)
    csl_e = cs_e[CHUNK - 1:CHUNK, :]
    ecs_e = jnp.exp(cs_e)
    dte_e = jnp.exp(csl_e - cs_e)
    cd_e = jnp.exp(csl_e)
    xc = xs * dt_e
    xd = xc * dte_e
    return dict(xs=xs, bm=bm, cm=cm, tri=tri, dt=dt, a_neg=a_neg, cs=cs, ecs_e=ecs_e, dte_e=dte_e, cd_e=cd_e,
                dt_e=dt_e, xc=xc, xd=xd)


def _group(v, g, width):
    return v[:, g * width:(g + 1) * width]


def _ssd_fwd(pre, dtr, z, dtb, alog, dskip_e, gn):
    T = pre.shape[0]
    nc = T // CHUNK
    GW = SSD_WIDTH // 2

    def body(pre_ref, dtr_ref, z_ref, dtb_ref, alog_ref, de_ref, gn_ref, y_ref, ys_ref, sp_ref, st_ref):
        @pl.when(pl.program_id(0) == 0)
        def _():
            st_ref[...] = jnp.zeros_like(st_ref)

        e = _head_matrix()
        q = _ssd_chunk_common(pre_ref[...], dtr_ref[...], dtb_ref[...], alog_ref[...], e)
        cs, tri, xc, xd = q["cs"], q["tri"], q["xc"], q["xd"]
        cs_t = cs.T
        st = st_ref[...]
        sp_ref[0] = st
        lane = lax.broadcasted_iota(jnp.int32, (1, LANES), 1)
        halves = (lane < HEAD_DIM, lane >= HEAD_DIM)

        g_mat, y_off, s_new = [], [], []
        for g in range(2):
            bg = _group(q["bm"], g, SSD_STATE)
            cg = _group(q["cm"], g, SSD_STATE)
            bgb, cgb = bg.astype(bf16), cg.astype(bf16)
            g_mat.append(_dot_nt(cgb, bgb))
            y_off.append(_dot(cgb, _group(st, g, GW).astype(bf16)))
            s_new.append(_dot(bg.T.astype(bf16), _group(xd, g, GW).astype(bf16)))
        y_off = jnp.concatenate(y_off, axis=1) * q["ecs_e"]
        st_ref[...] = st * q["cd_e"] + jnp.concatenate(s_new, axis=1)

        pairs = []
        for j in range(SSD_HEADS // 2):
            xp = xc[:, j * LANES:(j + 1) * LANES]
            acc = jnp.zeros((CHUNK, LANES), f32)
            for hh in range(2):
                h = 2 * j + hh
                seg = cs[:, h:h + 1] - cs_t[h:h + 1, :]
                lm = jnp.exp(jnp.where(tri, seg, -1e30))
                m = (g_mat[h // 8] * lm).astype(bf16)
                acc = acc + _dot(m, jnp.where(halves[hh], xp, 0.0).astype(bf16))
            pairs.append(acc)
        y = jnp.concatenate(pairs, axis=1) + y_off + q["xs"] * de_ref[...]
        y_ref[...] = y

        zz = z_ref[...]
        v = y * (zz * _sigmoid(zz))
        outs = []
        for g in range(2):
            vg = _group(v, g, GW)
            outs.append(vg * lax.rsqrt(jnp.mean(vg * vg, axis=-1, keepdims=True) + EPS))
        ys_ref[...] = (jnp.concatenate(outs, axis=1) * gn_ref[...]).astype(bf16)

    ch = lambda n: pl.BlockSpec((CHUNK, n), lambda c: (c, 0))
    return pl.pallas_call(
        body, name="ssd_fwd", grid=(nc,),
        in_specs=[ch(XBC_WIDTH), ch(LANES), ch(SSD_WIDTH), _full((1, LANES)), _full((1, LANES)), _full((1, SSD_WIDTH)),
                  _full((1, SSD_WIDTH))],
        out_specs=[ch(SSD_WIDTH), ch(SSD_WIDTH), pl.BlockSpec((1, SSD_STATE, SSD_WIDTH), lambda c: (c, 0, 0))],
        out_shape=[jax.ShapeDtypeStruct((T, SSD_WIDTH), f32), jax.ShapeDtypeStruct((T, SSD_WIDTH), bf16),
                   jax.ShapeDtypeStruct((nc, SSD_STATE, SSD_WIDTH), f32)],
        scratch_shapes=[pltpu.VMEM((SSD_STATE, SSD_WIDTH), f32)],
        compiler_params=_cparams(("arbitrary",), VMEM_MID),
    )(pre, dtr, z, dtb, alog, dskip_e, gn)


def _w_out_spec():
    n = 2 * SSD_WIDTH // N_CHIPS
    return pl.BlockSpec((N_CHIPS, n, D_MODEL), lambda *_: (0, OUT_OFF // n, 0))


def _out_proj_fwd(x, ys, yc, gath, g):
    T = x.shape[0]
    tm = min(512, T)
    n = 2 * SSD_WIDTH // N_CHIPS

    def body(x_ref, ys_ref, yc_ref, w_ref, g_ref, h_ref, u_ref):
        h = (x_ref[...] + _dot(ys_ref[:, 0:n], w_ref[0]) + _dot(ys_ref[:, n:], w_ref[1])
             + _dot(yc_ref[:, 0:n], w_ref[2]) + _dot(yc_ref[:, n:], w_ref[3]))
        h_ref[...] = h
        u_ref[...] = _rms(h, g_ref[...]).astype(bf16)

    row = pl.BlockSpec((tm, D_MODEL), lambda i: (i, 0))
    return pl.pallas_call(
        body, name="out_proj_fwd", grid=(T // tm,),
        in_specs=[row, row, row, _w_out_spec(), _full((1, D_MODEL))],
        out_specs=[row, row],
        out_shape=[jax.ShapeDtypeStruct((T, D_MODEL), f32), jax.ShapeDtypeStruct((T, D_MODEL), bf16)],
        compiler_params=_cparams(("parallel",), VMEM_MID),
    )(x, ys, yc, gath, g)


def _w_up_spec():
    return pl.BlockSpec((1, D_MODEL, D_MODEL), lambda i, b: (b, UP_OFF // D_MODEL, 0))


def _w_down_spec():
    return pl.BlockSpec((1, D_MODEL, D_MODEL), lambda i, b: (b, DOWN_OFF // D_MODEL, 0))


def _mlp_fwd(h1, u1, gath, g_next):
    T = h1.shape[0]
    tm = min(512, T)
    nb = D_FF // D_MODEL

    def body(h_ref, u_ref, wu_ref, wd_ref, g_ref, r_ref, h2_ref, u2_ref, acc_ref):
        b = pl.program_id(1)

        @pl.when(b == 0)
        def _():
            acc_ref[...] = jnp.zeros_like(acc_ref)

        r = jnp.maximum(_dot(u_ref[...], wu_ref[0]), 0.0)
        r_ref[...] = r.astype(bf16)
        acc_ref[...] += _dot((r * r).astype(bf16), wd_ref[0])

        @pl.when(b == nb - 1)
        def _():
            h2 = h_ref[...] + acc_ref[...]
            h2_ref[...] = h2
            u2_ref[...] = _rms(h2, g_ref[...]).astype(bf16)

    row = pl.BlockSpec((tm, D_MODEL), lambda i, b: (i, 0))
    return pl.pallas_call(
        body, name="mlp_fwd", grid=(T // tm, nb),
        in_specs=[row, row, _w_up_spec(), _w_down_spec(), _full((1, D_MODEL))],
        out_specs=[pl.BlockSpec((tm, D_MODEL), lambda i, b: (i, b)), row, row],
        out_shape=[jax.ShapeDtypeStruct((T, D_FF), bf16), jax.ShapeDtypeStruct((T, D_MODEL), f32),
                   jax.ShapeDtypeStruct((T, D_MODEL), bf16)],
        scratch_shapes=[pltpu.VMEM((tm, D_MODEL), f32)],
        compiler_params=_cparams(("parallel", "arbitrary"), VMEM_MID),
    )(h1, u1, gath, gath, g_next)


def _ple_loss(h2, u2, p, tgt, gath, b_pg, w_ple, g_ple, g_fin, g_pg):
    T = h2.shape[0]
    tm = min(256, T)
    npg = D_MODEL // N_CHIPS

    def body(h2_ref, u2_ref, p_ref, t_ref, wpg_ref, bpg_ref, wple_ref, gple_ref, gfin_ref, gpg_ref,
             loss_ref, dh2_ref, dh2b_ref, dgp_ref, dep_ref, dgfin_ref, dgple_ref, dbpg_ref, dgpg_ref):
        @pl.when(pl.program_id(0) == 0)
        def _():
            loss_ref[...] = jnp.zeros_like(loss_ref)
            dgfin_ref[...] = jnp.zeros_like(dgfin_ref)
            dgple_ref[...] = jnp.zeros_like(dgple_ref)
            dbpg_ref[...] = jnp.zeros_like(dbpg_ref)
            dgpg_ref[...] = jnp.zeros_like(dgpg_ref)

        h2 = h2_ref[...]
        gate_pre = bpg_ref[...]
        for b in range(N_CHIPS):
            gate_pre = gate_pre + _dot(u2_ref[:, b * npg:(b + 1) * npg], wpg_ref[b])
        gate = _sigmoid(gate_pre)
        e_pre = _dot(p_ref[...].astype(bf16), wple_ref[...])
        emb = _rms(e_pre, gple_ref[...])
        h3 = h2 + gate * emb
        diff = _rms(h3, gfin_ref[...]) - t_ref[...]
        sq = jnp.sum(jnp.sum(diff * diff, axis=1, keepdims=True), axis=0, keepdims=True)
        loss_ref[...] += (0.5 / D_MODEL) * sq
        dh3, dgfin = _rms_bwd(diff * (1.0 / D_MODEL), h3, gfin_ref[...])
        dgfin_ref[...] += dgfin
        dgp = dh3 * emb * gate * (1.0 - gate)
        dbpg_ref[...] += jnp.sum(dgp, axis=0, keepdims=True)
        dep, dgple = _rms_bwd(dh3 * gate, e_pre, gple_ref[...])
        dgple_ref[...] += dgple
        dgpb = dgp.astype(bf16)
        dgp_ref[...] = dgpb
        dep_ref[...] = dep.astype(bf16)
        du2 = jnp.concatenate([_dot_nt(dgpb, wpg_ref[b]) for b in range(N_CHIPS)], axis=1)
        dx, dgpg = _rms_bwd(du2, h2, gpg_ref[...])
        dgpg_ref[...] += dgpg
        dh2 = dh3 + dx
        dh2_ref[...] = dh2
        dh2b_ref[...] = dh2.astype(bf16)

    row = pl.BlockSpec((tm, D_MODEL), lambda i: (i, 0))
    vec = _full((1, D_MODEL))
    vshape = jax.ShapeDtypeStruct((1, D_MODEL), f32)
    return pl.pallas_call(
        body, name="ple_loss", grid=(T // tm,),
        in_specs=[row, row, pl.BlockSpec((tm, PLE_DIM), lambda i: (i, 0)), row,
                  pl.BlockSpec((N_CHIPS, npg, D_MODEL), lambda i: (0, PG_OFF // npg, 0)), vec, _full(w_ple.shape),
                  vec, vec, vec],
        out_specs=[_full((8, LANES)), row, row, row, row, vec, vec, vec, vec],
        out_shape=[jax.ShapeDtypeStruct((8, LANES), f32), jax.ShapeDtypeStruct((T, D_MODEL), f32),
                   jax.ShapeDtypeStruct((T, D_MODEL), bf16), jax.ShapeDtypeStruct((T, D_MODEL), bf16),
                   jax.ShapeDtypeStruct((T, D_MODEL), bf16), vshape, vshape, vshape, vshape],
        compiler_params=_cparams(("arbitrary",), VMEM_MID),
    )(h2, u2, p, tgt, gath, b_pg, w_ple, g_ple, g_fin, g_pg)


def _mlp_bwd(dh2, r, gath, h1, g):
    T = dh2.shape[0]
    tm = min(512, T)
    nb = D_FF // D_MODEL

    def body(dh2_ref, r_ref, wd_ref, wu_ref, h1_ref, g_ref, dhp_ref, dh1_ref, dh1b_ref, dg_ref, acc_ref):
        i, b = pl.program_id(0), pl.program_id(1)

        @pl.when(b == 0)
        def _():
            acc_ref[...] = jnp.zeros_like(acc_ref)

        @pl.when((b == 0) & (i == 0))
        def _():
            dg_ref[...] = jnp.zeros_like(dg_ref)

        dact = _dot_nt(dh2_ref[...].astype(bf16), wd_ref[0])
        dhp = (dact * 2.0 * r_ref[...].astype(f32)).astype(bf16)
        dhp_ref[...] = dhp
        acc_ref[...] += _dot_nt(dhp, wu_ref[0])

        @pl.when(b == nb - 1)
        def _():
            dx, dg = _rms_bwd(acc_ref[...], h1_ref[...], g_ref[...])
            dg_ref[...] += dg
            dh1 = dh2_ref[...] + dx
            dh1_ref[...] = dh1
            dh1b_ref[...] = dh1.astype(bf16)

    row = pl.BlockSpec((tm, D_MODEL), lambda i, b: (i, 0))
    return pl.pallas_call(
        body, name="mlp_bwd", grid=(T // tm, nb),
        in_specs=[row, pl.BlockSpec((tm, D_MODEL), lambda i, b: (i, b)), _w_down_spec(), _w_up_spec(), row,
                  _full((1, D_MODEL))],
        out_specs=[pl.BlockSpec((tm, D_MODEL), lambda i, b: (i, b)), row, row, _full((1, D_MODEL))],
        out_shape=[jax.ShapeDtypeStruct((T, D_FF), bf16), jax.ShapeDtypeStruct((T, D_MODEL), f32),
                   jax.ShapeDtypeStruct((T, D_MODEL), bf16), jax.ShapeDtypeStruct((1, D_MODEL), f32)],
        scratch_shapes=[pltpu.VMEM((tm, D_MODEL), f32)],
        compiler_params=_cparams(("arbitrary", "arbitrary"), VMEM_MID),
    )(dh2, r, gath, gath, h1, g)


def _out_proj_bwd(dh1, gath, co, ln_g, ln_b, rider=None):
    T = dh1.shape[0]
    tm = min(512, T)

    def body(dh_ref, w_ref, co_ref, g_ref, b_ref, dys_ref, dco_ref, dg_ref, db_ref):
        @pl.when(pl.program_id(0) == 0)
        def _():
            dg_ref[...] = jnp.zeros_like(dg_ref)
            db_ref[...] = jnp.zeros_like(db_ref)

        dhb = dh_ref[...].astype(bf16)
        dys_ref[...] = jnp.concatenate([_dot_nt(dhb, w_ref[0]), _dot_nt(dhb, w_ref[1])], axis=1)
        dyc = jnp.concatenate([_dot_nt(dhb, w_ref[2]), _dot_nt(dhb, w_ref[3])], axis=1)
        co = co_ref[...]
        mu = jnp.mean(co, axis=-1, keepdims=True)
        xc = co - mu
        rstd = lax.rsqrt(jnp.mean(xc * xc, axis=-1, keepdims=True) + EPS)
        xh = xc * rstd
        yn = xh * g_ref[...] + b_ref[...]
        dyn = dyc * _dsilu(yn)
        dg_ref[...] += jnp.sum(dyn * xh, axis=0, keepdims=True)
        db_ref[...] += jnp.sum(dyn, axis=0, keepdims=True)
        dxh = dyn * g_ref[...]
        dco_ref[...] = rstd * (dxh - jnp.mean(dxh, axis=-1, keepdims=True)
                               - xh * jnp.mean(dxh * xh, axis=-1, keepdims=True))

    row = pl.BlockSpec((tm, D_MODEL), lambda i: (i, 0))
    vec = _full((1, CONF_WIDTH))
    vshape = jax.ShapeDtypeStruct((1, CONF_WIDTH), f32)
    return _call(
        body, (dh1, gath, co, ln_g, ln_b), name="out_proj_bwd", grid=(T // tm,),
        in_specs=[row, _w_out_spec(), row, vec, vec],
        out_specs=[row, row, vec, vec],
        out_shape=[jax.ShapeDtypeStruct((T, SSD_WIDTH), f32), jax.ShapeDtypeStruct((T, CONF_WIDTH), f32), vshape, vshape],
        params=_cparams(("arbitrary",), VMEM_MID), rider=rider)


def _bwd_offsets(K):
    return [K - 1 - k for k in range(K)]


def _next_halo_spec(hb, tm, C, T):
    return pl.BlockSpec((hb, C), lambda i: (jnp.minimum((i + 1) * (tm // hb), T // hb - 1), 0))


DW_RB = 8
DW_UNROLL = 4
DW_ACC_VREGS = 32


def _conv_dw(dw_ref, bufd_ref, bufx_ref, phx_ref, offs_x, tm, C):
    K = len(offs_x)
    group = max(1, DW_ACC_VREGS // (C // LANES))
    for k0 in range(0, K, group):
        ks = list(range(k0, min(k0 + group, K)))

        def step(i, accs, ks=ks):
            for u in range(DW_UNROLL):
                r0 = pl.multiple_of((i * DW_UNROLL + u) * DW_RB, DW_RB)
                d = bufd_ref[pl.ds(r0, DW_RB), :]
                accs = tuple(acc + _window(bufx_ref, phx_ref, offs_x, offs_x[k], r0, DW_RB) * d
                             for k, acc in zip(ks, accs))
            return accs

        accs = lax.fori_loop(0, tm // (DW_RB * DW_UNROLL), step, tuple(jnp.zeros((DW_RB, C), f32) for _ in ks))
        for k, acc in zip(ks, accs):
            dw_ref[k:k + 1, :] += jnp.sum(acc, axis=0, keepdims=True)


def _fill_bwd_buffers(dcur_ref, dnext_ref, xcur_ref, xprev_ref, bufd_ref, bufx_ref, phd_ref, phx_ref, offs_d, offs_x,
                      hb, tm, first, last):
    bufd_ref[0:tm, :] = dcur_ref[...]
    bufd_ref[tm:tm + hb, :] = dnext_ref[...] * jnp.where(last, 0.0, 1.0)
    bufx_ref[0:hb, :] = xprev_ref[...] * jnp.where(first, 0.0, 1.0)
    bufx_ref[hb:hb + tm, :] = xcur_ref[...]
    _make_phases(bufd_ref, phd_ref, offs_d, tm)
    _make_phases(bufx_ref, phx_ref, offs_x, tm)


def _ssd_conv_bwd(dpre, xbc, w):
    T, C = xbc.shape
    K, hb = SSD_CONV, 8
    tm = min(256, T)
    nt = T // tm
    offs_d, offs_x = _bwd_offsets(K), _fwd_offsets(K, hb)

    def body(dcur_ref, dnext_ref, xcur_ref, xprev_ref, w_ref, dx_ref, dw_ref, db_ref, bufd_ref, bufx_ref, phd_ref, phx_ref):
        i = pl.program_id(0)

        @pl.when(i == 0)
        def _():
            dw_ref[...] = jnp.zeros_like(dw_ref)
            db_ref[...] = jnp.zeros_like(db_ref)

        _fill_bwd_buffers(dcur_ref, dnext_ref, xcur_ref, xprev_ref, bufd_ref, bufx_ref, phd_ref, phx_ref, offs_d, offs_x,
                          hb, tm, i == 0, i == nt - 1)

        def chunk(j, carry):
            r0 = pl.multiple_of(j * CONV_RB, CONV_RB)
            dx_ref[pl.ds(r0, CONV_RB), :] = _conv_rows(w_ref, bufd_ref, phd_ref, offs_d, r0, CONV_RB).astype(bf16)
            return carry

        lax.fori_loop(0, tm // CONV_RB, chunk, 0)
        _conv_dw(dw_ref, bufd_ref, bufx_ref, phx_ref, offs_x, tm, C)
        db_ref[...] += jnp.sum(dcur_ref[...], axis=0, keepdims=True)

    row = pl.BlockSpec((tm, C), lambda i: (i, 0))
    return pl.pallas_call(
        body, name="ssd_conv_bwd", grid=(nt,),
        in_specs=[row, _next_halo_spec(hb, tm, C, T), row, _prev_halo_spec(hb, tm, C), _full((SUBLANES * K, C))],
        out_specs=[row, _full((8, C)), _full((1, C))],
        out_shape=[jax.ShapeDtypeStruct((T, C), bf16), jax.ShapeDtypeStruct((8, C), f32), jax.ShapeDtypeStruct((1, C), f32)],
        scratch_shapes=[pltpu.VMEM((tm + hb, C), f32), pltpu.VMEM((hb + tm, C), f32),
                        pltpu.VMEM(_phase_shape(offs_d, tm, C), f32),
                        pltpu.VMEM(_phase_shape(offs_x, tm, C), f32)],
        compiler_params=_cparams(("arbitrary",), VMEM_BIG),
    )(dpre, dpre, xbc, xbc, _sublane_rows(w))


def _conf_conv_bwd(dco, v, w, cv, cg, rider=None):
    T, C = v.shape
    K, hb = CONF_KERNEL, 32
    tm = min(256, T)
    nt = T // tm
    offs_d, offs_x = _bwd_offsets(K), _fwd_offsets(K, hb)

    def body(dcur_ref, dnext_ref, vcur_ref, vprev_ref, w_ref, cv_ref, cg_ref, dcv_ref, dcg_ref, dw_ref, db_ref,
             bufd_ref, bufx_ref, phd_ref, phx_ref):
        i = pl.program_id(0)

        @pl.when(i == 0)
        def _():
            dw_ref[...] = jnp.zeros_like(dw_ref)
            db_ref[...] = jnp.zeros_like(db_ref)

        _fill_bwd_buffers(dcur_ref, dnext_ref, vcur_ref, vprev_ref, bufd_ref, bufx_ref, phd_ref, phx_ref, offs_d, offs_x,
                          hb, tm, i == 0, i == nt - 1)

        def chunk(j, carry):
            r0 = pl.multiple_of(j * CONV_RB, CONV_RB)
            rows = pl.ds(r0, CONV_RB)
            dv = _conv_rows(w_ref, bufd_ref, phd_ref, offs_d, r0, CONV_RB)
            s = _sigmoid(cg_ref[rows, :])
            dcv_ref[rows, :] = (dv * s).astype(bf16)
            dcg_ref[rows, :] = (dv * cv_ref[rows, :] * s * (1.0 - s)).astype(bf16)
            return carry

        lax.fori_loop(0, tm // CONV_RB, chunk, 0)
        _conv_dw(dw_ref, bufd_ref, bufx_ref, phx_ref, offs_x, tm, C)
        db_ref[...] += jnp.sum(dcur_ref[...], axis=0, keepdims=True)

    row = pl.BlockSpec((tm, C), lambda i: (i, 0))
    return _call(
        body, (dco, dco, v, v, _sublane_rows(w), cv, cg), name="conf_conv_bwd", grid=(nt,),
        in_specs=[row, _next_halo_spec(hb, tm, C, T), row, _prev_halo_spec(hb, tm, C), _full((SUBLANES * K, C)), row, row],
        out_specs=[row, row, _full((32, C)), _full((1, C))],
        out_shape=[jax.ShapeDtypeStruct((T, C), bf16), jax.ShapeDtypeStruct((T, C), bf16),
                   jax.ShapeDtypeStruct((32, C), f32), jax.ShapeDtypeStruct((1, C), f32)],
        scratch_shapes=[pltpu.VMEM((tm + hb, C), f32), pltpu.VMEM((hb + tm, C), f32),
                        pltpu.VMEM(_phase_shape(offs_d, tm, C), f32),
                        pltpu.VMEM(_phase_shape(offs_x, tm, C), f32)],
        params=_cparams(("arbitrary",), VMEM_BIG), rider=rider)


def _ssd_bwd(dys, y, z, pre, dtr, sprev, dtb, alog, dskip_e, gn):
    T = pre.shape[0]
    nc = T // CHUNK
    GW = SSD_WIDTH // 2

    def body(dys_ref, y_ref, z_ref, pre_ref, dtr_ref, sp_ref, dtb_ref, alog_ref, de_ref, gn_ref,
             dz_ref, dpre_ref, ddtr_ref, dgn_ref, dd_ref, dal_ref, ddtb_ref, ds_ref):
        @pl.when(pl.program_id(0) == 0)
        def _():
            ds_ref[...] = jnp.zeros_like(ds_ref)
            dgn_ref[...] = jnp.zeros_like(dgn_ref)
            dd_ref[...] = jnp.zeros_like(dd_ref)
            dal_ref[...] = jnp.zeros_like(dal_ref)
            ddtb_ref[...] = jnp.zeros_like(ddtb_ref)

        e = _head_matrix()
        pre = pre_ref[...]
        dtr_b = dtr_ref[...] + dtb_ref[...]
        q = _ssd_chunk_common(pre, dtr_ref[...], dtb_ref[...], alog_ref[...], e)
        cs, tri, xc, xd, xs, dt = q["cs"], q["tri"], q["xc"], q["xd"], q["xs"], q["dt"]
        cs_t = cs.T
        st = sp_ref[0]
        dsn = ds_ref[...]
        lane = lax.broadcasted_iota(jnp.int32, (1, LANES), 1)
        halves = (lane < HEAD_DIM, lane >= HEAD_DIM)
        row_i = lax.broadcasted_iota(jnp.int32, (CHUNK, CHUNK), 0)
        col_i = lax.broadcasted_iota(jnp.int32, (CHUNK, CHUNK), 1)
        tri_t = col_i >= row_i

        y = y_ref[...]
        zz = z_ref[...]
        sz = _sigmoid(zz)
        silu_z = zz * sz
        v = y * silu_z
        dout = dys_ref[...]
        gn_v = gn_ref[...]
        dv, vh = [], []
        for g in range(2):
            vg = _group(v, g, GW)
            rstd = lax.rsqrt(jnp.mean(vg * vg, axis=-1, keepdims=True) + EPS)
            vhg = vg * rstd
            dvh = _group(dout, g, GW) * _group(gn_v, g, GW)
            dv.append(rstd * (dvh - vhg * jnp.mean(dvh * vhg, axis=-1, keepdims=True)))
            vh.append(vhg)
        dv = jnp.concatenate(dv, axis=1)
        dgn_ref[...] += jnp.sum(dout * jnp.concatenate(vh, axis=1), axis=0, keepdims=True)
        dy = dv * silu_z
        dz_ref[...] = (dv * y * (sz * (1.0 + zz * (1.0 - sz)))).astype(bf16)

        dd_row = jnp.sum(dy * xs, axis=0, keepdims=True)
        dd_ref[...] += _contract(jnp.broadcast_to(dd_row, (8, SSD_WIDTH)), e)[0:1, :]
        dxs = dy * de_ref[...]

        dz_in = dy * q["ecs_e"]
        g_mat, gt_mat, dcm, dbm, dsp, dxd, y_off = [], [], [], [], [], [], []
        bgs, cgs = [], []
        for g in range(2):
            bg = _group(q["bm"], g, SSD_STATE)
            cg = _group(q["cm"], g, SSD_STATE)
            bgb, cgb = bg.astype(bf16), cg.astype(bf16)
            bgs.append(bgb)
            cgs.append(cgb)
            stg = _group(st, g, GW).astype(bf16)
            dsng = _group(dsn, g, GW).astype(bf16)
            dzg = _group(dz_in, g, GW).astype(bf16)
            g_mat.append(_dot_nt(cgb, bgb))
            gt_mat.append(_dot_nt(bgb, cgb))
            y_off.append(_dot(cgb, stg))
            dcm.append(_dot_nt(dzg, stg))
            dsp.append(_dot(cg.T.astype(bf16), dzg))
            dbm.append(_dot_nt(_group(xd, g, GW).astype(bf16), dsng))
            dxd.append(_dot(bgb, dsng))
        y_off = jnp.concatenate(y_off, axis=1) * q["ecs_e"]
        dxd = jnp.concatenate(dxd, axis=1)
        ds_ref[...] = dsn * q["cd_e"] + jnp.concatenate(dsp, axis=1)
        dcd_row = jnp.sum(dsn * st, axis=0, keepdims=True) * q["cd_e"]
        t_e = dxd * xd
        dcs = _contract(dy * y_off - t_e, e)
        last_row = _contract(jnp.broadcast_to(dcd_row + jnp.sum(t_e, axis=0, keepdims=True), (8, SSD_WIDTH)), e)[0:1, :]
        dxc_state = dxd * q["dte_e"]

        dg_acc = [jnp.zeros((CHUNK, CHUNK), f32), jnp.zeros((CHUNK, CHUNK), f32)]
        dgt_acc = [jnp.zeros((CHUNK, CHUNK), f32), jnp.zeros((CHUNK, CHUNK), f32)]
        dxc_pairs = []
        for j in range(SSD_HEADS // 2):
            dyp_f = dy[:, j * LANES:(j + 1) * LANES]
            xcp_f = xc[:, j * LANES:(j + 1) * LANES]
            acc = jnp.zeros((CHUNK, LANES), f32)
            for hh in range(2):
                h = 2 * j + hh
                g = h // 8
                dyp = jnp.where(halves[hh], dyp_f, 0.0).astype(bf16)
                xcp = jnp.where(halves[hh], xcp_f, 0.0).astype(bf16)
                lm = jnp.exp(jnp.where(tri, cs[:, h:h + 1] - cs_t[h:h + 1, :], -1e30))
                lm_t = jnp.exp(jnp.where(tri_t, cs_t[h:h + 1, :] - cs[:, h:h + 1], -1e30))
                dm = _dot_nt(dyp, xcp) * lm
                dm_t = _dot_nt(xcp, dyp) * lm_t
                acc = acc + _dot((gt_mat[g] * lm_t).astype(bf16), dyp)
                dg_acc[g] = dg_acc[g] + dm
                dgt_acc[g] = dgt_acc[g] + dm_t
                qd = jnp.sum(dm * g_mat[g] - dm_t * gt_mat[g], axis=1, keepdims=True)
                dcs = dcs + qd * (lane == h).astype(f32)
            dxc_pairs.append(acc)
        dxc = jnp.concatenate(dxc_pairs, axis=1) + dxc_state
        for g in range(2):
            dcm[g] = dcm[g] + _dot(dg_acc[g].astype(bf16), bgs[g])
            dbm[g] = dbm[g] + _dot(dgt_acc[g].astype(bf16), cgs[g])

        dxs = dxs + dxc * q["dt_e"]
        ddt = _contract(dxc * xs, e)
        dcs = dcs + jnp.where(row_i == CHUNK - 1, jnp.broadcast_to(last_row, (CHUNK, LANES)), 0.0)
        da = jnp.dot(tri_t.astype(f32), dcs, precision=lax.Precision.HIGHEST, preferred_element_type=f32)
        ddt = ddt + da * q["a_neg"]
        dal_ref[...] += jnp.sum(da * dt, axis=0, keepdims=True) * q["a_neg"]
        ddtr = ddt * _sigmoid(dtr_b) * (lane < SSD_HEADS).astype(f32)
        ddtb_ref[...] += jnp.sum(ddtr, axis=0, keepdims=True)
        ddtr_ref[...] = ddtr.astype(bf16)

        dact = jnp.concatenate([dxs, dbm[0], dbm[1], dcm[0], dcm[1]], axis=1)
        dpre_ref[...] = dact * _dsilu(pre)

    rev = lambda n: pl.BlockSpec((CHUNK, n), lambda c: (nc - 1 - c, 0))
    vec = _full((1, LANES))
    vshape = jax.ShapeDtypeStruct((1, LANES), f32)
    return pl.pallas_call(
        body, name="ssd_bwd", grid=(nc,),
        in_specs=[rev(SSD_WIDTH), rev(SSD_WIDTH), rev(SSD_WIDTH), rev(XBC_WIDTH), rev(LANES),
                  pl.BlockSpec((1, SSD_STATE, SSD_WIDTH), lambda c: (nc - 1 - c, 0, 0)),
                  vec, vec, _full((1, SSD_WIDTH)), _full((1, SSD_WIDTH))],
        out_specs=[rev(SSD_WIDTH), rev(XBC_WIDTH), rev(LANES), _full((1, SSD_WIDTH)), vec, vec, vec],
        out_shape=[jax.ShapeDtypeStruct((T, SSD_WIDTH), bf16), jax.ShapeDtypeStruct((T, XBC_WIDTH), f32),
                   jax.ShapeDtypeStruct((T, LANES), bf16), jax.ShapeDtypeStruct((1, SSD_WIDTH), f32),
                   vshape, vshape, vshape],
        scratch_shapes=[pltpu.VMEM((SSD_STATE, SSD_WIDTH), f32)],
        compiler_params=_cparams(("arbitrary",), VMEM_MID),
    )(dys, y, z, pre, dtr, sprev, dtb, alog, dskip_e, gn)


def _in_proj_bwd(dz, dxbc, dcv, dcg, ddt, gin, x, dh1, g):
    T = x.shape[0]
    tm = min(256, T)

    def body(dz_ref, dx_ref, dcv_ref, dcg_ref, ddt_ref, gin_ref, x_ref, dh_ref, g_ref, gx_ref, dg_ref, wt_ref):
        @pl.when(pl.program_id(0) == 0)
        def _():
            dg_ref[...] = jnp.zeros_like(dg_ref)
            _assemble_w_in_t(gin_ref, wt_ref)

        du = (_dot(dz_ref[...], wt_ref[0:O_XBC, :]) + _dot(dx_ref[...], wt_ref[O_XBC:O_DT, :])
              + _dot(dcv_ref[...], wt_ref[O_CV:O_CG, :]) + _dot(dcg_ref[...], wt_ref[O_CG:IN_WIDTH, :])
              + _dot(ddt_ref[...], wt_ref[O_DT:O_DT + LANES, :]))
        dx, dg = _rms_bwd(du, x_ref[...], g_ref[...])
        dg_ref[...] += dg
        gx_ref[...] = dh_ref[...] + dx

    row = lambda n: pl.BlockSpec((tm, n), lambda i: (i, 0))
    return pl.pallas_call(
        body, name="in_proj_bwd", grid=(T // tm,),
        in_specs=[row(SSD_WIDTH), row(XBC_WIDTH), row(CONF_WIDTH), row(CONF_WIDTH), row(LANES), _full(gin.shape),
                  row(D_MODEL), row(D_MODEL), _full((1, D_MODEL))],
        out_specs=[row(D_MODEL), _full((1, D_MODEL))],
        out_shape=[jax.ShapeDtypeStruct((T, D_MODEL), f32), jax.ShapeDtypeStruct((1, D_MODEL), f32)],
        scratch_shapes=[pltpu.VMEM((IN_WIDTH, D_MODEL), bf16)],
        compiler_params=_cparams(("arbitrary",), VMEM_BIG),
    )(dz, dxbc, dcv, dcg, ddt, gin, x, dh1, g)


def _weight_grad(a, g, name, square=False, slab=None, place=None, tk=512):
    T, K = a.shape
    N = g.shape[1]
    tk = min(tk, K)
    tn = 1024 if N % 1024 == 0 else min(512, N)
    tt = min(2048, T)

    def body(a_ref, g_ref, *rest):
        o_ref = rest[-1]
        acc = _dot_tn(_operand(a_ref[...]), g_ref[...].astype(bf16))
        t = pl.program_id(2)
        shaped = acc if slab is None else acc[None]

        @pl.when(t == 0)
        def _():
            o_ref[...] = shaped

        @pl.when(t > 0)
        def _():
            o_ref[...] += shaped

    def _operand(av):
        if square:
            av = av.astype(f32)
            av = av * av
        return av.astype(bf16)

    in_specs = [pl.BlockSpec((tt, tk), lambda i, j, t: (t, i)), pl.BlockSpec((tt, tn), lambda i, j, t: (t, j))]
    grid = (K // tk, N // tn, T // tt)
    params = _cparams(("parallel", "parallel", "arbitrary"), VMEM_MID)
    if slab is None:
        return pl.pallas_call(
            body, name=name, grid=grid, in_specs=in_specs,
            out_specs=pl.BlockSpec((tk, tn), lambda i, j, t: (i, j)),
            out_shape=jax.ShapeDtypeStruct((K, N), f32), compiler_params=params,
        )(a, g)
    return pl.pallas_call(
        body, name=name, grid=grid, in_specs=in_specs + [ANY],
        out_specs=pl.BlockSpec((1, tk, tn), lambda i, j, t: place(i, j)),
        out_shape=jax.ShapeDtypeStruct(slab.shape, f32), input_output_aliases={2: 0}, compiler_params=params,
    )(a, g, slab)


def _place():
    return lax.axis_index("x"), lax.axis_index("y"), lax.axis_index("c")


def _other_chips(x, y):
    return [(1 - x, y), (x, 1 - y), (1 - x, 1 - y)]


def _remote(src, dst, ssem, rsem, dev):
    return pltpu.make_async_remote_copy(src_ref=src, dst_ref=dst, send_sem=ssem, recv_sem=rsem, device_id=dev,
                                        device_id_type=MESH)


def _gather_weights(arrays, convw):
    n = len(arrays)
    halves = tuple(a.shape[1] // 2 for a in arrays)

    def body(*refs):
        cw_ref, cwo_ref = refs[n], refs[2 * n + 1]
        ssem, rsem, lsem = refs[2 * n + 2:]
        triples = tuple(zip(refs[:n], refs[n + 1:2 * n + 1], halves))
        x, y, c = _place()
        me_b = 2 * x + y
        sib = (x, y, 1 - c)
        chips = _other_chips(x, y)
        loc = pltpu.make_async_copy(cw_ref, cwo_ref.at[me_b], lsem)
        loc.start()
        sends = []
        for j, (src, dst, h) in enumerate(triples):
            mine = pl.ds(c * h, h)
            for k, (px, py) in enumerate(chips):
                s = 6 * j + k
                sends.append(_remote(src.at[me_b, mine], dst.at[me_b, mine], ssem.at[s], rsem.at[s], (px, py, c)))
        for k, (px, py) in enumerate(chips):
            sends.append(_remote(cw_ref, cwo_ref.at[me_b], ssem.at[6 * n + k], rsem.at[6 * n + k], (px, py, c)))
        for cp in sends:
            cp.start()
        for j, (src, dst, h) in enumerate(triples):
            mine = pl.ds(c * h, h)
            for k, (px, py) in enumerate(chips):
                b = 2 * px + py
                s = 6 * j + k
                _remote(src.at[b, mine], dst.at[b, mine], ssem.at[s], rsem.at[s], (px, py, c)).wait_recv()
                fw = _remote(dst.at[b, mine], dst.at[b, mine], ssem.at[s + 3], rsem.at[s + 3], sib)
                fw.start()
                sends.append(fw)
        for k, (px, py) in enumerate(chips):
            b = 2 * px + py
            _remote(cw_ref, cwo_ref.at[b], ssem.at[6 * n + k], rsem.at[6 * n + k], (px, py, c)).wait_recv()
        for j, (src, dst, h) in enumerate(triples):
            theirs = pl.ds((1 - c) * h, h)
            for k, (px, py) in enumerate(chips):
                b = 2 * px + py
                s = 6 * j + k + 3
                _remote(src.at[b, theirs], dst.at[b, theirs], ssem.at[s], rsem.at[s], sib).wait_recv()
        for cp in sends:
            cp.wait_send()
        loc.wait()

    return pl.pallas_call(
        body, name="gather_weights", in_specs=[ANY] * (n + 1), out_specs=[ANY] * (n + 1),
        out_shape=[jax.ShapeDtypeStruct(a.shape, bf16) for a in arrays]
        + [jax.ShapeDtypeStruct((N_CHIPS, CONVW_ROWS, D_MODEL), f32)],
        input_output_aliases={j: j for j in range(n)},
        scratch_shapes=[pltpu.SemaphoreType.DMA((6 * n + 3,)), pltpu.SemaphoreType.DMA((6 * n + 3,)),
                        pltpu.SemaphoreType.DMA(())],
    )(*arrays, convw)


def _gather_rider(gath0, lo, n):
    h = gath0.shape[1] // 2

    def copies(rins, routs, ssem, rsem, sending):
        (g_ref,), (o_ref,) = rins, routs
        x, y, c = _place()
        mine = pl.ds(c * h + lo, n)
        for k, (px, py) in enumerate(_other_chips(x, y)):
            b = 2 * x + y if sending else 2 * px + py
            yield _remote(g_ref.at[b, mine], o_ref.at[b, mine], ssem.at[k], rsem.at[k], (px, py, c))

    def start(*refs):
        for cp in copies(*refs, sending=True):
            cp.start()

    def finish(*refs):
        for cp in copies(*refs, sending=False):
            cp.wait()

    return _Rider([gath0], [jax.ShapeDtypeStruct(gath0.shape, gath0.dtype)], {0: 0}, 3, start, finish)


def _forward_to_sibling(gath):
    h = gath.shape[1] // 2

    def body(g_ref, o_ref, ssem, rsem):
        x, y, c = _place()
        sib = (x, y, 1 - c)
        mine, theirs = pl.ds(c * h, h), pl.ds((1 - c) * h, h)
        blocks = [2 * px + py for px, py in _other_chips(x, y)]
        sends = [_remote(g_ref.at[b, mine], o_ref.at[b, mine], ssem.at[k], rsem.at[k], sib) for k, b in enumerate(blocks)]
        for cp in sends:
            cp.start()
        for k, b in enumerate(blocks):
            _remote(g_ref.at[b, theirs], o_ref.at[b, theirs], ssem.at[k], rsem.at[k], sib).wait_recv()
        for cp in sends:
            cp.wait_send()

    return pl.pallas_call(
        body, name="forward_to_sibling", in_specs=[ANY], out_specs=ANY,
        out_shape=jax.ShapeDtypeStruct(gath.shape, gath.dtype), input_output_aliases={0: 0},
        scratch_shapes=[pltpu.SemaphoreType.DMA((3,)), pltpu.SemaphoreType.DMA((3,))],
    )(gath)


def _swap_copy(g_ref, r_ref, ssem, rsem):
    x, y, c = _place()
    h = r_ref.shape[1]
    return _remote(g_ref.at[:, pl.ds((1 - c) * h, h), :], r_ref, ssem.at[0], rsem.at[0], (x, y, 1 - c))


def _swap_rider(g):
    def start(rins, routs, ssem, rsem):
        _swap_copy(rins[0], routs[0], ssem, rsem).start()

    def finish(rins, routs, ssem, rsem):
        _swap_copy(rins[0], routs[0], ssem, rsem).wait()

    return _Rider([g], [jax.ShapeDtypeStruct((N_CHIPS, g.shape[1] // 2, g.shape[2]), g.dtype)], {}, 1, start, finish)


def _swap_halves(g):
    def body(g_ref, r_ref, ssem, rsem):
        cp = _swap_copy(g_ref, r_ref, ssem, rsem)
        cp.start()
        cp.wait()

    return pl.pallas_call(
        body, name="swap_halves", in_specs=[ANY], out_specs=ANY,
        out_shape=jax.ShapeDtypeStruct((N_CHIPS, g.shape[1] // 2, g.shape[2]), g.dtype),
        scratch_shapes=[pltpu.SemaphoreType.DMA((1,)), pltpu.SemaphoreType.DMA((1,))],
    )(g)


def _chip_sum(cidx, gslab, recv, name):
    half, C = recv.shape[1:]
    tr = half // 2 if (half // 2) % 16 == 0 else half

    def body(c_ref, g_ref, r_ref, o_ref):
        o_ref[...] = (g_ref[...] + r_ref[...]).astype(bf16)

    return pl.pallas_call(
        body, name=name,
        grid_spec=pltpu.PrefetchScalarGridSpec(
            num_scalar_prefetch=1, grid=(N_CHIPS, half // tr),
            in_specs=[pl.BlockSpec((1, tr, C), lambda b, i, c_ref: (b, c_ref[0] * (half // tr) + i, 0)),
                      pl.BlockSpec((1, tr, C), lambda b, i, c_ref: (b, i, 0))],
            out_specs=pl.BlockSpec((1, tr, C), lambda b, i, c_ref: (b, i, 0))),
        out_shape=jax.ShapeDtypeStruct((N_CHIPS, half, C), bf16),
        compiler_params=_cparams(("parallel", "parallel"), VMEM_MID),
    )(cidx, gslab, recv)


def _exchange_rider(h):
    def copies(rins, routs, ssem, rsem):
        x, y, c = _place()
        for k, (px, py) in enumerate(_other_chips(x, y)):
            yield _remote(rins[0].at[2 * px + py], routs[0].at[k], ssem.at[k], rsem.at[k], (px, py, c))

    def start(*refs):
        for cp in copies(*refs):
            cp.start()

    def finish(*refs):
        for cp in copies(*refs):
            cp.wait()

    return _Rider([h], [jax.ShapeDtypeStruct((3,) + h.shape[1:], h.dtype)], {}, 3, start, finish)


def _exchange(hb, small):
    def body(hb_ref, sm_ref, rb_ref, all_ref, ssem, rsem, lsem):
        x, y, c = _place()
        me = 4 * x + 2 * y + c
        chips = _other_chips(x, y)
        loc = pltpu.make_async_copy(sm_ref, all_ref.at[me], lsem)
        loc.start()
        sends = []
        for k, (px, py) in enumerate(chips):
            sends.append(_remote(hb_ref.at[2 * px + py], rb_ref.at[k], ssem.at[3 + k], rsem.at[3 + k], (px, py, c)))
        peers = []
        for r in range(1, N_DEV):
            peer = ((1 - x) if r & 4 else x, (1 - y) if r & 2 else y, (1 - c) if r & 1 else c)
            peers.append(peer)
            sends.append(_remote(sm_ref, all_ref.at[me], ssem.at[5 + r], rsem.at[5 + r], peer))
        for cp in sends:
            cp.start()
        for k, (px, py) in enumerate(chips):
            _remote(hb_ref.at[0], rb_ref.at[k], ssem.at[3 + k], rsem.at[3 + k], (px, py, c)).wait_recv()
        for r, peer in zip(range(1, N_DEV), peers):
            pid = 4 * peer[0] + 2 * peer[1] + peer[2]
            _remote(sm_ref, all_ref.at[pid], ssem.at[5 + r], rsem.at[5 + r], peer).wait_recv()
        for cp in sends:
            cp.wait_send()
        loc.wait()

    return pl.pallas_call(
        body, name="exchange", in_specs=[ANY, ANY], out_specs=[ANY, ANY],
        out_shape=[jax.ShapeDtypeStruct((3,) + hb.shape[1:], bf16),
                   jax.ShapeDtypeStruct((N_DEV, SMALL_ROWS, D_MODEL), f32)],
        scratch_shapes=[pltpu.SemaphoreType.DMA((13,)), pltpu.SemaphoreType.DMA((13,)), pltpu.SemaphoreType.DMA(())],
    )(hb, small)


def _final_sum(idx, gslab, recv_sib, recv_ici, name):
    half, C = recv_sib.shape[1:]
    tr = half // 2 if (half // 2) % 16 == 0 else half

    def body(i_ref, g_ref, r_ref, p_ref, o_ref):
        acc = g_ref[0] + r_ref[0]
        for k in range(3):
            acc = acc + p_ref[k].astype(f32)
        o_ref[...] = acc

    return pl.pallas_call(
        body, name=name,
        grid_spec=pltpu.PrefetchScalarGridSpec(
            num_scalar_prefetch=1, grid=(half // tr,),
            in_specs=[pl.BlockSpec((1, tr, C), lambda i, s: (s[1], s[0] * (half // tr) + i, 0)),
                      pl.BlockSpec((1, tr, C), lambda i, s: (s[1], i, 0)),
                      pl.BlockSpec((3, tr, C), lambda i, s: (0, i, 0))],
            out_specs=pl.BlockSpec((tr, C), lambda i, s: (s[0] * (half // tr) + i, 0))),
        out_shape=jax.ShapeDtypeStruct((2 * half, C), f32),
        compiler_params=_cparams(("parallel",), VMEM_MID),
    )(idx, gslab, recv_sib, recv_ici)


def _join_halves(ra, rb):
    ha, hb = ra.shape[0] // 2, rb.shape[0] // 2

    def body(a_ref, b_ref, ao_ref, bo_ref, ssem, rsem):
        x, y, c = _place()
        sib = (x, y, 1 - c)
        mine_a, theirs_a = pl.ds(c * ha, ha), pl.ds((1 - c) * ha, ha)
        mine_b, theirs_b = pl.ds(c * hb, hb), pl.ds((1 - c) * hb, hb)
        ca = _remote(a_ref.at[mine_a], ao_ref.at[mine_a], ssem.at[0], rsem.at[0], sib)
        cb = _remote(b_ref.at[mine_b], bo_ref.at[mine_b], ssem.at[1], rsem.at[1], sib)
        ca.start()
        cb.start()
        _remote(a_ref.at[theirs_a], ao_ref.at[theirs_a], ssem.at[0], rsem.at[0], sib).wait_recv()
        _remote(b_ref.at[theirs_b], bo_ref.at[theirs_b], ssem.at[1], rsem.at[1], sib).wait_recv()
        ca.wait_send()
        cb.wait_send()

    return pl.pallas_call(
        body, name="join_halves", in_specs=[ANY, ANY], out_specs=[ANY, ANY],
        out_shape=[jax.ShapeDtypeStruct(ra.shape, f32), jax.ShapeDtypeStruct(rb.shape, f32)],
        input_output_aliases={0: 0, 1: 1},
        scratch_shapes=[pltpu.SemaphoreType.DMA((2,)), pltpu.SemaphoreType.DMA((2,))],
    )(ra, rb)


def _shard_rows(gt):
    def body(g_ref, o_ref):
        for b in range(N_CHIPS):
            o_ref[b, 0:W_IN_ROWS, :] = g_ref[b * W_IN_ROWS:(b + 1) * W_IN_ROWS, :]
            o_ref[b, W_IN_ROWS:W_IN_ROWS_PAD, :] = jnp.zeros((W_IN_ROWS_PAD - W_IN_ROWS, LANES), f32)

    return pl.pallas_call(
        body, name="shard_rows", grid=(D_MODEL // LANES,),
        in_specs=[pl.BlockSpec((IN_WIDTH, LANES), lambda i: (0, i))],
        out_specs=pl.BlockSpec((N_CHIPS, W_IN_ROWS_PAD, LANES), lambda i: (0, 0, i)),
        out_shape=jax.ShapeDtypeStruct((N_CHIPS, W_IN_ROWS_PAD, D_MODEL), f32),
        compiler_params=_cparams(("parallel",), VMEM_MID),
    )(gt)


def _sum_small(all_small):
    def body(a_ref, o_ref):
        acc = a_ref[0]
        for d in range(1, N_DEV):
            acc = acc + a_ref[d]
        o_ref[...] = acc

    return pl.pallas_call(
        body, name="sum_small", out_shape=jax.ShapeDtypeStruct((SMALL_ROWS, D_MODEL), f32),
    )(all_small)


def _adamw(w, g, m, v, name, g_off=0, by_columns=False):
    R, C = w.shape
    tr = 256 if R % 256 == 0 else R
    assert g_off % tr == 0 and not (by_columns and g_off)
    c1 = 1.0 - ADAM_B1 ** ADAM_STEP
    c2 = 1.0 - ADAM_B2 ** ADAM_STEP

    def body(w_ref, g_ref, m_ref, v_ref, d_ref, mo_ref, vo_ref):
        gg = g_ref[...]
        m2 = ADAM_B1 * m_ref[...] + (1.0 - ADAM_B1) * gg
        v2 = ADAM_B2 * v_ref[...] + (1.0 - ADAM_B2) * (gg * gg)
        mo_ref[...] = m2
        vo_ref[...] = v2
        d_ref[...] = -ADAM_LR * ((m2 / c1) / (jnp.sqrt(v2 / c2) + ADAM_EPS) + ADAM_WD * w_ref[...])

    if by_columns:
        blk = gblk = pl.BlockSpec((R, LANES), lambda i: (0, i))
        grid = (C // LANES,)
    else:
        blk = pl.BlockSpec((tr, C), lambda i: (i, 0))
        gblk = pl.BlockSpec((tr, C), lambda i: (g_off // tr + i, 0))
        grid = (R // tr,)
    shp = jax.ShapeDtypeStruct((R, C), f32)
    return pl.pallas_call(
        body, name=name, grid=grid, in_specs=[blk, gblk, blk, blk], out_specs=[blk] * 3, out_shape=[shp] * 3,
        compiler_params=_cparams(("parallel",), VMEM_MID),
    )(w, g, m, v)


def _pad_lanes(v):
    return jnp.pad(v, ((0, 0), (0, LANES - v.shape[1])))


def _local_step(x, p, tgt, gath0, cidx, gin, S):
    dtb = _pad_lanes(S["dt_bias"])
    alog = _pad_lanes(S["A_log"])
    dskip_e = jnp.repeat(S["D_skip"], HEAD_DIM, axis=1)

    early = GATHER_EARLY_ROWS
    u0, z, xbc, cv, cg, dtr, v, gath1 = _in_proj_fwd(x, S["mix_norm_g"], gin, rider=_gather_rider(gath0, 0, early))
    co, yc, gath = _conf_fwd(v, S["conf_dw_w"], S["conf_dw_b"], S["conf_ln_g"], S["conf_ln_b"],
                             rider=_gather_rider(gath1, early, SLAB_A // 2 - early))
    gath = _forward_to_sibling(gath)
    w_ple = jnp.concatenate([_ple_of_slab(gath[b]) for b in range(N_CHIPS)], axis=1)
    pre = _ssd_conv_fwd(xbc, S["ssd_conv_w"], S["ssd_conv_b"])
    y, ys, sprev = _ssd_fwd(pre, dtr, z, dtb, alog, dskip_e, S["ssd_norm_g"])
    h1, u1 = _out_proj_fwd(x, ys, yc, gath, S["mlp_norm_g"])
    r, h2, u2 = _mlp_fwd(h1, u1, gath, S["ple_gate_norm_g"])
    loss, dh2, dh2b, dgp, dep, dg_fin, dg_ple, db_pg, dg_pg = _ple_loss(
        h2, u2, p, tgt, gath, S["b_ple_gate"], w_ple, S["ple_norm_g"], S["final_norm_g"], S["ple_gate_norm_g"])

    npg = D_MODEL // N_CHIPS
    ga = lax.empty((N_CHIPS, SLAB_A, D_MODEL), f32)
    ga = _weight_grad(u2, dgp, "dw_ple_gate", slab=ga, tk=npg, place=lambda i, j: (i, PG_OFF // npg, j))
    ga = _weight_grad(r, dh2b, "dw_down", square=True, slab=ga, place=lambda i, j: (i // 2, DOWN_OFF // 512 + i % 2, j))
    gw_ple = _weight_grad(p, dep, "dw_ple")
    dhp, dh1, dh1b, dg_mlp = _mlp_bwd(dh2, r, gath, h1, S["mlp_norm_g"])
    ga = _weight_grad(u1, dhp, "dw_up", slab=ga, place=lambda i, j: (j, UP_OFF // 512 + i, 0))
    ga = _weight_grad(ys, dh1b, "dw_out_ssd", slab=ga, place=lambda i, j: (i, OUT_OFF // 512, j))
    ga = _weight_grad(yc, dh1b, "dw_out_conf", slab=ga, place=lambda i, j: (2 + i, OUT_OFF // 512, j))
    n_ple = D_MODEL // N_CHIPS
    ple_rows = jnp.stack([_rows(gw_ple[:, b * n_ple:(b + 1) * n_ple]) for b in range(N_CHIPS)], axis=0)
    ga = lax.dynamic_update_slice(ga, ple_rows, (0, PLE_OFF, 0))
    dys, dco, dg_ln, db_ln, recv_a = _out_proj_bwd(dh1, gath, co, S["conf_ln_g"], S["conf_ln_b"], rider=_swap_rider(ga))
    ha = _chip_sum(cidx, ga, recv_a, "chip_sum_a")
    dcv, dcg, dw_conf, db_conf, ici_a = _conf_conv_bwd(dco, v, S["conf_dw_w"], cv, cg, rider=_exchange_rider(ha))
    dz, dpre, ddtr, dg_ssdn, dd, dal, ddtb = _ssd_bwd(dys, y, z, pre, dtr, sprev, dtb, alog, dskip_e, S["ssd_norm_g"])
    dxbc, dw_sconv, db_sconv = _ssd_conv_bwd(dpre, xbc, S["ssd_conv_w"])
    gx, dg_mix = _in_proj_bwd(dz, dxbc, dcv, dcg, ddtr, gin, x, dh1, S["mix_norm_g"])

    gw_in = jnp.concatenate([
        _weight_grad(dz, u0, "dw_in_z"), _weight_grad(dxbc, u0, "dw_in_xbc"),
        _weight_grad(ddtr, u0, "dw_in_dt")[:SSD_HEADS],
        _weight_grad(dcv, u0, "dw_in_cv"), _weight_grad(dcg, u0, "dw_in_cg")], axis=0)
    small = {
        "mix_norm_g": dg_mix, "ssd_conv_w": dw_sconv[:SSD_CONV], "ssd_conv_b": db_sconv,
        "dt_bias": ddtb[:, :SSD_HEADS], "A_log": dal[:, :SSD_HEADS], "D_skip": dd[:, :SSD_HEADS],
        "ssd_norm_g": dg_ssdn, "conf_dw_w": dw_conf[:CONF_KERNEL], "conf_dw_b": db_conf,
        "conf_ln_g": dg_ln, "conf_ln_b": db_ln, "mlp_norm_g": dg_mlp, "ple_gate_norm_g": dg_pg,
        "b_ple_gate": db_pg, "ple_norm_g": dg_ple, "final_norm_g": dg_fin,
    }
    return loss, gx, ga, recv_a, ici_a, gw_in, small


def _rows(a):
    return a.reshape(-1, D_MODEL)


def _pad_rows(a, n):
    flat = a.reshape(-1)
    return jnp.pad(flat, (0, n * D_MODEL - flat.shape[0])).reshape(n, D_MODEL)


def _ple_of_slab(slab):
    return slab[PLE_OFF:PLE_OFF + PLE_ROWS].reshape(PLE_DIM, D_MODEL // N_CHIPS)


SMALL_LAYOUT = (("mix_norm_g", 1), ("ssd_norm_g", 1), ("conf_dw_b", 1), ("conf_ln_g", 1), ("conf_ln_b", 1),
                ("mlp_norm_g", 1), ("ple_gate_norm_g", 1), ("b_ple_gate", 1), ("ple_norm_g", 1), ("final_norm_g", 1),
                ("ssd_conv_b", 2), ("dt_bias", 1), ("A_log", 1), ("D_skip", 1), ("loss", 1),
                ("ssd_conv_w", 6), ("conf_dw_w", 31))


def _pack_small(d):
    parts = [_pad_rows(d[n], r) for n, r in SMALL_LAYOUT]
    used = sum(r for _, r in SMALL_LAYOUT)
    parts.append(jnp.zeros((SMALL_ROWS - used, D_MODEL), f32))
    return jnp.concatenate(parts, axis=0)


def _unpack_small(a, shapes):
    out, o = {}, 0
    for n, r in SMALL_LAYOUT:
        shp = shapes[n]
        size = 1
        for s in shp:
            size *= s
        out[n] = a[o:o + r].reshape(-1)[:size].reshape(shp)
        o += r
    return out


BIG = ("w_in", "w_out", "w_up", "w_down", "w_ple_gate", "w_ple")
BIG_A = (("w_up", UP_OFF), ("w_down", DOWN_OFF), ("w_out", OUT_OFF), ("w_ple_gate", PG_OFF))
WEIGHTS = ("mix_norm_g", "w_in", "ssd_conv_w", "ssd_conv_b", "dt_bias", "A_log", "D_skip", "ssd_norm_g", "conf_dw_w",
           "conf_dw_b", "conf_ln_g", "conf_ln_b", "w_out", "mlp_norm_g", "w_up", "w_down", "ple_gate_norm_g",
           "w_ple_gate", "b_ple_gate", "w_ple", "ple_norm_g", "final_norm_g")


def kernel(x, p, mix_norm_g, w_in, ssd_conv_w, ssd_conv_b, dt_bias, A_log, D_skip, ssd_norm_g, conf_dw_w, conf_dw_b, conf_ln_g, conf_ln_b, w_out, mlp_norm_g, w_up, w_down, ple_gate_norm_g, w_ple_gate, b_ple_gate, w_ple, ple_norm_g, final_norm_g, loss_target, m_mix_norm_g, m_w_in, m_ssd_conv_w, m_ssd_conv_b, m_dt_bias, m_A_log, m_D_skip, m_ssd_norm_g, m_conf_dw_w, m_conf_dw_b, m_conf_ln_g, m_conf_ln_b, m_w_out, m_mlp_norm_g, m_w_up, m_w_down, m_ple_gate_norm_g, m_w_ple_gate, m_b_ple_gate, m_w_ple, m_ple_norm_g, m_final_norm_g, v_mix_norm_g, v_w_in, v_ssd_conv_w, v_ssd_conv_b, v_dt_bias, v_A_log, v_D_skip, v_ssd_norm_g, v_conf_dw_w, v_conf_dw_b, v_conf_ln_g, v_conf_ln_b, v_w_out, v_mlp_norm_g, v_w_up, v_w_down, v_ple_gate_norm_g, v_w_ple_gate, v_b_ple_gate, v_w_ple, v_ple_norm_g, v_final_norm_g):
    w = dict(mix_norm_g=mix_norm_g, w_in=w_in, ssd_conv_w=ssd_conv_w, ssd_conv_b=ssd_conv_b, dt_bias=dt_bias, A_log=A_log,
             D_skip=D_skip, ssd_norm_g=ssd_norm_g, conf_dw_w=conf_dw_w, conf_dw_b=conf_dw_b, conf_ln_g=conf_ln_g,
             conf_ln_b=conf_ln_b, w_out=w_out, mlp_norm_g=mlp_norm_g, w_up=w_up, w_down=w_down,
             ple_gate_norm_g=ple_gate_norm_g, w_ple_gate=w_ple_gate, b_ple_gate=b_ple_gate, w_ple=w_ple,
             ple_norm_g=ple_norm_g, final_norm_g=final_norm_g)
    m = dict(mix_norm_g=m_mix_norm_g, w_in=m_w_in, ssd_conv_w=m_ssd_conv_w, ssd_conv_b=m_ssd_conv_b, dt_bias=m_dt_bias,
             A_log=m_A_log, D_skip=m_D_skip, ssd_norm_g=m_ssd_norm_g, conf_dw_w=m_conf_dw_w, conf_dw_b=m_conf_dw_b,
             conf_ln_g=m_conf_ln_g, conf_ln_b=m_conf_ln_b, w_out=m_w_out, mlp_norm_g=m_mlp_norm_g, w_up=m_w_up,
             w_down=m_w_down, ple_gate_norm_g=m_ple_gate_norm_g, w_ple_gate=m_w_ple_gate, b_ple_gate=m_b_ple_gate,
             w_ple=m_w_ple, ple_norm_g=m_ple_norm_g, final_norm_g=m_final_norm_g)
    v = dict(mix_norm_g=v_mix_norm_g, w_in=v_w_in, ssd_conv_w=v_ssd_conv_w, ssd_conv_b=v_ssd_conv_b, dt_bias=v_dt_bias,
             A_log=v_A_log, D_skip=v_D_skip, ssd_norm_g=v_ssd_norm_g, conf_dw_w=v_conf_dw_w, conf_dw_b=v_conf_dw_b,
             conf_ln_g=v_conf_ln_g, conf_ln_b=v_conf_ln_b, w_out=v_w_out, mlp_norm_g=v_mlp_norm_g, w_up=v_w_up,
             w_down=v_w_down, ple_gate_norm_g=v_ple_gate_norm_g, w_ple_gate=v_w_ple_gate, b_ple_gate=v_b_ple_gate,
             w_ple=v_w_ple, ple_norm_g=v_ple_norm_g, final_norm_g=v_final_norm_g)
    xi, yi, ci = lax.axis_index("x"), lax.axis_index("y"), lax.axis_index("c")
    chip = 2 * xi + yi

    slab = jnp.concatenate([w_up[0], w_down[0], w_out[0], w_ple_gate[0], _rows(w_ple[0])], axis=0).astype(bf16)
    gath0 = lax.dynamic_update_slice(jnp.zeros((N_CHIPS, SLAB_A, D_MODEL), bf16), slab[None], (chip, 0, 0))
    wt_shard = jnp.swapaxes(w_in, 1, 2).astype(bf16)
    gin0 = lax.dynamic_update_slice(jnp.zeros((N_CHIPS, W_IN_ROWS_PAD, D_MODEL), bf16), wt_shard, (chip, 0, 0))
    convw = _pad_rows(jnp.concatenate([ssd_conv_w[0].reshape(-1), conf_dw_w[0].reshape(-1)]), CONVW_ROWS)
    gin, cwg = _gather_weights([gin0], convw)
    n_sc = SSD_CONV * (XBC_WIDTH // N_CHIPS)
    n_cf = CONF_KERNEL * (CONF_WIDTH // N_CHIPS)
    S = {n: w[n][0] for n in ("mix_norm_g", "ssd_conv_b", "dt_bias", "A_log", "D_skip", "ssd_norm_g", "conf_dw_b",
                              "conf_ln_g", "conf_ln_b", "mlp_norm_g", "ple_gate_norm_g", "b_ple_gate", "ple_norm_g")}
    S = {n: a.reshape(1, -1) for n, a in S.items()}
    S["final_norm_g"] = final_norm_g.reshape(1, -1)
    S["ssd_conv_w"] = jnp.concatenate(
        [cwg[b].reshape(-1)[:n_sc].reshape(SSD_CONV, XBC_WIDTH // N_CHIPS) for b in range(N_CHIPS)], axis=1)
    S["conf_dw_w"] = jnp.concatenate(
        [cwg[b].reshape(-1)[n_sc:n_sc + n_cf].reshape(CONF_KERNEL, CONF_WIDTH // N_CHIPS) for b in range(N_CHIPS)], axis=1)

    cidx = jnp.stack([ci, chip]).astype(jnp.int32)
    loss8, grad_x, ga, recv_a, ici_a, gw_in, gsmall = _local_step(x[0], p[0, 0], loss_target[0], gath0, cidx, gin, S)

    n_in = IN_WIDTH // N_CHIPS
    gb = _shard_rows(gw_in)
    gsmall = dict(gsmall)
    gsmall["loss"] = loss8[0:1, 0:1]
    small = _pack_small(gsmall)

    recv_b = _swap_halves(gb)
    hb = _chip_sum(cidx, gb, recv_b, "chip_sum_b")
    ici_b, all_small = _exchange(hb, small)
    ra = _final_sum(cidx, ga, recv_a, ici_a, "final_sum_a")
    rb = _final_sum(cidx, gb, recv_b, ici_b, "final_sum_b")
    ra, rb = _join_halves(ra, rb)
    tot_small = _sum_small(all_small)

    shapes = {n: (tuple(w[n].shape[1:]) if n != "final_norm_g" else (D_MODEL,)) for n in WEIGHTS if n not in BIG}
    shapes["ssd_conv_w"] = (SSD_CONV, XBC_WIDTH)
    shapes["conf_dw_w"] = (CONF_KERNEL, CONF_WIDTH)
    shapes["loss"] = (1,)
    tot = _unpack_small(tot_small, shapes)
    loss = tot["loss"].reshape(())
    n1, n2 = XBC_WIDTH // N_CHIPS, CONF_WIDTH // N_CHIPS
    tot["ssd_conv_w"] = lax.dynamic_slice(tot["ssd_conv_w"], (0, chip * n1), (SSD_CONV, n1))
    tot["conf_dw_w"] = lax.dynamic_slice(tot["conf_dw_w"], (0, chip * n2), (CONF_KERNEL, n2))

    g_in_t = rb[:W_IN_ROWS]
    grads = {"w_ple": _ple_of_slab(ra), "w_in": jnp.swapaxes(g_in_t, 0, 1)}
    for n, off in BIG_A:
        grads[n] = ra[off:off + w[n].shape[1]]
    for n in WEIGHTS:
        if n not in BIG:
            grads[n] = tot[n]
    grads = {n: g.reshape(w[n].shape) for n, g in grads.items()}

    delta, new_m, new_v = {}, {}, {}
    for n, off in BIG_A:
        d_, m_, v_ = _adamw(w[n][0], ra, m[n][0], v[n][0], "adamw_" + n, g_off=off)
        delta[n], new_m[n], new_v[n] = d_[None], m_[None], v_[None]
    d_, m_, v_ = _adamw(w_ple[0], grads["w_ple"][0], m_w_ple[0], v_w_ple[0], "adamw_w_ple")
    delta["w_ple"], new_m["w_ple"], new_v["w_ple"] = d_[None], m_[None], v_[None]
    tr_ = lambda a: jnp.swapaxes(a[0], 0, 1)
    d_, m_, v_ = _adamw(tr_(w_in), g_in_t, tr_(m_w_in), tr_(v_w_in), "adamw_w_in", by_columns=True)
    delta["w_in"], new_m["w_in"], new_v["w_in"] = (jnp.swapaxes(a, 0, 1)[None] for a in (d_, m_, v_))
    small_names = [n for n in WEIGHTS if n not in BIG]
    sizes = {n: int(w[n].size) for n in small_names}
    rows_needed = sum(-(-sizes[n] // D_MODEL) for n in small_names)
    rows_pad = -(-rows_needed // 8) * 8

    def pack(d):
        parts = [_pad_rows(d[n], -(-sizes[n] // D_MODEL)) for n in small_names]
        parts.append(jnp.zeros((rows_pad - rows_needed, D_MODEL), f32))
        return jnp.concatenate(parts, axis=0)

    sd, sm, sv = _adamw(pack(w), pack(grads), pack(m), pack(v), "adamw_small")

    def unpack(a, n, o):
        r = -(-sizes[n] // D_MODEL)
        return a[o:o + r].reshape(-1)[:sizes[n]].reshape(w[n].shape), o + r

    o = 0
    for n in small_names:
        delta[n], _ = unpack(sd, n, o)
        new_m[n], _ = unpack(sm, n, o)
        new_v[n], o = unpack(sv, n, o)

    return (loss, grad_x[None], *[grads[n] for n in WEIGHTS], *[delta[n] for n in WEIGHTS],
            *[new_m[n] for n in WEIGHTS], *[new_v[n] for n in WEIGHTS])
```

```python
import jax
import jax.numpy as jnp
from jax import lax
from jax.experimental import pallas as pl
from jax.experimental.pallas import tpu as pltpu

f32 = jnp.float32
bf16 = jnp.bfloat16

D_MODEL = 1024
SSD_WIDTH = 1024
SSD_HEADS = 16
HEAD_DIM = 64
SSD_STATE = 128
XBC_WIDTH = 1536
SSD_CONV = 4
CHUNK = 128
CONF_WIDTH = 1024
CONF_KERNEL = 31
D_FF = 4096
PLE_DIM = 256
IN_WIDTH = 4624
EPS = 1e-6
N_CHIPS = 4
N_DEV = 8

ADAM_LR = 0.001
ADAM_B1 = 0.9
ADAM_B2 = 0.999
ADAM_EPS = 1e-08
ADAM_WD = 0.01
ADAM_STEP = 10

LANES = 128
VMEM_BIG = 56 * 1024 * 1024
VMEM_MID = 40 * 1024 * 1024

UP_OFF, DOWN_OFF, OUT_OFF, PG_OFF, PLE_OFF = 0, 1024, 2048, 2560, 2816
PLE_ROWS = 64
SLAB_A = PLE_OFF + PLE_ROWS
GATHER_EARLY_ROWS = 480
W_IN_ROWS = 1156
W_IN_ROWS_PAD = 1184
CONVW_ROWS = 16
SMALL_ROWS = 56

MESH = pl.DeviceIdType.MESH
ANY = pl.BlockSpec(memory_space=pl.ANY)


def _cparams(sem=None, vmem=None):
    return pltpu.CompilerParams(dimension_semantics=sem, vmem_limit_bytes=vmem)


def _full(shape):
    n = len(shape)
    return pl.BlockSpec(shape, lambda *_: (0,) * n)


class _Rider:
    def __init__(self, inputs, out_shapes, aliases, n_sems, start, finish):
        self.inputs, self.out_shapes, self.aliases = list(inputs), list(out_shapes), dict(aliases)
        self.n_sems, self.start, self.finish = n_sems, start, finish


def _call(body, args, *, name, grid, in_specs, out_specs, out_shape, scratch_shapes=(), params=None, rider=None):
    if rider is None:
        return pl.pallas_call(body, name=name, grid=grid, in_specs=in_specs, out_specs=out_specs, out_shape=out_shape,
                              scratch_shapes=list(scratch_shapes), compiler_params=params)(*args)
    ni, no, ns = len(in_specs), len(out_specs), len(scratch_shapes)
    ri, ro = len(rider.inputs), len(rider.out_shapes)
    (steps,) = grid

    def with_rider(*refs):
        ins, refs = refs[:ni], refs[ni:]
        rins, refs = refs[:ri], refs[ri:]
        outs, refs = refs[:no], refs[no:]
        routs, refs = refs[:ro], refs[ro:]
        scratch, (ssem, rsem) = refs[:ns], refs[ns:]
        step = pl.program_id(0)

        @pl.when(step == 0)
        def _():
            rider.start(rins, routs, ssem, rsem)

        body(*ins, *outs, *scratch)

        @pl.when(step == steps - 1)
        def _():
            rider.finish(rins, routs, ssem, rsem)

    sems = [pltpu.SemaphoreType.DMA((rider.n_sems,)), pltpu.SemaphoreType.DMA((rider.n_sems,))]
    return pl.pallas_call(
        with_rider, name=name, grid=grid, in_specs=list(in_specs) + [ANY] * ri, out_specs=list(out_specs) + [ANY] * ro,
        out_shape=list(out_shape) + rider.out_shapes, scratch_shapes=list(scratch_shapes) + sems,
        input_output_aliases={ni + a: no + b for a, b in rider.aliases.items()}, compiler_params=params,
    )(*args, *rider.inputs)


def _dot(a, b):
    return jnp.dot(a, b, preferred_element_type=f32)


def _dot_nt(a, b):
    return lax.dot_general(a, b, (((1,), (1,)), ((), ())), preferred_element_type=f32)


def _dot_tn(a, b):
    return lax.dot_general(a, b, (((0,), (0,)), ((), ())), preferred_element_type=f32)


def _sigmoid(x):
    return jax.nn.sigmoid(x)


def _rms(x, g):
    r = lax.rsqrt(jnp.mean(x * x, axis=-1, keepdims=True) + EPS)
    return x * r * g


def _rms_bwd(dy, x, g):
    r = lax.rsqrt(jnp.mean(x * x, axis=-1, keepdims=True) + EPS)
    xh = x * r
    dg = jnp.sum(dy * xh, axis=0, keepdims=True)
    dxh = dy * g
    dx = r * (dxh - xh * jnp.mean(dxh * xh, axis=-1, keepdims=True))
    return dx, dg


def _dsilu(x):
    s = _sigmoid(x)
    return s * (1.0 + x * (1.0 - s))


def _split3(x):
    hi = x.astype(bf16)
    r1 = x - hi.astype(f32)
    mid = r1.astype(bf16)
    lo = (r1 - mid.astype(f32)).astype(bf16)
    return hi, mid, lo


def _head_matrix():
    row = lax.broadcasted_iota(jnp.int32, (LANES, SSD_WIDTH), 0)
    col = lax.broadcasted_iota(jnp.int32, (LANES, SSD_WIDTH), 1)
    lo = row * HEAD_DIM
    return ((col >= lo) & (col < lo + HEAD_DIM)).astype(bf16)


def _expand(x, e):
    hi, mid, lo = _split3(x)
    return _dot(hi, e) + _dot(mid, e) + _dot(lo, e)


def _contract(x, e):
    hi, mid, lo = _split3(x)
    return _dot_nt(hi, e) + _dot_nt(mid, e) + _dot_nt(lo, e)


O_XBC = SSD_WIDTH
O_DT = O_XBC + XBC_WIDTH
O_CV = O_DT + SSD_HEADS
O_CG = O_CV + CONF_WIDTH


def _assemble_w_in_t(gin_ref, wt_ref):
    for b in range(N_CHIPS):
        wt_ref[b * W_IN_ROWS:(b + 1) * W_IN_ROWS, :] = gin_ref[b, 0:W_IN_ROWS, :]


def _in_proj_fwd(x, g, gin, rider=None):
    T = x.shape[0]
    tm = min(256, T)

    def body(x_ref, g_ref, gin_ref, u_ref, z_ref, xbc_ref, cv_ref, cg_ref, dt_ref, v_ref, wt_ref):
        @pl.when(pl.program_id(0) == 0)
        def _():
            _assemble_w_in_t(gin_ref, wt_ref)

        ub = _rms(x_ref[...], g_ref[...]).astype(bf16)
        u_ref[...] = ub
        z_ref[...] = _dot_nt(ub, wt_ref[0:O_XBC, :])
        xbc_ref[...] = _dot_nt(ub, wt_ref[O_XBC:O_DT, :])
        cv = _dot_nt(ub, wt_ref[O_CV:O_CG, :])
        cg = _dot_nt(ub, wt_ref[O_CG:IN_WIDTH, :])
        cv_ref[...] = cv
        cg_ref[...] = cg
        v_ref[...] = cv * _sigmoid(cg)
        dt_ref[...] = _dot_nt(ub, wt_ref[O_DT:O_DT + LANES, :])

    row = lambda n: pl.BlockSpec((tm, n), lambda i: (i, 0))
    return _call(
        body, (x, g, gin), name="in_proj_fwd", grid=(T // tm,),
        in_specs=[row(D_MODEL), _full((1, D_MODEL)), _full(gin.shape)],
        out_specs=[row(D_MODEL), row(SSD_WIDTH), row(XBC_WIDTH), row(CONF_WIDTH), row(CONF_WIDTH), row(LANES),
                   row(CONF_WIDTH)],
        out_shape=[jax.ShapeDtypeStruct((T, D_MODEL), bf16), jax.ShapeDtypeStruct((T, SSD_WIDTH), f32),
                   jax.ShapeDtypeStruct((T, XBC_WIDTH), f32), jax.ShapeDtypeStruct((T, CONF_WIDTH), f32),
                   jax.ShapeDtypeStruct((T, CONF_WIDTH), f32), jax.ShapeDtypeStruct((T, LANES), f32),
                   jax.ShapeDtypeStruct((T, CONF_WIDTH), f32)],
        scratch_shapes=[pltpu.VMEM((IN_WIDTH, D_MODEL), bf16)],
        params=_cparams(("arbitrary",), VMEM_BIG), rider=rider)


SUBLANES = 8


def _phases(offsets):
    return sorted({o % SUBLANES for o in offsets} - {0})


def _phase_shape(offsets, tm, C):
    a_max = max([o // SUBLANES for o in offsets if o % SUBLANES] or [0])
    return (max(len(_phases(offsets)), 1), tm + SUBLANES * a_max, C)


def _make_phases(buf_ref, ph_ref, offsets, tm):
    for idx, b in enumerate(_phases(offsets)):
        n = tm + SUBLANES * max(o // SUBLANES for o in offsets if o % SUBLANES == b)
        ph_ref[idx, 0:n, :] = buf_ref[pl.ds(b, n), :]


def _window(buf_ref, ph_ref, offsets, o, r0, rb):
    a, b = divmod(o, SUBLANES)
    if b == 0:
        return buf_ref[pl.ds(r0 + SUBLANES * a, rb), :]
    return ph_ref[_phases(offsets).index(b), pl.ds(r0 + SUBLANES * a, rb), :]


def _conv_rows(wb_ref, buf_ref, ph_ref, offsets, r0, rb):
    nsub = rb // SUBLANES
    accs = [None] * nsub
    for k, o in enumerate(offsets):
        wk = wb_ref[pl.ds(SUBLANES * k, SUBLANES), :]
        for s in range(nsub):
            term = wk * _window(buf_ref, ph_ref, offsets, o, r0 + SUBLANES * s, SUBLANES)
            accs[s] = term if accs[s] is None else accs[s] + term
    return accs[0] if nsub == 1 else jnp.concatenate(accs, axis=0)


def _sublane_rows(w):
    return jnp.repeat(w, SUBLANES, axis=0)


def _fwd_offsets(K, hb):
    return [hb - (K - 1) + k for k in range(K)]


def _prev_halo_spec(hb, tm, C):
    return pl.BlockSpec((hb, C), lambda i: (jnp.maximum(i * (tm // hb) - 1, 0), 0))


CONV_RB = 16


def _ssd_conv_fwd(xbc, w, b):
    T, C = xbc.shape
    K, hb = SSD_CONV, 8
    tm = min(256, T)
    offs = _fwd_offsets(K, hb)

    def body(cur_ref, halo_ref, w_ref, b_ref, pre_ref, buf_ref, ph_ref):
        keep = jnp.where(pl.program_id(0) > 0, 1.0, 0.0)
        buf_ref[0:hb, :] = halo_ref[...] * keep
        buf_ref[hb:hb + tm, :] = cur_ref[...]
        _make_phases(buf_ref, ph_ref, offs, tm)

        def chunk(i, carry):
            r0 = pl.multiple_of(i * CONV_RB, CONV_RB)
            pre_ref[pl.ds(r0, CONV_RB), :] = _conv_rows(w_ref, buf_ref, ph_ref, offs, r0, CONV_RB) + b_ref[...]
            return carry

        lax.fori_loop(0, tm // CONV_RB, chunk, 0)

    return pl.pallas_call(
        body, name="ssd_conv_fwd", grid=(T // tm,),
        in_specs=[pl.BlockSpec((tm, C), lambda i: (i, 0)), _prev_halo_spec(hb, tm, C), _full((SUBLANES * K, C)),
                  _full((1, C))],
        out_specs=pl.BlockSpec((tm, C), lambda i: (i, 0)),
        out_shape=jax.ShapeDtypeStruct((T, C), f32),
        scratch_shapes=[pltpu.VMEM((hb + tm, C), f32), pltpu.VMEM(_phase_shape(offs, tm, C), f32)],
        compiler_params=_cparams(("parallel",), VMEM_MID),
    )(xbc, xbc, _sublane_rows(w), b)


def _conf_fwd(v, w, b, ln_g, ln_b, rider=None):
    T, C = v.shape
    K, hb = CONF_KERNEL, 32
    tm = min(256, T)
    offs = _fwd_offsets(K, hb)
    rb = 2 * CONV_RB

    def body(cur_ref, halo_ref, w_ref, b_ref, g_ref, bb_ref, co_ref, y_ref, buf_ref, ph_ref):
        keep = jnp.where(pl.program_id(0) > 0, 1.0, 0.0)
        buf_ref[0:hb, :] = halo_ref[...] * keep
        buf_ref[hb:hb + tm, :] = cur_ref[...]
        _make_phases(buf_ref, ph_ref, offs, tm)

        def chunk(i, carry):
            r0 = pl.multiple_of(i * rb, rb)
            co = _conv_rows(w_ref, buf_ref, ph_ref, offs, r0, rb) + b_ref[...]
            co_ref[pl.ds(r0, rb), :] = co
            mu = jnp.mean(co, axis=-1, keepdims=True)
            xc = co - mu
            yn = xc * lax.rsqrt(jnp.mean(xc * xc, axis=-1, keepdims=True) + EPS) * g_ref[...] + bb_ref[...]
            y_ref[pl.ds(r0, rb), :] = (yn * _sigmoid(yn)).astype(bf16)
            return carry

        lax.fori_loop(0, tm // rb, chunk, 0)

    return _call(
        body, (v, v, _sublane_rows(w), b, ln_g, ln_b), name="conf_fwd", grid=(T // tm,),
        in_specs=[pl.BlockSpec((tm, C), lambda i: (i, 0)), _prev_halo_spec(hb, tm, C), _full((SUBLANES * K, C)),
                  _full((1, C)), _full((1, C)), _full((1, C))],
        out_specs=[pl.BlockSpec((tm, C), lambda i: (i, 0)), pl.BlockSpec((tm, C), lambda i: (i, 0))],
        out_shape=[jax.ShapeDtypeStruct((T, C), f32), jax.ShapeDtypeStruct((T, C), bf16)],
        scratch_shapes=[pltpu.VMEM((hb + tm, C), f32), pltpu.VMEM(_phase_shape(offs, tm, C), f32)],
        params=_cparams(("arbitrary",), VMEM_MID), rider=rider)


def _ssd_chunk_common(pre, dtr, dtb, alog, e):
    act = pre * _sigmoid(pre)
    xs = act[:, :SSD_WIDTH]
    bm = act[:, SSD_WIDTH:SSD_WIDTH + 2 * SSD_STATE]
    cm = act[:, SSD_WIDTH + 2 * SSD_STATE:]
    row = lax.broadcasted_iota(jnp.int32, (CHUNK, CHUNK), 0)
    col = lax.broadcasted_iota(jnp.int32, (CHUNK, CHUNK), 1)
    tri = row >= col
    dt = jax.nn.softplus(dtr + dtb)
    a_neg = -jnp.exp(alog)
    a = dt * a_neg
    cs = jnp.dot(tri.astype(f32), a, precision=lax.Precision.HIGHEST, preferred_element_type=f32)
    cs_e = _expand(cs, e)
    dt_e = _expand(dt, e)
    csl_e = cs_e[CHUNK - 1:CHUNK, :]
    ecs_e = jnp.exp(cs_e)
    dte_e = jnp.exp(csl_e - cs_e)
    cd_e = jnp.exp(csl_e)
    xc = xs * dt_e
    xd = xc * dte_e
    return dict(xs=xs, bm=bm, cm=cm, tri=tri, dt=dt, a_neg=a_neg, cs=cs, ecs_e=ecs_e, dte_e=dte_e, cd_e=cd_e,
                dt_e=dt_e, xc=xc, xd=xd)


def _group(v, g, width):
    return v[:, g * width:(g + 1) * width]


def _ssd_fwd(pre, dtr, z, dtb, alog, dskip_e, gn):
    T = pre.shape[0]
    nc = T // CHUNK
    GW = SSD_WIDTH // 2

    def body(pre_ref, dtr_ref, z_ref, dtb_ref, alog_ref, de_ref, gn_ref, y_ref, ys_ref, sp_ref, st_ref):
        @pl.when(pl.program_id(0) == 0)
        def _():
            st_ref[...] = jnp.zeros_like(st_ref)

        e = _head_matrix()
        q = _ssd_chunk_common(pre_ref[...], dtr_ref[...], dtb_ref[...], alog_ref[...], e)
        cs, tri, xc, xd = q["cs"], q["tri"], q["xc"], q["xd"]
        cs_t = cs.T
        st = st_ref[...]
        sp_ref[0] = st
        lane = lax.broadcasted_iota(jnp.int32, (1, LANES), 1)
        halves = (lane < HEAD_DIM, lane >= HEAD_DIM)

        g_mat, y_off, s_new = [], [], []
        for g in range(2):
            bg = _group(q["bm"], g, SSD_STATE)
            cg = _group(q["cm"], g, SSD_STATE)
            bgb, cgb = bg.astype(bf16), cg.astype(bf16)
            g_mat.append(_dot_nt(cgb, bgb))
            y_off.append(_dot(cgb, _group(st, g, GW).astype(bf16)))
            s_new.append(_dot(bg.T.astype(bf16), _group(xd, g, GW).astype(bf16)))
        y_off = jnp.concatenate(y_off, axis=1) * q["ecs_e"]
        st_ref[...] = st * q["cd_e"] + jnp.concatenate(s_new, axis=1)

        pairs = []
        for j in range(SSD_HEADS // 2):
            xp = xc[:, j * LANES:(j + 1) * LANES]
            acc = jnp.zeros((CHUNK, LANES), f32)
            for hh in range(2):
                h = 2 * j + hh
                seg = cs[:, h:h + 1] - cs_t[h:h + 1, :]
                lm = jnp.exp(jnp.where(tri, seg, -1e30))
                m = (g_mat[h // 8] * lm).astype(bf16)
                acc = acc + _dot(m, jnp.where(halves[hh], xp, 0.0).astype(bf16))
            pairs.append(acc)
        y = jnp.concatenate(pairs, axis=1) + y_off + q["xs"] * de_ref[...]
        y_ref[...] = y

        zz = z_ref[...]
        v = y * (zz * _sigmoid(zz))
        outs = []
        for g in range(2):
            vg = _group(v, g, GW)
            outs.append(vg * lax.rsqrt(jnp.mean(vg * vg, axis=-1, keepdims=True) + EPS))
        ys_ref[...] = (jnp.concatenate(outs, axis=1) * gn_ref[...]).astype(bf16)

    ch = lambda n: pl.BlockSpec((CHUNK, n), lambda c: (c, 0))
    return pl.pallas_call(
        body, name="ssd_fwd", grid=(nc,),
        in_specs=[ch(XBC_WIDTH), ch(LANES), ch(SSD_WIDTH), _full((1, LANES)), _full((1, LANES)), _full((1, SSD_WIDTH)),
                  _full((1, SSD_WIDTH))],
        out_specs=[ch(SSD_WIDTH), ch(SSD_WIDTH), pl.BlockSpec((1, SSD_STATE, SSD_WIDTH), lambda c: (c, 0, 0))],
        out_shape=[jax.ShapeDtypeStruct((T, SSD_WIDTH), f32), jax.ShapeDtypeStruct((T, SSD_WIDTH), bf16),
                   jax.ShapeDtypeStruct((nc, SSD_STATE, SSD_WIDTH), f32)],
        scratch_shapes=[pltpu.VMEM((SSD_STATE, SSD_WIDTH), f32)],
        compiler_params=_cparams(("arbitrary",), VMEM_MID),
    )(pre, dtr, z, dtb, alog, dskip_e, gn)


def _w_out_spec():
    n = 2 * SSD_WIDTH // N_CHIPS
    return pl.BlockSpec((N_CHIPS, n, D_MODEL), lambda *_: (0, OUT_OFF // n, 0))


def _out_proj_fwd(x, ys, yc, gath, g):
    T = x.shape[0]
    tm = min(512, T)
    n = 2 * SSD_WIDTH // N_CHIPS

    def body(x_ref, ys_ref, yc_ref, w_ref, g_ref, h_ref, u_ref):
        h = (x_ref[...] + _dot(ys_ref[:, 0:n], w_ref[0]) + _dot(ys_ref[:, n:], w_ref[1])
             + _dot(yc_ref[:, 0:n], w_ref[2]) + _dot(yc_ref[:, n:], w_ref[3]))
        h_ref[...] = h
        u_ref[...] = _rms(h, g_ref[...]).astype(bf16)

    row = pl.BlockSpec((tm, D_MODEL), lambda i: (i, 0))
    return pl.pallas_call(
        body, name="out_proj_fwd", grid=(T // tm,),
        in_specs=[row, row, row, _w_out_spec(), _full((1, D_MODEL))],
        out_specs=[row, row],
        out_shape=[jax.ShapeDtypeStruct((T, D_MODEL), f32), jax.ShapeDtypeStruct((T, D_MODEL), bf16)],
        compiler_params=_cparams(("parallel",), VMEM_MID),
    )(x, ys, yc, gath, g)


def _w_up_spec():
    return pl.BlockSpec((1, D_MODEL, D_MODEL), lambda i, b: (b, UP_OFF // D_MODEL, 0))


def _w_down_spec():
    return pl.BlockSpec((1, D_MODEL, D_MODEL), lambda i, b: (b, DOWN_OFF // D_MODEL, 0))


def _mlp_fwd(h1, u1, gath, g_next):
    T = h1.shape[0]
    tm = min(512, T)
    nb = D_FF // D_MODEL

    def body(h_ref, u_ref, wu_ref, wd_ref, g_ref, r_ref, h2_ref, u2_ref, acc_ref):
        b = pl.program_id(1)

        @pl.when(b == 0)
        def _():
            acc_ref[...] = jnp.zeros_like(acc_ref)

        r = jnp.maximum(_dot(u_ref[...], wu_ref[0]), 0.0)
        r_ref[...] = r.astype(bf16)
        acc_ref[...] += _dot((r * r).astype(bf16), wd_ref[0])

        @pl.when(b == nb - 1)
        def _():
            h2 = h_ref[...] + acc_ref[...]
            h2_ref[...] = h2
            u2_ref[...] = _rms(h2, g_ref[...]).astype(bf16)

    row = pl.BlockSpec((tm, D_MODEL), lambda i, b: (i, 0))
    return pl.pallas_call(
        body, name="mlp_fwd", grid=(T // tm, nb),
        in_specs=[row, row, _w_up_spec(), _w_down_spec(), _full((1, D_MODEL))],
        out_specs=[pl.BlockSpec((tm, D_MODEL), lambda i, b: (i, b)), row, row],
        out_shape=[jax.ShapeDtypeStruct((T, D_FF), bf16), jax.ShapeDtypeStruct((T, D_MODEL), f32),
                   jax.ShapeDtypeStruct((T, D_MODEL), bf16)],
        scratch_shapes=[pltpu.VMEM((tm, D_MODEL), f32)],
        compiler_params=_cparams(("parallel", "arbitrary"), VMEM_MID),
    )(h1, u1, gath, gath, g_next)


def _ple_loss(h2, u2, p, tgt, gath, b_pg, w_ple, g_ple, g_fin, g_pg):
    T = h2.shape[0]
    tm = min(256, T)
    npg = D_MODEL // N_CHIPS

    def body(h2_ref, u2_ref, p_ref, t_ref, wpg_ref, bpg_ref, wple_ref, gple_ref, gfin_ref, gpg_ref,
             loss_ref, dh2_ref, dh2b_ref, dgp_ref, dep_ref, dgfin_ref, dgple_ref, dbpg_ref, dgpg_ref):
        @pl.when(pl.program_id(0) == 0)
        def _():
            loss_ref[...] = jnp.zeros_like(loss_ref)
            dgfin_ref[...] = jnp.zeros_like(dgfin_ref)
            dgple_ref[...] = jnp.zeros_like(dgple_ref)
            dbpg_ref[...] = jnp.zeros_like(dbpg_ref)
            dgpg_ref[...] = jnp.zeros_like(dgpg_ref)

        h2 = h2_ref[...]
        gate_pre = bpg_ref[...]
        for b in range(N_CHIPS):
            gate_pre = gate_pre + _dot(u2_ref[:, b * npg:(b + 1) * npg], wpg_ref[b])
        gate = _sigmoid(gate_pre)
        e_pre = _dot(p_ref[...].astype(bf16), wple_ref[...])
        emb = _rms(e_pre, gple_ref[...])
        h3 = h2 + gate * emb
        diff = _rms(h3, gfin_ref[...]) - t_ref[...]
        sq = jnp.sum(jnp.sum(diff * diff, axis=1, keepdims=True), axis=0, keepdims=True)
        loss_ref[...] += (0.5 / D_MODEL) * sq
        dh3, dgfin = _rms_bwd(diff * (1.0 / D_MODEL), h3, gfin_ref[...])
        dgfin_ref[...] += dgfin
        dgp = dh3 * emb * gate * (1.0 - gate)
        dbpg_ref[...] += jnp.sum(dgp, axis=0, keepdims=True)
        dep, dgple = _rms_bwd(dh3 * gate, e_pre, gple_ref[...])
        dgple_ref[...] += dgple
        dgpb = dgp.astype(bf16)
        dgp_ref[...] = dgpb
        dep_ref[...] = dep.astype(bf16)
        du2 = jnp.concatenate([_dot_nt(dgpb, wpg_ref[b]) for b in range(N_CHIPS)], axis=1)
        dx, dgpg = _rms_bwd(du2, h2, gpg_ref[...])
        dgpg_ref[...] += dgpg
        dh2 = dh3 + dx
        dh2_ref[...] = dh2
        dh2b_ref[...] = dh2.astype(bf16)

    row = pl.BlockSpec((tm, D_MODEL), lambda i: (i, 0))
    vec = _full((1, D_MODEL))
    vshape = jax.ShapeDtypeStruct((1, D_MODEL), f32)
    return pl.pallas_call(
        body, name="ple_loss", grid=(T // tm,),
        in_specs=[row, row, pl.BlockSpec((tm, PLE_DIM), lambda i: (i, 0)), row,
                  pl.BlockSpec((N_CHIPS, npg, D_MODEL), lambda i: (0, PG_OFF // npg, 0)), vec, _full(w_ple.shape),
                  vec, vec, vec],
        out_specs=[_full((8, LANES)), row, row, row, row, vec, vec, vec, vec],
        out_shape=[jax.ShapeDtypeStruct((8, LANES), f32), jax.ShapeDtypeStruct((T, D_MODEL), f32),
                   jax.ShapeDtypeStruct((T, D_MODEL), bf16), jax.ShapeDtypeStruct((T, D_MODEL), bf16),
                   jax.ShapeDtypeStruct((T, D_MODEL), bf16), vshape, vshape, vshape, vshape],
        compiler_params=_cparams(("arbitrary",), VMEM_MID),
    )(h2, u2, p, tgt, gath, b_pg, w_ple, g_ple, g_fin, g_pg)


def _mlp_bwd(dh2, r, gath, h1, g):
    T = dh2.shape[0]
    tm = min(512, T)
    nb = D_FF // D_MODEL

    def body(dh2_ref, r_ref, wd_ref, wu_ref, h1_ref, g_ref, dhp_ref, dh1_ref, dh1b_ref, dg_ref, acc_ref):
        i, b = pl.program_id(0), pl.program_id(1)

        @pl.when(b == 0)
        def _():
            acc_ref[...] = jnp.zeros_like(acc_ref)

        @pl.when((b == 0) & (i == 0))
        def _():
            dg_ref[...] = jnp.zeros_like(dg_ref)

        dact = _dot_nt(dh2_ref[...].astype(bf16), wd_ref[0])
        dhp = (dact * 2.0 * r_ref[...].astype(f32)).astype(bf16)
        dhp_ref[...] = dhp
        acc_ref[...] += _dot_nt(dhp, wu_ref[0])

        @pl.when(b == nb - 1)
        def _():
            dx, dg = _rms_bwd(acc_ref[...], h1_ref[...], g_ref[...])
            dg_ref[...] += dg
            dh1 = dh2_ref[...] + dx
            dh1_ref[...] = dh1
            dh1b_ref[...] = dh1.astype(bf16)

    row = pl.BlockSpec((tm, D_MODEL), lambda i, b: (i, 0))
    return pl.pallas_call(
        body, name="mlp_bwd", grid=(T // tm, nb),
        in_specs=[row, pl.BlockSpec((tm, D_MODEL), lambda i, b: (i, b)), _w_down_spec(), _w_up_spec(), row,
                  _full((1, D_MODEL))],
        out_specs=[pl.BlockSpec((tm, D_MODEL), lambda i, b: (i, b)), row, row, _full((1, D_MODEL))],
        out_shape=[jax.ShapeDtypeStruct((T, D_FF), bf16), jax.ShapeDtypeStruct((T, D_MODEL), f32),
                   jax.ShapeDtypeStruct((T, D_MODEL), bf16), jax.ShapeDtypeStruct((1, D_MODEL), f32)],
        scratch_shapes=[pltpu.VMEM((tm, D_MODEL), f32)],
        compiler_params=_cparams(("arbitrary", "arbitrary"), VMEM_MID),
    )(dh2, r, gath, gath, h1, g)


def _out_proj_bwd(dh1, gath, co, ln_g, ln_b, rider=None):
    T = dh1.shape[0]
    tm = min(512, T)

    def body(dh_ref, w_ref, co_ref, g_ref, b_ref, dys_ref, dco_ref, dg_ref, db_ref):
        @pl.when(pl.program_id(0) == 0)
        def _():
            dg_ref[...] = jnp.zeros_like(dg_ref)
            db_ref[...] = jnp.zeros_like(db_ref)

        dhb = dh_ref[...].astype(bf16)
        dys_ref[...] = jnp.concatenate([_dot_nt(dhb, w_ref[0]), _dot_nt(dhb, w_ref[1])], axis=1)
        dyc = jnp.concatenate([_dot_nt(dhb, w_ref[2]), _dot_nt(dhb, w_ref[3])], axis=1)
        co = co_ref[...]
        mu = jnp.mean(co, axis=-1, keepdims=True)
        xc = co - mu
        rstd = lax.rsqrt(jnp.mean(xc * xc, axis=-1, keepdims=True) + EPS)
        xh = xc * rstd
        yn = xh * g_ref[...] + b_ref[...]
        dyn = dyc * _dsilu(yn)
        dg_ref[...] += jnp.sum(dyn * xh, axis=0, keepdims=True)
        db_ref[...] += jnp.sum(dyn, axis=0, keepdims=True)
        dxh = dyn * g_ref[...]
        dco_ref[...] = rstd * (dxh - jnp.mean(dxh, axis=-1, keepdims=True)
                               - xh * jnp.mean(dxh * xh, axis=-1, keepdims=True))

    row = pl.BlockSpec((tm, D_MODEL), lambda i: (i, 0))
    vec = _full((1, CONF_WIDTH))
    vshape = jax.ShapeDtypeStruct((1, CONF_WIDTH), f32)
    return _call(
        body, (dh1, gath, co, ln_g, ln_b), name="out_proj_bwd", grid=(T // tm,),
        in_specs=[row, _w_out_spec(), row, vec, vec],
        out_specs=[row, row, vec, vec],
        out_shape=[jax.ShapeDtypeStruct((T, SSD_WIDTH), f32), jax.ShapeDtypeStruct((T, CONF_WIDTH), f32), vshape, vshape],
        params=_cparams(("arbitrary",), VMEM_MID), rider=rider)


def _bwd_offsets(K):
    return [K - 1 - k for k in range(K)]


def _next_halo_spec(hb, tm, C, T):
    return pl.BlockSpec((hb, C), lambda i: (jnp.minimum((i + 1) * (tm // hb), T // hb - 1), 0))


DW_RB = 8
DW_UNROLL = 4
DW_ACC_VREGS = 32


def _conv_dw(dw_ref, bufd_ref, bufx_ref, phx_ref, offs_x, tm, C):
    K = len(offs_x)
    group = max(1, DW_ACC_VREGS // (C // LANES))
    for k0 in range(0, K, group):
        ks = list(range(k0, min(k0 + group, K)))

        def step(i, accs, ks=ks):
            for u in range(DW_UNROLL):
                r0 = pl.multiple_of((i * DW_UNROLL + u) * DW_RB, DW_RB)
                d = bufd_ref[pl.ds(r0, DW_RB), :]
                accs = tuple(acc + _window(bufx_ref, phx_ref, offs_x, offs_x[k], r0, DW_RB) * d
                             for k, acc in zip(ks, accs))
            return accs

        accs = lax.fori_loop(0, tm // (DW_RB * DW_UNROLL), step, tuple(jnp.zeros((DW_RB, C), f32) for _ in ks))
        for k, acc in zip(ks, accs):
            dw_ref[k:k + 1, :] += jnp.sum(acc, axis=0, keepdims=True)


def _fill_bwd_buffers(dcur_ref, dnext_ref, xcur_ref, xprev_ref, bufd_ref, bufx_ref, phd_ref, phx_ref, offs_d, offs_x,
                      hb, tm, first, last):
    bufd_ref[0:tm, :] = dcur_ref[...]
    bufd_ref[tm:tm + hb, :] = dnext_ref[...] * jnp.where(last, 0.0, 1.0)
    bufx_ref[0:hb, :] = xprev_ref[...] * jnp.where(first, 0.0, 1.0)
    bufx_ref[hb:hb + tm, :] = xcur_ref[...]
    _make_phases(bufd_ref, phd_ref, offs_d, tm)
    _make_phases(bufx_ref, phx_ref, offs_x, tm)


def _ssd_conv_bwd(dpre, xbc, w):
    T, C = xbc.shape
    K, hb = SSD_CONV, 8
    tm = min(256, T)
    nt = T // tm
    offs_d, offs_x = _bwd_offsets(K), _fwd_offsets(K, hb)

    def body(dcur_ref, dnext_ref, xcur_ref, xprev_ref, w_ref, dx_ref, dw_ref, db_ref, bufd_ref, bufx_ref, phd_ref, phx_ref):
        i = pl.program_id(0)

        @pl.when(i == 0)
        def _():
            dw_ref[...] = jnp.zeros_like(dw_ref)
            db_ref[...] = jnp.zeros_like(db_ref)

        _fill_bwd_buffers(dcur_ref, dnext_ref, xcur_ref, xprev_ref, bufd_ref, bufx_ref, phd_ref, phx_ref, offs_d, offs_x,
                          hb, tm, i == 0, i == nt - 1)

        def chunk(j, carry):
            r0 = pl.multiple_of(j * CONV_RB, CONV_RB)
            dx_ref[pl.ds(r0, CONV_RB), :] = _conv_rows(w_ref, bufd_ref, phd_ref, offs_d, r0, CONV_RB).astype(bf16)
            return carry

        lax.fori_loop(0, tm // CONV_RB, chunk, 0)
        _conv_dw(dw_ref, bufd_ref, bufx_ref, phx_ref, offs_x, tm, C)
        db_ref[...] += jnp.sum(dcur_ref[...], axis=0, keepdims=True)

    row = pl.BlockSpec((tm, C), lambda i: (i, 0))
    return pl.pallas_call(
        body, name="ssd_conv_bwd", grid=(nt,),
        in_specs=[row, _next_halo_spec(hb, tm, C, T), row, _prev_halo_spec(hb, tm, C), _full((SUBLANES * K, C))],
        out_specs=[row, _full((8, C)), _full((1, C))],
        out_shape=[jax.ShapeDtypeStruct((T, C), bf16), jax.ShapeDtypeStruct((8, C), f32), jax.ShapeDtypeStruct((1, C), f32)],
        scratch_shapes=[pltpu.VMEM((tm + hb, C), f32), pltpu.VMEM((hb + tm, C), f32),
                        pltpu.VMEM(_phase_shape(offs_d, tm, C), f32),
                        pltpu.VMEM(_phase_shape(offs_x, tm, C), f32)],
        compiler_params=_cparams(("arbitrary",), VMEM_BIG),
    )(dpre, dpre, xbc, xbc, _sublane_rows(w))


def _conf_conv_bwd(dco, v, w, cv, cg, rider=None):
    T, C = v.shape
    K, hb = CONF_KERNEL, 32
    tm = min(256, T)
    nt = T // tm
    offs_d, offs_x = _bwd_offsets(K), _fwd_offsets(K, hb)

    def body(dcur_ref, dnext_ref, vcur_ref, vprev_ref, w_ref, cv_ref, cg_ref, dcv_ref, dcg_ref, dw_ref, db_ref,
             bufd_ref, bufx_ref, phd_ref, phx_ref):
        i = pl.program_id(0)

        @pl.when(i == 0)
        def _():
            dw_ref[...] = jnp.zeros_like(dw_ref)
            db_ref[...] = jnp.zeros_like(db_ref)

        _fill_bwd_buffers(dcur_ref, dnext_ref, vcur_ref, vprev_ref, bufd_ref, bufx_ref, phd_ref, phx_ref, offs_d, offs_x,
                          hb, tm, i == 0, i == nt - 1)

        def chunk(j, carry):
            r0 = pl.multiple_of(j * CONV_RB, CONV_RB)
            rows = pl.ds(r0, CONV_RB)
            dv = _conv_rows(w_ref, bufd_ref, phd_ref, offs_d, r0, CONV_RB)
            s = _sigmoid(cg_ref[rows, :])
            dcv_ref[rows, :] = (dv * s).astype(bf16)
            dcg_ref[rows, :] = (dv * cv_ref[rows, :] * s * (1.0 - s)).astype(bf16)
            return carry

        lax.fori_loop(0, tm // CONV_RB, chunk, 0)
        _conv_dw(dw_ref, bufd_ref, bufx_ref, phx_ref, offs_x, tm, C)
        db_ref[...] += jnp.sum(dcur_ref[...], axis=0, keepdims=True)

    row = pl.BlockSpec((tm, C), lambda i: (i, 0))
    return _call(
        body, (dco, dco, v, v, _sublane_rows(w), cv, cg), name="conf_conv_bwd", grid=(nt,),
        in_specs=[row, _next_halo_spec(hb, tm, C, T), row, _prev_halo_spec(hb, tm, C), _full((SUBLANES * K, C)), row, row],
        out_specs=[row, row, _full((32, C)), _full((1, C))],
        out_shape=[jax.ShapeDtypeStruct((T, C), bf16), jax.ShapeDtypeStruct((T, C), bf16),
                   jax.ShapeDtypeStruct((32, C), f32), jax.ShapeDtypeStruct((1, C), f32)],
        scratch_shapes=[pltpu.VMEM((tm + hb, C), f32), pltpu.VMEM((hb + tm, C), f32),
                        pltpu.VMEM(_phase_shape(offs_d, tm, C), f32),
                        pltpu.VMEM(_phase_shape(offs_x, tm, C), f32)],
        params=_cparams(("arbitrary",), VMEM_BIG), rider=rider)


def _ssd_bwd(dys, y, z, pre, dtr, sprev, dtb, alog, dskip_e, gn):
    T = pre.shape[0]
    nc = T // CHUNK
    GW = SSD_WIDTH // 2

    def body(dys_ref, y_ref, z_ref, pre_ref, dtr_ref, sp_ref, dtb_ref, alog_ref, de_ref, gn_ref,
             dz_ref, dpre_ref, ddtr_ref, dgn_ref, dd_ref, dal_ref, ddtb_ref, ds_ref):
        @pl.when(pl.program_id(0) == 0)
        def _():
            ds_ref[...] = jnp.zeros_like(ds_ref)
            dgn_ref[...] = jnp.zeros_like(dgn_ref)
            dd_ref[...] = jnp.zeros_like(dd_ref)
            dal_ref[...] = jnp.zeros_like(dal_ref)
            ddtb_ref[...] = jnp.zeros_like(ddtb_ref)

        e = _head_matrix()
        pre = pre_ref[...]
        dtr_b = dtr_ref[...] + dtb_ref[...]
        q = _ssd_chunk_common(pre, dtr_ref[...], dtb_ref[...], alog_ref[...], e)
        cs, tri, xc, xd, xs, dt = q["cs"], q["tri"], q["xc"], q["xd"], q["xs"], q["dt"]
        cs_t = cs.T
        st = sp_ref[0]
        dsn = ds_ref[...]
        lane = lax.broadcasted_iota(jnp.int32, (1, LANES), 1)
        halves = (lane < HEAD_DIM, lane >= HEAD_DIM)
        row_i = lax.broadcasted_iota(jnp.int32, (CHUNK, CHUNK), 0)
        col_i = lax.broadcasted_iota(jnp.int32, (CHUNK, CHUNK), 1)
        tri_t = col_i >= row_i

        y = y_ref[...]
        zz = z_ref[...]
        sz = _sigmoid(zz)
        silu_z = zz * sz
        v = y * silu_z
        dout = dys_ref[...]
        gn_v = gn_ref[...]
        dv, vh = [], []
        for g in range(2):
            vg = _group(v, g, GW)
            rstd = lax.rsqrt(jnp.mean(vg * vg, axis=-1, keepdims=True) + EPS)
            vhg = vg * rstd
            dvh = _group(dout, g, GW) * _group(gn_v, g, GW)
            dv.append(rstd * (dvh - vhg * jnp.mean(dvh * vhg, axis=-1, keepdims=True)))
            vh.append(vhg)
        dv = jnp.concatenate(dv, axis=1)
        dgn_ref[...] += jnp.sum(dout * jnp.concatenate(vh, axis=1), axis=0, keepdims=True)
        dy = dv * silu_z
        dz_ref[...] = (dv * y * (sz * (1.0 + zz * (1.0 - sz)))).astype(bf16)

        dd_row = jnp.sum(dy * xs, axis=0, keepdims=True)
        dd_ref[...] += _contract(jnp.broadcast_to(dd_row, (8, SSD_WIDTH)), e)[0:1, :]
        dxs = dy * de_ref[...]

        dz_in = dy * q["ecs_e"]
        g_mat, gt_mat, dcm, dbm, dsp, dxd, y_off = [], [], [], [], [], [], []
        bgs, cgs = [], []
        for g in range(2):
            bg = _group(q["bm"], g, SSD_STATE)
            cg = _group(q["cm"], g, SSD_STATE)
            bgb, cgb = bg.astype(bf16), cg.astype(bf16)
            bgs.append(bgb)
            cgs.append(cgb)
            stg = _group(st, g, GW).astype(bf16)
            dsng = _group(dsn, g, GW).astype(bf16)
            dzg = _group(dz_in, g, GW).astype(bf16)
            g_mat.append(_dot_nt(cgb, bgb))
            gt_mat.append(_dot_nt(bgb, cgb))
            y_off.append(_dot(cgb, stg))
            dcm.append(_dot_nt(dzg, stg))
            dsp.append(_dot(cg.T.astype(bf16), dzg))
            dbm.append(_dot_nt(_group(xd, g, GW).astype(bf16), dsng))
            dxd.append(_dot(bgb, dsng))
        y_off = jnp.concatenate(y_off, axis=1) * q["ecs_e"]
        dxd = jnp.concatenate(dxd, axis=1)
        ds_ref[...] = dsn * q["cd_e"] + jnp.concatenate(dsp, axis=1)
        dcd_row = jnp.sum(dsn * st, axis=0, keepdims=True) * q["cd_e"]
        t_e = dxd * xd
        dcs = _contract(dy * y_off - t_e, e)
        last_row = _contract(jnp.broadcast_to(dcd_row + jnp.sum(t_e, axis=0, keepdims=True), (8, SSD_WIDTH)), e)[0:1, :]
        dxc_state = dxd * q["dte_e"]

        dg_acc = [jnp.zeros((CHUNK, CHUNK), f32), jnp.zeros((CHUNK, CHUNK), f32)]
        dgt_acc = [jnp.zeros((CHUNK, CHUNK), f32), jnp.zeros((CHUNK, CHUNK), f32)]
        dxc_pairs = []
        for j in range(SSD_HEADS // 2):
            dyp_f = dy[:, j * LANES:(j + 1) * LANES]
            xcp_f = xc[:, j * LANES:(j + 1) * LANES]
            acc = jnp.zeros((CHUNK, LANES), f32)
            for hh in range(2):
                h = 2 * j + hh
                g = h // 8
                dyp = jnp.where(halves[hh], dyp_f, 0.0).astype(bf16)
                xcp = jnp.where(halves[hh], xcp_f, 0.0).astype(bf16)
                lm = jnp.exp(jnp.where(tri, cs[:, h:h + 1] - cs_t[h:h + 1, :], -1e30))
                lm_t = jnp.exp(jnp.where(tri_t, cs_t[h:h + 1, :] - cs[:, h:h + 1], -1e30))
                dm = _dot_nt(dyp, xcp) * lm
                dm_t = _dot_nt(xcp, dyp) * lm_t
                acc = acc + _dot((gt_mat[g] * lm_t).astype(bf16), dyp)
                dg_acc[g] = dg_acc[g] + dm
                dgt_acc[g] = dgt_acc[g] + dm_t
                qd = jnp.sum(dm * g_mat[g] - dm_t * gt_mat[g], axis=1, keepdims=True)
                dcs = dcs + qd * (lane == h).astype(f32)
            dxc_pairs.append(acc)
        dxc = jnp.concatenate(dxc_pairs, axis=1) + dxc_state
        for g in range(2):
            dcm[g] = dcm[g] + _dot(dg_acc[g].astype(bf16), bgs[g])
            dbm[g] = dbm[g] + _dot(dgt_acc[g].astype(bf16), cgs[g])

        dxs = dxs + dxc * q["dt_e"]
        ddt = _contract(dxc * xs, e)
        dcs = dcs + jnp.where(row_i == CHUNK - 1, jnp.broadcast_to(last_row, (CHUNK, LANES)), 0.0)
        da = jnp.dot(tri_t.astype(f32), dcs, precision=lax.Precision.HIGHEST, preferred_element_type=f32)
        ddt = ddt + da * q["a_neg"]
        dal_ref[...] += jnp.sum(da * dt, axis=0, keepdims=True) * q["a_neg"]
        ddtr = ddt * _sigmoid(dtr_b) * (lane < SSD_HEADS).astype(f32)
        ddtb_ref[...] += jnp.sum(ddtr, axis=0, keepdims=True)
        ddtr_ref[...] = ddtr.astype(bf16)

        dact = jnp.concatenate([dxs, dbm[0], dbm[1], dcm[0], dcm[1]], axis=1)
        dpre_ref[...] = dact * _dsilu(pre)

    rev = lambda n: pl.BlockSpec((CHUNK, n), lambda c: (nc - 1 - c, 0))
    vec = _full((1, LANES))
    vshape = jax.ShapeDtypeStruct((1, LANES), f32)
    return pl.pallas_call(
        body, name="ssd_bwd", grid=(nc,),
        in_specs=[rev(SSD_WIDTH), rev(SSD_WIDTH), rev(SSD_WIDTH), rev(XBC_WIDTH), rev(LANES),
                  pl.BlockSpec((1, SSD_STATE, SSD_WIDTH), lambda c: (nc - 1 - c, 0, 0)),
                  vec, vec, _full((1, SSD_WIDTH)), _full((1, SSD_WIDTH))],
        out_specs=[rev(SSD_WIDTH), rev(XBC_WIDTH), rev(LANES), _full((1, SSD_WIDTH)), vec, vec, vec],
        out_shape=[jax.ShapeDtypeStruct((T, SSD_WIDTH), bf16), jax.ShapeDtypeStruct((T, XBC_WIDTH), f32),
                   jax.ShapeDtypeStruct((T, LANES), bf16), jax.ShapeDtypeStruct((1, SSD_WIDTH), f32),
                   vshape, vshape, vshape],
        scratch_shapes=[pltpu.VMEM((SSD_STATE, SSD_WIDTH), f32)],
        compiler_params=_cparams(("arbitrary",), VMEM_MID),
    )(dys, y, z, pre, dtr, sprev, dtb, alog, dskip_e, gn)


def _in_proj_bwd(dz, dxbc, dcv, dcg, ddt, gin, x, dh1, g):
    T = x.shape[0]
    tm = min(256, T)

    def body(dz_ref, dx_ref, dcv_ref, dcg_ref, ddt_ref, gin_ref, x_ref, dh_ref, g_ref, gx_ref, dg_ref, wt_ref):
        @pl.when(pl.program_id(0) == 0)
        def _():
            dg_ref[...] = jnp.zeros_like(dg_ref)
            _assemble_w_in_t(gin_ref, wt_ref)

        du = (_dot(dz_ref[...], wt_ref[0:O_XBC, :]) + _dot(dx_ref[...], wt_ref[O_XBC:O_DT, :])
              + _dot(dcv_ref[...], wt_ref[O_CV:O_CG, :]) + _dot(dcg_ref[...], wt_ref[O_CG:IN_WIDTH, :])
              + _dot(ddt_ref[...], wt_ref[O_DT:O_DT + LANES, :]))
        dx, dg = _rms_bwd(du, x_ref[...], g_ref[...])
        dg_ref[...] += dg
        gx_ref[...] = dh_ref[...] + dx

    row = lambda n: pl.BlockSpec((tm, n), lambda i: (i, 0))
    return pl.pallas_call(
        body, name="in_proj_bwd", grid=(T // tm,),
        in_specs=[row(SSD_WIDTH), row(XBC_WIDTH), row(CONF_WIDTH), row(CONF_WIDTH), row(LANES), _full(gin.shape),
                  row(D_MODEL), row(D_MODEL), _full((1, D_MODEL))],
        out_specs=[row(D_MODEL), _full((1, D_MODEL))],
        out_shape=[jax.ShapeDtypeStruct((T, D_MODEL), f32), jax.ShapeDtypeStruct((1, D_MODEL), f32)],
        scratch_shapes=[pltpu.VMEM((IN_WIDTH, D_MODEL), bf16)],
        compiler_params=_cparams(("arbitrary",), VMEM_BIG),
    )(dz, dxbc, dcv, dcg, ddt, gin, x, dh1, g)


def _weight_grad(a, g, name, square=False, slab=None, place=None, tk=512):
    T, K = a.shape
    N = g.shape[1]
    tk = min(tk, K)
    tn = 1024 if N % 1024 == 0 else min(512, N)
    tt = min(2048, T)

    def body(a_ref, g_ref, *rest):
        o_ref = rest[-1]
        acc = _dot_tn(_operand(a_ref[...]), g_ref[...].astype(bf16))
        t = pl.program_id(2)
        shaped = acc if slab is None else acc[None]

        @pl.when(t == 0)
        def _():
            o_ref[...] = shaped

        @pl.when(t > 0)
        def _():
            o_ref[...] += shaped

    def _operand(av):
        if square:
            av = av.astype(f32)
            av = av * av
        return av.astype(bf16)

    in_specs = [pl.BlockSpec((tt, tk), lambda i, j, t: (t, i)), pl.BlockSpec((tt, tn), lambda i, j, t: (t, j))]
    grid = (K // tk, N // tn, T // tt)
    params = _cparams(("parallel", "parallel", "arbitrary"), VMEM_MID)
    if slab is None:
        return pl.pallas_call(
            body, name=name, grid=grid, in_specs=in_specs,
            out_specs=pl.BlockSpec((tk, tn), lambda i, j, t: (i, j)),
            out_shape=jax.ShapeDtypeStruct((K, N), f32), compiler_params=params,
        )(a, g)
    return pl.pallas_call(
        body, name=name, grid=grid, in_specs=in_specs + [ANY],
        out_specs=pl.BlockSpec((1, tk, tn), lambda i, j, t: place(i, j)),
        out_shape=jax.ShapeDtypeStruct(slab.shape, f32), input_output_aliases={2: 0}, compiler_params=params,
    )(a, g, slab)


def _place():
    return lax.axis_index("x"), lax.axis_index("y"), lax.axis_index("c")


def _other_chips(x, y):
    return [(1 - x, y), (x, 1 - y), (1 - x, 1 - y)]


def _remote(src, dst, ssem, rsem, dev):
    return pltpu.make_async_remote_copy(src_ref=src, dst_ref=dst, send_sem=ssem, recv_sem=rsem, device_id=dev,
                                        device_id_type=MESH)


def _gather_weights(arrays, convw):
    n = len(arrays)
    halves = tuple(a.shape[1] // 2 for a in arrays)

    def body(*refs):
        cw_ref, cwo_ref = refs[n], refs[2 * n + 1]
        ssem, rsem, lsem = refs[2 * n + 2:]
        triples = tuple(zip(refs[:n], refs[n + 1:2 * n + 1], halves))
        x, y, c = _place()
        me_b = 2 * x + y
        sib = (x, y, 1 - c)
        chips = _other_chips(x, y)
        loc = pltpu.make_async_copy(cw_ref, cwo_ref.at[me_b], lsem)
        loc.start()
        sends = []
        for j, (src, dst, h) in enumerate(triples):
            mine = pl.ds(c * h, h)
            for k, (px, py) in enumerate(chips):
                s = 6 * j + k
                sends.append(_remote(src.at[me_b, mine], dst.at[me_b, mine], ssem.at[s], rsem.at[s], (px, py, c)))
        for k, (px, py) in enumerate(chips):
            sends.append(_remote(cw_ref, cwo_ref.at[me_b], ssem.at[6 * n + k], rsem.at[6 * n + k], (px, py, c)))
        for cp in sends:
            cp.start()
        for j, (src, dst, h) in enumerate(triples):
            mine = pl.ds(c * h, h)
            for k, (px, py) in enumerate(chips):
                b = 2 * px + py
                s = 6 * j + k
                _remote(src.at[b, mine], dst.at[b, mine], ssem.at[s], rsem.at[s], (px, py, c)).wait_recv()
                fw = _remote(dst.at[b, mine], dst.at[b, mine], ssem.at[s + 3], rsem.at[s + 3], sib)
                fw.start()
                sends.append(fw)
        for k, (px, py) in enumerate(chips):
            b = 2 * px + py
            _remote(cw_ref, cwo_ref.at[b], ssem.at[6 * n + k], rsem.at[6 * n + k], (px, py, c)).wait_recv()
        for j, (src, dst, h) in enumerate(triples):
            theirs = pl.ds((1 - c) * h, h)
            for k, (px, py) in enumerate(chips):
                b = 2 * px + py
                s = 6 * j + k + 3
                _remote(src.at[b, theirs], dst.at[b, theirs], ssem.at[s], rsem.at[s], sib).wait_recv()
        for cp in sends:
            cp.wait_send()
        loc.wait()

    return pl.pallas_call(
        body, name="gather_weights", in_specs=[ANY] * (n + 1), out_specs=[ANY] * (n + 1),
        out_shape=[jax.ShapeDtypeStruct(a.shape, bf16) for a in arrays]
        + [jax.ShapeDtypeStruct((N_CHIPS, CONVW_ROWS, D_MODEL), f32)],
        input_output_aliases={j: j for j in range(n)},
        scratch_shapes=[pltpu.SemaphoreType.DMA((6 * n + 3,)), pltpu.SemaphoreType.DMA((6 * n + 3,)),
                        pltpu.SemaphoreType.DMA(())],
    )(*arrays, convw)


def _gather_rider(gath0, lo, n):
    h = gath0.shape[1] // 2

    def copies(rins, routs, ssem, rsem, sending):
        (g_ref,), (o_ref,) = rins, routs
        x, y, c = _place()
        mine = pl.ds(c * h + lo, n)
        for k, (px, py) in enumerate(_other_chips(x, y)):
            b = 2 * x + y if sending else 2 * px + py
            yield _remote(g_ref.at[b, mine], o_ref.at[b, mine], ssem.at[k], rsem.at[k], (px, py, c))

    def start(*refs):
        for cp in copies(*refs, sending=True):
            cp.start()

    def finish(*refs):
        for cp in copies(*refs, sending=False):
            cp.wait()

    return _Rider([gath0], [jax.ShapeDtypeStruct(gath0.shape, gath0.dtype)], {0: 0}, 3, start, finish)


def _forward_to_sibling(gath):
    h = gath.shape[1] // 2

    def body(g_ref, o_ref, ssem, rsem):
        x, y, c = _place()
        sib = (x, y, 1 - c)
        mine, theirs = pl.ds(c * h, h), pl.ds((1 - c) * h, h)
        blocks = [2 * px + py for px, py in _other_chips(x, y)]
        sends = [_remote(g_ref.at[b, mine], o_ref.at[b, mine], ssem.at[k], rsem.at[k], sib) for k, b in enumerate(blocks)]
        for cp in sends:
            cp.start()
        for k, b in enumerate(blocks):
            _remote(g_ref.at[b, theirs], o_ref.at[b, theirs], ssem.at[k], rsem.at[k], sib).wait_recv()
        for cp in sends:
            cp.wait_send()

    return pl.pallas_call(
        body, name="forward_to_sibling", in_specs=[ANY], out_specs=ANY,
        out_shape=jax.ShapeDtypeStruct(gath.shape, gath.dtype), input_output_aliases={0: 0},
        scratch_shapes=[pltpu.SemaphoreType.DMA((3,)), pltpu.SemaphoreType.DMA((3,))],
    )(gath)


def _swap_copy(g_ref, r_ref, ssem, rsem):
    x, y, c = _place()
    h = r_ref.shape[1]
    return _remote(g_ref.at[:, pl.ds((1 - c) * h, h), :], r_ref, ssem.at[0], rsem.at[0], (x, y, 1 - c))


def _swap_rider(g):
    def start(rins, routs, ssem, rsem):
        _swap_copy(rins[0], routs[0], ssem, rsem).start()

    def finish(rins, routs, ssem, rsem):
        _swap_copy(rins[0], routs[0], ssem, rsem).wait()

    return _Rider([g], [jax.ShapeDtypeStruct((N_CHIPS, g.shape[1] // 2, g.shape[2]), g.dtype)], {}, 1, start, finish)


def _swap_halves(g):
    def body(g_ref, r_ref, ssem, rsem):
        cp = _swap_copy(g_ref, r_ref, ssem, rsem)
        cp.start()
        cp.wait()

    return pl.pallas_call(
        body, name="swap_halves", in_specs=[ANY], out_specs=ANY,
        out_shape=jax.ShapeDtypeStruct((N_CHIPS, g.shape[1] // 2, g.shape[2]), g.dtype),
        scratch_shapes=[pltpu.SemaphoreType.DMA((1,)), pltpu.SemaphoreType.DMA((1,))],
    )(g)


def _chip_sum(cidx, gslab, recv, name):
    half, C = recv.shape[1:]
    tr = half // 2 if (half // 2) % 16 == 0 else half

    def body(c_ref, g_ref, r_ref, o_ref):
        o_ref[...] = (g_ref[...] + r_ref[...]).astype(bf16)

    return pl.pallas_call(
        body, name=name,
        grid_spec=pltpu.PrefetchScalarGridSpec(
            num_scalar_prefetch=1, grid=(N_CHIPS, half // tr),
            in_specs=[pl.BlockSpec((1, tr, C), lambda b, i, c_ref: (b, c_ref[0] * (half // tr) + i, 0)),
                      pl.BlockSpec((1, tr, C), lambda b, i, c_ref: (b, i, 0))],
            out_specs=pl.BlockSpec((1, tr, C), lambda b, i, c_ref: (b, i, 0))),
        out_shape=jax.ShapeDtypeStruct((N_CHIPS, half, C), bf16),
        compiler_params=_cparams(("parallel", "parallel"), VMEM_MID),
    )(cidx, gslab, recv)


def _exchange_rider(h):
    def copies(rins, routs, ssem, rsem):
        x, y, c = _place()
        for k, (px, py) in enumerate(_other_chips(x, y)):
            yield _remote(rins[0].at[2 * px + py], routs[0].at[k], ssem.at[k], rsem.at[k], (px, py, c))

    def start(*refs):
        for cp in copies(*refs):
            cp.start()

    def finish(*refs):
        for cp in copies(*refs):
            cp.wait()

    return _Rider([h], [jax.ShapeDtypeStruct((3,) + h.shape[1:], h.dtype)], {}, 3, start, finish)


def _exchange(hb, small):
    def body(hb_ref, sm_ref, rb_ref, all_ref, ssem, rsem, lsem):
        x, y, c = _place()
        me = 4 * x + 2 * y + c
        chips = _other_chips(x, y)
        loc = pltpu.make_async_copy(sm_ref, all_ref.at[me], lsem)
        loc.start()
        sends = []
        for k, (px, py) in enumerate(chips):
            sends.append(_remote(hb_ref.at[2 * px + py], rb_ref.at[k], ssem.at[3 + k], rsem.at[3 + k], (px, py, c)))
        peers = []
        for r in range(1, N_DEV):
            peer = ((1 - x) if r & 4 else x, (1 - y) if r & 2 else y, (1 - c) if r & 1 else c)
            peers.append(peer)
            sends.append(_remote(sm_ref, all_ref.at[me], ssem.at[5 + r], rsem.at[5 + r], peer))
        for cp in sends:
            cp.start()
        for k, (px, py) in enumerate(chips):
            _remote(hb_ref.at[0], rb_ref.at[k], ssem.at[3 + k], rsem.at[3 + k], (px, py, c)).wait_recv()
        for r, peer in zip(range(1, N_DEV), peers):
            pid = 4 * peer[0] + 2 * peer[1] + peer[2]
            _remote(sm_ref, all_ref.at[pid], ssem.at[5 + r], rsem.at[5 + r], peer).wait_recv()
        for cp in sends:
            cp.wait_send()
        loc.wait()

    return pl.pallas_call(
        body, name="exchange", in_specs=[ANY, ANY], out_specs=[ANY, ANY],
        out_shape=[jax.ShapeDtypeStruct((3,) + hb.shape[1:], bf16),
                   jax.ShapeDtypeStruct((N_DEV, SMALL_ROWS, D_MODEL), f32)],
        scratch_shapes=[pltpu.SemaphoreType.DMA((13,)), pltpu.SemaphoreType.DMA((13,)), pltpu.SemaphoreType.DMA(())],
    )(hb, small)


def _final_sum(idx, gslab, recv_sib, recv_ici, name):
    half, C = recv_sib.shape[1:]
    tr = half // 2 if (half // 2) % 16 == 0 else half

    def body(i_ref, g_ref, r_ref, p_ref, o_ref):
        acc = g_ref[0] + r_ref[0]
        for k in range(3):
            acc = acc + p_ref[k].astype(f32)
        o_ref[...] = acc

    return pl.pallas_call(
        body, name=name,
        grid_spec=pltpu.PrefetchScalarGridSpec(
            num_scalar_prefetch=1, grid=(half // tr,),
            in_specs=[pl.BlockSpec((1, tr, C), lambda i, s: (s[1], s[0] * (half // tr) + i, 0)),
                      pl.BlockSpec((1, tr, C), lambda i, s: (s[1], i, 0)),
                      pl.BlockSpec((3, tr, C), lambda i, s: (0, i, 0))],
            out_specs=pl.BlockSpec((tr, C), lambda i, s: (s[0] * (half // tr) + i, 0))),
        out_shape=jax.ShapeDtypeStruct((2 * half, C), f32),
        compiler_params=_cparams(("parallel",), VMEM_MID),
    )(idx, gslab, recv_sib, recv_ici)


def _join_halves(ra, rb):
    ha, hb = ra.shape[0] // 2, rb.shape[0] // 2

    def body(a_ref, b_ref, ao_ref, bo_ref, ssem, rsem):
        x, y, c = _place()
        sib = (x, y, 1 - c)
        mine_a, theirs_a = pl.ds(c * ha, ha), pl.ds((1 - c) * ha, ha)
        mine_b, theirs_b = pl.ds(c * hb, hb), pl.ds((1 - c) * hb, hb)
        ca = _remote(a_ref.at[mine_a], ao_ref.at[mine_a], ssem.at[0], rsem.at[0], sib)
        cb = _remote(b_ref.at[mine_b], bo_ref.at[mine_b], ssem.at[1], rsem.at[1], sib)
        ca.start()
        cb.start()
        _remote(a_ref.at[theirs_a], ao_ref.at[theirs_a], ssem.at[0], rsem.at[0], sib).wait_recv()
        _remote(b_ref.at[theirs_b], bo_ref.at[theirs_b], ssem.at[1], rsem.at[1], sib).wait_recv()
        ca.wait_send()
        cb.wait_send()

    return pl.pallas_call(
        body, name="join_halves", in_specs=[ANY, ANY], out_specs=[ANY, ANY],
        out_shape=[jax.ShapeDtypeStruct(ra.shape, f32), jax.ShapeDtypeStruct(rb.shape, f32)],
        input_output_aliases={0: 0, 1: 1},
        scratch_shapes=[pltpu.SemaphoreType.DMA((2,)), pltpu.SemaphoreType.DMA((2,))],
    )(ra, rb)


def _shard_rows(gt):
    def body(g_ref, o_ref):
        for b in range(N_CHIPS):
            o_ref[b, 0:W_IN_ROWS, :] = g_ref[b * W_IN_ROWS:(b + 1) * W_IN_ROWS, :]
            o_ref[b, W_IN_ROWS:W_IN_ROWS_PAD, :] = jnp.zeros((W_IN_ROWS_PAD - W_IN_ROWS, LANES), f32)

    return pl.pallas_call(
        body, name="shard_rows", grid=(D_MODEL // LANES,),
        in_specs=[pl.BlockSpec((IN_WIDTH, LANES), lambda i: (0, i))],
        out_specs=pl.BlockSpec((N_CHIPS, W_IN_ROWS_PAD, LANES), lambda i: (0, 0, i)),
        out_shape=jax.ShapeDtypeStruct((N_CHIPS, W_IN_ROWS_PAD, D_MODEL), f32),
        compiler_params=_cparams(("parallel",), VMEM_MID),
    )(gt)


def _sum_small(all_small):
    def body(a_ref, o_ref):
        acc = a_ref[0]
        for d in range(1, N_DEV):
            acc = acc + a_ref[d]
        o_ref[...] = acc

    return pl.pallas_call(
        body, name="sum_small", out_shape=jax.ShapeDtypeStruct((SMALL_ROWS, D_MODEL), f32),
    )(all_small)


def _adamw(w, g, m, v, name, g_off=0, by_columns=False):
    R, C = w.shape
    tr = 256 if R % 256 == 0 else R
    assert g_off % tr == 0 and not (by_columns and g_off)
    c1 = 1.0 - ADAM_B1 ** ADAM_STEP
    c2 = 1.0 - ADAM_B2 ** ADAM_STEP

    def body(w_ref, g_ref, m_ref, v_ref, d_ref, mo_ref, vo_ref):
        gg = g_ref[...]
        m2 = ADAM_B1 * m_ref[...] + (1.0 - ADAM_B1) * gg
        v2 = ADAM_B2 * v_ref[...] + (1.0 - ADAM_B2) * (gg * gg)
        mo_ref[...] = m2
        vo_ref[...] = v2
        d_ref[...] = -ADAM_LR * ((m2 / c1) / (jnp.sqrt(v2 / c2) + ADAM_EPS) + ADAM_WD * w_ref[...])

    if by_columns:
        blk = gblk = pl.BlockSpec((R, LANES), lambda i: (0, i))
        grid = (C // LANES,)
    else:
        blk = pl.BlockSpec((tr, C), lambda i: (i, 0))
        gblk = pl.BlockSpec((tr, C), lambda i: (g_off // tr + i, 0))
        grid = (R // tr,)
    shp = jax.ShapeDtypeStruct((R, C), f32)
    return pl.pallas_call(
        body, name=name, grid=grid, in_specs=[blk, gblk, blk, blk], out_specs=[blk] * 3, out_shape=[shp] * 3,
        compiler_params=_cparams(("parallel",), VMEM_MID),
    )(w, g, m, v)


def _pad_lanes(v):
    return jnp.pad(v, ((0, 0), (0, LANES - v.shape[1])))


def _local_step(x, p, tgt, gath0, cidx, gin, S):
    dtb = _pad_lanes(S["dt_bias"])
    alog = _pad_lanes(S["A_log"])
    dskip_e = jnp.repeat(S["D_skip"], HEAD_DIM, axis=1)

    early = GATHER_EARLY_ROWS
    u0, z, xbc, cv, cg, dtr, v, gath1 = _in_proj_fwd(x, S["mix_norm_g"], gin, rider=_gather_rider(gath0, 0, early))
    co, yc, gath = _conf_fwd(v, S["conf_dw_w"], S["conf_dw_b"], S["conf_ln_g"], S["conf_ln_b"],
                             rider=_gather_rider(gath1, early, SLAB_A // 2 - early))
    gath = _forward_to_sibling(gath)
    w_ple = jnp.concatenate([_ple_of_slab(gath[b]) for b in range(N_CHIPS)], axis=1)
    pre = _ssd_conv_fwd(xbc, S["ssd_conv_w"], S["ssd_conv_b"])
    y, ys, sprev = _ssd_fwd(pre, dtr, z, dtb, alog, dskip_e, S["ssd_norm_g"])
    h1, u1 = _out_proj_fwd(x, ys, yc, gath, S["mlp_norm_g"])
    r, h2, u2 = _mlp_fwd(h1, u1, gath, S["ple_gate_norm_g"])
    loss, dh2, dh2b, dgp, dep, dg_fin, dg_ple, db_pg, dg_pg = _ple_loss(
        h2, u2, p, tgt, gath, S["b_ple_gate"], w_ple, S["ple_norm_g"], S["final_norm_g"], S["ple_gate_norm_g"])

    npg = D_MODEL // N_CHIPS
    ga = lax.empty((N_CHIPS, SLAB_A, D_MODEL), f32)
    ga = _weight_grad(u2, dgp, "dw_ple_gate", slab=ga, tk=npg, place=lambda i, j: (i, PG_OFF // npg, j))
    ga = _weight_grad(r, dh2b, "dw_down", square=True, slab=ga, place=lambda i, j: (i // 2, DOWN_OFF // 512 + i % 2, j))
    gw_ple = _weight_grad(p, dep, "dw_ple")
    dhp, dh1, dh1b, dg_mlp = _mlp_bwd(dh2, r, gath, h1, S["mlp_norm_g"])
    ga = _weight_grad(u1, dhp, "dw_up", slab=ga, place=lambda i, j: (j, UP_OFF // 512 + i, 0))
    ga = _weight_grad(ys, dh1b, "dw_out_ssd", slab=ga, place=lambda i, j: (i, OUT_OFF // 512, j))
    ga = _weight_grad(yc, dh1b, "dw_out_conf", slab=ga, place=lambda i, j: (2 + i, OUT_OFF // 512, j))
    n_ple = D_MODEL // N_CHIPS
    ple_rows = jnp.stack([_rows(gw_ple[:, b * n_ple:(b + 1) * n_ple]) for b in range(N_CHIPS)], axis=0)
    ga = lax.dynamic_update_slice(ga, ple_rows, (0, PLE_OFF, 0))
    dys, dco, dg_ln, db_ln, recv_a = _out_proj_bwd(dh1, gath, co, S["conf_ln_g"], S["conf_ln_b"], rider=_swap_rider(ga))
    ha = _chip_sum(cidx, ga, recv_a, "chip_sum_a")
    dcv, dcg, dw_conf, db_conf, ici_a = _conf_conv_bwd(dco, v, S["conf_dw_w"], cv, cg, rider=_exchange_rider(ha))
    dz, dpre, ddtr, dg_ssdn, dd, dal, ddtb = _ssd_bwd(dys, y, z, pre, dtr, sprev, dtb, alog, dskip_e, S["ssd_norm_g"])
    dxbc, dw_sconv, db_sconv = _ssd_conv_bwd(dpre, xbc, S["ssd_conv_w"])
    gx, dg_mix = _in_proj_bwd(dz, dxbc, dcv, dcg, ddtr, gin, x, dh1, S["mix_norm_g"])

    gw_in = jnp.concatenate([
        _weight_grad(dz, u0, "dw_in_z"), _weight_grad(dxbc, u0, "dw_in_xbc"),
        _weight_grad(ddtr, u0, "dw_in_dt")[:SSD_HEADS],
        _weight_grad(dcv, u0, "dw_in_cv"), _weight_grad(dcg, u0, "dw_in_cg")], axis=0)
    small = {
        "mix_norm_g": dg_mix, "ssd_conv_w": dw_sconv, "ssd_conv_b": db_sconv, "dt_bias": ddtb, "A_log": dal, "D_skip": dd,
        "ssd_norm_g": dg_ssdn, "conf_dw_w": dw_conf, "conf_dw_b": db_conf, "conf_ln_g": dg_ln, "conf_ln_b": db_ln,
        "mlp_norm_g": dg_mlp, "ple_gate_norm_g": dg_pg, "b_ple_gate": db_pg, "ple_norm_g": dg_ple,
        "final_norm_g": dg_fin, "loss": loss,
    }
    return gx, ga, recv_a, ici_a, gw_in, small


def _rows(a):
    return a.reshape(-1, D_MODEL)


def _pad_rows(a, n):
    flat = a.reshape(-1)
    return jnp.pad(flat, (0, n * D_MODEL - flat.shape[0])).reshape(n, D_MODEL)


def _ple_of_slab(slab):
    return slab[PLE_OFF:PLE_OFF + PLE_ROWS].reshape(PLE_DIM, D_MODEL // N_CHIPS)


ROW_VEC = {"mix_norm_g": 0, "ssd_norm_g": 1, "conf_dw_b": 2, "conf_ln_g": 3, "conf_ln_b": 4, "mlp_norm_g": 5,
           "ple_gate_norm_g": 6, "b_ple_gate": 7, "ple_norm_g": 8, "final_norm_g": 9}
ROW_CONV_B = 10
ROW_HEADS = 12
ROW_CONV_W = 16
ROW_DW = 24
HEAD_LANES = {"dt_bias": 0, "A_log": 1, "D_skip": 2, "loss": 3}
SMALL_ORDER = ("mix_norm_g", "ssd_conv_w", "ssd_conv_b", "dt_bias", "A_log", "D_skip", "ssd_norm_g", "conf_dw_w",
               "conf_dw_b", "conf_ln_g", "conf_ln_b", "mlp_norm_g", "ple_gate_norm_g", "b_ple_gate", "ple_norm_g",
               "final_norm_g")
SPLIT = XBC_WIDTH - D_MODEL


def _pack_small(raw):
    names = list(ROW_VEC) + ["ssd_conv_b", "dt_bias", "A_log", "D_skip", "loss", "ssd_conv_w", "conf_dw_w"]

    def body(*refs):
        r = dict(zip(names, refs[:-1]))
        o_ref = refs[-1]
        o_ref[...] = jnp.zeros_like(o_ref)
        for n, row in ROW_VEC.items():
            o_ref[row:row + 1, :] = r[n][...]
        o_ref[ROW_CONV_B:ROW_CONV_B + 1, :] = r["ssd_conv_b"][:, 0:D_MODEL]
        o_ref[ROW_CONV_B + 1:ROW_CONV_B + 2, 0:SPLIT] = r["ssd_conv_b"][:, D_MODEL:]
        for n, j in HEAD_LANES.items():
            o_ref[ROW_HEADS:ROW_HEADS + 1, j * LANES:(j + 1) * LANES] = r[n][0:1, :]
        for k in range(SSD_CONV):
            o_ref[ROW_CONV_W + 2 * k:ROW_CONV_W + 2 * k + 1, :] = r["ssd_conv_w"][k:k + 1, 0:D_MODEL]
            o_ref[ROW_CONV_W + 2 * k + 1:ROW_CONV_W + 2 * k + 2, 0:SPLIT] = r["ssd_conv_w"][k:k + 1, D_MODEL:]
        o_ref[ROW_DW:ROW_DW + 32, :] = r["conf_dw_w"][...]

    return pl.pallas_call(
        body, name="pack_small", out_shape=jax.ShapeDtypeStruct((SMALL_ROWS, D_MODEL), f32),
    )(*[raw[n] for n in names])


def _adamw_small(cidx, tot, w, m, v):
    c1 = 1.0 - ADAM_B1 ** ADAM_STEP
    c2 = 1.0 - ADAM_B2 ** ADAM_STEP
    n_par = len(SMALL_ORDER)

    def shard(full, chip, width):
        out = full[:, 0:width]
        for b in range(1, N_CHIPS):
            out = jnp.where(chip == b, full[:, b * width:(b + 1) * width], out)
        return out

    def grad_of(n, t_ref, chip):
        if n in ROW_VEC:
            return t_ref[ROW_VEC[n]:ROW_VEC[n] + 1, :]
        if n == "ssd_conv_b":
            return jnp.concatenate([t_ref[ROW_CONV_B:ROW_CONV_B + 1, :], t_ref[ROW_CONV_B + 1:ROW_CONV_B + 2, 0:SPLIT]], axis=1)
        if n in HEAD_LANES:
            j = HEAD_LANES[n]
            return t_ref[ROW_HEADS:ROW_HEADS + 1, j * LANES:j * LANES + SSD_HEADS]
        if n == "ssd_conv_w":
            rows = [jnp.concatenate([t_ref[ROW_CONV_W + 2 * k:ROW_CONV_W + 2 * k + 1, :],
                                     t_ref[ROW_CONV_W + 2 * k + 1:ROW_CONV_W + 2 * k + 2, 0:SPLIT]], axis=1)
                    for k in range(SSD_CONV)]
            return shard(jnp.concatenate(rows, axis=0), chip, XBC_WIDTH // N_CHIPS)
        return shard(t_ref[ROW_DW:ROW_DW + CONF_KERNEL, :], chip, CONF_WIDTH // N_CHIPS)

    def body(c_ref, t_ref, *refs):
        ins, outs = refs[:3 * n_par], refs[3 * n_par:]
        chip = c_ref[1]
        for i, n in enumerate(SMALL_ORDER):
            w_ref, m_ref, v_ref = ins[3 * i:3 * i + 3]
            g_ref, d_ref, mo_ref, vo_ref = outs[4 * i:4 * i + 4]
            g = grad_of(n, t_ref, chip)
            m2 = ADAM_B1 * m_ref[...] + (1.0 - ADAM_B1) * g
            v2 = ADAM_B2 * v_ref[...] + (1.0 - ADAM_B2) * (g * g)
            g_ref[...] = g
            mo_ref[...] = m2
            vo_ref[...] = v2
            d_ref[...] = -ADAM_LR * ((m2 / c1) / (jnp.sqrt(v2 / c2) + ADAM_EPS) + ADAM_WD * w_ref[...])

    args, in_specs, out_specs, out_shape = [], [], [], []
    for n in SMALL_ORDER:
        shp = w[n].shape
        spec = pl.BlockSpec(shp, lambda i, c_ref: (0, 0))
        args += [w[n], m[n], v[n]]
        in_specs += [spec] * 3
        out_specs += [spec] * 4
        out_shape += [jax.ShapeDtypeStruct(shp, f32)] * 4
    outs = pl.pallas_call(
        body, name="adamw_small",
        grid_spec=pltpu.PrefetchScalarGridSpec(
            num_scalar_prefetch=1, grid=(1,),
            in_specs=[pl.BlockSpec(tot.shape, lambda i, c_ref: (0, 0))] + in_specs, out_specs=out_specs),
        out_shape=out_shape,
    )(cidx, tot, *args)
    grad, delta, new_m, new_v = {}, {}, {}, {}
    for i, n in enumerate(SMALL_ORDER):
        grad[n], delta[n], new_m[n], new_v[n] = outs[4 * i:4 * i + 4]
    return grad, delta, new_m, new_v


BIG = ("w_in", "w_out", "w_up", "w_down", "w_ple_gate", "w_ple")
BIG_A = (("w_up", UP_OFF), ("w_down", DOWN_OFF), ("w_out", OUT_OFF), ("w_ple_gate", PG_OFF))
WEIGHTS = ("mix_norm_g", "w_in", "ssd_conv_w", "ssd_conv_b", "dt_bias", "A_log", "D_skip", "ssd_norm_g", "conf_dw_w",
           "conf_dw_b", "conf_ln_g", "conf_ln_b", "w_out", "mlp_norm_g", "w_up", "w_down", "ple_gate_norm_g",
           "w_ple_gate", "b_ple_gate", "w_ple", "ple_norm_g", "final_norm_g")


def kernel(x, p, mix_norm_g, w_in, ssd_conv_w, ssd_conv_b, dt_bias, A_log, D_skip, ssd_norm_g, conf_dw_w, conf_dw_b, conf_ln_g, conf_ln_b, w_out, mlp_norm_g, w_up, w_down, ple_gate_norm_g, w_ple_gate, b_ple_gate, w_ple, ple_norm_g, final_norm_g, loss_target, m_mix_norm_g, m_w_in, m_ssd_conv_w, m_ssd_conv_b, m_dt_bias, m_A_log, m_D_skip, m_ssd_norm_g, m_conf_dw_w, m_conf_dw_b, m_conf_ln_g, m_conf_ln_b, m_w_out, m_mlp_norm_g, m_w_up, m_w_down, m_ple_gate_norm_g, m_w_ple_gate, m_b_ple_gate, m_w_ple, m_ple_norm_g, m_final_norm_g, v_mix_norm_g, v_w_in, v_ssd_conv_w, v_ssd_conv_b, v_dt_bias, v_A_log, v_D_skip, v_ssd_norm_g, v_conf_dw_w, v_conf_dw_b, v_conf_ln_g, v_conf_ln_b, v_w_out, v_mlp_norm_g, v_w_up, v_w_down, v_ple_gate_norm_g, v_w_ple_gate, v_b_ple_gate, v_w_ple, v_ple_norm_g, v_final_norm_g):
    w = dict(mix_norm_g=mix_norm_g, w_in=w_in, ssd_conv_w=ssd_conv_w, ssd_conv_b=ssd_conv_b, dt_bias=dt_bias, A_log=A_log,
             D_skip=D_skip, ssd_norm_g=ssd_norm_g, conf_dw_w=conf_dw_w, conf_dw_b=conf_dw_b, conf_ln_g=conf_ln_g,
             conf_ln_b=conf_ln_b, w_out=w_out, mlp_norm_g=mlp_norm_g, w_up=w_up, w_down=w_down,
             ple_gate_norm_g=ple_gate_norm_g, w_ple_gate=w_ple_gate, b_ple_gate=b_ple_gate, w_ple=w_ple,
             ple_norm_g=ple_norm_g, final_norm_g=final_norm_g)
    m = dict(mix_norm_g=m_mix_norm_g, w_in=m_w_in, ssd_conv_w=m_ssd_conv_w, ssd_conv_b=m_ssd_conv_b, dt_bias=m_dt_bias,
             A_log=m_A_log, D_skip=m_D_skip, ssd_norm_g=m_ssd_norm_g, conf_dw_w=m_conf_dw_w, conf_dw_b=m_conf_dw_b,
             conf_ln_g=m_conf_ln_g, conf_ln_b=m_conf_ln_b, w_out=m_w_out, mlp_norm_g=m_mlp_norm_g, w_up=m_w_up,
             w_down=m_w_down, ple_gate_norm_g=m_ple_gate_norm_g, w_ple_gate=m_w_ple_gate, b_ple_gate=m_b_ple_gate,
             w_ple=m_w_ple, ple_norm_g=m_ple_norm_g, final_norm_g=m_final_norm_g)
    v = dict(mix_norm_g=v_mix_norm_g, w_in=v_w_in, ssd_conv_w=v_ssd_conv_w, ssd_conv_b=v_ssd_conv_b, dt_bias=v_dt_bias,
             A_log=v_A_log, D_skip=v_D_skip, ssd_norm_g=v_ssd_norm_g, conf_dw_w=v_conf_dw_w, conf_dw_b=v_conf_dw_b,
             conf_ln_g=v_conf_ln_g, conf_ln_b=v_conf_ln_b, w_out=v_w_out, mlp_norm_g=v_mlp_norm_g, w_up=v_w_up,
             w_down=v_w_down, ple_gate_norm_g=v_ple_gate_norm_g, w_ple_gate=v_w_ple_gate, b_ple_gate=v_b_ple_gate,
             w_ple=v_w_ple, ple_norm_g=v_ple_norm_g, final_norm_g=v_final_norm_g)
    xi, yi, ci = lax.axis_index("x"), lax.axis_index("y"), lax.axis_index("c")
    chip = 2 * xi + yi

    slab = jnp.concatenate([w_up[0], w_down[0], w_out[0], w_ple_gate[0], _rows(w_ple[0])], axis=0).astype(bf16)
    gath0 = lax.dynamic_update_slice(jnp.zeros((N_CHIPS, SLAB_A, D_MODEL), bf16), slab[None], (chip, 0, 0))
    wt_shard = jnp.swapaxes(w_in, 1, 2).astype(bf16)
    gin0 = lax.dynamic_update_slice(jnp.zeros((N_CHIPS, W_IN_ROWS_PAD, D_MODEL), bf16), wt_shard, (chip, 0, 0))
    convw = _pad_rows(jnp.concatenate([ssd_conv_w[0].reshape(-1), conf_dw_w[0].reshape(-1)]), CONVW_ROWS)
    gin, cwg = _gather_weights([gin0], convw)
    n_sc = SSD_CONV * (XBC_WIDTH // N_CHIPS)
    n_cf = CONF_KERNEL * (CONF_WIDTH // N_CHIPS)
    S = {n: w[n][0] for n in ("mix_norm_g", "ssd_conv_b", "dt_bias", "A_log", "D_skip", "ssd_norm_g", "conf_dw_b",
                              "conf_ln_g", "conf_ln_b", "mlp_norm_g", "ple_gate_norm_g", "b_ple_gate", "ple_norm_g")}
    S = {n: a.reshape(1, -1) for n, a in S.items()}
    S["final_norm_g"] = final_norm_g.reshape(1, -1)
    S["ssd_conv_w"] = jnp.concatenate(
        [cwg[b].reshape(-1)[:n_sc].reshape(SSD_CONV, XBC_WIDTH // N_CHIPS) for b in range(N_CHIPS)], axis=1)
    S["conf_dw_w"] = jnp.concatenate(
        [cwg[b].reshape(-1)[n_sc:n_sc + n_cf].reshape(CONF_KERNEL, CONF_WIDTH // N_CHIPS) for b in range(N_CHIPS)], axis=1)

    cidx = jnp.stack([ci, chip]).astype(jnp.int32)
    grad_x, ga, recv_a, ici_a, gw_in, gsmall = _local_step(x[0], p[0, 0], loss_target[0], gath0, cidx, gin, S)

    gb = _shard_rows(gw_in)
    small = _pack_small(gsmall)
    recv_b = _swap_halves(gb)
    hb = _chip_sum(cidx, gb, recv_b, "chip_sum_b")
    ici_b, all_small = _exchange(hb, small)
    ra = _final_sum(cidx, ga, recv_a, ici_a, "final_sum_a")
    rb = _final_sum(cidx, gb, recv_b, ici_b, "final_sum_b")
    ra, rb = _join_halves(ra, rb)
    tot_small = _sum_small(all_small)

    loss = tot_small[ROW_HEADS, HEAD_LANES["loss"] * LANES]

    two_d = lambda a: a.reshape(a.shape[-2:]) if a.ndim > 1 else a.reshape(1, -1)
    small_w, small_m, small_v = ({n: two_d(d[n]) for n in SMALL_ORDER} for d in (w, m, v))
    grads, delta, new_m, new_v = _adamw_small(cidx, tot_small, small_w, small_m, small_v)
    g_in_t = rb[:W_IN_ROWS]
    grads["w_ple"] = _ple_of_slab(ra)
    grads["w_in"] = jnp.swapaxes(g_in_t, 0, 1)
    for n, off in BIG_A:
        grads[n] = ra[off:off + w[n].shape[1]]
        delta[n], new_m[n], new_v[n] = _adamw(w[n][0], ra, m[n][0], v[n][0], "adamw_" + n, g_off=off)
    delta["w_ple"], new_m["w_ple"], new_v["w_ple"] = _adamw(w_ple[0], grads["w_ple"], m_w_ple[0], v_w_ple[0], "adamw_w_ple")
    tr_ = lambda a: jnp.swapaxes(a[0], 0, 1)
    d_, m_, v_ = _adamw(tr_(w_in), g_in_t, tr_(m_w_in), tr_(v_w_in), "adamw_w_in", by_columns=True)
    delta["w_in"], new_m["w_in"], new_v["w_in"] = (jnp.swapaxes(a, 0, 1) for a in (d_, m_, v_))

    shaped = lambda d: [d[n].reshape(w[n].shape) for n in WEIGHTS]
    return (loss, grad_x[None], *shaped(grads), *shaped(delta), *shaped(new_m), *shaped(new_v))
```

```python
import jax
import jax.numpy as jnp
from jax import lax
from jax.experimental import pallas as pl
from jax.experimental.pallas import tpu as pltpu

f32 = jnp.float32
bf16 = jnp.bfloat16

D_MODEL = 1024
SSD_WIDTH = 1024
SSD_HEADS = 16
HEAD_DIM = 64
SSD_STATE = 128
XBC_WIDTH = 1536
SSD_CONV = 4
CHUNK = 128
CONF_WIDTH = 1024
CONF_KERNEL = 31
D_FF = 4096
PLE_DIM = 256
IN_WIDTH = 4624
EPS = 1e-6
N_CHIPS = 4
N_DEV = 8

ADAM_LR = 0.001
ADAM_B1 = 0.9
ADAM_B2 = 0.999
ADAM_EPS = 1e-08
ADAM_WD = 0.01
ADAM_STEP = 10

LANES = 128
VMEM_BIG = 56 * 1024 * 1024
VMEM_MID = 40 * 1024 * 1024

UP_OFF, DOWN_OFF, OUT_OFF, PG_OFF, PLE_OFF = 0, 1024, 2048, 2560, 2816
PLE_ROWS = 64
SLAB_A = PLE_OFF + PLE_ROWS
GATHER_EARLY_ROWS = 480
W_IN_ROWS = 1156
W_IN_ROWS_PAD = 1184
CONVW_ROWS = 16
SMALL_ROWS = 56

MESH = pl.DeviceIdType.MESH
ANY = pl.BlockSpec(memory_space=pl.ANY)


def _pallas(body, **kw):
    call = pl.pallas_call(body, **kw)

    def run(*args):
        return call(*[a if a.dtype == jnp.int32 else pltpu.with_memory_space_constraint(a, pltpu.HBM) for a in args])

    return run


def _cparams(sem=None, vmem=None):
    return pltpu.CompilerParams(dimension_semantics=sem, vmem_limit_bytes=vmem)


def _full(shape):
    n = len(shape)
    return pl.BlockSpec(shape, lambda *_: (0,) * n)


class _Rider:
    def __init__(self, inputs, out_shapes, aliases, n_sems, start, finish):
        self.inputs, self.out_shapes, self.aliases = list(inputs), list(out_shapes), dict(aliases)
        self.n_sems, self.start, self.finish = n_sems, start, finish


def _call(body, args, *, name, grid, in_specs, out_specs, out_shape, scratch_shapes=(), params=None, rider=None):
    if rider is None:
        return _pallas(body, name=name, grid=grid, in_specs=in_specs, out_specs=out_specs, out_shape=out_shape,
                              scratch_shapes=list(scratch_shapes), compiler_params=params)(*args)
    ni, no, ns = len(in_specs), len(out_specs), len(scratch_shapes)
    ri, ro = len(rider.inputs), len(rider.out_shapes)
    (steps,) = grid

    def with_rider(*refs):
        ins, refs = refs[:ni], refs[ni:]
        rins, refs = refs[:ri], refs[ri:]
        outs, refs = refs[:no], refs[no:]
        routs, refs = refs[:ro], refs[ro:]
        scratch, (ssem, rsem) = refs[:ns], refs[ns:]
        step = pl.program_id(0)

        @pl.when(step == 0)
        def _():
            rider.start(rins, routs, ssem, rsem)

        body(*ins, *outs, *scratch)

        @pl.when(step == steps - 1)
        def _():
            rider.finish(rins, routs, ssem, rsem)

    sems = [pltpu.SemaphoreType.DMA((rider.n_sems,)), pltpu.SemaphoreType.DMA((rider.n_sems,))]
    return _pallas(
        with_rider, name=name, grid=grid, in_specs=list(in_specs) + [ANY] * ri, out_specs=list(out_specs) + [ANY] * ro,
        out_shape=list(out_shape) + rider.out_shapes, scratch_shapes=list(scratch_shapes) + sems,
        input_output_aliases={ni + a: no + b for a, b in rider.aliases.items()}, compiler_params=params,
    )(*args, *rider.inputs)


def _dot(a, b):
    return jnp.dot(a, b, preferred_element_type=f32)


def _dot_nt(a, b):
    return lax.dot_general(a, b, (((1,), (1,)), ((), ())), preferred_element_type=f32)


def _dot_tn(a, b):
    return lax.dot_general(a, b, (((0,), (0,)), ((), ())), preferred_element_type=f32)


def _sigmoid(x):
    return jax.nn.sigmoid(x)


def _rms(x, g):
    r = lax.rsqrt(jnp.mean(x * x, axis=-1, keepdims=True) + EPS)
    return x * r * g


def _rms_bwd(dy, x, g):
    r = lax.rsqrt(jnp.mean(x * x, axis=-1, keepdims=True) + EPS)
    xh = x * r
    dg = jnp.sum(dy * xh, axis=0, keepdims=True)
    dxh = dy * g
    dx = r * (dxh - xh * jnp.mean(dxh * xh, axis=-1, keepdims=True))
    return dx, dg


def _dsilu(x):
    s = _sigmoid(x)
    return s * (1.0 + x * (1.0 - s))


def _split3(x):
    hi = x.astype(bf16)
    r1 = x - hi.astype(f32)
    mid = r1.astype(bf16)
    lo = (r1 - mid.astype(f32)).astype(bf16)
    return hi, mid, lo


def _head_matrix():
    row = lax.broadcasted_iota(jnp.int32, (LANES, SSD_WIDTH), 0)
    col = lax.broadcasted_iota(jnp.int32, (LANES, SSD_WIDTH), 1)
    lo = row * HEAD_DIM
    return ((col >= lo) & (col < lo + HEAD_DIM)).astype(bf16)


def _expand(x, e):
    hi, mid, lo = _split3(x)
    return _dot(hi, e) + _dot(mid, e) + _dot(lo, e)


def _contract(x, e):
    hi, mid, lo = _split3(x)
    return _dot_nt(hi, e) + _dot_nt(mid, e) + _dot_nt(lo, e)


O_XBC = SSD_WIDTH
O_DT = O_XBC + XBC_WIDTH
O_CV = O_DT + SSD_HEADS
O_CG = O_CV + CONF_WIDTH


def _assemble_w_in_t(gin_ref, wt_ref):
    for b in range(N_CHIPS):
        wt_ref[b * W_IN_ROWS:(b + 1) * W_IN_ROWS, :] = gin_ref[b, 0:W_IN_ROWS, :]


def _in_proj_fwd(x, g, gin, rider=None):
    T = x.shape[0]
    tm = min(256, T)

    def body(x_ref, g_ref, gin_ref, u_ref, z_ref, xbc_ref, cv_ref, cg_ref, dt_ref, v_ref, wt_ref):
        @pl.when(pl.program_id(0) == 0)
        def _():
            _assemble_w_in_t(gin_ref, wt_ref)

        ub = _rms(x_ref[...], g_ref[...]).astype(bf16)
        u_ref[...] = ub
        z_ref[...] = _dot_nt(ub, wt_ref[0:O_XBC, :])
        xbc_ref[...] = _dot_nt(ub, wt_ref[O_XBC:O_DT, :])
        cv = _dot_nt(ub, wt_ref[O_CV:O_CG, :])
        cg = _dot_nt(ub, wt_ref[O_CG:IN_WIDTH, :])
        cv_ref[...] = cv
        cg_ref[...] = cg
        v_ref[...] = cv * _sigmoid(cg)
        dt_ref[...] = _dot_nt(ub, wt_ref[O_DT:O_DT + LANES, :])

    row = lambda n: pl.BlockSpec((tm, n), lambda i: (i, 0))
    return _call(
        body, (x, g, gin), name="in_proj_fwd", grid=(T // tm,),
        in_specs=[row(D_MODEL), _full((1, D_MODEL)), _full(gin.shape)],
        out_specs=[row(D_MODEL), row(SSD_WIDTH), row(XBC_WIDTH), row(CONF_WIDTH), row(CONF_WIDTH), row(LANES),
                   row(CONF_WIDTH)],
        out_shape=[jax.ShapeDtypeStruct((T, D_MODEL), bf16), jax.ShapeDtypeStruct((T, SSD_WIDTH), f32),
                   jax.ShapeDtypeStruct((T, XBC_WIDTH), f32), jax.ShapeDtypeStruct((T, CONF_WIDTH), f32),
                   jax.ShapeDtypeStruct((T, CONF_WIDTH), f32), jax.ShapeDtypeStruct((T, LANES), f32),
                   jax.ShapeDtypeStruct((T, CONF_WIDTH), f32)],
        scratch_shapes=[pltpu.VMEM((IN_WIDTH, D_MODEL), bf16)],
        params=_cparams(("arbitrary",), VMEM_BIG), rider=rider)


SUBLANES = 8


def _phases(offsets):
    return sorted({o % SUBLANES for o in offsets} - {0})


def _phase_shape(offsets, tm, C):
    a_max = max([o // SUBLANES for o in offsets if o % SUBLANES] or [0])
    return (max(len(_phases(offsets)), 1), tm + SUBLANES * a_max, C)


def _make_phases(buf_ref, ph_ref, offsets, tm):
    for idx, b in enumerate(_phases(offsets)):
        n = tm + SUBLANES * max(o // SUBLANES for o in offsets if o % SUBLANES == b)
        ph_ref[idx, 0:n, :] = buf_ref[pl.ds(b, n), :]


def _window(buf_ref, ph_ref, offsets, o, r0, rb):
    a, b = divmod(o, SUBLANES)
    if b == 0:
        return buf_ref[pl.ds(r0 + SUBLANES * a, rb), :]
    return ph_ref[_phases(offsets).index(b), pl.ds(r0 + SUBLANES * a, rb), :]


def _conv_rows(wb_ref, buf_ref, ph_ref, offsets, r0, rb):
    nsub = rb // SUBLANES
    accs = [None] * nsub
    for k, o in enumerate(offsets):
        wk = wb_ref[pl.ds(SUBLANES * k, SUBLANES), :]
        for s in range(nsub):
            term = wk * _window(buf_ref, ph_ref, offsets, o, r0 + SUBLANES * s, SUBLANES)
            accs[s] = term if accs[s] is None else accs[s] + term
    return accs[0] if nsub == 1 else jnp.concatenate(accs, axis=0)


def _sublane_rows(w):
    return jnp.repeat(w, SUBLANES, axis=0)


def _fwd_offsets(K, hb):
    return [hb - (K - 1) + k for k in range(K)]


def _prev_halo_spec(hb, tm, C):
    return pl.BlockSpec((hb, C), lambda i: (jnp.maximum(i * (tm // hb) - 1, 0), 0))


CONV_RB = 16


def _ssd_conv_fwd(xbc, w, b):
    T, C = xbc.shape
    K, hb = SSD_CONV, 8
    tm = min(256, T)
    offs = _fwd_offsets(K, hb)

    def body(cur_ref, halo_ref, w_ref, b_ref, pre_ref, buf_ref, ph_ref):
        keep = jnp.where(pl.program_id(0) > 0, 1.0, 0.0)
        buf_ref[0:hb, :] = halo_ref[...] * keep
        buf_ref[hb:hb + tm, :] = cur_ref[...]
        _make_phases(buf_ref, ph_ref, offs, tm)

        def chunk(i, carry):
            r0 = pl.multiple_of(i * CONV_RB, CONV_RB)
            pre_ref[pl.ds(r0, CONV_RB), :] = _conv_rows(w_ref, buf_ref, ph_ref, offs, r0, CONV_RB) + b_ref[...]
            return carry

        lax.fori_loop(0, tm // CONV_RB, chunk, 0)

    return _pallas(
        body, name="ssd_conv_fwd", grid=(T // tm,),
        in_specs=[pl.BlockSpec((tm, C), lambda i: (i, 0)), _prev_halo_spec(hb, tm, C), _full((SUBLANES * K, C)),
                  _full((1, C))],
        out_specs=pl.BlockSpec((tm, C), lambda i: (i, 0)),
        out_shape=jax.ShapeDtypeStruct((T, C), f32),
        scratch_shapes=[pltpu.VMEM((hb + tm, C), f32), pltpu.VMEM(_phase_shape(offs, tm, C), f32)],
        compiler_params=_cparams(("parallel",), VMEM_MID),
    )(xbc, xbc, _sublane_rows(w), b)


def _conf_fwd(v, w, b, ln_g, ln_b, rider=None):
    T, C = v.shape
    K, hb = CONF_KERNEL, 32
    tm = min(256, T)
    offs = _fwd_offsets(K, hb)
    rb = 2 * CONV_RB

    def body(cur_ref, halo_ref, w_ref, b_ref, g_ref, bb_ref, co_ref, y_ref, buf_ref, ph_ref):
        keep = jnp.where(pl.program_id(0) > 0, 1.0, 0.0)
        buf_ref[0:hb, :] = halo_ref[...] * keep
        buf_ref[hb:hb + tm, :] = cur_ref[...]
        _make_phases(buf_ref, ph_ref, offs, tm)

        def chunk(i, carry):
            r0 = pl.multiple_of(i * rb, rb)
            co = _conv_rows(w_ref, buf_ref, ph_ref, offs, r0, rb) + b_ref[...]
            co_ref[pl.ds(r0, rb), :] = co
            mu = jnp.mean(co, axis=-1, keepdims=True)
            xc = co - mu
            yn = xc * lax.rsqrt(jnp.mean(xc * xc, axis=-1, keepdims=True) + EPS) * g_ref[...] + bb_ref[...]
            y_ref[pl.ds(r0, rb), :] = (yn * _sigmoid(yn)).astype(bf16)
            return carry

        lax.fori_loop(0, tm // rb, chunk, 0)

    return _call(
        body, (v, v, _sublane_rows(w), b, ln_g, ln_b), name="conf_fwd", grid=(T // tm,),
        in_specs=[pl.BlockSpec((tm, C), lambda i: (i, 0)), _prev_halo_spec(hb, tm, C), _full((SUBLANES * K, C)),
                  _full((1, C)), _full((1, C)), _full((1, C))],
        out_specs=[pl.BlockSpec((tm, C), lambda i: (i, 0)), pl.BlockSpec((tm, C), lambda i: (i, 0))],
        out_shape=[jax.ShapeDtypeStruct((T, C), f32), jax.ShapeDtypeStruct((T, C), bf16)],
        scratch_shapes=[pltpu.VMEM((hb + tm, C), f32), pltpu.VMEM(_phase_shape(offs, tm, C), f32)],
        params=_cparams(("arbitrary",), VMEM_MID), rider=rider)


def _ssd_chunk_common(pre, dtr, dtb, alog, e):
    act = pre * _sigmoid(pre)
    xs = act[:, :SSD_WIDTH]
    bm = act[:, SSD_WIDTH:SSD_WIDTH + 2 * SSD_STATE]
    cm = act[:, SSD_WIDTH + 2 * SSD_STATE:]
    row = lax.broadcasted_iota(jnp.int32, (CHUNK, CHUNK), 0)
    col = lax.broadcasted_iota(jnp.int32, (CHUNK, CHUNK), 1)
    tri = row >= col
    dt = jax.nn.softplus(dtr + dtb)
    a_neg = -jnp.exp(alog)
    a = dt * a_neg
    cs = jnp.dot(tri.astype(f32), a, precision=lax.Precision.HIGHEST, preferred_element_type=f32)
    cs_e = _expand(cs, e)
    dt_e = _expand(dt, e)
    csl_e = cs_e[CHUNK - 1:CHUNK, :]
    ecs_e = jnp.exp(cs_e)
    dte_e = jnp.exp(csl_e - cs_e)
    cd_e = jnp.exp(csl_e)
    xc = xs * dt_e
    xd = xc * dte_e
    return dict(xs=xs, bm=bm, cm=cm, tri=tri, dt=dt, a_neg=a_neg, cs=cs, ecs_e=ecs_e, dte_e=dte_e, cd_e=cd_e,
                dt_e=dt_e, xc=xc, xd=xd)


def _group(v, g, width):
    return v[:, g * width:(g + 1) * width]


def _ssd_fwd(pre, dtr, z, dtb, alog, dskip_e, gn):
    T = pre.shape[0]
    nc = T // CHUNK
    GW = SSD_WIDTH // 2

    def body(pre_ref, dtr_ref, z_ref, dtb_ref, alog_ref, de_ref, gn_ref, y_ref, ys_ref, sp_ref, st_ref):
        @pl.when(pl.program_id(0) == 0)
        def _():
            st_ref[...] = jnp.zeros_like(st_ref)

        e = _head_matrix()
        q = _ssd_chunk_common(pre_ref[...], dtr_ref[...], dtb_ref[...], alog_ref[...], e)
        cs, tri, xc, xd = q["cs"], q["tri"], q["xc"], q["xd"]
        cs_t = cs.T
        st = st_ref[...]
        sp_ref[0] = st
        lane = lax.broadcasted_iota(jnp.int32, (1, LANES), 1)
        halves = (lane < HEAD_DIM, lane >= HEAD_DIM)

        g_mat, y_off, s_new = [], [], []
        for g in range(2):
            bg = _group(q["bm"], g, SSD_STATE)
            cg = _group(q["cm"], g, SSD_STATE)
            bgb, cgb = bg.astype(bf16), cg.astype(bf16)
            g_mat.append(_dot_nt(cgb, bgb))
            y_off.append(_dot(cgb, _group(st, g, GW).astype(bf16)))
            s_new.append(_dot(bg.T.astype(bf16), _group(xd, g, GW).astype(bf16)))
        y_off = jnp.concatenate(y_off, axis=1) * q["ecs_e"]
        st_ref[...] = st * q["cd_e"] + jnp.concatenate(s_new, axis=1)

        pairs = []
        for j in range(SSD_HEADS // 2):
            xp = xc[:, j * LANES:(j + 1) * LANES]
            acc = jnp.zeros((CHUNK, LANES), f32)
            for hh in range(2):
                h = 2 * j + hh
                seg = cs[:, h:h + 1] - cs_t[h:h + 1, :]
                lm = jnp.exp(jnp.where(tri, seg, -1e30))
                m = (g_mat[h // 8] * lm).astype(bf16)
                acc = acc + _dot(m, jnp.where(halves[hh], xp, 0.0).astype(bf16))
            pairs.append(acc)
        y = jnp.concatenate(pairs, axis=1) + y_off + q["xs"] * de_ref[...]
        y_ref[...] = y

        zz = z_ref[...]
        v = y * (zz * _sigmoid(zz))
        outs = []
        for g in range(2):
            vg = _group(v, g, GW)
            outs.append(vg * lax.rsqrt(jnp.mean(vg * vg, axis=-1, keepdims=True) + EPS))
        ys_ref[...] = (jnp.concatenate(outs, axis=1) * gn_ref[...]).astype(bf16)

    ch = lambda n: pl.BlockSpec((CHUNK, n), lambda c: (c, 0))
    return _pallas(
        body, name="ssd_fwd", grid=(nc,),
        in_specs=[ch(XBC_WIDTH), ch(LANES), ch(SSD_WIDTH), _full((1, LANES)), _full((1, LANES)), _full((1, SSD_WIDTH)),
                  _full((1, SSD_WIDTH))],
        out_specs=[ch(SSD_WIDTH), ch(SSD_WIDTH), pl.BlockSpec((1, SSD_STATE, SSD_WIDTH), lambda c: (c, 0, 0))],
        out_shape=[jax.ShapeDtypeStruct((T, SSD_WIDTH), f32), jax.ShapeDtypeStruct((T, SSD_WIDTH), bf16),
                   jax.ShapeDtypeStruct((nc, SSD_STATE, SSD_WIDTH), f32)],
        scratch_shapes=[pltpu.VMEM((SSD_STATE, SSD_WIDTH), f32)],
        compiler_params=_cparams(("arbitrary",), VMEM_MID),
    )(pre, dtr, z, dtb, alog, dskip_e, gn)


def _w_out_spec():
    n = 2 * SSD_WIDTH // N_CHIPS
    return pl.BlockSpec((N_CHIPS, n, D_MODEL), lambda *_: (0, OUT_OFF // n, 0))


def _out_proj_fwd(x, ys, yc, gath, g):
    T = x.shape[0]
    tm = min(512, T)
    n = 2 * SSD_WIDTH // N_CHIPS

    def body(x_ref, ys_ref, yc_ref, w_ref, g_ref, h_ref, u_ref):
        h = (x_ref[...] + _dot(ys_ref[:, 0:n], w_ref[0]) + _dot(ys_ref[:, n:], w_ref[1])
             + _dot(yc_ref[:, 0:n], w_ref[2]) + _dot(yc_ref[:, n:], w_ref[3]))
        h_ref[...] = h
        u_ref[...] = _rms(h, g_ref[...]).astype(bf16)

    row = pl.BlockSpec((tm, D_MODEL), lambda i: (i, 0))
    return _pallas(
        body, name="out_proj_fwd", grid=(T // tm,),
        in_specs=[row, row, row, _w_out_spec(), _full((1, D_MODEL))],
        out_specs=[row, row],
        out_shape=[jax.ShapeDtypeStruct((T, D_MODEL), f32), jax.ShapeDtypeStruct((T, D_MODEL), bf16)],
        compiler_params=_cparams(("parallel",), VMEM_MID),
    )(x, ys, yc, gath, g)


def _w_up_spec():
    return pl.BlockSpec((1, D_MODEL, D_MODEL), lambda i, b: (b, UP_OFF // D_MODEL, 0))


def _w_down_spec():
    return pl.BlockSpec((1, D_MODEL, D_MODEL), lambda i, b: (b, DOWN_OFF // D_MODEL, 0))


def _mlp_fwd(h1, u1, gath, g_next):
    T = h1.shape[0]
    tm = min(512, T)
    nb = D_FF // D_MODEL

    def body(h_ref, u_ref, wu_ref, wd_ref, g_ref, r_ref, h2_ref, u2_ref, acc_ref):
        b = pl.program_id(1)

        @pl.when(b == 0)
        def _():
            acc_ref[...] = jnp.zeros_like(acc_ref)

        r = jnp.maximum(_dot(u_ref[...], wu_ref[0]), 0.0)
        r_ref[...] = r.astype(bf16)
        acc_ref[...] += _dot((r * r).astype(bf16), wd_ref[0])

        @pl.when(b == nb - 1)
        def _():
            h2 = h_ref[...] + acc_ref[...]
            h2_ref[...] = h2
            u2_ref[...] = _rms(h2, g_ref[...]).astype(bf16)

    row = pl.BlockSpec((tm, D_MODEL), lambda i, b: (i, 0))
    return _pallas(
        body, name="mlp_fwd", grid=(T // tm, nb),
        in_specs=[row, row, _w_up_spec(), _w_down_spec(), _full((1, D_MODEL))],
        out_specs=[pl.BlockSpec((tm, D_MODEL), lambda i, b: (i, b)), row, row],
        out_shape=[jax.ShapeDtypeStruct((T, D_FF), bf16), jax.ShapeDtypeStruct((T, D_MODEL), f32),
                   jax.ShapeDtypeStruct((T, D_MODEL), bf16)],
        scratch_shapes=[pltpu.VMEM((tm, D_MODEL), f32)],
        compiler_params=_cparams(("parallel", "arbitrary"), VMEM_MID),
    )(h1, u1, gath, gath, g_next)


def _ple_loss(h2, u2, p, tgt, gath, b_pg, w_ple, g_ple, g_fin, g_pg):
    T = h2.shape[0]
    tm = min(256, T)
    npg = D_MODEL // N_CHIPS

    def body(h2_ref, u2_ref, p_ref, t_ref, wpg_ref, bpg_ref, wple_ref, gple_ref, gfin_ref, gpg_ref,
             loss_ref, dh2_ref, dh2b_ref, dgp_ref, dep_ref, dgfin_ref, dgple_ref, dbpg_ref, dgpg_ref):
        @pl.when(pl.program_id(0) == 0)
        def _():
            loss_ref[...] = jnp.zeros_like(loss_ref)
            dgfin_ref[...] = jnp.zeros_like(dgfin_ref)
            dgple_ref[...] = jnp.zeros_like(dgple_ref)
            dbpg_ref[...] = jnp.zeros_like(dbpg_ref)
            dgpg_ref[...] = jnp.zeros_like(dgpg_ref)

        h2 = h2_ref[...]
        gate_pre = bpg_ref[...]
        for b in range(N_CHIPS):
            gate_pre = gate_pre + _dot(u2_ref[:, b * npg:(b + 1) * npg], wpg_ref[b])
        gate = _sigmoid(gate_pre)
        e_pre = _dot(p_ref[...].astype(bf16), wple_ref[...])
        emb = _rms(e_pre, gple_ref[...])
        h3 = h2 + gate * emb
        diff = _rms(h3, gfin_ref[...]) - t_ref[...]
        sq = jnp.sum(jnp.sum(diff * diff, axis=1, keepdims=True), axis=0, keepdims=True)
        loss_ref[...] += (0.5 / D_MODEL) * sq
        dh3, dgfin = _rms_bwd(diff * (1.0 / D_MODEL), h3, gfin_ref[...])
        dgfin_ref[...] += dgfin
        dgp = dh3 * emb * gate * (1.0 - gate)
        dbpg_ref[...] += jnp.sum(dgp, axis=0, keepdims=True)
        dep, dgple = _rms_bwd(dh3 * gate, e_pre, gple_ref[...])
        dgple_ref[...] += dgple
        dgpb = dgp.astype(bf16)
        dgp_ref[...] = dgpb
        dep_ref[...] = dep.astype(bf16)
        du2 = jnp.concatenate([_dot_nt(dgpb, wpg_ref[b]) for b in range(N_CHIPS)], axis=1)
        dx, dgpg = _rms_bwd(du2, h2, gpg_ref[...])
        dgpg_ref[...] += dgpg
        dh2 = dh3 + dx
        dh2_ref[...] = dh2
        dh2b_ref[...] = dh2.astype(bf16)

    row = pl.BlockSpec((tm, D_MODEL), lambda i: (i, 0))
    vec = _full((1, D_MODEL))
    vshape = jax.ShapeDtypeStruct((1, D_MODEL), f32)
    return _pallas(
        body, name="ple_loss", grid=(T // tm,),
        in_specs=[row, row, pl.BlockSpec((tm, PLE_DIM), lambda i: (i, 0)), row,
                  pl.BlockSpec((N_CHIPS, npg, D_MODEL), lambda i: (0, PG_OFF // npg, 0)), vec, _full(w_ple.shape),
                  vec, vec, vec],
        out_specs=[_full((8, LANES)), row, row, row, row, vec, vec, vec, vec],
        out_shape=[jax.ShapeDtypeStruct((8, LANES), f32), jax.ShapeDtypeStruct((T, D_MODEL), f32),
                   jax.ShapeDtypeStruct((T, D_MODEL), bf16), jax.ShapeDtypeStruct((T, D_MODEL), bf16),
                   jax.ShapeDtypeStruct((T, D_MODEL), bf16), vshape, vshape, vshape, vshape],
        compiler_params=_cparams(("arbitrary",), VMEM_MID),
    )(h2, u2, p, tgt, gath, b_pg, w_ple, g_ple, g_fin, g_pg)


def _mlp_bwd(dh2, r, gath, h1, g):
    T = dh2.shape[0]
    tm = min(512, T)
    nb = D_FF // D_MODEL

    def body(dh2_ref, r_ref, wd_ref, wu_ref, h1_ref, g_ref, dhp_ref, dh1_ref, dh1b_ref, dg_ref, acc_ref):
        i, b = pl.program_id(0), pl.program_id(1)

        @pl.when(b == 0)
        def _():
            acc_ref[...] = jnp.zeros_like(acc_ref)

        @pl.when((b == 0) & (i == 0))
        def _():
            dg_ref[...] = jnp.zeros_like(dg_ref)

        dact = _dot_nt(dh2_ref[...].astype(bf16), wd_ref[0])
        dhp = (dact * 2.0 * r_ref[...].astype(f32)).astype(bf16)
        dhp_ref[...] = dhp
        acc_ref[...] += _dot_nt(dhp, wu_ref[0])

        @pl.when(b == nb - 1)
        def _():
            dx, dg = _rms_bwd(acc_ref[...], h1_ref[...], g_ref[...])
            dg_ref[...] += dg
            dh1 = dh2_ref[...] + dx
            dh1_ref[...] = dh1
            dh1b_ref[...] = dh1.astype(bf16)

    row = pl.BlockSpec((tm, D_MODEL), lambda i, b: (i, 0))
    return _pallas(
        body, name="mlp_bwd", grid=(T // tm, nb),
        in_specs=[row, pl.BlockSpec((tm, D_MODEL), lambda i, b: (i, b)), _w_down_spec(), _w_up_spec(), row,
                  _full((1, D_MODEL))],
        out_specs=[pl.BlockSpec((tm, D_MODEL), lambda i, b: (i, b)), row, row, _full((1, D_MODEL))],
        out_shape=[jax.ShapeDtypeStruct((T, D_FF), bf16), jax.ShapeDtypeStruct((T, D_MODEL), f32),
                   jax.ShapeDtypeStruct((T, D_MODEL), bf16), jax.ShapeDtypeStruct((1, D_MODEL), f32)],
        scratch_shapes=[pltpu.VMEM((tm, D_MODEL), f32)],
        compiler_params=_cparams(("arbitrary", "arbitrary"), VMEM_MID),
    )(dh2, r, gath, gath, h1, g)


def _out_proj_bwd(dh1, gath, co, ln_g, ln_b, rider=None):
    T = dh1.shape[0]
    tm = min(512, T)

    def body(dh_ref, w_ref, co_ref, g_ref, b_ref, dys_ref, dco_ref, dg_ref, db_ref):
        @pl.when(pl.program_id(0) == 0)
        def _():
            dg_ref[...] = jnp.zeros_like(dg_ref)
            db_ref[...] = jnp.zeros_like(db_ref)

        dhb = dh_ref[...].astype(bf16)
        dys_ref[...] = jnp.concatenate([_dot_nt(dhb, w_ref[0]), _dot_nt(dhb, w_ref[1])], axis=1)
        dyc = jnp.concatenate([_dot_nt(dhb, w_ref[2]), _dot_nt(dhb, w_ref[3])], axis=1)
        co = co_ref[...]
        mu = jnp.mean(co, axis=-1, keepdims=True)
        xc = co - mu
        rstd = lax.rsqrt(jnp.mean(xc * xc, axis=-1, keepdims=True) + EPS)
        xh = xc * rstd
        yn = xh * g_ref[...] + b_ref[...]
        dyn = dyc * _dsilu(yn)
        dg_ref[...] += jnp.sum(dyn * xh, axis=0, keepdims=True)
        db_ref[...] += jnp.sum(dyn, axis=0, keepdims=True)
        dxh = dyn * g_ref[...]
        dco_ref[...] = rstd * (dxh - jnp.mean(dxh, axis=-1, keepdims=True)
                               - xh * jnp.mean(dxh * xh, axis=-1, keepdims=True))

    row = pl.BlockSpec((tm, D_MODEL), lambda i: (i, 0))
    vec = _full((1, CONF_WIDTH))
    vshape = jax.ShapeDtypeStruct((1, CONF_WIDTH), f32)
    return _call(
        body, (dh1, gath, co, ln_g, ln_b), name="out_proj_bwd", grid=(T // tm,),
        in_specs=[row, _w_out_spec(), row, vec, vec],
        out_specs=[row, row, vec, vec],
        out_shape=[jax.ShapeDtypeStruct((T, SSD_WIDTH), f32), jax.ShapeDtypeStruct((T, CONF_WIDTH), f32), vshape, vshape],
        params=_cparams(("arbitrary",), VMEM_MID), rider=rider)


def _bwd_offsets(K):
    return [K - 1 - k for k in range(K)]


def _next_halo_spec(hb, tm, C, T):
    return pl.BlockSpec((hb, C), lambda i: (jnp.minimum((i + 1) * (tm // hb), T // hb - 1), 0))


DW_RB = 8
DW_UNROLL = 4
DW_ACC_VREGS = 32


def _conv_dw(dw_ref, bufd_ref, bufx_ref, phx_ref, offs_x, tm, C):
    K = len(offs_x)
    group = max(1, DW_ACC_VREGS // (C // LANES))
    for k0 in range(0, K, group):
        ks = list(range(k0, min(k0 + group, K)))

        def step(i, accs, ks=ks):
            for u in range(DW_UNROLL):
                r0 = pl.multiple_of((i * DW_UNROLL + u) * DW_RB, DW_RB)
                d = bufd_ref[pl.ds(r0, DW_RB), :]
                accs = tuple(acc + _window(bufx_ref, phx_ref, offs_x, offs_x[k], r0, DW_RB) * d
                             for k, acc in zip(ks, accs))
            return accs

        accs = lax.fori_loop(0, tm // (DW_RB * DW_UNROLL), step, tuple(jnp.zeros((DW_RB, C), f32) for _ in ks))
        for k, acc in zip(ks, accs):
            dw_ref[k:k + 1, :] += jnp.sum(acc, axis=0, keepdims=True)


def _fill_bwd_buffers(dcur_ref, dnext_ref, xcur_ref, xprev_ref, bufd_ref, bufx_ref, phd_ref, phx_ref, offs_d, offs_x,
                      hb, tm, first, last):
    bufd_ref[0:tm, :] = dcur_ref[...]
    bufd_ref[tm:tm + hb, :] = dnext_ref[...] * jnp.where(last, 0.0, 1.0)
    bufx_ref[0:hb, :] = xprev_ref[...] * jnp.where(first, 0.0, 1.0)
    bufx_ref[hb:hb + tm, :] = xcur_ref[...]
    _make_phases(bufd_ref, phd_ref, offs_d, tm)
    _make_phases(bufx_ref, phx_ref, offs_x, tm)


def _ssd_conv_bwd(dpre, xbc, w):
    T, C = xbc.shape
    K, hb = SSD_CONV, 8
    tm = min(256, T)
    nt = T // tm
    offs_d, offs_x = _bwd_offsets(K), _fwd_offsets(K, hb)

    def body(dcur_ref, dnext_ref, xcur_ref, xprev_ref, w_ref, dx_ref, dw_ref, db_ref, bufd_ref, bufx_ref, phd_ref, phx_ref):
        i = pl.program_id(0)

        @pl.when(i == 0)
        def _():
            dw_ref[...] = jnp.zeros_like(dw_ref)
            db_ref[...] = jnp.zeros_like(db_ref)

        _fill_bwd_buffers(dcur_ref, dnext_ref, xcur_ref, xprev_ref, bufd_ref, bufx_ref, phd_ref, phx_ref, offs_d, offs_x,
                          hb, tm, i == 0, i == nt - 1)

        def chunk(j, carry):
            r0 = pl.multiple_of(j * CONV_RB, CONV_RB)
            dx_ref[pl.ds(r0, CONV_RB), :] = _conv_rows(w_ref, bufd_ref, phd_ref, offs_d, r0, CONV_RB).astype(bf16)
            return carry

        lax.fori_loop(0, tm // CONV_RB, chunk, 0)
        _conv_dw(dw_ref, bufd_ref, bufx_ref, phx_ref, offs_x, tm, C)
        db_ref[...] += jnp.sum(dcur_ref[...], axis=0, keepdims=True)

    row = pl.BlockSpec((tm, C), lambda i: (i, 0))
    return _pallas(
        body, name="ssd_conv_bwd", grid=(nt,),
        in_specs=[row, _next_halo_spec(hb, tm, C, T), row, _prev_halo_spec(hb, tm, C), _full((SUBLANES * K, C))],
        out_specs=[row, _full((8, C)), _full((1, C))],
        out_shape=[jax.ShapeDtypeStruct((T, C), bf16), jax.ShapeDtypeStruct((8, C), f32), jax.ShapeDtypeStruct((1, C), f32)],
        scratch_shapes=[pltpu.VMEM((tm + hb, C), f32), pltpu.VMEM((hb + tm, C), f32),
                        pltpu.VMEM(_phase_shape(offs_d, tm, C), f32),
                        pltpu.VMEM(_phase_shape(offs_x, tm, C), f32)],
        compiler_params=_cparams(("arbitrary",), VMEM_BIG),
    )(dpre, dpre, xbc, xbc, _sublane_rows(w))


def _conf_conv_bwd(dco, v, w, cv, cg, rider=None):
    T, C = v.shape
    K, hb = CONF_KERNEL, 32
    tm = min(256, T)
    nt = T // tm
    offs_d, offs_x = _bwd_offsets(K), _fwd_offsets(K, hb)

    def body(dcur_ref, dnext_ref, vcur_ref, vprev_ref, w_ref, cv_ref, cg_ref, dcv_ref, dcg_ref, dw_ref, db_ref,
             bufd_ref, bufx_ref, phd_ref, phx_ref):
        i = pl.program_id(0)

        @pl.when(i == 0)
        def _():
            dw_ref[...] = jnp.zeros_like(dw_ref)
            db_ref[...] = jnp.zeros_like(db_ref)

        _fill_bwd_buffers(dcur_ref, dnext_ref, vcur_ref, vprev_ref, bufd_ref, bufx_ref, phd_ref, phx_ref, offs_d, offs_x,
                          hb, tm, i == 0, i == nt - 1)

        def chunk(j, carry):
            r0 = pl.multiple_of(j * CONV_RB, CONV_RB)
            rows = pl.ds(r0, CONV_RB)
            dv = _conv_rows(w_ref, bufd_ref, phd_ref, offs_d, r0, CONV_RB)
            s = _sigmoid(cg_ref[rows, :])
            dcv_ref[rows, :] = (dv * s).astype(bf16)
            dcg_ref[rows, :] = (dv * cv_ref[rows, :] * s * (1.0 - s)).astype(bf16)
            return carry

        lax.fori_loop(0, tm // CONV_RB, chunk, 0)
        _conv_dw(dw_ref, bufd_ref, bufx_ref, phx_ref, offs_x, tm, C)
        db_ref[...] += jnp.sum(dcur_ref[...], axis=0, keepdims=True)

    row = pl.BlockSpec((tm, C), lambda i: (i, 0))
    return _call(
        body, (dco, dco, v, v, _sublane_rows(w), cv, cg), name="conf_conv_bwd", grid=(nt,),
        in_specs=[row, _next_halo_spec(hb, tm, C, T), row, _prev_halo_spec(hb, tm, C), _full((SUBLANES * K, C)), row, row],
        out_specs=[row, row, _full((32, C)), _full((1, C))],
        out_shape=[jax.ShapeDtypeStruct((T, C), bf16), jax.ShapeDtypeStruct((T, C), bf16),
                   jax.ShapeDtypeStruct((32, C), f32), jax.ShapeDtypeStruct((1, C), f32)],
        scratch_shapes=[pltpu.VMEM((tm + hb, C), f32), pltpu.VMEM((hb + tm, C), f32),
                        pltpu.VMEM(_phase_shape(offs_d, tm, C), f32),
                        pltpu.VMEM(_phase_shape(offs_x, tm, C), f32)],
        params=_cparams(("arbitrary",), VMEM_BIG), rider=rider)


def _ssd_bwd(dys, y, z, pre, dtr, sprev, dtb, alog, dskip_e, gn):
    T = pre.shape[0]
    nc = T // CHUNK
    GW = SSD_WIDTH // 2

    def body(dys_ref, y_ref, z_ref, pre_ref, dtr_ref, sp_ref, dtb_ref, alog_ref, de_ref, gn_ref,
             dz_ref, dpre_ref, ddtr_ref, dgn_ref, dd_ref, dal_ref, ddtb_ref, ds_ref):
        @pl.when(pl.program_id(0) == 0)
        def _():
            ds_ref[...] = jnp.zeros_like(ds_ref)
            dgn_ref[...] = jnp.zeros_like(dgn_ref)
            dd_ref[...] = jnp.zeros_like(dd_ref)
            dal_ref[...] = jnp.zeros_like(dal_ref)
            ddtb_ref[...] = jnp.zeros_like(ddtb_ref)

        e = _head_matrix()
        pre = pre_ref[...]
        dtr_b = dtr_ref[...] + dtb_ref[...]
        q = _ssd_chunk_common(pre, dtr_ref[...], dtb_ref[...], alog_ref[...], e)
        cs, tri, xc, xd, xs, dt = q["cs"], q["tri"], q["xc"], q["xd"], q["xs"], q["dt"]
        cs_t = cs.T
        st = sp_ref[0]
        dsn = ds_ref[...]
        lane = lax.broadcasted_iota(jnp.int32, (1, LANES), 1)
        halves = (lane < HEAD_DIM, lane >= HEAD_DIM)
        row_i = lax.broadcasted_iota(jnp.int32, (CHUNK, CHUNK), 0)
        col_i = lax.broadcasted_iota(jnp.int32, (CHUNK, CHUNK), 1)
        tri_t = col_i >= row_i

        y = y_ref[...]
        zz = z_ref[...]
        sz = _sigmoid(zz)
        silu_z = zz * sz
        v = y * silu_z
        dout = dys_ref[...]
        gn_v = gn_ref[...]
        dv, vh = [], []
        for g in range(2):
            vg = _group(v, g, GW)
            rstd = lax.rsqrt(jnp.mean(vg * vg, axis=-1, keepdims=True) + EPS)
            vhg = vg * rstd
            dvh = _group(dout, g, GW) * _group(gn_v, g, GW)
            dv.append(rstd * (dvh - vhg * jnp.mean(dvh * vhg, axis=-1, keepdims=True)))
            vh.append(vhg)
        dv = jnp.concatenate(dv, axis=1)
        dgn_ref[...] += jnp.sum(dout * jnp.concatenate(vh, axis=1), axis=0, keepdims=True)
        dy = dv * silu_z
        dz_ref[...] = (dv * y * (sz * (1.0 + zz * (1.0 - sz)))).astype(bf16)

        dd_row = jnp.sum(dy * xs, axis=0, keepdims=True)
        dd_ref[...] += _contract(jnp.broadcast_to(dd_row, (8, SSD_WIDTH)), e)[0:1, :]
        dxs = dy * de_ref[...]

        dz_in = dy * q["ecs_e"]
        g_mat, gt_mat, dcm, dbm, dsp, dxd, y_off = [], [], [], [], [], [], []
        bgs, cgs = [], []
        for g in range(2):
            bg = _group(q["bm"], g, SSD_STATE)
            cg = _group(q["cm"], g, SSD_STATE)
            bgb, cgb = bg.astype(bf16), cg.astype(bf16)
            bgs.append(bgb)
            cgs.append(cgb)
            stg = _group(st, g, GW).astype(bf16)
            dsng = _group(dsn, g, GW).astype(bf16)
            dzg = _group(dz_in, g, GW).astype(bf16)
            g_mat.append(_dot_nt(cgb, bgb))
            gt_mat.append(_dot_nt(bgb, cgb))
            y_off.append(_dot(cgb, stg))
            dcm.append(_dot_nt(dzg, stg))
            dsp.append(_dot(cg.T.astype(bf16), dzg))
            dbm.append(_dot_nt(_group(xd, g, GW).astype(bf16), dsng))
            dxd.append(_dot(bgb, dsng))
        y_off = jnp.concatenate(y_off, axis=1) * q["ecs_e"]
        dxd = jnp.concatenate(dxd, axis=1)
        ds_ref[...] = dsn * q["cd_e"] + jnp.concatenate(dsp, axis=1)
        dcd_row = jnp.sum(dsn * st, axis=0, keepdims=True) * q["cd_e"]
        t_e = dxd * xd
        dcs = _contract(dy * y_off - t_e, e)
        last_row = _contract(jnp.broadcast_to(dcd_row + jnp.sum(t_e, axis=0, keepdims=True), (8, SSD_WIDTH)), e)[0:1, :]
        dxc_state = dxd * q["dte_e"]

        dg_acc = [jnp.zeros((CHUNK, CHUNK), f32), jnp.zeros((CHUNK, CHUNK), f32)]
        dgt_acc = [jnp.zeros((CHUNK, CHUNK), f32), jnp.zeros((CHUNK, CHUNK), f32)]
        dxc_pairs = []
        for j in range(SSD_HEADS // 2):
            dyp_f = dy[:, j * LANES:(j + 1) * LANES]
            xcp_f = xc[:, j * LANES:(j + 1) * LANES]
            acc = jnp.zeros((CHUNK, LANES), f32)
            for hh in range(2):
                h = 2 * j + hh
                g = h // 8
                dyp = jnp.where(halves[hh], dyp_f, 0.0).astype(bf16)
                xcp = jnp.where(halves[hh], xcp_f, 0.0).astype(bf16)
                lm = jnp.exp(jnp.where(tri, cs[:, h:h + 1] - cs_t[h:h + 1, :], -1e30))
                lm_t = jnp.exp(jnp.where(tri_t, cs_t[h:h + 1, :] - cs[:, h:h + 1], -1e30))
                dm = _dot_nt(dyp, xcp) * lm
                dm_t = _dot_nt(xcp, dyp) * lm_t
                acc = acc + _dot((gt_mat[g] * lm_t).astype(bf16), dyp)
                dg_acc[g] = dg_acc[g] + dm
                dgt_acc[g] = dgt_acc[g] + dm_t
                qd = jnp.sum(dm * g_mat[g] - dm_t * gt_mat[g], axis=1, keepdims=True)
                dcs = dcs + qd * (lane == h).astype(f32)
            dxc_pairs.append(acc)
        dxc = jnp.concatenate(dxc_pairs, axis=1) + dxc_state
        for g in range(2):
            dcm[g] = dcm[g] + _dot(dg_acc[g].astype(bf16), bgs[g])
            dbm[g] = dbm[g] + _dot(dgt_acc[g].astype(bf16), cgs[g])

        dxs = dxs + dxc * q["dt_e"]
        ddt = _contract(dxc * xs, e)
        dcs = dcs + jnp.where(row_i == CHUNK - 1, jnp.broadcast_to(last_row, (CHUNK, LANES)), 0.0)
        da = jnp.dot(tri_t.astype(f32), dcs, precision=lax.Precision.HIGHEST, preferred_element_type=f32)
        ddt = ddt + da * q["a_neg"]
        dal_ref[...] += jnp.sum(da * dt, axis=0, keepdims=True) * q["a_neg"]
        ddtr = ddt * _sigmoid(dtr_b) * (lane < SSD_HEADS).astype(f32)
        ddtb_ref[...] += jnp.sum(ddtr, axis=0, keepdims=True)
        ddtr_ref[...] = ddtr.astype(bf16)

        dact = jnp.concatenate([dxs, dbm[0], dbm[1], dcm[0], dcm[1]], axis=1)
        dpre_ref[...] = dact * _dsilu(pre)

    rev = lambda n: pl.BlockSpec((CHUNK, n), lambda c: (nc - 1 - c, 0))
    vec = _full((1, LANES))
    vshape = jax.ShapeDtypeStruct((1, LANES), f32)
    return _pallas(
        body, name="ssd_bwd", grid=(nc,),
        in_specs=[rev(SSD_WIDTH), rev(SSD_WIDTH), rev(SSD_WIDTH), rev(XBC_WIDTH), rev(LANES),
                  pl.BlockSpec((1, SSD_STATE, SSD_WIDTH), lambda c: (nc - 1 - c, 0, 0)),
                  vec, vec, _full((1, SSD_WIDTH)), _full((1, SSD_WIDTH))],
        out_specs=[rev(SSD_WIDTH), rev(XBC_WIDTH), rev(LANES), _full((1, SSD_WIDTH)), vec, vec, vec],
        out_shape=[jax.ShapeDtypeStruct((T, SSD_WIDTH), bf16), jax.ShapeDtypeStruct((T, XBC_WIDTH), f32),
                   jax.ShapeDtypeStruct((T, LANES), bf16), jax.ShapeDtypeStruct((1, SSD_WIDTH), f32),
                   vshape, vshape, vshape],
        scratch_shapes=[pltpu.VMEM((SSD_STATE, SSD_WIDTH), f32)],
        compiler_params=_cparams(("arbitrary",), VMEM_MID),
    )(dys, y, z, pre, dtr, sprev, dtb, alog, dskip_e, gn)


def _in_proj_bwd(dz, dxbc, dcv, dcg, ddt, gin, x, dh1, g):
    T = x.shape[0]
    tm = min(256, T)

    def body(dz_ref, dx_ref, dcv_ref, dcg_ref, ddt_ref, gin_ref, x_ref, dh_ref, g_ref, gx_ref, dg_ref, wt_ref):
        @pl.when(pl.program_id(0) == 0)
        def _():
            dg_ref[...] = jnp.zeros_like(dg_ref)
            _assemble_w_in_t(gin_ref, wt_ref)

        du = (_dot(dz_ref[...], wt_ref[0:O_XBC, :]) + _dot(dx_ref[...], wt_ref[O_XBC:O_DT, :])
              + _dot(dcv_ref[...], wt_ref[O_CV:O_CG, :]) + _dot(dcg_ref[...], wt_ref[O_CG:IN_WIDTH, :])
              + _dot(ddt_ref[...], wt_ref[O_DT:O_DT + LANES, :]))
        dx, dg = _rms_bwd(du, x_ref[...], g_ref[...])
        dg_ref[...] += dg
        gx_ref[...] = dh_ref[...] + dx

    row = lambda n: pl.BlockSpec((tm, n), lambda i: (i, 0))
    return _pallas(
        body, name="in_proj_bwd", grid=(T // tm,),
        in_specs=[row(SSD_WIDTH), row(XBC_WIDTH), row(CONF_WIDTH), row(CONF_WIDTH), row(LANES), _full(gin.shape),
                  row(D_MODEL), row(D_MODEL), _full((1, D_MODEL))],
        out_specs=[row(D_MODEL), _full((1, D_MODEL))],
        out_shape=[jax.ShapeDtypeStruct((T, D_MODEL), f32), jax.ShapeDtypeStruct((1, D_MODEL), f32)],
        scratch_shapes=[pltpu.VMEM((IN_WIDTH, D_MODEL), bf16)],
        compiler_params=_cparams(("arbitrary",), VMEM_BIG),
    )(dz, dxbc, dcv, dcg, ddt, gin, x, dh1, g)


def _weight_grad(a, g, name, square=False, slab=None, place=None, tk=512):
    T, K = a.shape
    N = g.shape[1]
    tk = min(tk, K)
    tn = 1024 if N % 1024 == 0 else min(512, N)
    tt = min(2048, T)

    def body(a_ref, g_ref, *rest):
        o_ref = rest[-1]
        acc = _dot_tn(_operand(a_ref[...]), g_ref[...].astype(bf16))
        t = pl.program_id(2)
        shaped = acc if slab is None else acc[None]

        @pl.when(t == 0)
        def _():
            o_ref[...] = shaped

        @pl.when(t > 0)
        def _():
            o_ref[...] += shaped

    def _operand(av):
        if square:
            av = av.astype(f32)
            av = av * av
        return av.astype(bf16)

    in_specs = [pl.BlockSpec((tt, tk), lambda i, j, t: (t, i)), pl.BlockSpec((tt, tn), lambda i, j, t: (t, j))]
    grid = (K // tk, N // tn, T // tt)
    params = _cparams(("parallel", "parallel", "arbitrary"), VMEM_MID)
    if slab is None:
        return _pallas(
            body, name=name, grid=grid, in_specs=in_specs,
            out_specs=pl.BlockSpec((tk, tn), lambda i, j, t: (i, j)),
            out_shape=jax.ShapeDtypeStruct((K, N), f32), compiler_params=params,
        )(a, g)
    return _pallas(
        body, name=name, grid=grid, in_specs=in_specs + [ANY],
        out_specs=pl.BlockSpec((1, tk, tn), lambda i, j, t: place(i, j)),
        out_shape=jax.ShapeDtypeStruct(slab.shape, f32), input_output_aliases={2: 0}, compiler_params=params,
    )(a, g, slab)


def _place():
    return lax.axis_index("x"), lax.axis_index("y"), lax.axis_index("c")


def _other_chips(x, y):
    return [(1 - x, y), (x, 1 - y), (1 - x, 1 - y)]


def _remote(src, dst, ssem, rsem, dev):
    return pltpu.make_async_remote_copy(src_ref=src, dst_ref=dst, send_sem=ssem, recv_sem=rsem, device_id=dev,
                                        device_id_type=MESH)


def _gather_weights(arrays, convw):
    n = len(arrays)
    halves = tuple(a.shape[1] // 2 for a in arrays)

    def body(*refs):
        cw_ref, cwo_ref = refs[n], refs[2 * n + 1]
        ssem, rsem, lsem = refs[2 * n + 2:]
        triples = tuple(zip(refs[:n], refs[n + 1:2 * n + 1], halves))
        x, y, c = _place()
        me_b = 2 * x + y
        sib = (x, y, 1 - c)
        chips = _other_chips(x, y)
        loc = pltpu.make_async_copy(cw_ref, cwo_ref.at[me_b], lsem)
        loc.start()
        sends = []
        for j, (src, dst, h) in enumerate(triples):
            mine = pl.ds(c * h, h)
            for k, (px, py) in enumerate(chips):
                s = 6 * j + k
                sends.append(_remote(src.at[me_b, mine], dst.at[me_b, mine], ssem.at[s], rsem.at[s], (px, py, c)))
        for k, (px, py) in enumerate(chips):
            sends.append(_remote(cw_ref, cwo_ref.at[me_b], ssem.at[6 * n + k], rsem.at[6 * n + k], (px, py, c)))
        for cp in sends:
            cp.start()
        for j, (src, dst, h) in enumerate(triples):
            mine = pl.ds(c * h, h)
            for k, (px, py) in enumerate(chips):
                b = 2 * px + py
                s = 6 * j + k
                _remote(src.at[b, mine], dst.at[b, mine], ssem.at[s], rsem.at[s], (px, py, c)).wait_recv()
                fw = _remote(dst.at[b, mine], dst.at[b, mine], ssem.at[s + 3], rsem.at[s + 3], sib)
                fw.start()
                sends.append(fw)
        for k, (px, py) in enumerate(chips):
            b = 2 * px + py
            _remote(cw_ref, cwo_ref.at[b], ssem.at[6 * n + k], rsem.at[6 * n + k], (px, py, c)).wait_recv()
        for j, (src, dst, h) in enumerate(triples):
            theirs = pl.ds((1 - c) * h, h)
            for k, (px, py) in enumerate(chips):
                b = 2 * px + py
                s = 6 * j + k + 3
                _remote(src.at[b, theirs], dst.at[b, theirs], ssem.at[s], rsem.at[s], sib).wait_recv()
        for cp in sends:
            cp.wait_send()
        loc.wait()

    return _pallas(
        body, name="gather_weights", in_specs=[ANY] * (n + 1), out_specs=[ANY] * (n + 1),
        out_shape=[jax.ShapeDtypeStruct(a.shape, bf16) for a in arrays]
        + [jax.ShapeDtypeStruct((N_CHIPS, CONVW_ROWS, D_MODEL), f32)],
        input_output_aliases={j: j for j in range(n)},
        scratch_shapes=[pltpu.SemaphoreType.DMA((6 * n + 3,)), pltpu.SemaphoreType.DMA((6 * n + 3,)),
                        pltpu.SemaphoreType.DMA(())],
    )(*arrays, convw)


def _gather_rider(gath0, lo, n):
    h = gath0.shape[1] // 2

    def copies(rins, routs, ssem, rsem, sending):
        (g_ref,), (o_ref,) = rins, routs
        x, y, c = _place()
        mine = pl.ds(c * h + lo, n)
        for k, (px, py) in enumerate(_other_chips(x, y)):
            b = 2 * x + y if sending else 2 * px + py
            yield _remote(g_ref.at[b, mine], o_ref.at[b, mine], ssem.at[k], rsem.at[k], (px, py, c))

    def start(*refs):
        for cp in copies(*refs, sending=True):
            cp.start()

    def finish(*refs):
        for cp in copies(*refs, sending=False):
            cp.wait()

    return _Rider([gath0], [jax.ShapeDtypeStruct(gath0.shape, gath0.dtype)], {0: 0}, 3, start, finish)


def _forward_to_sibling(gath):
    h = gath.shape[1] // 2

    def body(g_ref, o_ref, ssem, rsem):
        x, y, c = _place()
        sib = (x, y, 1 - c)
        mine, theirs = pl.ds(c * h, h), pl.ds((1 - c) * h, h)
        blocks = [2 * px + py for px, py in _other_chips(x, y)]
        sends = [_remote(g_ref.at[b, mine], o_ref.at[b, mine], ssem.at[k], rsem.at[k], sib) for k, b in enumerate(blocks)]
        for cp in sends:
            cp.start()
        for k, b in enumerate(blocks):
            _remote(g_ref.at[b, theirs], o_ref.at[b, theirs], ssem.at[k], rsem.at[k], sib).wait_recv()
        for cp in sends:
            cp.wait_send()

    return _pallas(
        body, name="forward_to_sibling", in_specs=[ANY], out_specs=ANY,
        out_shape=jax.ShapeDtypeStruct(gath.shape, gath.dtype), input_output_aliases={0: 0},
        scratch_shapes=[pltpu.SemaphoreType.DMA((3,)), pltpu.SemaphoreType.DMA((3,))],
    )(gath)


def _swap_copy(g_ref, r_ref, ssem, rsem):
    x, y, c = _place()
    h = r_ref.shape[1]
    return _remote(g_ref.at[:, pl.ds((1 - c) * h, h), :], r_ref, ssem.at[0], rsem.at[0], (x, y, 1 - c))


def _swap_rider(g):
    def start(rins, routs, ssem, rsem):
        _swap_copy(rins[0], routs[0], ssem, rsem).start()

    def finish(rins, routs, ssem, rsem):
        _swap_copy(rins[0], routs[0], ssem, rsem).wait()

    return _Rider([g], [jax.ShapeDtypeStruct((N_CHIPS, g.shape[1] // 2, g.shape[2]), g.dtype)], {}, 1, start, finish)


def _swap_halves(g):
    def body(g_ref, r_ref, ssem, rsem):
        cp = _swap_copy(g_ref, r_ref, ssem, rsem)
        cp.start()
        cp.wait()

    return _pallas(
        body, name="swap_halves", in_specs=[ANY], out_specs=ANY,
        out_shape=jax.ShapeDtypeStruct((N_CHIPS, g.shape[1] // 2, g.shape[2]), g.dtype),
        scratch_shapes=[pltpu.SemaphoreType.DMA((1,)), pltpu.SemaphoreType.DMA((1,))],
    )(g)


def _chip_sum(cidx, gslab, recv, name):
    half, C = recv.shape[1:]
    tr = half // 2 if (half // 2) % 16 == 0 else half

    def body(c_ref, g_ref, r_ref, o_ref):
        o_ref[...] = (g_ref[...] + r_ref[...]).astype(bf16)

    return _pallas(
        body, name=name,
        grid_spec=pltpu.PrefetchScalarGridSpec(
            num_scalar_prefetch=1, grid=(N_CHIPS, half // tr),
            in_specs=[pl.BlockSpec((1, tr, C), lambda b, i, c_ref: (b, c_ref[0] * (half // tr) + i, 0)),
                      pl.BlockSpec((1, tr, C), lambda b, i, c_ref: (b, i, 0))],
            out_specs=pl.BlockSpec((1, tr, C), lambda b, i, c_ref: (b, i, 0))),
        out_shape=jax.ShapeDtypeStruct((N_CHIPS, half, C), bf16),
        compiler_params=_cparams(("parallel", "parallel"), VMEM_MID),
    )(cidx, gslab, recv)


def _exchange_rider(h):
    def copies(rins, routs, ssem, rsem):
        x, y, c = _place()
        for k, (px, py) in enumerate(_other_chips(x, y)):
            yield _remote(rins[0].at[2 * px + py], routs[0].at[k], ssem.at[k], rsem.at[k], (px, py, c))

    def start(*refs):
        for cp in copies(*refs):
            cp.start()

    def finish(*refs):
        for cp in copies(*refs):
            cp.wait()

    return _Rider([h], [jax.ShapeDtypeStruct((3,) + h.shape[1:], h.dtype)], {}, 3, start, finish)


def _exchange(hb, small):
    def body(hb_ref, sm_ref, rb_ref, all_ref, ssem, rsem, lsem):
        x, y, c = _place()
        me = 4 * x + 2 * y + c
        chips = _other_chips(x, y)
        loc = pltpu.make_async_copy(sm_ref, all_ref.at[me], lsem)
        loc.start()
        sends = []
        for k, (px, py) in enumerate(chips):
            sends.append(_remote(hb_ref.at[2 * px + py], rb_ref.at[k], ssem.at[3 + k], rsem.at[3 + k], (px, py, c)))
        peers = []
        for r in range(1, N_DEV):
            peer = ((1 - x) if r & 4 else x, (1 - y) if r & 2 else y, (1 - c) if r & 1 else c)
            peers.append(peer)
            sends.append(_remote(sm_ref, all_ref.at[me], ssem.at[5 + r], rsem.at[5 + r], peer))
        for cp in sends:
            cp.start()
        for k, (px, py) in enumerate(chips):
            _remote(hb_ref.at[0], rb_ref.at[k], ssem.at[3 + k], rsem.at[3 + k], (px, py, c)).wait_recv()
        for r, peer in zip(range(1, N_DEV), peers):
            pid = 4 * peer[0] + 2 * peer[1] + peer[2]
            _remote(sm_ref, all_ref.at[pid], ssem.at[5 + r], rsem.at[5 + r], peer).wait_recv()
        for cp in sends:
            cp.wait_send()
        loc.wait()

    return _pallas(
        body, name="exchange", in_specs=[ANY, ANY], out_specs=[ANY, ANY],
        out_shape=[jax.ShapeDtypeStruct((3,) + hb.shape[1:], bf16),
                   jax.ShapeDtypeStruct((N_DEV, SMALL_ROWS, D_MODEL), f32)],
        scratch_shapes=[pltpu.SemaphoreType.DMA((13,)), pltpu.SemaphoreType.DMA((13,)), pltpu.SemaphoreType.DMA(())],
    )(hb, small)


def _final_sum(idx, gslab, recv_sib, recv_ici, name):
    half, C = recv_sib.shape[1:]
    tr = half // 2 if (half // 2) % 16 == 0 else half

    def body(i_ref, g_ref, r_ref, p_ref, o_ref):
        acc = g_ref[0] + r_ref[0]
        for k in range(3):
            acc = acc + p_ref[k].astype(f32)
        o_ref[...] = acc

    return _pallas(
        body, name=name,
        grid_spec=pltpu.PrefetchScalarGridSpec(
            num_scalar_prefetch=1, grid=(half // tr,),
            in_specs=[pl.BlockSpec((1, tr, C), lambda i, s: (s[1], s[0] * (half // tr) + i, 0)),
                      pl.BlockSpec((1, tr, C), lambda i, s: (s[1], i, 0)),
                      pl.BlockSpec((3, tr, C), lambda i, s: (0, i, 0))],
            out_specs=pl.BlockSpec((tr, C), lambda i, s: (s[0] * (half // tr) + i, 0))),
        out_shape=jax.ShapeDtypeStruct((2 * half, C), f32),
        compiler_params=_cparams(("parallel",), VMEM_MID),
    )(idx, gslab, recv_sib, recv_ici)


def _join_halves(ra, rb):
    ha, hb = ra.shape[0] // 2, rb.shape[0] // 2

    def body(a_ref, b_ref, ao_ref, bo_ref, ssem, rsem):
        x, y, c = _place()
        sib = (x, y, 1 - c)
        mine_a, theirs_a = pl.ds(c * ha, ha), pl.ds((1 - c) * ha, ha)
        mine_b, theirs_b = pl.ds(c * hb, hb), pl.ds((1 - c) * hb, hb)
        ca = _remote(a_ref.at[mine_a], ao_ref.at[mine_a], ssem.at[0], rsem.at[0], sib)
        cb = _remote(b_ref.at[mine_b], bo_ref.at[mine_b], ssem.at[1], rsem.at[1], sib)
        ca.start()
        cb.start()
        _remote(a_ref.at[theirs_a], ao_ref.at[theirs_a], ssem.at[0], rsem.at[0], sib).wait_recv()
        _remote(b_ref.at[theirs_b], bo_ref.at[theirs_b], ssem.at[1], rsem.at[1], sib).wait_recv()
        ca.wait_send()
        cb.wait_send()

    return _pallas(
        body, name="join_halves", in_specs=[ANY, ANY], out_specs=[ANY, ANY],
        out_shape=[jax.ShapeDtypeStruct(ra.shape, f32), jax.ShapeDtypeStruct(rb.shape, f32)],
        input_output_aliases={0: 0, 1: 1},
        scratch_shapes=[pltpu.SemaphoreType.DMA((2,)), pltpu.SemaphoreType.DMA((2,))],
    )(ra, rb)


def _shard_rows(gt):
    def body(g_ref, o_ref):
        for b in range(N_CHIPS):
            o_ref[b, 0:W_IN_ROWS, :] = g_ref[b * W_IN_ROWS:(b + 1) * W_IN_ROWS, :]
            o_ref[b, W_IN_ROWS:W_IN_ROWS_PAD, :] = jnp.zeros((W_IN_ROWS_PAD - W_IN_ROWS, LANES), f32)

    return _pallas(
        body, name="shard_rows", grid=(D_MODEL // LANES,),
        in_specs=[pl.BlockSpec((IN_WIDTH, LANES), lambda i: (0, i))],
        out_specs=pl.BlockSpec((N_CHIPS, W_IN_ROWS_PAD, LANES), lambda i: (0, 0, i)),
        out_shape=jax.ShapeDtypeStruct((N_CHIPS, W_IN_ROWS_PAD, D_MODEL), f32),
        compiler_params=_cparams(("parallel",), VMEM_MID),
    )(gt)


def _sum_small(all_small):
    def body(a_ref, o_ref):
        acc = a_ref[0]
        for d in range(1, N_DEV):
            acc = acc + a_ref[d]
        o_ref[...] = acc

    return _pallas(
        body, name="sum_small", out_shape=jax.ShapeDtypeStruct((SMALL_ROWS, D_MODEL), f32),
    )(all_small)


def _adamw(w, g, m, v, name, g_off=0, by_columns=False):
    R, C = w.shape
    tr = 256 if R % 256 == 0 else R
    assert g_off % tr == 0 and not (by_columns and g_off)
    c1 = 1.0 - ADAM_B1 ** ADAM_STEP
    c2 = 1.0 - ADAM_B2 ** ADAM_STEP

    def body(w_ref, g_ref, m_ref, v_ref, d_ref, mo_ref, vo_ref):
        gg = g_ref[...]
        m2 = ADAM_B1 * m_ref[...] + (1.0 - ADAM_B1) * gg
        v2 = ADAM_B2 * v_ref[...] + (1.0 - ADAM_B2) * (gg * gg)
        mo_ref[...] = m2
        vo_ref[...] = v2
        d_ref[...] = -ADAM_LR * ((m2 / c1) / (jnp.sqrt(v2 / c2) + ADAM_EPS) + ADAM_WD * w_ref[...])

    if by_columns:
        blk = gblk = pl.BlockSpec((R, LANES), lambda i: (0, i))
        grid = (C // LANES,)
    else:
        blk = pl.BlockSpec((tr, C), lambda i: (i, 0))
        gblk = pl.BlockSpec((tr, C), lambda i: (g_off // tr + i, 0))
        grid = (R // tr,)
    shp = jax.ShapeDtypeStruct((R, C), f32)
    return _pallas(
        body, name=name, grid=grid, in_specs=[blk, gblk, blk, blk], out_specs=[blk] * 3, out_shape=[shp] * 3,
        compiler_params=_cparams(("parallel",), VMEM_MID),
    )(w, g, m, v)


def _pad_lanes(v):
    return jnp.pad(v, ((0, 0), (0, LANES - v.shape[1])))


def _local_step(x, p, tgt, gath0, cidx, gin, S):
    dtb = _pad_lanes(S["dt_bias"])
    alog = _pad_lanes(S["A_log"])
    dskip_e = jnp.repeat(S["D_skip"], HEAD_DIM, axis=1)

    early = GATHER_EARLY_ROWS
    u0, z, xbc, cv, cg, dtr, v, gath1 = _in_proj_fwd(x, S["mix_norm_g"], gin, rider=_gather_rider(gath0, 0, early))
    co, yc, gath = _conf_fwd(v, S["conf_dw_w"], S["conf_dw_b"], S["conf_ln_g"], S["conf_ln_b"],
                             rider=_gather_rider(gath1, early, SLAB_A // 2 - early))
    gath = _forward_to_sibling(gath)
    w_ple = jnp.concatenate([_ple_of_slab(gath[b]) for b in range(N_CHIPS)], axis=1)
    pre = _ssd_conv_fwd(xbc, S["ssd_conv_w"], S["ssd_conv_b"])
    y, ys, sprev = _ssd_fwd(pre, dtr, z, dtb, alog, dskip_e, S["ssd_norm_g"])
    h1, u1 = _out_proj_fwd(x, ys, yc, gath, S["mlp_norm_g"])
    r, h2, u2 = _mlp_fwd(h1, u1, gath, S["ple_gate_norm_g"])
    loss, dh2, dh2b, dgp, dep, dg_fin, dg_ple, db_pg, dg_pg = _ple_loss(
        h2, u2, p, tgt, gath, S["b_ple_gate"], w_ple, S["ple_norm_g"], S["final_norm_g"], S["ple_gate_norm_g"])

    npg = D_MODEL // N_CHIPS
    ga = lax.empty((N_CHIPS, SLAB_A, D_MODEL), f32)
    ga = _weight_grad(u2, dgp, "dw_ple_gate", slab=ga, tk=npg, place=lambda i, j: (i, PG_OFF // npg, j))
    ga = _weight_grad(r, dh2b, "dw_down", square=True, slab=ga, place=lambda i, j: (i // 2, DOWN_OFF // 512 + i % 2, j))
    gw_ple = _weight_grad(p, dep, "dw_ple")
    dhp, dh1, dh1b, dg_mlp = _mlp_bwd(dh2, r, gath, h1, S["mlp_norm_g"])
    ga = _weight_grad(u1, dhp, "dw_up", slab=ga, place=lambda i, j: (j, UP_OFF // 512 + i, 0))
    ga = _weight_grad(ys, dh1b, "dw_out_ssd", slab=ga, place=lambda i, j: (i, OUT_OFF // 512, j))
    ga = _weight_grad(yc, dh1b, "dw_out_conf", slab=ga, place=lambda i, j: (2 + i, OUT_OFF // 512, j))
    n_ple = D_MODEL // N_CHIPS
    ple_rows = jnp.stack([_rows(gw_ple[:, b * n_ple:(b + 1) * n_ple]) for b in range(N_CHIPS)], axis=0)
    ga = lax.dynamic_update_slice(ga, ple_rows, (0, PLE_OFF, 0))
    dys, dco, dg_ln, db_ln, recv_a = _out_proj_bwd(dh1, gath, co, S["conf_ln_g"], S["conf_ln_b"], rider=_swap_rider(ga))
    ha = _chip_sum(cidx, ga, recv_a, "chip_sum_a")
    dcv, dcg, dw_conf, db_conf, ici_a = _conf_conv_bwd(dco, v, S["conf_dw_w"], cv, cg, rider=_exchange_rider(ha))
    dz, dpre, ddtr, dg_ssdn, dd, dal, ddtb = _ssd_bwd(dys, y, z, pre, dtr, sprev, dtb, alog, dskip_e, S["ssd_norm_g"])
    dxbc, dw_sconv, db_sconv = _ssd_conv_bwd(dpre, xbc, S["ssd_conv_w"])
    gx, dg_mix = _in_proj_bwd(dz, dxbc, dcv, dcg, ddtr, gin, x, dh1, S["mix_norm_g"])

    gw_in = jnp.concatenate([
        _weight_grad(dz, u0, "dw_in_z"), _weight_grad(dxbc, u0, "dw_in_xbc"),
        _weight_grad(ddtr, u0, "dw_in_dt")[:SSD_HEADS],
        _weight_grad(dcv, u0, "dw_in_cv"), _weight_grad(dcg, u0, "dw_in_cg")], axis=0)
    small = {
        "mix_norm_g": dg_mix, "ssd_conv_w": dw_sconv, "ssd_conv_b": db_sconv, "dt_bias": ddtb, "A_log": dal, "D_skip": dd,
        "ssd_norm_g": dg_ssdn, "conf_dw_w": dw_conf, "conf_dw_b": db_conf, "conf_ln_g": dg_ln, "conf_ln_b": db_ln,
        "mlp_norm_g": dg_mlp, "ple_gate_norm_g": dg_pg, "b_ple_gate": db_pg, "ple_norm_g": dg_ple,
        "final_norm_g": dg_fin, "loss": loss,
    }
    return gx, ga, recv_a, ici_a, gw_in, small


def _rows(a):
    return a.reshape(-1, D_MODEL)


def _pad_rows(a, n):
    flat = a.reshape(-1)
    return jnp.pad(flat, (0, n * D_MODEL - flat.shape[0])).reshape(n, D_MODEL)


def _ple_of_slab(slab):
    return slab[PLE_OFF:PLE_OFF + PLE_ROWS].reshape(PLE_DIM, D_MODEL // N_CHIPS)


ROW_VEC = {"mix_norm_g": 0, "ssd_norm_g": 1, "conf_dw_b": 2, "conf_ln_g": 3, "conf_ln_b": 4, "mlp_norm_g": 5,
           "ple_gate_norm_g": 6, "b_ple_gate": 7, "ple_norm_g": 8, "final_norm_g": 9}
ROW_CONV_B = 10
ROW_HEADS = 12
ROW_CONV_W = 16
ROW_DW = 24
HEAD_LANES = {"dt_bias": 0, "A_log": 1, "D_skip": 2, "loss": 3}
SMALL_ORDER = ("mix_norm_g", "ssd_conv_w", "ssd_conv_b", "dt_bias", "A_log", "D_skip", "ssd_norm_g", "conf_dw_w",
               "conf_dw_b", "conf_ln_g", "conf_ln_b", "mlp_norm_g", "ple_gate_norm_g", "b_ple_gate", "ple_norm_g",
               "final_norm_g")
SPLIT = XBC_WIDTH - D_MODEL


def _pack_small(raw):
    names = list(ROW_VEC) + ["ssd_conv_b", "dt_bias", "A_log", "D_skip", "loss", "ssd_conv_w", "conf_dw_w"]

    def body(*refs):
        r = dict(zip(names, refs[:-1]))
        o_ref = refs[-1]
        o_ref[...] = jnp.zeros_like(o_ref)
        for n, row in ROW_VEC.items():
            o_ref[row:row + 1, :] = r[n][...]
        o_ref[ROW_CONV_B:ROW_CONV_B + 1, :] = r["ssd_conv_b"][:, 0:D_MODEL]
        o_ref[ROW_CONV_B + 1:ROW_CONV_B + 2, 0:SPLIT] = r["ssd_conv_b"][:, D_MODEL:]
        for n, j in HEAD_LANES.items():
            o_ref[ROW_HEADS:ROW_HEADS + 1, j * LANES:(j + 1) * LANES] = r[n][0:1, :]
        for k in range(SSD_CONV):
            o_ref[ROW_CONV_W + 2 * k:ROW_CONV_W + 2 * k + 1, :] = r["ssd_conv_w"][k:k + 1, 0:D_MODEL]
            o_ref[ROW_CONV_W + 2 * k + 1:ROW_CONV_W + 2 * k + 2, 0:SPLIT] = r["ssd_conv_w"][k:k + 1, D_MODEL:]
        o_ref[ROW_DW:ROW_DW + 32, :] = r["conf_dw_w"][...]

    return _pallas(
        body, name="pack_small", out_shape=jax.ShapeDtypeStruct((SMALL_ROWS, D_MODEL), f32),
    )(*[raw[n] for n in names])


def _adamw_small(cidx, tot, w, m, v):
    c1 = 1.0 - ADAM_B1 ** ADAM_STEP
    c2 = 1.0 - ADAM_B2 ** ADAM_STEP
    n_par = len(SMALL_ORDER)

    def shard(full, chip, width):
        out = full[:, 0:width]
        for b in range(1, N_CHIPS):
            out = jnp.where(chip == b, full[:, b * width:(b + 1) * width], out)
        return out

    def grad_of(n, t_ref, chip):
        if n in ROW_VEC:
            return t_ref[ROW_VEC[n]:ROW_VEC[n] + 1, :]
        if n == "ssd_conv_b":
            return jnp.concatenate([t_ref[ROW_CONV_B:ROW_CONV_B + 1, :], t_ref[ROW_CONV_B + 1:ROW_CONV_B + 2, 0:SPLIT]], axis=1)
        if n in HEAD_LANES:
            j = HEAD_LANES[n]
            return t_ref[ROW_HEADS:ROW_HEADS + 1, j * LANES:j * LANES + SSD_HEADS]
        if n == "ssd_conv_w":
            rows = [jnp.concatenate([t_ref[ROW_CONV_W + 2 * k:ROW_CONV_W + 2 * k + 1, :],
                                     t_ref[ROW_CONV_W + 2 * k + 1:ROW_CONV_W + 2 * k + 2, 0:SPLIT]], axis=1)
                    for k in range(SSD_CONV)]
            return shard(jnp.concatenate(rows, axis=0), chip, XBC_WIDTH // N_CHIPS)
        return shard(t_ref[ROW_DW:ROW_DW + CONF_KERNEL, :], chip, CONF_WIDTH // N_CHIPS)

    def body(c_ref, t_ref, *refs):
        ins, outs = refs[:3 * n_par], refs[3 * n_par:]
        chip = c_ref[1]
        for i, n in enumerate(SMALL_ORDER):
            w_ref, m_ref, v_ref = ins[3 * i:3 * i + 3]
            g_ref, d_ref, mo_ref, vo_ref = outs[4 * i:4 * i + 4]
            g = grad_of(n, t_ref, chip)
            m2 = ADAM_B1 * m_ref[...] + (1.0 - ADAM_B1) * g
            v2 = ADAM_B2 * v_ref[...] + (1.0 - ADAM_B2) * (g * g)
            g_ref[...] = g
            mo_ref[...] = m2
            vo_ref[...] = v2
            d_ref[...] = -ADAM_LR * ((m2 / c1) / (jnp.sqrt(v2 / c2) + ADAM_EPS) + ADAM_WD * w_ref[...])

    args, in_specs, out_specs, out_shape = [], [], [], []
    for n in SMALL_ORDER:
        shp = w[n].shape
        spec = pl.BlockSpec(shp, lambda i, c_ref: (0, 0))
        args += [w[n], m[n], v[n]]
        in_specs += [spec] * 3
        out_specs += [spec] * 4
        out_shape += [jax.ShapeDtypeStruct(shp, f32)] * 4
    outs = _pallas(
        body, name="adamw_small",
        grid_spec=pltpu.PrefetchScalarGridSpec(
            num_scalar_prefetch=1, grid=(1,),
            in_specs=[pl.BlockSpec(tot.shape, lambda i, c_ref: (0, 0))] + in_specs, out_specs=out_specs),
        out_shape=out_shape,
    )(cidx, tot, *args)
    grad, delta, new_m, new_v = {}, {}, {}, {}
    for i, n in enumerate(SMALL_ORDER):
        grad[n], delta[n], new_m[n], new_v[n] = outs[4 * i:4 * i + 4]
    return grad, delta, new_m, new_v


BIG = ("w_in", "w_out", "w_up", "w_down", "w_ple_gate", "w_ple")
BIG_A = (("w_up", UP_OFF), ("w_down", DOWN_OFF), ("w_out", OUT_OFF), ("w_ple_gate", PG_OFF))
WEIGHTS = ("mix_norm_g", "w_in", "ssd_conv_w", "ssd_conv_b", "dt_bias", "A_log", "D_skip", "ssd_norm_g", "conf_dw_w",
           "conf_dw_b", "conf_ln_g", "conf_ln_b", "w_out", "mlp_norm_g", "w_up", "w_down", "ple_gate_norm_g",
           "w_ple_gate", "b_ple_gate", "w_ple", "ple_norm_g", "final_norm_g")


def kernel(x, p, mix_norm_g, w_in, ssd_conv_w, ssd_conv_b, dt_bias, A_log, D_skip, ssd_norm_g, conf_dw_w, conf_dw_b, conf_ln_g, conf_ln_b, w_out, mlp_norm_g, w_up, w_down, ple_gate_norm_g, w_ple_gate, b_ple_gate, w_ple, ple_norm_g, final_norm_g, loss_target, m_mix_norm_g, m_w_in, m_ssd_conv_w, m_ssd_conv_b, m_dt_bias, m_A_log, m_D_skip, m_ssd_norm_g, m_conf_dw_w, m_conf_dw_b, m_conf_ln_g, m_conf_ln_b, m_w_out, m_mlp_norm_g, m_w_up, m_w_down, m_ple_gate_norm_g, m_w_ple_gate, m_b_ple_gate, m_w_ple, m_ple_norm_g, m_final_norm_g, v_mix_norm_g, v_w_in, v_ssd_conv_w, v_ssd_conv_b, v_dt_bias, v_A_log, v_D_skip, v_ssd_norm_g, v_conf_dw_w, v_conf_dw_b, v_conf_ln_g, v_conf_ln_b, v_w_out, v_mlp_norm_g, v_w_up, v_w_down, v_ple_gate_norm_g, v_w_ple_gate, v_b_ple_gate, v_w_ple, v_ple_norm_g, v_final_norm_g):
    w = dict(mix_norm_g=mix_norm_g, w_in=w_in, ssd_conv_w=ssd_conv_w, ssd_conv_b=ssd_conv_b, dt_bias=dt_bias, A_log=A_log,
             D_skip=D_skip, ssd_norm_g=ssd_norm_g, conf_dw_w=conf_dw_w, conf_dw_b=conf_dw_b, conf_ln_g=conf_ln_g,
             conf_ln_b=conf_ln_b, w_out=w_out, mlp_norm_g=mlp_norm_g, w_up=w_up, w_down=w_down,
             ple_gate_norm_g=ple_gate_norm_g, w_ple_gate=w_ple_gate, b_ple_gate=b_ple_gate, w_ple=w_ple,
             ple_norm_g=ple_norm_g, final_norm_g=final_norm_g)
    m = dict(mix_norm_g=m_mix_norm_g, w_in=m_w_in, ssd_conv_w=m_ssd_conv_w, ssd_conv_b=m_ssd_conv_b, dt_bias=m_dt_bias,
             A_log=m_A_log, D_skip=m_D_skip, ssd_norm_g=m_ssd_norm_g, conf_dw_w=m_conf_dw_w, conf_dw_b=m_conf_dw_b,
             conf_ln_g=m_conf_ln_g, conf_ln_b=m_conf_ln_b, w_out=m_w_out, mlp_norm_g=m_mlp_norm_g, w_up=m_w_up,
             w_down=m_w_down, ple_gate_norm_g=m_ple_gate_norm_g, w_ple_gate=m_w_ple_gate, b_ple_gate=m_b_ple_gate,
             w_ple=m_w_ple, ple_norm_g=m_ple_norm_g, final_norm_g=m_final_norm_g)
    v = dict(mix_norm_g=v_mix_norm_g, w_in=v_w_in, ssd_conv_w=v_ssd_conv_w, ssd_conv_b=v_ssd_conv_b, dt_bias=v_dt_bias,
             A_log=v_A_log, D_skip=v_D_skip, ssd_norm_g=v_ssd_norm_g, conf_dw_w=v_conf_dw_w, conf_dw_b=v_conf_dw_b,
             conf_ln_g=v_conf_ln_g, conf_ln_b=v_conf_ln_b, w_out=v_w_out, mlp_norm_g=v_mlp_norm_g, w_up=v_w_up,
             w_down=v_w_down, ple_gate_norm_g=v_ple_gate_norm_g, w_ple_gate=v_w_ple_gate, b_ple_gate=v_b_ple_gate,
             w_ple=v_w_ple, ple_norm_g=v_ple_norm_g, final_norm_g=v_final_norm_g)
    xi, yi, ci = lax.axis_index("x"), lax.axis_index("y"), lax.axis_index("c")
    chip = 2 * xi + yi

    slab = jnp.concatenate([w_up[0], w_down[0], w_out[0], w_ple_gate[0], _rows(w_ple[0])], axis=0).astype(bf16)
    gath0 = lax.dynamic_update_slice(jnp.zeros((N_CHIPS, SLAB_A, D_MODEL), bf16), slab[None], (chip, 0, 0))
    wt_shard = jnp.swapaxes(w_in, 1, 2).astype(bf16)
    gin0 = lax.dynamic_update_slice(jnp.zeros((N_CHIPS, W_IN_ROWS_PAD, D_MODEL), bf16), wt_shard, (chip, 0, 0))
    convw = _pad_rows(jnp.concatenate([ssd_conv_w[0].reshape(-1), conf_dw_w[0].reshape(-1)]), CONVW_ROWS)
    gin, cwg = _gather_weights([gin0], convw)
    n_sc = SSD_CONV * (XBC_WIDTH // N_CHIPS)
    n_cf = CONF_KERNEL * (CONF_WIDTH // N_CHIPS)
    S = {n: w[n][0] for n in ("mix_norm_g", "ssd_conv_b", "dt_bias", "A_log", "D_skip", "ssd_norm_g", "conf_dw_b",
                              "conf_ln_g", "conf_ln_b", "mlp_norm_g", "ple_gate_norm_g", "b_ple_gate", "ple_norm_g")}
    S = {n: a.reshape(1, -1) for n, a in S.items()}
    S["final_norm_g"] = final_norm_g.reshape(1, -1)
    S["ssd_conv_w"] = jnp.concatenate(
        [cwg[b].reshape(-1)[:n_sc].reshape(SSD_CONV, XBC_WIDTH // N_CHIPS) for b in range(N_CHIPS)], axis=1)
    S["conf_dw_w"] = jnp.concatenate(
        [cwg[b].reshape(-1)[n_sc:n_sc + n_cf].reshape(CONF_KERNEL, CONF_WIDTH // N_CHIPS) for b in range(N_CHIPS)], axis=1)

    cidx = jnp.stack([ci, chip]).astype(jnp.int32)
    grad_x, ga, recv_a, ici_a, gw_in, gsmall = _local_step(x[0], p[0, 0], loss_target[0], gath0, cidx, gin, S)

    gb = _shard_rows(gw_in)
    small = _pack_small(gsmall)
    recv_b = _swap_halves(gb)
    hb = _chip_sum(cidx, gb, recv_b, "chip_sum_b")
    ici_b, all_small = _exchange(hb, small)
    ra = _final_sum(cidx, ga, recv_a, ici_a, "final_sum_a")
    rb = _final_sum(cidx, gb, recv_b, ici_b, "final_sum_b")
    ra, rb = _join_halves(ra, rb)
    tot_small = _sum_small(all_small)

    loss = tot_small[ROW_HEADS, HEAD_LANES["loss"] * LANES]

    two_d = lambda a: a.reshape(a.shape[-2:]) if a.ndim > 1 else a.reshape(1, -1)
    small_w, small_m, small_v = ({n: two_d(d[n]) for n in SMALL_ORDER} for d in (w, m, v))
    grads, delta, new_m, new_v = _adamw_small(cidx, tot_small, small_w, small_m, small_v)
    g_in_t = rb[:W_IN_ROWS]
    grads["w_ple"] = _ple_of_slab(ra)
    grads["w_in"] = jnp.swapaxes(g_in_t, 0, 1)
    for n, off in BIG_A:
        grads[n] = ra[off:off + w[n].shape[1]]
        delta[n], new_m[n], new_v[n] = _adamw(w[n][0], ra, m[n][0], v[n][0], "adamw_" + n, g_off=off)
    delta["w_ple"], new_m["w_ple"], new_v["w_ple"] = _adamw(w_ple[0], grads["w_ple"], m_w_ple[0], v_w_ple[0], "adamw_w_ple")
    tr_ = lambda a: jnp.swapaxes(a[0], 0, 1)
    d_, m_, v_ = _adamw(tr_(w_in), g_in_t, tr_(m_w_in), tr_(v_w_in), "adamw_w_in", by_columns=True)
    delta["w_in"], new_m["w_in"], new_v["w_in"] = (jnp.swapaxes(a, 0, 1) for a in (d_, m_, v_))

    shaped = lambda d: [d[n].reshape(w[n].shape) for n in WEIGHTS]
    return (loss, grad_x[None], *shaped(grads), *shaped(delta), *shaped(new_m), *shaped(new_v))
```

```python
import jax
import jax.numpy as jnp
from jax import lax
from jax.experimental import pallas as pl
from jax.experimental.pallas import tpu as pltpu

f32 = jnp.float32
bf16 = jnp.bfloat16

D_MODEL = 1024
SSD_WIDTH = 1024
SSD_HEADS = 16
HEAD_DIM = 64
SSD_STATE = 128
XBC_WIDTH = 1536
SSD_CONV = 4
CHUNK = 128
CONF_WIDTH = 1024
CONF_KERNEL = 31
D_FF = 4096
PLE_DIM = 256
IN_WIDTH = 4624
EPS = 1e-6
N_CHIPS = 4
N_DEV = 8

ADAM_LR = 0.001
ADAM_B1 = 0.9
ADAM_B2 = 0.999
ADAM_EPS = 1e-08
ADAM_WD = 0.01
ADAM_STEP = 10

LANES = 128
VMEM_BIG = 56 * 1024 * 1024
VMEM_MID = 40 * 1024 * 1024

UP_OFF, DOWN_OFF, OUT_OFF, PG_OFF, PLE_OFF = 0, 1024, 2048, 2560, 2816
PLE_ROWS = 64
SLAB_A = PLE_OFF + PLE_ROWS
GATHER_EARLY_ROWS = 480
W_IN_ROWS = 1156
W_IN_ROWS_PAD = 1184
CONVW_ROWS = 16
SMALL_ROWS = 56

MESH = pl.DeviceIdType.MESH
ANY = pl.BlockSpec(memory_space=pl.ANY)


PIN_BYTES = 256 * 1024


def _pallas(body, **kw):
    call = pl.pallas_call(body, **kw)

    def pin(a):
        small = a.size * a.dtype.itemsize <= PIN_BYTES
        return pltpu.with_memory_space_constraint(a, pltpu.HBM) if small and a.dtype != jnp.int32 else a

    def run(*args):
        return call(*[pin(a) for a in args])

    return run


def _cparams(sem=None, vmem=None):
    return pltpu.CompilerParams(dimension_semantics=sem, vmem_limit_bytes=vmem)


def _full(shape):
    n = len(shape)
    return pl.BlockSpec(shape, lambda *_: (0,) * n)


class _Rider:
    def __init__(self, inputs, out_shapes, aliases, n_sems, start, finish):
        self.inputs, self.out_shapes, self.aliases = list(inputs), list(out_shapes), dict(aliases)
        self.n_sems, self.start, self.finish = n_sems, start, finish


def _call(body, args, *, name, grid, in_specs, out_specs, out_shape, scratch_shapes=(), params=None, rider=None):
    if rider is None:
        return _pallas(body, name=name, grid=grid, in_specs=in_specs, out_specs=out_specs, out_shape=out_shape,
                              scratch_shapes=list(scratch_shapes), compiler_params=params)(*args)
    ni, no, ns = len(in_specs), len(out_specs), len(scratch_shapes)
    ri, ro = len(rider.inputs), len(rider.out_shapes)
    (steps,) = grid

    def with_rider(*refs):
        ins, refs = refs[:ni], refs[ni:]
        rins, refs = refs[:ri], refs[ri:]
        outs, refs = refs[:no], refs[no:]
        routs, refs = refs[:ro], refs[ro:]
        scratch, (ssem, rsem) = refs[:ns], refs[ns:]
        step = pl.program_id(0)

        @pl.when(step == 0)
        def _():
            rider.start(rins, routs, ssem, rsem)

        body(*ins, *outs, *scratch)

        @pl.when(step == steps - 1)
        def _():
            rider.finish(rins, routs, ssem, rsem)

    sems = [pltpu.SemaphoreType.DMA((rider.n_sems,)), pltpu.SemaphoreType.DMA((rider.n_sems,))]
    return _pallas(
        with_rider, name=name, grid=grid, in_specs=list(in_specs) + [ANY] * ri, out_specs=list(out_specs) + [ANY] * ro,
        out_shape=list(out_shape) + rider.out_shapes, scratch_shapes=list(scratch_shapes) + sems,
        input_output_aliases={ni + a: no + b for a, b in rider.aliases.items()}, compiler_params=params,
    )(*args, *rider.inputs)


def _dot(a, b):
    return jnp.dot(a, b, preferred_element_type=f32)


def _dot_nt(a, b):
    return lax.dot_general(a, b, (((1,), (1,)), ((), ())), preferred_element_type=f32)


def _dot_tn(a, b):
    return lax.dot_general(a, b, (((0,), (0,)), ((), ())), preferred_element_type=f32)


def _sigmoid(x):
    return jax.nn.sigmoid(x)


def _rms(x, g):
    r = lax.rsqrt(jnp.mean(x * x, axis=-1, keepdims=True) + EPS)
    return x * r * g


def _rms_bwd(dy, x, g):
    r = lax.rsqrt(jnp.mean(x * x, axis=-1, keepdims=True) + EPS)
    xh = x * r
    dg = jnp.sum(dy * xh, axis=0, keepdims=True)
    dxh = dy * g
    dx = r * (dxh - xh * jnp.mean(dxh * xh, axis=-1, keepdims=True))
    return dx, dg


def _dsilu(x):
    s = _sigmoid(x)
    return s * (1.0 + x * (1.0 - s))


def _split3(x):
    hi = x.astype(bf16)
    r1 = x - hi.astype(f32)
    mid = r1.astype(bf16)
    lo = (r1 - mid.astype(f32)).astype(bf16)
    return hi, mid, lo


def _head_matrix():
    row = lax.broadcasted_iota(jnp.int32, (LANES, SSD_WIDTH), 0)
    col = lax.broadcasted_iota(jnp.int32, (LANES, SSD_WIDTH), 1)
    lo = row * HEAD_DIM
    return ((col >= lo) & (col < lo + HEAD_DIM)).astype(bf16)


def _expand(x, e):
    hi, mid, lo = _split3(x)
    return _dot(hi, e) + _dot(mid, e) + _dot(lo, e)


def _contract(x, e):
    hi = x.astype(bf16)
    mid = (x - hi.astype(f32)).astype(bf16)
    return _dot_nt(hi, e) + _dot_nt(mid, e)


O_XBC = SSD_WIDTH
O_DT = O_XBC + XBC_WIDTH
O_CV = O_DT + SSD_HEADS
O_CG = O_CV + CONF_WIDTH


def _assemble_w_in_t(gin_ref, wt_ref):
    for b in range(N_CHIPS):
        wt_ref[b * W_IN_ROWS:(b + 1) * W_IN_ROWS, :] = gin_ref[b, 0:W_IN_ROWS, :]


def _in_proj_fwd(x, g, gin, rider=None):
    T = x.shape[0]
    tm = min(256, T)

    def body(x_ref, g_ref, gin_ref, u_ref, z_ref, xbc_ref, cv_ref, cg_ref, dt_ref, v_ref, wt_ref):
        @pl.when(pl.program_id(0) == 0)
        def _():
            _assemble_w_in_t(gin_ref, wt_ref)

        ub = _rms(x_ref[...], g_ref[...]).astype(bf16)
        u_ref[...] = ub
        z_ref[...] = _dot_nt(ub, wt_ref[0:O_XBC, :])
        xbc_ref[...] = _dot_nt(ub, wt_ref[O_XBC:O_DT, :])
        cv = _dot_nt(ub, wt_ref[O_CV:O_CG, :])
        cg = _dot_nt(ub, wt_ref[O_CG:IN_WIDTH, :])
        cv_ref[...] = cv
        cg_ref[...] = cg
        v_ref[...] = cv * _sigmoid(cg)
        dt_ref[...] = _dot_nt(ub, wt_ref[O_DT:O_DT + LANES, :])

    row = lambda n: pl.BlockSpec((tm, n), lambda i: (i, 0))
    return _call(
        body, (x, g, gin), name="in_proj_fwd", grid=(T // tm,),
        in_specs=[row(D_MODEL), _full((1, D_MODEL)), _full(gin.shape)],
        out_specs=[row(D_MODEL), row(SSD_WIDTH), row(XBC_WIDTH), row(CONF_WIDTH), row(CONF_WIDTH), row(LANES),
                   row(CONF_WIDTH)],
        out_shape=[jax.ShapeDtypeStruct((T, D_MODEL), bf16), jax.ShapeDtypeStruct((T, SSD_WIDTH), f32),
                   jax.ShapeDtypeStruct((T, XBC_WIDTH), f32), jax.ShapeDtypeStruct((T, CONF_WIDTH), f32),
                   jax.ShapeDtypeStruct((T, CONF_WIDTH), f32), jax.ShapeDtypeStruct((T, LANES), f32),
                   jax.ShapeDtypeStruct((T, CONF_WIDTH), f32)],
        scratch_shapes=[pltpu.VMEM((IN_WIDTH, D_MODEL), bf16)],
        params=_cparams(("arbitrary",), VMEM_BIG), rider=rider)


SUBLANES = 8


def _phases(offsets):
    return sorted({o % SUBLANES for o in offsets} - {0})


def _phase_shape(offsets, tm, C):
    a_max = max([o // SUBLANES for o in offsets if o % SUBLANES] or [0])
    return (max(len(_phases(offsets)), 1), tm + SUBLANES * a_max, C)


def _make_phases(buf_ref, ph_ref, offsets, tm):
    for idx, b in enumerate(_phases(offsets)):
        n = tm + SUBLANES * max(o // SUBLANES for o in offsets if o % SUBLANES == b)
        ph_ref[idx, 0:n, :] = buf_ref[pl.ds(b, n), :]


def _window(buf_ref, ph_ref, offsets, o, r0, rb):
    a, b = divmod(o, SUBLANES)
    if b == 0:
        return buf_ref[pl.ds(r0 + SUBLANES * a, rb), :]
    return ph_ref[_phases(offsets).index(b), pl.ds(r0 + SUBLANES * a, rb), :]


def _conv_rows(wb_ref, buf_ref, ph_ref, offsets, r0, rb):
    nsub = rb // SUBLANES
    accs = [None] * nsub
    for k, o in enumerate(offsets):
        wk = wb_ref[pl.ds(SUBLANES * k, SUBLANES), :]
        for s in range(nsub):
            term = wk * _window(buf_ref, ph_ref, offsets, o, r0 + SUBLANES * s, SUBLANES)
            accs[s] = term if accs[s] is None else accs[s] + term
    return accs[0] if nsub == 1 else jnp.concatenate(accs, axis=0)


def _sublane_rows(w):
    return jnp.repeat(w, SUBLANES, axis=0)


def _fwd_offsets(K, hb):
    return [hb - (K - 1) + k for k in range(K)]


def _prev_halo_spec(hb, tm, C):
    return pl.BlockSpec((hb, C), lambda i: (jnp.maximum(i * (tm // hb) - 1, 0), 0))


CONV_RB = 16


def _ssd_conv_fwd(xbc, w, b):
    T, C = xbc.shape
    K, hb = SSD_CONV, 8
    tm = min(256, T)
    offs = _fwd_offsets(K, hb)

    def body(cur_ref, halo_ref, w_ref, b_ref, pre_ref, buf_ref, ph_ref):
        keep = jnp.where(pl.program_id(0) > 0, 1.0, 0.0)
        buf_ref[0:hb, :] = halo_ref[...] * keep
        buf_ref[hb:hb + tm, :] = cur_ref[...]
        _make_phases(buf_ref, ph_ref, offs, tm)

        def chunk(i, carry):
            r0 = pl.multiple_of(i * CONV_RB, CONV_RB)
            pre_ref[pl.ds(r0, CONV_RB), :] = _conv_rows(w_ref, buf_ref, ph_ref, offs, r0, CONV_RB) + b_ref[...]
            return carry

        lax.fori_loop(0, tm // CONV_RB, chunk, 0)

    return _pallas(
        body, name="ssd_conv_fwd", grid=(T // tm,),
        in_specs=[pl.BlockSpec((tm, C), lambda i: (i, 0)), _prev_halo_spec(hb, tm, C), _full((SUBLANES * K, C)),
                  _full((1, C))],
        out_specs=pl.BlockSpec((tm, C), lambda i: (i, 0)),
        out_shape=jax.ShapeDtypeStruct((T, C), f32),
        scratch_shapes=[pltpu.VMEM((hb + tm, C), f32), pltpu.VMEM(_phase_shape(offs, tm, C), f32)],
        compiler_params=_cparams(("parallel",), VMEM_MID),
    )(xbc, xbc, _sublane_rows(w), b)


def _conf_fwd(v, w, b, ln_g, ln_b, rider=None):
    T, C = v.shape
    K, hb = CONF_KERNEL, 32
    tm = min(256, T)
    offs = _fwd_offsets(K, hb)
    rb = 2 * CONV_RB

    def body(cur_ref, halo_ref, w_ref, b_ref, g_ref, bb_ref, co_ref, y_ref, buf_ref, ph_ref):
        keep = jnp.where(pl.program_id(0) > 0, 1.0, 0.0)
        buf_ref[0:hb, :] = halo_ref[...] * keep
        buf_ref[hb:hb + tm, :] = cur_ref[...]
        _make_phases(buf_ref, ph_ref, offs, tm)

        def chunk(i, carry):
            r0 = pl.multiple_of(i * rb, rb)
            co = _conv_rows(w_ref, buf_ref, ph_ref, offs, r0, rb) + b_ref[...]
            co_ref[pl.ds(r0, rb), :] = co
            mu = jnp.mean(co, axis=-1, keepdims=True)
            xc = co - mu
            yn = xc * lax.rsqrt(jnp.mean(xc * xc, axis=-1, keepdims=True) + EPS) * g_ref[...] + bb_ref[...]
            y_ref[pl.ds(r0, rb), :] = (yn * _sigmoid(yn)).astype(bf16)
            return carry

        lax.fori_loop(0, tm // rb, chunk, 0)

    return _call(
        body, (v, v, _sublane_rows(w), b, ln_g, ln_b), name="conf_fwd", grid=(T // tm,),
        in_specs=[pl.BlockSpec((tm, C), lambda i: (i, 0)), _prev_halo_spec(hb, tm, C), _full((SUBLANES * K, C)),
                  _full((1, C)), _full((1, C)), _full((1, C))],
        out_specs=[pl.BlockSpec((tm, C), lambda i: (i, 0)), pl.BlockSpec((tm, C), lambda i: (i, 0))],
        out_shape=[jax.ShapeDtypeStruct((T, C), f32), jax.ShapeDtypeStruct((T, C), bf16)],
        scratch_shapes=[pltpu.VMEM((hb + tm, C), f32), pltpu.VMEM(_phase_shape(offs, tm, C), f32)],
        params=_cparams(("arbitrary",), VMEM_MID), rider=rider)


def _ssd_chunk_common(pre, dtr, dtb, alog, e):
    act = pre * _sigmoid(pre)
    xs = act[:, :SSD_WIDTH]
    bm = act[:, SSD_WIDTH:SSD_WIDTH + 2 * SSD_STATE]
    cm = act[:, SSD_WIDTH + 2 * SSD_STATE:]
    row = lax.broadcasted_iota(jnp.int32, (CHUNK, CHUNK), 0)
    col = lax.broadcasted_iota(jnp.int32, (CHUNK, CHUNK), 1)
    tri = row >= col
    dt = jax.nn.softplus(dtr + dtb)
    a_neg = -jnp.exp(alog)
    a = dt * a_neg
    cs = jnp.dot(tri.astype(f32), a, precision=lax.Precision.HIGHEST, preferred_element_type=f32)
    cs_e = _expand(cs, e)
    dt_e = _expand(dt, e)
    csl_e = cs_e[CHUNK - 1:CHUNK, :]
    ecs_e = jnp.exp(cs_e)
    dte_e = jnp.exp(csl_e - cs_e)
    cd_e = jnp.exp(csl_e)
    xc = xs * dt_e
    xd = xc * dte_e
    return dict(xs=xs, bm=bm, cm=cm, tri=tri, dt=dt, a_neg=a_neg, cs=cs, ecs_e=ecs_e, dte_e=dte_e, cd_e=cd_e,
                dt_e=dt_e, xc=xc, xd=xd)


def _group(v, g, width):
    return v[:, g * width:(g + 1) * width]


def _ssd_fwd(pre, dtr, z, dtb, alog, dskip_e, gn):
    T = pre.shape[0]
    nc = T // CHUNK
    GW = SSD_WIDTH // 2

    def body(pre_ref, dtr_ref, z_ref, dtb_ref, alog_ref, de_ref, gn_ref, y_ref, ys_ref, sp_ref, st_ref):
        @pl.when(pl.program_id(0) == 0)
        def _():
            st_ref[...] = jnp.zeros_like(st_ref)

        e = _head_matrix()
        q = _ssd_chunk_common(pre_ref[...], dtr_ref[...], dtb_ref[...], alog_ref[...], e)
        cs, tri, xc, xd = q["cs"], q["tri"], q["xc"], q["xd"]
        cs_t = cs.T
        st = st_ref[...]
        sp_ref[0] = st
        lane = lax.broadcasted_iota(jnp.int32, (1, LANES), 1)
        halves = (lane < HEAD_DIM, lane >= HEAD_DIM)

        g_mat, y_off, s_new = [], [], []
        for g in range(2):
            bg = _group(q["bm"], g, SSD_STATE)
            cg = _group(q["cm"], g, SSD_STATE)
            bgb, cgb = bg.astype(bf16), cg.astype(bf16)
            g_mat.append(_dot_nt(cgb, bgb))
            y_off.append(_dot(cgb, _group(st, g, GW).astype(bf16)))
            s_new.append(_dot(bg.T.astype(bf16), _group(xd, g, GW).astype(bf16)))
        y_off = jnp.concatenate(y_off, axis=1) * q["ecs_e"]
        st_ref[...] = st * q["cd_e"] + jnp.concatenate(s_new, axis=1)

        pairs = []
        for j in range(SSD_HEADS // 2):
            xp = xc[:, j * LANES:(j + 1) * LANES]
            acc = jnp.zeros((CHUNK, LANES), f32)
            for hh in range(2):
                h = 2 * j + hh
                seg = cs[:, h:h + 1] - cs_t[h:h + 1, :]
                lm = jnp.exp(jnp.where(tri, seg, -1e30))
                m = (g_mat[h // 8] * lm).astype(bf16)
                acc = acc + _dot(m, jnp.where(halves[hh], xp, 0.0).astype(bf16))
            pairs.append(acc)
        y = jnp.concatenate(pairs, axis=1) + y_off + q["xs"] * de_ref[...]
        y_ref[...] = y

        zz = z_ref[...]
        v = y * (zz * _sigmoid(zz))
        outs = []
        for g in range(2):
            vg = _group(v, g, GW)
            outs.append(vg * lax.rsqrt(jnp.mean(vg * vg, axis=-1, keepdims=True) + EPS))
        ys_ref[...] = (jnp.concatenate(outs, axis=1) * gn_ref[...]).astype(bf16)

    ch = lambda n: pl.BlockSpec((CHUNK, n), lambda c: (c, 0))
    return _pallas(
        body, name="ssd_fwd", grid=(nc,),
        in_specs=[ch(XBC_WIDTH), ch(LANES), ch(SSD_WIDTH), _full((1, LANES)), _full((1, LANES)), _full((1, SSD_WIDTH)),
                  _full((1, SSD_WIDTH))],
        out_specs=[ch(SSD_WIDTH), ch(SSD_WIDTH), pl.BlockSpec((1, SSD_STATE, SSD_WIDTH), lambda c: (c, 0, 0))],
        out_shape=[jax.ShapeDtypeStruct((T, SSD_WIDTH), f32), jax.ShapeDtypeStruct((T, SSD_WIDTH), bf16),
                   jax.ShapeDtypeStruct((nc, SSD_STATE, SSD_WIDTH), f32)],
        scratch_shapes=[pltpu.VMEM((SSD_STATE, SSD_WIDTH), f32)],
        compiler_params=_cparams(("arbitrary",), VMEM_MID),
    )(pre, dtr, z, dtb, alog, dskip_e, gn)


def _w_out_spec():
    n = 2 * SSD_WIDTH // N_CHIPS
    return pl.BlockSpec((N_CHIPS, n, D_MODEL), lambda *_: (0, OUT_OFF // n, 0))


def _out_proj_fwd(x, ys, yc, gath, g):
    T = x.shape[0]
    tm = min(512, T)
    n = 2 * SSD_WIDTH // N_CHIPS

    def body(x_ref, ys_ref, yc_ref, w_ref, g_ref, h_ref, u_ref):
        h = (x_ref[...] + _dot(ys_ref[:, 0:n], w_ref[0]) + _dot(ys_ref[:, n:], w_ref[1])
             + _dot(yc_ref[:, 0:n], w_ref[2]) + _dot(yc_ref[:, n:], w_ref[3]))
        h_ref[...] = h
        u_ref[...] = _rms(h, g_ref[...]).astype(bf16)

    row = pl.BlockSpec((tm, D_MODEL), lambda i: (i, 0))
    return _pallas(
        body, name="out_proj_fwd", grid=(T // tm,),
        in_specs=[row, row, row, _w_out_spec(), _full((1, D_MODEL))],
        out_specs=[row, row],
        out_shape=[jax.ShapeDtypeStruct((T, D_MODEL), f32), jax.ShapeDtypeStruct((T, D_MODEL), bf16)],
        compiler_params=_cparams(("parallel",), VMEM_MID),
    )(x, ys, yc, gath, g)


def _w_up_spec():
    return pl.BlockSpec((1, D_MODEL, D_MODEL), lambda i, b: (b, UP_OFF // D_MODEL, 0))


def _w_down_spec():
    return pl.BlockSpec((1, D_MODEL, D_MODEL), lambda i, b: (b, DOWN_OFF // D_MODEL, 0))


def _mlp_fwd(h1, u1, gath, g_next):
    T = h1.shape[0]
    tm = min(512, T)
    nb = D_FF // D_MODEL

    def body(h_ref, u_ref, wu_ref, wd_ref, g_ref, r_ref, h2_ref, u2_ref, acc_ref):
        b = pl.program_id(1)

        @pl.when(b == 0)
        def _():
            acc_ref[...] = jnp.zeros_like(acc_ref)

        r = jnp.maximum(_dot(u_ref[...], wu_ref[0]), 0.0)
        r_ref[...] = r.astype(bf16)
        acc_ref[...] += _dot((r * r).astype(bf16), wd_ref[0])

        @pl.when(b == nb - 1)
        def _():
            h2 = h_ref[...] + acc_ref[...]
            h2_ref[...] = h2
            u2_ref[...] = _rms(h2, g_ref[...]).astype(bf16)

    row = pl.BlockSpec((tm, D_MODEL), lambda i, b: (i, 0))
    return _pallas(
        body, name="mlp_fwd", grid=(T // tm, nb),
        in_specs=[row, row, _w_up_spec(), _w_down_spec(), _full((1, D_MODEL))],
        out_specs=[pl.BlockSpec((tm, D_MODEL), lambda i, b: (i, b)), row, row],
        out_shape=[jax.ShapeDtypeStruct((T, D_FF), bf16), jax.ShapeDtypeStruct((T, D_MODEL), f32),
                   jax.ShapeDtypeStruct((T, D_MODEL), bf16)],
        scratch_shapes=[pltpu.VMEM((tm, D_MODEL), f32)],
        compiler_params=_cparams(("parallel", "arbitrary"), VMEM_MID),
    )(h1, u1, gath, gath, g_next)


def _ple_loss(h2, u2, p, tgt, gath, b_pg, w_ple, g_ple, g_fin, g_pg):
    T = h2.shape[0]
    tm = min(256, T)
    npg = D_MODEL // N_CHIPS

    def body(h2_ref, u2_ref, p_ref, t_ref, wpg_ref, bpg_ref, wple_ref, gple_ref, gfin_ref, gpg_ref,
             loss_ref, dh2_ref, dh2b_ref, dgp_ref, dep_ref, dgfin_ref, dgple_ref, dbpg_ref, dgpg_ref):
        @pl.when(pl.program_id(0) == 0)
        def _():
            loss_ref[...] = jnp.zeros_like(loss_ref)
            dgfin_ref[...] = jnp.zeros_like(dgfin_ref)
            dgple_ref[...] = jnp.zeros_like(dgple_ref)
            dbpg_ref[...] = jnp.zeros_like(dbpg_ref)
            dgpg_ref[...] = jnp.zeros_like(dgpg_ref)

        h2 = h2_ref[...]
        gate_pre = bpg_ref[...]
        for b in range(N_CHIPS):
            gate_pre = gate_pre + _dot(u2_ref[:, b * npg:(b + 1) * npg], wpg_ref[b])
        gate = _sigmoid(gate_pre)
        e_pre = _dot(p_ref[...].astype(bf16), wple_ref[...])
        emb = _rms(e_pre, gple_ref[...])
        h3 = h2 + gate * emb
        diff = _rms(h3, gfin_ref[...]) - t_ref[...]
        sq = jnp.sum(jnp.sum(diff * diff, axis=1, keepdims=True), axis=0, keepdims=True)
        loss_ref[...] += (0.5 / D_MODEL) * sq
        dh3, dgfin = _rms_bwd(diff * (1.0 / D_MODEL), h3, gfin_ref[...])
        dgfin_ref[...] += dgfin
        dgp = dh3 * emb * gate * (1.0 - gate)
        dbpg_ref[...] += jnp.sum(dgp, axis=0, keepdims=True)
        dep, dgple = _rms_bwd(dh3 * gate, e_pre, gple_ref[...])
        dgple_ref[...] += dgple
        dgpb = dgp.astype(bf16)
        dgp_ref[...] = dgpb
        dep_ref[...] = dep.astype(bf16)
        du2 = jnp.concatenate([_dot_nt(dgpb, wpg_ref[b]) for b in range(N_CHIPS)], axis=1)
        dx, dgpg = _rms_bwd(du2, h2, gpg_ref[...])
        dgpg_ref[...] += dgpg
        dh2 = dh3 + dx
        dh2_ref[...] = dh2
        dh2b_ref[...] = dh2.astype(bf16)

    row = pl.BlockSpec((tm, D_MODEL), lambda i: (i, 0))
    vec = _full((1, D_MODEL))
    vshape = jax.ShapeDtypeStruct((1, D_MODEL), f32)
    return _pallas(
        body, name="ple_loss", grid=(T // tm,),
        in_specs=[row, row, pl.BlockSpec((tm, PLE_DIM), lambda i: (i, 0)), row,
                  pl.BlockSpec((N_CHIPS, npg, D_MODEL), lambda i: (0, PG_OFF // npg, 0)), vec, _full(w_ple.shape),
                  vec, vec, vec],
        out_specs=[_full((8, LANES)), row, row, row, row, vec, vec, vec, vec],
        out_shape=[jax.ShapeDtypeStruct((8, LANES), f32), jax.ShapeDtypeStruct((T, D_MODEL), f32),
                   jax.ShapeDtypeStruct((T, D_MODEL), bf16), jax.ShapeDtypeStruct((T, D_MODEL), bf16),
                   jax.ShapeDtypeStruct((T, D_MODEL), bf16), vshape, vshape, vshape, vshape],
        compiler_params=_cparams(("arbitrary",), VMEM_MID),
    )(h2, u2, p, tgt, gath, b_pg, w_ple, g_ple, g_fin, g_pg)


def _mlp_bwd(dh2, r, gath, h1, g):
    T = dh2.shape[0]
    tm = min(512, T)
    nb = D_FF // D_MODEL

    def body(dh2_ref, r_ref, wd_ref, wu_ref, h1_ref, g_ref, dhp_ref, dh1_ref, dh1b_ref, dg_ref, acc_ref):
        i, b = pl.program_id(0), pl.program_id(1)

        @pl.when(b == 0)
        def _():
            acc_ref[...] = jnp.zeros_like(acc_ref)

        @pl.when((b == 0) & (i == 0))
        def _():
            dg_ref[...] = jnp.zeros_like(dg_ref)

        dact = _dot_nt(dh2_ref[...].astype(bf16), wd_ref[0])
        dhp = (dact * 2.0 * r_ref[...].astype(f32)).astype(bf16)
        dhp_ref[...] = dhp
        acc_ref[...] += _dot_nt(dhp, wu_ref[0])

        @pl.when(b == nb - 1)
        def _():
            dx, dg = _rms_bwd(acc_ref[...], h1_ref[...], g_ref[...])
            dg_ref[...] += dg
            dh1 = dh2_ref[...] + dx
            dh1_ref[...] = dh1
            dh1b_ref[...] = dh1.astype(bf16)

    row = pl.BlockSpec((tm, D_MODEL), lambda i, b: (i, 0))
    return _pallas(
        body, name="mlp_bwd", grid=(T // tm, nb),
        in_specs=[row, pl.BlockSpec((tm, D_MODEL), lambda i, b: (i, b)), _w_down_spec(), _w_up_spec(), row,
                  _full((1, D_MODEL))],
        out_specs=[pl.BlockSpec((tm, D_MODEL), lambda i, b: (i, b)), row, row, _full((1, D_MODEL))],
        out_shape=[jax.ShapeDtypeStruct((T, D_FF), bf16), jax.ShapeDtypeStruct((T, D_MODEL), f32),
                   jax.ShapeDtypeStruct((T, D_MODEL), bf16), jax.ShapeDtypeStruct((1, D_MODEL), f32)],
        scratch_shapes=[pltpu.VMEM((tm, D_MODEL), f32)],
        compiler_params=_cparams(("arbitrary", "arbitrary"), VMEM_MID),
    )(dh2, r, gath, gath, h1, g)


def _out_proj_bwd(dh1, gath, co, ln_g, ln_b, rider=None):
    T = dh1.shape[0]
    tm = min(512, T)

    def body(dh_ref, w_ref, co_ref, g_ref, b_ref, dys_ref, dco_ref, dg_ref, db_ref):
        @pl.when(pl.program_id(0) == 0)
        def _():
            dg_ref[...] = jnp.zeros_like(dg_ref)
            db_ref[...] = jnp.zeros_like(db_ref)

        dhb = dh_ref[...].astype(bf16)
        dys_ref[...] = jnp.concatenate([_dot_nt(dhb, w_ref[0]), _dot_nt(dhb, w_ref[1])], axis=1)
        dyc = jnp.concatenate([_dot_nt(dhb, w_ref[2]), _dot_nt(dhb, w_ref[3])], axis=1)
        co = co_ref[...]
        mu = jnp.mean(co, axis=-1, keepdims=True)
        xc = co - mu
        rstd = lax.rsqrt(jnp.mean(xc * xc, axis=-1, keepdims=True) + EPS)
        xh = xc * rstd
        yn = xh * g_ref[...] + b_ref[...]
        dyn = dyc * _dsilu(yn)
        dg_ref[...] += jnp.sum(dyn * xh, axis=0, keepdims=True)
        db_ref[...] += jnp.sum(dyn, axis=0, keepdims=True)
        dxh = dyn * g_ref[...]
        dco_ref[...] = rstd * (dxh - jnp.mean(dxh, axis=-1, keepdims=True)
                               - xh * jnp.mean(dxh * xh, axis=-1, keepdims=True))

    row = pl.BlockSpec((tm, D_MODEL), lambda i: (i, 0))
    vec = _full((1, CONF_WIDTH))
    vshape = jax.ShapeDtypeStruct((1, CONF_WIDTH), f32)
    return _call(
        body, (dh1, gath, co, ln_g, ln_b), name="out_proj_bwd", grid=(T // tm,),
        in_specs=[row, _w_out_spec(), row, vec, vec],
        out_specs=[row, row, vec, vec],
        out_shape=[jax.ShapeDtypeStruct((T, SSD_WIDTH), f32), jax.ShapeDtypeStruct((T, CONF_WIDTH), f32), vshape, vshape],
        params=_cparams(("arbitrary",), VMEM_MID), rider=rider)


def _bwd_offsets(K):
    return [K - 1 - k for k in range(K)]


def _next_halo_spec(hb, tm, C, T):
    return pl.BlockSpec((hb, C), lambda i: (jnp.minimum((i + 1) * (tm // hb), T // hb - 1), 0))


DW_RB = 8
DW_UNROLL = 4
DW_ACC_VREGS = 32


def _conv_dw(dw_ref, bufd_ref, bufx_ref, phx_ref, offs_x, tm, C):
    K = len(offs_x)
    group = max(1, DW_ACC_VREGS // (C // LANES))
    for k0 in range(0, K, group):
        ks = list(range(k0, min(k0 + group, K)))

        def step(i, accs, ks=ks):
            for u in range(DW_UNROLL):
                r0 = pl.multiple_of((i * DW_UNROLL + u) * DW_RB, DW_RB)
                d = bufd_ref[pl.ds(r0, DW_RB), :]
                accs = tuple(acc + _window(bufx_ref, phx_ref, offs_x, offs_x[k], r0, DW_RB) * d
                             for k, acc in zip(ks, accs))
            return accs

        accs = lax.fori_loop(0, tm // (DW_RB * DW_UNROLL), step, tuple(jnp.zeros((DW_RB, C), f32) for _ in ks))
        for k, acc in zip(ks, accs):
            dw_ref[k:k + 1, :] += jnp.sum(acc, axis=0, keepdims=True)


def _fill_bwd_buffers(dcur_ref, dnext_ref, xcur_ref, xprev_ref, bufd_ref, bufx_ref, phd_ref, phx_ref, offs_d, offs_x,
                      hb, tm, first, last):
    bufd_ref[0:tm, :] = dcur_ref[...]
    bufd_ref[tm:tm + hb, :] = dnext_ref[...] * jnp.where(last, 0.0, 1.0)
    bufx_ref[0:hb, :] = xprev_ref[...] * jnp.where(first, 0.0, 1.0)
    bufx_ref[hb:hb + tm, :] = xcur_ref[...]
    _make_phases(bufd_ref, phd_ref, offs_d, tm)
    _make_phases(bufx_ref, phx_ref, offs_x, tm)


def _ssd_conv_bwd(dpre, xbc, w):
    T, C = xbc.shape
    K, hb = SSD_CONV, 8
    tm = min(256, T)
    nt = T // tm
    offs_d, offs_x = _bwd_offsets(K), _fwd_offsets(K, hb)

    def body(dcur_ref, dnext_ref, xcur_ref, xprev_ref, w_ref, dx_ref, dw_ref, db_ref, bufd_ref, bufx_ref, phd_ref, phx_ref):
        i = pl.program_id(0)

        @pl.when(i == 0)
        def _():
            dw_ref[...] = jnp.zeros_like(dw_ref)
            db_ref[...] = jnp.zeros_like(db_ref)

        _fill_bwd_buffers(dcur_ref, dnext_ref, xcur_ref, xprev_ref, bufd_ref, bufx_ref, phd_ref, phx_ref, offs_d, offs_x,
                          hb, tm, i == 0, i == nt - 1)

        def chunk(j, carry):
            r0 = pl.multiple_of(j * CONV_RB, CONV_RB)
            dx_ref[pl.ds(r0, CONV_RB), :] = _conv_rows(w_ref, bufd_ref, phd_ref, offs_d, r0, CONV_RB).astype(bf16)
            return carry

        lax.fori_loop(0, tm // CONV_RB, chunk, 0)
        _conv_dw(dw_ref, bufd_ref, bufx_ref, phx_ref, offs_x, tm, C)
        db_ref[...] += jnp.sum(dcur_ref[...], axis=0, keepdims=True)

    row = pl.BlockSpec((tm, C), lambda i: (i, 0))
    return _pallas(
        body, name="ssd_conv_bwd", grid=(nt,),
        in_specs=[row, _next_halo_spec(hb, tm, C, T), row, _prev_halo_spec(hb, tm, C), _full((SUBLANES * K, C))],
        out_specs=[row, _full((8, C)), _full((1, C))],
        out_shape=[jax.ShapeDtypeStruct((T, C), bf16), jax.ShapeDtypeStruct((8, C), f32), jax.ShapeDtypeStruct((1, C), f32)],
        scratch_shapes=[pltpu.VMEM((tm + hb, C), f32), pltpu.VMEM((hb + tm, C), f32),
                        pltpu.VMEM(_phase_shape(offs_d, tm, C), f32),
                        pltpu.VMEM(_phase_shape(offs_x, tm, C), f32)],
        compiler_params=_cparams(("arbitrary",), VMEM_BIG),
    )(dpre, dpre, xbc, xbc, _sublane_rows(w))


def _conf_conv_bwd(dco, v, w, cv, cg, rider=None):
    T, C = v.shape
    K, hb = CONF_KERNEL, 32
    tm = min(256, T)
    nt = T // tm
    offs_d, offs_x = _bwd_offsets(K), _fwd_offsets(K, hb)

    def body(dcur_ref, dnext_ref, vcur_ref, vprev_ref, w_ref, cv_ref, cg_ref, dcv_ref, dcg_ref, dw_ref, db_ref,
             bufd_ref, bufx_ref, phd_ref, phx_ref):
        i = pl.program_id(0)

        @pl.when(i == 0)
        def _():
            dw_ref[...] = jnp.zeros_like(dw_ref)
            db_ref[...] = jnp.zeros_like(db_ref)

        _fill_bwd_buffers(dcur_ref, dnext_ref, vcur_ref, vprev_ref, bufd_ref, bufx_ref, phd_ref, phx_ref, offs_d, offs_x,
                          hb, tm, i == 0, i == nt - 1)

        def chunk(j, carry):
            r0 = pl.multiple_of(j * CONV_RB, CONV_RB)
            rows = pl.ds(r0, CONV_RB)
            dv = _conv_rows(w_ref, bufd_ref, phd_ref, offs_d, r0, CONV_RB)
            s = _sigmoid(cg_ref[rows, :])
            dcv_ref[rows, :] = (dv * s).astype(bf16)
            dcg_ref[rows, :] = (dv * cv_ref[rows, :] * s * (1.0 - s)).astype(bf16)
            return carry

        lax.fori_loop(0, tm // CONV_RB, chunk, 0)
        _conv_dw(dw_ref, bufd_ref, bufx_ref, phx_ref, offs_x, tm, C)
        db_ref[...] += jnp.sum(dcur_ref[...], axis=0, keepdims=True)

    row = pl.BlockSpec((tm, C), lambda i: (i, 0))
    return _call(
        body, (dco, dco, v, v, _sublane_rows(w), cv, cg), name="conf_conv_bwd", grid=(nt,),
        in_specs=[row, _next_halo_spec(hb, tm, C, T), row, _prev_halo_spec(hb, tm, C), _full((SUBLANES * K, C)), row, row],
        out_specs=[row, row, _full((32, C)), _full((1, C))],
        out_shape=[jax.ShapeDtypeStruct((T, C), bf16), jax.ShapeDtypeStruct((T, C), bf16),
                   jax.ShapeDtypeStruct((32, C), f32), jax.ShapeDtypeStruct((1, C), f32)],
        scratch_shapes=[pltpu.VMEM((tm + hb, C), f32), pltpu.VMEM((hb + tm, C), f32),
                        pltpu.VMEM(_phase_shape(offs_d, tm, C), f32),
                        pltpu.VMEM(_phase_shape(offs_x, tm, C), f32)],
        params=_cparams(("arbitrary",), VMEM_BIG), rider=rider)


def _ssd_bwd(dys, y, z, pre, dtr, sprev, dtb, alog, dskip_e, gn):
    T = pre.shape[0]
    nc = T // CHUNK
    GW = SSD_WIDTH // 2

    def body(dys_ref, y_ref, z_ref, pre_ref, dtr_ref, sp_ref, dtb_ref, alog_ref, de_ref, gn_ref,
             dz_ref, dpre_ref, ddtr_ref, dgn_ref, dd_ref, dal_ref, ddtb_ref, ds_ref):
        @pl.when(pl.program_id(0) == 0)
        def _():
            ds_ref[...] = jnp.zeros_like(ds_ref)
            dgn_ref[...] = jnp.zeros_like(dgn_ref)
            dd_ref[...] = jnp.zeros_like(dd_ref)
            dal_ref[...] = jnp.zeros_like(dal_ref)
            ddtb_ref[...] = jnp.zeros_like(ddtb_ref)

        e = _head_matrix()
        pre = pre_ref[...]
        dtr_b = dtr_ref[...] + dtb_ref[...]
        q = _ssd_chunk_common(pre, dtr_ref[...], dtb_ref[...], alog_ref[...], e)
        cs, tri, xc, xd, xs, dt = q["cs"], q["tri"], q["xc"], q["xd"], q["xs"], q["dt"]
        cs_t = cs.T
        st = sp_ref[0]
        dsn = ds_ref[...]
        lane = lax.broadcasted_iota(jnp.int32, (1, LANES), 1)
        halves = (lane < HEAD_DIM, lane >= HEAD_DIM)
        row_i = lax.broadcasted_iota(jnp.int32, (CHUNK, CHUNK), 0)
        col_i = lax.broadcasted_iota(jnp.int32, (CHUNK, CHUNK), 1)
        tri_t = col_i >= row_i

        y = y_ref[...]
        zz = z_ref[...]
        sz = _sigmoid(zz)
        silu_z = zz * sz
        v = y * silu_z
        dout = dys_ref[...]
        gn_v = gn_ref[...]
        dv, vh = [], []
        for g in range(2):
            vg = _group(v, g, GW)
            rstd = lax.rsqrt(jnp.mean(vg * vg, axis=-1, keepdims=True) + EPS)
            vhg = vg * rstd
            dvh = _group(dout, g, GW) * _group(gn_v, g, GW)
            dv.append(rstd * (dvh - vhg * jnp.mean(dvh * vhg, axis=-1, keepdims=True)))
            vh.append(vhg)
        dv = jnp.concatenate(dv, axis=1)
        dgn_ref[...] += jnp.sum(dout * jnp.concatenate(vh, axis=1), axis=0, keepdims=True)
        dy = dv * silu_z
        dz_ref[...] = (dv * y * (sz * (1.0 + zz * (1.0 - sz)))).astype(bf16)

        dd_row = jnp.sum(dy * xs, axis=0, keepdims=True)
        dd_ref[...] += _contract(jnp.broadcast_to(dd_row, (8, SSD_WIDTH)), e)[0:1, :]
        dxs = dy * de_ref[...]

        dz_in = dy * q["ecs_e"]
        g_mat, gt_mat, dcm, dbm, dsp, dxd, y_off = [], [], [], [], [], [], []
        bgs, cgs = [], []
        for g in range(2):
            bg = _group(q["bm"], g, SSD_STATE)
            cg = _group(q["cm"], g, SSD_STATE)
            bgb, cgb = bg.astype(bf16), cg.astype(bf16)
            bgs.append(bgb)
            cgs.append(cgb)
            stg = _group(st, g, GW).astype(bf16)
            dsng = _group(dsn, g, GW).astype(bf16)
            dzg = _group(dz_in, g, GW).astype(bf16)
            g_mat.append(_dot_nt(cgb, bgb))
            gt_mat.append(_dot_nt(bgb, cgb))
            y_off.append(_dot(cgb, stg))
            dcm.append(_dot_nt(dzg, stg))
            dsp.append(_dot(cg.T.astype(bf16), dzg))
            dbm.append(_dot_nt(_group(xd, g, GW).astype(bf16), dsng))
            dxd.append(_dot(bgb, dsng))
        y_off = jnp.concatenate(y_off, axis=1) * q["ecs_e"]
        dxd = jnp.concatenate(dxd, axis=1)
        ds_ref[...] = dsn * q["cd_e"] + jnp.concatenate(dsp, axis=1)
        dcd_row = jnp.sum(dsn * st, axis=0, keepdims=True) * q["cd_e"]
        t_e = dxd * xd
        dcs = _contract(dy * y_off - t_e, e)
        last_row = _contract(jnp.broadcast_to(dcd_row + jnp.sum(t_e, axis=0, keepdims=True), (8, SSD_WIDTH)), e)[0:1, :]
        dxc_state = dxd * q["dte_e"]

        dg_acc = [jnp.zeros((CHUNK, CHUNK), f32), jnp.zeros((CHUNK, CHUNK), f32)]
        dgt_acc = [jnp.zeros((CHUNK, CHUNK), f32), jnp.zeros((CHUNK, CHUNK), f32)]
        dxc_pairs = []
        for j in range(SSD_HEADS // 2):
            dyp_f = dy[:, j * LANES:(j + 1) * LANES]
            xcp_f = xc[:, j * LANES:(j + 1) * LANES]
            acc = jnp.zeros((CHUNK, LANES), f32)
            for hh in range(2):
                h = 2 * j + hh
                g = h // 8
                dyp = jnp.where(halves[hh], dyp_f, 0.0).astype(bf16)
                xcp = jnp.where(halves[hh], xcp_f, 0.0).astype(bf16)
                lm = jnp.exp(jnp.where(tri, cs[:, h:h + 1] - cs_t[h:h + 1, :], -1e30))
                lm_t = jnp.exp(jnp.where(tri_t, cs_t[h:h + 1, :] - cs[:, h:h + 1], -1e30))
                dm = _dot_nt(dyp, xcp) * lm
                dm_t = _dot_nt(xcp, dyp) * lm_t
                acc = acc + _dot((gt_mat[g] * lm_t).astype(bf16), dyp)
                dg_acc[g] = dg_acc[g] + dm
                dgt_acc[g] = dgt_acc[g] + dm_t
                qd = jnp.sum(dm * g_mat[g] - dm_t * gt_mat[g], axis=1, keepdims=True)
                dcs = dcs + qd * (lane == h).astype(f32)
            dxc_pairs.append(acc)
        dxc = jnp.concatenate(dxc_pairs, axis=1) + dxc_state
        for g in range(2):
            dcm[g] = dcm[g] + _dot(dg_acc[g].astype(bf16), bgs[g])
            dbm[g] = dbm[g] + _dot(dgt_acc[g].astype(bf16), cgs[g])

        dxs = dxs + dxc * q["dt_e"]
        ddt = _contract(dxc * xs, e)
        dcs = dcs + jnp.where(row_i == CHUNK - 1, jnp.broadcast_to(last_row, (CHUNK, LANES)), 0.0)
        da = jnp.dot(tri_t.astype(f32), dcs, precision=lax.Precision.HIGHEST, preferred_element_type=f32)
        ddt = ddt + da * q["a_neg"]
        dal_ref[...] += jnp.sum(da * dt, axis=0, keepdims=True) * q["a_neg"]
        ddtr = ddt * _sigmoid(dtr_b) * (lane < SSD_HEADS).astype(f32)
        ddtb_ref[...] += jnp.sum(ddtr, axis=0, keepdims=True)
        ddtr_ref[...] = ddtr.astype(bf16)

        dact = jnp.concatenate([dxs, dbm[0], dbm[1], dcm[0], dcm[1]], axis=1)
        dpre_ref[...] = dact * _dsilu(pre)

    rev = lambda n: pl.BlockSpec((CHUNK, n), lambda c: (nc - 1 - c, 0))
    vec = _full((1, LANES))
    vshape = jax.ShapeDtypeStruct((1, LANES), f32)
    return _pallas(
        body, name="ssd_bwd", grid=(nc,),
        in_specs=[rev(SSD_WIDTH), rev(SSD_WIDTH), rev(SSD_WIDTH), rev(XBC_WIDTH), rev(LANES),
                  pl.BlockSpec((1, SSD_STATE, SSD_WIDTH), lambda c: (nc - 1 - c, 0, 0)),
                  vec, vec, _full((1, SSD_WIDTH)), _full((1, SSD_WIDTH))],
        out_specs=[rev(SSD_WIDTH), rev(XBC_WIDTH), rev(LANES), _full((1, SSD_WIDTH)), vec, vec, vec],
        out_shape=[jax.ShapeDtypeStruct((T, SSD_WIDTH), bf16), jax.ShapeDtypeStruct((T, XBC_WIDTH), f32),
                   jax.ShapeDtypeStruct((T, LANES), bf16), jax.ShapeDtypeStruct((1, SSD_WIDTH), f32),
                   vshape, vshape, vshape],
        scratch_shapes=[pltpu.VMEM((SSD_STATE, SSD_WIDTH), f32)],
        compiler_params=_cparams(("arbitrary",), VMEM_MID),
    )(dys, y, z, pre, dtr, sprev, dtb, alog, dskip_e, gn)


def _in_proj_bwd(dz, dxbc, dcv, dcg, ddt, gin, x, dh1, g):
    T = x.shape[0]
    tm = min(256, T)

    def body(dz_ref, dx_ref, dcv_ref, dcg_ref, ddt_ref, gin_ref, x_ref, dh_ref, g_ref, gx_ref, dg_ref, wt_ref):
        @pl.when(pl.program_id(0) == 0)
        def _():
            dg_ref[...] = jnp.zeros_like(dg_ref)
            _assemble_w_in_t(gin_ref, wt_ref)

        du = (_dot(dz_ref[...], wt_ref[0:O_XBC, :]) + _dot(dx_ref[...], wt_ref[O_XBC:O_DT, :])
              + _dot(dcv_ref[...], wt_ref[O_CV:O_CG, :]) + _dot(dcg_ref[...], wt_ref[O_CG:IN_WIDTH, :])
              + _dot(ddt_ref[...], wt_ref[O_DT:O_DT + LANES, :]))
        dx, dg = _rms_bwd(du, x_ref[...], g_ref[...])
        dg_ref[...] += dg
        gx_ref[...] = dh_ref[...] + dx

    row = lambda n: pl.BlockSpec((tm, n), lambda i: (i, 0))
    return _pallas(
        body, name="in_proj_bwd", grid=(T // tm,),
        in_specs=[row(SSD_WIDTH), row(XBC_WIDTH), row(CONF_WIDTH), row(CONF_WIDTH), row(LANES), _full(gin.shape),
                  row(D_MODEL), row(D_MODEL), _full((1, D_MODEL))],
        out_specs=[row(D_MODEL), _full((1, D_MODEL))],
        out_shape=[jax.ShapeDtypeStruct((T, D_MODEL), f32), jax.ShapeDtypeStruct((1, D_MODEL), f32)],
        scratch_shapes=[pltpu.VMEM((IN_WIDTH, D_MODEL), bf16)],
        compiler_params=_cparams(("arbitrary",), VMEM_BIG),
    )(dz, dxbc, dcv, dcg, ddt, gin, x, dh1, g)


def _weight_grad(a, g, name, square=False, slab=None, place=None, tk=512):
    T, K = a.shape
    N = g.shape[1]
    tk = min(tk, K)
    tn = 1024 if N % 1024 == 0 else min(512, N)
    tt = min(2048, T)

    def body(a_ref, g_ref, *rest):
        o_ref = rest[-1]
        acc = _dot_tn(_operand(a_ref[...]), g_ref[...].astype(bf16))
        t = pl.program_id(2)
        shaped = acc if slab is None else acc[None]

        @pl.when(t == 0)
        def _():
            o_ref[...] = shaped

        @pl.when(t > 0)
        def _():
            o_ref[...] += shaped

    def _operand(av):
        if square:
            av = av.astype(f32)
            av = av * av
        return av.astype(bf16)

    in_specs = [pl.BlockSpec((tt, tk), lambda i, j, t: (t, i)), pl.BlockSpec((tt, tn), lambda i, j, t: (t, j))]
    grid = (K // tk, N // tn, T // tt)
    params = _cparams(("parallel", "parallel", "arbitrary"), VMEM_MID)
    if slab is None:
        return _pallas(
            body, name=name, grid=grid, in_specs=in_specs,
            out_specs=pl.BlockSpec((tk, tn), lambda i, j, t: (i, j)),
            out_shape=jax.ShapeDtypeStruct((K, N), f32), compiler_params=params,
        )(a, g)
    return _pallas(
        body, name=name, grid=grid, in_specs=in_specs + [ANY],
        out_specs=pl.BlockSpec((1, tk, tn), lambda i, j, t: place(i, j)),
        out_shape=jax.ShapeDtypeStruct(slab.shape, f32), input_output_aliases={2: 0}, compiler_params=params,
    )(a, g, slab)


def _place():
    return lax.axis_index("x"), lax.axis_index("y"), lax.axis_index("c")


def _other_chips(x, y):
    return [(1 - x, y), (x, 1 - y), (1 - x, 1 - y)]


def _remote(src, dst, ssem, rsem, dev):
    return pltpu.make_async_remote_copy(src_ref=src, dst_ref=dst, send_sem=ssem, recv_sem=rsem, device_id=dev,
                                        device_id_type=MESH)


def _gather_weights(arrays, convw):
    n = len(arrays)
    halves = tuple(a.shape[1] // 2 for a in arrays)

    def body(*refs):
        cw_ref, cwo_ref = refs[n], refs[2 * n + 1]
        ssem, rsem, lsem = refs[2 * n + 2:]
        triples = tuple(zip(refs[:n], refs[n + 1:2 * n + 1], halves))
        x, y, c = _place()
        me_b = 2 * x + y
        sib = (x, y, 1 - c)
        chips = _other_chips(x, y)
        loc = pltpu.make_async_copy(cw_ref, cwo_ref.at[me_b], lsem)
        loc.start()
        sends = []
        for j, (src, dst, h) in enumerate(triples):
            mine = pl.ds(c * h, h)
            for k, (px, py) in enumerate(chips):
                s = 6 * j + k
                sends.append(_remote(src.at[me_b, mine], dst.at[me_b, mine], ssem.at[s], rsem.at[s], (px, py, c)))
        for k, (px, py) in enumerate(chips):
            sends.append(_remote(cw_ref, cwo_ref.at[me_b], ssem.at[6 * n + k], rsem.at[6 * n + k], (px, py, c)))
        for cp in sends:
            cp.start()
        for j, (src, dst, h) in enumerate(triples):
            mine = pl.ds(c * h, h)
            for k, (px, py) in enumerate(chips):
                b = 2 * px + py
                s = 6 * j + k
                _remote(src.at[b, mine], dst.at[b, mine], ssem.at[s], rsem.at[s], (px, py, c)).wait_recv()
                fw = _remote(dst.at[b, mine], dst.at[b, mine], ssem.at[s + 3], rsem.at[s + 3], sib)
                fw.start()
                sends.append(fw)
        for k, (px, py) in enumerate(chips):
            b = 2 * px + py
            _remote(cw_ref, cwo_ref.at[b], ssem.at[6 * n + k], rsem.at[6 * n + k], (px, py, c)).wait_recv()
        for j, (src, dst, h) in enumerate(triples):
            theirs = pl.ds((1 - c) * h, h)
            for k, (px, py) in enumerate(chips):
                b = 2 * px + py
                s = 6 * j + k + 3
                _remote(src.at[b, theirs], dst.at[b, theirs], ssem.at[s], rsem.at[s], sib).wait_recv()
        for cp in sends:
            cp.wait_send()
        loc.wait()

    return _pallas(
        body, name="gather_weights", in_specs=[ANY] * (n + 1), out_specs=[ANY] * (n + 1),
        out_shape=[jax.ShapeDtypeStruct(a.shape, bf16) for a in arrays]
        + [jax.ShapeDtypeStruct((N_CHIPS, CONVW_ROWS, D_MODEL), f32)],
        input_output_aliases={j: j for j in range(n)},
        scratch_shapes=[pltpu.SemaphoreType.DMA((6 * n + 3,)), pltpu.SemaphoreType.DMA((6 * n + 3,)),
                        pltpu.SemaphoreType.DMA(())],
    )(*arrays, convw)


def _gather_rider(gath0, lo, n):
    h = gath0.shape[1] // 2

    def copies(rins, routs, ssem, rsem, sending):
        (g_ref,), (o_ref,) = rins, routs
        x, y, c = _place()
        mine = pl.ds(c * h + lo, n)
        for k, (px, py) in enumerate(_other_chips(x, y)):
            b = 2 * x + y if sending else 2 * px + py
            yield _remote(g_ref.at[b, mine], o_ref.at[b, mine], ssem.at[k], rsem.at[k], (px, py, c))

    def start(*refs):
        for cp in copies(*refs, sending=True):
            cp.start()

    def finish(*refs):
        for cp in copies(*refs, sending=False):
            cp.wait()

    return _Rider([gath0], [jax.ShapeDtypeStruct(gath0.shape, gath0.dtype)], {0: 0}, 3, start, finish)


def _forward_to_sibling(gath):
    h = gath.shape[1] // 2

    def body(g_ref, o_ref, ssem, rsem):
        x, y, c = _place()
        sib = (x, y, 1 - c)
        mine, theirs = pl.ds(c * h, h), pl.ds((1 - c) * h, h)
        blocks = [2 * px + py for px, py in _other_chips(x, y)]
        sends = [_remote(g_ref.at[b, mine], o_ref.at[b, mine], ssem.at[k], rsem.at[k], sib) for k, b in enumerate(blocks)]
        for cp in sends:
            cp.start()
        for k, b in enumerate(blocks):
            _remote(g_ref.at[b, theirs], o_ref.at[b, theirs], ssem.at[k], rsem.at[k], sib).wait_recv()
        for cp in sends:
            cp.wait_send()

    return _pallas(
        body, name="forward_to_sibling", in_specs=[ANY], out_specs=ANY,
        out_shape=jax.ShapeDtypeStruct(gath.shape, gath.dtype), input_output_aliases={0: 0},
        scratch_shapes=[pltpu.SemaphoreType.DMA((3,)), pltpu.SemaphoreType.DMA((3,))],
    )(gath)


def _swap_copy(g_ref, r_ref, ssem, rsem):
    x, y, c = _place()
    h = r_ref.shape[1]
    return _remote(g_ref.at[:, pl.ds((1 - c) * h, h), :], r_ref, ssem.at[0], rsem.at[0], (x, y, 1 - c))


def _swap_rider(g):
    def start(rins, routs, ssem, rsem):
        _swap_copy(rins[0], routs[0], ssem, rsem).start()

    def finish(rins, routs, ssem, rsem):
        _swap_copy(rins[0], routs[0], ssem, rsem).wait()

    return _Rider([g], [jax.ShapeDtypeStruct((N_CHIPS, g.shape[1] // 2, g.shape[2]), g.dtype)], {}, 1, start, finish)


def _swap_halves(g):
    def body(g_ref, r_ref, ssem, rsem):
        cp = _swap_copy(g_ref, r_ref, ssem, rsem)
        cp.start()
        cp.wait()

    return _pallas(
        body, name="swap_halves", in_specs=[ANY], out_specs=ANY,
        out_shape=jax.ShapeDtypeStruct((N_CHIPS, g.shape[1] // 2, g.shape[2]), g.dtype),
        scratch_shapes=[pltpu.SemaphoreType.DMA((1,)), pltpu.SemaphoreType.DMA((1,))],
    )(g)


def _chip_sum(cidx, gslab, recv, name):
    half, C = recv.shape[1:]
    tr = half // 2 if (half // 2) % 16 == 0 else half

    def body(c_ref, g_ref, r_ref, o_ref):
        o_ref[...] = (g_ref[...] + r_ref[...]).astype(bf16)

    return _pallas(
        body, name=name,
        grid_spec=pltpu.PrefetchScalarGridSpec(
            num_scalar_prefetch=1, grid=(N_CHIPS, half // tr),
            in_specs=[pl.BlockSpec((1, tr, C), lambda b, i, c_ref: (b, c_ref[0] * (half // tr) + i, 0)),
                      pl.BlockSpec((1, tr, C), lambda b, i, c_ref: (b, i, 0))],
            out_specs=pl.BlockSpec((1, tr, C), lambda b, i, c_ref: (b, i, 0))),
        out_shape=jax.ShapeDtypeStruct((N_CHIPS, half, C), bf16),
        compiler_params=_cparams(("parallel", "parallel"), VMEM_MID),
    )(cidx, gslab, recv)


def _exchange_rider(h):
    def copies(rins, routs, ssem, rsem):
        x, y, c = _place()
        for k, (px, py) in enumerate(_other_chips(x, y)):
            yield _remote(rins[0].at[2 * px + py], routs[0].at[k], ssem.at[k], rsem.at[k], (px, py, c))

    def start(*refs):
        for cp in copies(*refs):
            cp.start()

    def finish(*refs):
        for cp in copies(*refs):
            cp.wait()

    return _Rider([h], [jax.ShapeDtypeStruct((3,) + h.shape[1:], h.dtype)], {}, 3, start, finish)


def _exchange(hb, small):
    def body(hb_ref, sm_ref, rb_ref, all_ref, ssem, rsem, lsem):
        x, y, c = _place()
        me = 4 * x + 2 * y + c
        chips = _other_chips(x, y)
        loc = pltpu.make_async_copy(sm_ref, all_ref.at[me], lsem)
        loc.start()
        sends = []
        for k, (px, py) in enumerate(chips):
            sends.append(_remote(hb_ref.at[2 * px + py], rb_ref.at[k], ssem.at[3 + k], rsem.at[3 + k], (px, py, c)))
        peers = []
        for r in range(1, N_DEV):
            peer = ((1 - x) if r & 4 else x, (1 - y) if r & 2 else y, (1 - c) if r & 1 else c)
            peers.append(peer)
            sends.append(_remote(sm_ref, all_ref.at[me], ssem.at[5 + r], rsem.at[5 + r], peer))
        for cp in sends:
            cp.start()
        for k, (px, py) in enumerate(chips):
            _remote(hb_ref.at[0], rb_ref.at[k], ssem.at[3 + k], rsem.at[3 + k], (px, py, c)).wait_recv()
        for r, peer in zip(range(1, N_DEV), peers):
            pid = 4 * peer[0] + 2 * peer[1] + peer[2]
            _remote(sm_ref, all_ref.at[pid], ssem.at[5 + r], rsem.at[5 + r], peer).wait_recv()
        for cp in sends:
            cp.wait_send()
        loc.wait()

    return _pallas(
        body, name="exchange", in_specs=[ANY, ANY], out_specs=[ANY, ANY],
        out_shape=[jax.ShapeDtypeStruct((3,) + hb.shape[1:], bf16),
                   jax.ShapeDtypeStruct((N_DEV, SMALL_ROWS, D_MODEL), f32)],
        scratch_shapes=[pltpu.SemaphoreType.DMA((13,)), pltpu.SemaphoreType.DMA((13,)), pltpu.SemaphoreType.DMA(())],
    )(hb, small)


def _final_sum(idx, gslab, recv_sib, recv_ici, name):
    half, C = recv_sib.shape[1:]
    tr = half // 2 if (half // 2) % 16 == 0 else half

    def body(i_ref, g_ref, r_ref, p_ref, o_ref):
        acc = g_ref[0] + r_ref[0]
        for k in range(3):
            acc = acc + p_ref[k].astype(f32)
        o_ref[...] = acc

    return _pallas(
        body, name=name,
        grid_spec=pltpu.PrefetchScalarGridSpec(
            num_scalar_prefetch=1, grid=(half // tr,),
            in_specs=[pl.BlockSpec((1, tr, C), lambda i, s: (s[1], s[0] * (half // tr) + i, 0)),
                      pl.BlockSpec((1, tr, C), lambda i, s: (s[1], i, 0)),
                      pl.BlockSpec((3, tr, C), lambda i, s: (0, i, 0))],
            out_specs=pl.BlockSpec((tr, C), lambda i, s: (s[0] * (half // tr) + i, 0))),
        out_shape=jax.ShapeDtypeStruct((2 * half, C), f32),
        compiler_params=_cparams(("parallel",), VMEM_MID),
    )(idx, gslab, recv_sib, recv_ici)


def _join_halves(ra, rb):
    ha, hb = ra.shape[0] // 2, rb.shape[0] // 2

    def body(a_ref, b_ref, ao_ref, bo_ref, ssem, rsem):
        x, y, c = _place()
        sib = (x, y, 1 - c)
        mine_a, theirs_a = pl.ds(c * ha, ha), pl.ds((1 - c) * ha, ha)
        mine_b, theirs_b = pl.ds(c * hb, hb), pl.ds((1 - c) * hb, hb)
        ca = _remote(a_ref.at[mine_a], ao_ref.at[mine_a], ssem.at[0], rsem.at[0], sib)
        cb = _remote(b_ref.at[mine_b], bo_ref.at[mine_b], ssem.at[1], rsem.at[1], sib)
        ca.start()
        cb.start()
        _remote(a_ref.at[theirs_a], ao_ref.at[theirs_a], ssem.at[0], rsem.at[0], sib).wait_recv()
        _remote(b_ref.at[theirs_b], bo_ref.at[theirs_b], ssem.at[1], rsem.at[1], sib).wait_recv()
        ca.wait_send()
        cb.wait_send()

    return _pallas(
        body, name="join_halves", in_specs=[ANY, ANY], out_specs=[ANY, ANY],
        out_shape=[jax.ShapeDtypeStruct(ra.shape, f32), jax.ShapeDtypeStruct(rb.shape, f32)],
        input_output_aliases={0: 0, 1: 1},
        scratch_shapes=[pltpu.SemaphoreType.DMA((2,)), pltpu.SemaphoreType.DMA((2,))],
    )(ra, rb)


def _shard_rows(gt):
    def body(g_ref, o_ref):
        for b in range(N_CHIPS):
            o_ref[b, 0:W_IN_ROWS, :] = g_ref[b * W_IN_ROWS:(b + 1) * W_IN_ROWS, :]
            o_ref[b, W_IN_ROWS:W_IN_ROWS_PAD, :] = jnp.zeros((W_IN_ROWS_PAD - W_IN_ROWS, LANES), f32)

    return _pallas(
        body, name="shard_rows", grid=(D_MODEL // LANES,),
        in_specs=[pl.BlockSpec((IN_WIDTH, LANES), lambda i: (0, i))],
        out_specs=pl.BlockSpec((N_CHIPS, W_IN_ROWS_PAD, LANES), lambda i: (0, 0, i)),
        out_shape=jax.ShapeDtypeStruct((N_CHIPS, W_IN_ROWS_PAD, D_MODEL), f32),
        compiler_params=_cparams(("parallel",), VMEM_MID),
    )(gt)


def _sum_small(all_small):
    def body(a_ref, o_ref):
        acc = a_ref[0]
        for d in range(1, N_DEV):
            acc = acc + a_ref[d]
        o_ref[...] = acc

    return _pallas(
        body, name="sum_small", out_shape=jax.ShapeDtypeStruct((SMALL_ROWS, D_MODEL), f32),
    )(all_small)


def _adamw(w, g, m, v, name, g_off=0, by_columns=False):
    R, C = w.shape
    tr = 256 if R % 256 == 0 else R
    assert g_off % tr == 0 and not (by_columns and g_off)
    c1 = 1.0 - ADAM_B1 ** ADAM_STEP
    c2 = 1.0 - ADAM_B2 ** ADAM_STEP

    def body(w_ref, g_ref, m_ref, v_ref, d_ref, mo_ref, vo_ref):
        gg = g_ref[...]
        m2 = ADAM_B1 * m_ref[...] + (1.0 - ADAM_B1) * gg
        v2 = ADAM_B2 * v_ref[...] + (1.0 - ADAM_B2) * (gg * gg)
        mo_ref[...] = m2
        vo_ref[...] = v2
        d_ref[...] = -ADAM_LR * ((m2 / c1) / (jnp.sqrt(v2 / c2) + ADAM_EPS) + ADAM_WD * w_ref[...])

    if by_columns:
        blk = gblk = pl.BlockSpec((R, LANES), lambda i: (0, i))
        grid = (C // LANES,)
    else:
        blk = pl.BlockSpec((tr, C), lambda i: (i, 0))
        gblk = pl.BlockSpec((tr, C), lambda i: (g_off // tr + i, 0))
        grid = (R // tr,)
    shp = jax.ShapeDtypeStruct((R, C), f32)
    return _pallas(
        body, name=name, grid=grid, in_specs=[blk, gblk, blk, blk], out_specs=[blk] * 3, out_shape=[shp] * 3,
        compiler_params=_cparams(("parallel",), VMEM_MID),
    )(w, g, m, v)


def _pad_lanes(v):
    return jnp.pad(v, ((0, 0), (0, LANES - v.shape[1])))


def _local_step(x, p, tgt, gath0, cidx, gin, S):
    dtb = _pad_lanes(S["dt_bias"])
    alog = _pad_lanes(S["A_log"])
    dskip_e = jnp.repeat(S["D_skip"], HEAD_DIM, axis=1)

    early = GATHER_EARLY_ROWS
    u0, z, xbc, cv, cg, dtr, v, gath1 = _in_proj_fwd(x, S["mix_norm_g"], gin, rider=_gather_rider(gath0, 0, early))
    co, yc, gath = _conf_fwd(v, S["conf_dw_w"], S["conf_dw_b"], S["conf_ln_g"], S["conf_ln_b"],
                             rider=_gather_rider(gath1, early, SLAB_A // 2 - early))
    gath = _forward_to_sibling(gath)
    w_ple = jnp.concatenate([_ple_of_slab(gath[b]) for b in range(N_CHIPS)], axis=1)
    pre = _ssd_conv_fwd(xbc, S["ssd_conv_w"], S["ssd_conv_b"])
    y, ys, sprev = _ssd_fwd(pre, dtr, z, dtb, alog, dskip_e, S["ssd_norm_g"])
    h1, u1 = _out_proj_fwd(x, ys, yc, gath, S["mlp_norm_g"])
    r, h2, u2 = _mlp_fwd(h1, u1, gath, S["ple_gate_norm_g"])
    loss, dh2, dh2b, dgp, dep, dg_fin, dg_ple, db_pg, dg_pg = _ple_loss(
        h2, u2, p, tgt, gath, S["b_ple_gate"], w_ple, S["ple_norm_g"], S["final_norm_g"], S["ple_gate_norm_g"])

    npg = D_MODEL // N_CHIPS
    ga = lax.empty((N_CHIPS, SLAB_A, D_MODEL), f32)
    ga = _weight_grad(u2, dgp, "dw_ple_gate", slab=ga, tk=npg, place=lambda i, j: (i, PG_OFF // npg, j))
    ga = _weight_grad(r, dh2b, "dw_down", square=True, slab=ga, place=lambda i, j: (i // 2, DOWN_OFF // 512 + i % 2, j))
    gw_ple = _weight_grad(p, dep, "dw_ple")
    dhp, dh1, dh1b, dg_mlp = _mlp_bwd(dh2, r, gath, h1, S["mlp_norm_g"])
    ga = _weight_grad(u1, dhp, "dw_up", slab=ga, place=lambda i, j: (j, UP_OFF // 512 + i, 0))
    ga = _weight_grad(ys, dh1b, "dw_out_ssd", slab=ga, place=lambda i, j: (i, OUT_OFF // 512, j))
    ga = _weight_grad(yc, dh1b, "dw_out_conf", slab=ga, place=lambda i, j: (2 + i, OUT_OFF // 512, j))
    n_ple = D_MODEL // N_CHIPS
    ple_rows = jnp.stack([_rows(gw_ple[:, b * n_ple:(b + 1) * n_ple]) for b in range(N_CHIPS)], axis=0)
    ga = lax.dynamic_update_slice(ga, ple_rows, (0, PLE_OFF, 0))
    dys, dco, dg_ln, db_ln, recv_a = _out_proj_bwd(dh1, gath, co, S["conf_ln_g"], S["conf_ln_b"], rider=_swap_rider(ga))
    ha = _chip_sum(cidx, ga, recv_a, "chip_sum_a")
    dcv, dcg, dw_conf, db_conf, ici_a = _conf_conv_bwd(dco, v, S["conf_dw_w"], cv, cg, rider=_exchange_rider(ha))
    dz, dpre, ddtr, dg_ssdn, dd, dal, ddtb = _ssd_bwd(dys, y, z, pre, dtr, sprev, dtb, alog, dskip_e, S["ssd_norm_g"])
    dxbc, dw_sconv, db_sconv = _ssd_conv_bwd(dpre, xbc, S["ssd_conv_w"])
    gx, dg_mix = _in_proj_bwd(dz, dxbc, dcv, dcg, ddtr, gin, x, dh1, S["mix_norm_g"])

    gw_in = jnp.concatenate([
        _weight_grad(dz, u0, "dw_in_z"), _weight_grad(dxbc, u0, "dw_in_xbc"),
        _weight_grad(ddtr, u0, "dw_in_dt")[:SSD_HEADS],
        _weight_grad(dcv, u0, "dw_in_cv"), _weight_grad(dcg, u0, "dw_in_cg")], axis=0)
    small = {
        "mix_norm_g": dg_mix, "ssd_conv_w": dw_sconv, "ssd_conv_b": db_sconv, "dt_bias": ddtb, "A_log": dal, "D_skip": dd,
        "ssd_norm_g": dg_ssdn, "conf_dw_w": dw_conf, "conf_dw_b": db_conf, "conf_ln_g": dg_ln, "conf_ln_b": db_ln,
        "mlp_norm_g": dg_mlp, "ple_gate_norm_g": dg_pg, "b_ple_gate": db_pg, "ple_norm_g": dg_ple,
        "final_norm_g": dg_fin, "loss": loss,
    }
    return gx, ga, recv_a, ici_a, gw_in, small


def _rows(a):
    return a.reshape(-1, D_MODEL)


def _pad_rows(a, n):
    flat = a.reshape(-1)
    return jnp.pad(flat, (0, n * D_MODEL - flat.shape[0])).reshape(n, D_MODEL)


def _ple_of_slab(slab):
    return slab[PLE_OFF:PLE_OFF + PLE_ROWS].reshape(PLE_DIM, D_MODEL // N_CHIPS)


ROW_VEC = {"mix_norm_g": 0, "ssd_norm_g": 1, "conf_dw_b": 2, "conf_ln_g": 3, "conf_ln_b": 4, "mlp_norm_g": 5,
           "ple_gate_norm_g": 6, "b_ple_gate": 7, "ple_norm_g": 8, "final_norm_g": 9}
ROW_CONV_B = 10
ROW_HEADS = 12
ROW_CONV_W = 16
ROW_DW = 24
HEAD_LANES = {"dt_bias": 0, "A_log": 1, "D_skip": 2, "loss": 3}
SMALL_ORDER = ("mix_norm_g", "ssd_conv_w", "ssd_conv_b", "dt_bias", "A_log", "D_skip", "ssd_norm_g", "conf_dw_w",
               "conf_dw_b", "conf_ln_g", "conf_ln_b", "mlp_norm_g", "ple_gate_norm_g", "b_ple_gate", "ple_norm_g",
               "final_norm_g")
SPLIT = XBC_WIDTH - D_MODEL


def _pack_small(raw):
    names = list(ROW_VEC) + ["ssd_conv_b", "dt_bias", "A_log", "D_skip", "loss", "ssd_conv_w", "conf_dw_w"]

    def body(*refs):
        r = dict(zip(names, refs[:-1]))
        o_ref = refs[-1]
        o_ref[...] = jnp.zeros_like(o_ref)
        for n, row in ROW_VEC.items():
            o_ref[row:row + 1, :] = r[n][...]
        o_ref[ROW_CONV_B:ROW_CONV_B + 1, :] = r["ssd_conv_b"][:, 0:D_MODEL]
        o_ref[ROW_CONV_B + 1:ROW_CONV_B + 2, 0:SPLIT] = r["ssd_conv_b"][:, D_MODEL:]
        for n, j in HEAD_LANES.items():
            o_ref[ROW_HEADS:ROW_HEADS + 1, j * LANES:(j + 1) * LANES] = r[n][0:1, :]
        for k in range(SSD_CONV):
            o_ref[ROW_CONV_W + 2 * k:ROW_CONV_W + 2 * k + 1, :] = r["ssd_conv_w"][k:k + 1, 0:D_MODEL]
            o_ref[ROW_CONV_W + 2 * k + 1:ROW_CONV_W + 2 * k + 2, 0:SPLIT] = r["ssd_conv_w"][k:k + 1, D_MODEL:]
        o_ref[ROW_DW:ROW_DW + 32, :] = r["conf_dw_w"][...]

    return _pallas(
        body, name="pack_small", out_shape=jax.ShapeDtypeStruct((SMALL_ROWS, D_MODEL), f32),
    )(*[raw[n] for n in names])


def _adamw_small(cidx, tot, w, m, v):
    c1 = 1.0 - ADAM_B1 ** ADAM_STEP
    c2 = 1.0 - ADAM_B2 ** ADAM_STEP
    n_par = len(SMALL_ORDER)

    def shard(full, chip, width):
        out = full[:, 0:width]
        for b in range(1, N_CHIPS):
            out = jnp.where(chip == b, full[:, b * width:(b + 1) * width], out)
        return out

    def grad_of(n, t_ref, chip):
        if n in ROW_VEC:
            return t_ref[ROW_VEC[n]:ROW_VEC[n] + 1, :]
        if n == "ssd_conv_b":
            return jnp.concatenate([t_ref[ROW_CONV_B:ROW_CONV_B + 1, :], t_ref[ROW_CONV_B + 1:ROW_CONV_B + 2, 0:SPLIT]], axis=1)
        if n in HEAD_LANES:
            j = HEAD_LANES[n]
            return t_ref[ROW_HEADS:ROW_HEADS + 1, j * LANES:j * LANES + SSD_HEADS]
        if n == "ssd_conv_w":
            rows = [jnp.concatenate([t_ref[ROW_CONV_W + 2 * k:ROW_CONV_W + 2 * k + 1, :],
                                     t_ref[ROW_CONV_W + 2 * k + 1:ROW_CONV_W + 2 * k + 2, 0:SPLIT]], axis=1)
                    for k in range(SSD_CONV)]
            return shard(jnp.concatenate(rows, axis=0), chip, XBC_WIDTH // N_CHIPS)
        return shard(t_ref[ROW_DW:ROW_DW + CONF_KERNEL, :], chip, CONF_WIDTH // N_CHIPS)

    def body(c_ref, t_ref, *refs):
        ins, outs = refs[:3 * n_par], refs[3 * n_par:]
        chip = c_ref[1]
        for i, n in enumerate(SMALL_ORDER):
            w_ref, m_ref, v_ref = ins[3 * i:3 * i + 3]
            g_ref, d_ref, mo_ref, vo_ref = outs[4 * i:4 * i + 4]
            g = grad_of(n, t_ref, chip)
            m2 = ADAM_B1 * m_ref[...] + (1.0 - ADAM_B1) * g
            v2 = ADAM_B2 * v_ref[...] + (1.0 - ADAM_B2) * (g * g)
            g_ref[...] = g
            mo_ref[...] = m2
            vo_ref[...] = v2
            d_ref[...] = -ADAM_LR * ((m2 / c1) / (jnp.sqrt(v2 / c2) + ADAM_EPS) + ADAM_WD * w_ref[...])

    args, in_specs, out_specs, out_shape = [], [], [], []
    for n in SMALL_ORDER:
        shp = w[n].shape
        spec = pl.BlockSpec(shp, lambda i, c_ref: (0, 0))
        args += [w[n], m[n], v[n]]
        in_specs += [spec] * 3
        out_specs += [spec] * 4
        out_shape += [jax.ShapeDtypeStruct(shp, f32)] * 4
    outs = _pallas(
        body, name="adamw_small",
        grid_spec=pltpu.PrefetchScalarGridSpec(
            num_scalar_prefetch=1, grid=(1,),
            in_specs=[pl.BlockSpec(tot.shape, lambda i, c_ref: (0, 0))] + in_specs, out_specs=out_specs),
        out_shape=out_shape,
    )(cidx, tot, *args)
    grad, delta, new_m, new_v = {}, {}, {}, {}
    for i, n in enumerate(SMALL_ORDER):
        grad[n], delta[n], new_m[n], new_v[n] = outs[4 * i:4 * i + 4]
    return grad, delta, new_m, new_v


BIG = ("w_in", "w_out", "w_up", "w_down", "w_ple_gate", "w_ple")
BIG_A = (("w_up", UP_OFF), ("w_down", DOWN_OFF), ("w_out", OUT_OFF), ("w_ple_gate", PG_OFF))
WEIGHTS = ("mix_norm_g", "w_in", "ssd_conv_w", "ssd_conv_b", "dt_bias", "A_log", "D_skip", "ssd_norm_g", "conf_dw_w",
           "conf_dw_b", "conf_ln_g", "conf_ln_b", "w_out", "mlp_norm_g", "w_up", "w_down", "ple_gate_norm_g",
           "w_ple_gate", "b_ple_gate", "w_ple", "ple_norm_g", "final_norm_g")


def kernel(x, p, mix_norm_g, w_in, ssd_conv_w, ssd_conv_b, dt_bias, A_log, D_skip, ssd_norm_g, conf_dw_w, conf_dw_b, conf_ln_g, conf_ln_b, w_out, mlp_norm_g, w_up, w_down, ple_gate_norm_g, w_ple_gate, b_ple_gate, w_ple, ple_norm_g, final_norm_g, loss_target, m_mix_norm_g, m_w_in, m_ssd_conv_w, m_ssd_conv_b, m_dt_bias, m_A_log, m_D_skip, m_ssd_norm_g, m_conf_dw_w, m_conf_dw_b, m_conf_ln_g, m_conf_ln_b, m_w_out, m_mlp_norm_g, m_w_up, m_w_down, m_ple_gate_norm_g, m_w_ple_gate, m_b_ple_gate, m_w_ple, m_ple_norm_g, m_final_norm_g, v_mix_norm_g, v_w_in, v_ssd_conv_w, v_ssd_conv_b, v_dt_bias, v_A_log, v_D_skip, v_ssd_norm_g, v_conf_dw_w, v_conf_dw_b, v_conf_ln_g, v_conf_ln_b, v_w_out, v_mlp_norm_g, v_w_up, v_w_down, v_ple_gate_norm_g, v_w_ple_gate, v_b_ple_gate, v_w_ple, v_ple_norm_g, v_final_norm_g):
    w = dict(mix_norm_g=mix_norm_g, w_in=w_in, ssd_conv_w=ssd_conv_w, ssd_conv_b=ssd_conv_b, dt_bias=dt_bias, A_log=A_log,
             D_skip=D_skip, ssd_norm_g=ssd_norm_g, conf_dw_w=conf_dw_w, conf_dw_b=conf_dw_b, conf_ln_g=conf_ln_g,
             conf_ln_b=conf_ln_b, w_out=w_out, mlp_norm_g=mlp_norm_g, w_up=w_up, w_down=w_down,
             ple_gate_norm_g=ple_gate_norm_g, w_ple_gate=w_ple_gate, b_ple_gate=b_ple_gate, w_ple=w_ple,
             ple_norm_g=ple_norm_g, final_norm_g=final_norm_g)
    m = dict(mix_norm_g=m_mix_norm_g, w_in=m_w_in, ssd_conv_w=m_ssd_conv_w, ssd_conv_b=m_ssd_conv_b, dt_bias=m_dt_bias,
             A_log=m_A_log, D_skip=m_D_skip, ssd_norm_g=m_ssd_norm_g, conf_dw_w=m_conf_dw_w, conf_dw_b=m_conf_dw_b,
             conf_ln_g=m_conf_ln_g, conf_ln_b=m_conf_ln_b, w_out=m_w_out, mlp_norm_g=m_mlp_norm_g, w_up=m_w_up,
             w_down=m_w_down, ple_gate_norm_g=m_ple_gate_norm_g, w_ple_gate=m_w_ple_gate, b_ple_gate=m_b_ple_gate,
             w_ple=m_w_ple, ple_norm_g=m_ple_norm_g, final_norm_g=m_final_norm_g)
    v = dict(mix_norm_g=v_mix_norm_g, w_in=v_w_in, ssd_conv_w=v_ssd_conv_w, ssd_conv_b=v_ssd_conv_b, dt_bias=v_dt_bias,
             A_log=v_A_log, D_skip=v_D_skip, ssd_norm_g=v_ssd_norm_g, conf_dw_w=v_conf_dw_w, conf_dw_b=v_conf_dw_b,
             conf_ln_g=v_conf_ln_g, conf_ln_b=v_conf_ln_b, w_out=v_w_out, mlp_norm_g=v_mlp_norm_g, w_up=v_w_up,
             w_down=v_w_down, ple_gate_norm_g=v_ple_gate_norm_g, w_ple_gate=v_w_ple_gate, b_ple_gate=v_b_ple_gate,
             w_ple=v_w_ple, ple_norm_g=v_ple_norm_g, final_norm_g=v_final_norm_g)
    xi, yi, ci = lax.axis_index("x"), lax.axis_index("y"), lax.axis_index("c")
    chip = 2 * xi + yi

    slab = jnp.concatenate([w_up[0], w_down[0], w_out[0], w_ple_gate[0], _rows(w_ple[0])], axis=0).astype(bf16)
    gath0 = lax.dynamic_update_slice(jnp.zeros((N_CHIPS, SLAB_A, D_MODEL), bf16), slab[None], (chip, 0, 0))
    wt_shard = jnp.swapaxes(w_in, 1, 2).astype(bf16)
    gin0 = lax.dynamic_update_slice(jnp.zeros((N_CHIPS, W_IN_ROWS_PAD, D_MODEL), bf16), wt_shard, (chip, 0, 0))
    convw = _pad_rows(jnp.concatenate([ssd_conv_w[0].reshape(-1), conf_dw_w[0].reshape(-1)]), CONVW_ROWS)
    gin, cwg = _gather_weights([gin0], convw)
    n_sc = SSD_CONV * (XBC_WIDTH // N_CHIPS)
    n_cf = CONF_KERNEL * (CONF_WIDTH // N_CHIPS)
    S = {n: w[n][0] for n in ("mix_norm_g", "ssd_conv_b", "dt_bias", "A_log", "D_skip", "ssd_norm_g", "conf_dw_b",
                              "conf_ln_g", "conf_ln_b", "mlp_norm_g", "ple_gate_norm_g", "b_ple_gate", "ple_norm_g")}
    S = {n: a.reshape(1, -1) for n, a in S.items()}
    S["final_norm_g"] = final_norm_g.reshape(1, -1)
    S["ssd_conv_w"] = jnp.concatenate(
        [cwg[b].reshape(-1)[:n_sc].reshape(SSD_CONV, XBC_WIDTH // N_CHIPS) for b in range(N_CHIPS)], axis=1)
    S["conf_dw_w"] = jnp.concatenate(
        [cwg[b].reshape(-1)[n_sc:n_sc + n_cf].reshape(CONF_KERNEL, CONF_WIDTH // N_CHIPS) for b in range(N_CHIPS)], axis=1)

    cidx = jnp.stack([ci, chip]).astype(jnp.int32)
    grad_x, ga, recv_a, ici_a, gw_in, gsmall = _local_step(x[0], p[0, 0], loss_target[0], gath0, cidx, gin, S)

    gb = _shard_rows(gw_in)
    small = _pack_small(gsmall)
    recv_b = _swap_halves(gb)
    hb = _chip_sum(cidx, gb, recv_b, "chip_sum_b")
    ici_b, all_small = _exchange(hb, small)
    ra = _final_sum(cidx, ga, recv_a, ici_a, "final_sum_a")
    rb = _final_sum(cidx, gb, recv_b, ici_b, "final_sum_b")
    ra, rb = _join_halves(ra, rb)
    tot_small = _sum_small(all_small)

    loss = tot_small[ROW_HEADS, HEAD_LANES["loss"] * LANES]

    two_d = lambda a: a.reshape(a.shape[-2:]) if a.ndim > 1 else a.reshape(1, -1)
    small_w, small_m, small_v = ({n: two_d(d[n]) for n in SMALL_ORDER} for d in (w, m, v))
    grads, delta, new_m, new_v = _adamw_small(cidx, tot_small, small_w, small_m, small_v)
    g_in_t = rb[:W_IN_ROWS]
    grads["w_ple"] = _ple_of_slab(ra)
    grads["w_in"] = jnp.swapaxes(g_in_t, 0, 1)
    for n, off in BIG_A:
        grads[n] = ra[off:off + w[n].shape[1]]
        delta[n], new_m[n], new_v[n] = _adamw(w[n][0], ra, m[n][0], v[n][0], "adamw_" + n, g_off=off)
    delta["w_ple"], new_m["w_ple"], new_v["w_ple"] = _adamw(w_ple[0], grads["w_ple"], m_w_ple[0], v_w_ple[0], "adamw_w_ple")
    tr_ = lambda a: jnp.swapaxes(a[0], 0, 1)
    d_, m_, v_ = _adamw(tr_(w_in), g_in_t, tr_(m_w_in), tr_(v_w_in), "adamw_w_in", by_columns=True)
    delta["w_in"], new_m["w_in"], new_v["w_in"] = (jnp.swapaxes(a, 0, 1) for a in (d_, m_, v_))

    shaped = lambda d: [d[n].reshape(w[n].shape) for n in WEIGHTS]
    return (loss, grad_x[None], *shaped(grads), *shaped(delta), *shaped(new_m), *shaped(new_v))
```

```python
import jax
import jax.numpy as jnp
from jax import lax
from jax.experimental import pallas as pl
from jax.experimental.pallas import tpu as pltpu

f32 = jnp.float32
bf16 = jnp.bfloat16

D_MODEL = 1024
SSD_WIDTH = 1024
SSD_HEADS = 16
HEAD_DIM = 64
SSD_STATE = 128
XBC_WIDTH = 1536
SSD_CONV = 4
CHUNK = 128
CONF_WIDTH = 1024
CONF_KERNEL = 31
D_FF = 4096
PLE_DIM = 256
IN_WIDTH = 4624
EPS = 1e-6
N_CHIPS = 4
N_DEV = 8

ADAM_LR = 0.001
ADAM_B1 = 0.9
ADAM_B2 = 0.999
ADAM_EPS = 1e-08
ADAM_WD = 0.01
ADAM_STEP = 10

LANES = 128
VMEM_BIG = 56 * 1024 * 1024
VMEM_MID = 40 * 1024 * 1024

UP_OFF, DOWN_OFF, OUT_OFF, PG_OFF, PLE_OFF = 0, 1024, 2048, 2560, 2816
PLE_ROWS = 64
SLAB_A = PLE_OFF + PLE_ROWS
GATHER_EARLY_ROWS = 480
W_IN_ROWS = 1156
W_IN_ROWS_PAD = 1184
CONVW_ROWS = 16
SMALL_ROWS = 56

MESH = pl.DeviceIdType.MESH
ANY = pl.BlockSpec(memory_space=pl.ANY)


PIN_SMALL = 256 * 1024


def _pallas(body, pin_bytes=None, **kw):
    call = pl.pallas_call(body, **kw)

    def pin(a):
        wanted = pin_bytes is None or a.size * a.dtype.itemsize <= pin_bytes
        return pltpu.with_memory_space_constraint(a, pltpu.HBM) if wanted and a.dtype != jnp.int32 else a

    def run(*args):
        return call(*[pin(a) for a in args])

    return run


def _cparams(sem=None, vmem=None):
    return pltpu.CompilerParams(dimension_semantics=sem, vmem_limit_bytes=vmem)


def _full(shape):
    n = len(shape)
    return pl.BlockSpec(shape, lambda *_: (0,) * n)


class _Rider:
    def __init__(self, inputs, out_shapes, aliases, n_sems, start, finish):
        self.inputs, self.out_shapes, self.aliases = list(inputs), list(out_shapes), dict(aliases)
        self.n_sems, self.start, self.finish = n_sems, start, finish


def _call(body, args, *, name, grid, in_specs, out_specs, out_shape, scratch_shapes=(), params=None, rider=None):
    if rider is None:
        return _pallas(body, name=name, grid=grid, in_specs=in_specs, out_specs=out_specs, out_shape=out_shape,
                              scratch_shapes=list(scratch_shapes), compiler_params=params)(*args)
    ni, no, ns = len(in_specs), len(out_specs), len(scratch_shapes)
    ri, ro = len(rider.inputs), len(rider.out_shapes)
    (steps,) = grid

    def with_rider(*refs):
        ins, refs = refs[:ni], refs[ni:]
        rins, refs = refs[:ri], refs[ri:]
        outs, refs = refs[:no], refs[no:]
        routs, refs = refs[:ro], refs[ro:]
        scratch, (ssem, rsem) = refs[:ns], refs[ns:]
        step = pl.program_id(0)

        @pl.when(step == 0)
        def _():
            rider.start(rins, routs, ssem, rsem)

        body(*ins, *outs, *scratch)

        @pl.when(step == steps - 1)
        def _():
            rider.finish(rins, routs, ssem, rsem)

    sems = [pltpu.SemaphoreType.DMA((rider.n_sems,)), pltpu.SemaphoreType.DMA((rider.n_sems,))]
    return _pallas(
        with_rider, name=name, grid=grid, in_specs=list(in_specs) + [ANY] * ri, out_specs=list(out_specs) + [ANY] * ro,
        out_shape=list(out_shape) + rider.out_shapes, scratch_shapes=list(scratch_shapes) + sems,
        input_output_aliases={ni + a: no + b for a, b in rider.aliases.items()}, compiler_params=params,
    )(*args, *rider.inputs)


def _dot(a, b):
    return jnp.dot(a, b, preferred_element_type=f32)


def _dot_nt(a, b):
    return lax.dot_general(a, b, (((1,), (1,)), ((), ())), preferred_element_type=f32)


def _dot_tn(a, b):
    return lax.dot_general(a, b, (((0,), (0,)), ((), ())), preferred_element_type=f32)


def _sigmoid(x):
    return jax.nn.sigmoid(x)


def _rms(x, g):
    r = lax.rsqrt(jnp.mean(x * x, axis=-1, keepdims=True) + EPS)
    return x * r * g


def _rms_bwd(dy, x, g):
    r = lax.rsqrt(jnp.mean(x * x, axis=-1, keepdims=True) + EPS)
    xh = x * r
    dg = jnp.sum(dy * xh, axis=0, keepdims=True)
    dxh = dy * g
    dx = r * (dxh - xh * jnp.mean(dxh * xh, axis=-1, keepdims=True))
    return dx, dg


def _dsilu(x):
    s = _sigmoid(x)
    return s * (1.0 + x * (1.0 - s))


def _split3(x):
    hi = x.astype(bf16)
    r1 = x - hi.astype(f32)
    mid = r1.astype(bf16)
    lo = (r1 - mid.astype(f32)).astype(bf16)
    return hi, mid, lo


def _head_matrix():
    row = lax.broadcasted_iota(jnp.int32, (LANES, SSD_WIDTH), 0)
    col = lax.broadcasted_iota(jnp.int32, (LANES, SSD_WIDTH), 1)
    lo = row * HEAD_DIM
    return ((col >= lo) & (col < lo + HEAD_DIM)).astype(bf16)


def _expand(x, e):
    hi, mid, lo = _split3(x)
    return _dot(hi, e) + _dot(mid, e) + _dot(lo, e)


def _contract(x, e):
    hi = x.astype(bf16)
    mid = (x - hi.astype(f32)).astype(bf16)
    return _dot_nt(hi, e) + _dot_nt(mid, e)


O_XBC = SSD_WIDTH
O_DT = O_XBC + XBC_WIDTH
O_CV = O_DT + SSD_HEADS
O_CG = O_CV + CONF_WIDTH


def _assemble_w_in_t(gin_ref, wt_ref):
    for b in range(N_CHIPS):
        wt_ref[b * W_IN_ROWS:(b + 1) * W_IN_ROWS, :] = gin_ref[b, 0:W_IN_ROWS, :]


def _in_proj_fwd(x, g, gin, rider=None):
    T = x.shape[0]
    tm = min(256, T)

    def body(x_ref, g_ref, gin_ref, u_ref, z_ref, xbc_ref, cv_ref, cg_ref, dt_ref, v_ref, wt_ref):
        @pl.when(pl.program_id(0) == 0)
        def _():
            _assemble_w_in_t(gin_ref, wt_ref)

        ub = _rms(x_ref[...], g_ref[...]).astype(bf16)
        u_ref[...] = ub
        z_ref[...] = _dot_nt(ub, wt_ref[0:O_XBC, :])
        xbc_ref[...] = _dot_nt(ub, wt_ref[O_XBC:O_DT, :])
        cv = _dot_nt(ub, wt_ref[O_CV:O_CG, :])
        cg = _dot_nt(ub, wt_ref[O_CG:IN_WIDTH, :])
        cv_ref[...] = cv
        cg_ref[...] = cg
        v_ref[...] = cv * _sigmoid(cg)
        dt_ref[...] = _dot_nt(ub, wt_ref[O_DT:O_DT + LANES, :])

    row = lambda n: pl.BlockSpec((tm, n), lambda i: (i, 0))
    return _call(
        body, (x, g, gin), name="in_proj_fwd", grid=(T // tm,),
        in_specs=[row(D_MODEL), _full((1, D_MODEL)), _full(gin.shape)],
        out_specs=[row(D_MODEL), row(SSD_WIDTH), row(XBC_WIDTH), row(CONF_WIDTH), row(CONF_WIDTH), row(LANES),
                   row(CONF_WIDTH)],
        out_shape=[jax.ShapeDtypeStruct((T, D_MODEL), bf16), jax.ShapeDtypeStruct((T, SSD_WIDTH), f32),
                   jax.ShapeDtypeStruct((T, XBC_WIDTH), f32), jax.ShapeDtypeStruct((T, CONF_WIDTH), f32),
                   jax.ShapeDtypeStruct((T, CONF_WIDTH), f32), jax.ShapeDtypeStruct((T, LANES), f32),
                   jax.ShapeDtypeStruct((T, CONF_WIDTH), f32)],
        scratch_shapes=[pltpu.VMEM((IN_WIDTH, D_MODEL), bf16)],
        params=_cparams(("arbitrary",), VMEM_BIG), rider=rider)


SUBLANES = 8


def _phases(offsets):
    return sorted({o % SUBLANES for o in offsets} - {0})


def _phase_shape(offsets, tm, C):
    a_max = max([o // SUBLANES for o in offsets if o % SUBLANES] or [0])
    return (max(len(_phases(offsets)), 1), tm + SUBLANES * a_max, C)


def _make_phases(buf_ref, ph_ref, offsets, tm):
    for idx, b in enumerate(_phases(offsets)):
        n = tm + SUBLANES * max(o // SUBLANES for o in offsets if o % SUBLANES == b)
        ph_ref[idx, 0:n, :] = buf_ref[pl.ds(b, n), :]


def _window(buf_ref, ph_ref, offsets, o, r0, rb):
    a, b = divmod(o, SUBLANES)
    if b == 0:
        return buf_ref[pl.ds(r0 + SUBLANES * a, rb), :]
    return ph_ref[_phases(offsets).index(b), pl.ds(r0 + SUBLANES * a, rb), :]


def _conv_rows(wb_ref, buf_ref, ph_ref, offsets, r0, rb):
    nsub = rb // SUBLANES
    accs = [None] * nsub
    for k, o in enumerate(offsets):
        wk = wb_ref[pl.ds(SUBLANES * k, SUBLANES), :]
        for s in range(nsub):
            term = wk * _window(buf_ref, ph_ref, offsets, o, r0 + SUBLANES * s, SUBLANES)
            accs[s] = term if accs[s] is None else accs[s] + term
    return accs[0] if nsub == 1 else jnp.concatenate(accs, axis=0)


def _sublane_rows(w):
    return jnp.repeat(w, SUBLANES, axis=0)


def _fwd_offsets(K, hb):
    return [hb - (K - 1) + k for k in range(K)]


def _prev_halo_spec(hb, tm, C):
    return pl.BlockSpec((hb, C), lambda i: (jnp.maximum(i * (tm // hb) - 1, 0), 0))


CONV_RB = 16


def _ssd_conv_fwd(xbc, w, b):
    T, C = xbc.shape
    K, hb = SSD_CONV, 8
    tm = min(256, T)
    offs = _fwd_offsets(K, hb)

    def body(cur_ref, halo_ref, w_ref, b_ref, pre_ref, buf_ref, ph_ref):
        keep = jnp.where(pl.program_id(0) > 0, 1.0, 0.0)
        buf_ref[0:hb, :] = halo_ref[...] * keep
        buf_ref[hb:hb + tm, :] = cur_ref[...]
        _make_phases(buf_ref, ph_ref, offs, tm)

        def chunk(i, carry):
            r0 = pl.multiple_of(i * CONV_RB, CONV_RB)
            pre_ref[pl.ds(r0, CONV_RB), :] = _conv_rows(w_ref, buf_ref, ph_ref, offs, r0, CONV_RB) + b_ref[...]
            return carry

        lax.fori_loop(0, tm // CONV_RB, chunk, 0)

    return _pallas(
        body, name="ssd_conv_fwd", grid=(T // tm,),
        in_specs=[pl.BlockSpec((tm, C), lambda i: (i, 0)), _prev_halo_spec(hb, tm, C), _full((SUBLANES * K, C)),
                  _full((1, C))],
        out_specs=pl.BlockSpec((tm, C), lambda i: (i, 0)),
        out_shape=jax.ShapeDtypeStruct((T, C), f32),
        scratch_shapes=[pltpu.VMEM((hb + tm, C), f32), pltpu.VMEM(_phase_shape(offs, tm, C), f32)],
        compiler_params=_cparams(("parallel",), VMEM_MID),
    )(xbc, xbc, _sublane_rows(w), b)


def _conf_fwd(v, w, b, ln_g, ln_b, rider=None):
    T, C = v.shape
    K, hb = CONF_KERNEL, 32
    tm = min(256, T)
    offs = _fwd_offsets(K, hb)
    rb = 2 * CONV_RB

    def body(cur_ref, halo_ref, w_ref, b_ref, g_ref, bb_ref, co_ref, y_ref, buf_ref, ph_ref):
        keep = jnp.where(pl.program_id(0) > 0, 1.0, 0.0)
        buf_ref[0:hb, :] = halo_ref[...] * keep
        buf_ref[hb:hb + tm, :] = cur_ref[...]
        _make_phases(buf_ref, ph_ref, offs, tm)

        def chunk(i, carry):
            r0 = pl.multiple_of(i * rb, rb)
            co = _conv_rows(w_ref, buf_ref, ph_ref, offs, r0, rb) + b_ref[...]
            co_ref[pl.ds(r0, rb), :] = co
            mu = jnp.mean(co, axis=-1, keepdims=True)
            xc = co - mu
            yn = xc * lax.rsqrt(jnp.mean(xc * xc, axis=-1, keepdims=True) + EPS) * g_ref[...] + bb_ref[...]
            y_ref[pl.ds(r0, rb), :] = (yn * _sigmoid(yn)).astype(bf16)
            return carry

        lax.fori_loop(0, tm // rb, chunk, 0)

    return _call(
        body, (v, v, _sublane_rows(w), b, ln_g, ln_b), name="conf_fwd", grid=(T // tm,),
        in_specs=[pl.BlockSpec((tm, C), lambda i: (i, 0)), _prev_halo_spec(hb, tm, C), _full((SUBLANES * K, C)),
                  _full((1, C)), _full((1, C)), _full((1, C))],
        out_specs=[pl.BlockSpec((tm, C), lambda i: (i, 0)), pl.BlockSpec((tm, C), lambda i: (i, 0))],
        out_shape=[jax.ShapeDtypeStruct((T, C), f32), jax.ShapeDtypeStruct((T, C), bf16)],
        scratch_shapes=[pltpu.VMEM((hb + tm, C), f32), pltpu.VMEM(_phase_shape(offs, tm, C), f32)],
        params=_cparams(("arbitrary",), VMEM_MID), rider=rider)


def _ssd_chunk_common(pre, dtr, dtb, alog, e):
    act = pre * _sigmoid(pre)
    xs = act[:, :SSD_WIDTH]
    bm = act[:, SSD_WIDTH:SSD_WIDTH + 2 * SSD_STATE]
    cm = act[:, SSD_WIDTH + 2 * SSD_STATE:]
    row = lax.broadcasted_iota(jnp.int32, (CHUNK, CHUNK), 0)
    col = lax.broadcasted_iota(jnp.int32, (CHUNK, CHUNK), 1)
    tri = row >= col
    dt = jax.nn.softplus(dtr + dtb)
    a_neg = -jnp.exp(alog)
    a = dt * a_neg
    cs = jnp.dot(tri.astype(f32), a, precision=lax.Precision.HIGHEST, preferred_element_type=f32)
    cs_e = _expand(cs, e)
    dt_e = _expand(dt, e)
    csl_e = cs_e[CHUNK - 1:CHUNK, :]
    ecs_e = jnp.exp(cs_e)
    dte_e = jnp.exp(csl_e - cs_e)
    cd_e = jnp.exp(csl_e)
    xc = xs * dt_e
    xd = xc * dte_e
    return dict(xs=xs, bm=bm, cm=cm, tri=tri, dt=dt, a_neg=a_neg, cs=cs, ecs_e=ecs_e, dte_e=dte_e, cd_e=cd_e,
                dt_e=dt_e, xc=xc, xd=xd)


def _group(v, g, width):
    return v[:, g * width:(g + 1) * width]


def _ssd_fwd(pre, dtr, z, dtb, alog, dskip_e, gn):
    T = pre.shape[0]
    nc = T // CHUNK
    GW = SSD_WIDTH // 2

    def body(pre_ref, dtr_ref, z_ref, dtb_ref, alog_ref, de_ref, gn_ref, y_ref, ys_ref, sp_ref, st_ref):
        @pl.when(pl.program_id(0) == 0)
        def _():
            st_ref[...] = jnp.zeros_like(st_ref)

        e = _head_matrix()
        q = _ssd_chunk_common(pre_ref[...], dtr_ref[...], dtb_ref[...], alog_ref[...], e)
        cs, tri, xc, xd = q["cs"], q["tri"], q["xc"], q["xd"]
        cs_t = cs.T
        st = st_ref[...]
        sp_ref[0] = st
        lane = lax.broadcasted_iota(jnp.int32, (1, LANES), 1)
        halves = (lane < HEAD_DIM, lane >= HEAD_DIM)

        g_mat, y_off, s_new = [], [], []
        for g in range(2):
            bg = _group(q["bm"], g, SSD_STATE)
            cg = _group(q["cm"], g, SSD_STATE)
            bgb, cgb = bg.astype(bf16), cg.astype(bf16)
            g_mat.append(_dot_nt(cgb, bgb))
            y_off.append(_dot(cgb, _group(st, g, GW).astype(bf16)))
            s_new.append(_dot(bg.T.astype(bf16), _group(xd, g, GW).astype(bf16)))
        y_off = jnp.concatenate(y_off, axis=1) * q["ecs_e"]
        st_ref[...] = st * q["cd_e"] + jnp.concatenate(s_new, axis=1)

        pairs = []
        for j in range(SSD_HEADS // 2):
            xp = xc[:, j * LANES:(j + 1) * LANES]
            acc = jnp.zeros((CHUNK, LANES), f32)
            for hh in range(2):
                h = 2 * j + hh
                seg = cs[:, h:h + 1] - cs_t[h:h + 1, :]
                lm = jnp.exp(jnp.where(tri, seg, -1e30))
                m = (g_mat[h // 8] * lm).astype(bf16)
                acc = acc + _dot(m, jnp.where(halves[hh], xp, 0.0).astype(bf16))
            pairs.append(acc)
        y = jnp.concatenate(pairs, axis=1) + y_off + q["xs"] * de_ref[...]
        y_ref[...] = y

        zz = z_ref[...]
        v = y * (zz * _sigmoid(zz))
        outs = []
        for g in range(2):
            vg = _group(v, g, GW)
            outs.append(vg * lax.rsqrt(jnp.mean(vg * vg, axis=-1, keepdims=True) + EPS))
        ys_ref[...] = (jnp.concatenate(outs, axis=1) * gn_ref[...]).astype(bf16)

    ch = lambda n: pl.BlockSpec((CHUNK, n), lambda c: (c, 0))
    return _pallas(
        body, name="ssd_fwd", grid=(nc,),
        in_specs=[ch(XBC_WIDTH), ch(LANES), ch(SSD_WIDTH), _full((1, LANES)), _full((1, LANES)), _full((1, SSD_WIDTH)),
                  _full((1, SSD_WIDTH))],
        out_specs=[ch(SSD_WIDTH), ch(SSD_WIDTH), pl.BlockSpec((1, SSD_STATE, SSD_WIDTH), lambda c: (c, 0, 0))],
        out_shape=[jax.ShapeDtypeStruct((T, SSD_WIDTH), f32), jax.ShapeDtypeStruct((T, SSD_WIDTH), bf16),
                   jax.ShapeDtypeStruct((nc, SSD_STATE, SSD_WIDTH), f32)],
        scratch_shapes=[pltpu.VMEM((SSD_STATE, SSD_WIDTH), f32)],
        compiler_params=_cparams(("arbitrary",), VMEM_MID),
    )(pre, dtr, z, dtb, alog, dskip_e, gn)


def _w_out_spec():
    n = 2 * SSD_WIDTH // N_CHIPS
    return pl.BlockSpec((N_CHIPS, n, D_MODEL), lambda *_: (0, OUT_OFF // n, 0))


def _out_proj_fwd(x, ys, yc, gath, g):
    T = x.shape[0]
    tm = min(512, T)
    n = 2 * SSD_WIDTH // N_CHIPS

    def body(x_ref, ys_ref, yc_ref, w_ref, g_ref, h_ref, u_ref):
        h = (x_ref[...] + _dot(ys_ref[:, 0:n], w_ref[0]) + _dot(ys_ref[:, n:], w_ref[1])
             + _dot(yc_ref[:, 0:n], w_ref[2]) + _dot(yc_ref[:, n:], w_ref[3]))
        h_ref[...] = h
        u_ref[...] = _rms(h, g_ref[...]).astype(bf16)

    row = pl.BlockSpec((tm, D_MODEL), lambda i: (i, 0))
    return _pallas(
        body, name="out_proj_fwd", grid=(T // tm,),
        in_specs=[row, row, row, _w_out_spec(), _full((1, D_MODEL))],
        out_specs=[row, row],
        out_shape=[jax.ShapeDtypeStruct((T, D_MODEL), f32), jax.ShapeDtypeStruct((T, D_MODEL), bf16)],
        compiler_params=_cparams(("parallel",), VMEM_MID),
    )(x, ys, yc, gath, g)


def _w_up_spec():
    return pl.BlockSpec((1, D_MODEL, D_MODEL), lambda i, b: (b, UP_OFF // D_MODEL, 0))


def _w_down_spec():
    return pl.BlockSpec((1, D_MODEL, D_MODEL), lambda i, b: (b, DOWN_OFF // D_MODEL, 0))


def _mlp_fwd(h1, u1, gath, g_next):
    T = h1.shape[0]
    tm = min(512, T)
    nb = D_FF // D_MODEL

    def body(h_ref, u_ref, wu_ref, wd_ref, g_ref, r_ref, h2_ref, u2_ref, acc_ref):
        b = pl.program_id(1)

        @pl.when(b == 0)
        def _():
            acc_ref[...] = jnp.zeros_like(acc_ref)

        r = jnp.maximum(_dot(u_ref[...], wu_ref[0]), 0.0)
        r_ref[...] = r.astype(bf16)
        acc_ref[...] += _dot((r * r).astype(bf16), wd_ref[0])

        @pl.when(b == nb - 1)
        def _():
            h2 = h_ref[...] + acc_ref[...]
            h2_ref[...] = h2
            u2_ref[...] = _rms(h2, g_ref[...]).astype(bf16)

    row = pl.BlockSpec((tm, D_MODEL), lambda i, b: (i, 0))
    return _pallas(
        body, name="mlp_fwd", grid=(T // tm, nb),
        in_specs=[row, row, _w_up_spec(), _w_down_spec(), _full((1, D_MODEL))],
        out_specs=[pl.BlockSpec((tm, D_MODEL), lambda i, b: (i, b)), row, row],
        out_shape=[jax.ShapeDtypeStruct((T, D_FF), bf16), jax.ShapeDtypeStruct((T, D_MODEL), f32),
                   jax.ShapeDtypeStruct((T, D_MODEL), bf16)],
        scratch_shapes=[pltpu.VMEM((tm, D_MODEL), f32)],
        compiler_params=_cparams(("parallel", "arbitrary"), VMEM_MID),
    )(h1, u1, gath, gath, g_next)


def _ple_loss(h2, u2, p, tgt, gath, b_pg, w_ple, g_ple, g_fin, g_pg):
    T = h2.shape[0]
    tm = min(256, T)
    npg = D_MODEL // N_CHIPS

    def body(h2_ref, u2_ref, p_ref, t_ref, wpg_ref, bpg_ref, wple_ref, gple_ref, gfin_ref, gpg_ref,
             loss_ref, dh2_ref, dh2b_ref, dgp_ref, dep_ref, dgfin_ref, dgple_ref, dbpg_ref, dgpg_ref):
        @pl.when(pl.program_id(0) == 0)
        def _():
            loss_ref[...] = jnp.zeros_like(loss_ref)
            dgfin_ref[...] = jnp.zeros_like(dgfin_ref)
            dgple_ref[...] = jnp.zeros_like(dgple_ref)
            dbpg_ref[...] = jnp.zeros_like(dbpg_ref)
            dgpg_ref[...] = jnp.zeros_like(dgpg_ref)

        h2 = h2_ref[...]
        gate_pre = bpg_ref[...]
        for b in range(N_CHIPS):
            gate_pre = gate_pre + _dot(u2_ref[:, b * npg:(b + 1) * npg], wpg_ref[b])
        gate = _sigmoid(gate_pre)
        e_pre = _dot(p_ref[...].astype(bf16), wple_ref[...])
        emb = _rms(e_pre, gple_ref[...])
        h3 = h2 + gate * emb
        diff = _rms(h3, gfin_ref[...]) - t_ref[...]
        sq = jnp.sum(jnp.sum(diff * diff, axis=1, keepdims=True), axis=0, keepdims=True)
        loss_ref[...] += (0.5 / D_MODEL) * sq
        dh3, dgfin = _rms_bwd(diff * (1.0 / D_MODEL), h3, gfin_ref[...])
        dgfin_ref[...] += dgfin
        dgp = dh3 * emb * gate * (1.0 - gate)
        dbpg_ref[...] += jnp.sum(dgp, axis=0, keepdims=True)
        dep, dgple = _rms_bwd(dh3 * gate, e_pre, gple_ref[...])
        dgple_ref[...] += dgple
        dgpb = dgp.astype(bf16)
        dgp_ref[...] = dgpb
        dep_ref[...] = dep.astype(bf16)
        du2 = jnp.concatenate([_dot_nt(dgpb, wpg_ref[b]) for b in range(N_CHIPS)], axis=1)
        dx, dgpg = _rms_bwd(du2, h2, gpg_ref[...])
        dgpg_ref[...] += dgpg
        dh2 = dh3 + dx
        dh2_ref[...] = dh2
        dh2b_ref[...] = dh2.astype(bf16)

    row = pl.BlockSpec((tm, D_MODEL), lambda i: (i, 0))
    vec = _full((1, D_MODEL))
    vshape = jax.ShapeDtypeStruct((1, D_MODEL), f32)
    return _pallas(
        body, name="ple_loss", grid=(T // tm,),
        in_specs=[row, row, pl.BlockSpec((tm, PLE_DIM), lambda i: (i, 0)), row,
                  pl.BlockSpec((N_CHIPS, npg, D_MODEL), lambda i: (0, PG_OFF // npg, 0)), vec, _full(w_ple.shape),
                  vec, vec, vec],
        out_specs=[_full((8, LANES)), row, row, row, row, vec, vec, vec, vec],
        out_shape=[jax.ShapeDtypeStruct((8, LANES), f32), jax.ShapeDtypeStruct((T, D_MODEL), f32),
                   jax.ShapeDtypeStruct((T, D_MODEL), bf16), jax.ShapeDtypeStruct((T, D_MODEL), bf16),
                   jax.ShapeDtypeStruct((T, D_MODEL), bf16), vshape, vshape, vshape, vshape],
        compiler_params=_cparams(("arbitrary",), VMEM_MID),
    )(h2, u2, p, tgt, gath, b_pg, w_ple, g_ple, g_fin, g_pg)


def _mlp_bwd(dh2, r, gath, h1, g):
    T = dh2.shape[0]
    tm = min(512, T)
    nb = D_FF // D_MODEL

    def body(dh2_ref, r_ref, wd_ref, wu_ref, h1_ref, g_ref, dhp_ref, dh1_ref, dh1b_ref, dg_ref, acc_ref):
        i, b = pl.program_id(0), pl.program_id(1)

        @pl.when(b == 0)
        def _():
            acc_ref[...] = jnp.zeros_like(acc_ref)

        @pl.when((b == 0) & (i == 0))
        def _():
            dg_ref[...] = jnp.zeros_like(dg_ref)

        dact = _dot_nt(dh2_ref[...].astype(bf16), wd_ref[0])
        dhp = (dact * 2.0 * r_ref[...].astype(f32)).astype(bf16)
        dhp_ref[...] = dhp
        acc_ref[...] += _dot_nt(dhp, wu_ref[0])

        @pl.when(b == nb - 1)
        def _():
            dx, dg = _rms_bwd(acc_ref[...], h1_ref[...], g_ref[...])
            dg_ref[...] += dg
            dh1 = dh2_ref[...] + dx
            dh1_ref[...] = dh1
            dh1b_ref[...] = dh1.astype(bf16)

    row = pl.BlockSpec((tm, D_MODEL), lambda i, b: (i, 0))
    return _pallas(
        body, name="mlp_bwd", grid=(T // tm, nb),
        in_specs=[row, pl.BlockSpec((tm, D_MODEL), lambda i, b: (i, b)), _w_down_spec(), _w_up_spec(), row,
                  _full((1, D_MODEL))],
        out_specs=[pl.BlockSpec((tm, D_MODEL), lambda i, b: (i, b)), row, row, _full((1, D_MODEL))],
        out_shape=[jax.ShapeDtypeStruct((T, D_FF), bf16), jax.ShapeDtypeStruct((T, D_MODEL), f32),
                   jax.ShapeDtypeStruct((T, D_MODEL), bf16), jax.ShapeDtypeStruct((1, D_MODEL), f32)],
        scratch_shapes=[pltpu.VMEM((tm, D_MODEL), f32)],
        compiler_params=_cparams(("arbitrary", "arbitrary"), VMEM_MID),
    )(dh2, r, gath, gath, h1, g)


def _out_proj_bwd(dh1, gath, co, ln_g, ln_b, rider=None):
    T = dh1.shape[0]
    tm = min(512, T)

    def body(dh_ref, w_ref, co_ref, g_ref, b_ref, dys_ref, dco_ref, dg_ref, db_ref):
        @pl.when(pl.program_id(0) == 0)
        def _():
            dg_ref[...] = jnp.zeros_like(dg_ref)
            db_ref[...] = jnp.zeros_like(db_ref)

        dhb = dh_ref[...].astype(bf16)
        dys_ref[...] = jnp.concatenate([_dot_nt(dhb, w_ref[0]), _dot_nt(dhb, w_ref[1])], axis=1)
        dyc = jnp.concatenate([_dot_nt(dhb, w_ref[2]), _dot_nt(dhb, w_ref[3])], axis=1)
        co = co_ref[...]
        mu = jnp.mean(co, axis=-1, keepdims=True)
        xc = co - mu
        rstd = lax.rsqrt(jnp.mean(xc * xc, axis=-1, keepdims=True) + EPS)
        xh = xc * rstd
        yn = xh * g_ref[...] + b_ref[...]
        dyn = dyc * _dsilu(yn)
        dg_ref[...] += jnp.sum(dyn * xh, axis=0, keepdims=True)
        db_ref[...] += jnp.sum(dyn, axis=0, keepdims=True)
        dxh = dyn * g_ref[...]
        dco_ref[...] = rstd * (dxh - jnp.mean(dxh, axis=-1, keepdims=True)
                               - xh * jnp.mean(dxh * xh, axis=-1, keepdims=True))

    row = pl.BlockSpec((tm, D_MODEL), lambda i: (i, 0))
    vec = _full((1, CONF_WIDTH))
    vshape = jax.ShapeDtypeStruct((1, CONF_WIDTH), f32)
    return _call(
        body, (dh1, gath, co, ln_g, ln_b), name="out_proj_bwd", grid=(T // tm,),
        in_specs=[row, _w_out_spec(), row, vec, vec],
        out_specs=[row, row, vec, vec],
        out_shape=[jax.ShapeDtypeStruct((T, SSD_WIDTH), f32), jax.ShapeDtypeStruct((T, CONF_WIDTH), f32), vshape, vshape],
        params=_cparams(("arbitrary",), VMEM_MID), rider=rider)


def _bwd_offsets(K):
    return [K - 1 - k for k in range(K)]


def _next_halo_spec(hb, tm, C, T):
    return pl.BlockSpec((hb, C), lambda i: (jnp.minimum((i + 1) * (tm // hb), T // hb - 1), 0))


DW_RB = 8
DW_UNROLL = 4
DW_ACC_VREGS = 32


def _conv_dw(dw_ref, bufd_ref, bufx_ref, phx_ref, offs_x, tm, C):
    K = len(offs_x)
    group = max(1, DW_ACC_VREGS // (C // LANES))
    for k0 in range(0, K, group):
        ks = list(range(k0, min(k0 + group, K)))

        def step(i, accs, ks=ks):
            for u in range(DW_UNROLL):
                r0 = pl.multiple_of((i * DW_UNROLL + u) * DW_RB, DW_RB)
                d = bufd_ref[pl.ds(r0, DW_RB), :]
                accs = tuple(acc + _window(bufx_ref, phx_ref, offs_x, offs_x[k], r0, DW_RB) * d
                             for k, acc in zip(ks, accs))
            return accs

        accs = lax.fori_loop(0, tm // (DW_RB * DW_UNROLL), step, tuple(jnp.zeros((DW_RB, C), f32) for _ in ks))
        for k, acc in zip(ks, accs):
            dw_ref[k:k + 1, :] += jnp.sum(acc, axis=0, keepdims=True)


def _fill_bwd_buffers(dcur_ref, dnext_ref, xcur_ref, xprev_ref, bufd_ref, bufx_ref, phd_ref, phx_ref, offs_d, offs_x,
                      hb, tm, first, last):
    bufd_ref[0:tm, :] = dcur_ref[...]
    bufd_ref[tm:tm + hb, :] = dnext_ref[...] * jnp.where(last, 0.0, 1.0)
    bufx_ref[0:hb, :] = xprev_ref[...] * jnp.where(first, 0.0, 1.0)
    bufx_ref[hb:hb + tm, :] = xcur_ref[...]
    _make_phases(bufd_ref, phd_ref, offs_d, tm)
    _make_phases(bufx_ref, phx_ref, offs_x, tm)


def _ssd_conv_bwd(dpre, xbc, w):
    T, C = xbc.shape
    K, hb = SSD_CONV, 8
    tm = min(256, T)
    nt = T // tm
    offs_d, offs_x = _bwd_offsets(K), _fwd_offsets(K, hb)

    def body(dcur_ref, dnext_ref, xcur_ref, xprev_ref, w_ref, dx_ref, dw_ref, db_ref, bufd_ref, bufx_ref, phd_ref, phx_ref):
        i = pl.program_id(0)

        @pl.when(i == 0)
        def _():
            dw_ref[...] = jnp.zeros_like(dw_ref)
            db_ref[...] = jnp.zeros_like(db_ref)

        _fill_bwd_buffers(dcur_ref, dnext_ref, xcur_ref, xprev_ref, bufd_ref, bufx_ref, phd_ref, phx_ref, offs_d, offs_x,
                          hb, tm, i == 0, i == nt - 1)

        def chunk(j, carry):
            r0 = pl.multiple_of(j * CONV_RB, CONV_RB)
            dx_ref[pl.ds(r0, CONV_RB), :] = _conv_rows(w_ref, bufd_ref, phd_ref, offs_d, r0, CONV_RB).astype(bf16)
            return carry

        lax.fori_loop(0, tm // CONV_RB, chunk, 0)
        _conv_dw(dw_ref, bufd_ref, bufx_ref, phx_ref, offs_x, tm, C)
        db_ref[...] += jnp.sum(dcur_ref[...], axis=0, keepdims=True)

    row = pl.BlockSpec((tm, C), lambda i: (i, 0))
    return _pallas(
        body, name="ssd_conv_bwd", grid=(nt,),
        in_specs=[row, _next_halo_spec(hb, tm, C, T), row, _prev_halo_spec(hb, tm, C), _full((SUBLANES * K, C))],
        out_specs=[row, _full((8, C)), _full((1, C))],
        out_shape=[jax.ShapeDtypeStruct((T, C), bf16), jax.ShapeDtypeStruct((8, C), f32), jax.ShapeDtypeStruct((1, C), f32)],
        scratch_shapes=[pltpu.VMEM((tm + hb, C), f32), pltpu.VMEM((hb + tm, C), f32),
                        pltpu.VMEM(_phase_shape(offs_d, tm, C), f32),
                        pltpu.VMEM(_phase_shape(offs_x, tm, C), f32)],
        compiler_params=_cparams(("arbitrary",), VMEM_BIG),
    )(dpre, dpre, xbc, xbc, _sublane_rows(w))


def _conf_conv_bwd(dco, v, w, cv, cg, rider=None):
    T, C = v.shape
    K, hb = CONF_KERNEL, 32
    tm = min(256, T)
    nt = T // tm
    offs_d, offs_x = _bwd_offsets(K), _fwd_offsets(K, hb)

    def body(dcur_ref, dnext_ref, vcur_ref, vprev_ref, w_ref, cv_ref, cg_ref, dcv_ref, dcg_ref, dw_ref, db_ref,
             bufd_ref, bufx_ref, phd_ref, phx_ref):
        i = pl.program_id(0)

        @pl.when(i == 0)
        def _():
            dw_ref[...] = jnp.zeros_like(dw_ref)
            db_ref[...] = jnp.zeros_like(db_ref)

        _fill_bwd_buffers(dcur_ref, dnext_ref, vcur_ref, vprev_ref, bufd_ref, bufx_ref, phd_ref, phx_ref, offs_d, offs_x,
                          hb, tm, i == 0, i == nt - 1)

        def chunk(j, carry):
            r0 = pl.multiple_of(j * CONV_RB, CONV_RB)
            rows = pl.ds(r0, CONV_RB)
            dv = _conv_rows(w_ref, bufd_ref, phd_ref, offs_d, r0, CONV_RB)
            s = _sigmoid(cg_ref[rows, :])
            dcv_ref[rows, :] = (dv * s).astype(bf16)
            dcg_ref[rows, :] = (dv * cv_ref[rows, :] * s * (1.0 - s)).astype(bf16)
            return carry

        lax.fori_loop(0, tm // CONV_RB, chunk, 0)
        _conv_dw(dw_ref, bufd_ref, bufx_ref, phx_ref, offs_x, tm, C)
        db_ref[...] += jnp.sum(dcur_ref[...], axis=0, keepdims=True)

    row = pl.BlockSpec((tm, C), lambda i: (i, 0))
    return _call(
        body, (dco, dco, v, v, _sublane_rows(w), cv, cg), name="conf_conv_bwd", grid=(nt,),
        in_specs=[row, _next_halo_spec(hb, tm, C, T), row, _prev_halo_spec(hb, tm, C), _full((SUBLANES * K, C)), row, row],
        out_specs=[row, row, _full((32, C)), _full((1, C))],
        out_shape=[jax.ShapeDtypeStruct((T, C), bf16), jax.ShapeDtypeStruct((T, C), bf16),
                   jax.ShapeDtypeStruct((32, C), f32), jax.ShapeDtypeStruct((1, C), f32)],
        scratch_shapes=[pltpu.VMEM((tm + hb, C), f32), pltpu.VMEM((hb + tm, C), f32),
                        pltpu.VMEM(_phase_shape(offs_d, tm, C), f32),
                        pltpu.VMEM(_phase_shape(offs_x, tm, C), f32)],
        params=_cparams(("arbitrary",), VMEM_BIG), rider=rider)


def _ssd_bwd(dys, y, z, pre, dtr, sprev, dtb, alog, dskip_e, gn):
    T = pre.shape[0]
    nc = T // CHUNK
    GW = SSD_WIDTH // 2

    def body(dys_ref, y_ref, z_ref, pre_ref, dtr_ref, sp_ref, dtb_ref, alog_ref, de_ref, gn_ref,
             dz_ref, dpre_ref, ddtr_ref, dgn_ref, dd_ref, dal_ref, ddtb_ref, ds_ref):
        @pl.when(pl.program_id(0) == 0)
        def _():
            ds_ref[...] = jnp.zeros_like(ds_ref)
            dgn_ref[...] = jnp.zeros_like(dgn_ref)
            dd_ref[...] = jnp.zeros_like(dd_ref)
            dal_ref[...] = jnp.zeros_like(dal_ref)
            ddtb_ref[...] = jnp.zeros_like(ddtb_ref)

        e = _head_matrix()
        pre = pre_ref[...]
        dtr_b = dtr_ref[...] + dtb_ref[...]
        q = _ssd_chunk_common(pre, dtr_ref[...], dtb_ref[...], alog_ref[...], e)
        cs, tri, xc, xd, xs, dt = q["cs"], q["tri"], q["xc"], q["xd"], q["xs"], q["dt"]
        cs_t = cs.T
        st = sp_ref[0]
        dsn = ds_ref[...]
        lane = lax.broadcasted_iota(jnp.int32, (1, LANES), 1)
        halves = (lane < HEAD_DIM, lane >= HEAD_DIM)
        row_i = lax.broadcasted_iota(jnp.int32, (CHUNK, CHUNK), 0)
        col_i = lax.broadcasted_iota(jnp.int32, (CHUNK, CHUNK), 1)
        tri_t = col_i >= row_i

        y = y_ref[...]
        zz = z_ref[...]
        sz = _sigmoid(zz)
        silu_z = zz * sz
        v = y * silu_z
        dout = dys_ref[...]
        gn_v = gn_ref[...]
        dv, vh = [], []
        for g in range(2):
            vg = _group(v, g, GW)
            rstd = lax.rsqrt(jnp.mean(vg * vg, axis=-1, keepdims=True) + EPS)
            vhg = vg * rstd
            dvh = _group(dout, g, GW) * _group(gn_v, g, GW)
            dv.append(rstd * (dvh - vhg * jnp.mean(dvh * vhg, axis=-1, keepdims=True)))
            vh.append(vhg)
        dv = jnp.concatenate(dv, axis=1)
        dgn_ref[...] += jnp.sum(dout * jnp.concatenate(vh, axis=1), axis=0, keepdims=True)
        dy = dv * silu_z
        dz_ref[...] = (dv * y * (sz * (1.0 + zz * (1.0 - sz)))).astype(bf16)

        dd_row = jnp.sum(dy * xs, axis=0, keepdims=True)
        dd_ref[...] += _contract(jnp.broadcast_to(dd_row, (8, SSD_WIDTH)), e)[0:1, :]
        dxs = dy * de_ref[...]

        dz_in = dy * q["ecs_e"]
        g_mat, gt_mat, dcm, dbm, dsp, dxd, y_off = [], [], [], [], [], [], []
        bgs, cgs = [], []
        for g in range(2):
            bg = _group(q["bm"], g, SSD_STATE)
            cg = _group(q["cm"], g, SSD_STATE)
            bgb, cgb = bg.astype(bf16), cg.astype(bf16)
            bgs.append(bgb)
            cgs.append(cgb)
            stg = _group(st, g, GW).astype(bf16)
            dsng = _group(dsn, g, GW).astype(bf16)
            dzg = _group(dz_in, g, GW).astype(bf16)
            g_mat.append(_dot_nt(cgb, bgb))
            gt_mat.append(_dot_nt(bgb, cgb))
            y_off.append(_dot(cgb, stg))
            dcm.append(_dot_nt(dzg, stg))
            dsp.append(_dot(cg.T.astype(bf16), dzg))
            dbm.append(_dot_nt(_group(xd, g, GW).astype(bf16), dsng))
            dxd.append(_dot(bgb, dsng))
        y_off = jnp.concatenate(y_off, axis=1) * q["ecs_e"]
        dxd = jnp.concatenate(dxd, axis=1)
        ds_ref[...] = dsn * q["cd_e"] + jnp.concatenate(dsp, axis=1)
        dcd_row = jnp.sum(dsn * st, axis=0, keepdims=True) * q["cd_e"]
        t_e = dxd * xd
        dcs = _contract(dy * y_off - t_e, e)
        last_row = _contract(jnp.broadcast_to(dcd_row + jnp.sum(t_e, axis=0, keepdims=True), (8, SSD_WIDTH)), e)[0:1, :]
        dxc_state = dxd * q["dte_e"]

        dg_acc = [jnp.zeros((CHUNK, CHUNK), f32), jnp.zeros((CHUNK, CHUNK), f32)]
        dgt_acc = [jnp.zeros((CHUNK, CHUNK), f32), jnp.zeros((CHUNK, CHUNK), f32)]
        dxc_pairs = []
        for j in range(SSD_HEADS // 2):
            dyp_f = dy[:, j * LANES:(j + 1) * LANES]
            xcp_f = xc[:, j * LANES:(j + 1) * LANES]
            acc = jnp.zeros((CHUNK, LANES), f32)
            for hh in range(2):
                h = 2 * j + hh
                g = h // 8
                dyp = jnp.where(halves[hh], dyp_f, 0.0).astype(bf16)
                xcp = jnp.where(halves[hh], xcp_f, 0.0).astype(bf16)
                lm = jnp.exp(jnp.where(tri, cs[:, h:h + 1] - cs_t[h:h + 1, :], -1e30))
                lm_t = jnp.exp(jnp.where(tri_t, cs_t[h:h + 1, :] - cs[:, h:h + 1], -1e30))
                dm = _dot_nt(dyp, xcp) * lm
                dm_t = _dot_nt(xcp, dyp) * lm_t
                acc = acc + _dot((gt_mat[g] * lm_t).astype(bf16), dyp)
                dg_acc[g] = dg_acc[g] + dm
                dgt_acc[g] = dgt_acc[g] + dm_t
                qd = jnp.sum(dm * g_mat[g] - dm_t * gt_mat[g], axis=1, keepdims=True)
                dcs = dcs + qd * (lane == h).astype(f32)
            dxc_pairs.append(acc)
        dxc = jnp.concatenate(dxc_pairs, axis=1) + dxc_state
        for g in range(2):
            dcm[g] = dcm[g] + _dot(dg_acc[g].astype(bf16), bgs[g])
            dbm[g] = dbm[g] + _dot(dgt_acc[g].astype(bf16), cgs[g])

        dxs = dxs + dxc * q["dt_e"]
        ddt = _contract(dxc * xs, e)
        dcs = dcs + jnp.where(row_i == CHUNK - 1, jnp.broadcast_to(last_row, (CHUNK, LANES)), 0.0)
        da = jnp.dot(tri_t.astype(f32), dcs, precision=lax.Precision.HIGHEST, preferred_element_type=f32)
        ddt = ddt + da * q["a_neg"]
        dal_ref[...] += jnp.sum(da * dt, axis=0, keepdims=True) * q["a_neg"]
        ddtr = ddt * _sigmoid(dtr_b) * (lane < SSD_HEADS).astype(f32)
        ddtb_ref[...] += jnp.sum(ddtr, axis=0, keepdims=True)
        ddtr_ref[...] = ddtr.astype(bf16)

        dact = jnp.concatenate([dxs, dbm[0], dbm[1], dcm[0], dcm[1]], axis=1)
        dpre_ref[...] = dact * _dsilu(pre)

    rev = lambda n: pl.BlockSpec((CHUNK, n), lambda c: (nc - 1 - c, 0))
    vec = _full((1, LANES))
    vshape = jax.ShapeDtypeStruct((1, LANES), f32)
    return _pallas(
        body, name="ssd_bwd", grid=(nc,),
        in_specs=[rev(SSD_WIDTH), rev(SSD_WIDTH), rev(SSD_WIDTH), rev(XBC_WIDTH), rev(LANES),
                  pl.BlockSpec((1, SSD_STATE, SSD_WIDTH), lambda c: (nc - 1 - c, 0, 0)),
                  vec, vec, _full((1, SSD_WIDTH)), _full((1, SSD_WIDTH))],
        out_specs=[rev(SSD_WIDTH), rev(XBC_WIDTH), rev(LANES), _full((1, SSD_WIDTH)), vec, vec, vec],
        out_shape=[jax.ShapeDtypeStruct((T, SSD_WIDTH), bf16), jax.ShapeDtypeStruct((T, XBC_WIDTH), f32),
                   jax.ShapeDtypeStruct((T, LANES), bf16), jax.ShapeDtypeStruct((1, SSD_WIDTH), f32),
                   vshape, vshape, vshape],
        scratch_shapes=[pltpu.VMEM((SSD_STATE, SSD_WIDTH), f32)],
        compiler_params=_cparams(("arbitrary",), VMEM_MID),
    )(dys, y, z, pre, dtr, sprev, dtb, alog, dskip_e, gn)


def _in_proj_bwd(dz, dxbc, dcv, dcg, ddt, gin, x, dh1, g):
    T = x.shape[0]
    tm = min(256, T)

    def body(dz_ref, dx_ref, dcv_ref, dcg_ref, ddt_ref, gin_ref, x_ref, dh_ref, g_ref, gx_ref, dg_ref, wt_ref):
        @pl.when(pl.program_id(0) == 0)
        def _():
            dg_ref[...] = jnp.zeros_like(dg_ref)
            _assemble_w_in_t(gin_ref, wt_ref)

        du = (_dot(dz_ref[...], wt_ref[0:O_XBC, :]) + _dot(dx_ref[...], wt_ref[O_XBC:O_DT, :])
              + _dot(dcv_ref[...], wt_ref[O_CV:O_CG, :]) + _dot(dcg_ref[...], wt_ref[O_CG:IN_WIDTH, :])
              + _dot(ddt_ref[...], wt_ref[O_DT:O_DT + LANES, :]))
        dx, dg = _rms_bwd(du, x_ref[...], g_ref[...])
        dg_ref[...] += dg
        gx_ref[...] = dh_ref[...] + dx

    row = lambda n: pl.BlockSpec((tm, n), lambda i: (i, 0))
    return _pallas(
        body, name="in_proj_bwd", grid=(T // tm,),
        in_specs=[row(SSD_WIDTH), row(XBC_WIDTH), row(CONF_WIDTH), row(CONF_WIDTH), row(LANES), _full(gin.shape),
                  row(D_MODEL), row(D_MODEL), _full((1, D_MODEL))],
        out_specs=[row(D_MODEL), _full((1, D_MODEL))],
        out_shape=[jax.ShapeDtypeStruct((T, D_MODEL), f32), jax.ShapeDtypeStruct((1, D_MODEL), f32)],
        scratch_shapes=[pltpu.VMEM((IN_WIDTH, D_MODEL), bf16)],
        compiler_params=_cparams(("arbitrary",), VMEM_BIG),
    )(dz, dxbc, dcv, dcg, ddt, gin, x, dh1, g)


def _weight_grad(a, g, name, square=False, slab=None, place=None, tk=512):
    T, K = a.shape
    N = g.shape[1]
    tk = min(tk, K)
    tn = 1024 if N % 1024 == 0 else min(512, N)
    tt = min(2048, T)

    def body(a_ref, g_ref, *rest):
        o_ref = rest[-1]
        acc = _dot_tn(_operand(a_ref[...]), g_ref[...].astype(bf16))
        t = pl.program_id(2)
        shaped = acc if slab is None else acc[None]

        @pl.when(t == 0)
        def _():
            o_ref[...] = shaped

        @pl.when(t > 0)
        def _():
            o_ref[...] += shaped

    def _operand(av):
        if square:
            av = av.astype(f32)
            av = av * av
        return av.astype(bf16)

    in_specs = [pl.BlockSpec((tt, tk), lambda i, j, t: (t, i)), pl.BlockSpec((tt, tn), lambda i, j, t: (t, j))]
    grid = (K // tk, N // tn, T // tt)
    params = _cparams(("parallel", "parallel", "arbitrary"), VMEM_MID)
    if slab is None:
        return _pallas(
            body, pin_bytes=PIN_SMALL, name=name, grid=grid, in_specs=in_specs,
            out_specs=pl.BlockSpec((tk, tn), lambda i, j, t: (i, j)),
            out_shape=jax.ShapeDtypeStruct((K, N), f32), compiler_params=params,
        )(a, g)
    return _pallas(
        body, pin_bytes=PIN_SMALL, name=name, grid=grid, in_specs=in_specs + [ANY],
        out_specs=pl.BlockSpec((1, tk, tn), lambda i, j, t: place(i, j)),
        out_shape=jax.ShapeDtypeStruct(slab.shape, f32), input_output_aliases={2: 0}, compiler_params=params,
    )(a, g, slab)


def _place():
    return lax.axis_index("x"), lax.axis_index("y"), lax.axis_index("c")


def _other_chips(x, y):
    return [(1 - x, y), (x, 1 - y), (1 - x, 1 - y)]


def _remote(src, dst, ssem, rsem, dev):
    return pltpu.make_async_remote_copy(src_ref=src, dst_ref=dst, send_sem=ssem, recv_sem=rsem, device_id=dev,
                                        device_id_type=MESH)


def _gather_weights(arrays, convw):
    n = len(arrays)
    halves = tuple(a.shape[1] // 2 for a in arrays)

    def body(*refs):
        cw_ref, cwo_ref = refs[n], refs[2 * n + 1]
        ssem, rsem, lsem = refs[2 * n + 2:]
        triples = tuple(zip(refs[:n], refs[n + 1:2 * n + 1], halves))
        x, y, c = _place()
        me_b = 2 * x + y
        sib = (x, y, 1 - c)
        chips = _other_chips(x, y)
        loc = pltpu.make_async_copy(cw_ref, cwo_ref.at[me_b], lsem)
        loc.start()
        sends = []
        for j, (src, dst, h) in enumerate(triples):
            mine = pl.ds(c * h, h)
            for k, (px, py) in enumerate(chips):
                s = 6 * j + k
                sends.append(_remote(src.at[me_b, mine], dst.at[me_b, mine], ssem.at[s], rsem.at[s], (px, py, c)))
        for k, (px, py) in enumerate(chips):
            sends.append(_remote(cw_ref, cwo_ref.at[me_b], ssem.at[6 * n + k], rsem.at[6 * n + k], (px, py, c)))
        for cp in sends:
            cp.start()
        for j, (src, dst, h) in enumerate(triples):
            mine = pl.ds(c * h, h)
            for k, (px, py) in enumerate(chips):
                b = 2 * px + py
                s = 6 * j + k
                _remote(src.at[b, mine], dst.at[b, mine], ssem.at[s], rsem.at[s], (px, py, c)).wait_recv()
                fw = _remote(dst.at[b, mine], dst.at[b, mine], ssem.at[s + 3], rsem.at[s + 3], sib)
                fw.start()
                sends.append(fw)
        for k, (px, py) in enumerate(chips):
            b = 2 * px + py
            _remote(cw_ref, cwo_ref.at[b], ssem.at[6 * n + k], rsem.at[6 * n + k], (px, py, c)).wait_recv()
        for j, (src, dst, h) in enumerate(triples):
            theirs = pl.ds((1 - c) * h, h)
            for k, (px, py) in enumerate(chips):
                b = 2 * px + py
                s = 6 * j + k + 3
                _remote(src.at[b, theirs], dst.at[b, theirs], ssem.at[s], rsem.at[s], sib).wait_recv()
        for cp in sends:
            cp.wait_send()
        loc.wait()

    return _pallas(
        body, name="gather_weights", in_specs=[ANY] * (n + 1), out_specs=[ANY] * (n + 1),
        out_shape=[jax.ShapeDtypeStruct(a.shape, bf16) for a in arrays]
        + [jax.ShapeDtypeStruct((N_CHIPS, CONVW_ROWS, D_MODEL), f32)],
        input_output_aliases={j: j for j in range(n)},
        scratch_shapes=[pltpu.SemaphoreType.DMA((6 * n + 3,)), pltpu.SemaphoreType.DMA((6 * n + 3,)),
                        pltpu.SemaphoreType.DMA(())],
    )(*arrays, convw)


def _gather_rider(gath0, lo, n):
    h = gath0.shape[1] // 2

    def copies(rins, routs, ssem, rsem, sending):
        (g_ref,), (o_ref,) = rins, routs
        x, y, c = _place()
        mine = pl.ds(c * h + lo, n)
        for k, (px, py) in enumerate(_other_chips(x, y)):
            b = 2 * x + y if sending else 2 * px + py
            yield _remote(g_ref.at[b, mine], o_ref.at[b, mine], ssem.at[k], rsem.at[k], (px, py, c))

    def start(*refs):
        for cp in copies(*refs, sending=True):
            cp.start()

    def finish(*refs):
        for cp in copies(*refs, sending=False):
            cp.wait()

    return _Rider([gath0], [jax.ShapeDtypeStruct(gath0.shape, gath0.dtype)], {0: 0}, 3, start, finish)


def _forward_to_sibling(gath):
    h = gath.shape[1] // 2

    def body(g_ref, o_ref, ssem, rsem):
        x, y, c = _place()
        sib = (x, y, 1 - c)
        mine, theirs = pl.ds(c * h, h), pl.ds((1 - c) * h, h)
        blocks = [2 * px + py for px, py in _other_chips(x, y)]
        sends = [_remote(g_ref.at[b, mine], o_ref.at[b, mine], ssem.at[k], rsem.at[k], sib) for k, b in enumerate(blocks)]
        for cp in sends:
            cp.start()
        for k, b in enumerate(blocks):
            _remote(g_ref.at[b, theirs], o_ref.at[b, theirs], ssem.at[k], rsem.at[k], sib).wait_recv()
        for cp in sends:
            cp.wait_send()

    return _pallas(
        body, name="forward_to_sibling", in_specs=[ANY], out_specs=ANY,
        out_shape=jax.ShapeDtypeStruct(gath.shape, gath.dtype), input_output_aliases={0: 0},
        scratch_shapes=[pltpu.SemaphoreType.DMA((3,)), pltpu.SemaphoreType.DMA((3,))],
    )(gath)


def _swap_copy(g_ref, r_ref, ssem, rsem):
    x, y, c = _place()
    h = r_ref.shape[1]
    return _remote(g_ref.at[:, pl.ds((1 - c) * h, h), :], r_ref, ssem.at[0], rsem.at[0], (x, y, 1 - c))


def _swap_rider(g):
    def start(rins, routs, ssem, rsem):
        _swap_copy(rins[0], routs[0], ssem, rsem).start()

    def finish(rins, routs, ssem, rsem):
        _swap_copy(rins[0], routs[0], ssem, rsem).wait()

    return _Rider([g], [jax.ShapeDtypeStruct((N_CHIPS, g.shape[1] // 2, g.shape[2]), g.dtype)], {}, 1, start, finish)


def _swap_halves(g):
    def body(g_ref, r_ref, ssem, rsem):
        cp = _swap_copy(g_ref, r_ref, ssem, rsem)
        cp.start()
        cp.wait()

    return _pallas(
        body, name="swap_halves", in_specs=[ANY], out_specs=ANY,
        out_shape=jax.ShapeDtypeStruct((N_CHIPS, g.shape[1] // 2, g.shape[2]), g.dtype),
        scratch_shapes=[pltpu.SemaphoreType.DMA((1,)), pltpu.SemaphoreType.DMA((1,))],
    )(g)


def _chip_sum(cidx, gslab, recv, name):
    half, C = recv.shape[1:]
    tr = half // 2 if (half // 2) % 16 == 0 else half

    def body(c_ref, g_ref, r_ref, o_ref):
        o_ref[...] = (g_ref[...] + r_ref[...]).astype(bf16)

    return _pallas(
        body, name=name,
        grid_spec=pltpu.PrefetchScalarGridSpec(
            num_scalar_prefetch=1, grid=(N_CHIPS, half // tr),
            in_specs=[pl.BlockSpec((1, tr, C), lambda b, i, c_ref: (b, c_ref[0] * (half // tr) + i, 0)),
                      pl.BlockSpec((1, tr, C), lambda b, i, c_ref: (b, i, 0))],
            out_specs=pl.BlockSpec((1, tr, C), lambda b, i, c_ref: (b, i, 0))),
        out_shape=jax.ShapeDtypeStruct((N_CHIPS, half, C), bf16),
        compiler_params=_cparams(("parallel", "parallel"), VMEM_MID),
    )(cidx, gslab, recv)


def _exchange_rider(h):
    def copies(rins, routs, ssem, rsem):
        x, y, c = _place()
        for k, (px, py) in enumerate(_other_chips(x, y)):
            yield _remote(rins[0].at[2 * px + py], routs[0].at[k], ssem.at[k], rsem.at[k], (px, py, c))

    def start(*refs):
        for cp in copies(*refs):
            cp.start()

    def finish(*refs):
        for cp in copies(*refs):
            cp.wait()

    return _Rider([h], [jax.ShapeDtypeStruct((3,) + h.shape[1:], h.dtype)], {}, 3, start, finish)


def _exchange(hb, small):
    def body(hb_ref, sm_ref, rb_ref, all_ref, ssem, rsem, lsem):
        x, y, c = _place()
        me = 4 * x + 2 * y + c
        chips = _other_chips(x, y)
        loc = pltpu.make_async_copy(sm_ref, all_ref.at[me], lsem)
        loc.start()
        sends = []
        for k, (px, py) in enumerate(chips):
            sends.append(_remote(hb_ref.at[2 * px + py], rb_ref.at[k], ssem.at[3 + k], rsem.at[3 + k], (px, py, c)))
        peers = []
        for r in range(1, N_DEV):
            peer = ((1 - x) if r & 4 else x, (1 - y) if r & 2 else y, (1 - c) if r & 1 else c)
            peers.append(peer)
            sends.append(_remote(sm_ref, all_ref.at[me], ssem.at[5 + r], rsem.at[5 + r], peer))
        for cp in sends:
            cp.start()
        for k, (px, py) in enumerate(chips):
            _remote(hb_ref.at[0], rb_ref.at[k], ssem.at[3 + k], rsem.at[3 + k], (px, py, c)).wait_recv()
        for r, peer in zip(range(1, N_DEV), peers):
            pid = 4 * peer[0] + 2 * peer[1] + peer[2]
            _remote(sm_ref, all_ref.at[pid], ssem.at[5 + r], rsem.at[5 + r], peer).wait_recv()
        for cp in sends:
            cp.wait_send()
        loc.wait()

    return _pallas(
        body, name="exchange", in_specs=[ANY, ANY], out_specs=[ANY, ANY],
        out_shape=[jax.ShapeDtypeStruct((3,) + hb.shape[1:], bf16),
                   jax.ShapeDtypeStruct((N_DEV, SMALL_ROWS, D_MODEL), f32)],
        scratch_shapes=[pltpu.SemaphoreType.DMA((13,)), pltpu.SemaphoreType.DMA((13,)), pltpu.SemaphoreType.DMA(())],
    )(hb, small)


def _final_sum(idx, gslab, recv_sib, recv_ici, name):
    half, C = recv_sib.shape[1:]
    tr = half // 2 if (half // 2) % 16 == 0 else half

    def body(i_ref, g_ref, r_ref, p_ref, o_ref):
        acc = g_ref[0] + r_ref[0]
        for k in range(3):
            acc = acc + p_ref[k].astype(f32)
        o_ref[...] = acc

    return _pallas(
        body, name=name,
        grid_spec=pltpu.PrefetchScalarGridSpec(
            num_scalar_prefetch=1, grid=(half // tr,),
            in_specs=[pl.BlockSpec((1, tr, C), lambda i, s: (s[1], s[0] * (half // tr) + i, 0)),
                      pl.BlockSpec((1, tr, C), lambda i, s: (s[1], i, 0)),
                      pl.BlockSpec((3, tr, C), lambda i, s: (0, i, 0))],
            out_specs=pl.BlockSpec((tr, C), lambda i, s: (s[0] * (half // tr) + i, 0))),
        out_shape=jax.ShapeDtypeStruct((2 * half, C), f32),
        compiler_params=_cparams(("parallel",), VMEM_MID),
    )(idx, gslab, recv_sib, recv_ici)


def _join_halves(ra, rb):
    ha, hb = ra.shape[0] // 2, rb.shape[0] // 2

    def body(a_ref, b_ref, ao_ref, bo_ref, ssem, rsem):
        x, y, c = _place()
        sib = (x, y, 1 - c)
        mine_a, theirs_a = pl.ds(c * ha, ha), pl.ds((1 - c) * ha, ha)
        mine_b, theirs_b = pl.ds(c * hb, hb), pl.ds((1 - c) * hb, hb)
        ca = _remote(a_ref.at[mine_a], ao_ref.at[mine_a], ssem.at[0], rsem.at[0], sib)
        cb = _remote(b_ref.at[mine_b], bo_ref.at[mine_b], ssem.at[1], rsem.at[1], sib)
        ca.start()
        cb.start()
        _remote(a_ref.at[theirs_a], ao_ref.at[theirs_a], ssem.at[0], rsem.at[0], sib).wait_recv()
        _remote(b_ref.at[theirs_b], bo_ref.at[theirs_b], ssem.at[1], rsem.at[1], sib).wait_recv()
        ca.wait_send()
        cb.wait_send()

    return _pallas(
        body, name="join_halves", in_specs=[ANY, ANY], out_specs=[ANY, ANY],
        out_shape=[jax.ShapeDtypeStruct(ra.shape, f32), jax.ShapeDtypeStruct(rb.shape, f32)],
        input_output_aliases={0: 0, 1: 1},
        scratch_shapes=[pltpu.SemaphoreType.DMA((2,)), pltpu.SemaphoreType.DMA((2,))],
    )(ra, rb)


def _shard_rows(gt):
    def body(g_ref, o_ref):
        for b in range(N_CHIPS):
            o_ref[b, 0:W_IN_ROWS, :] = g_ref[b * W_IN_ROWS:(b + 1) * W_IN_ROWS, :]
            o_ref[b, W_IN_ROWS:W_IN_ROWS_PAD, :] = jnp.zeros((W_IN_ROWS_PAD - W_IN_ROWS, LANES), f32)

    return _pallas(
        body, name="shard_rows", grid=(D_MODEL // LANES,),
        in_specs=[pl.BlockSpec((IN_WIDTH, LANES), lambda i: (0, i))],
        out_specs=pl.BlockSpec((N_CHIPS, W_IN_ROWS_PAD, LANES), lambda i: (0, 0, i)),
        out_shape=jax.ShapeDtypeStruct((N_CHIPS, W_IN_ROWS_PAD, D_MODEL), f32),
        compiler_params=_cparams(("parallel",), VMEM_MID),
    )(gt)


def _sum_small(all_small):
    def body(a_ref, o_ref):
        acc = a_ref[0]
        for d in range(1, N_DEV):
            acc = acc + a_ref[d]
        o_ref[...] = acc

    return _pallas(
        body, name="sum_small", out_shape=jax.ShapeDtypeStruct((SMALL_ROWS, D_MODEL), f32),
    )(all_small)


def _adamw(w, g, m, v, name, g_off=0, by_columns=False):
    R, C = w.shape
    tr = 256 if R % 256 == 0 else R
    assert g_off % tr == 0 and not (by_columns and g_off)
    c1 = 1.0 - ADAM_B1 ** ADAM_STEP
    c2 = 1.0 - ADAM_B2 ** ADAM_STEP

    def body(w_ref, g_ref, m_ref, v_ref, d_ref, mo_ref, vo_ref):
        gg = g_ref[...]
        m2 = ADAM_B1 * m_ref[...] + (1.0 - ADAM_B1) * gg
        v2 = ADAM_B2 * v_ref[...] + (1.0 - ADAM_B2) * (gg * gg)
        mo_ref[...] = m2
        vo_ref[...] = v2
        d_ref[...] = -ADAM_LR * ((m2 / c1) / (jnp.sqrt(v2 / c2) + ADAM_EPS) + ADAM_WD * w_ref[...])

    if by_columns:
        blk = gblk = pl.BlockSpec((R, LANES), lambda i: (0, i))
        grid = (C // LANES,)
    else:
        blk = pl.BlockSpec((tr, C), lambda i: (i, 0))
        gblk = pl.BlockSpec((tr, C), lambda i: (g_off // tr + i, 0))
        grid = (R // tr,)
    shp = jax.ShapeDtypeStruct((R, C), f32)
    return _pallas(
        body, pin_bytes=PIN_SMALL, name=name, grid=grid, in_specs=[blk, gblk, blk, blk], out_specs=[blk] * 3,
        out_shape=[shp] * 3,
        compiler_params=_cparams(("parallel",), VMEM_MID),
    )(w, g, m, v)


def _pad_lanes(v):
    return jnp.pad(v, ((0, 0), (0, LANES - v.shape[1])))


def _local_step(x, p, tgt, gath0, cidx, gin, S):
    dtb = _pad_lanes(S["dt_bias"])
    alog = _pad_lanes(S["A_log"])
    dskip_e = jnp.repeat(S["D_skip"], HEAD_DIM, axis=1)

    early = GATHER_EARLY_ROWS
    u0, z, xbc, cv, cg, dtr, v, gath1 = _in_proj_fwd(x, S["mix_norm_g"], gin, rider=_gather_rider(gath0, 0, early))
    co, yc, gath = _conf_fwd(v, S["conf_dw_w"], S["conf_dw_b"], S["conf_ln_g"], S["conf_ln_b"],
                             rider=_gather_rider(gath1, early, SLAB_A // 2 - early))
    gath = _forward_to_sibling(gath)
    w_ple = jnp.concatenate([_ple_of_slab(gath[b]) for b in range(N_CHIPS)], axis=1)
    pre = _ssd_conv_fwd(xbc, S["ssd_conv_w"], S["ssd_conv_b"])
    y, ys, sprev = _ssd_fwd(pre, dtr, z, dtb, alog, dskip_e, S["ssd_norm_g"])
    h1, u1 = _out_proj_fwd(x, ys, yc, gath, S["mlp_norm_g"])
    r, h2, u2 = _mlp_fwd(h1, u1, gath, S["ple_gate_norm_g"])
    loss, dh2, dh2b, dgp, dep, dg_fin, dg_ple, db_pg, dg_pg = _ple_loss(
        h2, u2, p, tgt, gath, S["b_ple_gate"], w_ple, S["ple_norm_g"], S["final_norm_g"], S["ple_gate_norm_g"])

    npg = D_MODEL // N_CHIPS
    ga = lax.empty((N_CHIPS, SLAB_A, D_MODEL), f32)
    ga = _weight_grad(u2, dgp, "dw_ple_gate", slab=ga, tk=npg, place=lambda i, j: (i, PG_OFF // npg, j))
    ga = _weight_grad(r, dh2b, "dw_down", square=True, slab=ga, place=lambda i, j: (i // 2, DOWN_OFF // 512 + i % 2, j))
    gw_ple = _weight_grad(p, dep, "dw_ple")
    dhp, dh1, dh1b, dg_mlp = _mlp_bwd(dh2, r, gath, h1, S["mlp_norm_g"])
    ga = _weight_grad(u1, dhp, "dw_up", slab=ga, place=lambda i, j: (j, UP_OFF // 512 + i, 0))
    ga = _weight_grad(ys, dh1b, "dw_out_ssd", slab=ga, place=lambda i, j: (i, OUT_OFF // 512, j))
    ga = _weight_grad(yc, dh1b, "dw_out_conf", slab=ga, place=lambda i, j: (2 + i, OUT_OFF // 512, j))
    n_ple = D_MODEL // N_CHIPS
    ple_rows = jnp.stack([_rows(gw_ple[:, b * n_ple:(b + 1) * n_ple]) for b in range(N_CHIPS)], axis=0)
    ga = lax.dynamic_update_slice(ga, ple_rows, (0, PLE_OFF, 0))
    dys, dco, dg_ln, db_ln, recv_a = _out_proj_bwd(dh1, gath, co, S["conf_ln_g"], S["conf_ln_b"], rider=_swap_rider(ga))
    ha = _chip_sum(cidx, ga, recv_a, "chip_sum_a")
    dcv, dcg, dw_conf, db_conf, ici_a = _conf_conv_bwd(dco, v, S["conf_dw_w"], cv, cg, rider=_exchange_rider(ha))
    dz, dpre, ddtr, dg_ssdn, dd, dal, ddtb = _ssd_bwd(dys, y, z, pre, dtr, sprev, dtb, alog, dskip_e, S["ssd_norm_g"])
    dxbc, dw_sconv, db_sconv = _ssd_conv_bwd(dpre, xbc, S["ssd_conv_w"])
    gx, dg_mix = _in_proj_bwd(dz, dxbc, dcv, dcg, ddtr, gin, x, dh1, S["mix_norm_g"])

    gw_in = jnp.concatenate([
        _weight_grad(dz, u0, "dw_in_z"), _weight_grad(dxbc, u0, "dw_in_xbc"),
        _weight_grad(ddtr, u0, "dw_in_dt")[:SSD_HEADS],
        _weight_grad(dcv, u0, "dw_in_cv"), _weight_grad(dcg, u0, "dw_in_cg")], axis=0)
    small = {
        "mix_norm_g": dg_mix, "ssd_conv_w": dw_sconv, "ssd_conv_b": db_sconv, "dt_bias": ddtb, "A_log": dal, "D_skip": dd,
        "ssd_norm_g": dg_ssdn, "conf_dw_w": dw_conf, "conf_dw_b": db_conf, "conf_ln_g": dg_ln, "conf_ln_b": db_ln,
        "mlp_norm_g": dg_mlp, "ple_gate_norm_g": dg_pg, "b_ple_gate": db_pg, "ple_norm_g": dg_ple,
        "final_norm_g": dg_fin, "loss": loss,
    }
    return gx, ga, recv_a, ici_a, gw_in, small


def _rows(a):
    return a.reshape(-1, D_MODEL)


def _pad_rows(a, n):
    flat = a.reshape(-1)
    return jnp.pad(flat, (0, n * D_MODEL - flat.shape[0])).reshape(n, D_MODEL)


def _ple_of_slab(slab):
    return slab[PLE_OFF:PLE_OFF + PLE_ROWS].reshape(PLE_DIM, D_MODEL // N_CHIPS)


ROW_VEC = {"mix_norm_g": 0, "ssd_norm_g": 1, "conf_dw_b": 2, "conf_ln_g": 3, "conf_ln_b": 4, "mlp_norm_g": 5,
           "ple_gate_norm_g": 6, "b_ple_gate": 7, "ple_norm_g": 8, "final_norm_g": 9}
ROW_CONV_B = 10
ROW_HEADS = 12
ROW_CONV_W = 16
ROW_DW = 24
HEAD_LANES = {"dt_bias": 0, "A_log": 1, "D_skip": 2, "loss": 3}
SMALL_ORDER = ("mix_norm_g", "ssd_conv_w", "ssd_conv_b", "dt_bias", "A_log", "D_skip", "ssd_norm_g", "conf_dw_w",
               "conf_dw_b", "conf_ln_g", "conf_ln_b", "mlp_norm_g", "ple_gate_norm_g", "b_ple_gate", "ple_norm_g",
               "final_norm_g")
SPLIT = XBC_WIDTH - D_MODEL


def _pack_small(raw):
    names = list(ROW_VEC) + ["ssd_conv_b", "dt_bias", "A_log", "D_skip", "loss", "ssd_conv_w", "conf_dw_w"]

    def body(*refs):
        r = dict(zip(names, refs[:-1]))
        o_ref = refs[-1]
        o_ref[...] = jnp.zeros_like(o_ref)
        for n, row in ROW_VEC.items():
            o_ref[row:row + 1, :] = r[n][...]
        o_ref[ROW_CONV_B:ROW_CONV_B + 1, :] = r["ssd_conv_b"][:, 0:D_MODEL]
        o_ref[ROW_CONV_B + 1:ROW_CONV_B + 2, 0:SPLIT] = r["ssd_conv_b"][:, D_MODEL:]
        for n, j in HEAD_LANES.items():
            o_ref[ROW_HEADS:ROW_HEADS + 1, j * LANES:(j + 1) * LANES] = r[n][0:1, :]
        for k in range(SSD_CONV):
            o_ref[ROW_CONV_W + 2 * k:ROW_CONV_W + 2 * k + 1, :] = r["ssd_conv_w"][k:k + 1, 0:D_MODEL]
            o_ref[ROW_CONV_W + 2 * k + 1:ROW_CONV_W + 2 * k + 2, 0:SPLIT] = r["ssd_conv_w"][k:k + 1, D_MODEL:]
        o_ref[ROW_DW:ROW_DW + 32, :] = r["conf_dw_w"][...]

    return _pallas(
        body, name="pack_small", out_shape=jax.ShapeDtypeStruct((SMALL_ROWS, D_MODEL), f32),
    )(*[raw[n] for n in names])


def _adamw_small(cidx, tot, w, m, v):
    c1 = 1.0 - ADAM_B1 ** ADAM_STEP
    c2 = 1.0 - ADAM_B2 ** ADAM_STEP
    n_par = len(SMALL_ORDER)

    def shard(full, chip, width):
        out = full[:, 0:width]
        for b in range(1, N_CHIPS):
            out = jnp.where(chip == b, full[:, b * width:(b + 1) * width], out)
        return out

    def grad_of(n, t_ref, chip):
        if n in ROW_VEC:
            return t_ref[ROW_VEC[n]:ROW_VEC[n] + 1, :]
        if n == "ssd_conv_b":
            return jnp.concatenate([t_ref[ROW_CONV_B:ROW_CONV_B + 1, :], t_ref[ROW_CONV_B + 1:ROW_CONV_B + 2, 0:SPLIT]], axis=1)
        if n in HEAD_LANES:
            j = HEAD_LANES[n]
            return t_ref[ROW_HEADS:ROW_HEADS + 1, j * LANES:j * LANES + SSD_HEADS]
        if n == "ssd_conv_w":
            rows = [jnp.concatenate([t_ref[ROW_CONV_W + 2 * k:ROW_CONV_W + 2 * k + 1, :],
                                     t_ref[ROW_CONV_W + 2 * k + 1:ROW_CONV_W + 2 * k + 2, 0:SPLIT]], axis=1)
                    for k in range(SSD_CONV)]
            return shard(jnp.concatenate(rows, axis=0), chip, XBC_WIDTH // N_CHIPS)
        return shard(t_ref[ROW_DW:ROW_DW + CONF_KERNEL, :], chip, CONF_WIDTH // N_CHIPS)

    def body(c_ref, t_ref, *refs):
        ins, outs = refs[:3 * n_par], refs[3 * n_par:]
        chip = c_ref[1]
        for i, n in enumerate(SMALL_ORDER):
            w_ref, m_ref, v_ref = ins[3 * i:3 * i + 3]
            g_ref, d_ref, mo_ref, vo_ref = outs[4 * i:4 * i + 4]
            g = grad_of(n, t_ref, chip)
            m2 = ADAM_B1 * m_ref[...] + (1.0 - ADAM_B1) * g
            v2 = ADAM_B2 * v_ref[...] + (1.0 - ADAM_B2) * (g * g)
            g_ref[...] = g
            mo_ref[...] = m2
            vo_ref[...] = v2
            d_ref[...] = -ADAM_LR * ((m2 / c1) / (jnp.sqrt(v2 / c2) + ADAM_EPS) + ADAM_WD * w_ref[...])

    args, in_specs, out_specs, out_shape = [], [], [], []
    for n in SMALL_ORDER:
        shp = w[n].shape
        spec = pl.BlockSpec(shp, lambda i, c_ref: (0, 0))
        args += [w[n], m[n], v[n]]
        in_specs += [spec] * 3
        out_specs += [spec] * 4
        out_shape += [jax.ShapeDtypeStruct(shp, f32)] * 4
    outs = _pallas(
        body, name="adamw_small",
        grid_spec=pltpu.PrefetchScalarGridSpec(
            num_scalar_prefetch=1, grid=(1,),
            in_specs=[pl.BlockSpec(tot.shape, lambda i, c_ref: (0, 0))] + in_specs, out_specs=out_specs),
        out_shape=out_shape,
    )(cidx, tot, *args)
    grad, delta, new_m, new_v = {}, {}, {}, {}
    for i, n in enumerate(SMALL_ORDER):
        grad[n], delta[n], new_m[n], new_v[n] = outs[4 * i:4 * i + 4]
    return grad, delta, new_m, new_v


BIG = ("w_in", "w_out", "w_up", "w_down", "w_ple_gate", "w_ple")
BIG_A = (("w_up", UP_OFF), ("w_down", DOWN_OFF), ("w_out", OUT_OFF), ("w_ple_gate", PG_OFF))
WEIGHTS = ("mix_norm_g", "w_in", "ssd_conv_w", "ssd_conv_b", "dt_bias", "A_log", "D_skip", "ssd_norm_g", "conf_dw_w",
           "conf_dw_b", "conf_ln_g", "conf_ln_b", "w_out", "mlp_norm_g", "w_up", "w_down", "ple_gate_norm_g",
           "w_ple_gate", "b_ple_gate", "w_ple", "ple_norm_g", "final_norm_g")


def kernel(x, p, mix_norm_g, w_in, ssd_conv_w, ssd_conv_b, dt_bias, A_log, D_skip, ssd_norm_g, conf_dw_w, conf_dw_b, conf_ln_g, conf_ln_b, w_out, mlp_norm_g, w_up, w_down, ple_gate_norm_g, w_ple_gate, b_ple_gate, w_ple, ple_norm_g, final_norm_g, loss_target, m_mix_norm_g, m_w_in, m_ssd_conv_w, m_ssd_conv_b, m_dt_bias, m_A_log, m_D_skip, m_ssd_norm_g, m_conf_dw_w, m_conf_dw_b, m_conf_ln_g, m_conf_ln_b, m_w_out, m_mlp_norm_g, m_w_up, m_w_down, m_ple_gate_norm_g, m_w_ple_gate, m_b_ple_gate, m_w_ple, m_ple_norm_g, m_final_norm_g, v_mix_norm_g, v_w_in, v_ssd_conv_w, v_ssd_conv_b, v_dt_bias, v_A_log, v_D_skip, v_ssd_norm_g, v_conf_dw_w, v_conf_dw_b, v_conf_ln_g, v_conf_ln_b, v_w_out, v_mlp_norm_g, v_w_up, v_w_down, v_ple_gate_norm_g, v_w_ple_gate, v_b_ple_gate, v_w_ple, v_ple_norm_g, v_final_norm_g):
    w = dict(mix_norm_g=mix_norm_g, w_in=w_in, ssd_conv_w=ssd_conv_w, ssd_conv_b=ssd_conv_b, dt_bias=dt_bias, A_log=A_log,
             D_skip=D_skip, ssd_norm_g=ssd_norm_g, conf_dw_w=conf_dw_w, conf_dw_b=conf_dw_b, conf_ln_g=conf_ln_g,
             conf_ln_b=conf_ln_b, w_out=w_out, mlp_norm_g=mlp_norm_g, w_up=w_up, w_down=w_down,
             ple_gate_norm_g=ple_gate_norm_g, w_ple_gate=w_ple_gate, b_ple_gate=b_ple_gate, w_ple=w_ple,
             ple_norm_g=ple_norm_g, final_norm_g=final_norm_g)
    m = dict(mix_norm_g=m_mix_norm_g, w_in=m_w_in, ssd_conv_w=m_ssd_conv_w, ssd_conv_b=m_ssd_conv_b, dt_bias=m_dt_bias,
             A_log=m_A_log, D_skip=m_D_skip, ssd_norm_g=m_ssd_norm_g, conf_dw_w=m_conf_dw_w, conf_dw_b=m_conf_dw_b,
             conf_ln_g=m_conf_ln_g, conf_ln_b=m_conf_ln_b, w_out=m_w_out, mlp_norm_g=m_mlp_norm_g, w_up=m_w_up,
             w_down=m_w_down, ple_gate_norm_g=m_ple_gate_norm_g, w_ple_gate=m_w_ple_gate, b_ple_gate=m_b_ple_gate,
             w_ple=m_w_ple, ple_norm_g=m_ple_norm_g, final_norm_g=m_final_norm_g)
    v = dict(mix_norm_g=v_mix_norm_g, w_in=v_w_in, ssd_conv_w=v_ssd_conv_w, ssd_conv_b=v_ssd_conv_b, dt_bias=v_dt_bias,
             A_log=v_A_log, D_skip=v_D_skip, ssd_norm_g=v_ssd_norm_g, conf_dw_w=v_conf_dw_w, conf_dw_b=v_conf_dw_b,
             conf_ln_g=v_conf_ln_g, conf_ln_b=v_conf_ln_b, w_out=v_w_out, mlp_norm_g=v_mlp_norm_g, w_up=v_w_up,
             w_down=v_w_down, ple_gate_norm_g=v_ple_gate_norm_g, w_ple_gate=v_w_ple_gate, b_ple_gate=v_b_ple_gate,
             w_ple=v_w_ple, ple_norm_g=v_ple_norm_g, final_norm_g=v_final_norm_g)
    xi, yi, ci = lax.axis_index("x"), lax.axis_index("y"), lax.axis_index("c")
    chip = 2 * xi + yi

    slab = jnp.concatenate([w_up[0], w_down[0], w_out[0], w_ple_gate[0], _rows(w_ple[0])], axis=0).astype(bf16)
    gath0 = lax.dynamic_update_slice(lax.empty((N_CHIPS, SLAB_A, D_MODEL), bf16), slab[None], (chip, 0, 0))
    wt_shard = jnp.swapaxes(w_in, 1, 2).astype(bf16)
    wt_shard = jnp.pad(wt_shard, ((0, 0), (0, W_IN_ROWS_PAD - W_IN_ROWS), (0, 0)))
    gin0 = lax.dynamic_update_slice(lax.empty((N_CHIPS, W_IN_ROWS_PAD, D_MODEL), bf16), wt_shard, (chip, 0, 0))
    convw = _pad_rows(jnp.concatenate([ssd_conv_w[0].reshape(-1), conf_dw_w[0].reshape(-1)]), CONVW_ROWS)
    gin, cwg = _gather_weights([gin0], convw)
    n_sc = SSD_CONV * (XBC_WIDTH // N_CHIPS)
    n_cf = CONF_KERNEL * (CONF_WIDTH // N_CHIPS)
    S = {n: w[n][0] for n in ("mix_norm_g", "ssd_conv_b", "dt_bias", "A_log", "D_skip", "ssd_norm_g", "conf_dw_b",
                              "conf_ln_g", "conf_ln_b", "mlp_norm_g", "ple_gate_norm_g", "b_ple_gate", "ple_norm_g")}
    S = {n: a.reshape(1, -1) for n, a in S.items()}
    S["final_norm_g"] = final_norm_g.reshape(1, -1)
    S["ssd_conv_w"] = jnp.concatenate(
        [cwg[b].reshape(-1)[:n_sc].reshape(SSD_CONV, XBC_WIDTH // N_CHIPS) for b in range(N_CHIPS)], axis=1)
    S["conf_dw_w"] = jnp.concatenate(
        [cwg[b].reshape(-1)[n_sc:n_sc + n_cf].reshape(CONF_KERNEL, CONF_WIDTH // N_CHIPS) for b in range(N_CHIPS)], axis=1)

    cidx = jnp.stack([ci, chip]).astype(jnp.int32)
    grad_x, ga, recv_a, ici_a, gw_in, gsmall = _local_step(x[0], p[0, 0], loss_target[0], gath0, cidx, gin, S)

    gb = _shard_rows(gw_in)
    small = _pack_small(gsmall)
    recv_b = _swap_halves(gb)
    hb = _chip_sum(cidx, gb, recv_b, "chip_sum_b")
    ici_b, all_small = _exchange(hb, small)
    ra = _final_sum(cidx, ga, recv_a, ici_a, "final_sum_a")
    rb = _final_sum(cidx, gb, recv_b, ici_b, "final_sum_b")
    ra, rb = _join_halves(ra, rb)
    tot_small = _sum_small(all_small)

    loss = tot_small[ROW_HEADS, HEAD_LANES["loss"] * LANES]

    two_d = lambda a: a.reshape(a.shape[-2:]) if a.ndim > 1 else a.reshape(1, -1)
    small_w, small_m, small_v = ({n: two_d(d[n]) for n in SMALL_ORDER} for d in (w, m, v))
    grads, delta, new_m, new_v = _adamw_small(cidx, tot_small, small_w, small_m, small_v)
    g_in_t = rb[:W_IN_ROWS]
    grads["w_ple"] = _ple_of_slab(ra)
    grads["w_in"] = jnp.swapaxes(g_in_t, 0, 1)
    for n, off in BIG_A:
        grads[n] = ra[off:off + w[n].shape[1]]
        delta[n], new_m[n], new_v[n] = _adamw(w[n][0], ra, m[n][0], v[n][0], "adamw_" + n, g_off=off)
    delta["w_ple"], new_m["w_ple"], new_v["w_ple"] = _adamw(w_ple[0], grads["w_ple"], m_w_ple[0], v_w_ple[0], "adamw_w_ple")
    tr_ = lambda a: jnp.swapaxes(a[0], 0, 1)
    d_, m_, v_ = _adamw(tr_(w_in), g_in_t, tr_(m_w_in), tr_(v_w_in), "adamw_w_in", by_columns=True)
    delta["w_in"], new_m["w_in"], new_v["w_in"] = (jnp.swapaxes(a, 0, 1) for a in (d_, m_, v_))

    shaped = lambda d: [d[n].reshape(w[n].shape) for n in WEIGHTS]
    return (loss, grad_x[None], *shaped(grads), *shaped(delta), *shaped(new_m), *shaped(new_v))
```

```python
import jax
import jax.numpy as jnp
from jax import lax
from jax.experimental import pallas as pl
from jax.experimental.pallas import tpu as pltpu

f32 = jnp.float32
bf16 = jnp.bfloat16

D_MODEL = 1024
SSD_WIDTH = 1024
SSD_HEADS = 16
HEAD_DIM = 64
SSD_STATE = 128
XBC_WIDTH = 1536
SSD_CONV = 4
CHUNK = 128
CONF_WIDTH = 1024
CONF_KERNEL = 31
D_FF = 4096
PLE_DIM = 256
IN_WIDTH = 4624
EPS = 1e-6
N_CHIPS = 4
N_DEV = 8

ADAM_LR = 0.001
ADAM_B1 = 0.9
ADAM_B2 = 0.999
ADAM_EPS = 1e-08
ADAM_WD = 0.01
ADAM_STEP = 10

LANES = 128
VMEM_BIG = 56 * 1024 * 1024
VMEM_MID = 40 * 1024 * 1024

UP_OFF, DOWN_OFF, OUT_OFF, PG_OFF, PLE_OFF = 0, 1024, 2048, 2560, 2816
PLE_ROWS = 64
SLAB_A = PLE_OFF + PLE_ROWS
GATHER_EARLY_ROWS = 480
W_IN_ROWS = 1156
W_IN_ROWS_PAD = 1184
CONVW_ROWS = 16
SMALL_ROWS = 56

MESH = pl.DeviceIdType.MESH
ANY = pl.BlockSpec(memory_space=pl.ANY)


PIN_SMALL = 256 * 1024


def _pallas(body, pin_bytes=None, **kw):
    call = pl.pallas_call(body, **kw)

    def pin(a):
        wanted = pin_bytes is None or a.size * a.dtype.itemsize <= pin_bytes
        return pltpu.with_memory_space_constraint(a, pltpu.HBM) if wanted and a.dtype != jnp.int32 else a

    def run(*args):
        return call(*[pin(a) for a in args])

    return run


def _cparams(sem=None, vmem=None):
    return pltpu.CompilerParams(dimension_semantics=sem, vmem_limit_bytes=vmem)


def _full(shape):
    n = len(shape)
    return pl.BlockSpec(shape, lambda *_: (0,) * n)


class _Rider:
    def __init__(self, inputs, out_shapes, aliases, n_sems, start, finish):
        self.inputs, self.out_shapes, self.aliases = list(inputs), list(out_shapes), dict(aliases)
        self.n_sems, self.start, self.finish = n_sems, start, finish


def _call(body, args, *, name, grid, in_specs, out_specs, out_shape, scratch_shapes=(), params=None, rider=None):
    if rider is None:
        return _pallas(body, name=name, grid=grid, in_specs=in_specs, out_specs=out_specs, out_shape=out_shape,
                              scratch_shapes=list(scratch_shapes), compiler_params=params)(*args)
    ni, no, ns = len(in_specs), len(out_specs), len(scratch_shapes)
    ri, ro = len(rider.inputs), len(rider.out_shapes)
    (steps,) = grid

    def with_rider(*refs):
        ins, refs = refs[:ni], refs[ni:]
        rins, refs = refs[:ri], refs[ri:]
        outs, refs = refs[:no], refs[no:]
        routs, refs = refs[:ro], refs[ro:]
        scratch, (ssem, rsem) = refs[:ns], refs[ns:]
        step = pl.program_id(0)

        @pl.when(step == 0)
        def _():
            rider.start(rins, routs, ssem, rsem)

        body(*ins, *outs, *scratch)

        @pl.when(step == steps - 1)
        def _():
            rider.finish(rins, routs, ssem, rsem)

    sems = [pltpu.SemaphoreType.DMA((rider.n_sems,)), pltpu.SemaphoreType.DMA((rider.n_sems,))]
    return _pallas(
        with_rider, name=name, grid=grid, in_specs=list(in_specs) + [ANY] * ri, out_specs=list(out_specs) + [ANY] * ro,
        out_shape=list(out_shape) + rider.out_shapes, scratch_shapes=list(scratch_shapes) + sems,
        input_output_aliases={ni + a: no + b for a, b in rider.aliases.items()}, compiler_params=params,
    )(*args, *rider.inputs)


def _dot(a, b):
    return jnp.dot(a, b, preferred_element_type=f32)


def _dot_nt(a, b):
    return lax.dot_general(a, b, (((1,), (1,)), ((), ())), preferred_element_type=f32)


def _dot_tn(a, b):
    return lax.dot_general(a, b, (((0,), (0,)), ((), ())), preferred_element_type=f32)


def _sigmoid(x):
    return jax.nn.sigmoid(x)


def _rms(x, g):
    r = lax.rsqrt(jnp.mean(x * x, axis=-1, keepdims=True) + EPS)
    return x * r * g


def _rms_bwd(dy, x, g):
    r = lax.rsqrt(jnp.mean(x * x, axis=-1, keepdims=True) + EPS)
    xh = x * r
    dg = jnp.sum(dy * xh, axis=0, keepdims=True)
    dxh = dy * g
    dx = r * (dxh - xh * jnp.mean(dxh * xh, axis=-1, keepdims=True))
    return dx, dg


def _dsilu(x):
    s = _sigmoid(x)
    return s * (1.0 + x * (1.0 - s))


def _split3(x):
    hi = x.astype(bf16)
    r1 = x - hi.astype(f32)
    mid = r1.astype(bf16)
    lo = (r1 - mid.astype(f32)).astype(bf16)
    return hi, mid, lo


def _head_matrix():
    row = lax.broadcasted_iota(jnp.int32, (LANES, SSD_WIDTH), 0)
    col = lax.broadcasted_iota(jnp.int32, (LANES, SSD_WIDTH), 1)
    lo = row * HEAD_DIM
    return ((col >= lo) & (col < lo + HEAD_DIM)).astype(bf16)


def _expand(x, e):
    hi, mid, lo = _split3(x)
    return _dot(hi, e) + _dot(mid, e) + _dot(lo, e)


def _contract(x, e):
    hi = x.astype(bf16)
    mid = (x - hi.astype(f32)).astype(bf16)
    return _dot_nt(hi, e) + _dot_nt(mid, e)


O_XBC = SSD_WIDTH
O_DT = O_XBC + XBC_WIDTH
O_CV = O_DT + SSD_HEADS
O_CG = O_CV + CONF_WIDTH


def _assemble_w_in_t(gin_ref, wt_ref):
    for b in range(N_CHIPS):
        wt_ref[b * W_IN_ROWS:(b + 1) * W_IN_ROWS, :] = gin_ref[b, 0:W_IN_ROWS, :]


def _in_proj_fwd(x, g, gin, rider=None):
    T = x.shape[0]
    tm = min(256, T)

    def body(x_ref, g_ref, gin_ref, u_ref, z_ref, xbc_ref, cv_ref, cg_ref, dt_ref, v_ref, wt_ref):
        @pl.when(pl.program_id(0) == 0)
        def _():
            _assemble_w_in_t(gin_ref, wt_ref)

        ub = _rms(x_ref[...], g_ref[...]).astype(bf16)
        u_ref[...] = ub
        z_ref[...] = _dot_nt(ub, wt_ref[0:O_XBC, :])
        xbc_ref[...] = _dot_nt(ub, wt_ref[O_XBC:O_DT, :])
        cv = _dot_nt(ub, wt_ref[O_CV:O_CG, :])
        cg = _dot_nt(ub, wt_ref[O_CG:IN_WIDTH, :])
        cv_ref[...] = cv
        cg_ref[...] = cg
        v_ref[...] = cv * _sigmoid(cg)
        dt_ref[...] = _dot_nt(ub, wt_ref[O_DT:O_DT + LANES, :])

    row = lambda n: pl.BlockSpec((tm, n), lambda i: (i, 0))
    return _call(
        body, (x, g, gin), name="in_proj_fwd", grid=(T // tm,),
        in_specs=[row(D_MODEL), _full((1, D_MODEL)), _full(gin.shape)],
        out_specs=[row(D_MODEL), row(SSD_WIDTH), row(XBC_WIDTH), row(CONF_WIDTH), row(CONF_WIDTH), row(LANES),
                   row(CONF_WIDTH)],
        out_shape=[jax.ShapeDtypeStruct((T, D_MODEL), bf16), jax.ShapeDtypeStruct((T, SSD_WIDTH), f32),
                   jax.ShapeDtypeStruct((T, XBC_WIDTH), f32), jax.ShapeDtypeStruct((T, CONF_WIDTH), f32),
                   jax.ShapeDtypeStruct((T, CONF_WIDTH), f32), jax.ShapeDtypeStruct((T, LANES), f32),
                   jax.ShapeDtypeStruct((T, CONF_WIDTH), f32)],
        scratch_shapes=[pltpu.VMEM((IN_WIDTH, D_MODEL), bf16)],
        params=_cparams(("arbitrary",), VMEM_BIG), rider=rider)


SUBLANES = 8


def _phases(offsets):
    return sorted({o % SUBLANES for o in offsets} - {0})


def _phase_shape(offsets, tm, C):
    a_max = max([o // SUBLANES for o in offsets if o % SUBLANES] or [0])
    return (max(len(_phases(offsets)), 1), tm + SUBLANES * a_max, C)


def _make_phases(buf_ref, ph_ref, offsets, tm):
    for idx, b in enumerate(_phases(offsets)):
        n = tm + SUBLANES * max(o // SUBLANES for o in offsets if o % SUBLANES == b)
        ph_ref[idx, 0:n, :] = buf_ref[pl.ds(b, n), :]


def _window(buf_ref, ph_ref, offsets, o, r0, rb):
    a, b = divmod(o, SUBLANES)
    if b == 0:
        return buf_ref[pl.ds(r0 + SUBLANES * a, rb), :]
    return ph_ref[_phases(offsets).index(b), pl.ds(r0 + SUBLANES * a, rb), :]


def _conv_rows(wb_ref, buf_ref, ph_ref, offsets, r0, rb):
    nsub = rb // SUBLANES
    accs = [None] * nsub
    for k, o in enumerate(offsets):
        wk = wb_ref[pl.ds(SUBLANES * k, SUBLANES), :]
        for s in range(nsub):
            term = wk * _window(buf_ref, ph_ref, offsets, o, r0 + SUBLANES * s, SUBLANES)
            accs[s] = term if accs[s] is None else accs[s] + term
    return accs[0] if nsub == 1 else jnp.concatenate(accs, axis=0)


def _sublane_rows(w):
    return jnp.repeat(w, SUBLANES, axis=0)


def _fwd_offsets(K, hb):
    return [hb - (K - 1) + k for k in range(K)]


def _prev_halo_spec(hb, tm, C):
    return pl.BlockSpec((hb, C), lambda i: (jnp.maximum(i * (tm // hb) - 1, 0), 0))


CONV_RB = 16


def _ssd_conv_fwd(xbc, w, b):
    T, C = xbc.shape
    K, hb = SSD_CONV, 8
    tm = min(256, T)
    offs = _fwd_offsets(K, hb)

    def body(cur_ref, halo_ref, w_ref, b_ref, pre_ref, buf_ref, ph_ref):
        keep = jnp.where(pl.program_id(0) > 0, 1.0, 0.0)
        buf_ref[0:hb, :] = halo_ref[...] * keep
        buf_ref[hb:hb + tm, :] = cur_ref[...]
        _make_phases(buf_ref, ph_ref, offs, tm)

        def chunk(i, carry):
            r0 = pl.multiple_of(i * CONV_RB, CONV_RB)
            pre_ref[pl.ds(r0, CONV_RB), :] = _conv_rows(w_ref, buf_ref, ph_ref, offs, r0, CONV_RB) + b_ref[...]
            return carry

        lax.fori_loop(0, tm // CONV_RB, chunk, 0)

    return _pallas(
        body, name="ssd_conv_fwd", grid=(T // tm,),
        in_specs=[pl.BlockSpec((tm, C), lambda i: (i, 0)), _prev_halo_spec(hb, tm, C), _full((SUBLANES * K, C)),
                  _full((1, C))],
        out_specs=pl.BlockSpec((tm, C), lambda i: (i, 0)),
        out_shape=jax.ShapeDtypeStruct((T, C), f32),
        scratch_shapes=[pltpu.VMEM((hb + tm, C), f32), pltpu.VMEM(_phase_shape(offs, tm, C), f32)],
        compiler_params=_cparams(("parallel",), VMEM_MID),
    )(xbc, xbc, _sublane_rows(w), b)


def _conf_fwd(v, w, b, ln_g, ln_b, rider=None):
    T, C = v.shape
    K, hb = CONF_KERNEL, 32
    tm = min(256, T)
    offs = _fwd_offsets(K, hb)
    rb = 2 * CONV_RB

    def body(cur_ref, halo_ref, w_ref, b_ref, g_ref, bb_ref, co_ref, y_ref, buf_ref, ph_ref):
        keep = jnp.where(pl.program_id(0) > 0, 1.0, 0.0)
        buf_ref[0:hb, :] = halo_ref[...] * keep
        buf_ref[hb:hb + tm, :] = cur_ref[...]
        _make_phases(buf_ref, ph_ref, offs, tm)

        def chunk(i, carry):
            r0 = pl.multiple_of(i * rb, rb)
            co = _conv_rows(w_ref, buf_ref, ph_ref, offs, r0, rb) + b_ref[...]
            co_ref[pl.ds(r0, rb), :] = co
            mu = jnp.mean(co, axis=-1, keepdims=True)
            xc = co - mu
            yn = xc * lax.rsqrt(jnp.mean(xc * xc, axis=-1, keepdims=True) + EPS) * g_ref[...] + bb_ref[...]
            y_ref[pl.ds(r0, rb), :] = (yn * _sigmoid(yn)).astype(bf16)
            return carry

        lax.fori_loop(0, tm // rb, chunk, 0)

    return _call(
        body, (v, v, _sublane_rows(w), b, ln_g, ln_b), name="conf_fwd", grid=(T // tm,),
        in_specs=[pl.BlockSpec((tm, C), lambda i: (i, 0)), _prev_halo_spec(hb, tm, C), _full((SUBLANES * K, C)),
                  _full((1, C)), _full((1, C)), _full((1, C))],
        out_specs=[pl.BlockSpec((tm, C), lambda i: (i, 0)), pl.BlockSpec((tm, C), lambda i: (i, 0))],
        out_shape=[jax.ShapeDtypeStruct((T, C), f32), jax.ShapeDtypeStruct((T, C), bf16)],
        scratch_shapes=[pltpu.VMEM((hb + tm, C), f32), pltpu.VMEM(_phase_shape(offs, tm, C), f32)],
        params=_cparams(("arbitrary",), VMEM_MID), rider=rider)


def _ssd_chunk_common(pre, dtr, dtb, alog, e):
    act = pre * _sigmoid(pre)
    xs = act[:, :SSD_WIDTH]
    bm = act[:, SSD_WIDTH:SSD_WIDTH + 2 * SSD_STATE]
    cm = act[:, SSD_WIDTH + 2 * SSD_STATE:]
    row = lax.broadcasted_iota(jnp.int32, (CHUNK, CHUNK), 0)
    col = lax.broadcasted_iota(jnp.int32, (CHUNK, CHUNK), 1)
    tri = row >= col
    dt = jax.nn.softplus(dtr + dtb)
    a_neg = -jnp.exp(alog)
    a = dt * a_neg
    cs = jnp.dot(tri.astype(f32), a, precision=lax.Precision.HIGHEST, preferred_element_type=f32)
    cs_e = _expand(cs, e)
    dt_e = _expand(dt, e)
    csl_e = cs_e[CHUNK - 1:CHUNK, :]
    ecs_e = jnp.exp(cs_e)
    dte_e = jnp.exp(csl_e - cs_e)
    cd_e = jnp.exp(csl_e)
    xc = xs * dt_e
    xd = xc * dte_e
    return dict(xs=xs, bm=bm, cm=cm, tri=tri, dt=dt, a_neg=a_neg, cs=cs, ecs_e=ecs_e, dte_e=dte_e, cd_e=cd_e,
                dt_e=dt_e, xc=xc, xd=xd)


def _group(v, g, width):
    return v[:, g * width:(g + 1) * width]


def _ssd_fwd(pre, dtr, z, dtb, alog, dskip_e, gn):
    T = pre.shape[0]
    nc = T // CHUNK
    GW = SSD_WIDTH // 2

    def body(pre_ref, dtr_ref, z_ref, dtb_ref, alog_ref, de_ref, gn_ref, y_ref, ys_ref, sp_ref, st_ref):
        @pl.when(pl.program_id(0) == 0)
        def _():
            st_ref[...] = jnp.zeros_like(st_ref)

        e = _head_matrix()
        q = _ssd_chunk_common(pre_ref[...], dtr_ref[...], dtb_ref[...], alog_ref[...], e)
        cs, tri, xc, xd = q["cs"], q["tri"], q["xc"], q["xd"]
        cs_t = cs.T
        st = st_ref[...]
        sp_ref[0] = st
        lane = lax.broadcasted_iota(jnp.int32, (1, LANES), 1)
        halves = (lane < HEAD_DIM, lane >= HEAD_DIM)

        g_mat, y_off, s_new = [], [], []
        for g in range(2):
            bg = _group(q["bm"], g, SSD_STATE)
            cg = _group(q["cm"], g, SSD_STATE)
            bgb, cgb = bg.astype(bf16), cg.astype(bf16)
            g_mat.append(_dot_nt(cgb, bgb))
            y_off.append(_dot(cgb, _group(st, g, GW).astype(bf16)))
            s_new.append(_dot(bg.T.astype(bf16), _group(xd, g, GW).astype(bf16)))
        y_off = jnp.concatenate(y_off, axis=1) * q["ecs_e"]
        st_ref[...] = st * q["cd_e"] + jnp.concatenate(s_new, axis=1)

        pairs = []
        for j in range(SSD_HEADS // 2):
            xp = xc[:, j * LANES:(j + 1) * LANES]
            acc = jnp.zeros((CHUNK, LANES), f32)
            for hh in range(2):
                h = 2 * j + hh
                seg = cs[:, h:h + 1] - cs_t[h:h + 1, :]
                lm = jnp.exp(jnp.where(tri, seg, -1e30))
                m = (g_mat[h // 8] * lm).astype(bf16)
                acc = acc + _dot(m, jnp.where(halves[hh], xp, 0.0).astype(bf16))
            pairs.append(acc)
        y = jnp.concatenate(pairs, axis=1) + y_off + q["xs"] * de_ref[...]
        y_ref[...] = y

        zz = z_ref[...]
        v = y * (zz * _sigmoid(zz))
        outs = []
        for g in range(2):
            vg = _group(v, g, GW)
            outs.append(vg * lax.rsqrt(jnp.mean(vg * vg, axis=-1, keepdims=True) + EPS))
        ys_ref[...] = (jnp.concatenate(outs, axis=1) * gn_ref[...]).astype(bf16)

    ch = lambda n: pl.BlockSpec((CHUNK, n), lambda c: (c, 0))
    return _pallas(
        body, name="ssd_fwd", grid=(nc,),
        in_specs=[ch(XBC_WIDTH), ch(LANES), ch(SSD_WIDTH), _full((1, LANES)), _full((1, LANES)), _full((1, SSD_WIDTH)),
                  _full((1, SSD_WIDTH))],
        out_specs=[ch(SSD_WIDTH), ch(SSD_WIDTH), pl.BlockSpec((1, SSD_STATE, SSD_WIDTH), lambda c: (c, 0, 0))],
        out_shape=[jax.ShapeDtypeStruct((T, SSD_WIDTH), f32), jax.ShapeDtypeStruct((T, SSD_WIDTH), bf16),
                   jax.ShapeDtypeStruct((nc, SSD_STATE, SSD_WIDTH), f32)],
        scratch_shapes=[pltpu.VMEM((SSD_STATE, SSD_WIDTH), f32)],
        compiler_params=_cparams(("arbitrary",), VMEM_MID),
    )(pre, dtr, z, dtb, alog, dskip_e, gn)


def _w_out_spec():
    n = 2 * SSD_WIDTH // N_CHIPS
    return pl.BlockSpec((N_CHIPS, n, D_MODEL), lambda *_: (0, OUT_OFF // n, 0))


def _out_proj_fwd(x, ys, yc, gath, g):
    T = x.shape[0]
    tm = min(512, T)
    n = 2 * SSD_WIDTH // N_CHIPS

    def body(x_ref, ys_ref, yc_ref, w_ref, g_ref, h_ref, u_ref):
        h = (x_ref[...] + _dot(ys_ref[:, 0:n], w_ref[0]) + _dot(ys_ref[:, n:], w_ref[1])
             + _dot(yc_ref[:, 0:n], w_ref[2]) + _dot(yc_ref[:, n:], w_ref[3]))
        h_ref[...] = h
        u_ref[...] = _rms(h, g_ref[...]).astype(bf16)

    row = pl.BlockSpec((tm, D_MODEL), lambda i: (i, 0))
    return _pallas(
        body, name="out_proj_fwd", grid=(T // tm,),
        in_specs=[row, row, row, _w_out_spec(), _full((1, D_MODEL))],
        out_specs=[row, row],
        out_shape=[jax.ShapeDtypeStruct((T, D_MODEL), f32), jax.ShapeDtypeStruct((T, D_MODEL), bf16)],
        compiler_params=_cparams(("parallel",), VMEM_MID),
    )(x, ys, yc, gath, g)


def _w_up_spec():
    return pl.BlockSpec((1, D_MODEL, D_MODEL), lambda i, b: (b, UP_OFF // D_MODEL, 0))


def _w_down_spec():
    return pl.BlockSpec((1, D_MODEL, D_MODEL), lambda i, b: (b, DOWN_OFF // D_MODEL, 0))


def _mlp_fwd(h1, u1, gath, g_next):
    T = h1.shape[0]
    tm = min(512, T)
    nb = D_FF // D_MODEL

    def body(h_ref, u_ref, wu_ref, wd_ref, g_ref, r_ref, h2_ref, u2_ref, acc_ref):
        b = pl.program_id(1)

        @pl.when(b == 0)
        def _():
            acc_ref[...] = jnp.zeros_like(acc_ref)

        r = jnp.maximum(_dot(u_ref[...], wu_ref[0]), 0.0)
        r_ref[...] = r.astype(bf16)
        acc_ref[...] += _dot((r * r).astype(bf16), wd_ref[0])

        @pl.when(b == nb - 1)
        def _():
            h2 = h_ref[...] + acc_ref[...]
            h2_ref[...] = h2
            u2_ref[...] = _rms(h2, g_ref[...]).astype(bf16)

    row = pl.BlockSpec((tm, D_MODEL), lambda i, b: (i, 0))
    return _pallas(
        body, name="mlp_fwd", grid=(T // tm, nb),
        in_specs=[row, row, _w_up_spec(), _w_down_spec(), _full((1, D_MODEL))],
        out_specs=[pl.BlockSpec((tm, D_MODEL), lambda i, b: (i, b)), row, row],
        out_shape=[jax.ShapeDtypeStruct((T, D_FF), bf16), jax.ShapeDtypeStruct((T, D_MODEL), f32),
                   jax.ShapeDtypeStruct((T, D_MODEL), bf16)],
        scratch_shapes=[pltpu.VMEM((tm, D_MODEL), f32)],
        compiler_params=_cparams(("parallel", "arbitrary"), VMEM_MID),
    )(h1, u1, gath, gath, g_next)


def _ple_loss(h2, u2, p, tgt, gath, b_pg, w_ple, g_ple, g_fin, g_pg):
    T = h2.shape[0]
    tm = min(256, T)
    npg = D_MODEL // N_CHIPS

    def body(h2_ref, u2_ref, p_ref, t_ref, wpg_ref, bpg_ref, wple_ref, gple_ref, gfin_ref, gpg_ref,
             loss_ref, dh2_ref, dh2b_ref, dgp_ref, dep_ref, dgfin_ref, dgple_ref, dbpg_ref, dgpg_ref):
        @pl.when(pl.program_id(0) == 0)
        def _():
            loss_ref[...] = jnp.zeros_like(loss_ref)
            dgfin_ref[...] = jnp.zeros_like(dgfin_ref)
            dgple_ref[...] = jnp.zeros_like(dgple_ref)
            dbpg_ref[...] = jnp.zeros_like(dbpg_ref)
            dgpg_ref[...] = jnp.zeros_like(dgpg_ref)

        h2 = h2_ref[...]
        gate_pre = bpg_ref[...]
        for b in range(N_CHIPS):
            gate_pre = gate_pre + _dot(u2_ref[:, b * npg:(b + 1) * npg], wpg_ref[b])
        gate = _sigmoid(gate_pre)
        e_pre = _dot(p_ref[...].astype(bf16), wple_ref[...])
        emb = _rms(e_pre, gple_ref[...])
        h3 = h2 + gate * emb
        diff = _rms(h3, gfin_ref[...]) - t_ref[...]
        sq = jnp.sum(jnp.sum(diff * diff, axis=1, keepdims=True), axis=0, keepdims=True)
        loss_ref[...] += (0.5 / D_MODEL) * sq
        dh3, dgfin = _rms_bwd(diff * (1.0 / D_MODEL), h3, gfin_ref[...])
        dgfin_ref[...] += dgfin
        dgp = dh3 * emb * gate * (1.0 - gate)
        dbpg_ref[...] += jnp.sum(dgp, axis=0, keepdims=True)
        dep, dgple = _rms_bwd(dh3 * gate, e_pre, gple_ref[...])
        dgple_ref[...] += dgple
        dgpb = dgp.astype(bf16)
        dgp_ref[...] = dgpb
        dep_ref[...] = dep.astype(bf16)
        du2 = jnp.concatenate([_dot_nt(dgpb, wpg_ref[b]) for b in range(N_CHIPS)], axis=1)
        dx, dgpg = _rms_bwd(du2, h2, gpg_ref[...])
        dgpg_ref[...] += dgpg
        dh2 = dh3 + dx
        dh2_ref[...] = dh2
        dh2b_ref[...] = dh2.astype(bf16)

    row = pl.BlockSpec((tm, D_MODEL), lambda i: (i, 0))
    vec = _full((1, D_MODEL))
    vshape = jax.ShapeDtypeStruct((1, D_MODEL), f32)
    return _pallas(
        body, name="ple_loss", grid=(T // tm,),
        in_specs=[row, row, pl.BlockSpec((tm, PLE_DIM), lambda i: (i, 0)), row,
                  pl.BlockSpec((N_CHIPS, npg, D_MODEL), lambda i: (0, PG_OFF // npg, 0)), vec, _full(w_ple.shape),
                  vec, vec, vec],
        out_specs=[_full((8, LANES)), row, row, row, row, vec, vec, vec, vec],
        out_shape=[jax.ShapeDtypeStruct((8, LANES), f32), jax.ShapeDtypeStruct((T, D_MODEL), f32),
                   jax.ShapeDtypeStruct((T, D_MODEL), bf16), jax.ShapeDtypeStruct((T, D_MODEL), bf16),
                   jax.ShapeDtypeStruct((T, D_MODEL), bf16), vshape, vshape, vshape, vshape],
        compiler_params=_cparams(("arbitrary",), VMEM_MID),
    )(h2, u2, p, tgt, gath, b_pg, w_ple, g_ple, g_fin, g_pg)


def _mlp_bwd(dh2, r, gath, h1, g):
    T = dh2.shape[0]
    tm = min(512, T)
    nb = D_FF // D_MODEL

    def body(dh2_ref, r_ref, wd_ref, wu_ref, h1_ref, g_ref, dhp_ref, dh1_ref, dh1b_ref, dg_ref, acc_ref):
        i, b = pl.program_id(0), pl.program_id(1)

        @pl.when(b == 0)
        def _():
            acc_ref[...] = jnp.zeros_like(acc_ref)

        @pl.when((b == 0) & (i == 0))
        def _():
            dg_ref[...] = jnp.zeros_like(dg_ref)

        dact = _dot_nt(dh2_ref[...].astype(bf16), wd_ref[0])
        dhp = (dact * 2.0 * r_ref[...].astype(f32)).astype(bf16)
        dhp_ref[...] = dhp
        acc_ref[...] += _dot_nt(dhp, wu_ref[0])

        @pl.when(b == nb - 1)
        def _():
            dx, dg = _rms_bwd(acc_ref[...], h1_ref[...], g_ref[...])
            dg_ref[...] += dg
            dh1 = dh2_ref[...] + dx
            dh1_ref[...] = dh1
            dh1b_ref[...] = dh1.astype(bf16)

    row = pl.BlockSpec((tm, D_MODEL), lambda i, b: (i, 0))
    return _pallas(
        body, name="mlp_bwd", grid=(T // tm, nb),
        in_specs=[row, pl.BlockSpec((tm, D_MODEL), lambda i, b: (i, b)), _w_down_spec(), _w_up_spec(), row,
                  _full((1, D_MODEL))],
        out_specs=[pl.BlockSpec((tm, D_MODEL), lambda i, b: (i, b)), row, row, _full((1, D_MODEL))],
        out_shape=[jax.ShapeDtypeStruct((T, D_FF), bf16), jax.ShapeDtypeStruct((T, D_MODEL), f32),
                   jax.ShapeDtypeStruct((T, D_MODEL), bf16), jax.ShapeDtypeStruct((1, D_MODEL), f32)],
        scratch_shapes=[pltpu.VMEM((tm, D_MODEL), f32)],
        compiler_params=_cparams(("arbitrary", "arbitrary"), VMEM_MID),
    )(dh2, r, gath, gath, h1, g)


def _out_proj_bwd(dh1, gath, co, ln_g, ln_b, rider=None):
    T = dh1.shape[0]
    tm = min(512, T)

    def body(dh_ref, w_ref, co_ref, g_ref, b_ref, dys_ref, dco_ref, dg_ref, db_ref):
        @pl.when(pl.program_id(0) == 0)
        def _():
            dg_ref[...] = jnp.zeros_like(dg_ref)
            db_ref[...] = jnp.zeros_like(db_ref)

        dhb = dh_ref[...].astype(bf16)
        dys_ref[...] = jnp.concatenate([_dot_nt(dhb, w_ref[0]), _dot_nt(dhb, w_ref[1])], axis=1)
        dyc = jnp.concatenate([_dot_nt(dhb, w_ref[2]), _dot_nt(dhb, w_ref[3])], axis=1)
        co = co_ref[...]
        mu = jnp.mean(co, axis=-1, keepdims=True)
        xc = co - mu
        rstd = lax.rsqrt(jnp.mean(xc * xc, axis=-1, keepdims=True) + EPS)
        xh = xc * rstd
        yn = xh * g_ref[...] + b_ref[...]
        dyn = dyc * _dsilu(yn)
        dg_ref[...] += jnp.sum(dyn * xh, axis=0, keepdims=True)
        db_ref[...] += jnp.sum(dyn, axis=0, keepdims=True)
        dxh = dyn * g_ref[...]
        dco_ref[...] = rstd * (dxh - jnp.mean(dxh, axis=-1, keepdims=True)
                               - xh * jnp.mean(dxh * xh, axis=-1, keepdims=True))

    row = pl.BlockSpec((tm, D_MODEL), lambda i: (i, 0))
    vec = _full((1, CONF_WIDTH))
    vshape = jax.ShapeDtypeStruct((1, CONF_WIDTH), f32)
    return _call(
        body, (dh1, gath, co, ln_g, ln_b), name="out_proj_bwd", grid=(T // tm,),
        in_specs=[row, _w_out_spec(), row, vec, vec],
        out_specs=[row, row, vec, vec],
        out_shape=[jax.ShapeDtypeStruct((T, SSD_WIDTH), f32), jax.ShapeDtypeStruct((T, CONF_WIDTH), f32), vshape, vshape],
        params=_cparams(("arbitrary",), VMEM_MID), rider=rider)


def _bwd_offsets(K):
    return [K - 1 - k for k in range(K)]


def _next_halo_spec(hb, tm, C, T):
    return pl.BlockSpec((hb, C), lambda i: (jnp.minimum((i + 1) * (tm // hb), T // hb - 1), 0))


DW_RB = 8
DW_UNROLL = 4
DW_ACC_VREGS = 32


def _conv_dw(dw_ref, bufd_ref, bufx_ref, phx_ref, offs_x, tm, C):
    K = len(offs_x)
    group = max(1, DW_ACC_VREGS // (C // LANES))
    for k0 in range(0, K, group):
        ks = list(range(k0, min(k0 + group, K)))

        def step(i, accs, ks=ks):
            for u in range(DW_UNROLL):
                r0 = pl.multiple_of((i * DW_UNROLL + u) * DW_RB, DW_RB)
                d = bufd_ref[pl.ds(r0, DW_RB), :]
                accs = tuple(acc + _window(bufx_ref, phx_ref, offs_x, offs_x[k], r0, DW_RB) * d
                             for k, acc in zip(ks, accs))
            return accs

        accs = lax.fori_loop(0, tm // (DW_RB * DW_UNROLL), step, tuple(jnp.zeros((DW_RB, C), f32) for _ in ks))
        for k, acc in zip(ks, accs):
            dw_ref[k:k + 1, :] += jnp.sum(acc, axis=0, keepdims=True)


def _fill_bwd_buffers(dcur_ref, dnext_ref, xcur_ref, xprev_ref, bufd_ref, bufx_ref, phd_ref, phx_ref, offs_d, offs_x,
                      hb, tm, first, last):
    bufd_ref[0:tm, :] = dcur_ref[...]
    bufd_ref[tm:tm + hb, :] = dnext_ref[...] * jnp.where(last, 0.0, 1.0)
    bufx_ref[0:hb, :] = xprev_ref[...] * jnp.where(first, 0.0, 1.0)
    bufx_ref[hb:hb + tm, :] = xcur_ref[...]
    _make_phases(bufd_ref, phd_ref, offs_d, tm)
    _make_phases(bufx_ref, phx_ref, offs_x, tm)


def _ssd_conv_bwd(dpre, xbc, w):
    T, C = xbc.shape
    K, hb = SSD_CONV, 8
    tm = min(256, T)
    nt = T // tm
    offs_d, offs_x = _bwd_offsets(K), _fwd_offsets(K, hb)

    def body(dcur_ref, dnext_ref, xcur_ref, xprev_ref, w_ref, dx_ref, dw_ref, db_ref, bufd_ref, bufx_ref, phd_ref, phx_ref):
        i = pl.program_id(0)

        @pl.when(i == 0)
        def _():
            dw_ref[...] = jnp.zeros_like(dw_ref)
            db_ref[...] = jnp.zeros_like(db_ref)

        _fill_bwd_buffers(dcur_ref, dnext_ref, xcur_ref, xprev_ref, bufd_ref, bufx_ref, phd_ref, phx_ref, offs_d, offs_x,
                          hb, tm, i == 0, i == nt - 1)

        def chunk(j, carry):
            r0 = pl.multiple_of(j * CONV_RB, CONV_RB)
            dx_ref[pl.ds(r0, CONV_RB), :] = _conv_rows(w_ref, bufd_ref, phd_ref, offs_d, r0, CONV_RB).astype(bf16)
            return carry

        lax.fori_loop(0, tm // CONV_RB, chunk, 0)
        _conv_dw(dw_ref, bufd_ref, bufx_ref, phx_ref, offs_x, tm, C)
        db_ref[...] += jnp.sum(dcur_ref[...], axis=0, keepdims=True)

    row = pl.BlockSpec((tm, C), lambda i: (i, 0))
    return _pallas(
        body, name="ssd_conv_bwd", grid=(nt,),
        in_specs=[row, _next_halo_spec(hb, tm, C, T), row, _prev_halo_spec(hb, tm, C), _full((SUBLANES * K, C))],
        out_specs=[row, _full((8, C)), _full((1, C))],
        out_shape=[jax.ShapeDtypeStruct((T, C), bf16), jax.ShapeDtypeStruct((8, C), f32), jax.ShapeDtypeStruct((1, C), f32)],
        scratch_shapes=[pltpu.VMEM((tm + hb, C), f32), pltpu.VMEM((hb + tm, C), f32),
                        pltpu.VMEM(_phase_shape(offs_d, tm, C), f32),
                        pltpu.VMEM(_phase_shape(offs_x, tm, C), f32)],
        compiler_params=_cparams(("arbitrary",), VMEM_BIG),
    )(dpre, dpre, xbc, xbc, _sublane_rows(w))


def _conf_conv_bwd(dco, v, w, cv, cg, rider=None):
    T, C = v.shape
    K, hb = CONF_KERNEL, 32
    tm = min(256, T)
    nt = T // tm
    offs_d, offs_x = _bwd_offsets(K), _fwd_offsets(K, hb)

    def body(dcur_ref, dnext_ref, vcur_ref, vprev_ref, w_ref, cv_ref, cg_ref, dcv_ref, dcg_ref, dw_ref, db_ref,
             bufd_ref, bufx_ref, phd_ref, phx_ref):
        i = pl.program_id(0)

        @pl.when(i == 0)
        def _():
            dw_ref[...] = jnp.zeros_like(dw_ref)
            db_ref[...] = jnp.zeros_like(db_ref)

        _fill_bwd_buffers(dcur_ref, dnext_ref, vcur_ref, vprev_ref, bufd_ref, bufx_ref, phd_ref, phx_ref, offs_d, offs_x,
                          hb, tm, i == 0, i == nt - 1)

        def chunk(j, carry):
            r0 = pl.multiple_of(j * CONV_RB, CONV_RB)
            rows = pl.ds(r0, CONV_RB)
            dv = _conv_rows(w_ref, bufd_ref, phd_ref, offs_d, r0, CONV_RB)
            s = _sigmoid(cg_ref[rows, :])
            dcv_ref[rows, :] = (dv * s).astype(bf16)
            dcg_ref[rows, :] = (dv * cv_ref[rows, :] * s * (1.0 - s)).astype(bf16)
            return carry

        lax.fori_loop(0, tm // CONV_RB, chunk, 0)
        _conv_dw(dw_ref, bufd_ref, bufx_ref, phx_ref, offs_x, tm, C)
        db_ref[...] += jnp.sum(dcur_ref[...], axis=0, keepdims=True)

    row = pl.BlockSpec((tm, C), lambda i: (i, 0))
    return _call(
        body, (dco, dco, v, v, _sublane_rows(w), cv, cg), name="conf_conv_bwd", grid=(nt,),
        in_specs=[row, _next_halo_spec(hb, tm, C, T), row, _prev_halo_spec(hb, tm, C), _full((SUBLANES * K, C)), row, row],
        out_specs=[row, row, _full((32, C)), _full((1, C))],
        out_shape=[jax.ShapeDtypeStruct((T, C), bf16), jax.ShapeDtypeStruct((T, C), bf16),
                   jax.ShapeDtypeStruct((32, C), f32), jax.ShapeDtypeStruct((1, C), f32)],
        scratch_shapes=[pltpu.VMEM((tm + hb, C), f32), pltpu.VMEM((hb + tm, C), f32),
                        pltpu.VMEM(_phase_shape(offs_d, tm, C), f32),
                        pltpu.VMEM(_phase_shape(offs_x, tm, C), f32)],
        params=_cparams(("arbitrary",), VMEM_BIG), rider=rider)


def _ssd_bwd(dys, y, z, pre, dtr, sprev, dtb, alog, dskip_e, gn):
    T = pre.shape[0]
    nc = T // CHUNK
    GW = SSD_WIDTH // 2

    def body(dys_ref, y_ref, z_ref, pre_ref, dtr_ref, sp_ref, dtb_ref, alog_ref, de_ref, gn_ref,
             dz_ref, dpre_ref, ddtr_ref, dgn_ref, dd_ref, dal_ref, ddtb_ref, ds_ref):
        @pl.when(pl.program_id(0) == 0)
        def _():
            ds_ref[...] = jnp.zeros_like(ds_ref)
            dgn_ref[...] = jnp.zeros_like(dgn_ref)
            dd_ref[...] = jnp.zeros_like(dd_ref)
            dal_ref[...] = jnp.zeros_like(dal_ref)
            ddtb_ref[...] = jnp.zeros_like(ddtb_ref)

        e = _head_matrix()
        pre = pre_ref[...]
        dtr_b = dtr_ref[...] + dtb_ref[...]
        q = _ssd_chunk_common(pre, dtr_ref[...], dtb_ref[...], alog_ref[...], e)
        cs, tri, xc, xd, xs, dt = q["cs"], q["tri"], q["xc"], q["xd"], q["xs"], q["dt"]
        cs_t = cs.T
        st = sp_ref[0]
        dsn = ds_ref[...]
        lane = lax.broadcasted_iota(jnp.int32, (1, LANES), 1)
        halves = (lane < HEAD_DIM, lane >= HEAD_DIM)
        row_i = lax.broadcasted_iota(jnp.int32, (CHUNK, CHUNK), 0)
        col_i = lax.broadcasted_iota(jnp.int32, (CHUNK, CHUNK), 1)
        tri_t = col_i >= row_i

        y = y_ref[...]
        zz = z_ref[...]
        sz = _sigmoid(zz)
        silu_z = zz * sz
        v = y * silu_z
        dout = dys_ref[...]
        gn_v = gn_ref[...]
        dv, vh = [], []
        for g in range(2):
            vg = _group(v, g, GW)
            rstd = lax.rsqrt(jnp.mean(vg * vg, axis=-1, keepdims=True) + EPS)
            vhg = vg * rstd
            dvh = _group(dout, g, GW) * _group(gn_v, g, GW)
            dv.append(rstd * (dvh - vhg * jnp.mean(dvh * vhg, axis=-1, keepdims=True)))
            vh.append(vhg)
        dv = jnp.concatenate(dv, axis=1)
        dgn_ref[...] += jnp.sum(dout * jnp.concatenate(vh, axis=1), axis=0, keepdims=True)
        dy = dv * silu_z
        dz_ref[...] = (dv * y * (sz * (1.0 + zz * (1.0 - sz)))).astype(bf16)

        dd_row = jnp.sum(dy * xs, axis=0, keepdims=True)
        dd_ref[...] += _contract(jnp.broadcast_to(dd_row, (8, SSD_WIDTH)), e)[0:1, :]
        dxs = dy * de_ref[...]

        dz_in = dy * q["ecs_e"]
        g_mat, gt_mat, dcm, dbm, dsp, dxd, y_off = [], [], [], [], [], [], []
        bgs, cgs = [], []
        for g in range(2):
            bg = _group(q["bm"], g, SSD_STATE)
            cg = _group(q["cm"], g, SSD_STATE)
            bgb, cgb = bg.astype(bf16), cg.astype(bf16)
            bgs.append(bgb)
            cgs.append(cgb)
            stg = _group(st, g, GW).astype(bf16)
            dsng = _group(dsn, g, GW).astype(bf16)
            dzg = _group(dz_in, g, GW).astype(bf16)
            g_mat.append(_dot_nt(cgb, bgb))
            gt_mat.append(_dot_nt(bgb, cgb))
            y_off.append(_dot(cgb, stg))
            dcm.append(_dot_nt(dzg, stg))
            dsp.append(_dot(cg.T.astype(bf16), dzg))
            dbm.append(_dot_nt(_group(xd, g, GW).astype(bf16), dsng))
            dxd.append(_dot(bgb, dsng))
        y_off = jnp.concatenate(y_off, axis=1) * q["ecs_e"]
        dxd = jnp.concatenate(dxd, axis=1)
        ds_ref[...] = dsn * q["cd_e"] + jnp.concatenate(dsp, axis=1)
        dcd_row = jnp.sum(dsn * st, axis=0, keepdims=True) * q["cd_e"]
        t_e = dxd * xd
        dcs = _contract(dy * y_off - t_e, e)
        last_row = _contract(jnp.broadcast_to(dcd_row + jnp.sum(t_e, axis=0, keepdims=True), (8, SSD_WIDTH)), e)[0:1, :]
        dxc_state = dxd * q["dte_e"]

        dg_acc = [jnp.zeros((CHUNK, CHUNK), f32), jnp.zeros((CHUNK, CHUNK), f32)]
        dgt_acc = [jnp.zeros((CHUNK, CHUNK), f32), jnp.zeros((CHUNK, CHUNK), f32)]
        dxc_pairs = []
        for j in range(SSD_HEADS // 2):
            dyp_f = dy[:, j * LANES:(j + 1) * LANES]
            xcp_f = xc[:, j * LANES:(j + 1) * LANES]
            acc = jnp.zeros((CHUNK, LANES), f32)
            for hh in range(2):
                h = 2 * j + hh
                g = h // 8
                dyp = jnp.where(halves[hh], dyp_f, 0.0).astype(bf16)
                xcp = jnp.where(halves[hh], xcp_f, 0.0).astype(bf16)
                lm = jnp.exp(jnp.where(tri, cs[:, h:h + 1] - cs_t[h:h + 1, :], -1e30))
                lm_t = jnp.exp(jnp.where(tri_t, cs_t[h:h + 1, :] - cs[:, h:h + 1], -1e30))
                dm = _dot_nt(dyp, xcp) * lm
                dm_t = _dot_nt(xcp, dyp) * lm_t
                acc = acc + _dot((gt_mat[g] * lm_t).astype(bf16), dyp)
                dg_acc[g] = dg_acc[g] + dm
                dgt_acc[g] = dgt_acc[g] + dm_t
                qd = jnp.sum(dm * g_mat[g] - dm_t * gt_mat[g], axis=1, keepdims=True)
                dcs = dcs + qd * (lane == h).astype(f32)
            dxc_pairs.append(acc)
        dxc = jnp.concatenate(dxc_pairs, axis=1) + dxc_state
        for g in range(2):
            dcm[g] = dcm[g] + _dot(dg_acc[g].astype(bf16), bgs[g])
            dbm[g] = dbm[g] + _dot(dgt_acc[g].astype(bf16), cgs[g])

        dxs = dxs + dxc * q["dt_e"]
        ddt = _contract(dxc * xs, e)
        dcs = dcs + jnp.where(row_i == CHUNK - 1, jnp.broadcast_to(last_row, (CHUNK, LANES)), 0.0)
        da = jnp.dot(tri_t.astype(f32), dcs, precision=lax.Precision.HIGHEST, preferred_element_type=f32)
        ddt = ddt + da * q["a_neg"]
        dal_ref[...] += jnp.sum(da * dt, axis=0, keepdims=True) * q["a_neg"]
        ddtr = ddt * _sigmoid(dtr_b) * (lane < SSD_HEADS).astype(f32)
        ddtb_ref[...] += jnp.sum(ddtr, axis=0, keepdims=True)
        ddtr_ref[...] = ddtr.astype(bf16)

        dact = jnp.concatenate([dxs, dbm[0], dbm[1], dcm[0], dcm[1]], axis=1)
        dpre_ref[...] = dact * _dsilu(pre)

    rev = lambda n: pl.BlockSpec((CHUNK, n), lambda c: (nc - 1 - c, 0))
    vec = _full((1, LANES))
    vshape = jax.ShapeDtypeStruct((1, LANES), f32)
    return _pallas(
        body, name="ssd_bwd", grid=(nc,),
        in_specs=[rev(SSD_WIDTH), rev(SSD_WIDTH), rev(SSD_WIDTH), rev(XBC_WIDTH), rev(LANES),
                  pl.BlockSpec((1, SSD_STATE, SSD_WIDTH), lambda c: (nc - 1 - c, 0, 0)),
                  vec, vec, _full((1, SSD_WIDTH)), _full((1, SSD_WIDTH))],
        out_specs=[rev(SSD_WIDTH), rev(XBC_WIDTH), rev(LANES), _full((1, SSD_WIDTH)), vec, vec, vec],
        out_shape=[jax.ShapeDtypeStruct((T, SSD_WIDTH), bf16), jax.ShapeDtypeStruct((T, XBC_WIDTH), f32),
                   jax.ShapeDtypeStruct((T, LANES), bf16), jax.ShapeDtypeStruct((1, SSD_WIDTH), f32),
                   vshape, vshape, vshape],
        scratch_shapes=[pltpu.VMEM((SSD_STATE, SSD_WIDTH), f32)],
        compiler_params=_cparams(("arbitrary",), VMEM_MID),
    )(dys, y, z, pre, dtr, sprev, dtb, alog, dskip_e, gn)


def _in_proj_bwd(dz, dxbc, dcv, dcg, ddt, gin, x, dh1, g):
    T = x.shape[0]
    tm = min(256, T)

    def body(dz_ref, dx_ref, dcv_ref, dcg_ref, ddt_ref, gin_ref, x_ref, dh_ref, g_ref, gx_ref, dg_ref, wt_ref):
        @pl.when(pl.program_id(0) == 0)
        def _():
            dg_ref[...] = jnp.zeros_like(dg_ref)
            _assemble_w_in_t(gin_ref, wt_ref)

        du = (_dot(dz_ref[...], wt_ref[0:O_XBC, :]) + _dot(dx_ref[...], wt_ref[O_XBC:O_DT, :])
              + _dot(dcv_ref[...], wt_ref[O_CV:O_CG, :]) + _dot(dcg_ref[...], wt_ref[O_CG:IN_WIDTH, :])
              + _dot(ddt_ref[...], wt_ref[O_DT:O_DT + LANES, :]))
        dx, dg = _rms_bwd(du, x_ref[...], g_ref[...])
        dg_ref[...] += dg
        gx_ref[...] = dh_ref[...] + dx

    row = lambda n: pl.BlockSpec((tm, n), lambda i: (i, 0))
    return _pallas(
        body, name="in_proj_bwd", grid=(T // tm,),
        in_specs=[row(SSD_WIDTH), row(XBC_WIDTH), row(CONF_WIDTH), row(CONF_WIDTH), row(LANES), _full(gin.shape),
                  row(D_MODEL), row(D_MODEL), _full((1, D_MODEL))],
        out_specs=[row(D_MODEL), _full((1, D_MODEL))],
        out_shape=[jax.ShapeDtypeStruct((T, D_MODEL), f32), jax.ShapeDtypeStruct((1, D_MODEL), f32)],
        scratch_shapes=[pltpu.VMEM((IN_WIDTH, D_MODEL), bf16)],
        compiler_params=_cparams(("arbitrary",), VMEM_BIG),
    )(dz, dxbc, dcv, dcg, ddt, gin, x, dh1, g)


def _weight_grad(a, g, name, square=False, slab=None, place=None, tk=512):
    T, K = a.shape
    N = g.shape[1]
    tk = min(tk, K)
    tn = 1024 if N % 1024 == 0 else min(512, N)
    tt = min(2048, T)

    def body(a_ref, g_ref, *rest):
        o_ref = rest[-1]
        acc = _dot_tn(_operand(a_ref[...]), g_ref[...].astype(bf16))
        t = pl.program_id(2)
        shaped = acc if slab is None else acc[None]

        @pl.when(t == 0)
        def _():
            o_ref[...] = shaped

        @pl.when(t > 0)
        def _():
            o_ref[...] += shaped

    def _operand(av):
        if square:
            av = av.astype(f32)
            av = av * av
        return av.astype(bf16)

    in_specs = [pl.BlockSpec((tt, tk), lambda i, j, t: (t, i)), pl.BlockSpec((tt, tn), lambda i, j, t: (t, j))]
    grid = (K // tk, N // tn, T // tt)
    params = _cparams(("parallel", "parallel", "arbitrary"), VMEM_MID)
    if slab is None:
        return _pallas(
            body, pin_bytes=PIN_SMALL, name=name, grid=grid, in_specs=in_specs,
            out_specs=pl.BlockSpec((tk, tn), lambda i, j, t: (i, j)),
            out_shape=jax.ShapeDtypeStruct((K, N), f32), compiler_params=params,
        )(a, g)
    return _pallas(
        body, pin_bytes=PIN_SMALL, name=name, grid=grid, in_specs=in_specs + [ANY],
        out_specs=pl.BlockSpec((1, tk, tn), lambda i, j, t: place(i, j)),
        out_shape=jax.ShapeDtypeStruct(slab.shape, f32), input_output_aliases={2: 0}, compiler_params=params,
    )(a, g, slab)


def _place():
    return lax.axis_index("x"), lax.axis_index("y"), lax.axis_index("c")


def _other_chips(x, y):
    return [(1 - x, y), (x, 1 - y), (1 - x, 1 - y)]


def _remote(src, dst, ssem, rsem, dev):
    return pltpu.make_async_remote_copy(src_ref=src, dst_ref=dst, send_sem=ssem, recv_sem=rsem, device_id=dev,
                                        device_id_type=MESH)


def _gather_weights(arrays, convw):
    n = len(arrays)
    halves = tuple(a.shape[1] // 2 for a in arrays)

    def body(*refs):
        cw_ref, cwo_ref = refs[n], refs[2 * n + 1]
        ssem, rsem, lsem = refs[2 * n + 2:]
        triples = tuple(zip(refs[:n], refs[n + 1:2 * n + 1], halves))
        x, y, c = _place()
        me_b = 2 * x + y
        sib = (x, y, 1 - c)
        chips = _other_chips(x, y)
        loc = pltpu.make_async_copy(cw_ref, cwo_ref.at[me_b], lsem)
        loc.start()
        sends = []
        for j, (src, dst, h) in enumerate(triples):
            mine = pl.ds(c * h, h)
            for k, (px, py) in enumerate(chips):
                s = 6 * j + k
                sends.append(_remote(src.at[me_b, mine], dst.at[me_b, mine], ssem.at[s], rsem.at[s], (px, py, c)))
        for k, (px, py) in enumerate(chips):
            sends.append(_remote(cw_ref, cwo_ref.at[me_b], ssem.at[6 * n + k], rsem.at[6 * n + k], (px, py, c)))
        for cp in sends:
            cp.start()
        for j, (src, dst, h) in enumerate(triples):
            mine = pl.ds(c * h, h)
            for k, (px, py) in enumerate(chips):
                b = 2 * px + py
                s = 6 * j + k
                _remote(src.at[b, mine], dst.at[b, mine], ssem.at[s], rsem.at[s], (px, py, c)).wait_recv()
                fw = _remote(dst.at[b, mine], dst.at[b, mine], ssem.at[s + 3], rsem.at[s + 3], sib)
                fw.start()
                sends.append(fw)
        for k, (px, py) in enumerate(chips):
            b = 2 * px + py
            _remote(cw_ref, cwo_ref.at[b], ssem.at[6 * n + k], rsem.at[6 * n + k], (px, py, c)).wait_recv()
        for j, (src, dst, h) in enumerate(triples):
            theirs = pl.ds((1 - c) * h, h)
            for k, (px, py) in enumerate(chips):
                b = 2 * px + py
                s = 6 * j + k + 3
                _remote(src.at[b, theirs], dst.at[b, theirs], ssem.at[s], rsem.at[s], sib).wait_recv()
        for cp in sends:
            cp.wait_send()
        loc.wait()

    return _pallas(
        body, name="gather_weights", in_specs=[ANY] * (n + 1), out_specs=[ANY] * (n + 1),
        out_shape=[jax.ShapeDtypeStruct(a.shape, bf16) for a in arrays]
        + [jax.ShapeDtypeStruct((N_CHIPS, CONVW_ROWS, D_MODEL), f32)],
        input_output_aliases={j: j for j in range(n)},
        scratch_shapes=[pltpu.SemaphoreType.DMA((6 * n + 3,)), pltpu.SemaphoreType.DMA((6 * n + 3,)),
                        pltpu.SemaphoreType.DMA(())],
    )(*arrays, convw)


def _gather_rider(gath0, lo, n):
    h = gath0.shape[1] // 2

    def copies(rins, routs, ssem, rsem, sending):
        (g_ref,), (o_ref,) = rins, routs
        x, y, c = _place()
        mine = pl.ds(c * h + lo, n)
        for k, (px, py) in enumerate(_other_chips(x, y)):
            b = 2 * x + y if sending else 2 * px + py
            yield _remote(g_ref.at[b, mine], o_ref.at[b, mine], ssem.at[k], rsem.at[k], (px, py, c))

    def start(*refs):
        for cp in copies(*refs, sending=True):
            cp.start()

    def finish(*refs):
        for cp in copies(*refs, sending=False):
            cp.wait()

    return _Rider([gath0], [jax.ShapeDtypeStruct(gath0.shape, gath0.dtype)], {0: 0}, 3, start, finish)


def _forward_to_sibling(gath):
    h = gath.shape[1] // 2

    def body(g_ref, o_ref, ssem, rsem):
        x, y, c = _place()
        sib = (x, y, 1 - c)
        mine, theirs = pl.ds(c * h, h), pl.ds((1 - c) * h, h)
        blocks = [2 * px + py for px, py in _other_chips(x, y)]
        sends = [_remote(g_ref.at[b, mine], o_ref.at[b, mine], ssem.at[k], rsem.at[k], sib) for k, b in enumerate(blocks)]
        for cp in sends:
            cp.start()
        for k, b in enumerate(blocks):
            _remote(g_ref.at[b, theirs], o_ref.at[b, theirs], ssem.at[k], rsem.at[k], sib).wait_recv()
        for cp in sends:
            cp.wait_send()

    return _pallas(
        body, name="forward_to_sibling", in_specs=[ANY], out_specs=ANY,
        out_shape=jax.ShapeDtypeStruct(gath.shape, gath.dtype), input_output_aliases={0: 0},
        scratch_shapes=[pltpu.SemaphoreType.DMA((3,)), pltpu.SemaphoreType.DMA((3,))],
    )(gath)


def _swap_copy(g_ref, r_ref, ssem, rsem):
    x, y, c = _place()
    h = r_ref.shape[1]
    return _remote(g_ref.at[:, pl.ds((1 - c) * h, h), :], r_ref, ssem.at[0], rsem.at[0], (x, y, 1 - c))


def _swap_rider(g):
    def start(rins, routs, ssem, rsem):
        _swap_copy(rins[0], routs[0], ssem, rsem).start()

    def finish(rins, routs, ssem, rsem):
        _swap_copy(rins[0], routs[0], ssem, rsem).wait()

    return _Rider([g], [jax.ShapeDtypeStruct((N_CHIPS, g.shape[1] // 2, g.shape[2]), g.dtype)], {}, 1, start, finish)


def _swap_halves(g):
    def body(g_ref, r_ref, ssem, rsem):
        cp = _swap_copy(g_ref, r_ref, ssem, rsem)
        cp.start()
        cp.wait()

    return _pallas(
        body, name="swap_halves", in_specs=[ANY], out_specs=ANY,
        out_shape=jax.ShapeDtypeStruct((N_CHIPS, g.shape[1] // 2, g.shape[2]), g.dtype),
        scratch_shapes=[pltpu.SemaphoreType.DMA((1,)), pltpu.SemaphoreType.DMA((1,))],
    )(g)


def _chip_sum(cidx, gslab, recv, name):
    half, C = recv.shape[1:]
    tr = half // 2 if (half // 2) % 16 == 0 else half

    def body(c_ref, g_ref, r_ref, o_ref):
        o_ref[...] = (g_ref[...] + r_ref[...]).astype(bf16)

    return _pallas(
        body, name=name,
        grid_spec=pltpu.PrefetchScalarGridSpec(
            num_scalar_prefetch=1, grid=(N_CHIPS, half // tr),
            in_specs=[pl.BlockSpec((1, tr, C), lambda b, i, c_ref: (b, c_ref[0] * (half // tr) + i, 0)),
                      pl.BlockSpec((1, tr, C), lambda b, i, c_ref: (b, i, 0))],
            out_specs=pl.BlockSpec((1, tr, C), lambda b, i, c_ref: (b, i, 0))),
        out_shape=jax.ShapeDtypeStruct((N_CHIPS, half, C), bf16),
        compiler_params=_cparams(("parallel", "parallel"), VMEM_MID),
    )(cidx, gslab, recv)


def _exchange_rider(h):
    def copies(rins, routs, ssem, rsem):
        x, y, c = _place()
        for k, (px, py) in enumerate(_other_chips(x, y)):
            yield _remote(rins[0].at[2 * px + py], routs[0].at[k], ssem.at[k], rsem.at[k], (px, py, c))

    def start(*refs):
        for cp in copies(*refs):
            cp.start()

    def finish(*refs):
        for cp in copies(*refs):
            cp.wait()

    return _Rider([h], [jax.ShapeDtypeStruct((3,) + h.shape[1:], h.dtype)], {}, 3, start, finish)


def _exchange(hb, small):
    def body(hb_ref, sm_ref, rb_ref, all_ref, ssem, rsem, lsem):
        x, y, c = _place()
        me = 4 * x + 2 * y + c
        chips = _other_chips(x, y)
        loc = pltpu.make_async_copy(sm_ref, all_ref.at[me], lsem)
        loc.start()
        sends = []
        for k, (px, py) in enumerate(chips):
            sends.append(_remote(hb_ref.at[2 * px + py], rb_ref.at[k], ssem.at[3 + k], rsem.at[3 + k], (px, py, c)))
        peers = []
        for r in range(1, N_DEV):
            peer = ((1 - x) if r & 4 else x, (1 - y) if r & 2 else y, (1 - c) if r & 1 else c)
            peers.append(peer)
            sends.append(_remote(sm_ref, all_ref.at[me], ssem.at[5 + r], rsem.at[5 + r], peer))
        for cp in sends:
            cp.start()
        for k, (px, py) in enumerate(chips):
            _remote(hb_ref.at[0], rb_ref.at[k], ssem.at[3 + k], rsem.at[3 + k], (px, py, c)).wait_recv()
        for r, peer in zip(range(1, N_DEV), peers):
            pid = 4 * peer[0] + 2 * peer[1] + peer[2]
            _remote(sm_ref, all_ref.at[pid], ssem.at[5 + r], rsem.at[5 + r], peer).wait_recv()
        for cp in sends:
            cp.wait_send()
        loc.wait()

    return _pallas(
        body, name="exchange", in_specs=[ANY, ANY], out_specs=[ANY, ANY],
        out_shape=[jax.ShapeDtypeStruct((3,) + hb.shape[1:], bf16),
                   jax.ShapeDtypeStruct((N_DEV, SMALL_ROWS, D_MODEL), f32)],
        scratch_shapes=[pltpu.SemaphoreType.DMA((13,)), pltpu.SemaphoreType.DMA((13,)), pltpu.SemaphoreType.DMA(())],
    )(hb, small)


def _final_sum(idx, gslab, recv_sib, recv_ici, name):
    half, C = recv_sib.shape[1:]
    tr = half // 2 if (half // 2) % 16 == 0 else half

    def body(i_ref, g_ref, r_ref, p_ref, o_ref):
        acc = g_ref[0] + r_ref[0]
        for k in range(3):
            acc = acc + p_ref[k].astype(f32)
        o_ref[...] = acc

    return _pallas(
        body, name=name,
        grid_spec=pltpu.PrefetchScalarGridSpec(
            num_scalar_prefetch=1, grid=(half // tr,),
            in_specs=[pl.BlockSpec((1, tr, C), lambda i, s: (s[1], s[0] * (half // tr) + i, 0)),
                      pl.BlockSpec((1, tr, C), lambda i, s: (s[1], i, 0)),
                      pl.BlockSpec((3, tr, C), lambda i, s: (0, i, 0))],
            out_specs=pl.BlockSpec((tr, C), lambda i, s: (s[0] * (half // tr) + i, 0))),
        out_shape=jax.ShapeDtypeStruct((2 * half, C), f32),
        compiler_params=_cparams(("parallel",), VMEM_MID),
    )(idx, gslab, recv_sib, recv_ici)


def _join_halves(ra, rb):
    ha, hb = ra.shape[0] // 2, rb.shape[0] // 2

    def body(a_ref, b_ref, ao_ref, bo_ref, ssem, rsem):
        x, y, c = _place()
        sib = (x, y, 1 - c)
        mine_a, theirs_a = pl.ds(c * ha, ha), pl.ds((1 - c) * ha, ha)
        mine_b, theirs_b = pl.ds(c * hb, hb), pl.ds((1 - c) * hb, hb)
        ca = _remote(a_ref.at[mine_a], ao_ref.at[mine_a], ssem.at[0], rsem.at[0], sib)
        cb = _remote(b_ref.at[mine_b], bo_ref.at[mine_b], ssem.at[1], rsem.at[1], sib)
        ca.start()
        cb.start()
        _remote(a_ref.at[theirs_a], ao_ref.at[theirs_a], ssem.at[0], rsem.at[0], sib).wait_recv()
        _remote(b_ref.at[theirs_b], bo_ref.at[theirs_b], ssem.at[1], rsem.at[1], sib).wait_recv()
        ca.wait_send()
        cb.wait_send()

    return _pallas(
        body, name="join_halves", in_specs=[ANY, ANY], out_specs=[ANY, ANY],
        out_shape=[jax.ShapeDtypeStruct(ra.shape, f32), jax.ShapeDtypeStruct(rb.shape, f32)],
        input_output_aliases={0: 0, 1: 1},
        scratch_shapes=[pltpu.SemaphoreType.DMA((2,)), pltpu.SemaphoreType.DMA((2,))],
    )(ra, rb)


def _shard_rows(gt):
    def body(g_ref, o_ref):
        for b in range(N_CHIPS):
            o_ref[b, 0:W_IN_ROWS, :] = g_ref[b * W_IN_ROWS:(b + 1) * W_IN_ROWS, :]
            o_ref[b, W_IN_ROWS:W_IN_ROWS_PAD, :] = jnp.zeros((W_IN_ROWS_PAD - W_IN_ROWS, LANES), f32)

    return _pallas(
        body, name="shard_rows", grid=(D_MODEL // LANES,),
        in_specs=[pl.BlockSpec((IN_WIDTH, LANES), lambda i: (0, i))],
        out_specs=pl.BlockSpec((N_CHIPS, W_IN_ROWS_PAD, LANES), lambda i: (0, 0, i)),
        out_shape=jax.ShapeDtypeStruct((N_CHIPS, W_IN_ROWS_PAD, D_MODEL), f32),
        compiler_params=_cparams(("parallel",), VMEM_MID),
    )(gt)


def _sum_small(all_small):
    def body(a_ref, o_ref):
        acc = a_ref[0]
        for d in range(1, N_DEV):
            acc = acc + a_ref[d]
        o_ref[...] = acc

    return _pallas(
        body, name="sum_small", out_shape=jax.ShapeDtypeStruct((SMALL_ROWS, D_MODEL), f32),
    )(all_small)


def _adamw(w, g, m, v, name, g_off=0, block_rows=None):
    R, C = w.shape
    tr = block_rows or (256 if R % 256 == 0 else R)
    assert g_off % tr == 0 and R % tr == 0
    c1 = 1.0 - ADAM_B1 ** ADAM_STEP
    c2 = 1.0 - ADAM_B2 ** ADAM_STEP

    def body(w_ref, g_ref, m_ref, v_ref, d_ref, mo_ref, vo_ref):
        gg = g_ref[...]
        m2 = ADAM_B1 * m_ref[...] + (1.0 - ADAM_B1) * gg
        v2 = ADAM_B2 * v_ref[...] + (1.0 - ADAM_B2) * (gg * gg)
        mo_ref[...] = m2
        vo_ref[...] = v2
        d_ref[...] = -ADAM_LR * ((m2 / c1) / (jnp.sqrt(v2 / c2) + ADAM_EPS) + ADAM_WD * w_ref[...])

    blk = pl.BlockSpec((tr, C), lambda i: (i, 0))
    gblk = pl.BlockSpec((tr, C), lambda i: (g_off // tr + i, 0))
    grid = (R // tr,)
    shp = jax.ShapeDtypeStruct((R, C), f32)
    return _pallas(
        body, pin_bytes=PIN_SMALL, name=name, grid=grid, in_specs=[blk, gblk, blk, blk], out_specs=[blk] * 3,
        out_shape=[shp] * 3,
        compiler_params=_cparams(("parallel",), VMEM_MID),
    )(w, g, m, v)


def _pad_lanes(v):
    return jnp.pad(v, ((0, 0), (0, LANES - v.shape[1])))


def _local_step(x, p, tgt, gath0, cidx, gin, S):
    dtb = _pad_lanes(S["dt_bias"])
    alog = _pad_lanes(S["A_log"])
    dskip_e = jnp.repeat(S["D_skip"], HEAD_DIM, axis=1)

    early = GATHER_EARLY_ROWS
    u0, z, xbc, cv, cg, dtr, v, gath1 = _in_proj_fwd(x, S["mix_norm_g"], gin, rider=_gather_rider(gath0, 0, early))
    co, yc, gath = _conf_fwd(v, S["conf_dw_w"], S["conf_dw_b"], S["conf_ln_g"], S["conf_ln_b"],
                             rider=_gather_rider(gath1, early, SLAB_A // 2 - early))
    gath = _forward_to_sibling(gath)
    w_ple = jnp.concatenate([_ple_of_slab(gath[b]) for b in range(N_CHIPS)], axis=1)
    pre = _ssd_conv_fwd(xbc, S["ssd_conv_w"], S["ssd_conv_b"])
    y, ys, sprev = _ssd_fwd(pre, dtr, z, dtb, alog, dskip_e, S["ssd_norm_g"])
    h1, u1 = _out_proj_fwd(x, ys, yc, gath, S["mlp_norm_g"])
    r, h2, u2 = _mlp_fwd(h1, u1, gath, S["ple_gate_norm_g"])
    loss, dh2, dh2b, dgp, dep, dg_fin, dg_ple, db_pg, dg_pg = _ple_loss(
        h2, u2, p, tgt, gath, S["b_ple_gate"], w_ple, S["ple_norm_g"], S["final_norm_g"], S["ple_gate_norm_g"])

    npg = D_MODEL // N_CHIPS
    ga = lax.empty((N_CHIPS, SLAB_A, D_MODEL), f32)
    ga = _weight_grad(u2, dgp, "dw_ple_gate", slab=ga, tk=npg, place=lambda i, j: (i, PG_OFF // npg, j))
    ga = _weight_grad(r, dh2b, "dw_down", square=True, slab=ga, place=lambda i, j: (i // 2, DOWN_OFF // 512 + i % 2, j))
    gw_ple = _weight_grad(p, dep, "dw_ple")
    dhp, dh1, dh1b, dg_mlp = _mlp_bwd(dh2, r, gath, h1, S["mlp_norm_g"])
    ga = _weight_grad(u1, dhp, "dw_up", slab=ga, place=lambda i, j: (j, UP_OFF // 512 + i, 0))
    ga = _weight_grad(ys, dh1b, "dw_out_ssd", slab=ga, place=lambda i, j: (i, OUT_OFF // 512, j))
    ga = _weight_grad(yc, dh1b, "dw_out_conf", slab=ga, place=lambda i, j: (2 + i, OUT_OFF // 512, j))
    n_ple = D_MODEL // N_CHIPS
    ple_rows = jnp.stack([_rows(gw_ple[:, b * n_ple:(b + 1) * n_ple]) for b in range(N_CHIPS)], axis=0)
    ga = lax.dynamic_update_slice(ga, ple_rows, (0, PLE_OFF, 0))
    dys, dco, dg_ln, db_ln, recv_a = _out_proj_bwd(dh1, gath, co, S["conf_ln_g"], S["conf_ln_b"], rider=_swap_rider(ga))
    ha = _chip_sum(cidx, ga, recv_a, "chip_sum_a")
    dcv, dcg, dw_conf, db_conf, ici_a = _conf_conv_bwd(dco, v, S["conf_dw_w"], cv, cg, rider=_exchange_rider(ha))
    dz, dpre, ddtr, dg_ssdn, dd, dal, ddtb = _ssd_bwd(dys, y, z, pre, dtr, sprev, dtb, alog, dskip_e, S["ssd_norm_g"])
    dxbc, dw_sconv, db_sconv = _ssd_conv_bwd(dpre, xbc, S["ssd_conv_w"])
    gx, dg_mix = _in_proj_bwd(dz, dxbc, dcv, dcg, ddtr, gin, x, dh1, S["mix_norm_g"])

    gw_in = jnp.concatenate([
        _weight_grad(dz, u0, "dw_in_z"), _weight_grad(dxbc, u0, "dw_in_xbc"),
        _weight_grad(ddtr, u0, "dw_in_dt")[:SSD_HEADS],
        _weight_grad(dcv, u0, "dw_in_cv"), _weight_grad(dcg, u0, "dw_in_cg")], axis=0)
    small = {
        "mix_norm_g": dg_mix, "ssd_conv_w": dw_sconv, "ssd_conv_b": db_sconv, "dt_bias": ddtb, "A_log": dal, "D_skip": dd,
        "ssd_norm_g": dg_ssdn, "conf_dw_w": dw_conf, "conf_dw_b": db_conf, "conf_ln_g": dg_ln, "conf_ln_b": db_ln,
        "mlp_norm_g": dg_mlp, "ple_gate_norm_g": dg_pg, "b_ple_gate": db_pg, "ple_norm_g": dg_ple,
        "final_norm_g": dg_fin, "loss": loss,
    }
    return gx, ga, recv_a, ici_a, gw_in, small


def _rows(a):
    return a.reshape(-1, D_MODEL)


def _pad_rows(a, n):
    flat = a.reshape(-1)
    return jnp.pad(flat, (0, n * D_MODEL - flat.shape[0])).reshape(n, D_MODEL)


def _ple_of_slab(slab):
    return slab[PLE_OFF:PLE_OFF + PLE_ROWS].reshape(PLE_DIM, D_MODEL // N_CHIPS)


ROW_VEC = {"mix_norm_g": 0, "ssd_norm_g": 1, "conf_dw_b": 2, "conf_ln_g": 3, "conf_ln_b": 4, "mlp_norm_g": 5,
           "ple_gate_norm_g": 6, "b_ple_gate": 7, "ple_norm_g": 8, "final_norm_g": 9}
ROW_CONV_B = 10
ROW_HEADS = 12
ROW_CONV_W = 16
ROW_DW = 24
HEAD_LANES = {"dt_bias": 0, "A_log": 1, "D_skip": 2, "loss": 3}
SMALL_ORDER = ("mix_norm_g", "ssd_conv_w", "ssd_conv_b", "dt_bias", "A_log", "D_skip", "ssd_norm_g", "conf_dw_w",
               "conf_dw_b", "conf_ln_g", "conf_ln_b", "mlp_norm_g", "ple_gate_norm_g", "b_ple_gate", "ple_norm_g",
               "final_norm_g")
SPLIT = XBC_WIDTH - D_MODEL


def _pack_small(raw):
    names = list(ROW_VEC) + ["ssd_conv_b", "dt_bias", "A_log", "D_skip", "loss", "ssd_conv_w", "conf_dw_w"]

    def body(*refs):
        r = dict(zip(names, refs[:-1]))
        o_ref = refs[-1]
        o_ref[...] = jnp.zeros_like(o_ref)
        for n, row in ROW_VEC.items():
            o_ref[row:row + 1, :] = r[n][...]
        o_ref[ROW_CONV_B:ROW_CONV_B + 1, :] = r["ssd_conv_b"][:, 0:D_MODEL]
        o_ref[ROW_CONV_B + 1:ROW_CONV_B + 2, 0:SPLIT] = r["ssd_conv_b"][:, D_MODEL:]
        for n, j in HEAD_LANES.items():
            o_ref[ROW_HEADS:ROW_HEADS + 1, j * LANES:(j + 1) * LANES] = r[n][0:1, :]
        for k in range(SSD_CONV):
            o_ref[ROW_CONV_W + 2 * k:ROW_CONV_W + 2 * k + 1, :] = r["ssd_conv_w"][k:k + 1, 0:D_MODEL]
            o_ref[ROW_CONV_W + 2 * k + 1:ROW_CONV_W + 2 * k + 2, 0:SPLIT] = r["ssd_conv_w"][k:k + 1, D_MODEL:]
        o_ref[ROW_DW:ROW_DW + 32, :] = r["conf_dw_w"][...]

    return _pallas(
        body, name="pack_small", out_shape=jax.ShapeDtypeStruct((SMALL_ROWS, D_MODEL), f32),
    )(*[raw[n] for n in names])


def _adamw_small(cidx, tot, w, m, v):
    c1 = 1.0 - ADAM_B1 ** ADAM_STEP
    c2 = 1.0 - ADAM_B2 ** ADAM_STEP
    n_par = len(SMALL_ORDER)

    def shard(full, chip, width):
        out = full[:, 0:width]
        for b in range(1, N_CHIPS):
            out = jnp.where(chip == b, full[:, b * width:(b + 1) * width], out)
        return out

    def grad_of(n, t_ref, chip):
        if n in ROW_VEC:
            return t_ref[ROW_VEC[n]:ROW_VEC[n] + 1, :]
        if n == "ssd_conv_b":
            return jnp.concatenate([t_ref[ROW_CONV_B:ROW_CONV_B + 1, :], t_ref[ROW_CONV_B + 1:ROW_CONV_B + 2, 0:SPLIT]], axis=1)
        if n in HEAD_LANES:
            j = HEAD_LANES[n]
            return t_ref[ROW_HEADS:ROW_HEADS + 1, j * LANES:j * LANES + SSD_HEADS]
        if n == "ssd_conv_w":
            rows = [jnp.concatenate([t_ref[ROW_CONV_W + 2 * k:ROW_CONV_W + 2 * k + 1, :],
                                     t_ref[ROW_CONV_W + 2 * k + 1:ROW_CONV_W + 2 * k + 2, 0:SPLIT]], axis=1)
                    for k in range(SSD_CONV)]
            return shard(jnp.concatenate(rows, axis=0), chip, XBC_WIDTH // N_CHIPS)
        return shard(t_ref[ROW_DW:ROW_DW + CONF_KERNEL, :], chip, CONF_WIDTH // N_CHIPS)

    def body(c_ref, t_ref, *refs):
        ins, outs = refs[:3 * n_par], refs[3 * n_par:]
        chip = c_ref[1]
        for i, n in enumerate(SMALL_ORDER):
            w_ref, m_ref, v_ref = ins[3 * i:3 * i + 3]
            g_ref, d_ref, mo_ref, vo_ref = outs[4 * i:4 * i + 4]
            g = grad_of(n, t_ref, chip)
            m2 = ADAM_B1 * m_ref[...] + (1.0 - ADAM_B1) * g
            v2 = ADAM_B2 * v_ref[...] + (1.0 - ADAM_B2) * (g * g)
            g_ref[...] = g
            mo_ref[...] = m2
            vo_ref[...] = v2
            d_ref[...] = -ADAM_LR * ((m2 / c1) / (jnp.sqrt(v2 / c2) + ADAM_EPS) + ADAM_WD * w_ref[...])

    args, in_specs, out_specs, out_shape = [], [], [], []
    for n in SMALL_ORDER:
        shp = w[n].shape
        spec = pl.BlockSpec(shp, lambda i, c_ref: (0, 0))
        args += [w[n], m[n], v[n]]
        in_specs += [spec] * 3
        out_specs += [spec] * 4
        out_shape += [jax.ShapeDtypeStruct(shp, f32)] * 4
    outs = _pallas(
        body, name="adamw_small",
        grid_spec=pltpu.PrefetchScalarGridSpec(
            num_scalar_prefetch=1, grid=(1,),
            in_specs=[pl.BlockSpec(tot.shape, lambda i, c_ref: (0, 0))] + in_specs, out_specs=out_specs),
        out_shape=out_shape,
    )(cidx, tot, *args)
    grad, delta, new_m, new_v = {}, {}, {}, {}
    for i, n in enumerate(SMALL_ORDER):
        grad[n], delta[n], new_m[n], new_v[n] = outs[4 * i:4 * i + 4]
    return grad, delta, new_m, new_v


BIG = ("w_in", "w_out", "w_up", "w_down", "w_ple_gate", "w_ple")
BIG_A = (("w_up", UP_OFF), ("w_down", DOWN_OFF), ("w_out", OUT_OFF), ("w_ple_gate", PG_OFF))
WEIGHTS = ("mix_norm_g", "w_in", "ssd_conv_w", "ssd_conv_b", "dt_bias", "A_log", "D_skip", "ssd_norm_g", "conf_dw_w",
           "conf_dw_b", "conf_ln_g", "conf_ln_b", "w_out", "mlp_norm_g", "w_up", "w_down", "ple_gate_norm_g",
           "w_ple_gate", "b_ple_gate", "w_ple", "ple_norm_g", "final_norm_g")


def kernel(x, p, mix_norm_g, w_in, ssd_conv_w, ssd_conv_b, dt_bias, A_log, D_skip, ssd_norm_g, conf_dw_w, conf_dw_b, conf_ln_g, conf_ln_b, w_out, mlp_norm_g, w_up, w_down, ple_gate_norm_g, w_ple_gate, b_ple_gate, w_ple, ple_norm_g, final_norm_g, loss_target, m_mix_norm_g, m_w_in, m_ssd_conv_w, m_ssd_conv_b, m_dt_bias, m_A_log, m_D_skip, m_ssd_norm_g, m_conf_dw_w, m_conf_dw_b, m_conf_ln_g, m_conf_ln_b, m_w_out, m_mlp_norm_g, m_w_up, m_w_down, m_ple_gate_norm_g, m_w_ple_gate, m_b_ple_gate, m_w_ple, m_ple_norm_g, m_final_norm_g, v_mix_norm_g, v_w_in, v_ssd_conv_w, v_ssd_conv_b, v_dt_bias, v_A_log, v_D_skip, v_ssd_norm_g, v_conf_dw_w, v_conf_dw_b, v_conf_ln_g, v_conf_ln_b, v_w_out, v_mlp_norm_g, v_w_up, v_w_down, v_ple_gate_norm_g, v_w_ple_gate, v_b_ple_gate, v_w_ple, v_ple_norm_g, v_final_norm_g):
    w = dict(mix_norm_g=mix_norm_g, w_in=w_in, ssd_conv_w=ssd_conv_w, ssd_conv_b=ssd_conv_b, dt_bias=dt_bias, A_log=A_log,
             D_skip=D_skip, ssd_norm_g=ssd_norm_g, conf_dw_w=conf_dw_w, conf_dw_b=conf_dw_b, conf_ln_g=conf_ln_g,
             conf_ln_b=conf_ln_b, w_out=w_out, mlp_norm_g=mlp_norm_g, w_up=w_up, w_down=w_down,
             ple_gate_norm_g=ple_gate_norm_g, w_ple_gate=w_ple_gate, b_ple_gate=b_ple_gate, w_ple=w_ple,
             ple_norm_g=ple_norm_g, final_norm_g=final_norm_g)
    m = dict(mix_norm_g=m_mix_norm_g, w_in=m_w_in, ssd_conv_w=m_ssd_conv_w, ssd_conv_b=m_ssd_conv_b, dt_bias=m_dt_bias,
             A_log=m_A_log, D_skip=m_D_skip, ssd_norm_g=m_ssd_norm_g, conf_dw_w=m_conf_dw_w, conf_dw_b=m_conf_dw_b,
             conf_ln_g=m_conf_ln_g, conf_ln_b=m_conf_ln_b, w_out=m_w_out, mlp_norm_g=m_mlp_norm_g, w_up=m_w_up,
             w_down=m_w_down, ple_gate_norm_g=m_ple_gate_norm_g, w_ple_gate=m_w_ple_gate, b_ple_gate=m_b_ple_gate,
             w_ple=m_w_ple, ple_norm_g=m_ple_norm_g, final_norm_g=m_final_norm_g)
    v = dict(mix_norm_g=v_mix_norm_g, w_in=v_w_in, ssd_conv_w=v_ssd_conv_w, ssd_conv_b=v_ssd_conv_b, dt_bias=v_dt_bias,
             A_log=v_A_log, D_skip=v_D_skip, ssd_norm_g=v_ssd_norm_g, conf_dw_w=v_conf_dw_w, conf_dw_b=v_conf_dw_b,
             conf_ln_g=v_conf_ln_g, conf_ln_b=v_conf_ln_b, w_out=v_w_out, mlp_norm_g=v_mlp_norm_g, w_up=v_w_up,
             w_down=v_w_down, ple_gate_norm_g=v_ple_gate_norm_g, w_ple_gate=v_w_ple_gate, b_ple_gate=v_b_ple_gate,
             w_ple=v_w_ple, ple_norm_g=v_ple_norm_g, final_norm_g=v_final_norm_g)
    xi, yi, ci = lax.axis_index("x"), lax.axis_index("y"), lax.axis_index("c")
    chip = 2 * xi + yi

    slab = jnp.concatenate([w_up[0], w_down[0], w_out[0], w_ple_gate[0], _rows(w_ple[0])], axis=0).astype(bf16)
    gath0 = lax.dynamic_update_slice(lax.empty((N_CHIPS, SLAB_A, D_MODEL), bf16), slab[None], (chip, 0, 0))
    wt_shard = jnp.swapaxes(w_in, 1, 2).astype(bf16)
    wt_shard = jnp.pad(wt_shard, ((0, 0), (0, W_IN_ROWS_PAD - W_IN_ROWS), (0, 0)))
    gin0 = lax.dynamic_update_slice(lax.empty((N_CHIPS, W_IN_ROWS_PAD, D_MODEL), bf16), wt_shard, (chip, 0, 0))
    convw = _pad_rows(jnp.concatenate([ssd_conv_w[0].reshape(-1), conf_dw_w[0].reshape(-1)]), CONVW_ROWS)
    gin, cwg = _gather_weights([gin0], convw)
    n_sc = SSD_CONV * (XBC_WIDTH // N_CHIPS)
    n_cf = CONF_KERNEL * (CONF_WIDTH // N_CHIPS)
    S = {n: w[n][0] for n in ("mix_norm_g", "ssd_conv_b", "dt_bias", "A_log", "D_skip", "ssd_norm_g", "conf_dw_b",
                              "conf_ln_g", "conf_ln_b", "mlp_norm_g", "ple_gate_norm_g", "b_ple_gate", "ple_norm_g")}
    S = {n: a.reshape(1, -1) for n, a in S.items()}
    S["final_norm_g"] = final_norm_g.reshape(1, -1)
    S["ssd_conv_w"] = jnp.concatenate(
        [cwg[b].reshape(-1)[:n_sc].reshape(SSD_CONV, XBC_WIDTH // N_CHIPS) for b in range(N_CHIPS)], axis=1)
    S["conf_dw_w"] = jnp.concatenate(
        [cwg[b].reshape(-1)[n_sc:n_sc + n_cf].reshape(CONF_KERNEL, CONF_WIDTH // N_CHIPS) for b in range(N_CHIPS)], axis=1)

    cidx = jnp.stack([ci, chip]).astype(jnp.int32)
    grad_x, ga, recv_a, ici_a, gw_in, gsmall = _local_step(x[0], p[0, 0], loss_target[0], gath0, cidx, gin, S)

    gb = _shard_rows(gw_in)
    small = _pack_small(gsmall)
    recv_b = _swap_halves(gb)
    hb = _chip_sum(cidx, gb, recv_b, "chip_sum_b")
    ici_b, all_small = _exchange(hb, small)
    ra = _final_sum(cidx, ga, recv_a, ici_a, "final_sum_a")
    rb = _final_sum(cidx, gb, recv_b, ici_b, "final_sum_b")
    ra, rb = _join_halves(ra, rb)
    tot_small = _sum_small(all_small)

    loss = tot_small[ROW_HEADS, HEAD_LANES["loss"] * LANES]

    two_d = lambda a: a.reshape(a.shape[-2:]) if a.ndim > 1 else a.reshape(1, -1)
    small_w, small_m, small_v = ({n: two_d(d[n]) for n in SMALL_ORDER} for d in (w, m, v))
    grads, delta, new_m, new_v = _adamw_small(cidx, tot_small, small_w, small_m, small_v)
    lin = lambda a: jnp.swapaxes(a, 1, 2).reshape(W_IN_ROWS * D_MODEL // LANES, LANES)
    unlin = lambda a: jnp.swapaxes(a.reshape(1, W_IN_ROWS, D_MODEL), 1, 2)
    g_in = rb[:W_IN_ROWS].reshape(W_IN_ROWS * D_MODEL // LANES, LANES)
    grads["w_ple"] = _ple_of_slab(ra)
    grads["w_in"] = unlin(g_in)
    for n, off in BIG_A:
        grads[n] = ra[off:off + w[n].shape[1]]
        delta[n], new_m[n], new_v[n] = _adamw(w[n][0], ra, m[n][0], v[n][0], "adamw_" + n, g_off=off)
    delta["w_ple"], new_m["w_ple"], new_v["w_ple"] = _adamw(w_ple[0], grads["w_ple"], m_w_ple[0], v_w_ple[0], "adamw_w_ple")
    d_, m_, v_ = _adamw(lin(w_in), g_in, lin(m_w_in), lin(v_w_in), "adamw_w_in", block_rows=544)
    delta["w_in"], new_m["w_in"], new_v["w_in"] = (unlin(a) for a in (d_, m_, v_))

    shaped = lambda d: [d[n].reshape(w[n].shape) for n in WEIGHTS]
    return (loss, grad_x[None], *shaped(grads), *shaped(delta), *shaped(new_m), *shaped(new_v))
```

```python
import jax
import jax.numpy as jnp
from jax import lax
from jax.experimental import pallas as pl
from jax.experimental.pallas import tpu as pltpu

f32 = jnp.float32
bf16 = jnp.bfloat16

D_MODEL = 1024
SSD_WIDTH = 1024
SSD_HEADS = 16
HEAD_DIM = 64
SSD_STATE = 128
XBC_WIDTH = 1536
SSD_CONV = 4
CHUNK = 128
CONF_WIDTH = 1024
CONF_KERNEL = 31
D_FF = 4096
PLE_DIM = 256
IN_WIDTH = 4624
EPS = 1e-6
N_CHIPS = 4
N_DEV = 8

ADAM_LR = 0.001
ADAM_B1 = 0.9
ADAM_B2 = 0.999
ADAM_EPS = 1e-08
ADAM_WD = 0.01
ADAM_STEP = 10

LANES = 128
VMEM_BIG = 56 * 1024 * 1024
VMEM_MID = 40 * 1024 * 1024

UP_OFF, DOWN_OFF, OUT_OFF, PG_OFF, PLE_OFF = 0, 1024, 2048, 2560, 2816
PLE_ROWS = 64
SLAB_A = PLE_OFF + PLE_ROWS
GATHER_EARLY_ROWS = 480
W_IN_ROWS = 1156
W_IN_ROWS_PAD = 1184
CONVW_ROWS = 16
SMALL_ROWS = 56

MESH = pl.DeviceIdType.MESH
ANY = pl.BlockSpec(memory_space=pl.ANY)


PIN_SMALL = 256 * 1024


def _pallas(body, pin_bytes=None, **kw):
    call = pl.pallas_call(body, **kw)

    def pin(a):
        wanted = pin_bytes is None or a.size * a.dtype.itemsize <= pin_bytes
        return pltpu.with_memory_space_constraint(a, pltpu.HBM) if wanted and a.dtype != jnp.int32 else a

    def run(*args):
        return call(*[pin(a) for a in args])

    return run


def _cparams(sem=None, vmem=None):
    return pltpu.CompilerParams(dimension_semantics=sem, vmem_limit_bytes=vmem)


def _full(shape):
    n = len(shape)
    return pl.BlockSpec(shape, lambda *_: (0,) * n)


class _Rider:
    def __init__(self, inputs, out_shapes, aliases, n_sems, start, finish):
        self.inputs, self.out_shapes, self.aliases = list(inputs), list(out_shapes), dict(aliases)
        self.n_sems, self.start, self.finish = n_sems, start, finish


def _call(body, args, *, name, grid, in_specs, out_specs, out_shape, scratch_shapes=(), params=None, rider=None):
    if rider is None:
        return _pallas(body, name=name, grid=grid, in_specs=in_specs, out_specs=out_specs, out_shape=out_shape,
                              scratch_shapes=list(scratch_shapes), compiler_params=params)(*args)
    ni, no, ns = len(in_specs), len(out_specs), len(scratch_shapes)
    ri, ro = len(rider.inputs), len(rider.out_shapes)
    (steps,) = grid

    def with_rider(*refs):
        ins, refs = refs[:ni], refs[ni:]
        rins, refs = refs[:ri], refs[ri:]
        outs, refs = refs[:no], refs[no:]
        routs, refs = refs[:ro], refs[ro:]
        scratch, (ssem, rsem) = refs[:ns], refs[ns:]
        step = pl.program_id(0)

        @pl.when(step == 0)
        def _():
            rider.start(rins, routs, ssem, rsem)

        body(*ins, *outs, *scratch)

        @pl.when(step == steps - 1)
        def _():
            rider.finish(rins, routs, ssem, rsem)

    sems = [pltpu.SemaphoreType.DMA((rider.n_sems,)), pltpu.SemaphoreType.DMA((rider.n_sems,))]
    return _pallas(
        with_rider, name=name, grid=grid, in_specs=list(in_specs) + [ANY] * ri, out_specs=list(out_specs) + [ANY] * ro,
        out_shape=list(out_shape) + rider.out_shapes, scratch_shapes=list(scratch_shapes) + sems,
        input_output_aliases={ni + a: no + b for a, b in rider.aliases.items()}, compiler_params=params,
    )(*args, *rider.inputs)


def _dot(a, b):
    return jnp.dot(a, b, preferred_element_type=f32)


def _dot_nt(a, b):
    return lax.dot_general(a, b, (((1,), (1,)), ((), ())), preferred_element_type=f32)


def _dot_tn(a, b):
    return lax.dot_general(a, b, (((0,), (0,)), ((), ())), preferred_element_type=f32)


def _sigmoid(x):
    return jax.nn.sigmoid(x)


def _rms(x, g):
    r = lax.rsqrt(jnp.mean(x * x, axis=-1, keepdims=True) + EPS)
    return x * r * g


def _rms_bwd(dy, x, g):
    r = lax.rsqrt(jnp.mean(x * x, axis=-1, keepdims=True) + EPS)
    xh = x * r
    dg = jnp.sum(dy * xh, axis=0, keepdims=True)
    dxh = dy * g
    dx = r * (dxh - xh * jnp.mean(dxh * xh, axis=-1, keepdims=True))
    return dx, dg


def _dsilu(x):
    s = _sigmoid(x)
    return s * (1.0 + x * (1.0 - s))


def _split3(x):
    hi = x.astype(bf16)
    r1 = x - hi.astype(f32)
    mid = r1.astype(bf16)
    lo = (r1 - mid.astype(f32)).astype(bf16)
    return hi, mid, lo


def _head_matrix():
    row = lax.broadcasted_iota(jnp.int32, (LANES, SSD_WIDTH), 0)
    col = lax.broadcasted_iota(jnp.int32, (LANES, SSD_WIDTH), 1)
    lo = row * HEAD_DIM
    return ((col >= lo) & (col < lo + HEAD_DIM)).astype(bf16)


def _expand(x, e):
    hi, mid, lo = _split3(x)
    return _dot(hi, e) + _dot(mid, e) + _dot(lo, e)


def _contract(x, e):
    hi = x.astype(bf16)
    mid = (x - hi.astype(f32)).astype(bf16)
    return _dot_nt(hi, e) + _dot_nt(mid, e)


O_XBC = SSD_WIDTH
O_DT = O_XBC + XBC_WIDTH
O_CV = O_DT + SSD_HEADS
O_CG = O_CV + CONF_WIDTH


def _assemble_w_in_t(gin_ref, wt_ref):
    for b in range(N_CHIPS):
        wt_ref[b * W_IN_ROWS:(b + 1) * W_IN_ROWS, :] = gin_ref[b, 0:W_IN_ROWS, :]


def _in_proj_fwd(x, g, gin, rider=None):
    T = x.shape[0]
    tm = min(256, T)

    def body(x_ref, g_ref, gin_ref, u_ref, z_ref, xbc_ref, cv_ref, cg_ref, dt_ref, v_ref, wt_ref):
        @pl.when(pl.program_id(0) == 0)
        def _():
            _assemble_w_in_t(gin_ref, wt_ref)

        ub = _rms(x_ref[...], g_ref[...]).astype(bf16)
        u_ref[...] = ub
        z_ref[...] = _dot_nt(ub, wt_ref[0:O_XBC, :])
        xbc_ref[...] = _dot_nt(ub, wt_ref[O_XBC:O_DT, :])
        cv = _dot_nt(ub, wt_ref[O_CV:O_CG, :])
        cg = _dot_nt(ub, wt_ref[O_CG:IN_WIDTH, :])
        cv_ref[...] = cv
        cg_ref[...] = cg
        v_ref[...] = cv * _sigmoid(cg)
        dt_ref[...] = _dot_nt(ub, wt_ref[O_DT:O_DT + LANES, :])

    row = lambda n: pl.BlockSpec((tm, n), lambda i: (i, 0))
    return _call(
        body, (x, g, gin), name="in_proj_fwd", grid=(T // tm,),
        in_specs=[row(D_MODEL), _full((1, D_MODEL)), _full(gin.shape)],
        out_specs=[row(D_MODEL), row(SSD_WIDTH), row(XBC_WIDTH), row(CONF_WIDTH), row(CONF_WIDTH), row(LANES),
                   row(CONF_WIDTH)],
        out_shape=[jax.ShapeDtypeStruct((T, D_MODEL), bf16), jax.ShapeDtypeStruct((T, SSD_WIDTH), f32),
                   jax.ShapeDtypeStruct((T, XBC_WIDTH), f32), jax.ShapeDtypeStruct((T, CONF_WIDTH), f32),
                   jax.ShapeDtypeStruct((T, CONF_WIDTH), f32), jax.ShapeDtypeStruct((T, LANES), f32),
                   jax.ShapeDtypeStruct((T, CONF_WIDTH), f32)],
        scratch_shapes=[pltpu.VMEM((IN_WIDTH, D_MODEL), bf16)],
        params=_cparams(("arbitrary",), VMEM_BIG), rider=rider)


SUBLANES = 8


def _phases(offsets):
    return sorted({o % SUBLANES for o in offsets} - {0})


def _phase_shape(offsets, tm, C):
    a_max = max([o // SUBLANES for o in offsets if o % SUBLANES] or [0])
    return (max(len(_phases(offsets)), 1), tm + SUBLANES * a_max, C)


def _make_phases(buf_ref, ph_ref, offsets, tm):
    for idx, b in enumerate(_phases(offsets)):
        n = tm + SUBLANES * max(o // SUBLANES for o in offsets if o % SUBLANES == b)
        ph_ref[idx, 0:n, :] = buf_ref[pl.ds(b, n), :]


def _window(buf_ref, ph_ref, offsets, o, r0, rb):
    a, b = divmod(o, SUBLANES)
    if b == 0:
        return buf_ref[pl.ds(r0 + SUBLANES * a, rb), :]
    return ph_ref[_phases(offsets).index(b), pl.ds(r0 + SUBLANES * a, rb), :]


def _conv_rows(wb_ref, buf_ref, ph_ref, offsets, r0, rb):
    nsub = rb // SUBLANES
    accs = [None] * nsub
    for k, o in enumerate(offsets):
        wk = wb_ref[pl.ds(SUBLANES * k, SUBLANES), :]
        for s in range(nsub):
            term = wk * _window(buf_ref, ph_ref, offsets, o, r0 + SUBLANES * s, SUBLANES)
            accs[s] = term if accs[s] is None else accs[s] + term
    return accs[0] if nsub == 1 else jnp.concatenate(accs, axis=0)


def _sublane_rows(w):
    return jnp.repeat(w, SUBLANES, axis=0)


def _fwd_offsets(K, hb):
    return [hb - (K - 1) + k for k in range(K)]


def _prev_halo_spec(hb, tm, C):
    return pl.BlockSpec((hb, C), lambda i: (jnp.maximum(i * (tm // hb) - 1, 0), 0))


CONV_RB = 16


def _ssd_conv_fwd(xbc, w, b):
    T, C = xbc.shape
    K, hb = SSD_CONV, 8
    tm = min(256, T)
    offs = _fwd_offsets(K, hb)

    def body(cur_ref, halo_ref, w_ref, b_ref, pre_ref, buf_ref, ph_ref):
        keep = jnp.where(pl.program_id(0) > 0, 1.0, 0.0)
        buf_ref[0:hb, :] = halo_ref[...] * keep
        buf_ref[hb:hb + tm, :] = cur_ref[...]
        _make_phases(buf_ref, ph_ref, offs, tm)

        def chunk(i, carry):
            r0 = pl.multiple_of(i * CONV_RB, CONV_RB)
            pre_ref[pl.ds(r0, CONV_RB), :] = _conv_rows(w_ref, buf_ref, ph_ref, offs, r0, CONV_RB) + b_ref[...]
            return carry

        lax.fori_loop(0, tm // CONV_RB, chunk, 0)

    return _pallas(
        body, name="ssd_conv_fwd", grid=(T // tm,),
        in_specs=[pl.BlockSpec((tm, C), lambda i: (i, 0)), _prev_halo_spec(hb, tm, C), _full((SUBLANES * K, C)),
                  _full((1, C))],
        out_specs=pl.BlockSpec((tm, C), lambda i: (i, 0)),
        out_shape=jax.ShapeDtypeStruct((T, C), f32),
        scratch_shapes=[pltpu.VMEM((hb + tm, C), f32), pltpu.VMEM(_phase_shape(offs, tm, C), f32)],
        compiler_params=_cparams(("parallel",), VMEM_MID),
    )(xbc, xbc, _sublane_rows(w), b)


def _conf_fwd(v, w, b, ln_g, ln_b, rider=None):
    T, C = v.shape
    K, hb = CONF_KERNEL, 32
    tm = min(256, T)
    offs = _fwd_offsets(K, hb)
    rb = 2 * CONV_RB

    def body(cur_ref, halo_ref, w_ref, b_ref, g_ref, bb_ref, co_ref, y_ref, buf_ref, ph_ref):
        keep = jnp.where(pl.program_id(0) > 0, 1.0, 0.0)
        buf_ref[0:hb, :] = halo_ref[...] * keep
        buf_ref[hb:hb + tm, :] = cur_ref[...]
        _make_phases(buf_ref, ph_ref, offs, tm)

        def chunk(i, carry):
            r0 = pl.multiple_of(i * rb, rb)
            co = _conv_rows(w_ref, buf_ref, ph_ref, offs, r0, rb) + b_ref[...]
            co_ref[pl.ds(r0, rb), :] = co
            mu = jnp.mean(co, axis=-1, keepdims=True)
            xc = co - mu
            yn = xc * lax.rsqrt(jnp.mean(xc * xc, axis=-1, keepdims=True) + EPS) * g_ref[...] + bb_ref[...]
            y_ref[pl.ds(r0, rb), :] = (yn * _sigmoid(yn)).astype(bf16)
            return carry

        lax.fori_loop(0, tm // rb, chunk, 0)

    return _call(
        body, (v, v, _sublane_rows(w), b, ln_g, ln_b), name="conf_fwd", grid=(T // tm,),
        in_specs=[pl.BlockSpec((tm, C), lambda i: (i, 0)), _prev_halo_spec(hb, tm, C), _full((SUBLANES * K, C)),
                  _full((1, C)), _full((1, C)), _full((1, C))],
        out_specs=[pl.BlockSpec((tm, C), lambda i: (i, 0)), pl.BlockSpec((tm, C), lambda i: (i, 0))],
        out_shape=[jax.ShapeDtypeStruct((T, C), f32), jax.ShapeDtypeStruct((T, C), bf16)],
        scratch_shapes=[pltpu.VMEM((hb + tm, C), f32), pltpu.VMEM(_phase_shape(offs, tm, C), f32)],
        params=_cparams(("arbitrary",), VMEM_MID), rider=rider)


def _ssd_chunk_common(pre, dtr, dtb, alog, e):
    act = pre * _sigmoid(pre)
    xs = act[:, :SSD_WIDTH]
    bm = act[:, SSD_WIDTH:SSD_WIDTH + 2 * SSD_STATE]
    cm = act[:, SSD_WIDTH + 2 * SSD_STATE:]
    row = lax.broadcasted_iota(jnp.int32, (CHUNK, CHUNK), 0)
    col = lax.broadcasted_iota(jnp.int32, (CHUNK, CHUNK), 1)
    tri = row >= col
    dt = jax.nn.softplus(dtr + dtb)
    a_neg = -jnp.exp(alog)
    a = dt * a_neg
    cs = jnp.dot(tri.astype(f32), a, precision=lax.Precision.HIGHEST, preferred_element_type=f32)
    cs_e = _expand(cs, e)
    dt_e = _expand(dt, e)
    csl_e = cs_e[CHUNK - 1:CHUNK, :]
    ecs_e = jnp.exp(cs_e)
    dte_e = jnp.exp(csl_e - cs_e)
    cd_e = jnp.exp(csl_e)
    xc = xs * dt_e
    xd = xc * dte_e
    return dict(xs=xs, bm=bm, cm=cm, tri=tri, dt=dt, a_neg=a_neg, cs=cs, ecs_e=ecs_e, dte_e=dte_e, cd_e=cd_e,
                dt_e=dt_e, xc=xc, xd=xd)


def _group(v, g, width):
    return v[:, g * width:(g + 1) * width]


def _ssd_fwd(pre, dtr, z, dtb, alog, dskip_e, gn):
    T = pre.shape[0]
    nc = T // CHUNK
    GW = SSD_WIDTH // 2

    def body(pre_ref, dtr_ref, z_ref, dtb_ref, alog_ref, de_ref, gn_ref, y_ref, ys_ref, sp_ref, st_ref):
        @pl.when(pl.program_id(0) == 0)
        def _():
            st_ref[...] = jnp.zeros_like(st_ref)

        e = _head_matrix()
        q = _ssd_chunk_common(pre_ref[...], dtr_ref[...], dtb_ref[...], alog_ref[...], e)
        cs, tri, xc, xd = q["cs"], q["tri"], q["xc"], q["xd"]
        cs_t = cs.T
        st = st_ref[...]
        sp_ref[0] = st
        lane = lax.broadcasted_iota(jnp.int32, (1, LANES), 1)
        halves = (lane < HEAD_DIM, lane >= HEAD_DIM)

        g_mat, y_off, s_new = [], [], []
        for g in range(2):
            bg = _group(q["bm"], g, SSD_STATE)
            cg = _group(q["cm"], g, SSD_STATE)
            bgb, cgb = bg.astype(bf16), cg.astype(bf16)
            g_mat.append(_dot_nt(cgb, bgb))
            y_off.append(_dot(cgb, _group(st, g, GW).astype(bf16)))
            s_new.append(_dot(bg.T.astype(bf16), _group(xd, g, GW).astype(bf16)))
        y_off = jnp.concatenate(y_off, axis=1) * q["ecs_e"]
        st_ref[...] = st * q["cd_e"] + jnp.concatenate(s_new, axis=1)

        pairs = []
        for j in range(SSD_HEADS // 2):
            xp = xc[:, j * LANES:(j + 1) * LANES]
            acc = jnp.zeros((CHUNK, LANES), f32)
            for hh in range(2):
                h = 2 * j + hh
                seg = cs[:, h:h + 1] - cs_t[h:h + 1, :]
                lm = jnp.exp(jnp.where(tri, seg, -1e30))
                m = (g_mat[h // 8] * lm).astype(bf16)
                acc = acc + _dot(m, jnp.where(halves[hh], xp, 0.0).astype(bf16))
            pairs.append(acc)
        y = jnp.concatenate(pairs, axis=1) + y_off + q["xs"] * de_ref[...]
        y_ref[...] = y

        zz = z_ref[...]
        v = y * (zz * _sigmoid(zz))
        outs = []
        for g in range(2):
            vg = _group(v, g, GW)
            outs.append(vg * lax.rsqrt(jnp.mean(vg * vg, axis=-1, keepdims=True) + EPS))
        ys_ref[...] = (jnp.concatenate(outs, axis=1) * gn_ref[...]).astype(bf16)

    ch = lambda n: pl.BlockSpec((CHUNK, n), lambda c: (c, 0))
    return _pallas(
        body, name="ssd_fwd", grid=(nc,),
        in_specs=[ch(XBC_WIDTH), ch(LANES), ch(SSD_WIDTH), _full((1, LANES)), _full((1, LANES)), _full((1, SSD_WIDTH)),
                  _full((1, SSD_WIDTH))],
        out_specs=[ch(SSD_WIDTH), ch(SSD_WIDTH), pl.BlockSpec((1, SSD_STATE, SSD_WIDTH), lambda c: (c, 0, 0))],
        out_shape=[jax.ShapeDtypeStruct((T, SSD_WIDTH), f32), jax.ShapeDtypeStruct((T, SSD_WIDTH), bf16),
                   jax.ShapeDtypeStruct((nc, SSD_STATE, SSD_WIDTH), f32)],
        scratch_shapes=[pltpu.VMEM((SSD_STATE, SSD_WIDTH), f32)],
        compiler_params=_cparams(("arbitrary",), VMEM_MID),
    )(pre, dtr, z, dtb, alog, dskip_e, gn)


def _w_out_spec():
    n = 2 * SSD_WIDTH // N_CHIPS
    return pl.BlockSpec((N_CHIPS, n, D_MODEL), lambda *_: (0, OUT_OFF // n, 0))


def _out_proj_fwd(x, ys, yc, gath, g):
    T = x.shape[0]
    tm = min(512, T)
    n = 2 * SSD_WIDTH // N_CHIPS

    def body(x_ref, ys_ref, yc_ref, w_ref, g_ref, h_ref, u_ref):
        h = (x_ref[...] + _dot(ys_ref[:, 0:n], w_ref[0]) + _dot(ys_ref[:, n:], w_ref[1])
             + _dot(yc_ref[:, 0:n], w_ref[2]) + _dot(yc_ref[:, n:], w_ref[3]))
        h_ref[...] = h
        u_ref[...] = _rms(h, g_ref[...]).astype(bf16)

    row = pl.BlockSpec((tm, D_MODEL), lambda i: (i, 0))
    return _pallas(
        body, name="out_proj_fwd", grid=(T // tm,),
        in_specs=[row, row, row, _w_out_spec(), _full((1, D_MODEL))],
        out_specs=[row, row],
        out_shape=[jax.ShapeDtypeStruct((T, D_MODEL), f32), jax.ShapeDtypeStruct((T, D_MODEL), bf16)],
        compiler_params=_cparams(("parallel",), VMEM_MID),
    )(x, ys, yc, gath, g)


def _w_up_spec():
    return pl.BlockSpec((1, D_MODEL, D_MODEL), lambda i, b: (b, UP_OFF // D_MODEL, 0))


def _w_down_spec():
    return pl.BlockSpec((1, D_MODEL, D_MODEL), lambda i, b: (b, DOWN_OFF // D_MODEL, 0))


def _mlp_fwd(h1, u1, gath, g_next):
    T = h1.shape[0]
    tm = min(512, T)
    nb = D_FF // D_MODEL

    def body(h_ref, u_ref, wu_ref, wd_ref, g_ref, r_ref, h2_ref, u2_ref, acc_ref):
        b = pl.program_id(1)

        @pl.when(b == 0)
        def _():
            acc_ref[...] = jnp.zeros_like(acc_ref)

        r = jnp.maximum(_dot(u_ref[...], wu_ref[0]), 0.0)
        r_ref[...] = r.astype(bf16)
        acc_ref[...] += _dot((r * r).astype(bf16), wd_ref[0])

        @pl.when(b == nb - 1)
        def _():
            h2 = h_ref[...] + acc_ref[...]
            h2_ref[...] = h2
            u2_ref[...] = _rms(h2, g_ref[...]).astype(bf16)

    row = pl.BlockSpec((tm, D_MODEL), lambda i, b: (i, 0))
    return _pallas(
        body, name="mlp_fwd", grid=(T // tm, nb),
        in_specs=[row, row, _w_up_spec(), _w_down_spec(), _full((1, D_MODEL))],
        out_specs=[pl.BlockSpec((tm, D_MODEL), lambda i, b: (i, b)), row, row],
        out_shape=[jax.ShapeDtypeStruct((T, D_FF), bf16), jax.ShapeDtypeStruct((T, D_MODEL), f32),
                   jax.ShapeDtypeStruct((T, D_MODEL), bf16)],
        scratch_shapes=[pltpu.VMEM((tm, D_MODEL), f32)],
        compiler_params=_cparams(("parallel", "arbitrary"), VMEM_MID),
    )(h1, u1, gath, gath, g_next)


def _ple_loss(h2, u2, p, tgt, gath, b_pg, w_ple, g_ple, g_fin, g_pg):
    T = h2.shape[0]
    tm = min(256, T)
    npg = D_MODEL // N_CHIPS

    def body(h2_ref, u2_ref, p_ref, t_ref, wpg_ref, bpg_ref, wple_ref, gple_ref, gfin_ref, gpg_ref,
             loss_ref, dh2_ref, dh2b_ref, dgp_ref, dep_ref, dgfin_ref, dgple_ref, dbpg_ref, dgpg_ref):
        @pl.when(pl.program_id(0) == 0)
        def _():
            loss_ref[...] = jnp.zeros_like(loss_ref)
            dgfin_ref[...] = jnp.zeros_like(dgfin_ref)
            dgple_ref[...] = jnp.zeros_like(dgple_ref)
            dbpg_ref[...] = jnp.zeros_like(dbpg_ref)
            dgpg_ref[...] = jnp.zeros_like(dgpg_ref)

        h2 = h2_ref[...]
        gate_pre = bpg_ref[...]
        for b in range(N_CHIPS):
            gate_pre = gate_pre + _dot(u2_ref[:, b * npg:(b + 1) * npg], wpg_ref[b])
        gate = _sigmoid(gate_pre)
        e_pre = _dot(p_ref[...].astype(bf16), wple_ref[...])
        emb = _rms(e_pre, gple_ref[...])
        h3 = h2 + gate * emb
        diff = _rms(h3, gfin_ref[...]) - t_ref[...]
        sq = jnp.sum(jnp.sum(diff * diff, axis=1, keepdims=True), axis=0, keepdims=True)
        loss_ref[...] += (0.5 / D_MODEL) * sq
        dh3, dgfin = _rms_bwd(diff * (1.0 / D_MODEL), h3, gfin_ref[...])
        dgfin_ref[...] += dgfin
        dgp = dh3 * emb * gate * (1.0 - gate)
        dbpg_ref[...] += jnp.sum(dgp, axis=0, keepdims=True)
        dep, dgple = _rms_bwd(dh3 * gate, e_pre, gple_ref[...])
        dgple_ref[...] += dgple
        dgpb = dgp.astype(bf16)
        dgp_ref[...] = dgpb
        dep_ref[...] = dep.astype(bf16)
        du2 = jnp.concatenate([_dot_nt(dgpb, wpg_ref[b]) for b in range(N_CHIPS)], axis=1)
        dx, dgpg = _rms_bwd(du2, h2, gpg_ref[...])
        dgpg_ref[...] += dgpg
        dh2 = dh3 + dx
        dh2_ref[...] = dh2
        dh2b_ref[...] = dh2.astype(bf16)

    row = pl.BlockSpec((tm, D_MODEL), lambda i: (i, 0))
    vec = _full((1, D_MODEL))
    vshape = jax.ShapeDtypeStruct((1, D_MODEL), f32)
    return _pallas(
        body, name="ple_loss", grid=(T // tm,),
        in_specs=[row, row, pl.BlockSpec((tm, PLE_DIM), lambda i: (i, 0)), row,
                  pl.BlockSpec((N_CHIPS, npg, D_MODEL), lambda i: (0, PG_OFF // npg, 0)), vec, _full(w_ple.shape),
                  vec, vec, vec],
        out_specs=[_full((8, LANES)), row, row, row, row, vec, vec, vec, vec],
        out_shape=[jax.ShapeDtypeStruct((8, LANES), f32), jax.ShapeDtypeStruct((T, D_MODEL), f32),
                   jax.ShapeDtypeStruct((T, D_MODEL), bf16), jax.ShapeDtypeStruct((T, D_MODEL), bf16),
                   jax.ShapeDtypeStruct((T, D_MODEL), bf16), vshape, vshape, vshape, vshape],
        compiler_params=_cparams(("arbitrary",), VMEM_MID),
    )(h2, u2, p, tgt, gath, b_pg, w_ple, g_ple, g_fin, g_pg)


def _mlp_bwd(dh2, r, gath, h1, g):
    T = dh2.shape[0]
    tm = min(512, T)
    nb = D_FF // D_MODEL

    def body(dh2_ref, r_ref, wd_ref, wu_ref, h1_ref, g_ref, dhp_ref, dh1_ref, dh1b_ref, dg_ref, acc_ref):
        i, b = pl.program_id(0), pl.program_id(1)

        @pl.when(b == 0)
        def _():
            acc_ref[...] = jnp.zeros_like(acc_ref)

        @pl.when((b == 0) & (i == 0))
        def _():
            dg_ref[...] = jnp.zeros_like(dg_ref)

        dact = _dot_nt(dh2_ref[...].astype(bf16), wd_ref[0])
        dhp = (dact * 2.0 * r_ref[...].astype(f32)).astype(bf16)
        dhp_ref[...] = dhp
        acc_ref[...] += _dot_nt(dhp, wu_ref[0])

        @pl.when(b == nb - 1)
        def _():
            dx, dg = _rms_bwd(acc_ref[...], h1_ref[...], g_ref[...])
            dg_ref[...] += dg
            dh1 = dh2_ref[...] + dx
            dh1_ref[...] = dh1
            dh1b_ref[...] = dh1.astype(bf16)

    row = pl.BlockSpec((tm, D_MODEL), lambda i, b: (i, 0))
    return _pallas(
        body, name="mlp_bwd", grid=(T // tm, nb),
        in_specs=[row, pl.BlockSpec((tm, D_MODEL), lambda i, b: (i, b)), _w_down_spec(), _w_up_spec(), row,
                  _full((1, D_MODEL))],
        out_specs=[pl.BlockSpec((tm, D_MODEL), lambda i, b: (i, b)), row, row, _full((1, D_MODEL))],
        out_shape=[jax.ShapeDtypeStruct((T, D_FF), bf16), jax.ShapeDtypeStruct((T, D_MODEL), f32),
                   jax.ShapeDtypeStruct((T, D_MODEL), bf16), jax.ShapeDtypeStruct((1, D_MODEL), f32)],
        scratch_shapes=[pltpu.VMEM((tm, D_MODEL), f32)],
        compiler_params=_cparams(("arbitrary", "arbitrary"), VMEM_MID),
    )(dh2, r, gath, gath, h1, g)


def _out_proj_bwd(dh1, gath, co, ln_g, ln_b, rider=None):
    T = dh1.shape[0]
    tm = min(512, T)

    def body(dh_ref, w_ref, co_ref, g_ref, b_ref, dys_ref, dco_ref, dg_ref, db_ref):
        @pl.when(pl.program_id(0) == 0)
        def _():
            dg_ref[...] = jnp.zeros_like(dg_ref)
            db_ref[...] = jnp.zeros_like(db_ref)

        dhb = dh_ref[...].astype(bf16)
        dys_ref[...] = jnp.concatenate([_dot_nt(dhb, w_ref[0]), _dot_nt(dhb, w_ref[1])], axis=1)
        dyc = jnp.concatenate([_dot_nt(dhb, w_ref[2]), _dot_nt(dhb, w_ref[3])], axis=1)
        co = co_ref[...]
        mu = jnp.mean(co, axis=-1, keepdims=True)
        xc = co - mu
        rstd = lax.rsqrt(jnp.mean(xc * xc, axis=-1, keepdims=True) + EPS)
        xh = xc * rstd
        yn = xh * g_ref[...] + b_ref[...]
        dyn = dyc * _dsilu(yn)
        dg_ref[...] += jnp.sum(dyn * xh, axis=0, keepdims=True)
        db_ref[...] += jnp.sum(dyn, axis=0, keepdims=True)
        dxh = dyn * g_ref[...]
        dco_ref[...] = rstd * (dxh - jnp.mean(dxh, axis=-1, keepdims=True)
                               - xh * jnp.mean(dxh * xh, axis=-1, keepdims=True))

    row = pl.BlockSpec((tm, D_MODEL), lambda i: (i, 0))
    vec = _full((1, CONF_WIDTH))
    vshape = jax.ShapeDtypeStruct((1, CONF_WIDTH), f32)
    return _call(
        body, (dh1, gath, co, ln_g, ln_b), name="out_proj_bwd", grid=(T // tm,),
        in_specs=[row, _w_out_spec(), row, vec, vec],
        out_specs=[row, row, vec, vec],
        out_shape=[jax.ShapeDtypeStruct((T, SSD_WIDTH), f32), jax.ShapeDtypeStruct((T, CONF_WIDTH), f32), vshape, vshape],
        params=_cparams(("arbitrary",), VMEM_MID), rider=rider)


def _bwd_offsets(K):
    return [K - 1 - k for k in range(K)]


def _next_halo_spec(hb, tm, C, T):
    return pl.BlockSpec((hb, C), lambda i: (jnp.minimum((i + 1) * (tm // hb), T // hb - 1), 0))


DW_RB = 8
DW_UNROLL = 4
DW_ACC_VREGS = 32


def _conv_dw(dw_ref, bufd_ref, bufx_ref, phx_ref, offs_x, tm, C):
    K = len(offs_x)
    group = max(1, DW_ACC_VREGS // (C // LANES))
    for k0 in range(0, K, group):
        ks = list(range(k0, min(k0 + group, K)))

        def step(i, accs, ks=ks):
            for u in range(DW_UNROLL):
                r0 = pl.multiple_of((i * DW_UNROLL + u) * DW_RB, DW_RB)
                d = bufd_ref[pl.ds(r0, DW_RB), :]
                accs = tuple(acc + _window(bufx_ref, phx_ref, offs_x, offs_x[k], r0, DW_RB) * d
                             for k, acc in zip(ks, accs))
            return accs

        accs = lax.fori_loop(0, tm // (DW_RB * DW_UNROLL), step, tuple(jnp.zeros((DW_RB, C), f32) for _ in ks))
        for k, acc in zip(ks, accs):
            dw_ref[k:k + 1, :] += jnp.sum(acc, axis=0, keepdims=True)


def _fill_bwd_buffers(dcur_ref, dnext_ref, xcur_ref, xprev_ref, bufd_ref, bufx_ref, phd_ref, phx_ref, offs_d, offs_x,
                      hb, tm, first, last):
    bufd_ref[0:tm, :] = dcur_ref[...]
    bufd_ref[tm:tm + hb, :] = dnext_ref[...] * jnp.where(last, 0.0, 1.0)
    bufx_ref[0:hb, :] = xprev_ref[...] * jnp.where(first, 0.0, 1.0)
    bufx_ref[hb:hb + tm, :] = xcur_ref[...]
    _make_phases(bufd_ref, phd_ref, offs_d, tm)
    _make_phases(bufx_ref, phx_ref, offs_x, tm)


def _ssd_conv_bwd(dpre, xbc, w):
    T, C = xbc.shape
    K, hb = SSD_CONV, 8
    tm = min(256, T)
    nt = T // tm
    offs_d, offs_x = _bwd_offsets(K), _fwd_offsets(K, hb)

    def body(dcur_ref, dnext_ref, xcur_ref, xprev_ref, w_ref, dx_ref, dw_ref, db_ref, bufd_ref, bufx_ref, phd_ref, phx_ref):
        i = pl.program_id(0)

        @pl.when(i == 0)
        def _():
            dw_ref[...] = jnp.zeros_like(dw_ref)
            db_ref[...] = jnp.zeros_like(db_ref)

        _fill_bwd_buffers(dcur_ref, dnext_ref, xcur_ref, xprev_ref, bufd_ref, bufx_ref, phd_ref, phx_ref, offs_d, offs_x,
                          hb, tm, i == 0, i == nt - 1)

        def chunk(j, carry):
            r0 = pl.multiple_of(j * CONV_RB, CONV_RB)
            dx_ref[pl.ds(r0, CONV_RB), :] = _conv_rows(w_ref, bufd_ref, phd_ref, offs_d, r0, CONV_RB).astype(bf16)
            return carry

        lax.fori_loop(0, tm // CONV_RB, chunk, 0)
        _conv_dw(dw_ref, bufd_ref, bufx_ref, phx_ref, offs_x, tm, C)
        db_ref[...] += jnp.sum(dcur_ref[...], axis=0, keepdims=True)

    row = pl.BlockSpec((tm, C), lambda i: (i, 0))
    return _pallas(
        body, name="ssd_conv_bwd", grid=(nt,),
        in_specs=[row, _next_halo_spec(hb, tm, C, T), row, _prev_halo_spec(hb, tm, C), _full((SUBLANES * K, C))],
        out_specs=[row, _full((8, C)), _full((1, C))],
        out_shape=[jax.ShapeDtypeStruct((T, C), bf16), jax.ShapeDtypeStruct((8, C), f32), jax.ShapeDtypeStruct((1, C), f32)],
        scratch_shapes=[pltpu.VMEM((tm + hb, C), f32), pltpu.VMEM((hb + tm, C), f32),
                        pltpu.VMEM(_phase_shape(offs_d, tm, C), f32),
                        pltpu.VMEM(_phase_shape(offs_x, tm, C), f32)],
        compiler_params=_cparams(("arbitrary",), VMEM_BIG),
    )(dpre, dpre, xbc, xbc, _sublane_rows(w))


def _conf_conv_bwd(dco, v, w, cv, cg, rider=None):
    T, C = v.shape
    K, hb = CONF_KERNEL, 32
    tm = min(256, T)
    nt = T // tm
    offs_d, offs_x = _bwd_offsets(K), _fwd_offsets(K, hb)

    def body(dcur_ref, dnext_ref, vcur_ref, vprev_ref, w_ref, cv_ref, cg_ref, dcv_ref, dcg_ref, dw_ref, db_ref,
             bufd_ref, bufx_ref, phd_ref, phx_ref):
        i = pl.program_id(0)

        @pl.when(i == 0)
        def _():
            dw_ref[...] = jnp.zeros_like(dw_ref)
            db_ref[...] = jnp.zeros_like(db_ref)

        _fill_bwd_buffers(dcur_ref, dnext_ref, vcur_ref, vprev_ref, bufd_ref, bufx_ref, phd_ref, phx_ref, offs_d, offs_x,
                          hb, tm, i == 0, i == nt - 1)

        def chunk(j, carry):
            r0 = pl.multiple_of(j * CONV_RB, CONV_RB)
            rows = pl.ds(r0, CONV_RB)
            dv = _conv_rows(w_ref, bufd_ref, phd_ref, offs_d, r0, CONV_RB)
            s = _sigmoid(cg_ref[rows, :])
            dcv_ref[rows, :] = (dv * s).astype(bf16)
            dcg_ref[rows, :] = (dv * cv_ref[rows, :] * s * (1.0 - s)).astype(bf16)
            return carry

        lax.fori_loop(0, tm // CONV_RB, chunk, 0)
        _conv_dw(dw_ref, bufd_ref, bufx_ref, phx_ref, offs_x, tm, C)
        db_ref[...] += jnp.sum(dcur_ref[...], axis=0, keepdims=True)

    row = pl.BlockSpec((tm, C), lambda i: (i, 0))
    return _call(
        body, (dco, dco, v, v, _sublane_rows(w), cv, cg), name="conf_conv_bwd", grid=(nt,),
        in_specs=[row, _next_halo_spec(hb, tm, C, T), row, _prev_halo_spec(hb, tm, C), _full((SUBLANES * K, C)), row, row],
        out_specs=[row, row, _full((32, C)), _full((1, C))],
        out_shape=[jax.ShapeDtypeStruct((T, C), bf16), jax.ShapeDtypeStruct((T, C), bf16),
                   jax.ShapeDtypeStruct((32, C), f32), jax.ShapeDtypeStruct((1, C), f32)],
        scratch_shapes=[pltpu.VMEM((tm + hb, C), f32), pltpu.VMEM((hb + tm, C), f32),
                        pltpu.VMEM(_phase_shape(offs_d, tm, C), f32),
                        pltpu.VMEM(_phase_shape(offs_x, tm, C), f32)],
        params=_cparams(("arbitrary",), VMEM_BIG), rider=rider)


def _ssd_bwd(dys, y, z, pre, dtr, sprev, dtb, alog, dskip_e, gn):
    T = pre.shape[0]
    nc = T // CHUNK
    GW = SSD_WIDTH // 2

    def body(dys_ref, y_ref, z_ref, pre_ref, dtr_ref, sp_ref, dtb_ref, alog_ref, de_ref, gn_ref,
             dz_ref, dpre_ref, ddtr_ref, dgn_ref, dd_ref, dal_ref, ddtb_ref, ds_ref):
        @pl.when(pl.program_id(0) == 0)
        def _():
            ds_ref[...] = jnp.zeros_like(ds_ref)
            dgn_ref[...] = jnp.zeros_like(dgn_ref)
            dd_ref[...] = jnp.zeros_like(dd_ref)
            dal_ref[...] = jnp.zeros_like(dal_ref)
            ddtb_ref[...] = jnp.zeros_like(ddtb_ref)

        e = _head_matrix()
        pre = pre_ref[...]
        dtr_b = dtr_ref[...] + dtb_ref[...]
        q = _ssd_chunk_common(pre, dtr_ref[...], dtb_ref[...], alog_ref[...], e)
        cs, tri, xc, xd, xs, dt = q["cs"], q["tri"], q["xc"], q["xd"], q["xs"], q["dt"]
        cs_t = cs.T
        st = sp_ref[0]
        dsn = ds_ref[...]
        lane = lax.broadcasted_iota(jnp.int32, (1, LANES), 1)
        halves = (lane < HEAD_DIM, lane >= HEAD_DIM)
        row_i = lax.broadcasted_iota(jnp.int32, (CHUNK, CHUNK), 0)
        col_i = lax.broadcasted_iota(jnp.int32, (CHUNK, CHUNK), 1)
        tri_t = col_i >= row_i

        y = y_ref[...]
        zz = z_ref[...]
        sz = _sigmoid(zz)
        silu_z = zz * sz
        v = y * silu_z
        dout = dys_ref[...]
        gn_v = gn_ref[...]
        dv, vh = [], []
        for g in range(2):
            vg = _group(v, g, GW)
            rstd = lax.rsqrt(jnp.mean(vg * vg, axis=-1, keepdims=True) + EPS)
            vhg = vg * rstd
            dvh = _group(dout, g, GW) * _group(gn_v, g, GW)
            dv.append(rstd * (dvh - vhg * jnp.mean(dvh * vhg, axis=-1, keepdims=True)))
            vh.append(vhg)
        dv = jnp.concatenate(dv, axis=1)
        dgn_ref[...] += jnp.sum(dout * jnp.concatenate(vh, axis=1), axis=0, keepdims=True)
        dy = dv * silu_z
        dz_ref[...] = (dv * y * (sz * (1.0 + zz * (1.0 - sz)))).astype(bf16)

        dd_row = jnp.sum(dy * xs, axis=0, keepdims=True)
        dd_ref[...] += _contract(jnp.broadcast_to(dd_row, (8, SSD_WIDTH)), e)[0:1, :]
        dxs = dy * de_ref[...]

        dz_in = dy * q["ecs_e"]
        g_mat, gt_mat, dcm, dbm, dsp, dxd, y_off = [], [], [], [], [], [], []
        bgs, cgs = [], []
        for g in range(2):
            bg = _group(q["bm"], g, SSD_STATE)
            cg = _group(q["cm"], g, SSD_STATE)
            bgb, cgb = bg.astype(bf16), cg.astype(bf16)
            bgs.append(bgb)
            cgs.append(cgb)
            stg = _group(st, g, GW).astype(bf16)
            dsng = _group(dsn, g, GW).astype(bf16)
            dzg = _group(dz_in, g, GW).astype(bf16)
            g_mat.append(_dot_nt(cgb, bgb))
            gt_mat.append(_dot_nt(bgb, cgb))
            y_off.append(_dot(cgb, stg))
            dcm.append(_dot_nt(dzg, stg))
            dsp.append(_dot(cg.T.astype(bf16), dzg))
            dbm.append(_dot_nt(_group(xd, g, GW).astype(bf16), dsng))
            dxd.append(_dot(bgb, dsng))
        y_off = jnp.concatenate(y_off, axis=1) * q["ecs_e"]
        dxd = jnp.concatenate(dxd, axis=1)
        ds_ref[...] = dsn * q["cd_e"] + jnp.concatenate(dsp, axis=1)
        dcd_row = jnp.sum(dsn * st, axis=0, keepdims=True) * q["cd_e"]
        t_e = dxd * xd
        dcs = _contract(dy * y_off - t_e, e)
        last_row = _contract(jnp.broadcast_to(dcd_row + jnp.sum(t_e, axis=0, keepdims=True), (8, SSD_WIDTH)), e)[0:1, :]
        dxc_state = dxd * q["dte_e"]

        dg_acc = [jnp.zeros((CHUNK, CHUNK), f32), jnp.zeros((CHUNK, CHUNK), f32)]
        dgt_acc = [jnp.zeros((CHUNK, CHUNK), f32), jnp.zeros((CHUNK, CHUNK), f32)]
        dxc_pairs = []
        for j in range(SSD_HEADS // 2):
            dyp_f = dy[:, j * LANES:(j + 1) * LANES]
            xcp_f = xc[:, j * LANES:(j + 1) * LANES]
            acc = jnp.zeros((CHUNK, LANES), f32)
            for hh in range(2):
                h = 2 * j + hh
                g = h // 8
                dyp = jnp.where(halves[hh], dyp_f, 0.0).astype(bf16)
                xcp = jnp.where(halves[hh], xcp_f, 0.0).astype(bf16)
                lm = jnp.exp(jnp.where(tri, cs[:, h:h + 1] - cs_t[h:h + 1, :], -1e30))
                lm_t = jnp.exp(jnp.where(tri_t, cs_t[h:h + 1, :] - cs[:, h:h + 1], -1e30))
                dm = _dot_nt(dyp, xcp) * lm
                dm_t = _dot_nt(xcp, dyp) * lm_t
                acc = acc + _dot((gt_mat[g] * lm_t).astype(bf16), dyp)
                dg_acc[g] = dg_acc[g] + dm
                dgt_acc[g] = dgt_acc[g] + dm_t
                qd = jnp.sum(dm * g_mat[g] - dm_t * gt_mat[g], axis=1, keepdims=True)
                dcs = dcs + qd * (lane == h).astype(f32)
            dxc_pairs.append(acc)
        dxc = jnp.concatenate(dxc_pairs, axis=1) + dxc_state
        for g in range(2):
            dcm[g] = dcm[g] + _dot(dg_acc[g].astype(bf16), bgs[g])
            dbm[g] = dbm[g] + _dot(dgt_acc[g].astype(bf16), cgs[g])

        dxs = dxs + dxc * q["dt_e"]
        ddt = _contract(dxc * xs, e)
        dcs = dcs + jnp.where(row_i == CHUNK - 1, jnp.broadcast_to(last_row, (CHUNK, LANES)), 0.0)
        da = jnp.dot(tri_t.astype(f32), dcs, precision=lax.Precision.HIGHEST, preferred_element_type=f32)
        ddt = ddt + da * q["a_neg"]
        dal_ref[...] += jnp.sum(da * dt, axis=0, keepdims=True) * q["a_neg"]
        ddtr = ddt * _sigmoid(dtr_b) * (lane < SSD_HEADS).astype(f32)
        ddtb_ref[...] += jnp.sum(ddtr, axis=0, keepdims=True)
        ddtr_ref[...] = ddtr.astype(bf16)

        dact = jnp.concatenate([dxs, dbm[0], dbm[1], dcm[0], dcm[1]], axis=1)
        dpre_ref[...] = dact * _dsilu(pre)

    rev = lambda n: pl.BlockSpec((CHUNK, n), lambda c: (nc - 1 - c, 0))
    vec = _full((1, LANES))
    vshape = jax.ShapeDtypeStruct((1, LANES), f32)
    return _pallas(
        body, name="ssd_bwd", grid=(nc,),
        in_specs=[rev(SSD_WIDTH), rev(SSD_WIDTH), rev(SSD_WIDTH), rev(XBC_WIDTH), rev(LANES),
                  pl.BlockSpec((1, SSD_STATE, SSD_WIDTH), lambda c: (nc - 1 - c, 0, 0)),
                  vec, vec, _full((1, SSD_WIDTH)), _full((1, SSD_WIDTH))],
        out_specs=[rev(SSD_WIDTH), rev(XBC_WIDTH), rev(LANES), _full((1, SSD_WIDTH)), vec, vec, vec],
        out_shape=[jax.ShapeDtypeStruct((T, SSD_WIDTH), bf16), jax.ShapeDtypeStruct((T, XBC_WIDTH), f32),
                   jax.ShapeDtypeStruct((T, LANES), bf16), jax.ShapeDtypeStruct((1, SSD_WIDTH), f32),
                   vshape, vshape, vshape],
        scratch_shapes=[pltpu.VMEM((SSD_STATE, SSD_WIDTH), f32)],
        compiler_params=_cparams(("arbitrary",), VMEM_MID),
    )(dys, y, z, pre, dtr, sprev, dtb, alog, dskip_e, gn)


def _in_proj_bwd(dz, dxbc, dcv, dcg, ddt, gin, x, dh1, g, rider=None):
    T = x.shape[0]
    tm = min(256, T)

    def body(dz_ref, dx_ref, dcv_ref, dcg_ref, ddt_ref, gin_ref, x_ref, dh_ref, g_ref, gx_ref, dg_ref, wt_ref):
        @pl.when(pl.program_id(0) == 0)
        def _():
            dg_ref[...] = jnp.zeros_like(dg_ref)
            _assemble_w_in_t(gin_ref, wt_ref)

        du = (_dot(dz_ref[...], wt_ref[0:O_XBC, :]) + _dot(dx_ref[...], wt_ref[O_XBC:O_DT, :])
              + _dot(dcv_ref[...], wt_ref[O_CV:O_CG, :]) + _dot(dcg_ref[...], wt_ref[O_CG:IN_WIDTH, :])
              + _dot(ddt_ref[...], wt_ref[O_DT:O_DT + LANES, :]))
        dx, dg = _rms_bwd(du, x_ref[...], g_ref[...])
        dg_ref[...] += dg
        gx_ref[...] = dh_ref[...] + dx

    row = lambda n: pl.BlockSpec((tm, n), lambda i: (i, 0))
    return _call(
        body, (dz, dxbc, dcv, dcg, ddt, gin, x, dh1, g), name="in_proj_bwd", grid=(T // tm,),
        in_specs=[row(SSD_WIDTH), row(XBC_WIDTH), row(CONF_WIDTH), row(CONF_WIDTH), row(LANES), _full(gin.shape),
                  row(D_MODEL), row(D_MODEL), _full((1, D_MODEL))],
        out_specs=[row(D_MODEL), _full((1, D_MODEL))],
        out_shape=[jax.ShapeDtypeStruct((T, D_MODEL), f32), jax.ShapeDtypeStruct((1, D_MODEL), f32)],
        scratch_shapes=[pltpu.VMEM((IN_WIDTH, D_MODEL), bf16)],
        params=_cparams(("arbitrary",), VMEM_BIG), rider=rider)


def _weight_grad(a, g, name, square=False, slab=None, place=None, tk=512):
    T, K = a.shape
    N = g.shape[1]
    tk = min(tk, K)
    tn = 1024 if N % 1024 == 0 else min(512, N)
    tt = min(2048, T)

    def body(a_ref, g_ref, *rest):
        o_ref = rest[-1]
        acc = _dot_tn(_operand(a_ref[...]), g_ref[...].astype(bf16))
        t = pl.program_id(2)
        shaped = acc if slab is None else acc[None]

        @pl.when(t == 0)
        def _():
            o_ref[...] = shaped

        @pl.when(t > 0)
        def _():
            o_ref[...] += shaped

    def _operand(av):
        if square:
            av = av.astype(f32)
            av = av * av
        return av.astype(bf16)

    in_specs = [pl.BlockSpec((tt, tk), lambda i, j, t: (t, i)), pl.BlockSpec((tt, tn), lambda i, j, t: (t, j))]
    grid = (K // tk, N // tn, T // tt)
    params = _cparams(("parallel", "parallel", "arbitrary"), VMEM_MID)
    if slab is None:
        return _pallas(
            body, pin_bytes=PIN_SMALL, name=name, grid=grid, in_specs=in_specs,
            out_specs=pl.BlockSpec((tk, tn), lambda i, j, t: (i, j)),
            out_shape=jax.ShapeDtypeStruct((K, N), f32), compiler_params=params,
        )(a, g)
    return _pallas(
        body, pin_bytes=PIN_SMALL, name=name, grid=grid, in_specs=in_specs + [ANY],
        out_specs=pl.BlockSpec((1, tk, tn), lambda i, j, t: place(i, j)),
        out_shape=jax.ShapeDtypeStruct(slab.shape, f32), input_output_aliases={2: 0}, compiler_params=params,
    )(a, g, slab)


def _place():
    return lax.axis_index("x"), lax.axis_index("y"), lax.axis_index("c")


def _other_chips(x, y):
    return [(1 - x, y), (x, 1 - y), (1 - x, 1 - y)]


def _remote(src, dst, ssem, rsem, dev):
    return pltpu.make_async_remote_copy(src_ref=src, dst_ref=dst, send_sem=ssem, recv_sem=rsem, device_id=dev,
                                        device_id_type=MESH)


def _gather_weights(arrays, convw):
    n = len(arrays)
    halves = tuple(a.shape[1] // 2 for a in arrays)

    def body(*refs):
        cw_ref, cwo_ref = refs[n], refs[2 * n + 1]
        ssem, rsem, lsem = refs[2 * n + 2:]
        triples = tuple(zip(refs[:n], refs[n + 1:2 * n + 1], halves))
        x, y, c = _place()
        me_b = 2 * x + y
        sib = (x, y, 1 - c)
        chips = _other_chips(x, y)
        loc = pltpu.make_async_copy(cw_ref, cwo_ref.at[me_b], lsem)
        loc.start()
        sends = []
        for j, (src, dst, h) in enumerate(triples):
            mine = pl.ds(c * h, h)
            for k, (px, py) in enumerate(chips):
                s = 6 * j + k
                sends.append(_remote(src.at[me_b, mine], dst.at[me_b, mine], ssem.at[s], rsem.at[s], (px, py, c)))
        for k, (px, py) in enumerate(chips):
            sends.append(_remote(cw_ref, cwo_ref.at[me_b], ssem.at[6 * n + k], rsem.at[6 * n + k], (px, py, c)))
        for cp in sends:
            cp.start()
        for j, (src, dst, h) in enumerate(triples):
            mine = pl.ds(c * h, h)
            for k, (px, py) in enumerate(chips):
                b = 2 * px + py
                s = 6 * j + k
                _remote(src.at[b, mine], dst.at[b, mine], ssem.at[s], rsem.at[s], (px, py, c)).wait_recv()
                fw = _remote(dst.at[b, mine], dst.at[b, mine], ssem.at[s + 3], rsem.at[s + 3], sib)
                fw.start()
                sends.append(fw)
        for k, (px, py) in enumerate(chips):
            b = 2 * px + py
            _remote(cw_ref, cwo_ref.at[b], ssem.at[6 * n + k], rsem.at[6 * n + k], (px, py, c)).wait_recv()
        for j, (src, dst, h) in enumerate(triples):
            theirs = pl.ds((1 - c) * h, h)
            for k, (px, py) in enumerate(chips):
                b = 2 * px + py
                s = 6 * j + k + 3
                _remote(src.at[b, theirs], dst.at[b, theirs], ssem.at[s], rsem.at[s], sib).wait_recv()
        for cp in sends:
            cp.wait_send()
        loc.wait()

    return _pallas(
        body, name="gather_weights", in_specs=[ANY] * (n + 1), out_specs=[ANY] * (n + 1),
        out_shape=[jax.ShapeDtypeStruct(a.shape, bf16) for a in arrays]
        + [jax.ShapeDtypeStruct((N_CHIPS, CONVW_ROWS, D_MODEL), f32)],
        input_output_aliases={j: j for j in range(n)},
        scratch_shapes=[pltpu.SemaphoreType.DMA((6 * n + 3,)), pltpu.SemaphoreType.DMA((6 * n + 3,)),
                        pltpu.SemaphoreType.DMA(())],
    )(*arrays, convw)


def _gather_rider(gath0, lo, n):
    h = gath0.shape[1] // 2

    def copies(rins, routs, ssem, rsem, sending):
        (g_ref,), (o_ref,) = rins, routs
        x, y, c = _place()
        mine = pl.ds(c * h + lo, n)
        for k, (px, py) in enumerate(_other_chips(x, y)):
            b = 2 * x + y if sending else 2 * px + py
            yield _remote(g_ref.at[b, mine], o_ref.at[b, mine], ssem.at[k], rsem.at[k], (px, py, c))

    def start(*refs):
        for cp in copies(*refs, sending=True):
            cp.start()

    def finish(*refs):
        for cp in copies(*refs, sending=False):
            cp.wait()

    return _Rider([gath0], [jax.ShapeDtypeStruct(gath0.shape, gath0.dtype)], {0: 0}, 3, start, finish)


def _forward_to_sibling(gath):
    h = gath.shape[1] // 2

    def body(g_ref, o_ref, ssem, rsem):
        x, y, c = _place()
        sib = (x, y, 1 - c)
        mine, theirs = pl.ds(c * h, h), pl.ds((1 - c) * h, h)
        blocks = [2 * px + py for px, py in _other_chips(x, y)]
        sends = [_remote(g_ref.at[b, mine], o_ref.at[b, mine], ssem.at[k], rsem.at[k], sib) for k, b in enumerate(blocks)]
        for cp in sends:
            cp.start()
        for k, b in enumerate(blocks):
            _remote(g_ref.at[b, theirs], o_ref.at[b, theirs], ssem.at[k], rsem.at[k], sib).wait_recv()
        for cp in sends:
            cp.wait_send()

    return _pallas(
        body, name="forward_to_sibling", in_specs=[ANY], out_specs=ANY,
        out_shape=jax.ShapeDtypeStruct(gath.shape, gath.dtype), input_output_aliases={0: 0},
        scratch_shapes=[pltpu.SemaphoreType.DMA((3,)), pltpu.SemaphoreType.DMA((3,))],
    )(gath)


def _swap_copy(g_ref, r_ref, ssem, rsem):
    x, y, c = _place()
    h = r_ref.shape[1]
    return _remote(g_ref.at[:, pl.ds((1 - c) * h, h), :], r_ref, ssem.at[0], rsem.at[0], (x, y, 1 - c))


def _swap_rider(g):
    def start(rins, routs, ssem, rsem):
        _swap_copy(rins[0], routs[0], ssem, rsem).start()

    def finish(rins, routs, ssem, rsem):
        _swap_copy(rins[0], routs[0], ssem, rsem).wait()

    return _Rider([g], [jax.ShapeDtypeStruct((N_CHIPS, g.shape[1] // 2, g.shape[2]), g.dtype)], {}, 1, start, finish)


def _swap_halves(g):
    def body(g_ref, r_ref, ssem, rsem):
        cp = _swap_copy(g_ref, r_ref, ssem, rsem)
        cp.start()
        cp.wait()

    return _pallas(
        body, name="swap_halves", in_specs=[ANY], out_specs=ANY,
        out_shape=jax.ShapeDtypeStruct((N_CHIPS, g.shape[1] // 2, g.shape[2]), g.dtype),
        scratch_shapes=[pltpu.SemaphoreType.DMA((1,)), pltpu.SemaphoreType.DMA((1,))],
    )(g)


def _chip_sum(cidx, gslab, recv, name):
    half, C = recv.shape[1:]
    tr = half // 2 if (half // 2) % 16 == 0 else half

    def body(c_ref, g_ref, r_ref, o_ref):
        o_ref[...] = (g_ref[...] + r_ref[...]).astype(bf16)

    return _pallas(
        body, name=name,
        grid_spec=pltpu.PrefetchScalarGridSpec(
            num_scalar_prefetch=1, grid=(N_CHIPS, half // tr),
            in_specs=[pl.BlockSpec((1, tr, C), lambda b, i, c_ref: (b, c_ref[0] * (half // tr) + i, 0)),
                      pl.BlockSpec((1, tr, C), lambda b, i, c_ref: (b, i, 0))],
            out_specs=pl.BlockSpec((1, tr, C), lambda b, i, c_ref: (b, i, 0))),
        out_shape=jax.ShapeDtypeStruct((N_CHIPS, half, C), bf16),
        compiler_params=_cparams(("parallel", "parallel"), VMEM_MID),
    )(cidx, gslab, recv)


def _exchange_rider(h):
    def copies(rins, routs, ssem, rsem):
        x, y, c = _place()
        for k, (px, py) in enumerate(_other_chips(x, y)):
            yield _remote(rins[0].at[2 * px + py], routs[0].at[k], ssem.at[k], rsem.at[k], (px, py, c))

    def start(*refs):
        for cp in copies(*refs):
            cp.start()

    def finish(*refs):
        for cp in copies(*refs):
            cp.wait()

    return _Rider([h], [jax.ShapeDtypeStruct((3,) + h.shape[1:], h.dtype)], {}, 3, start, finish)


def _gather_small(small):
    def body(sm_ref, all_ref, ssem, rsem, lsem):
        x, y, c = _place()
        me = 4 * x + 2 * y + c
        loc = pltpu.make_async_copy(sm_ref, all_ref.at[me], lsem)
        loc.start()
        sends, peers = [], []
        for r in range(1, N_DEV):
            peer = ((1 - x) if r & 4 else x, (1 - y) if r & 2 else y, (1 - c) if r & 1 else c)
            peers.append(peer)
            sends.append(_remote(sm_ref, all_ref.at[me], ssem.at[r - 1], rsem.at[r - 1], peer))
        for cp in sends:
            cp.start()
        for r, peer in zip(range(1, N_DEV), peers):
            pid = 4 * peer[0] + 2 * peer[1] + peer[2]
            _remote(sm_ref, all_ref.at[pid], ssem.at[r - 1], rsem.at[r - 1], peer).wait_recv()
        for cp in sends:
            cp.wait_send()
        loc.wait()

    return _pallas(
        body, name="gather_small", in_specs=[ANY], out_specs=ANY,
        out_shape=jax.ShapeDtypeStruct((N_DEV, SMALL_ROWS, D_MODEL), f32),
        scratch_shapes=[pltpu.SemaphoreType.DMA((7,)), pltpu.SemaphoreType.DMA((7,)), pltpu.SemaphoreType.DMA(())],
    )(small)


def _final_sum(idx, gslab, recv_sib, recv_ici, name):
    half, C = recv_sib.shape[1:]
    tr = half // 2 if (half // 2) % 16 == 0 else half

    def body(i_ref, g_ref, r_ref, p_ref, o_ref):
        acc = g_ref[0] + r_ref[0]
        for k in range(3):
            acc = acc + p_ref[k].astype(f32)
        o_ref[...] = acc

    return _pallas(
        body, name=name,
        grid_spec=pltpu.PrefetchScalarGridSpec(
            num_scalar_prefetch=1, grid=(half // tr,),
            in_specs=[pl.BlockSpec((1, tr, C), lambda i, s: (s[1], s[0] * (half // tr) + i, 0)),
                      pl.BlockSpec((1, tr, C), lambda i, s: (s[1], i, 0)),
                      pl.BlockSpec((3, tr, C), lambda i, s: (0, i, 0))],
            out_specs=pl.BlockSpec((tr, C), lambda i, s: (s[0] * (half // tr) + i, 0))),
        out_shape=jax.ShapeDtypeStruct((2 * half, C), f32),
        compiler_params=_cparams(("parallel",), VMEM_MID),
    )(idx, gslab, recv_sib, recv_ici)


def _join_halves(ra, rb):
    ha, hb = ra.shape[0] // 2, rb.shape[0] // 2

    def body(a_ref, b_ref, ao_ref, bo_ref, ssem, rsem):
        x, y, c = _place()
        sib = (x, y, 1 - c)
        mine_a, theirs_a = pl.ds(c * ha, ha), pl.ds((1 - c) * ha, ha)
        mine_b, theirs_b = pl.ds(c * hb, hb), pl.ds((1 - c) * hb, hb)
        ca = _remote(a_ref.at[mine_a], ao_ref.at[mine_a], ssem.at[0], rsem.at[0], sib)
        cb = _remote(b_ref.at[mine_b], bo_ref.at[mine_b], ssem.at[1], rsem.at[1], sib)
        ca.start()
        cb.start()
        _remote(a_ref.at[theirs_a], ao_ref.at[theirs_a], ssem.at[0], rsem.at[0], sib).wait_recv()
        _remote(b_ref.at[theirs_b], bo_ref.at[theirs_b], ssem.at[1], rsem.at[1], sib).wait_recv()
        ca.wait_send()
        cb.wait_send()

    return _pallas(
        body, name="join_halves", in_specs=[ANY, ANY], out_specs=[ANY, ANY],
        out_shape=[jax.ShapeDtypeStruct(ra.shape, f32), jax.ShapeDtypeStruct(rb.shape, f32)],
        input_output_aliases={0: 0, 1: 1},
        scratch_shapes=[pltpu.SemaphoreType.DMA((2,)), pltpu.SemaphoreType.DMA((2,))],
    )(ra, rb)


def _shard_rows(gt):
    def body(g_ref, o_ref):
        for b in range(N_CHIPS):
            o_ref[b, 0:W_IN_ROWS, :] = g_ref[b * W_IN_ROWS:(b + 1) * W_IN_ROWS, :]
            o_ref[b, W_IN_ROWS:W_IN_ROWS_PAD, :] = jnp.zeros((W_IN_ROWS_PAD - W_IN_ROWS, LANES), f32)

    return _pallas(
        body, name="shard_rows", grid=(D_MODEL // LANES,),
        in_specs=[pl.BlockSpec((IN_WIDTH, LANES), lambda i: (0, i))],
        out_specs=pl.BlockSpec((N_CHIPS, W_IN_ROWS_PAD, LANES), lambda i: (0, 0, i)),
        out_shape=jax.ShapeDtypeStruct((N_CHIPS, W_IN_ROWS_PAD, D_MODEL), f32),
        compiler_params=_cparams(("parallel",), VMEM_MID),
    )(gt)


def _sum_small(all_small):
    def body(a_ref, o_ref):
        acc = a_ref[0]
        for d in range(1, N_DEV):
            acc = acc + a_ref[d]
        o_ref[...] = acc

    return _pallas(
        body, name="sum_small", out_shape=jax.ShapeDtypeStruct((SMALL_ROWS, D_MODEL), f32),
    )(all_small)


def _adamw(w, g, m, v, name, g_off=0, by_columns=False):
    R, C = w.shape
    tr = 256 if R % 256 == 0 else R
    assert g_off % tr == 0 and not (by_columns and g_off)
    c1 = 1.0 - ADAM_B1 ** ADAM_STEP
    c2 = 1.0 - ADAM_B2 ** ADAM_STEP

    def body(w_ref, g_ref, m_ref, v_ref, d_ref, mo_ref, vo_ref):
        gg = g_ref[...]
        m2 = ADAM_B1 * m_ref[...] + (1.0 - ADAM_B1) * gg
        v2 = ADAM_B2 * v_ref[...] + (1.0 - ADAM_B2) * (gg * gg)
        mo_ref[...] = m2
        vo_ref[...] = v2
        d_ref[...] = -ADAM_LR * ((m2 / c1) / (jnp.sqrt(v2 / c2) + ADAM_EPS) + ADAM_WD * w_ref[...])

    if by_columns:
        blk = gblk = pl.BlockSpec((R, LANES), lambda i: (0, i))
        grid = (C // LANES,)
    else:
        blk = pl.BlockSpec((tr, C), lambda i: (i, 0))
        gblk = pl.BlockSpec((tr, C), lambda i: (g_off // tr + i, 0))
        grid = (R // tr,)
    shp = jax.ShapeDtypeStruct((R, C), f32)
    return _pallas(
        body, pin_bytes=PIN_SMALL, name=name, grid=grid, in_specs=[blk, gblk, blk, blk], out_specs=[blk] * 3,
        out_shape=[shp] * 3,
        compiler_params=_cparams(("parallel",), VMEM_MID),
    )(w, g, m, v)


def _pad_lanes(v):
    return jnp.pad(v, ((0, 0), (0, LANES - v.shape[1])))


def _local_step(x, p, tgt, gath0, cidx, gin, S):
    dtb = _pad_lanes(S["dt_bias"])
    alog = _pad_lanes(S["A_log"])
    dskip_e = jnp.repeat(S["D_skip"], HEAD_DIM, axis=1)

    early = GATHER_EARLY_ROWS
    u0, z, xbc, cv, cg, dtr, v, gath1 = _in_proj_fwd(x, S["mix_norm_g"], gin, rider=_gather_rider(gath0, 0, early))
    co, yc, gath = _conf_fwd(v, S["conf_dw_w"], S["conf_dw_b"], S["conf_ln_g"], S["conf_ln_b"],
                             rider=_gather_rider(gath1, early, SLAB_A // 2 - early))
    gath = _forward_to_sibling(gath)
    w_ple = jnp.concatenate([_ple_of_slab(gath[b]) for b in range(N_CHIPS)], axis=1)
    pre = _ssd_conv_fwd(xbc, S["ssd_conv_w"], S["ssd_conv_b"])
    y, ys, sprev = _ssd_fwd(pre, dtr, z, dtb, alog, dskip_e, S["ssd_norm_g"])
    h1, u1 = _out_proj_fwd(x, ys, yc, gath, S["mlp_norm_g"])
    r, h2, u2 = _mlp_fwd(h1, u1, gath, S["ple_gate_norm_g"])
    loss, dh2, dh2b, dgp, dep, dg_fin, dg_ple, db_pg, dg_pg = _ple_loss(
        h2, u2, p, tgt, gath, S["b_ple_gate"], w_ple, S["ple_norm_g"], S["final_norm_g"], S["ple_gate_norm_g"])

    npg = D_MODEL // N_CHIPS
    ga = lax.empty((N_CHIPS, SLAB_A, D_MODEL), f32)
    ga = _weight_grad(u2, dgp, "dw_ple_gate", slab=ga, tk=npg, place=lambda i, j: (i, PG_OFF // npg, j))
    ga = _weight_grad(r, dh2b, "dw_down", square=True, slab=ga, place=lambda i, j: (i // 2, DOWN_OFF // 512 + i % 2, j))
    gw_ple = _weight_grad(p, dep, "dw_ple")
    dhp, dh1, dh1b, dg_mlp = _mlp_bwd(dh2, r, gath, h1, S["mlp_norm_g"])
    ga = _weight_grad(u1, dhp, "dw_up", slab=ga, place=lambda i, j: (j, UP_OFF // 512 + i, 0))
    ga = _weight_grad(ys, dh1b, "dw_out_ssd", slab=ga, place=lambda i, j: (i, OUT_OFF // 512, j))
    ga = _weight_grad(yc, dh1b, "dw_out_conf", slab=ga, place=lambda i, j: (2 + i, OUT_OFF // 512, j))
    n_ple = D_MODEL // N_CHIPS
    ple_rows = jnp.stack([_rows(gw_ple[:, b * n_ple:(b + 1) * n_ple]) for b in range(N_CHIPS)], axis=0)
    ga = lax.dynamic_update_slice(ga, ple_rows, (0, PLE_OFF, 0))
    dys, dco, dg_ln, db_ln, recv_a = _out_proj_bwd(dh1, gath, co, S["conf_ln_g"], S["conf_ln_b"], rider=_swap_rider(ga))
    ha = _chip_sum(cidx, ga, recv_a, "chip_sum_a")
    dcv, dcg, dw_conf, db_conf, ici_a = _conf_conv_bwd(dco, v, S["conf_dw_w"], cv, cg, rider=_exchange_rider(ha))
    dz, dpre, ddtr, dg_ssdn, dd, dal, ddtb = _ssd_bwd(dys, y, z, pre, dtr, sprev, dtb, alog, dskip_e, S["ssd_norm_g"])
    dxbc, dw_sconv, db_sconv = _ssd_conv_bwd(dpre, xbc, S["ssd_conv_w"])
    gw_in = jnp.concatenate([
        _weight_grad(dz, u0, "dw_in_z"), _weight_grad(dxbc, u0, "dw_in_xbc"),
        _weight_grad(ddtr, u0, "dw_in_dt")[:SSD_HEADS],
        _weight_grad(dcv, u0, "dw_in_cv"), _weight_grad(dcg, u0, "dw_in_cg")], axis=0)
    gb = _shard_rows(gw_in)
    recv_b = _swap_halves(gb)
    hb = _chip_sum(cidx, gb, recv_b, "chip_sum_b")
    gx, dg_mix, ici_b = _in_proj_bwd(dz, dxbc, dcv, dcg, ddtr, gin, x, dh1, S["mix_norm_g"], rider=_exchange_rider(hb))
    small = {
        "mix_norm_g": dg_mix, "ssd_conv_w": dw_sconv, "ssd_conv_b": db_sconv, "dt_bias": ddtb, "A_log": dal, "D_skip": dd,
        "ssd_norm_g": dg_ssdn, "conf_dw_w": dw_conf, "conf_dw_b": db_conf, "conf_ln_g": dg_ln, "conf_ln_b": db_ln,
        "mlp_norm_g": dg_mlp, "ple_gate_norm_g": dg_pg, "b_ple_gate": db_pg, "ple_norm_g": dg_ple,
        "final_norm_g": dg_fin, "loss": loss,
    }
    return gx, (ga, recv_a, ici_a), (gb, recv_b, ici_b), small


def _rows(a):
    return a.reshape(-1, D_MODEL)


def _pad_rows(a, n):
    flat = a.reshape(-1)
    return jnp.pad(flat, (0, n * D_MODEL - flat.shape[0])).reshape(n, D_MODEL)


def _ple_of_slab(slab):
    return slab[PLE_OFF:PLE_OFF + PLE_ROWS].reshape(PLE_DIM, D_MODEL // N_CHIPS)


ROW_VEC = {"mix_norm_g": 0, "ssd_norm_g": 1, "conf_dw_b": 2, "conf_ln_g": 3, "conf_ln_b": 4, "mlp_norm_g": 5,
           "ple_gate_norm_g": 6, "b_ple_gate": 7, "ple_norm_g": 8, "final_norm_g": 9}
ROW_CONV_B = 10
ROW_HEADS = 12
ROW_CONV_W = 16
ROW_DW = 24
HEAD_LANES = {"dt_bias": 0, "A_log": 1, "D_skip": 2, "loss": 3}
SMALL_ORDER = ("mix_norm_g", "ssd_conv_w", "ssd_conv_b", "dt_bias", "A_log", "D_skip", "ssd_norm_g", "conf_dw_w",
               "conf_dw_b", "conf_ln_g", "conf_ln_b", "mlp_norm_g", "ple_gate_norm_g", "b_ple_gate", "ple_norm_g",
               "final_norm_g")
SPLIT = XBC_WIDTH - D_MODEL


def _pack_small(raw):
    names = list(ROW_VEC) + ["ssd_conv_b", "dt_bias", "A_log", "D_skip", "loss", "ssd_conv_w", "conf_dw_w"]

    def body(*refs):
        r = dict(zip(names, refs[:-1]))
        o_ref = refs[-1]
        o_ref[...] = jnp.zeros_like(o_ref)
        for n, row in ROW_VEC.items():
            o_ref[row:row + 1, :] = r[n][...]
        o_ref[ROW_CONV_B:ROW_CONV_B + 1, :] = r["ssd_conv_b"][:, 0:D_MODEL]
        o_ref[ROW_CONV_B + 1:ROW_CONV_B + 2, 0:SPLIT] = r["ssd_conv_b"][:, D_MODEL:]
        for n, j in HEAD_LANES.items():
            o_ref[ROW_HEADS:ROW_HEADS + 1, j * LANES:(j + 1) * LANES] = r[n][0:1, :]
        for k in range(SSD_CONV):
            o_ref[ROW_CONV_W + 2 * k:ROW_CONV_W + 2 * k + 1, :] = r["ssd_conv_w"][k:k + 1, 0:D_MODEL]
            o_ref[ROW_CONV_W + 2 * k + 1:ROW_CONV_W + 2 * k + 2, 0:SPLIT] = r["ssd_conv_w"][k:k + 1, D_MODEL:]
        o_ref[ROW_DW:ROW_DW + 32, :] = r["conf_dw_w"][...]

    return _pallas(
        body, name="pack_small", out_shape=jax.ShapeDtypeStruct((SMALL_ROWS, D_MODEL), f32),
    )(*[raw[n] for n in names])


def _adamw_small(cidx, tot, w, m, v):
    c1 = 1.0 - ADAM_B1 ** ADAM_STEP
    c2 = 1.0 - ADAM_B2 ** ADAM_STEP
    n_par = len(SMALL_ORDER)

    def shard(full, chip, width):
        out = full[:, 0:width]
        for b in range(1, N_CHIPS):
            out = jnp.where(chip == b, full[:, b * width:(b + 1) * width], out)
        return out

    def grad_of(n, t_ref, chip):
        if n in ROW_VEC:
            return t_ref[ROW_VEC[n]:ROW_VEC[n] + 1, :]
        if n == "ssd_conv_b":
            return jnp.concatenate([t_ref[ROW_CONV_B:ROW_CONV_B + 1, :], t_ref[ROW_CONV_B + 1:ROW_CONV_B + 2, 0:SPLIT]], axis=1)
        if n in HEAD_LANES:
            j = HEAD_LANES[n]
            return t_ref[ROW_HEADS:ROW_HEADS + 1, j * LANES:j * LANES + SSD_HEADS]
        if n == "ssd_conv_w":
            rows = [jnp.concatenate([t_ref[ROW_CONV_W + 2 * k:ROW_CONV_W + 2 * k + 1, :],
                                     t_ref[ROW_CONV_W + 2 * k + 1:ROW_CONV_W + 2 * k + 2, 0:SPLIT]], axis=1)
                    for k in range(SSD_CONV)]
            return shard(jnp.concatenate(rows, axis=0), chip, XBC_WIDTH // N_CHIPS)
        return shard(t_ref[ROW_DW:ROW_DW + CONF_KERNEL, :], chip, CONF_WIDTH // N_CHIPS)

    def body(c_ref, t_ref, *refs):
        ins, outs = refs[:3 * n_par], refs[3 * n_par:]
        chip = c_ref[1]
        for i, n in enumerate(SMALL_ORDER):
            w_ref, m_ref, v_ref = ins[3 * i:3 * i + 3]
            g_ref, d_ref, mo_ref, vo_ref = outs[4 * i:4 * i + 4]
            g = grad_of(n, t_ref, chip)
            m2 = ADAM_B1 * m_ref[...] + (1.0 - ADAM_B1) * g
            v2 = ADAM_B2 * v_ref[...] + (1.0 - ADAM_B2) * (g * g)
            g_ref[...] = g
            mo_ref[...] = m2
            vo_ref[...] = v2
            d_ref[...] = -ADAM_LR * ((m2 / c1) / (jnp.sqrt(v2 / c2) + ADAM_EPS) + ADAM_WD * w_ref[...])

    args, in_specs, out_specs, out_shape = [], [], [], []
    for n in SMALL_ORDER:
        shp = w[n].shape
        spec = pl.BlockSpec(shp, lambda i, c_ref: (0, 0))
        args += [w[n], m[n], v[n]]
        in_specs += [spec] * 3
        out_specs += [spec] * 4
        out_shape += [jax.ShapeDtypeStruct(shp, f32)] * 4
    outs = _pallas(
        body, name="adamw_small",
        grid_spec=pltpu.PrefetchScalarGridSpec(
            num_scalar_prefetch=1, grid=(1,),
            in_specs=[pl.BlockSpec(tot.shape, lambda i, c_ref: (0, 0))] + in_specs, out_specs=out_specs),
        out_shape=out_shape,
    )(cidx, tot, *args)
    grad, delta, new_m, new_v = {}, {}, {}, {}
    for i, n in enumerate(SMALL_ORDER):
        grad[n], delta[n], new_m[n], new_v[n] = outs[4 * i:4 * i + 4]
    return grad, delta, new_m, new_v


BIG = ("w_in", "w_out", "w_up", "w_down", "w_ple_gate", "w_ple")
BIG_A = (("w_up", UP_OFF), ("w_down", DOWN_OFF), ("w_out", OUT_OFF), ("w_ple_gate", PG_OFF))
WEIGHTS = ("mix_norm_g", "w_in", "ssd_conv_w", "ssd_conv_b", "dt_bias", "A_log", "D_skip", "ssd_norm_g", "conf_dw_w",
           "conf_dw_b", "conf_ln_g", "conf_ln_b", "w_out", "mlp_norm_g", "w_up", "w_down", "ple_gate_norm_g",
           "w_ple_gate", "b_ple_gate", "w_ple", "ple_norm_g", "final_norm_g")


def kernel(x, p, mix_norm_g, w_in, ssd_conv_w, ssd_conv_b, dt_bias, A_log, D_skip, ssd_norm_g, conf_dw_w, conf_dw_b, conf_ln_g, conf_ln_b, w_out, mlp_norm_g, w_up, w_down, ple_gate_norm_g, w_ple_gate, b_ple_gate, w_ple, ple_norm_g, final_norm_g, loss_target, m_mix_norm_g, m_w_in, m_ssd_conv_w, m_ssd_conv_b, m_dt_bias, m_A_log, m_D_skip, m_ssd_norm_g, m_conf_dw_w, m_conf_dw_b, m_conf_ln_g, m_conf_ln_b, m_w_out, m_mlp_norm_g, m_w_up, m_w_down, m_ple_gate_norm_g, m_w_ple_gate, m_b_ple_gate, m_w_ple, m_ple_norm_g, m_final_norm_g, v_mix_norm_g, v_w_in, v_ssd_conv_w, v_ssd_conv_b, v_dt_bias, v_A_log, v_D_skip, v_ssd_norm_g, v_conf_dw_w, v_conf_dw_b, v_conf_ln_g, v_conf_ln_b, v_w_out, v_mlp_norm_g, v_w_up, v_w_down, v_ple_gate_norm_g, v_w_ple_gate, v_b_ple_gate, v_w_ple, v_ple_norm_g, v_final_norm_g):
    w = dict(mix_norm_g=mix_norm_g, w_in=w_in, ssd_conv_w=ssd_conv_w, ssd_conv_b=ssd_conv_b, dt_bias=dt_bias, A_log=A_log,
             D_skip=D_skip, ssd_norm_g=ssd_norm_g, conf_dw_w=conf_dw_w, conf_dw_b=conf_dw_b, conf_ln_g=conf_ln_g,
             conf_ln_b=conf_ln_b, w_out=w_out, mlp_norm_g=mlp_norm_g, w_up=w_up, w_down=w_down,
             ple_gate_norm_g=ple_gate_norm_g, w_ple_gate=w_ple_gate, b_ple_gate=b_ple_gate, w_ple=w_ple,
             ple_norm_g=ple_norm_g, final_norm_g=final_norm_g)
    m = dict(mix_norm_g=m_mix_norm_g, w_in=m_w_in, ssd_conv_w=m_ssd_conv_w, ssd_conv_b=m_ssd_conv_b, dt_bias=m_dt_bias,
             A_log=m_A_log, D_skip=m_D_skip, ssd_norm_g=m_ssd_norm_g, conf_dw_w=m_conf_dw_w, conf_dw_b=m_conf_dw_b,
             conf_ln_g=m_conf_ln_g, conf_ln_b=m_conf_ln_b, w_out=m_w_out, mlp_norm_g=m_mlp_norm_g, w_up=m_w_up,
             w_down=m_w_down, ple_gate_norm_g=m_ple_gate_norm_g, w_ple_gate=m_w_ple_gate, b_ple_gate=m_b_ple_gate,
             w_ple=m_w_ple, ple_norm_g=m_ple_norm_g, final_norm_g=m_final_norm_g)
    v = dict(mix_norm_g=v_mix_norm_g, w_in=v_w_in, ssd_conv_w=v_ssd_conv_w, ssd_conv_b=v_ssd_conv_b, dt_bias=v_dt_bias,
             A_log=v_A_log, D_skip=v_D_skip, ssd_norm_g=v_ssd_norm_g, conf_dw_w=v_conf_dw_w, conf_dw_b=v_conf_dw_b,
             conf_ln_g=v_conf_ln_g, conf_ln_b=v_conf_ln_b, w_out=v_w_out, mlp_norm_g=v_mlp_norm_g, w_up=v_w_up,
             w_down=v_w_down, ple_gate_norm_g=v_ple_gate_norm_g, w_ple_gate=v_w_ple_gate, b_ple_gate=v_b_ple_gate,
             w_ple=v_w_ple, ple_norm_g=v_ple_norm_g, final_norm_g=v_final_norm_g)
    xi, yi, ci = lax.axis_index("x"), lax.axis_index("y"), lax.axis_index("c")
    chip = 2 * xi + yi

    slab = jnp.concatenate([w_up[0], w_down[0], w_out[0], w_ple_gate[0], _rows(w_ple[0])], axis=0).astype(bf16)
    gath0 = lax.dynamic_update_slice(lax.empty((N_CHIPS, SLAB_A, D_MODEL), bf16), slab[None], (chip, 0, 0))
    wt_shard = jnp.swapaxes(w_in, 1, 2).astype(bf16)
    wt_shard = jnp.pad(wt_shard, ((0, 0), (0, W_IN_ROWS_PAD - W_IN_ROWS), (0, 0)))
    gin0 = lax.dynamic_update_slice(lax.empty((N_CHIPS, W_IN_ROWS_PAD, D_MODEL), bf16), wt_shard, (chip, 0, 0))
    convw = _pad_rows(jnp.concatenate([ssd_conv_w[0].reshape(-1), conf_dw_w[0].reshape(-1)]), CONVW_ROWS)
    gin, cwg = _gather_weights([gin0], convw)
    n_sc = SSD_CONV * (XBC_WIDTH // N_CHIPS)
    n_cf = CONF_KERNEL * (CONF_WIDTH // N_CHIPS)
    S = {n: w[n][0] for n in ("mix_norm_g", "ssd_conv_b", "dt_bias", "A_log", "D_skip", "ssd_norm_g", "conf_dw_b",
                              "conf_ln_g", "conf_ln_b", "mlp_norm_g", "ple_gate_norm_g", "b_ple_gate", "ple_norm_g")}
    S = {n: a.reshape(1, -1) for n, a in S.items()}
    S["final_norm_g"] = final_norm_g.reshape(1, -1)
    S["ssd_conv_w"] = jnp.concatenate(
        [cwg[b].reshape(-1)[:n_sc].reshape(SSD_CONV, XBC_WIDTH // N_CHIPS) for b in range(N_CHIPS)], axis=1)
    S["conf_dw_w"] = jnp.concatenate(
        [cwg[b].reshape(-1)[n_sc:n_sc + n_cf].reshape(CONF_KERNEL, CONF_WIDTH // N_CHIPS) for b in range(N_CHIPS)], axis=1)

    cidx = jnp.stack([ci, chip]).astype(jnp.int32)
    grad_x, (ga, recv_a, ici_a), (gb, recv_b, ici_b), gsmall = _local_step(
        x[0], p[0, 0], loss_target[0], gath0, cidx, gin, S)

    all_small = _gather_small(_pack_small(gsmall))
    ra = _final_sum(cidx, ga, recv_a, ici_a, "final_sum_a")
    rb = _final_sum(cidx, gb, recv_b, ici_b, "final_sum_b")
    ra, rb = _join_halves(ra, rb)
    tot_small = _sum_small(all_small)

    loss = tot_small[ROW_HEADS, HEAD_LANES["loss"] * LANES]

    two_d = lambda a: a.reshape(a.shape[-2:]) if a.ndim > 1 else a.reshape(1, -1)
    small_w, small_m, small_v = ({n: two_d(d[n]) for n in SMALL_ORDER} for d in (w, m, v))
    grads, delta, new_m, new_v = _adamw_small(cidx, tot_small, small_w, small_m, small_v)
    g_in_t = rb[:W_IN_ROWS]
    grads["w_ple"] = _ple_of_slab(ra)
    grads["w_in"] = jnp.swapaxes(g_in_t, 0, 1)
    for n, off in BIG_A:
        grads[n] = ra[off:off + w[n].shape[1]]
        delta[n], new_m[n], new_v[n] = _adamw(w[n][0], ra, m[n][0], v[n][0], "adamw_" + n, g_off=off)
    delta["w_ple"], new_m["w_ple"], new_v["w_ple"] = _adamw(w_ple[0], grads["w_ple"], m_w_ple[0], v_w_ple[0], "adamw_w_ple")
    tr_ = lambda a: jnp.swapaxes(a[0], 0, 1)
    d_, m_, v_ = _adamw(tr_(w_in), g_in_t, tr_(m_w_in), tr_(v_w_in), "adamw_w_in", by_columns=True)
    delta["w_in"], new_m["w_in"], new_v["w_in"] = (jnp.swapaxes(a, 0, 1) for a in (d_, m_, v_))

    shaped = lambda d: [d[n].reshape(w[n].shape) for n in WEIGHTS]
    return (loss, grad_x[None], *shaped(grads), *shaped(delta), *shaped(new_m), *shaped(new_v))
```

```python
import jax
import jax.numpy as jnp
from jax import lax
from jax.experimental import pallas as pl
from jax.experimental.pallas import tpu as pltpu

f32 = jnp.float32
bf16 = jnp.bfloat16

D_MODEL = 1024
SSD_WIDTH = 1024
SSD_HEADS = 16
HEAD_DIM = 64
SSD_STATE = 128
XBC_WIDTH = 1536
SSD_CONV = 4
CHUNK = 128
CONF_WIDTH = 1024
CONF_KERNEL = 31
D_FF = 4096
PLE_DIM = 256
IN_WIDTH = 4624
EPS = 1e-6
N_CHIPS = 4
N_DEV = 8

ADAM_LR = 0.001
ADAM_B1 = 0.9
ADAM_B2 = 0.999
ADAM_EPS = 1e-08
ADAM_WD = 0.01
ADAM_STEP = 10

LANES = 128
VMEM_BIG = 56 * 1024 * 1024
VMEM_MID = 40 * 1024 * 1024

UP_OFF, DOWN_OFF, OUT_OFF, PG_OFF, PLE_OFF = 0, 1024, 2048, 2560, 2816
PLE_ROWS = 64
SLAB_A = PLE_OFF + PLE_ROWS
GATHER_EARLY_ROWS = 480
W_IN_ROWS = 1156
W_IN_ROWS_PAD = 1184
CONVW_ROWS = 16
SMALL_ROWS = 56

MESH = pl.DeviceIdType.MESH
ANY = pl.BlockSpec(memory_space=pl.ANY)


PIN_SMALL = 256 * 1024


def _pallas(body, pin_bytes=None, **kw):
    call = pl.pallas_call(body, **kw)

    def pin(a):
        wanted = pin_bytes is None or a.size * a.dtype.itemsize <= pin_bytes
        return pltpu.with_memory_space_constraint(a, pltpu.HBM) if wanted and a.dtype != jnp.int32 else a

    def run(*args):
        return call(*[pin(a) for a in args])

    return run


def _cparams(sem=None, vmem=None):
    return pltpu.CompilerParams(dimension_semantics=sem, vmem_limit_bytes=vmem)


def _full(shape):
    n = len(shape)
    return pl.BlockSpec(shape, lambda *_: (0,) * n)


class _Rider:
    def __init__(self, inputs, out_shapes, aliases, n_sems, start, finish):
        self.inputs, self.out_shapes, self.aliases = list(inputs), list(out_shapes), dict(aliases)
        self.n_sems, self.start, self.finish = n_sems, start, finish


def _call(body, args, *, name, grid, in_specs, out_specs, out_shape, scratch_shapes=(), params=None, rider=None):
    if rider is None:
        return _pallas(body, name=name, grid=grid, in_specs=in_specs, out_specs=out_specs, out_shape=out_shape,
                              scratch_shapes=list(scratch_shapes), compiler_params=params)(*args)
    ni, no, ns = len(in_specs), len(out_specs), len(scratch_shapes)
    ri, ro = len(rider.inputs), len(rider.out_shapes)
    (steps,) = grid

    def with_rider(*refs):
        ins, refs = refs[:ni], refs[ni:]
        rins, refs = refs[:ri], refs[ri:]
        outs, refs = refs[:no], refs[no:]
        routs, refs = refs[:ro], refs[ro:]
        scratch, (ssem, rsem) = refs[:ns], refs[ns:]
        step = pl.program_id(0)

        @pl.when(step == 0)
        def _():
            rider.start(rins, routs, ssem, rsem)

        body(*ins, *outs, *scratch)

        @pl.when(step == steps - 1)
        def _():
            rider.finish(rins, routs, ssem, rsem)

    sems = [pltpu.SemaphoreType.DMA((rider.n_sems,)), pltpu.SemaphoreType.DMA((rider.n_sems,))]
    return _pallas(
        with_rider, name=name, grid=grid, in_specs=list(in_specs) + [ANY] * ri, out_specs=list(out_specs) + [ANY] * ro,
        out_shape=list(out_shape) + rider.out_shapes, scratch_shapes=list(scratch_shapes) + sems,
        input_output_aliases={ni + a: no + b for a, b in rider.aliases.items()}, compiler_params=params,
    )(*args, *rider.inputs)


def _dot(a, b):
    return jnp.dot(a, b, preferred_element_type=f32)


def _dot_nt(a, b):
    return lax.dot_general(a, b, (((1,), (1,)), ((), ())), preferred_element_type=f32)


def _dot_tn(a, b):
    return lax.dot_general(a, b, (((0,), (0,)), ((), ())), preferred_element_type=f32)


def _sigmoid(x):
    return jax.nn.sigmoid(x)


def _rms(x, g):
    r = lax.rsqrt(jnp.mean(x * x, axis=-1, keepdims=True) + EPS)
    return x * r * g


def _rms_bwd(dy, x, g):
    r = lax.rsqrt(jnp.mean(x * x, axis=-1, keepdims=True) + EPS)
    xh = x * r
    dg = jnp.sum(dy * xh, axis=0, keepdims=True)
    dxh = dy * g
    dx = r * (dxh - xh * jnp.mean(dxh * xh, axis=-1, keepdims=True))
    return dx, dg


def _dsilu(x):
    s = _sigmoid(x)
    return s * (1.0 + x * (1.0 - s))


def _split3(x):
    hi = x.astype(bf16)
    r1 = x - hi.astype(f32)
    mid = r1.astype(bf16)
    lo = (r1 - mid.astype(f32)).astype(bf16)
    return hi, mid, lo


def _head_matrix():
    row = lax.broadcasted_iota(jnp.int32, (LANES, SSD_WIDTH), 0)
    col = lax.broadcasted_iota(jnp.int32, (LANES, SSD_WIDTH), 1)
    lo = row * HEAD_DIM
    return ((col >= lo) & (col < lo + HEAD_DIM)).astype(bf16)


def _expand(x, e):
    hi, mid, lo = _split3(x)
    return _dot(hi, e) + _dot(mid, e) + _dot(lo, e)


def _contract(x, e):
    hi = x.astype(bf16)
    mid = (x - hi.astype(f32)).astype(bf16)
    return _dot_nt(hi, e) + _dot_nt(mid, e)


O_XBC = SSD_WIDTH
O_DT = O_XBC + XBC_WIDTH
O_CV = O_DT + SSD_HEADS
O_CG = O_CV + CONF_WIDTH


def _assemble_w_in_t(gin_ref, wt_ref):
    for b in range(N_CHIPS):
        wt_ref[b * W_IN_ROWS:(b + 1) * W_IN_ROWS, :] = gin_ref[b, 0:W_IN_ROWS, :]


def _in_proj_fwd(x, g, gin, rider=None):
    T = x.shape[0]
    tm = min(256, T)

    def body(x_ref, g_ref, gin_ref, u_ref, z_ref, xbc_ref, cv_ref, cg_ref, dt_ref, v_ref, wt_ref):
        @pl.when(pl.program_id(0) == 0)
        def _():
            _assemble_w_in_t(gin_ref, wt_ref)

        ub = _rms(x_ref[...], g_ref[...]).astype(bf16)
        u_ref[...] = ub
        z_ref[...] = _dot_nt(ub, wt_ref[0:O_XBC, :])
        xbc_ref[...] = _dot_nt(ub, wt_ref[O_XBC:O_DT, :])
        cv = _dot_nt(ub, wt_ref[O_CV:O_CG, :])
        cg = _dot_nt(ub, wt_ref[O_CG:IN_WIDTH, :])
        cv_ref[...] = cv
        cg_ref[...] = cg
        v_ref[...] = cv * _sigmoid(cg)
        dt_ref[...] = _dot_nt(ub, wt_ref[O_DT:O_DT + LANES, :])

    row = lambda n: pl.BlockSpec((tm, n), lambda i: (i, 0))
    return _call(
        body, (x, g, gin), name="in_proj_fwd", grid=(T // tm,),
        in_specs=[row(D_MODEL), _full((1, D_MODEL)), _full(gin.shape)],
        out_specs=[row(D_MODEL), row(SSD_WIDTH), row(XBC_WIDTH), row(CONF_WIDTH), row(CONF_WIDTH), row(LANES),
                   row(CONF_WIDTH)],
        out_shape=[jax.ShapeDtypeStruct((T, D_MODEL), bf16), jax.ShapeDtypeStruct((T, SSD_WIDTH), f32),
                   jax.ShapeDtypeStruct((T, XBC_WIDTH), f32), jax.ShapeDtypeStruct((T, CONF_WIDTH), f32),
                   jax.ShapeDtypeStruct((T, CONF_WIDTH), f32), jax.ShapeDtypeStruct((T, LANES), f32),
                   jax.ShapeDtypeStruct((T, CONF_WIDTH), f32)],
        scratch_shapes=[pltpu.VMEM((IN_WIDTH, D_MODEL), bf16)],
        params=_cparams(("arbitrary",), VMEM_BIG), rider=rider)


SUBLANES = 8


def _phases(offsets):
    return sorted({o % SUBLANES for o in offsets} - {0})


def _phase_shape(offsets, tm, C):
    a_max = max([o // SUBLANES for o in offsets if o % SUBLANES] or [0])
    return (max(len(_phases(offsets)), 1), tm + SUBLANES * a_max, C)


def _make_phases(buf_ref, ph_ref, offsets, tm):
    for idx, b in enumerate(_phases(offsets)):
        n = tm + SUBLANES * max(o // SUBLANES for o in offsets if o % SUBLANES == b)
        ph_ref[idx, 0:n, :] = buf_ref[pl.ds(b, n), :]


def _window(buf_ref, ph_ref, offsets, o, r0, rb):
    a, b = divmod(o, SUBLANES)
    if b == 0:
        return buf_ref[pl.ds(r0 + SUBLANES * a, rb), :]
    return ph_ref[_phases(offsets).index(b), pl.ds(r0 + SUBLANES * a, rb), :]


def _conv_rows(wb_ref, buf_ref, ph_ref, offsets, r0, rb):
    nsub = rb // SUBLANES
    accs = [None] * nsub
    for k, o in enumerate(offsets):
        wk = wb_ref[pl.ds(SUBLANES * k, SUBLANES), :]
        for s in range(nsub):
            term = wk * _window(buf_ref, ph_ref, offsets, o, r0 + SUBLANES * s, SUBLANES)
            accs[s] = term if accs[s] is None else accs[s] + term
    return accs[0] if nsub == 1 else jnp.concatenate(accs, axis=0)


def _sublane_rows(w):
    return jnp.repeat(w, SUBLANES, axis=0)


def _fwd_offsets(K, hb):
    return [hb - (K - 1) + k for k in range(K)]


def _prev_halo_spec(hb, tm, C):
    return pl.BlockSpec((hb, C), lambda i: (jnp.maximum(i * (tm // hb) - 1, 0), 0))


CONV_RB = 16


def _ssd_conv_fwd(xbc, w, b):
    T, C = xbc.shape
    K, hb = SSD_CONV, 8
    tm = min(256, T)
    offs = _fwd_offsets(K, hb)

    def body(cur_ref, halo_ref, w_ref, b_ref, pre_ref, buf_ref, ph_ref):
        keep = jnp.where(pl.program_id(0) > 0, 1.0, 0.0)
        buf_ref[0:hb, :] = halo_ref[...] * keep
        buf_ref[hb:hb + tm, :] = cur_ref[...]
        _make_phases(buf_ref, ph_ref, offs, tm)

        def chunk(i, carry):
            r0 = pl.multiple_of(i * CONV_RB, CONV_RB)
            pre_ref[pl.ds(r0, CONV_RB), :] = _conv_rows(w_ref, buf_ref, ph_ref, offs, r0, CONV_RB) + b_ref[...]
            return carry

        lax.fori_loop(0, tm // CONV_RB, chunk, 0)

    return _pallas(
        body, name="ssd_conv_fwd", grid=(T // tm,),
        in_specs=[pl.BlockSpec((tm, C), lambda i: (i, 0)), _prev_halo_spec(hb, tm, C), _full((SUBLANES * K, C)),
                  _full((1, C))],
        out_specs=pl.BlockSpec((tm, C), lambda i: (i, 0)),
        out_shape=jax.ShapeDtypeStruct((T, C), f32),
        scratch_shapes=[pltpu.VMEM((hb + tm, C), f32), pltpu.VMEM(_phase_shape(offs, tm, C), f32)],
        compiler_params=_cparams(("parallel",), VMEM_MID),
    )(xbc, xbc, _sublane_rows(w), b)


def _conf_fwd(v, w, b, ln_g, ln_b, rider=None):
    T, C = v.shape
    K, hb = CONF_KERNEL, 32
    tm = min(256, T)
    offs = _fwd_offsets(K, hb)
    rb = 2 * CONV_RB

    def body(cur_ref, halo_ref, w_ref, b_ref, g_ref, bb_ref, co_ref, y_ref, buf_ref, ph_ref):
        keep = jnp.where(pl.program_id(0) > 0, 1.0, 0.0)
        buf_ref[0:hb, :] = halo_ref[...] * keep
        buf_ref[hb:hb + tm, :] = cur_ref[...]
        _make_phases(buf_ref, ph_ref, offs, tm)

        def chunk(i, carry):
            r0 = pl.multiple_of(i * rb, rb)
            co = _conv_rows(w_ref, buf_ref, ph_ref, offs, r0, rb) + b_ref[...]
            co_ref[pl.ds(r0, rb), :] = co
            mu = jnp.mean(co, axis=-1, keepdims=True)
            xc = co - mu
            yn = xc * lax.rsqrt(jnp.mean(xc * xc, axis=-1, keepdims=True) + EPS) * g_ref[...] + bb_ref[...]
            y_ref[pl.ds(r0, rb), :] = (yn * _sigmoid(yn)).astype(bf16)
            return carry

        lax.fori_loop(0, tm // rb, chunk, 0)

    return _call(
        body, (v, v, _sublane_rows(w), b, ln_g, ln_b), name="conf_fwd", grid=(T // tm,),
        in_specs=[pl.BlockSpec((tm, C), lambda i: (i, 0)), _prev_halo_spec(hb, tm, C), _full((SUBLANES * K, C)),
                  _full((1, C)), _full((1, C)), _full((1, C))],
        out_specs=[pl.BlockSpec((tm, C), lambda i: (i, 0)), pl.BlockSpec((tm, C), lambda i: (i, 0))],
        out_shape=[jax.ShapeDtypeStruct((T, C), f32), jax.ShapeDtypeStruct((T, C), bf16)],
        scratch_shapes=[pltpu.VMEM((hb + tm, C), f32), pltpu.VMEM(_phase_shape(offs, tm, C), f32)],
        params=_cparams(("arbitrary",), VMEM_MID), rider=rider)


def _ssd_chunk_common(pre, dtr, dtb, alog, e):
    act = pre * _sigmoid(pre)
    xs = act[:, :SSD_WIDTH]
    bm = act[:, SSD_WIDTH:SSD_WIDTH + 2 * SSD_STATE]
    cm = act[:, SSD_WIDTH + 2 * SSD_STATE:]
    row = lax.broadcasted_iota(jnp.int32, (CHUNK, CHUNK), 0)
    col = lax.broadcasted_iota(jnp.int32, (CHUNK, CHUNK), 1)
    tri = row >= col
    dt = jax.nn.softplus(dtr + dtb)
    a_neg = -jnp.exp(alog)
    a = dt * a_neg
    cs = jnp.dot(tri.astype(f32), a, precision=lax.Precision.HIGHEST, preferred_element_type=f32)
    cs_e = _expand(cs, e)
    dt_e = _expand(dt, e)
    csl_e = cs_e[CHUNK - 1:CHUNK, :]
    ecs_e = jnp.exp(cs_e)
    dte_e = jnp.exp(csl_e - cs_e)
    cd_e = jnp.exp(csl_e)
    xc = xs * dt_e
    xd = xc * dte_e
    return dict(xs=xs, bm=bm, cm=cm, tri=tri, dt=dt, a_neg=a_neg, cs=cs, ecs_e=ecs_e, dte_e=dte_e, cd_e=cd_e,
                dt_e=dt_e, xc=xc, xd=xd)


def _group(v, g, width):
    return v[:, g * width:(g + 1) * width]


def _ssd_fwd(pre, dtr, z, dtb, alog, dskip_e, gn):
    T = pre.shape[0]
    nc = T // CHUNK
    GW = SSD_WIDTH // 2

    def body(pre_ref, dtr_ref, z_ref, dtb_ref, alog_ref, de_ref, gn_ref, y_ref, ys_ref, sp_ref, st_ref):
        @pl.when(pl.program_id(0) == 0)
        def _():
            st_ref[...] = jnp.zeros_like(st_ref)

        e = _head_matrix()
        q = _ssd_chunk_common(pre_ref[...], dtr_ref[...], dtb_ref[...], alog_ref[...], e)
        cs, tri, xc, xd = q["cs"], q["tri"], q["xc"], q["xd"]
        cs_t = cs.T
        st = st_ref[...]
        sp_ref[0] = st
        lane = lax.broadcasted_iota(jnp.int32, (1, LANES), 1)
        halves = (lane < HEAD_DIM, lane >= HEAD_DIM)

        g_mat, y_off, s_new = [], [], []
        for g in range(2):
            bg = _group(q["bm"], g, SSD_STATE)
            cg = _group(q["cm"], g, SSD_STATE)
            bgb, cgb = bg.astype(bf16), cg.astype(bf16)
            g_mat.append(_dot_nt(cgb, bgb))
            y_off.append(_dot(cgb, _group(st, g, GW).astype(bf16)))
            s_new.append(_dot(bg.T.astype(bf16), _group(xd, g, GW).astype(bf16)))
        y_off = jnp.concatenate(y_off, axis=1) * q["ecs_e"]
        st_ref[...] = st * q["cd_e"] + jnp.concatenate(s_new, axis=1)

        pairs = []
        for j in range(SSD_HEADS // 2):
            xp = xc[:, j * LANES:(j + 1) * LANES]
            acc = jnp.zeros((CHUNK, LANES), f32)
            for hh in range(2):
                h = 2 * j + hh
                seg = cs[:, h:h + 1] - cs_t[h:h + 1, :]
                lm = jnp.exp(jnp.where(tri, seg, -1e30))
                m = (g_mat[h // 8] * lm).astype(bf16)
                acc = acc + _dot(m, jnp.where(halves[hh], xp, 0.0).astype(bf16))
            pairs.append(acc)
        y = jnp.concatenate(pairs, axis=1) + y_off + q["xs"] * de_ref[...]
        y_ref[...] = y

        zz = z_ref[...]
        v = y * (zz * _sigmoid(zz))
        outs = []
        for g in range(2):
            vg = _group(v, g, GW)
            outs.append(vg * lax.rsqrt(jnp.mean(vg * vg, axis=-1, keepdims=True) + EPS))
        ys_ref[...] = (jnp.concatenate(outs, axis=1) * gn_ref[...]).astype(bf16)

    ch = lambda n: pl.BlockSpec((CHUNK, n), lambda c: (c, 0))
    return _pallas(
        body, name="ssd_fwd", grid=(nc,),
        in_specs=[ch(XBC_WIDTH), ch(LANES), ch(SSD_WIDTH), _full((1, LANES)), _full((1, LANES)), _full((1, SSD_WIDTH)),
                  _full((1, SSD_WIDTH))],
        out_specs=[ch(SSD_WIDTH), ch(SSD_WIDTH), pl.BlockSpec((1, SSD_STATE, SSD_WIDTH), lambda c: (c, 0, 0))],
        out_shape=[jax.ShapeDtypeStruct((T, SSD_WIDTH), f32), jax.ShapeDtypeStruct((T, SSD_WIDTH), bf16),
                   jax.ShapeDtypeStruct((nc, SSD_STATE, SSD_WIDTH), f32)],
        scratch_shapes=[pltpu.VMEM((SSD_STATE, SSD_WIDTH), f32)],
        compiler_params=_cparams(("arbitrary",), VMEM_MID),
    )(pre, dtr, z, dtb, alog, dskip_e, gn)


def _w_out_spec():
    n = 2 * SSD_WIDTH // N_CHIPS
    return pl.BlockSpec((N_CHIPS, n, D_MODEL), lambda *_: (0, OUT_OFF // n, 0))


def _out_proj_fwd(x, ys, yc, gath, g):
    T = x.shape[0]
    tm = min(512, T)
    n = 2 * SSD_WIDTH // N_CHIPS

    def body(x_ref, ys_ref, yc_ref, w_ref, g_ref, h_ref, u_ref):
        h = (x_ref[...] + _dot(ys_ref[:, 0:n], w_ref[0]) + _dot(ys_ref[:, n:], w_ref[1])
             + _dot(yc_ref[:, 0:n], w_ref[2]) + _dot(yc_ref[:, n:], w_ref[3]))
        h_ref[...] = h
        u_ref[...] = _rms(h, g_ref[...]).astype(bf16)

    row = pl.BlockSpec((tm, D_MODEL), lambda i: (i, 0))
    return _pallas(
        body, name="out_proj_fwd", grid=(T // tm,),
        in_specs=[row, row, row, _w_out_spec(), _full((1, D_MODEL))],
        out_specs=[row, row],
        out_shape=[jax.ShapeDtypeStruct((T, D_MODEL), f32), jax.ShapeDtypeStruct((T, D_MODEL), bf16)],
        compiler_params=_cparams(("parallel",), VMEM_MID),
    )(x, ys, yc, gath, g)


def _w_up_spec():
    return pl.BlockSpec((1, D_MODEL, D_MODEL), lambda i, b: (b, UP_OFF // D_MODEL, 0))


def _w_down_spec():
    return pl.BlockSpec((1, D_MODEL, D_MODEL), lambda i, b: (b, DOWN_OFF // D_MODEL, 0))


def _mlp_fwd(h1, u1, gath, g_next):
    T = h1.shape[0]
    tm = min(512, T)
    nb = D_FF // D_MODEL

    def body(h_ref, u_ref, wu_ref, wd_ref, g_ref, r_ref, h2_ref, u2_ref, acc_ref):
        b = pl.program_id(1)

        @pl.when(b == 0)
        def _():
            acc_ref[...] = jnp.zeros_like(acc_ref)

        r = jnp.maximum(_dot(u_ref[...], wu_ref[0]), 0.0)
        r_ref[...] = r.astype(bf16)
        acc_ref[...] += _dot((r * r).astype(bf16), wd_ref[0])

        @pl.when(b == nb - 1)
        def _():
            h2 = h_ref[...] + acc_ref[...]
            h2_ref[...] = h2
            u2_ref[...] = _rms(h2, g_ref[...]).astype(bf16)

    row = pl.BlockSpec((tm, D_MODEL), lambda i, b: (i, 0))
    return _pallas(
        body, name="mlp_fwd", grid=(T // tm, nb),
        in_specs=[row, row, _w_up_spec(), _w_down_spec(), _full((1, D_MODEL))],
        out_specs=[pl.BlockSpec((tm, D_MODEL), lambda i, b: (i, b)), row, row],
        out_shape=[jax.ShapeDtypeStruct((T, D_FF), bf16), jax.ShapeDtypeStruct((T, D_MODEL), f32),
                   jax.ShapeDtypeStruct((T, D_MODEL), bf16)],
        scratch_shapes=[pltpu.VMEM((tm, D_MODEL), f32)],
        compiler_params=_cparams(("parallel", "arbitrary"), VMEM_MID),
    )(h1, u1, gath, gath, g_next)


def _ple_loss(h2, u2, p, tgt, gath, b_pg, w_ple, g_ple, g_fin, g_pg):
    T = h2.shape[0]
    tm = min(256, T)
    npg = D_MODEL // N_CHIPS

    def body(h2_ref, u2_ref, p_ref, t_ref, wpg_ref, bpg_ref, wple_ref, gple_ref, gfin_ref, gpg_ref,
             loss_ref, dh2_ref, dh2b_ref, dgp_ref, dep_ref, dgfin_ref, dgple_ref, dbpg_ref, dgpg_ref):
        @pl.when(pl.program_id(0) == 0)
        def _():
            loss_ref[...] = jnp.zeros_like(loss_ref)
            dgfin_ref[...] = jnp.zeros_like(dgfin_ref)
            dgple_ref[...] = jnp.zeros_like(dgple_ref)
            dbpg_ref[...] = jnp.zeros_like(dbpg_ref)
            dgpg_ref[...] = jnp.zeros_like(dgpg_ref)

        h2 = h2_ref[...]
        gate_pre = bpg_ref[...]
        for b in range(N_CHIPS):
            gate_pre = gate_pre + _dot(u2_ref[:, b * npg:(b + 1) * npg], wpg_ref[b])
        gate = _sigmoid(gate_pre)
        e_pre = _dot(p_ref[...].astype(bf16), wple_ref[...])
        emb = _rms(e_pre, gple_ref[...])
        h3 = h2 + gate * emb
        diff = _rms(h3, gfin_ref[...]) - t_ref[...]
        sq = jnp.sum(jnp.sum(diff * diff, axis=1, keepdims=True), axis=0, keepdims=True)
        loss_ref[...] += (0.5 / D_MODEL) * sq
        dh3, dgfin = _rms_bwd(diff * (1.0 / D_MODEL), h3, gfin_ref[...])
        dgfin_ref[...] += dgfin
        dgp = dh3 * emb * gate * (1.0 - gate)
        dbpg_ref[...] += jnp.sum(dgp, axis=0, keepdims=True)
        dep, dgple = _rms_bwd(dh3 * gate, e_pre, gple_ref[...])
        dgple_ref[...] += dgple
        dgpb = dgp.astype(bf16)
        dgp_ref[...] = dgpb
        dep_ref[...] = dep.astype(bf16)
        du2 = jnp.concatenate([_dot_nt(dgpb, wpg_ref[b]) for b in range(N_CHIPS)], axis=1)
        dx, dgpg = _rms_bwd(du2, h2, gpg_ref[...])
        dgpg_ref[...] += dgpg
        dh2 = dh3 + dx
        dh2_ref[...] = dh2
        dh2b_ref[...] = dh2.astype(bf16)

    row = pl.BlockSpec((tm, D_MODEL), lambda i: (i, 0))
    vec = _full((1, D_MODEL))
    vshape = jax.ShapeDtypeStruct((1, D_MODEL), f32)
    return _pallas(
        body, name="ple_loss", grid=(T // tm,),
        in_specs=[row, row, pl.BlockSpec((tm, PLE_DIM), lambda i: (i, 0)), row,
                  pl.BlockSpec((N_CHIPS, npg, D_MODEL), lambda i: (0, PG_OFF // npg, 0)), vec, _full(w_ple.shape),
                  vec, vec, vec],
        out_specs=[_full((8, LANES)), row, row, row, row, vec, vec, vec, vec],
        out_shape=[jax.ShapeDtypeStruct((8, LANES), f32), jax.ShapeDtypeStruct((T, D_MODEL), f32),
                   jax.ShapeDtypeStruct((T, D_MODEL), bf16), jax.ShapeDtypeStruct((T, D_MODEL), bf16),
                   jax.ShapeDtypeStruct((T, D_MODEL), bf16), vshape, vshape, vshape, vshape],
        compiler_params=_cparams(("arbitrary",), VMEM_MID),
    )(h2, u2, p, tgt, gath, b_pg, w_ple, g_ple, g_fin, g_pg)


def _mlp_bwd(dh2, r, gath, h1, g):
    T = dh2.shape[0]
    tm = min(512, T)
    nb = D_FF // D_MODEL

    def body(dh2_ref, r_ref, wd_ref, wu_ref, h1_ref, g_ref, dhp_ref, dh1_ref, dh1b_ref, dg_ref, acc_ref):
        i, b = pl.program_id(0), pl.program_id(1)

        @pl.when(b == 0)
        def _():
            acc_ref[...] = jnp.zeros_like(acc_ref)

        @pl.when((b == 0) & (i == 0))
        def _():
            dg_ref[...] = jnp.zeros_like(dg_ref)

        dact = _dot_nt(dh2_ref[...].astype(bf16), wd_ref[0])
        dhp = (dact * 2.0 * r_ref[...].astype(f32)).astype(bf16)
        dhp_ref[...] = dhp
        acc_ref[...] += _dot_nt(dhp, wu_ref[0])

        @pl.when(b == nb - 1)
        def _():
            dx, dg = _rms_bwd(acc_ref[...], h1_ref[...], g_ref[...])
            dg_ref[...] += dg
            dh1 = dh2_ref[...] + dx
            dh1_ref[...] = dh1
            dh1b_ref[...] = dh1.astype(bf16)

    row = pl.BlockSpec((tm, D_MODEL), lambda i, b: (i, 0))
    return _pallas(
        body, name="mlp_bwd", grid=(T // tm, nb),
        in_specs=[row, pl.BlockSpec((tm, D_MODEL), lambda i, b: (i, b)), _w_down_spec(), _w_up_spec(), row,
                  _full((1, D_MODEL))],
        out_specs=[pl.BlockSpec((tm, D_MODEL), lambda i, b: (i, b)), row, row, _full((1, D_MODEL))],
        out_shape=[jax.ShapeDtypeStruct((T, D_FF), bf16), jax.ShapeDtypeStruct((T, D_MODEL), f32),
                   jax.ShapeDtypeStruct((T, D_MODEL), bf16), jax.ShapeDtypeStruct((1, D_MODEL), f32)],
        scratch_shapes=[pltpu.VMEM((tm, D_MODEL), f32)],
        compiler_params=_cparams(("arbitrary", "arbitrary"), VMEM_MID),
    )(dh2, r, gath, gath, h1, g)


def _out_proj_bwd(dh1, gath, co, ln_g, ln_b, rider=None):
    T = dh1.shape[0]
    tm = min(512, T)

    def body(dh_ref, w_ref, co_ref, g_ref, b_ref, dys_ref, dco_ref, dg_ref, db_ref):
        @pl.when(pl.program_id(0) == 0)
        def _():
            dg_ref[...] = jnp.zeros_like(dg_ref)
            db_ref[...] = jnp.zeros_like(db_ref)

        dhb = dh_ref[...].astype(bf16)
        dys_ref[...] = jnp.concatenate([_dot_nt(dhb, w_ref[0]), _dot_nt(dhb, w_ref[1])], axis=1)
        dyc = jnp.concatenate([_dot_nt(dhb, w_ref[2]), _dot_nt(dhb, w_ref[3])], axis=1)
        co = co_ref[...]
        mu = jnp.mean(co, axis=-1, keepdims=True)
        xc = co - mu
        rstd = lax.rsqrt(jnp.mean(xc * xc, axis=-1, keepdims=True) + EPS)
        xh = xc * rstd
        yn = xh * g_ref[...] + b_ref[...]
        dyn = dyc * _dsilu(yn)
        dg_ref[...] += jnp.sum(dyn * xh, axis=0, keepdims=True)
        db_ref[...] += jnp.sum(dyn, axis=0, keepdims=True)
        dxh = dyn * g_ref[...]
        dco_ref[...] = rstd * (dxh - jnp.mean(dxh, axis=-1, keepdims=True)
                               - xh * jnp.mean(dxh * xh, axis=-1, keepdims=True))

    row = pl.BlockSpec((tm, D_MODEL), lambda i: (i, 0))
    vec = _full((1, CONF_WIDTH))
    vshape = jax.ShapeDtypeStruct((1, CONF_WIDTH), f32)
    return _call(
        body, (dh1, gath, co, ln_g, ln_b), name="out_proj_bwd", grid=(T // tm,),
        in_specs=[row, _w_out_spec(), row, vec, vec],
        out_specs=[row, row, vec, vec],
        out_shape=[jax.ShapeDtypeStruct((T, SSD_WIDTH), f32), jax.ShapeDtypeStruct((T, CONF_WIDTH), f32), vshape, vshape],
        params=_cparams(("arbitrary",), VMEM_MID), rider=rider)


def _bwd_offsets(K):
    return [K - 1 - k for k in range(K)]


def _next_halo_spec(hb, tm, C, T):
    return pl.BlockSpec((hb, C), lambda i: (jnp.minimum((i + 1) * (tm // hb), T // hb - 1), 0))


DW_RB = 8
DW_UNROLL = 16
DW_ACC_VREGS = 32


def _conv_dw(dw_ref, bufd_ref, bufx_ref, phx_ref, offs_x, tm, C):
    K = len(offs_x)
    group = max(1, DW_ACC_VREGS // (C // LANES))
    for k0 in range(0, K, group):
        ks = list(range(k0, min(k0 + group, K)))

        def step(i, accs, ks=ks):
            for u in range(DW_UNROLL):
                r0 = pl.multiple_of((i * DW_UNROLL + u) * DW_RB, DW_RB)
                d = bufd_ref[pl.ds(r0, DW_RB), :]
                accs = tuple(acc + _window(bufx_ref, phx_ref, offs_x, offs_x[k], r0, DW_RB) * d
                             for k, acc in zip(ks, accs))
            return accs

        accs = lax.fori_loop(0, tm // (DW_RB * DW_UNROLL), step, tuple(jnp.zeros((DW_RB, C), f32) for _ in ks))
        for k, acc in zip(ks, accs):
            dw_ref[k:k + 1, :] += jnp.sum(acc, axis=0, keepdims=True)


def _fill_bwd_buffers(dcur_ref, dnext_ref, xcur_ref, xprev_ref, bufd_ref, bufx_ref, phd_ref, phx_ref, offs_d, offs_x,
                      hb, tm, first, last):
    bufd_ref[0:tm, :] = dcur_ref[...]
    bufd_ref[tm:tm + hb, :] = dnext_ref[...] * jnp.where(last, 0.0, 1.0)
    bufx_ref[0:hb, :] = xprev_ref[...] * jnp.where(first, 0.0, 1.0)
    bufx_ref[hb:hb + tm, :] = xcur_ref[...]
    _make_phases(bufd_ref, phd_ref, offs_d, tm)
    _make_phases(bufx_ref, phx_ref, offs_x, tm)


def _ssd_conv_bwd(dpre, xbc, w):
    T, C = xbc.shape
    K, hb = SSD_CONV, 8
    tm = min(256, T)
    nt = T // tm
    offs_d, offs_x = _bwd_offsets(K), _fwd_offsets(K, hb)

    def body(dcur_ref, dnext_ref, xcur_ref, xprev_ref, w_ref, dx_ref, dw_ref, db_ref, bufd_ref, bufx_ref, phd_ref, phx_ref):
        i = pl.program_id(0)

        @pl.when(i == 0)
        def _():
            dw_ref[...] = jnp.zeros_like(dw_ref)
            db_ref[...] = jnp.zeros_like(db_ref)

        _fill_bwd_buffers(dcur_ref, dnext_ref, xcur_ref, xprev_ref, bufd_ref, bufx_ref, phd_ref, phx_ref, offs_d, offs_x,
                          hb, tm, i == 0, i == nt - 1)

        def chunk(j, carry):
            r0 = pl.multiple_of(j * CONV_RB, CONV_RB)
            dx_ref[pl.ds(r0, CONV_RB), :] = _conv_rows(w_ref, bufd_ref, phd_ref, offs_d, r0, CONV_RB).astype(bf16)
            return carry

        lax.fori_loop(0, tm // CONV_RB, chunk, 0)
        _conv_dw(dw_ref, bufd_ref, bufx_ref, phx_ref, offs_x, tm, C)
        db_ref[...] += jnp.sum(dcur_ref[...], axis=0, keepdims=True)

    row = pl.BlockSpec((tm, C), lambda i: (i, 0))
    return _pallas(
        body, name="ssd_conv_bwd", grid=(nt,),
        in_specs=[row, _next_halo_spec(hb, tm, C, T), row, _prev_halo_spec(hb, tm, C), _full((SUBLANES * K, C))],
        out_specs=[row, _full((8, C)), _full((1, C))],
        out_shape=[jax.ShapeDtypeStruct((T, C), bf16), jax.ShapeDtypeStruct((8, C), f32), jax.ShapeDtypeStruct((1, C), f32)],
        scratch_shapes=[pltpu.VMEM((tm + hb, C), f32), pltpu.VMEM((hb + tm, C), f32),
                        pltpu.VMEM(_phase_shape(offs_d, tm, C), f32),
                        pltpu.VMEM(_phase_shape(offs_x, tm, C), f32)],
        compiler_params=_cparams(("arbitrary",), VMEM_BIG),
    )(dpre, dpre, xbc, xbc, _sublane_rows(w))


def _conf_conv_bwd(dco, v, w, cv, cg, rider=None):
    T, C = v.shape
    K, hb = CONF_KERNEL, 32
    tm = min(256, T)
    nt = T // tm
    offs_d, offs_x = _bwd_offsets(K), _fwd_offsets(K, hb)

    def body(dcur_ref, dnext_ref, vcur_ref, vprev_ref, w_ref, cv_ref, cg_ref, dcv_ref, dcg_ref, dw_ref, db_ref,
             bufd_ref, bufx_ref, phd_ref, phx_ref):
        i = pl.program_id(0)

        @pl.when(i == 0)
        def _():
            dw_ref[...] = jnp.zeros_like(dw_ref)
            db_ref[...] = jnp.zeros_like(db_ref)

        _fill_bwd_buffers(dcur_ref, dnext_ref, vcur_ref, vprev_ref, bufd_ref, bufx_ref, phd_ref, phx_ref, offs_d, offs_x,
                          hb, tm, i == 0, i == nt - 1)

        def chunk(j, carry):
            r0 = pl.multiple_of(j * CONV_RB, CONV_RB)
            rows = pl.ds(r0, CONV_RB)
            dv = _conv_rows(w_ref, bufd_ref, phd_ref, offs_d, r0, CONV_RB)
            s = _sigmoid(cg_ref[rows, :])
            dcv_ref[rows, :] = (dv * s).astype(bf16)
            dcg_ref[rows, :] = (dv * cv_ref[rows, :] * s * (1.0 - s)).astype(bf16)
            return carry

        lax.fori_loop(0, tm // CONV_RB, chunk, 0)
        _conv_dw(dw_ref, bufd_ref, bufx_ref, phx_ref, offs_x, tm, C)
        db_ref[...] += jnp.sum(dcur_ref[...], axis=0, keepdims=True)

    row = pl.BlockSpec((tm, C), lambda i: (i, 0))
    return _call(
        body, (dco, dco, v, v, _sublane_rows(w), cv, cg), name="conf_conv_bwd", grid=(nt,),
        in_specs=[row, _next_halo_spec(hb, tm, C, T), row, _prev_halo_spec(hb, tm, C), _full((SUBLANES * K, C)), row, row],
        out_specs=[row, row, _full((32, C)), _full((1, C))],
        out_shape=[jax.ShapeDtypeStruct((T, C), bf16), jax.ShapeDtypeStruct((T, C), bf16),
                   jax.ShapeDtypeStruct((32, C), f32), jax.ShapeDtypeStruct((1, C), f32)],
        scratch_shapes=[pltpu.VMEM((tm + hb, C), f32), pltpu.VMEM((hb + tm, C), f32),
                        pltpu.VMEM(_phase_shape(offs_d, tm, C), f32),
                        pltpu.VMEM(_phase_shape(offs_x, tm, C), f32)],
        params=_cparams(("arbitrary",), VMEM_BIG), rider=rider)


def _ssd_bwd(dys, y, z, pre, dtr, sprev, dtb, alog, dskip_e, gn):
    T = pre.shape[0]
    nc = T // CHUNK
    GW = SSD_WIDTH // 2

    def body(dys_ref, y_ref, z_ref, pre_ref, dtr_ref, sp_ref, dtb_ref, alog_ref, de_ref, gn_ref,
             dz_ref, dpre_ref, ddtr_ref, dgn_ref, dd_ref, dal_ref, ddtb_ref, ds_ref):
        @pl.when(pl.program_id(0) == 0)
        def _():
            ds_ref[...] = jnp.zeros_like(ds_ref)
            dgn_ref[...] = jnp.zeros_like(dgn_ref)
            dd_ref[...] = jnp.zeros_like(dd_ref)
            dal_ref[...] = jnp.zeros_like(dal_ref)
            ddtb_ref[...] = jnp.zeros_like(ddtb_ref)

        e = _head_matrix()
        pre = pre_ref[...]
        dtr_b = dtr_ref[...] + dtb_ref[...]
        q = _ssd_chunk_common(pre, dtr_ref[...], dtb_ref[...], alog_ref[...], e)
        cs, tri, xc, xd, xs, dt = q["cs"], q["tri"], q["xc"], q["xd"], q["xs"], q["dt"]
        cs_t = cs.T
        st = sp_ref[0]
        dsn = ds_ref[...]
        lane = lax.broadcasted_iota(jnp.int32, (1, LANES), 1)
        halves = (lane < HEAD_DIM, lane >= HEAD_DIM)
        row_i = lax.broadcasted_iota(jnp.int32, (CHUNK, CHUNK), 0)
        col_i = lax.broadcasted_iota(jnp.int32, (CHUNK, CHUNK), 1)
        tri_t = col_i >= row_i

        y = y_ref[...]
        zz = z_ref[...]
        sz = _sigmoid(zz)
        silu_z = zz * sz
        v = y * silu_z
        dout = dys_ref[...]
        gn_v = gn_ref[...]
        dv, vh = [], []
        for g in range(2):
            vg = _group(v, g, GW)
            rstd = lax.rsqrt(jnp.mean(vg * vg, axis=-1, keepdims=True) + EPS)
            vhg = vg * rstd
            dvh = _group(dout, g, GW) * _group(gn_v, g, GW)
            dv.append(rstd * (dvh - vhg * jnp.mean(dvh * vhg, axis=-1, keepdims=True)))
            vh.append(vhg)
        dv = jnp.concatenate(dv, axis=1)
        dgn_ref[...] += jnp.sum(dout * jnp.concatenate(vh, axis=1), axis=0, keepdims=True)
        dy = dv * silu_z
        dz_ref[...] = (dv * y * (sz * (1.0 + zz * (1.0 - sz)))).astype(bf16)

        dd_row = jnp.sum(dy * xs, axis=0, keepdims=True)
        dd_ref[...] += _contract(jnp.broadcast_to(dd_row, (8, SSD_WIDTH)), e)[0:1, :]
        dxs = dy * de_ref[...]

        dz_in = dy * q["ecs_e"]
        g_mat, gt_mat, dcm, dbm, dsp, dxd, y_off = [], [], [], [], [], [], []
        bgs, cgs = [], []
        for g in range(2):
            bg = _group(q["bm"], g, SSD_STATE)
            cg = _group(q["cm"], g, SSD_STATE)
            bgb, cgb = bg.astype(bf16), cg.astype(bf16)
            bgs.append(bgb)
            cgs.append(cgb)
            stg = _group(st, g, GW).astype(bf16)
            dsng = _group(dsn, g, GW).astype(bf16)
            dzg = _group(dz_in, g, GW).astype(bf16)
            g_mat.append(_dot_nt(cgb, bgb))
            gt_mat.append(_dot_nt(bgb, cgb))
            y_off.append(_dot(cgb, stg))
            dcm.append(_dot_nt(dzg, stg))
            dsp.append(_dot(cg.T.astype(bf16), dzg))
            dbm.append(_dot_nt(_group(xd, g, GW).astype(bf16), dsng))
            dxd.append(_dot(bgb, dsng))
        y_off = jnp.concatenate(y_off, axis=1) * q["ecs_e"]
        dxd = jnp.concatenate(dxd, axis=1)
        ds_ref[...] = dsn * q["cd_e"] + jnp.concatenate(dsp, axis=1)
        dcd_row = jnp.sum(dsn * st, axis=0, keepdims=True) * q["cd_e"]
        t_e = dxd * xd
        dcs = _contract(dy * y_off - t_e, e)
        last_row = _contract(jnp.broadcast_to(dcd_row + jnp.sum(t_e, axis=0, keepdims=True), (8, SSD_WIDTH)), e)[0:1, :]
        dxc_state = dxd * q["dte_e"]

        dg_acc = [jnp.zeros((CHUNK, CHUNK), f32), jnp.zeros((CHUNK, CHUNK), f32)]
        dgt_acc = [jnp.zeros((CHUNK, CHUNK), f32), jnp.zeros((CHUNK, CHUNK), f32)]
        dxc_pairs = []
        for j in range(SSD_HEADS // 2):
            dyp_f = dy[:, j * LANES:(j + 1) * LANES]
            xcp_f = xc[:, j * LANES:(j + 1) * LANES]
            acc = jnp.zeros((CHUNK, LANES), f32)
            for hh in range(2):
                h = 2 * j + hh
                g = h // 8
                dyp = jnp.where(halves[hh], dyp_f, 0.0).astype(bf16)
                xcp = jnp.where(halves[hh], xcp_f, 0.0).astype(bf16)
                lm = jnp.exp(jnp.where(tri, cs[:, h:h + 1] - cs_t[h:h + 1, :], -1e30))
                lm_t = jnp.exp(jnp.where(tri_t, cs_t[h:h + 1, :] - cs[:, h:h + 1], -1e30))
                dm = _dot_nt(dyp, xcp) * lm
                dm_t = _dot_nt(xcp, dyp) * lm_t
                acc = acc + _dot((gt_mat[g] * lm_t).astype(bf16), dyp)
                dg_acc[g] = dg_acc[g] + dm
                dgt_acc[g] = dgt_acc[g] + dm_t
                qd = jnp.sum(dm * g_mat[g] - dm_t * gt_mat[g], axis=1, keepdims=True)
                dcs = dcs + qd * (lane == h).astype(f32)
            dxc_pairs.append(acc)
        dxc = jnp.concatenate(dxc_pairs, axis=1) + dxc_state
        for g in range(2):
            dcm[g] = dcm[g] + _dot(dg_acc[g].astype(bf16), bgs[g])
            dbm[g] = dbm[g] + _dot(dgt_acc[g].astype(bf16), cgs[g])

        dxs = dxs + dxc * q["dt_e"]
        ddt = _contract(dxc * xs, e)
        dcs = dcs + jnp.where(row_i == CHUNK - 1, jnp.broadcast_to(last_row, (CHUNK, LANES)), 0.0)
        da = jnp.dot(tri_t.astype(f32), dcs, precision=lax.Precision.HIGHEST, preferred_element_type=f32)
        ddt = ddt + da * q["a_neg"]
        dal_ref[...] += jnp.sum(da * dt, axis=0, keepdims=True) * q["a_neg"]
        ddtr = ddt * _sigmoid(dtr_b) * (lane < SSD_HEADS).astype(f32)
        ddtb_ref[...] += jnp.sum(ddtr, axis=0, keepdims=True)
        ddtr_ref[...] = ddtr.astype(bf16)

        dact = jnp.concatenate([dxs, dbm[0], dbm[1], dcm[0], dcm[1]], axis=1)
        dpre_ref[...] = dact * _dsilu(pre)

    rev = lambda n: pl.BlockSpec((CHUNK, n), lambda c: (nc - 1 - c, 0))
    vec = _full((1, LANES))
    vshape = jax.ShapeDtypeStruct((1, LANES), f32)
    return _pallas(
        body, name="ssd_bwd", grid=(nc,),
        in_specs=[rev(SSD_WIDTH), rev(SSD_WIDTH), rev(SSD_WIDTH), rev(XBC_WIDTH), rev(LANES),
                  pl.BlockSpec((1, SSD_STATE, SSD_WIDTH), lambda c: (nc - 1 - c, 0, 0)),
                  vec, vec, _full((1, SSD_WIDTH)), _full((1, SSD_WIDTH))],
        out_specs=[rev(SSD_WIDTH), rev(XBC_WIDTH), rev(LANES), _full((1, SSD_WIDTH)), vec, vec, vec],
        out_shape=[jax.ShapeDtypeStruct((T, SSD_WIDTH), bf16), jax.ShapeDtypeStruct((T, XBC_WIDTH), f32),
                   jax.ShapeDtypeStruct((T, LANES), bf16), jax.ShapeDtypeStruct((1, SSD_WIDTH), f32),
                   vshape, vshape, vshape],
        scratch_shapes=[pltpu.VMEM((SSD_STATE, SSD_WIDTH), f32)],
        compiler_params=_cparams(("arbitrary",), VMEM_MID),
    )(dys, y, z, pre, dtr, sprev, dtb, alog, dskip_e, gn)


def _in_proj_bwd(dz, dxbc, dcv, dcg, ddt, gin, x, dh1, g, rider=None):
    T = x.shape[0]
    tm = min(256, T)

    def body(dz_ref, dx_ref, dcv_ref, dcg_ref, ddt_ref, gin_ref, x_ref, dh_ref, g_ref, gx_ref, dg_ref, wt_ref):
        @pl.when(pl.program_id(0) == 0)
        def _():
            dg_ref[...] = jnp.zeros_like(dg_ref)
            _assemble_w_in_t(gin_ref, wt_ref)

        du = (_dot(dz_ref[...], wt_ref[0:O_XBC, :]) + _dot(dx_ref[...], wt_ref[O_XBC:O_DT, :])
              + _dot(dcv_ref[...], wt_ref[O_CV:O_CG, :]) + _dot(dcg_ref[...], wt_ref[O_CG:IN_WIDTH, :])
              + _dot(ddt_ref[...], wt_ref[O_DT:O_DT + LANES, :]))
        dx, dg = _rms_bwd(du, x_ref[...], g_ref[...])
        dg_ref[...] += dg
        gx_ref[...] = dh_ref[...] + dx

    row = lambda n: pl.BlockSpec((tm, n), lambda i: (i, 0))
    return _call(
        body, (dz, dxbc, dcv, dcg, ddt, gin, x, dh1, g), name="in_proj_bwd", grid=(T // tm,),
        in_specs=[row(SSD_WIDTH), row(XBC_WIDTH), row(CONF_WIDTH), row(CONF_WIDTH), row(LANES), _full(gin.shape),
                  row(D_MODEL), row(D_MODEL), _full((1, D_MODEL))],
        out_specs=[row(D_MODEL), _full((1, D_MODEL))],
        out_shape=[jax.ShapeDtypeStruct((T, D_MODEL), f32), jax.ShapeDtypeStruct((1, D_MODEL), f32)],
        scratch_shapes=[pltpu.VMEM((IN_WIDTH, D_MODEL), bf16)],
        params=_cparams(("arbitrary",), VMEM_BIG), rider=rider)


def _weight_grad(a, g, name, square=False, slab=None, place=None, tk=512):
    T, K = a.shape
    N = g.shape[1]
    tk = min(tk, K)
    tn = 1024 if N % 1024 == 0 else min(512, N)
    tt = min(2048, T)

    def body(a_ref, g_ref, *rest):
        o_ref = rest[-1]
        acc = _dot_tn(_operand(a_ref[...]), g_ref[...].astype(bf16))
        t = pl.program_id(2)
        shaped = acc if slab is None else acc[None]

        @pl.when(t == 0)
        def _():
            o_ref[...] = shaped

        @pl.when(t > 0)
        def _():
            o_ref[...] += shaped

    def _operand(av):
        if square:
            av = av.astype(f32)
            av = av * av
        return av.astype(bf16)

    in_specs = [pl.BlockSpec((tt, tk), lambda i, j, t: (t, i)), pl.BlockSpec((tt, tn), lambda i, j, t: (t, j))]
    grid = (K // tk, N // tn, T // tt)
    params = _cparams(("parallel", "parallel", "arbitrary"), VMEM_MID)
    if slab is None:
        return _pallas(
            body, pin_bytes=PIN_SMALL, name=name, grid=grid, in_specs=in_specs,
            out_specs=pl.BlockSpec((tk, tn), lambda i, j, t: (i, j)),
            out_shape=jax.ShapeDtypeStruct((K, N), f32), compiler_params=params,
        )(a, g)
    return _pallas(
        body, pin_bytes=PIN_SMALL, name=name, grid=grid, in_specs=in_specs + [ANY],
        out_specs=pl.BlockSpec((1, tk, tn), lambda i, j, t: place(i, j)),
        out_shape=jax.ShapeDtypeStruct(slab.shape, f32), input_output_aliases={2: 0}, compiler_params=params,
    )(a, g, slab)


def _place():
    return lax.axis_index("x"), lax.axis_index("y"), lax.axis_index("c")


def _other_chips(x, y):
    return [(1 - x, y), (x, 1 - y), (1 - x, 1 - y)]


def _remote(src, dst, ssem, rsem, dev):
    return pltpu.make_async_remote_copy(src_ref=src, dst_ref=dst, send_sem=ssem, recv_sem=rsem, device_id=dev,
                                        device_id_type=MESH)


def _gather_weights(arrays, convw):
    n = len(arrays)
    halves = tuple(a.shape[1] // 2 for a in arrays)

    def body(*refs):
        cw_ref, cwo_ref = refs[n], refs[2 * n + 1]
        ssem, rsem, lsem = refs[2 * n + 2:]
        triples = tuple(zip(refs[:n], refs[n + 1:2 * n + 1], halves))
        x, y, c = _place()
        me_b = 2 * x + y
        sib = (x, y, 1 - c)
        chips = _other_chips(x, y)
        loc = pltpu.make_async_copy(cw_ref, cwo_ref.at[me_b], lsem)
        loc.start()
        sends = []
        for j, (src, dst, h) in enumerate(triples):
            mine = pl.ds(c * h, h)
            for k, (px, py) in enumerate(chips):
                s = 6 * j + k
                sends.append(_remote(src.at[me_b, mine], dst.at[me_b, mine], ssem.at[s], rsem.at[s], (px, py, c)))
        for k, (px, py) in enumerate(chips):
            sends.append(_remote(cw_ref, cwo_ref.at[me_b], ssem.at[6 * n + k], rsem.at[6 * n + k], (px, py, c)))
        for cp in sends:
            cp.start()
        for j, (src, dst, h) in enumerate(triples):
            mine = pl.ds(c * h, h)
            for k, (px, py) in enumerate(chips):
                b = 2 * px + py
                s = 6 * j + k
                _remote(src.at[b, mine], dst.at[b, mine], ssem.at[s], rsem.at[s], (px, py, c)).wait_recv()
                fw = _remote(dst.at[b, mine], dst.at[b, mine], ssem.at[s + 3], rsem.at[s + 3], sib)
                fw.start()
                sends.append(fw)
        for k, (px, py) in enumerate(chips):
            b = 2 * px + py
            _remote(cw_ref, cwo_ref.at[b], ssem.at[6 * n + k], rsem.at[6 * n + k], (px, py, c)).wait_recv()
        for j, (src, dst, h) in enumerate(triples):
            theirs = pl.ds((1 - c) * h, h)
            for k, (px, py) in enumerate(chips):
                b = 2 * px + py
                s = 6 * j + k + 3
                _remote(src.at[b, theirs], dst.at[b, theirs], ssem.at[s], rsem.at[s], sib).wait_recv()
        for cp in sends:
            cp.wait_send()
        loc.wait()

    return _pallas(
        body, name="gather_weights", in_specs=[ANY] * (n + 1), out_specs=[ANY] * (n + 1),
        out_shape=[jax.ShapeDtypeStruct(a.shape, bf16) for a in arrays]
        + [jax.ShapeDtypeStruct((N_CHIPS, CONVW_ROWS, D_MODEL), f32)],
        input_output_aliases={j: j for j in range(n)},
        scratch_shapes=[pltpu.SemaphoreType.DMA((6 * n + 3,)), pltpu.SemaphoreType.DMA((6 * n + 3,)),
                        pltpu.SemaphoreType.DMA(())],
    )(*arrays, convw)


def _gather_rider(gath0, lo, n):
    h = gath0.shape[1] // 2

    def copies(rins, routs, ssem, rsem, sending):
        (g_ref,), (o_ref,) = rins, routs
        x, y, c = _place()
        mine = pl.ds(c * h + lo, n)
        for k, (px, py) in enumerate(_other_chips(x, y)):
            b = 2 * x + y if sending else 2 * px + py
            yield _remote(g_ref.at[b, mine], o_ref.at[b, mine], ssem.at[k], rsem.at[k], (px, py, c))

    def start(*refs):
        for cp in copies(*refs, sending=True):
            cp.start()

    def finish(*refs):
        for cp in copies(*refs, sending=False):
            cp.wait()

    return _Rider([gath0], [jax.ShapeDtypeStruct(gath0.shape, gath0.dtype)], {0: 0}, 3, start, finish)


def _forward_to_sibling(gath):
    h = gath.shape[1] // 2

    def body(g_ref, o_ref, ssem, rsem):
        x, y, c = _place()
        sib = (x, y, 1 - c)
        mine, theirs = pl.ds(c * h, h), pl.ds((1 - c) * h, h)
        blocks = [2 * px + py for px, py in _other_chips(x, y)]
        sends = [_remote(g_ref.at[b, mine], o_ref.at[b, mine], ssem.at[k], rsem.at[k], sib) for k, b in enumerate(blocks)]
        for cp in sends:
            cp.start()
        for k, b in enumerate(blocks):
            _remote(g_ref.at[b, theirs], o_ref.at[b, theirs], ssem.at[k], rsem.at[k], sib).wait_recv()
        for cp in sends:
            cp.wait_send()

    return _pallas(
        body, name="forward_to_sibling", in_specs=[ANY], out_specs=ANY,
        out_shape=jax.ShapeDtypeStruct(gath.shape, gath.dtype), input_output_aliases={0: 0},
        scratch_shapes=[pltpu.SemaphoreType.DMA((3,)), pltpu.SemaphoreType.DMA((3,))],
    )(gath)


def _swap_copy(g_ref, r_ref, ssem, rsem):
    x, y, c = _place()
    h = r_ref.shape[1]
    return _remote(g_ref.at[:, pl.ds((1 - c) * h, h), :], r_ref, ssem.at[0], rsem.at[0], (x, y, 1 - c))


def _swap_rider(g):
    def start(rins, routs, ssem, rsem):
        _swap_copy(rins[0], routs[0], ssem, rsem).start()

    def finish(rins, routs, ssem, rsem):
        _swap_copy(rins[0], routs[0], ssem, rsem).wait()

    return _Rider([g], [jax.ShapeDtypeStruct((N_CHIPS, g.shape[1] // 2, g.shape[2]), g.dtype)], {}, 1, start, finish)


def _swap_halves(g):
    def body(g_ref, r_ref, ssem, rsem):
        cp = _swap_copy(g_ref, r_ref, ssem, rsem)
        cp.start()
        cp.wait()

    return _pallas(
        body, name="swap_halves", in_specs=[ANY], out_specs=ANY,
        out_shape=jax.ShapeDtypeStruct((N_CHIPS, g.shape[1] // 2, g.shape[2]), g.dtype),
        scratch_shapes=[pltpu.SemaphoreType.DMA((1,)), pltpu.SemaphoreType.DMA((1,))],
    )(g)


def _chip_sum(cidx, gslab, recv, name):
    half, C = recv.shape[1:]
    tr = half // 2 if (half // 2) % 16 == 0 else half

    def body(c_ref, g_ref, r_ref, o_ref):
        o_ref[...] = (g_ref[...] + r_ref[...]).astype(bf16)

    return _pallas(
        body, name=name,
        grid_spec=pltpu.PrefetchScalarGridSpec(
            num_scalar_prefetch=1, grid=(N_CHIPS, half // tr),
            in_specs=[pl.BlockSpec((1, tr, C), lambda b, i, c_ref: (b, c_ref[0] * (half // tr) + i, 0)),
                      pl.BlockSpec((1, tr, C), lambda b, i, c_ref: (b, i, 0))],
            out_specs=pl.BlockSpec((1, tr, C), lambda b, i, c_ref: (b, i, 0))),
        out_shape=jax.ShapeDtypeStruct((N_CHIPS, half, C), bf16),
        compiler_params=_cparams(("parallel", "parallel"), VMEM_MID),
    )(cidx, gslab, recv)


def _exchange_rider(h):
    def copies(rins, routs, ssem, rsem):
        x, y, c = _place()
        for k, (px, py) in enumerate(_other_chips(x, y)):
            yield _remote(rins[0].at[2 * px + py], routs[0].at[k], ssem.at[k], rsem.at[k], (px, py, c))

    def start(*refs):
        for cp in copies(*refs):
            cp.start()

    def finish(*refs):
        for cp in copies(*refs):
            cp.wait()

    return _Rider([h], [jax.ShapeDtypeStruct((3,) + h.shape[1:], h.dtype)], {}, 3, start, finish)


def _gather_small(small):
    def body(sm_ref, all_ref, ssem, rsem, lsem):
        x, y, c = _place()
        me = 4 * x + 2 * y + c
        loc = pltpu.make_async_copy(sm_ref, all_ref.at[me], lsem)
        loc.start()
        sends, peers = [], []
        for r in range(1, N_DEV):
            peer = ((1 - x) if r & 4 else x, (1 - y) if r & 2 else y, (1 - c) if r & 1 else c)
            peers.append(peer)
            sends.append(_remote(sm_ref, all_ref.at[me], ssem.at[r - 1], rsem.at[r - 1], peer))
        for cp in sends:
            cp.start()
        for r, peer in zip(range(1, N_DEV), peers):
            pid = 4 * peer[0] + 2 * peer[1] + peer[2]
            _remote(sm_ref, all_ref.at[pid], ssem.at[r - 1], rsem.at[r - 1], peer).wait_recv()
        for cp in sends:
            cp.wait_send()
        loc.wait()

    return _pallas(
        body, name="gather_small", in_specs=[ANY], out_specs=ANY,
        out_shape=jax.ShapeDtypeStruct((N_DEV, SMALL_ROWS, D_MODEL), f32),
        scratch_shapes=[pltpu.SemaphoreType.DMA((7,)), pltpu.SemaphoreType.DMA((7,)), pltpu.SemaphoreType.DMA(())],
    )(small)


def _final_sum(idx, gslab, recv_sib, recv_ici, name):
    half, C = recv_sib.shape[1:]
    tr = half // 2 if (half // 2) % 16 == 0 else half

    def body(i_ref, g_ref, r_ref, p_ref, o_ref):
        acc = g_ref[0] + r_ref[0]
        for k in range(3):
            acc = acc + p_ref[k].astype(f32)
        o_ref[...] = acc

    return _pallas(
        body, name=name,
        grid_spec=pltpu.PrefetchScalarGridSpec(
            num_scalar_prefetch=1, grid=(half // tr,),
            in_specs=[pl.BlockSpec((1, tr, C), lambda i, s: (s[1], s[0] * (half // tr) + i, 0)),
                      pl.BlockSpec((1, tr, C), lambda i, s: (s[1], i, 0)),
                      pl.BlockSpec((3, tr, C), lambda i, s: (0, i, 0))],
            out_specs=pl.BlockSpec((tr, C), lambda i, s: (s[0] * (half // tr) + i, 0))),
        out_shape=jax.ShapeDtypeStruct((2 * half, C), f32),
        compiler_params=_cparams(("parallel",), VMEM_MID),
    )(idx, gslab, recv_sib, recv_ici)


def _join_halves(ra, rb):
    ha, hb = ra.shape[0] // 2, rb.shape[0] // 2

    def body(a_ref, b_ref, ao_ref, bo_ref, ssem, rsem):
        x, y, c = _place()
        sib = (x, y, 1 - c)
        mine_a, theirs_a = pl.ds(c * ha, ha), pl.ds((1 - c) * ha, ha)
        mine_b, theirs_b = pl.ds(c * hb, hb), pl.ds((1 - c) * hb, hb)
        ca = _remote(a_ref.at[mine_a], ao_ref.at[mine_a], ssem.at[0], rsem.at[0], sib)
        cb = _remote(b_ref.at[mine_b], bo_ref.at[mine_b], ssem.at[1], rsem.at[1], sib)
        ca.start()
        cb.start()
        _remote(a_ref.at[theirs_a], ao_ref.at[theirs_a], ssem.at[0], rsem.at[0], sib).wait_recv()
        _remote(b_ref.at[theirs_b], bo_ref.at[theirs_b], ssem.at[1], rsem.at[1], sib).wait_recv()
        ca.wait_send()
        cb.wait_send()

    return _pallas(
        body, name="join_halves", in_specs=[ANY, ANY], out_specs=[ANY, ANY],
        out_shape=[jax.ShapeDtypeStruct(ra.shape, f32), jax.ShapeDtypeStruct(rb.shape, f32)],
        input_output_aliases={0: 0, 1: 1},
        scratch_shapes=[pltpu.SemaphoreType.DMA((2,)), pltpu.SemaphoreType.DMA((2,))],
    )(ra, rb)


def _shard_rows(gt):
    def body(g_ref, o_ref):
        for b in range(N_CHIPS):
            o_ref[b, 0:W_IN_ROWS, :] = g_ref[b * W_IN_ROWS:(b + 1) * W_IN_ROWS, :]
            o_ref[b, W_IN_ROWS:W_IN_ROWS_PAD, :] = jnp.zeros((W_IN_ROWS_PAD - W_IN_ROWS, LANES), f32)

    return _pallas(
        body, name="shard_rows", grid=(D_MODEL // LANES,),
        in_specs=[pl.BlockSpec((IN_WIDTH, LANES), lambda i: (0, i))],
        out_specs=pl.BlockSpec((N_CHIPS, W_IN_ROWS_PAD, LANES), lambda i: (0, 0, i)),
        out_shape=jax.ShapeDtypeStruct((N_CHIPS, W_IN_ROWS_PAD, D_MODEL), f32),
        compiler_params=_cparams(("parallel",), VMEM_MID),
    )(gt)


def _sum_small(all_small):
    def body(a_ref, o_ref):
        acc = a_ref[0]
        for d in range(1, N_DEV):
            acc = acc + a_ref[d]
        o_ref[...] = acc

    return _pallas(
        body, name="sum_small", out_shape=jax.ShapeDtypeStruct((SMALL_ROWS, D_MODEL), f32),
    )(all_small)


def _adamw(w, g, m, v, name, g_off=0, by_columns=False):
    R, C = w.shape
    tr = 256 if R % 256 == 0 else R
    assert g_off % tr == 0 and not (by_columns and g_off)
    c1 = 1.0 - ADAM_B1 ** ADAM_STEP
    c2 = 1.0 - ADAM_B2 ** ADAM_STEP

    def body(w_ref, g_ref, m_ref, v_ref, d_ref, mo_ref, vo_ref):
        gg = g_ref[...]
        m2 = ADAM_B1 * m_ref[...] + (1.0 - ADAM_B1) * gg
        v2 = ADAM_B2 * v_ref[...] + (1.0 - ADAM_B2) * (gg * gg)
        mo_ref[...] = m2
        vo_ref[...] = v2
        d_ref[...] = -ADAM_LR * ((m2 / c1) / (jnp.sqrt(v2 / c2) + ADAM_EPS) + ADAM_WD * w_ref[...])

    if by_columns:
        blk = gblk = pl.BlockSpec((R, LANES), lambda i: (0, i))
        grid = (C // LANES,)
    else:
        blk = pl.BlockSpec((tr, C), lambda i: (i, 0))
        gblk = pl.BlockSpec((tr, C), lambda i: (g_off // tr + i, 0))
        grid = (R // tr,)
    shp = jax.ShapeDtypeStruct((R, C), f32)
    return _pallas(
        body, pin_bytes=PIN_SMALL, name=name, grid=grid, in_specs=[blk, gblk, blk, blk], out_specs=[blk] * 3,
        out_shape=[shp] * 3,
        compiler_params=_cparams(("parallel",), VMEM_MID),
    )(w, g, m, v)


def _pad_lanes(v):
    return jnp.pad(v, ((0, 0), (0, LANES - v.shape[1])))


def _local_step(x, p, tgt, gath0, cidx, gin, S):
    dtb = _pad_lanes(S["dt_bias"])
    alog = _pad_lanes(S["A_log"])
    dskip_e = jnp.repeat(S["D_skip"], HEAD_DIM, axis=1)

    early = GATHER_EARLY_ROWS
    u0, z, xbc, cv, cg, dtr, v, gath1 = _in_proj_fwd(x, S["mix_norm_g"], gin, rider=_gather_rider(gath0, 0, early))
    co, yc, gath = _conf_fwd(v, S["conf_dw_w"], S["conf_dw_b"], S["conf_ln_g"], S["conf_ln_b"],
                             rider=_gather_rider(gath1, early, SLAB_A // 2 - early))
    gath = _forward_to_sibling(gath)
    w_ple = jnp.concatenate([_ple_of_slab(gath[b]) for b in range(N_CHIPS)], axis=1)
    pre = _ssd_conv_fwd(xbc, S["ssd_conv_w"], S["ssd_conv_b"])
    y, ys, sprev = _ssd_fwd(pre, dtr, z, dtb, alog, dskip_e, S["ssd_norm_g"])
    h1, u1 = _out_proj_fwd(x, ys, yc, gath, S["mlp_norm_g"])
    r, h2, u2 = _mlp_fwd(h1, u1, gath, S["ple_gate_norm_g"])
    loss, dh2, dh2b, dgp, dep, dg_fin, dg_ple, db_pg, dg_pg = _ple_loss(
        h2, u2, p, tgt, gath, S["b_ple_gate"], w_ple, S["ple_norm_g"], S["final_norm_g"], S["ple_gate_norm_g"])

    npg = D_MODEL // N_CHIPS
    ga = lax.empty((N_CHIPS, SLAB_A, D_MODEL), f32)
    ga = _weight_grad(u2, dgp, "dw_ple_gate", slab=ga, tk=npg, place=lambda i, j: (i, PG_OFF // npg, j))
    ga = _weight_grad(r, dh2b, "dw_down", square=True, slab=ga, place=lambda i, j: (i // 2, DOWN_OFF // 512 + i % 2, j))
    gw_ple = _weight_grad(p, dep, "dw_ple")
    dhp, dh1, dh1b, dg_mlp = _mlp_bwd(dh2, r, gath, h1, S["mlp_norm_g"])
    ga = _weight_grad(u1, dhp, "dw_up", slab=ga, place=lambda i, j: (j, UP_OFF // 512 + i, 0))
    ga = _weight_grad(ys, dh1b, "dw_out_ssd", slab=ga, place=lambda i, j: (i, OUT_OFF // 512, j))
    ga = _weight_grad(yc, dh1b, "dw_out_conf", slab=ga, place=lambda i, j: (2 + i, OUT_OFF // 512, j))
    n_ple = D_MODEL // N_CHIPS
    ple_rows = jnp.stack([_rows(gw_ple[:, b * n_ple:(b + 1) * n_ple]) for b in range(N_CHIPS)], axis=0)
    ga = lax.dynamic_update_slice(ga, ple_rows, (0, PLE_OFF, 0))
    dys, dco, dg_ln, db_ln, recv_a = _out_proj_bwd(dh1, gath, co, S["conf_ln_g"], S["conf_ln_b"], rider=_swap_rider(ga))
    ha = _chip_sum(cidx, ga, recv_a, "chip_sum_a")
    dcv, dcg, dw_conf, db_conf, ici_a = _conf_conv_bwd(dco, v, S["conf_dw_w"], cv, cg, rider=_exchange_rider(ha))
    dz, dpre, ddtr, dg_ssdn, dd, dal, ddtb = _ssd_bwd(dys, y, z, pre, dtr, sprev, dtb, alog, dskip_e, S["ssd_norm_g"])
    dxbc, dw_sconv, db_sconv = _ssd_conv_bwd(dpre, xbc, S["ssd_conv_w"])
    gw_in = jnp.concatenate([
        _weight_grad(dz, u0, "dw_in_z"), _weight_grad(dxbc, u0, "dw_in_xbc"),
        _weight_grad(ddtr, u0, "dw_in_dt")[:SSD_HEADS],
        _weight_grad(dcv, u0, "dw_in_cv"), _weight_grad(dcg, u0, "dw_in_cg")], axis=0)
    gb = _shard_rows(gw_in)
    recv_b = _swap_halves(gb)
    hb = _chip_sum(cidx, gb, recv_b, "chip_sum_b")
    gx, dg_mix, ici_b = _in_proj_bwd(dz, dxbc, dcv, dcg, ddtr, gin, x, dh1, S["mix_norm_g"], rider=_exchange_rider(hb))
    small = {
        "mix_norm_g": dg_mix, "ssd_conv_w": dw_sconv, "ssd_conv_b": db_sconv, "dt_bias": ddtb, "A_log": dal, "D_skip": dd,
        "ssd_norm_g": dg_ssdn, "conf_dw_w": dw_conf, "conf_dw_b": db_conf, "conf_ln_g": dg_ln, "conf_ln_b": db_ln,
        "mlp_norm_g": dg_mlp, "ple_gate_norm_g": dg_pg, "b_ple_gate": db_pg, "ple_norm_g": dg_ple,
        "final_norm_g": dg_fin, "loss": loss,
    }
    return gx, (ga, recv_a, ici_a), (gb, recv_b, ici_b), small


def _rows(a):
    return a.reshape(-1, D_MODEL)


def _pad_rows(a, n):
    flat = a.reshape(-1)
    return jnp.pad(flat, (0, n * D_MODEL - flat.shape[0])).reshape(n, D_MODEL)


def _ple_of_slab(slab):
    return slab[PLE_OFF:PLE_OFF + PLE_ROWS].reshape(PLE_DIM, D_MODEL // N_CHIPS)


ROW_VEC = {"mix_norm_g": 0, "ssd_norm_g": 1, "conf_dw_b": 2, "conf_ln_g": 3, "conf_ln_b": 4, "mlp_norm_g": 5,
           "ple_gate_norm_g": 6, "b_ple_gate": 7, "ple_norm_g": 8, "final_norm_g": 9}
ROW_CONV_B = 10
ROW_HEADS = 12
ROW_CONV_W = 16
ROW_DW = 24
HEAD_LANES = {"dt_bias": 0, "A_log": 1, "D_skip": 2, "loss": 3}
SMALL_ORDER = ("mix_norm_g", "ssd_conv_w", "ssd_conv_b", "dt_bias", "A_log", "D_skip", "ssd_norm_g", "conf_dw_w",
               "conf_dw_b", "conf_ln_g", "conf_ln_b", "mlp_norm_g", "ple_gate_norm_g", "b_ple_gate", "ple_norm_g",
               "final_norm_g")
SPLIT = XBC_WIDTH - D_MODEL


def _pack_small(raw):
    names = list(ROW_VEC) + ["ssd_conv_b", "dt_bias", "A_log", "D_skip", "loss", "ssd_conv_w", "conf_dw_w"]

    def body(*refs):
        r = dict(zip(names, refs[:-1]))
        o_ref = refs[-1]
        o_ref[...] = jnp.zeros_like(o_ref)
        for n, row in ROW_VEC.items():
            o_ref[row:row + 1, :] = r[n][...]
        o_ref[ROW_CONV_B:ROW_CONV_B + 1, :] = r["ssd_conv_b"][:, 0:D_MODEL]
        o_ref[ROW_CONV_B + 1:ROW_CONV_B + 2, 0:SPLIT] = r["ssd_conv_b"][:, D_MODEL:]
        for n, j in HEAD_LANES.items():
            o_ref[ROW_HEADS:ROW_HEADS + 1, j * LANES:(j + 1) * LANES] = r[n][0:1, :]
        for k in range(SSD_CONV):
            o_ref[ROW_CONV_W + 2 * k:ROW_CONV_W + 2 * k + 1, :] = r["ssd_conv_w"][k:k + 1, 0:D_MODEL]
            o_ref[ROW_CONV_W + 2 * k + 1:ROW_CONV_W + 2 * k + 2, 0:SPLIT] = r["ssd_conv_w"][k:k + 1, D_MODEL:]
        o_ref[ROW_DW:ROW_DW + 32, :] = r["conf_dw_w"][...]

    return _pallas(
        body, name="pack_small", out_shape=jax.ShapeDtypeStruct((SMALL_ROWS, D_MODEL), f32),
    )(*[raw[n] for n in names])


def _adamw_small(cidx, tot, w, m, v):
    c1 = 1.0 - ADAM_B1 ** ADAM_STEP
    c2 = 1.0 - ADAM_B2 ** ADAM_STEP
    n_par = len(SMALL_ORDER)

    def shard(full, chip, width):
        out = full[:, 0:width]
        for b in range(1, N_CHIPS):
            out = jnp.where(chip == b, full[:, b * width:(b + 1) * width], out)
        return out

    def grad_of(n, t_ref, chip):
        if n in ROW_VEC:
            return t_ref[ROW_VEC[n]:ROW_VEC[n] + 1, :]
        if n == "ssd_conv_b":
            return jnp.concatenate([t_ref[ROW_CONV_B:ROW_CONV_B + 1, :], t_ref[ROW_CONV_B + 1:ROW_CONV_B + 2, 0:SPLIT]], axis=1)
        if n in HEAD_LANES:
            j = HEAD_LANES[n]
            return t_ref[ROW_HEADS:ROW_HEADS + 1, j * LANES:j * LANES + SSD_HEADS]
        if n == "ssd_conv_w":
            rows = [jnp.concatenate([t_ref[ROW_CONV_W + 2 * k:ROW_CONV_W + 2 * k + 1, :],
                                     t_ref[ROW_CONV_W + 2 * k + 1:ROW_CONV_W + 2 * k + 2, 0:SPLIT]], axis=1)
                    for k in range(SSD_CONV)]
            return shard(jnp.concatenate(rows, axis=0), chip, XBC_WIDTH // N_CHIPS)
        return shard(t_ref[ROW_DW:ROW_DW + CONF_KERNEL, :], chip, CONF_WIDTH // N_CHIPS)

    def body(c_ref, t_ref, *refs):
        ins, outs = refs[:3 * n_par], refs[3 * n_par:]
        chip = c_ref[1]
        for i, n in enumerate(SMALL_ORDER):
            w_ref, m_ref, v_ref = ins[3 * i:3 * i + 3]
            g_ref, d_ref, mo_ref, vo_ref = outs[4 * i:4 * i + 4]
            g = grad_of(n, t_ref, chip)
            m2 = ADAM_B1 * m_ref[...] + (1.0 - ADAM_B1) * g
            v2 = ADAM_B2 * v_ref[...] + (1.0 - ADAM_B2) * (g * g)
            g_ref[...] = g
            mo_ref[...] = m2
            vo_ref[...] = v2
            d_ref[...] = -ADAM_LR * ((m2 / c1) / (jnp.sqrt(v2 / c2) + ADAM_EPS) + ADAM_WD * w_ref[...])

    args, in_specs, out_specs, out_shape = [], [], [], []
    for n in SMALL_ORDER:
        shp = w[n].shape
        spec = pl.BlockSpec(shp, lambda i, c_ref: (0, 0))
        args += [w[n], m[n], v[n]]
        in_specs += [spec] * 3
        out_specs += [spec] * 4
        out_shape += [jax.ShapeDtypeStruct(shp, f32)] * 4
    outs = _pallas(
        body, name="adamw_small",
        grid_spec=pltpu.PrefetchScalarGridSpec(
            num_scalar_prefetch=1, grid=(1,),
            in_specs=[pl.BlockSpec(tot.shape, lambda i, c_ref: (0, 0))] + in_specs, out_specs=out_specs),
        out_shape=out_shape,
    )(cidx, tot, *args)
    grad, delta, new_m, new_v = {}, {}, {}, {}
    for i, n in enumerate(SMALL_ORDER):
        grad[n], delta[n], new_m[n], new_v[n] = outs[4 * i:4 * i + 4]
    return grad, delta, new_m, new_v


BIG = ("w_in", "w_out", "w_up", "w_down", "w_ple_gate", "w_ple")
BIG_A = (("w_up", UP_OFF), ("w_down", DOWN_OFF), ("w_out", OUT_OFF), ("w_ple_gate", PG_OFF))
WEIGHTS = ("mix_norm_g", "w_in", "ssd_conv_w", "ssd_conv_b", "dt_bias", "A_log", "D_skip", "ssd_norm_g", "conf_dw_w",
           "conf_dw_b", "conf_ln_g", "conf_ln_b", "w_out", "mlp_norm_g", "w_up", "w_down", "ple_gate_norm_g",
           "w_ple_gate", "b_ple_gate", "w_ple", "ple_norm_g", "final_norm_g")


def kernel(x, p, mix_norm_g, w_in, ssd_conv_w, ssd_conv_b, dt_bias, A_log, D_skip, ssd_norm_g, conf_dw_w, conf_dw_b, conf_ln_g, conf_ln_b, w_out, mlp_norm_g, w_up, w_down, ple_gate_norm_g, w_ple_gate, b_ple_gate, w_ple, ple_norm_g, final_norm_g, loss_target, m_mix_norm_g, m_w_in, m_ssd_conv_w, m_ssd_conv_b, m_dt_bias, m_A_log, m_D_skip, m_ssd_norm_g, m_conf_dw_w, m_conf_dw_b, m_conf_ln_g, m_conf_ln_b, m_w_out, m_mlp_norm_g, m_w_up, m_w_down, m_ple_gate_norm_g, m_w_ple_gate, m_b_ple_gate, m_w_ple, m_ple_norm_g, m_final_norm_g, v_mix_norm_g, v_w_in, v_ssd_conv_w, v_ssd_conv_b, v_dt_bias, v_A_log, v_D_skip, v_ssd_norm_g, v_conf_dw_w, v_conf_dw_b, v_conf_ln_g, v_conf_ln_b, v_w_out, v_mlp_norm_g, v_w_up, v_w_down, v_ple_gate_norm_g, v_w_ple_gate, v_b_ple_gate, v_w_ple, v_ple_norm_g, v_final_norm_g):
    w = dict(mix_norm_g=mix_norm_g, w_in=w_in, ssd_conv_w=ssd_conv_w, ssd_conv_b=ssd_conv_b, dt_bias=dt_bias, A_log=A_log,
             D_skip=D_skip, ssd_norm_g=ssd_norm_g, conf_dw_w=conf_dw_w, conf_dw_b=conf_dw_b, conf_ln_g=conf_ln_g,
             conf_ln_b=conf_ln_b, w_out=w_out, mlp_norm_g=mlp_norm_g, w_up=w_up, w_down=w_down,
             ple_gate_norm_g=ple_gate_norm_g, w_ple_gate=w_ple_gate, b_ple_gate=b_ple_gate, w_ple=w_ple,
             ple_norm_g=ple_norm_g, final_norm_g=final_norm_g)
    m = dict(mix_norm_g=m_mix_norm_g, w_in=m_w_in, ssd_conv_w=m_ssd_conv_w, ssd_conv_b=m_ssd_conv_b, dt_bias=m_dt_bias,
             A_log=m_A_log, D_skip=m_D_skip, ssd_norm_g=m_ssd_norm_g, conf_dw_w=m_conf_dw_w, conf_dw_b=m_conf_dw_b,
             conf_ln_g=m_conf_ln_g, conf_ln_b=m_conf_ln_b, w_out=m_w_out, mlp_norm_g=m_mlp_norm_g, w_up=m_w_up,
             w_down=m_w_down, ple_gate_norm_g=m_ple_gate_norm_g, w_ple_gate=m_w_ple_gate, b_ple_gate=m_b_ple_gate,
             w_ple=m_w_ple, ple_norm_g=m_ple_norm_g, final_norm_g=m_final_norm_g)
    v = dict(mix_norm_g=v_mix_norm_g, w_in=v_w_in, ssd_conv_w=v_ssd_conv_w, ssd_conv_b=v_ssd_conv_b, dt_bias=v_dt_bias,
             A_log=v_A_log, D_skip=v_D_skip, ssd_norm_g=v_ssd_norm_g, conf_dw_w=v_conf_dw_w, conf_dw_b=v_conf_dw_b,
             conf_ln_g=v_conf_ln_g, conf_ln_b=v_conf_ln_b, w_out=v_w_out, mlp_norm_g=v_mlp_norm_g, w_up=v_w_up,
             w_down=v_w_down, ple_gate_norm_g=v_ple_gate_norm_g, w_ple_gate=v_w_ple_gate, b_ple_gate=v_b_ple_gate,
             w_ple=v_w_ple, ple_norm_g=v_ple_norm_g, final_norm_g=v_final_norm_g)
    xi, yi, ci = lax.axis_index("x"), lax.axis_index("y"), lax.axis_index("c")
    chip = 2 * xi + yi

    slab = jnp.concatenate([w_up[0], w_down[0], w_out[0], w_ple_gate[0], _rows(w_ple[0])], axis=0).astype(bf16)
    gath0 = lax.dynamic_update_slice(lax.empty((N_CHIPS, SLAB_A, D_MODEL), bf16), slab[None], (chip, 0, 0))
    wt_shard = jnp.swapaxes(w_in, 1, 2).astype(bf16)
    wt_shard = jnp.pad(wt_shard, ((0, 0), (0, W_IN_ROWS_PAD - W_IN_ROWS), (0, 0)))
    gin0 = lax.dynamic_update_slice(lax.empty((N_CHIPS, W_IN_ROWS_PAD, D_MODEL), bf16), wt_shard, (chip, 0, 0))
    convw = _pad_rows(jnp.concatenate([ssd_conv_w[0].reshape(-1), conf_dw_w[0].reshape(-1)]), CONVW_ROWS)
    gin, cwg = _gather_weights([gin0], convw)
    n_sc = SSD_CONV * (XBC_WIDTH // N_CHIPS)
    n_cf = CONF_KERNEL * (CONF_WIDTH // N_CHIPS)
    S = {n: w[n][0] for n in ("mix_norm_g", "ssd_conv_b", "dt_bias", "A_log", "D_skip", "ssd_norm_g", "conf_dw_b",
                              "conf_ln_g", "conf_ln_b", "mlp_norm_g", "ple_gate_norm_g", "b_ple_gate", "ple_norm_g")}
    S = {n: a.reshape(1, -1) for n, a in S.items()}
    S["final_norm_g"] = final_norm_g.reshape(1, -1)
    S["ssd_conv_w"] = jnp.concatenate(
        [cwg[b].reshape(-1)[:n_sc].reshape(SSD_CONV, XBC_WIDTH // N_CHIPS) for b in range(N_CHIPS)], axis=1)
    S["conf_dw_w"] = jnp.concatenate(
        [cwg[b].reshape(-1)[n_sc:n_sc + n_cf].reshape(CONF_KERNEL, CONF_WIDTH // N_CHIPS) for b in range(N_CHIPS)], axis=1)

    cidx = jnp.stack([ci, chip]).astype(jnp.int32)
    grad_x, (ga, recv_a, ici_a), (gb, recv_b, ici_b), gsmall = _local_step(
        x[0], p[0, 0], loss_target[0], gath0, cidx, gin, S)

    all_small = _gather_small(_pack_small(gsmall))
    ra = _final_sum(cidx, ga, recv_a, ici_a, "final_sum_a")
    rb = _final_sum(cidx, gb, recv_b, ici_b, "final_sum_b")
    ra, rb = _join_halves(ra, rb)
    tot_small = _sum_small(all_small)

    loss = tot_small[ROW_HEADS, HEAD_LANES["loss"] * LANES]

    two_d = lambda a: a.reshape(a.shape[-2:]) if a.ndim > 1 else a.reshape(1, -1)
    small_w, small_m, small_v = ({n: two_d(d[n]) for n in SMALL_ORDER} for d in (w, m, v))
    grads, delta, new_m, new_v = _adamw_small(cidx, tot_small, small_w, small_m, small_v)
    g_in_t = rb[:W_IN_ROWS]
    grads["w_ple"] = _ple_of_slab(ra)
    grads["w_in"] = jnp.swapaxes(g_in_t, 0, 1)
    for n, off in BIG_A:
        grads[n] = ra[off:off + w[n].shape[1]]
        delta[n], new_m[n], new_v[n] = _adamw(w[n][0], ra, m[n][0], v[n][0], "adamw_" + n, g_off=off)
    delta["w_ple"], new_m["w_ple"], new_v["w_ple"] = _adamw(w_ple[0], grads["w_ple"], m_w_ple[0], v_w_ple[0], "adamw_w_ple")
    tr_ = lambda a: jnp.swapaxes(a[0], 0, 1)
    d_, m_, v_ = _adamw(tr_(w_in), g_in_t, tr_(m_w_in), tr_(v_w_in), "adamw_w_in", by_columns=True)
    delta["w_in"], new_m["w_in"], new_v["w_in"] = (jnp.swapaxes(a, 0, 1) for a in (d_, m_, v_))

    shaped = lambda d: [d[n].reshape(w[n].shape) for n in WEIGHTS]
    return (loss, grad_x[None], *shaped(grads), *shaped(delta), *shaped(new_m), *shaped(new_v))
```

```python
import jax
import jax.numpy as jnp
from jax import lax
from jax.experimental import pallas as pl
from jax.experimental.pallas import tpu as pltpu

f32 = jnp.float32
bf16 = jnp.bfloat16

D_MODEL = 1024
SSD_WIDTH = 1024
SSD_HEADS = 16
HEAD_DIM = 64
SSD_STATE = 128
XBC_WIDTH = 1536
SSD_CONV = 4
CHUNK = 128
CONF_WIDTH = 1024
CONF_KERNEL = 31
D_FF = 4096
PLE_DIM = 256
IN_WIDTH = 4624
EPS = 1e-6
N_CHIPS = 4
N_DEV = 8

ADAM_LR = 0.001
ADAM_B1 = 0.9
ADAM_B2 = 0.999
ADAM_EPS = 1e-08
ADAM_WD = 0.01
ADAM_STEP = 10

LANES = 128
VMEM_BIG = 56 * 1024 * 1024
VMEM_MID = 40 * 1024 * 1024

UP_OFF, DOWN_OFF, OUT_OFF, PG_OFF, PLE_OFF = 0, 1024, 2048, 2560, 2816
PLE_ROWS = 64
SLAB_A = PLE_OFF + PLE_ROWS
GATHER_EARLY_ROWS = 480
EXCHANGE_FIRST_ROWS = 960
W_IN_ROWS = 1156
W_IN_ROWS_PAD = 1184
CONVW_ROWS = 16
SMALL_ROWS = 56

MESH = pl.DeviceIdType.MESH
ANY = pl.BlockSpec(memory_space=pl.ANY)


PIN_SMALL = 256 * 1024


def _pallas(body, pin_bytes=None, **kw):
    call = pl.pallas_call(body, **kw)

    def pin(a):
        wanted = pin_bytes is None or a.size * a.dtype.itemsize <= pin_bytes
        return pltpu.with_memory_space_constraint(a, pltpu.HBM) if wanted and a.dtype != jnp.int32 else a

    def run(*args):
        return call(*[pin(a) for a in args])

    return run


def _cparams(sem=None, vmem=None):
    return pltpu.CompilerParams(dimension_semantics=sem, vmem_limit_bytes=vmem)


def _full(shape):
    n = len(shape)
    return pl.BlockSpec(shape, lambda *_: (0,) * n)


class _Rider:
    def __init__(self, inputs, out_shapes, aliases, n_sems, start, finish):
        self.inputs, self.out_shapes, self.aliases = list(inputs), list(out_shapes), dict(aliases)
        self.n_sems, self.start, self.finish = n_sems, start, finish


def _call(body, args, *, name, grid, in_specs, out_specs, out_shape, scratch_shapes=(), params=None, rider=None):
    if rider is None:
        return _pallas(body, name=name, grid=grid, in_specs=in_specs, out_specs=out_specs, out_shape=out_shape,
                              scratch_shapes=list(scratch_shapes), compiler_params=params)(*args)
    ni, no, ns = len(in_specs), len(out_specs), len(scratch_shapes)
    ri, ro = len(rider.inputs), len(rider.out_shapes)
    (steps,) = grid

    def with_rider(*refs):
        ins, refs = refs[:ni], refs[ni:]
        rins, refs = refs[:ri], refs[ri:]
        outs, refs = refs[:no], refs[no:]
        routs, refs = refs[:ro], refs[ro:]
        scratch, (ssem, rsem) = refs[:ns], refs[ns:]
        step = pl.program_id(0)

        @pl.when(step == 0)
        def _():
            rider.start(rins, routs, ssem, rsem)

        body(*ins, *outs, *scratch)

        @pl.when(step == steps - 1)
        def _():
            rider.finish(rins, routs, ssem, rsem)

    sems = [pltpu.SemaphoreType.DMA((rider.n_sems,)), pltpu.SemaphoreType.DMA((rider.n_sems,))]
    return _pallas(
        with_rider, name=name, grid=grid, in_specs=list(in_specs) + [ANY] * ri, out_specs=list(out_specs) + [ANY] * ro,
        out_shape=list(out_shape) + rider.out_shapes, scratch_shapes=list(scratch_shapes) + sems,
        input_output_aliases={ni + a: no + b for a, b in rider.aliases.items()}, compiler_params=params,
    )(*args, *rider.inputs)


def _dot(a, b):
    return jnp.dot(a, b, preferred_element_type=f32)


def _dot_nt(a, b):
    return lax.dot_general(a, b, (((1,), (1,)), ((), ())), preferred_element_type=f32)


def _dot_tn(a, b):
    return lax.dot_general(a, b, (((0,), (0,)), ((), ())), preferred_element_type=f32)


def _sigmoid(x):
    return jax.nn.sigmoid(x)


def _rms(x, g):
    r = lax.rsqrt(jnp.mean(x * x, axis=-1, keepdims=True) + EPS)
    return x * r * g


def _rms_bwd(dy, x, g):
    r = lax.rsqrt(jnp.mean(x * x, axis=-1, keepdims=True) + EPS)
    xh = x * r
    dg = jnp.sum(dy * xh, axis=0, keepdims=True)
    dxh = dy * g
    dx = r * (dxh - xh * jnp.mean(dxh * xh, axis=-1, keepdims=True))
    return dx, dg


def _dsilu(x):
    s = _sigmoid(x)
    return s * (1.0 + x * (1.0 - s))


def _split3(x):
    hi = x.astype(bf16)
    r1 = x - hi.astype(f32)
    mid = r1.astype(bf16)
    lo = (r1 - mid.astype(f32)).astype(bf16)
    return hi, mid, lo


def _head_matrix():
    row = lax.broadcasted_iota(jnp.int32, (LANES, SSD_WIDTH), 0)
    col = lax.broadcasted_iota(jnp.int32, (LANES, SSD_WIDTH), 1)
    lo = row * HEAD_DIM
    return ((col >= lo) & (col < lo + HEAD_DIM)).astype(bf16)


def _expand(x, e):
    hi, mid, lo = _split3(x)
    return _dot(hi, e) + _dot(mid, e) + _dot(lo, e)


def _contract(x, e):
    hi = x.astype(bf16)
    mid = (x - hi.astype(f32)).astype(bf16)
    return _dot_nt(hi, e) + _dot_nt(mid, e)


O_XBC = SSD_WIDTH
O_DT = O_XBC + XBC_WIDTH
O_CV = O_DT + SSD_HEADS
O_CG = O_CV + CONF_WIDTH


def _assemble_w_in_t(gin_ref, wt_ref):
    for b in range(N_CHIPS):
        wt_ref[b * W_IN_ROWS:(b + 1) * W_IN_ROWS, :] = gin_ref[b, 0:W_IN_ROWS, :]


def _in_proj_fwd(x, g, gin, rider=None):
    T = x.shape[0]
    tm = min(256, T)

    def body(x_ref, g_ref, gin_ref, u_ref, z_ref, xbc_ref, cv_ref, cg_ref, dt_ref, v_ref, wt_ref):
        @pl.when(pl.program_id(0) == 0)
        def _():
            _assemble_w_in_t(gin_ref, wt_ref)

        ub = _rms(x_ref[...], g_ref[...]).astype(bf16)
        u_ref[...] = ub
        z_ref[...] = _dot_nt(ub, wt_ref[0:O_XBC, :])
        xbc_ref[...] = _dot_nt(ub, wt_ref[O_XBC:O_DT, :])
        cv = _dot_nt(ub, wt_ref[O_CV:O_CG, :])
        cg = _dot_nt(ub, wt_ref[O_CG:IN_WIDTH, :])
        cv_ref[...] = cv
        cg_ref[...] = cg
        v_ref[...] = cv * _sigmoid(cg)
        dt_ref[...] = _dot_nt(ub, wt_ref[O_DT:O_DT + LANES, :])

    row = lambda n: pl.BlockSpec((tm, n), lambda i: (i, 0))
    return _call(
        body, (x, g, gin), name="in_proj_fwd", grid=(T // tm,),
        in_specs=[row(D_MODEL), _full((1, D_MODEL)), _full(gin.shape)],
        out_specs=[row(D_MODEL), row(SSD_WIDTH), row(XBC_WIDTH), row(CONF_WIDTH), row(CONF_WIDTH), row(LANES),
                   row(CONF_WIDTH)],
        out_shape=[jax.ShapeDtypeStruct((T, D_MODEL), bf16), jax.ShapeDtypeStruct((T, SSD_WIDTH), f32),
                   jax.ShapeDtypeStruct((T, XBC_WIDTH), f32), jax.ShapeDtypeStruct((T, CONF_WIDTH), f32),
                   jax.ShapeDtypeStruct((T, CONF_WIDTH), f32), jax.ShapeDtypeStruct((T, LANES), f32),
                   jax.ShapeDtypeStruct((T, CONF_WIDTH), f32)],
        scratch_shapes=[pltpu.VMEM((IN_WIDTH, D_MODEL), bf16)],
        params=_cparams(("arbitrary",), VMEM_BIG), rider=rider)


SUBLANES = 8


def _phases(offsets):
    return sorted({o % SUBLANES for o in offsets} - {0})


def _phase_shape(offsets, tm, C):
    a_max = max([o // SUBLANES for o in offsets if o % SUBLANES] or [0])
    return (max(len(_phases(offsets)), 1), tm + SUBLANES * a_max, C)


def _make_phases(buf_ref, ph_ref, offsets, tm):
    for idx, b in enumerate(_phases(offsets)):
        n = tm + SUBLANES * max(o // SUBLANES for o in offsets if o % SUBLANES == b)
        ph_ref[idx, 0:n, :] = buf_ref[pl.ds(b, n), :]


def _window(buf_ref, ph_ref, offsets, o, r0, rb):
    a, b = divmod(o, SUBLANES)
    if b == 0:
        return buf_ref[pl.ds(r0 + SUBLANES * a, rb), :]
    return ph_ref[_phases(offsets).index(b), pl.ds(r0 + SUBLANES * a, rb), :]


def _conv_rows(wb_ref, buf_ref, ph_ref, offsets, r0, rb):
    nsub = rb // SUBLANES
    accs = [None] * nsub
    for k, o in enumerate(offsets):
        wk = wb_ref[pl.ds(SUBLANES * k, SUBLANES), :]
        for s in range(nsub):
            term = wk * _window(buf_ref, ph_ref, offsets, o, r0 + SUBLANES * s, SUBLANES)
            accs[s] = term if accs[s] is None else accs[s] + term
    return accs[0] if nsub == 1 else jnp.concatenate(accs, axis=0)


def _sublane_rows(w):
    return jnp.repeat(w, SUBLANES, axis=0)


def _fwd_offsets(K, hb):
    return [hb - (K - 1) + k for k in range(K)]


def _prev_halo_spec(hb, tm, C):
    return pl.BlockSpec((hb, C), lambda i: (jnp.maximum(i * (tm // hb) - 1, 0), 0))


CONV_RB = 16


def _ssd_conv_fwd(xbc, w, b):
    T, C = xbc.shape
    K, hb = SSD_CONV, 8
    tm = min(256, T)
    offs = _fwd_offsets(K, hb)

    def body(cur_ref, halo_ref, w_ref, b_ref, pre_ref, buf_ref, ph_ref):
        keep = jnp.where(pl.program_id(0) > 0, 1.0, 0.0)
        buf_ref[0:hb, :] = halo_ref[...] * keep
        buf_ref[hb:hb + tm, :] = cur_ref[...]
        _make_phases(buf_ref, ph_ref, offs, tm)

        def chunk(i, carry):
            r0 = pl.multiple_of(i * CONV_RB, CONV_RB)
            pre_ref[pl.ds(r0, CONV_RB), :] = _conv_rows(w_ref, buf_ref, ph_ref, offs, r0, CONV_RB) + b_ref[...]
            return carry

        lax.fori_loop(0, tm // CONV_RB, chunk, 0)

    return _pallas(
        body, name="ssd_conv_fwd", grid=(T // tm,),
        in_specs=[pl.BlockSpec((tm, C), lambda i: (i, 0)), _prev_halo_spec(hb, tm, C), _full((SUBLANES * K, C)),
                  _full((1, C))],
        out_specs=pl.BlockSpec((tm, C), lambda i: (i, 0)),
        out_shape=jax.ShapeDtypeStruct((T, C), f32),
        scratch_shapes=[pltpu.VMEM((hb + tm, C), f32), pltpu.VMEM(_phase_shape(offs, tm, C), f32)],
        compiler_params=_cparams(("parallel",), VMEM_MID),
    )(xbc, xbc, _sublane_rows(w), b)


def _conf_fwd(v, w, b, ln_g, ln_b, rider=None):
    T, C = v.shape
    K, hb = CONF_KERNEL, 32
    tm = min(256, T)
    offs = _fwd_offsets(K, hb)
    rb = 2 * CONV_RB

    def body(cur_ref, halo_ref, w_ref, b_ref, g_ref, bb_ref, co_ref, y_ref, buf_ref, ph_ref):
        keep = jnp.where(pl.program_id(0) > 0, 1.0, 0.0)
        buf_ref[0:hb, :] = halo_ref[...] * keep
        buf_ref[hb:hb + tm, :] = cur_ref[...]
        _make_phases(buf_ref, ph_ref, offs, tm)

        def chunk(i, carry):
            r0 = pl.multiple_of(i * rb, rb)
            co = _conv_rows(w_ref, buf_ref, ph_ref, offs, r0, rb) + b_ref[...]
            co_ref[pl.ds(r0, rb), :] = co
            mu = jnp.mean(co, axis=-1, keepdims=True)
            xc = co - mu
            yn = xc * lax.rsqrt(jnp.mean(xc * xc, axis=-1, keepdims=True) + EPS) * g_ref[...] + bb_ref[...]
            y_ref[pl.ds(r0, rb), :] = (yn * _sigmoid(yn)).astype(bf16)
            return carry

        lax.fori_loop(0, tm // rb, chunk, 0)

    return _call(
        body, (v, v, _sublane_rows(w), b, ln_g, ln_b), name="conf_fwd", grid=(T // tm,),
        in_specs=[pl.BlockSpec((tm, C), lambda i: (i, 0)), _prev_halo_spec(hb, tm, C), _full((SUBLANES * K, C)),
                  _full((1, C)), _full((1, C)), _full((1, C))],
        out_specs=[pl.BlockSpec((tm, C), lambda i: (i, 0)), pl.BlockSpec((tm, C), lambda i: (i, 0))],
        out_shape=[jax.ShapeDtypeStruct((T, C), f32), jax.ShapeDtypeStruct((T, C), bf16)],
        scratch_shapes=[pltpu.VMEM((hb + tm, C), f32), pltpu.VMEM(_phase_shape(offs, tm, C), f32)],
        params=_cparams(("arbitrary",), VMEM_MID), rider=rider)


def _ssd_chunk_common(pre, dtr, dtb, alog, e):
    act = pre * _sigmoid(pre)
    xs = act[:, :SSD_WIDTH]
    bm = act[:, SSD_WIDTH:SSD_WIDTH + 2 * SSD_STATE]
    cm = act[:, SSD_WIDTH + 2 * SSD_STATE:]
    row = lax.broadcasted_iota(jnp.int32, (CHUNK, CHUNK), 0)
    col = lax.broadcasted_iota(jnp.int32, (CHUNK, CHUNK), 1)
    tri = row >= col
    dt = jax.nn.softplus(dtr + dtb)
    a_neg = -jnp.exp(alog)
    a = dt * a_neg
    cs = jnp.dot(tri.astype(f32), a, precision=lax.Precision.HIGHEST, preferred_element_type=f32)
    cs_e = _expand(cs, e)
    dt_e = _expand(dt, e)
    csl_e = cs_e[CHUNK - 1:CHUNK, :]
    ecs_e = jnp.exp(cs_e)
    dte_e = jnp.exp(csl_e - cs_e)
    cd_e = jnp.exp(csl_e)
    xc = xs * dt_e
    xd = xc * dte_e
    return dict(xs=xs, bm=bm, cm=cm, tri=tri, dt=dt, a_neg=a_neg, cs=cs, ecs_e=ecs_e, dte_e=dte_e, cd_e=cd_e,
                dt_e=dt_e, xc=xc, xd=xd)


def _group(v, g, width):
    return v[:, g * width:(g + 1) * width]


def _ssd_fwd(pre, dtr, z, dtb, alog, dskip_e, gn):
    T = pre.shape[0]
    nc = T // CHUNK
    GW = SSD_WIDTH // 2

    def body(pre_ref, dtr_ref, z_ref, dtb_ref, alog_ref, de_ref, gn_ref, y_ref, ys_ref, sp_ref, st_ref):
        @pl.when(pl.program_id(0) == 0)
        def _():
            st_ref[...] = jnp.zeros_like(st_ref)

        e = _head_matrix()
        q = _ssd_chunk_common(pre_ref[...], dtr_ref[...], dtb_ref[...], alog_ref[...], e)
        cs, tri, xc, xd = q["cs"], q["tri"], q["xc"], q["xd"]
        cs_t = cs.T
        st = st_ref[...]
        sp_ref[0] = st
        lane = lax.broadcasted_iota(jnp.int32, (1, LANES), 1)
        halves = (lane < HEAD_DIM, lane >= HEAD_DIM)

        g_mat, y_off, s_new = [], [], []
        for g in range(2):
            bg = _group(q["bm"], g, SSD_STATE)
            cg = _group(q["cm"], g, SSD_STATE)
            bgb, cgb = bg.astype(bf16), cg.astype(bf16)
            g_mat.append(_dot_nt(cgb, bgb))
            y_off.append(_dot(cgb, _group(st, g, GW).astype(bf16)))
            s_new.append(_dot(bg.T.astype(bf16), _group(xd, g, GW).astype(bf16)))
        y_off = jnp.concatenate(y_off, axis=1) * q["ecs_e"]
        st_ref[...] = st * q["cd_e"] + jnp.concatenate(s_new, axis=1)

        pairs = []
        for j in range(SSD_HEADS // 2):
            xp = xc[:, j * LANES:(j + 1) * LANES]
            acc = jnp.zeros((CHUNK, LANES), f32)
            for hh in range(2):
                h = 2 * j + hh
                seg = cs[:, h:h + 1] - cs_t[h:h + 1, :]
                lm = jnp.exp(jnp.where(tri, seg, -1e30))
                m = (g_mat[h // 8] * lm).astype(bf16)
                acc = acc + _dot(m, jnp.where(halves[hh], xp, 0.0).astype(bf16))
            pairs.append(acc)
        y = jnp.concatenate(pairs, axis=1) + y_off + q["xs"] * de_ref[...]
        y_ref[...] = y

        zz = z_ref[...]
        v = y * (zz * _sigmoid(zz))
        outs = []
        for g in range(2):
            vg = _group(v, g, GW)
            outs.append(vg * lax.rsqrt(jnp.mean(vg * vg, axis=-1, keepdims=True) + EPS))
        ys_ref[...] = (jnp.concatenate(outs, axis=1) * gn_ref[...]).astype(bf16)

    ch = lambda n: pl.BlockSpec((CHUNK, n), lambda c: (c, 0))
    return _pallas(
        body, name="ssd_fwd", grid=(nc,),
        in_specs=[ch(XBC_WIDTH), ch(LANES), ch(SSD_WIDTH), _full((1, LANES)), _full((1, LANES)), _full((1, SSD_WIDTH)),
                  _full((1, SSD_WIDTH))],
        out_specs=[ch(SSD_WIDTH), ch(SSD_WIDTH), pl.BlockSpec((1, SSD_STATE, SSD_WIDTH), lambda c: (c, 0, 0))],
        out_shape=[jax.ShapeDtypeStruct((T, SSD_WIDTH), f32), jax.ShapeDtypeStruct((T, SSD_WIDTH), bf16),
                   jax.ShapeDtypeStruct((nc, SSD_STATE, SSD_WIDTH), f32)],
        scratch_shapes=[pltpu.VMEM((SSD_STATE, SSD_WIDTH), f32)],
        compiler_params=_cparams(("arbitrary",), VMEM_MID),
    )(pre, dtr, z, dtb, alog, dskip_e, gn)


def _w_out_spec():
    n = 2 * SSD_WIDTH // N_CHIPS
    return pl.BlockSpec((N_CHIPS, n, D_MODEL), lambda *_: (0, OUT_OFF // n, 0))


def _out_proj_fwd(x, ys, yc, gath, g):
    T = x.shape[0]
    tm = min(512, T)
    n = 2 * SSD_WIDTH // N_CHIPS

    def body(x_ref, ys_ref, yc_ref, w_ref, g_ref, h_ref, u_ref):
        h = (x_ref[...] + _dot(ys_ref[:, 0:n], w_ref[0]) + _dot(ys_ref[:, n:], w_ref[1])
             + _dot(yc_ref[:, 0:n], w_ref[2]) + _dot(yc_ref[:, n:], w_ref[3]))
        h_ref[...] = h
        u_ref[...] = _rms(h, g_ref[...]).astype(bf16)

    row = pl.BlockSpec((tm, D_MODEL), lambda i: (i, 0))
    return _pallas(
        body, name="out_proj_fwd", grid=(T // tm,),
        in_specs=[row, row, row, _w_out_spec(), _full((1, D_MODEL))],
        out_specs=[row, row],
        out_shape=[jax.ShapeDtypeStruct((T, D_MODEL), f32), jax.ShapeDtypeStruct((T, D_MODEL), bf16)],
        compiler_params=_cparams(("parallel",), VMEM_MID),
    )(x, ys, yc, gath, g)


def _w_up_spec():
    return pl.BlockSpec((1, D_MODEL, D_MODEL), lambda i, b: (b, UP_OFF // D_MODEL, 0))


def _w_down_spec():
    return pl.BlockSpec((1, D_MODEL, D_MODEL), lambda i, b: (b, DOWN_OFF // D_MODEL, 0))


def _mlp_fwd(h1, u1, gath, g_next):
    T = h1.shape[0]
    tm = min(512, T)
    nb = D_FF // D_MODEL

    def body(h_ref, u_ref, wu_ref, wd_ref, g_ref, r_ref, h2_ref, u2_ref, acc_ref):
        b = pl.program_id(1)

        @pl.when(b == 0)
        def _():
            acc_ref[...] = jnp.zeros_like(acc_ref)

        r = jnp.maximum(_dot(u_ref[...], wu_ref[0]), 0.0)
        r_ref[...] = r.astype(bf16)
        acc_ref[...] += _dot((r * r).astype(bf16), wd_ref[0])

        @pl.when(b == nb - 1)
        def _():
            h2 = h_ref[...] + acc_ref[...]
            h2_ref[...] = h2
            u2_ref[...] = _rms(h2, g_ref[...]).astype(bf16)

    row = pl.BlockSpec((tm, D_MODEL), lambda i, b: (i, 0))
    return _pallas(
        body, name="mlp_fwd", grid=(T // tm, nb),
        in_specs=[row, row, _w_up_spec(), _w_down_spec(), _full((1, D_MODEL))],
        out_specs=[pl.BlockSpec((tm, D_MODEL), lambda i, b: (i, b)), row, row],
        out_shape=[jax.ShapeDtypeStruct((T, D_FF), bf16), jax.ShapeDtypeStruct((T, D_MODEL), f32),
                   jax.ShapeDtypeStruct((T, D_MODEL), bf16)],
        scratch_shapes=[pltpu.VMEM((tm, D_MODEL), f32)],
        compiler_params=_cparams(("parallel", "arbitrary"), VMEM_MID),
    )(h1, u1, gath, gath, g_next)


def _ple_loss(h2, u2, p, tgt, gath, b_pg, w_ple, g_ple, g_fin, g_pg):
    T = h2.shape[0]
    tm = min(256, T)
    npg = D_MODEL // N_CHIPS

    def body(h2_ref, u2_ref, p_ref, t_ref, wpg_ref, bpg_ref, wple_ref, gple_ref, gfin_ref, gpg_ref,
             loss_ref, dh2_ref, dh2b_ref, dgp_ref, dep_ref, dgfin_ref, dgple_ref, dbpg_ref, dgpg_ref):
        @pl.when(pl.program_id(0) == 0)
        def _():
            loss_ref[...] = jnp.zeros_like(loss_ref)
            dgfin_ref[...] = jnp.zeros_like(dgfin_ref)
            dgple_ref[...] = jnp.zeros_like(dgple_ref)
            dbpg_ref[...] = jnp.zeros_like(dbpg_ref)
            dgpg_ref[...] = jnp.zeros_like(dgpg_ref)

        h2 = h2_ref[...]
        gate_pre = bpg_ref[...]
        for b in range(N_CHIPS):
            gate_pre = gate_pre + _dot(u2_ref[:, b * npg:(b + 1) * npg], wpg_ref[b])
        gate = _sigmoid(gate_pre)
        e_pre = _dot(p_ref[...].astype(bf16), wple_ref[...])
        emb = _rms(e_pre, gple_ref[...])
        h3 = h2 + gate * emb
        diff = _rms(h3, gfin_ref[...]) - t_ref[...]
        sq = jnp.sum(jnp.sum(diff * diff, axis=1, keepdims=True), axis=0, keepdims=True)
        loss_ref[...] += (0.5 / D_MODEL) * sq
        dh3, dgfin = _rms_bwd(diff * (1.0 / D_MODEL), h3, gfin_ref[...])
        dgfin_ref[...] += dgfin
        dgp = dh3 * emb * gate * (1.0 - gate)
        dbpg_ref[...] += jnp.sum(dgp, axis=0, keepdims=True)
        dep, dgple = _rms_bwd(dh3 * gate, e_pre, gple_ref[...])
        dgple_ref[...] += dgple
        dgpb = dgp.astype(bf16)
        dgp_ref[...] = dgpb
        dep_ref[...] = dep.astype(bf16)
        du2 = jnp.concatenate([_dot_nt(dgpb, wpg_ref[b]) for b in range(N_CHIPS)], axis=1)
        dx, dgpg = _rms_bwd(du2, h2, gpg_ref[...])
        dgpg_ref[...] += dgpg
        dh2 = dh3 + dx
        dh2_ref[...] = dh2
        dh2b_ref[...] = dh2.astype(bf16)

    row = pl.BlockSpec((tm, D_MODEL), lambda i: (i, 0))
    vec = _full((1, D_MODEL))
    vshape = jax.ShapeDtypeStruct((1, D_MODEL), f32)
    return _pallas(
        body, name="ple_loss", grid=(T // tm,),
        in_specs=[row, row, pl.BlockSpec((tm, PLE_DIM), lambda i: (i, 0)), row,
                  pl.BlockSpec((N_CHIPS, npg, D_MODEL), lambda i: (0, PG_OFF // npg, 0)), vec, _full(w_ple.shape),
                  vec, vec, vec],
        out_specs=[_full((8, LANES)), row, row, row, row, vec, vec, vec, vec],
        out_shape=[jax.ShapeDtypeStruct((8, LANES), f32), jax.ShapeDtypeStruct((T, D_MODEL), f32),
                   jax.ShapeDtypeStruct((T, D_MODEL), bf16), jax.ShapeDtypeStruct((T, D_MODEL), bf16),
                   jax.ShapeDtypeStruct((T, D_MODEL), bf16), vshape, vshape, vshape, vshape],
        compiler_params=_cparams(("arbitrary",), VMEM_MID),
    )(h2, u2, p, tgt, gath, b_pg, w_ple, g_ple, g_fin, g_pg)


def _mlp_bwd(dh2, r, gath, h1, g):
    T = dh2.shape[0]
    tm = min(512, T)
    nb = D_FF // D_MODEL

    def body(dh2_ref, r_ref, wd_ref, wu_ref, h1_ref, g_ref, dhp_ref, dh1_ref, dh1b_ref, dg_ref, acc_ref):
        i, b = pl.program_id(0), pl.program_id(1)

        @pl.when(b == 0)
        def _():
            acc_ref[...] = jnp.zeros_like(acc_ref)

        @pl.when((b == 0) & (i == 0))
        def _():
            dg_ref[...] = jnp.zeros_like(dg_ref)

        dact = _dot_nt(dh2_ref[...].astype(bf16), wd_ref[0])
        dhp = (dact * 2.0 * r_ref[...].astype(f32)).astype(bf16)
        dhp_ref[...] = dhp
        acc_ref[...] += _dot_nt(dhp, wu_ref[0])

        @pl.when(b == nb - 1)
        def _():
            dx, dg = _rms_bwd(acc_ref[...], h1_ref[...], g_ref[...])
            dg_ref[...] += dg
            dh1 = dh2_ref[...] + dx
            dh1_ref[...] = dh1
            dh1b_ref[...] = dh1.astype(bf16)

    row = pl.BlockSpec((tm, D_MODEL), lambda i, b: (i, 0))
    return _pallas(
        body, name="mlp_bwd", grid=(T // tm, nb),
        in_specs=[row, pl.BlockSpec((tm, D_MODEL), lambda i, b: (i, b)), _w_down_spec(), _w_up_spec(), row,
                  _full((1, D_MODEL))],
        out_specs=[pl.BlockSpec((tm, D_MODEL), lambda i, b: (i, b)), row, row, _full((1, D_MODEL))],
        out_shape=[jax.ShapeDtypeStruct((T, D_FF), bf16), jax.ShapeDtypeStruct((T, D_MODEL), f32),
                   jax.ShapeDtypeStruct((T, D_MODEL), bf16), jax.ShapeDtypeStruct((1, D_MODEL), f32)],
        scratch_shapes=[pltpu.VMEM((tm, D_MODEL), f32)],
        compiler_params=_cparams(("arbitrary", "arbitrary"), VMEM_MID),
    )(dh2, r, gath, gath, h1, g)


def _out_proj_bwd(dh1, gath, co, ln_g, ln_b, rider=None):
    T = dh1.shape[0]
    tm = min(512, T)

    def body(dh_ref, w_ref, co_ref, g_ref, b_ref, dys_ref, dco_ref, dg_ref, db_ref):
        @pl.when(pl.program_id(0) == 0)
        def _():
            dg_ref[...] = jnp.zeros_like(dg_ref)
            db_ref[...] = jnp.zeros_like(db_ref)

        dhb = dh_ref[...].astype(bf16)
        dys_ref[...] = jnp.concatenate([_dot_nt(dhb, w_ref[0]), _dot_nt(dhb, w_ref[1])], axis=1)
        dyc = jnp.concatenate([_dot_nt(dhb, w_ref[2]), _dot_nt(dhb, w_ref[3])], axis=1)
        co = co_ref[...]
        mu = jnp.mean(co, axis=-1, keepdims=True)
        xc = co - mu
        rstd = lax.rsqrt(jnp.mean(xc * xc, axis=-1, keepdims=True) + EPS)
        xh = xc * rstd
        yn = xh * g_ref[...] + b_ref[...]
        dyn = dyc * _dsilu(yn)
        dg_ref[...] += jnp.sum(dyn * xh, axis=0, keepdims=True)
        db_ref[...] += jnp.sum(dyn, axis=0, keepdims=True)
        dxh = dyn * g_ref[...]
        dco_ref[...] = rstd * (dxh - jnp.mean(dxh, axis=-1, keepdims=True)
                               - xh * jnp.mean(dxh * xh, axis=-1, keepdims=True))

    row = pl.BlockSpec((tm, D_MODEL), lambda i: (i, 0))
    vec = _full((1, CONF_WIDTH))
    vshape = jax.ShapeDtypeStruct((1, CONF_WIDTH), f32)
    return _call(
        body, (dh1, gath, co, ln_g, ln_b), name="out_proj_bwd", grid=(T // tm,),
        in_specs=[row, _w_out_spec(), row, vec, vec],
        out_specs=[row, row, vec, vec],
        out_shape=[jax.ShapeDtypeStruct((T, SSD_WIDTH), f32), jax.ShapeDtypeStruct((T, CONF_WIDTH), f32), vshape, vshape],
        params=_cparams(("arbitrary",), VMEM_MID), rider=rider)


def _bwd_offsets(K):
    return [K - 1 - k for k in range(K)]


def _next_halo_spec(hb, tm, C, T):
    return pl.BlockSpec((hb, C), lambda i: (jnp.minimum((i + 1) * (tm // hb), T // hb - 1), 0))


DW_RB = 8
DW_UNROLL = 16
DW_ACC_VREGS = 32


def _conv_dw(dw_ref, bufd_ref, bufx_ref, phx_ref, offs_x, tm, C):
    K = len(offs_x)
    group = max(1, DW_ACC_VREGS // (C // LANES))
    for k0 in range(0, K, group):
        ks = list(range(k0, min(k0 + group, K)))

        def step(i, accs, ks=ks):
            for u in range(DW_UNROLL):
                r0 = pl.multiple_of((i * DW_UNROLL + u) * DW_RB, DW_RB)
                d = bufd_ref[pl.ds(r0, DW_RB), :]
                accs = tuple(acc + _window(bufx_ref, phx_ref, offs_x, offs_x[k], r0, DW_RB) * d
                             for k, acc in zip(ks, accs))
            return accs

        accs = lax.fori_loop(0, tm // (DW_RB * DW_UNROLL), step, tuple(jnp.zeros((DW_RB, C), f32) for _ in ks))
        for k, acc in zip(ks, accs):
            dw_ref[k:k + 1, :] += jnp.sum(acc, axis=0, keepdims=True)


def _fill_bwd_buffers(dcur_ref, dnext_ref, xcur_ref, xprev_ref, bufd_ref, bufx_ref, phd_ref, phx_ref, offs_d, offs_x,
                      hb, tm, first, last):
    bufd_ref[0:tm, :] = dcur_ref[...]
    bufd_ref[tm:tm + hb, :] = dnext_ref[...] * jnp.where(last, 0.0, 1.0)
    bufx_ref[0:hb, :] = xprev_ref[...] * jnp.where(first, 0.0, 1.0)
    bufx_ref[hb:hb + tm, :] = xcur_ref[...]
    _make_phases(bufd_ref, phd_ref, offs_d, tm)
    _make_phases(bufx_ref, phx_ref, offs_x, tm)


def _ssd_conv_bwd(dpre, xbc, w):
    T, C = xbc.shape
    K, hb = SSD_CONV, 8
    tm = min(256, T)
    nt = T // tm
    offs_d, offs_x = _bwd_offsets(K), _fwd_offsets(K, hb)

    def body(dcur_ref, dnext_ref, xcur_ref, xprev_ref, w_ref, dx_ref, dw_ref, db_ref, bufd_ref, bufx_ref, phd_ref, phx_ref):
        i = pl.program_id(0)

        @pl.when(i == 0)
        def _():
            dw_ref[...] = jnp.zeros_like(dw_ref)
            db_ref[...] = jnp.zeros_like(db_ref)

        _fill_bwd_buffers(dcur_ref, dnext_ref, xcur_ref, xprev_ref, bufd_ref, bufx_ref, phd_ref, phx_ref, offs_d, offs_x,
                          hb, tm, i == 0, i == nt - 1)

        def chunk(j, carry):
            r0 = pl.multiple_of(j * CONV_RB, CONV_RB)
            dx_ref[pl.ds(r0, CONV_RB), :] = _conv_rows(w_ref, bufd_ref, phd_ref, offs_d, r0, CONV_RB).astype(bf16)
            return carry

        lax.fori_loop(0, tm // CONV_RB, chunk, 0)
        _conv_dw(dw_ref, bufd_ref, bufx_ref, phx_ref, offs_x, tm, C)
        db_ref[...] += jnp.sum(dcur_ref[...], axis=0, keepdims=True)

    row = pl.BlockSpec((tm, C), lambda i: (i, 0))
    return _pallas(
        body, name="ssd_conv_bwd", grid=(nt,),
        in_specs=[row, _next_halo_spec(hb, tm, C, T), row, _prev_halo_spec(hb, tm, C), _full((SUBLANES * K, C))],
        out_specs=[row, _full((8, C)), _full((1, C))],
        out_shape=[jax.ShapeDtypeStruct((T, C), bf16), jax.ShapeDtypeStruct((8, C), f32), jax.ShapeDtypeStruct((1, C), f32)],
        scratch_shapes=[pltpu.VMEM((tm + hb, C), f32), pltpu.VMEM((hb + tm, C), f32),
                        pltpu.VMEM(_phase_shape(offs_d, tm, C), f32),
                        pltpu.VMEM(_phase_shape(offs_x, tm, C), f32)],
        compiler_params=_cparams(("arbitrary",), VMEM_BIG),
    )(dpre, dpre, xbc, xbc, _sublane_rows(w))


def _conf_conv_bwd(dco, v, w, cv, cg, rider=None):
    T, C = v.shape
    K, hb = CONF_KERNEL, 32
    tm = min(256, T)
    nt = T // tm
    offs_d, offs_x = _bwd_offsets(K), _fwd_offsets(K, hb)

    def body(dcur_ref, dnext_ref, vcur_ref, vprev_ref, w_ref, cv_ref, cg_ref, dcv_ref, dcg_ref, dw_ref, db_ref,
             bufd_ref, bufx_ref, phd_ref, phx_ref):
        i = pl.program_id(0)

        @pl.when(i == 0)
        def _():
            dw_ref[...] = jnp.zeros_like(dw_ref)
            db_ref[...] = jnp.zeros_like(db_ref)

        _fill_bwd_buffers(dcur_ref, dnext_ref, vcur_ref, vprev_ref, bufd_ref, bufx_ref, phd_ref, phx_ref, offs_d, offs_x,
                          hb, tm, i == 0, i == nt - 1)

        def chunk(j, carry):
            r0 = pl.multiple_of(j * CONV_RB, CONV_RB)
            rows = pl.ds(r0, CONV_RB)
            dv = _conv_rows(w_ref, bufd_ref, phd_ref, offs_d, r0, CONV_RB)
            s = _sigmoid(cg_ref[rows, :])
            dcv_ref[rows, :] = (dv * s).astype(bf16)
            dcg_ref[rows, :] = (dv * cv_ref[rows, :] * s * (1.0 - s)).astype(bf16)
            return carry

        lax.fori_loop(0, tm // CONV_RB, chunk, 0)
        _conv_dw(dw_ref, bufd_ref, bufx_ref, phx_ref, offs_x, tm, C)
        db_ref[...] += jnp.sum(dcur_ref[...], axis=0, keepdims=True)

    row = pl.BlockSpec((tm, C), lambda i: (i, 0))
    return _call(
        body, (dco, dco, v, v, _sublane_rows(w), cv, cg), name="conf_conv_bwd", grid=(nt,),
        in_specs=[row, _next_halo_spec(hb, tm, C, T), row, _prev_halo_spec(hb, tm, C), _full((SUBLANES * K, C)), row, row],
        out_specs=[row, row, _full((32, C)), _full((1, C))],
        out_shape=[jax.ShapeDtypeStruct((T, C), bf16), jax.ShapeDtypeStruct((T, C), bf16),
                   jax.ShapeDtypeStruct((32, C), f32), jax.ShapeDtypeStruct((1, C), f32)],
        scratch_shapes=[pltpu.VMEM((tm + hb, C), f32), pltpu.VMEM((hb + tm, C), f32),
                        pltpu.VMEM(_phase_shape(offs_d, tm, C), f32),
                        pltpu.VMEM(_phase_shape(offs_x, tm, C), f32)],
        params=_cparams(("arbitrary",), VMEM_BIG), rider=rider)


def _ssd_bwd(dys, y, z, pre, dtr, sprev, dtb, alog, dskip_e, gn, rider=None):
    T = pre.shape[0]
    nc = T // CHUNK
    GW = SSD_WIDTH // 2

    def body(dys_ref, y_ref, z_ref, pre_ref, dtr_ref, sp_ref, dtb_ref, alog_ref, de_ref, gn_ref,
             dz_ref, dpre_ref, ddtr_ref, dgn_ref, dd_ref, dal_ref, ddtb_ref, ds_ref):
        @pl.when(pl.program_id(0) == 0)
        def _():
            ds_ref[...] = jnp.zeros_like(ds_ref)
            dgn_ref[...] = jnp.zeros_like(dgn_ref)
            dd_ref[...] = jnp.zeros_like(dd_ref)
            dal_ref[...] = jnp.zeros_like(dal_ref)
            ddtb_ref[...] = jnp.zeros_like(ddtb_ref)

        e = _head_matrix()
        pre = pre_ref[...]
        dtr_b = dtr_ref[...] + dtb_ref[...]
        q = _ssd_chunk_common(pre, dtr_ref[...], dtb_ref[...], alog_ref[...], e)
        cs, tri, xc, xd, xs, dt = q["cs"], q["tri"], q["xc"], q["xd"], q["xs"], q["dt"]
        cs_t = cs.T
        st = sp_ref[0]
        dsn = ds_ref[...]
        lane = lax.broadcasted_iota(jnp.int32, (1, LANES), 1)
        halves = (lane < HEAD_DIM, lane >= HEAD_DIM)
        row_i = lax.broadcasted_iota(jnp.int32, (CHUNK, CHUNK), 0)
        col_i = lax.broadcasted_iota(jnp.int32, (CHUNK, CHUNK), 1)
        tri_t = col_i >= row_i

        y = y_ref[...]
        zz = z_ref[...]
        sz = _sigmoid(zz)
        silu_z = zz * sz
        v = y * silu_z
        dout = dys_ref[...]
        gn_v = gn_ref[...]
        dv, vh = [], []
        for g in range(2):
            vg = _group(v, g, GW)
            rstd = lax.rsqrt(jnp.mean(vg * vg, axis=-1, keepdims=True) + EPS)
            vhg = vg * rstd
            dvh = _group(dout, g, GW) * _group(gn_v, g, GW)
            dv.append(rstd * (dvh - vhg * jnp.mean(dvh * vhg, axis=-1, keepdims=True)))
            vh.append(vhg)
        dv = jnp.concatenate(dv, axis=1)
        dgn_ref[...] += jnp.sum(dout * jnp.concatenate(vh, axis=1), axis=0, keepdims=True)
        dy = dv * silu_z
        dz_ref[...] = (dv * y * (sz * (1.0 + zz * (1.0 - sz)))).astype(bf16)

        dd_row = jnp.sum(dy * xs, axis=0, keepdims=True)
        dd_ref[...] += _contract(jnp.broadcast_to(dd_row, (8, SSD_WIDTH)), e)[0:1, :]
        dxs = dy * de_ref[...]

        dz_in = dy * q["ecs_e"]
        g_mat, gt_mat, dcm, dbm, dsp, dxd, y_off = [], [], [], [], [], [], []
        bgs, cgs = [], []
        for g in range(2):
            bg = _group(q["bm"], g, SSD_STATE)
            cg = _group(q["cm"], g, SSD_STATE)
            bgb, cgb = bg.astype(bf16), cg.astype(bf16)
            bgs.append(bgb)
            cgs.append(cgb)
            stg = _group(st, g, GW).astype(bf16)
            dsng = _group(dsn, g, GW).astype(bf16)
            dzg = _group(dz_in, g, GW).astype(bf16)
            g_mat.append(_dot_nt(cgb, bgb))
            gt_mat.append(_dot_nt(bgb, cgb))
            y_off.append(_dot(cgb, stg))
            dcm.append(_dot_nt(dzg, stg))
            dsp.append(_dot(cg.T.astype(bf16), dzg))
            dbm.append(_dot_nt(_group(xd, g, GW).astype(bf16), dsng))
            dxd.append(_dot(bgb, dsng))
        y_off = jnp.concatenate(y_off, axis=1) * q["ecs_e"]
        dxd = jnp.concatenate(dxd, axis=1)
        ds_ref[...] = dsn * q["cd_e"] + jnp.concatenate(dsp, axis=1)
        dcd_row = jnp.sum(dsn * st, axis=0, keepdims=True) * q["cd_e"]
        t_e = dxd * xd
        dcs = _contract(dy * y_off - t_e, e)
        last_row = _contract(jnp.broadcast_to(dcd_row + jnp.sum(t_e, axis=0, keepdims=True), (8, SSD_WIDTH)), e)[0:1, :]
        dxc_state = dxd * q["dte_e"]

        dg_acc = [jnp.zeros((CHUNK, CHUNK), f32), jnp.zeros((CHUNK, CHUNK), f32)]
        dgt_acc = [jnp.zeros((CHUNK, CHUNK), f32), jnp.zeros((CHUNK, CHUNK), f32)]
        dxc_pairs = []
        for j in range(SSD_HEADS // 2):
            dyp_f = dy[:, j * LANES:(j + 1) * LANES]
            xcp_f = xc[:, j * LANES:(j + 1) * LANES]
            acc = jnp.zeros((CHUNK, LANES), f32)
            for hh in range(2):
                h = 2 * j + hh
                g = h // 8
                dyp = jnp.where(halves[hh], dyp_f, 0.0).astype(bf16)
                xcp = jnp.where(halves[hh], xcp_f, 0.0).astype(bf16)
                lm = jnp.exp(jnp.where(tri, cs[:, h:h + 1] - cs_t[h:h + 1, :], -1e30))
                lm_t = jnp.exp(jnp.where(tri_t, cs_t[h:h + 1, :] - cs[:, h:h + 1], -1e30))
                dm = _dot_nt(dyp, xcp) * lm
                dm_t = _dot_nt(xcp, dyp) * lm_t
                acc = acc + _dot((gt_mat[g] * lm_t).astype(bf16), dyp)
                dg_acc[g] = dg_acc[g] + dm
                dgt_acc[g] = dgt_acc[g] + dm_t
                qd = jnp.sum(dm * g_mat[g] - dm_t * gt_mat[g], axis=1, keepdims=True)
                dcs = dcs + qd * (lane == h).astype(f32)
            dxc_pairs.append(acc)
        dxc = jnp.concatenate(dxc_pairs, axis=1) + dxc_state
        for g in range(2):
            dcm[g] = dcm[g] + _dot(dg_acc[g].astype(bf16), bgs[g])
            dbm[g] = dbm[g] + _dot(dgt_acc[g].astype(bf16), cgs[g])

        dxs = dxs + dxc * q["dt_e"]
        ddt = _contract(dxc * xs, e)
        dcs = dcs + jnp.where(row_i == CHUNK - 1, jnp.broadcast_to(last_row, (CHUNK, LANES)), 0.0)
        da = jnp.dot(tri_t.astype(f32), dcs, precision=lax.Precision.HIGHEST, preferred_element_type=f32)
        ddt = ddt + da * q["a_neg"]
        dal_ref[...] += jnp.sum(da * dt, axis=0, keepdims=True) * q["a_neg"]
        ddtr = ddt * _sigmoid(dtr_b) * (lane < SSD_HEADS).astype(f32)
        ddtb_ref[...] += jnp.sum(ddtr, axis=0, keepdims=True)
        ddtr_ref[...] = ddtr.astype(bf16)

        dact = jnp.concatenate([dxs, dbm[0], dbm[1], dcm[0], dcm[1]], axis=1)
        dpre_ref[...] = dact * _dsilu(pre)

    rev = lambda n: pl.BlockSpec((CHUNK, n), lambda c: (nc - 1 - c, 0))
    vec = _full((1, LANES))
    vshape = jax.ShapeDtypeStruct((1, LANES), f32)
    return _call(
        body, (dys, y, z, pre, dtr, sprev, dtb, alog, dskip_e, gn), name="ssd_bwd", grid=(nc,),
        in_specs=[rev(SSD_WIDTH), rev(SSD_WIDTH), rev(SSD_WIDTH), rev(XBC_WIDTH), rev(LANES),
                  pl.BlockSpec((1, SSD_STATE, SSD_WIDTH), lambda c: (nc - 1 - c, 0, 0)),
                  vec, vec, _full((1, SSD_WIDTH)), _full((1, SSD_WIDTH))],
        out_specs=[rev(SSD_WIDTH), rev(XBC_WIDTH), rev(LANES), _full((1, SSD_WIDTH)), vec, vec, vec],
        out_shape=[jax.ShapeDtypeStruct((T, SSD_WIDTH), bf16), jax.ShapeDtypeStruct((T, XBC_WIDTH), f32),
                   jax.ShapeDtypeStruct((T, LANES), bf16), jax.ShapeDtypeStruct((1, SSD_WIDTH), f32),
                   vshape, vshape, vshape],
        scratch_shapes=[pltpu.VMEM((SSD_STATE, SSD_WIDTH), f32)],
        params=_cparams(("arbitrary",), VMEM_MID), rider=rider)


def _in_proj_bwd(dz, dxbc, dcv, dcg, ddt, gin, x, dh1, g, rider=None):
    T = x.shape[0]
    tm = min(256, T)

    def body(dz_ref, dx_ref, dcv_ref, dcg_ref, ddt_ref, gin_ref, x_ref, dh_ref, g_ref, gx_ref, dg_ref, wt_ref):
        @pl.when(pl.program_id(0) == 0)
        def _():
            dg_ref[...] = jnp.zeros_like(dg_ref)
            _assemble_w_in_t(gin_ref, wt_ref)

        du = (_dot(dz_ref[...], wt_ref[0:O_XBC, :]) + _dot(dx_ref[...], wt_ref[O_XBC:O_DT, :])
              + _dot(dcv_ref[...], wt_ref[O_CV:O_CG, :]) + _dot(dcg_ref[...], wt_ref[O_CG:IN_WIDTH, :])
              + _dot(ddt_ref[...], wt_ref[O_DT:O_DT + LANES, :]))
        dx, dg = _rms_bwd(du, x_ref[...], g_ref[...])
        dg_ref[...] += dg
        gx_ref[...] = dh_ref[...] + dx

    row = lambda n: pl.BlockSpec((tm, n), lambda i: (i, 0))
    return _call(
        body, (dz, dxbc, dcv, dcg, ddt, gin, x, dh1, g), name="in_proj_bwd", grid=(T // tm,),
        in_specs=[row(SSD_WIDTH), row(XBC_WIDTH), row(CONF_WIDTH), row(CONF_WIDTH), row(LANES), _full(gin.shape),
                  row(D_MODEL), row(D_MODEL), _full((1, D_MODEL))],
        out_specs=[row(D_MODEL), _full((1, D_MODEL))],
        out_shape=[jax.ShapeDtypeStruct((T, D_MODEL), f32), jax.ShapeDtypeStruct((1, D_MODEL), f32)],
        scratch_shapes=[pltpu.VMEM((IN_WIDTH, D_MODEL), bf16)],
        params=_cparams(("arbitrary",), VMEM_BIG), rider=rider)


def _weight_grad(a, g, name, square=False, slab=None, place=None, tk=512):
    T, K = a.shape
    N = g.shape[1]
    tk = min(tk, K)
    tn = 1024 if N % 1024 == 0 else min(512, N)
    tt = min(2048, T)

    def body(a_ref, g_ref, *rest):
        o_ref = rest[-1]
        acc = _dot_tn(_operand(a_ref[...]), g_ref[...].astype(bf16))
        t = pl.program_id(2)
        shaped = acc if slab is None else acc[None]

        @pl.when(t == 0)
        def _():
            o_ref[...] = shaped

        @pl.when(t > 0)
        def _():
            o_ref[...] += shaped

    def _operand(av):
        if square:
            av = av.astype(f32)
            av = av * av
        return av.astype(bf16)

    in_specs = [pl.BlockSpec((tt, tk), lambda i, j, t: (t, i)), pl.BlockSpec((tt, tn), lambda i, j, t: (t, j))]
    grid = (K // tk, N // tn, T // tt)
    params = _cparams(("parallel", "parallel", "arbitrary"), VMEM_MID)
    if slab is None:
        return _pallas(
            body, pin_bytes=PIN_SMALL, name=name, grid=grid, in_specs=in_specs,
            out_specs=pl.BlockSpec((tk, tn), lambda i, j, t: (i, j)),
            out_shape=jax.ShapeDtypeStruct((K, N), f32), compiler_params=params,
        )(a, g)
    return _pallas(
        body, pin_bytes=PIN_SMALL, name=name, grid=grid, in_specs=in_specs + [ANY],
        out_specs=pl.BlockSpec((1, tk, tn), lambda i, j, t: place(i, j)),
        out_shape=jax.ShapeDtypeStruct(slab.shape, f32), input_output_aliases={2: 0}, compiler_params=params,
    )(a, g, slab)


def _place():
    return lax.axis_index("x"), lax.axis_index("y"), lax.axis_index("c")


def _other_chips(x, y):
    return [(1 - x, y), (x, 1 - y), (1 - x, 1 - y)]


def _remote(src, dst, ssem, rsem, dev):
    return pltpu.make_async_remote_copy(src_ref=src, dst_ref=dst, send_sem=ssem, recv_sem=rsem, device_id=dev,
                                        device_id_type=MESH)


def _gather_weights(arrays, convw):
    n = len(arrays)
    halves = tuple(a.shape[1] // 2 for a in arrays)

    def body(*refs):
        cw_ref, cwo_ref = refs[n], refs[2 * n + 1]
        ssem, rsem, lsem = refs[2 * n + 2:]
        triples = tuple(zip(refs[:n], refs[n + 1:2 * n + 1], halves))
        x, y, c = _place()
        me_b = 2 * x + y
        sib = (x, y, 1 - c)
        chips = _other_chips(x, y)
        loc = pltpu.make_async_copy(cw_ref, cwo_ref.at[me_b], lsem)
        loc.start()
        sends = []
        for j, (src, dst, h) in enumerate(triples):
            mine = pl.ds(c * h, h)
            for k, (px, py) in enumerate(chips):
                s = 6 * j + k
                sends.append(_remote(src.at[me_b, mine], dst.at[me_b, mine], ssem.at[s], rsem.at[s], (px, py, c)))
        for k, (px, py) in enumerate(chips):
            sends.append(_remote(cw_ref, cwo_ref.at[me_b], ssem.at[6 * n + k], rsem.at[6 * n + k], (px, py, c)))
        for cp in sends:
            cp.start()
        for j, (src, dst, h) in enumerate(triples):
            mine = pl.ds(c * h, h)
            for k, (px, py) in enumerate(chips):
                b = 2 * px + py
                s = 6 * j + k
                _remote(src.at[b, mine], dst.at[b, mine], ssem.at[s], rsem.at[s], (px, py, c)).wait_recv()
                fw = _remote(dst.at[b, mine], dst.at[b, mine], ssem.at[s + 3], rsem.at[s + 3], sib)
                fw.start()
                sends.append(fw)
        for k, (px, py) in enumerate(chips):
            b = 2 * px + py
            _remote(cw_ref, cwo_ref.at[b], ssem.at[6 * n + k], rsem.at[6 * n + k], (px, py, c)).wait_recv()
        for j, (src, dst, h) in enumerate(triples):
            theirs = pl.ds((1 - c) * h, h)
            for k, (px, py) in enumerate(chips):
                b = 2 * px + py
                s = 6 * j + k + 3
                _remote(src.at[b, theirs], dst.at[b, theirs], ssem.at[s], rsem.at[s], sib).wait_recv()
        for cp in sends:
            cp.wait_send()
        loc.wait()

    return _pallas(
        body, name="gather_weights", in_specs=[ANY] * (n + 1), out_specs=[ANY] * (n + 1),
        out_shape=[jax.ShapeDtypeStruct(a.shape, bf16) for a in arrays]
        + [jax.ShapeDtypeStruct((N_CHIPS, CONVW_ROWS, D_MODEL), f32)],
        input_output_aliases={j: j for j in range(n)},
        scratch_shapes=[pltpu.SemaphoreType.DMA((6 * n + 3,)), pltpu.SemaphoreType.DMA((6 * n + 3,)),
                        pltpu.SemaphoreType.DMA(())],
    )(*arrays, convw)


def _gather_rider(gath0, lo, n):
    h = gath0.shape[1] // 2

    def copies(rins, routs, ssem, rsem, sending):
        (g_ref,), (o_ref,) = rins, routs
        x, y, c = _place()
        mine = pl.ds(c * h + lo, n)
        for k, (px, py) in enumerate(_other_chips(x, y)):
            b = 2 * x + y if sending else 2 * px + py
            yield _remote(g_ref.at[b, mine], o_ref.at[b, mine], ssem.at[k], rsem.at[k], (px, py, c))

    def start(*refs):
        for cp in copies(*refs, sending=True):
            cp.start()

    def finish(*refs):
        for cp in copies(*refs, sending=False):
            cp.wait()

    return _Rider([gath0], [jax.ShapeDtypeStruct(gath0.shape, gath0.dtype)], {0: 0}, 3, start, finish)


def _forward_to_sibling(gath):
    h = gath.shape[1] // 2

    def body(g_ref, o_ref, ssem, rsem):
        x, y, c = _place()
        sib = (x, y, 1 - c)
        mine, theirs = pl.ds(c * h, h), pl.ds((1 - c) * h, h)
        blocks = [2 * px + py for px, py in _other_chips(x, y)]
        sends = [_remote(g_ref.at[b, mine], o_ref.at[b, mine], ssem.at[k], rsem.at[k], sib) for k, b in enumerate(blocks)]
        for cp in sends:
            cp.start()
        for k, b in enumerate(blocks):
            _remote(g_ref.at[b, theirs], o_ref.at[b, theirs], ssem.at[k], rsem.at[k], sib).wait_recv()
        for cp in sends:
            cp.wait_send()

    return _pallas(
        body, name="forward_to_sibling", in_specs=[ANY], out_specs=ANY,
        out_shape=jax.ShapeDtypeStruct(gath.shape, gath.dtype), input_output_aliases={0: 0},
        scratch_shapes=[pltpu.SemaphoreType.DMA((3,)), pltpu.SemaphoreType.DMA((3,))],
    )(gath)


def _swap_copy(g_ref, r_ref, ssem, rsem):
    x, y, c = _place()
    h = r_ref.shape[1]
    return _remote(g_ref.at[:, pl.ds((1 - c) * h, h), :], r_ref, ssem.at[0], rsem.at[0], (x, y, 1 - c))


def _swap_rider(g):
    def start(rins, routs, ssem, rsem):
        _swap_copy(rins[0], routs[0], ssem, rsem).start()

    def finish(rins, routs, ssem, rsem):
        _swap_copy(rins[0], routs[0], ssem, rsem).wait()

    return _Rider([g], [jax.ShapeDtypeStruct((N_CHIPS, g.shape[1] // 2, g.shape[2]), g.dtype)], {}, 1, start, finish)


def _swap_halves(g):
    def body(g_ref, r_ref, ssem, rsem):
        cp = _swap_copy(g_ref, r_ref, ssem, rsem)
        cp.start()
        cp.wait()

    return _pallas(
        body, name="swap_halves", in_specs=[ANY], out_specs=ANY,
        out_shape=jax.ShapeDtypeStruct((N_CHIPS, g.shape[1] // 2, g.shape[2]), g.dtype),
        scratch_shapes=[pltpu.SemaphoreType.DMA((1,)), pltpu.SemaphoreType.DMA((1,))],
    )(g)


def _chip_sum(cidx, gslab, recv, name):
    half, C = recv.shape[1:]
    tr = half // 2 if (half // 2) % 16 == 0 else half

    def body(c_ref, g_ref, r_ref, o_ref):
        o_ref[...] = (g_ref[...] + r_ref[...]).astype(bf16)

    return _pallas(
        body, name=name,
        grid_spec=pltpu.PrefetchScalarGridSpec(
            num_scalar_prefetch=1, grid=(N_CHIPS, half // tr),
            in_specs=[pl.BlockSpec((1, tr, C), lambda b, i, c_ref: (b, c_ref[0] * (half // tr) + i, 0)),
                      pl.BlockSpec((1, tr, C), lambda b, i, c_ref: (b, i, 0))],
            out_specs=pl.BlockSpec((1, tr, C), lambda b, i, c_ref: (b, i, 0))),
        out_shape=jax.ShapeDtypeStruct((N_CHIPS, half, C), bf16),
        compiler_params=_cparams(("parallel", "parallel"), VMEM_MID),
    )(cidx, gslab, recv)


def _exchange_rider(h, lo=0, n=None, recv=None):
    n = h.shape[1] - lo if n is None else n

    def copies(rins, routs, ssem, rsem):
        x, y, c = _place()
        rows = pl.ds(lo, n)
        for k, (px, py) in enumerate(_other_chips(x, y)):
            yield _remote(rins[0].at[2 * px + py, rows], routs[0].at[k, rows], ssem.at[k], rsem.at[k], (px, py, c))

    def start(*refs):
        for cp in copies(*refs):
            cp.start()

    def finish(*refs):
        for cp in copies(*refs):
            cp.wait()

    out = jax.ShapeDtypeStruct((3,) + h.shape[1:], h.dtype)
    if recv is None:
        return _Rider([h], [out], {}, 3, start, finish)
    return _Rider([h, recv], [out], {1: 0}, 3, start, finish)


def _gather_small(small):
    def body(sm_ref, all_ref, ssem, rsem, lsem):
        x, y, c = _place()
        me = 4 * x + 2 * y + c
        loc = pltpu.make_async_copy(sm_ref, all_ref.at[me], lsem)
        loc.start()
        sends, peers = [], []
        for r in range(1, N_DEV):
            peer = ((1 - x) if r & 4 else x, (1 - y) if r & 2 else y, (1 - c) if r & 1 else c)
            peers.append(peer)
            sends.append(_remote(sm_ref, all_ref.at[me], ssem.at[r - 1], rsem.at[r - 1], peer))
        for cp in sends:
            cp.start()
        for r, peer in zip(range(1, N_DEV), peers):
            pid = 4 * peer[0] + 2 * peer[1] + peer[2]
            _remote(sm_ref, all_ref.at[pid], ssem.at[r - 1], rsem.at[r - 1], peer).wait_recv()
        for cp in sends:
            cp.wait_send()
        loc.wait()

    return _pallas(
        body, name="gather_small", in_specs=[ANY], out_specs=ANY,
        out_shape=jax.ShapeDtypeStruct((N_DEV, SMALL_ROWS, D_MODEL), f32),
        scratch_shapes=[pltpu.SemaphoreType.DMA((7,)), pltpu.SemaphoreType.DMA((7,)), pltpu.SemaphoreType.DMA(())],
    )(small)


def _final_sum(idx, gslab, recv_sib, recv_ici, name):
    half, C = recv_sib.shape[1:]
    tr = half // 2 if (half // 2) % 16 == 0 else half

    def body(i_ref, g_ref, r_ref, p_ref, o_ref):
        acc = g_ref[0] + r_ref[0]
        for k in range(3):
            acc = acc + p_ref[k].astype(f32)
        o_ref[...] = acc

    return _pallas(
        body, name=name,
        grid_spec=pltpu.PrefetchScalarGridSpec(
            num_scalar_prefetch=1, grid=(half // tr,),
            in_specs=[pl.BlockSpec((1, tr, C), lambda i, s: (s[1], s[0] * (half // tr) + i, 0)),
                      pl.BlockSpec((1, tr, C), lambda i, s: (s[1], i, 0)),
                      pl.BlockSpec((3, tr, C), lambda i, s: (0, i, 0))],
            out_specs=pl.BlockSpec((tr, C), lambda i, s: (s[0] * (half // tr) + i, 0))),
        out_shape=jax.ShapeDtypeStruct((2 * half, C), f32),
        compiler_params=_cparams(("parallel",), VMEM_MID),
    )(idx, gslab, recv_sib, recv_ici)


def _join_halves(ra, rb):
    ha, hb = ra.shape[0] // 2, rb.shape[0] // 2

    def body(a_ref, b_ref, ao_ref, bo_ref, ssem, rsem):
        x, y, c = _place()
        sib = (x, y, 1 - c)
        mine_a, theirs_a = pl.ds(c * ha, ha), pl.ds((1 - c) * ha, ha)
        mine_b, theirs_b = pl.ds(c * hb, hb), pl.ds((1 - c) * hb, hb)
        ca = _remote(a_ref.at[mine_a], ao_ref.at[mine_a], ssem.at[0], rsem.at[0], sib)
        cb = _remote(b_ref.at[mine_b], bo_ref.at[mine_b], ssem.at[1], rsem.at[1], sib)
        ca.start()
        cb.start()
        _remote(a_ref.at[theirs_a], ao_ref.at[theirs_a], ssem.at[0], rsem.at[0], sib).wait_recv()
        _remote(b_ref.at[theirs_b], bo_ref.at[theirs_b], ssem.at[1], rsem.at[1], sib).wait_recv()
        ca.wait_send()
        cb.wait_send()

    return _pallas(
        body, name="join_halves", in_specs=[ANY, ANY], out_specs=[ANY, ANY],
        out_shape=[jax.ShapeDtypeStruct(ra.shape, f32), jax.ShapeDtypeStruct(rb.shape, f32)],
        input_output_aliases={0: 0, 1: 1},
        scratch_shapes=[pltpu.SemaphoreType.DMA((2,)), pltpu.SemaphoreType.DMA((2,))],
    )(ra, rb)


def _shard_rows(gt):
    def body(g_ref, o_ref):
        for b in range(N_CHIPS):
            o_ref[b, 0:W_IN_ROWS, :] = g_ref[b * W_IN_ROWS:(b + 1) * W_IN_ROWS, :]
            o_ref[b, W_IN_ROWS:W_IN_ROWS_PAD, :] = jnp.zeros((W_IN_ROWS_PAD - W_IN_ROWS, LANES), f32)

    return _pallas(
        body, name="shard_rows", grid=(D_MODEL // LANES,),
        in_specs=[pl.BlockSpec((IN_WIDTH, LANES), lambda i: (0, i))],
        out_specs=pl.BlockSpec((N_CHIPS, W_IN_ROWS_PAD, LANES), lambda i: (0, 0, i)),
        out_shape=jax.ShapeDtypeStruct((N_CHIPS, W_IN_ROWS_PAD, D_MODEL), f32),
        compiler_params=_cparams(("parallel",), VMEM_MID),
    )(gt)


def _sum_small(all_small):
    def body(a_ref, o_ref):
        acc = a_ref[0]
        for d in range(1, N_DEV):
            acc = acc + a_ref[d]
        o_ref[...] = acc

    return _pallas(
        body, name="sum_small", out_shape=jax.ShapeDtypeStruct((SMALL_ROWS, D_MODEL), f32),
    )(all_small)


def _adamw(w, g, m, v, name, g_off=0, by_columns=False):
    R, C = w.shape
    tr = 256 if R % 256 == 0 else R
    assert g_off % tr == 0 and not (by_columns and g_off)
    c1 = 1.0 - ADAM_B1 ** ADAM_STEP
    c2 = 1.0 - ADAM_B2 ** ADAM_STEP

    def body(w_ref, g_ref, m_ref, v_ref, d_ref, mo_ref, vo_ref):
        gg = g_ref[...]
        m2 = ADAM_B1 * m_ref[...] + (1.0 - ADAM_B1) * gg
        v2 = ADAM_B2 * v_ref[...] + (1.0 - ADAM_B2) * (gg * gg)
        mo_ref[...] = m2
        vo_ref[...] = v2
        d_ref[...] = -ADAM_LR * ((m2 / c1) / (jnp.sqrt(v2 / c2) + ADAM_EPS) + ADAM_WD * w_ref[...])

    if by_columns:
        blk = gblk = pl.BlockSpec((R, LANES), lambda i: (0, i))
        grid = (C // LANES,)
    else:
        blk = pl.BlockSpec((tr, C), lambda i: (i, 0))
        gblk = pl.BlockSpec((tr, C), lambda i: (g_off // tr + i, 0))
        grid = (R // tr,)
    shp = jax.ShapeDtypeStruct((R, C), f32)
    return _pallas(
        body, pin_bytes=PIN_SMALL, name=name, grid=grid, in_specs=[blk, gblk, blk, blk], out_specs=[blk] * 3,
        out_shape=[shp] * 3,
        compiler_params=_cparams(("parallel",), VMEM_MID),
    )(w, g, m, v)


def _pad_lanes(v):
    return jnp.pad(v, ((0, 0), (0, LANES - v.shape[1])))


def _local_step(x, p, tgt, gath0, cidx, gin, S):
    dtb = _pad_lanes(S["dt_bias"])
    alog = _pad_lanes(S["A_log"])
    dskip_e = jnp.repeat(S["D_skip"], HEAD_DIM, axis=1)

    early = GATHER_EARLY_ROWS
    u0, z, xbc, cv, cg, dtr, v, gath1 = _in_proj_fwd(x, S["mix_norm_g"], gin, rider=_gather_rider(gath0, 0, early))
    co, yc, gath = _conf_fwd(v, S["conf_dw_w"], S["conf_dw_b"], S["conf_ln_g"], S["conf_ln_b"],
                             rider=_gather_rider(gath1, early, SLAB_A // 2 - early))
    gath = _forward_to_sibling(gath)
    w_ple = jnp.concatenate([_ple_of_slab(gath[b]) for b in range(N_CHIPS)], axis=1)
    pre = _ssd_conv_fwd(xbc, S["ssd_conv_w"], S["ssd_conv_b"])
    y, ys, sprev = _ssd_fwd(pre, dtr, z, dtb, alog, dskip_e, S["ssd_norm_g"])
    h1, u1 = _out_proj_fwd(x, ys, yc, gath, S["mlp_norm_g"])
    r, h2, u2 = _mlp_fwd(h1, u1, gath, S["ple_gate_norm_g"])
    loss, dh2, dh2b, dgp, dep, dg_fin, dg_ple, db_pg, dg_pg = _ple_loss(
        h2, u2, p, tgt, gath, S["b_ple_gate"], w_ple, S["ple_norm_g"], S["final_norm_g"], S["ple_gate_norm_g"])

    npg = D_MODEL // N_CHIPS
    ga = lax.empty((N_CHIPS, SLAB_A, D_MODEL), f32)
    ga = _weight_grad(u2, dgp, "dw_ple_gate", slab=ga, tk=npg, place=lambda i, j: (i, PG_OFF // npg, j))
    ga = _weight_grad(r, dh2b, "dw_down", square=True, slab=ga, place=lambda i, j: (i // 2, DOWN_OFF // 512 + i % 2, j))
    gw_ple = _weight_grad(p, dep, "dw_ple")
    dhp, dh1, dh1b, dg_mlp = _mlp_bwd(dh2, r, gath, h1, S["mlp_norm_g"])
    ga = _weight_grad(u1, dhp, "dw_up", slab=ga, place=lambda i, j: (j, UP_OFF // 512 + i, 0))
    ga = _weight_grad(ys, dh1b, "dw_out_ssd", slab=ga, place=lambda i, j: (i, OUT_OFF // 512, j))
    ga = _weight_grad(yc, dh1b, "dw_out_conf", slab=ga, place=lambda i, j: (2 + i, OUT_OFF // 512, j))
    n_ple = D_MODEL // N_CHIPS
    ple_rows = jnp.stack([_rows(gw_ple[:, b * n_ple:(b + 1) * n_ple]) for b in range(N_CHIPS)], axis=0)
    ga = lax.dynamic_update_slice(ga, ple_rows, (0, PLE_OFF, 0))
    dys, dco, dg_ln, db_ln, recv_a = _out_proj_bwd(dh1, gath, co, S["conf_ln_g"], S["conf_ln_b"], rider=_swap_rider(ga))
    ha = _chip_sum(cidx, ga, recv_a, "chip_sum_a")
    first = EXCHANGE_FIRST_ROWS
    dcv, dcg, dw_conf, db_conf, ici_a = _conf_conv_bwd(dco, v, S["conf_dw_w"], cv, cg,
                                                       rider=_exchange_rider(ha, 0, first))
    dz, dpre, ddtr, dg_ssdn, dd, dal, ddtb, ici_a = _ssd_bwd(
        dys, y, z, pre, dtr, sprev, dtb, alog, dskip_e, S["ssd_norm_g"],
        rider=_exchange_rider(ha, first, SLAB_A // 2 - first, recv=ici_a))
    dxbc, dw_sconv, db_sconv = _ssd_conv_bwd(dpre, xbc, S["ssd_conv_w"])
    gw_in = jnp.concatenate([
        _weight_grad(dz, u0, "dw_in_z"), _weight_grad(dxbc, u0, "dw_in_xbc"),
        _weight_grad(ddtr, u0, "dw_in_dt")[:SSD_HEADS],
        _weight_grad(dcv, u0, "dw_in_cv"), _weight_grad(dcg, u0, "dw_in_cg")], axis=0)
    gb = _shard_rows(gw_in)
    recv_b = _swap_halves(gb)
    hb = _chip_sum(cidx, gb, recv_b, "chip_sum_b")
    gx, dg_mix, ici_b = _in_proj_bwd(dz, dxbc, dcv, dcg, ddtr, gin, x, dh1, S["mix_norm_g"], rider=_exchange_rider(hb))
    small = {
        "mix_norm_g": dg_mix, "ssd_conv_w": dw_sconv, "ssd_conv_b": db_sconv, "dt_bias": ddtb, "A_log": dal, "D_skip": dd,
        "ssd_norm_g": dg_ssdn, "conf_dw_w": dw_conf, "conf_dw_b": db_conf, "conf_ln_g": dg_ln, "conf_ln_b": db_ln,
        "mlp_norm_g": dg_mlp, "ple_gate_norm_g": dg_pg, "b_ple_gate": db_pg, "ple_norm_g": dg_ple,
        "final_norm_g": dg_fin, "loss": loss,
    }
    return gx, (ga, recv_a, ici_a), (gb, recv_b, ici_b), small


def _rows(a):
    return a.reshape(-1, D_MODEL)


def _pad_rows(a, n):
    flat = a.reshape(-1)
    return jnp.pad(flat, (0, n * D_MODEL - flat.shape[0])).reshape(n, D_MODEL)


def _ple_of_slab(slab):
    return slab[PLE_OFF:PLE_OFF + PLE_ROWS].reshape(PLE_DIM, D_MODEL // N_CHIPS)


ROW_VEC = {"mix_norm_g": 0, "ssd_norm_g": 1, "conf_dw_b": 2, "conf_ln_g": 3, "conf_ln_b": 4, "mlp_norm_g": 5,
           "ple_gate_norm_g": 6, "b_ple_gate": 7, "ple_norm_g": 8, "final_norm_g": 9}
ROW_CONV_B = 10
ROW_HEADS = 12
ROW_CONV_W = 16
ROW_DW = 24
HEAD_LANES = {"dt_bias": 0, "A_log": 1, "D_skip": 2, "loss": 3}
SMALL_ORDER = ("mix_norm_g", "ssd_conv_w", "ssd_conv_b", "dt_bias", "A_log", "D_skip", "ssd_norm_g", "conf_dw_w",
               "conf_dw_b", "conf_ln_g", "conf_ln_b", "mlp_norm_g", "ple_gate_norm_g", "b_ple_gate", "ple_norm_g",
               "final_norm_g")
SPLIT = XBC_WIDTH - D_MODEL


def _pack_small(raw):
    names = list(ROW_VEC) + ["ssd_conv_b", "dt_bias", "A_log", "D_skip", "loss", "ssd_conv_w", "conf_dw_w"]

    def body(*refs):
        r = dict(zip(names, refs[:-1]))
        o_ref = refs[-1]
        o_ref[...] = jnp.zeros_like(o_ref)
        for n, row in ROW_VEC.items():
            o_ref[row:row + 1, :] = r[n][...]
        o_ref[ROW_CONV_B:ROW_CONV_B + 1, :] = r["ssd_conv_b"][:, 0:D_MODEL]
        o_ref[ROW_CONV_B + 1:ROW_CONV_B + 2, 0:SPLIT] = r["ssd_conv_b"][:, D_MODEL:]
        for n, j in HEAD_LANES.items():
            o_ref[ROW_HEADS:ROW_HEADS + 1, j * LANES:(j + 1) * LANES] = r[n][0:1, :]
        for k in range(SSD_CONV):
            o_ref[ROW_CONV_W + 2 * k:ROW_CONV_W + 2 * k + 1, :] = r["ssd_conv_w"][k:k + 1, 0:D_MODEL]
            o_ref[ROW_CONV_W + 2 * k + 1:ROW_CONV_W + 2 * k + 2, 0:SPLIT] = r["ssd_conv_w"][k:k + 1, D_MODEL:]
        o_ref[ROW_DW:ROW_DW + 32, :] = r["conf_dw_w"][...]

    return _pallas(
        body, name="pack_small", out_shape=jax.ShapeDtypeStruct((SMALL_ROWS, D_MODEL), f32),
    )(*[raw[n] for n in names])


def _adamw_small(cidx, tot, w, m, v):
    c1 = 1.0 - ADAM_B1 ** ADAM_STEP
    c2 = 1.0 - ADAM_B2 ** ADAM_STEP
    n_par = len(SMALL_ORDER)

    def shard(full, chip, width):
        out = full[:, 0:width]
        for b in range(1, N_CHIPS):
            out = jnp.where(chip == b, full[:, b * width:(b + 1) * width], out)
        return out

    def grad_of(n, t_ref, chip):
        if n in ROW_VEC:
            return t_ref[ROW_VEC[n]:ROW_VEC[n] + 1, :]
        if n == "ssd_conv_b":
            return jnp.concatenate([t_ref[ROW_CONV_B:ROW_CONV_B + 1, :], t_ref[ROW_CONV_B + 1:ROW_CONV_B + 2, 0:SPLIT]], axis=1)
        if n in HEAD_LANES:
            j = HEAD_LANES[n]
            return t_ref[ROW_HEADS:ROW_HEADS + 1, j * LANES:j * LANES + SSD_HEADS]
        if n == "ssd_conv_w":
            rows = [jnp.concatenate([t_ref[ROW_CONV_W + 2 * k:ROW_CONV_W + 2 * k + 1, :],
                                     t_ref[ROW_CONV_W + 2 * k + 1:ROW_CONV_W + 2 * k + 2, 0:SPLIT]], axis=1)
                    for k in range(SSD_CONV)]
            return shard(jnp.concatenate(rows, axis=0), chip, XBC_WIDTH // N_CHIPS)
        return shard(t_ref[ROW_DW:ROW_DW + CONF_KERNEL, :], chip, CONF_WIDTH // N_CHIPS)

    def body(c_ref, t_ref, *refs):
        ins, outs = refs[:3 * n_par], refs[3 * n_par:]
        chip = c_ref[1]
        for i, n in enumerate(SMALL_ORDER):
            w_ref, m_ref, v_ref = ins[3 * i:3 * i + 3]
            g_ref, d_ref, mo_ref, vo_ref = outs[4 * i:4 * i + 4]
            g = grad_of(n, t_ref, chip)
            m2 = ADAM_B1 * m_ref[...] + (1.0 - ADAM_B1) * g
            v2 = ADAM_B2 * v_ref[...] + (1.0 - ADAM_B2) * (g * g)
            g_ref[...] = g
            mo_ref[...] = m2
            vo_ref[...] = v2
            d_ref[...] = -ADAM_LR * ((m2 / c1) / (jnp.sqrt(v2 / c2) + ADAM_EPS) + ADAM_WD * w_ref[...])

    args, in_specs, out_specs, out_shape = [], [], [], []
    for n in SMALL_ORDER:
        shp = w[n].shape
        spec = pl.BlockSpec(shp, lambda i, c_ref: (0, 0))
        args += [w[n], m[n], v[n]]
        in_specs += [spec] * 3
        out_specs += [spec] * 4
        out_shape += [jax.ShapeDtypeStruct(shp, f32)] * 4
    outs = _pallas(
        body, name="adamw_small",
        grid_spec=pltpu.PrefetchScalarGridSpec(
            num_scalar_prefetch=1, grid=(1,),
            in_specs=[pl.BlockSpec(tot.shape, lambda i, c_ref: (0, 0))] + in_specs, out_specs=out_specs),
        out_shape=out_shape,
    )(cidx, tot, *args)
    grad, delta, new_m, new_v = {}, {}, {}, {}
    for i, n in enumerate(SMALL_ORDER):
        grad[n], delta[n], new_m[n], new_v[n] = outs[4 * i:4 * i + 4]
    return grad, delta, new_m, new_v


BIG = ("w_in", "w_out", "w_up", "w_down", "w_ple_gate", "w_ple")
BIG_A = (("w_up", UP_OFF), ("w_down", DOWN_OFF), ("w_out", OUT_OFF), ("w_ple_gate", PG_OFF))
WEIGHTS = ("mix_norm_g", "w_in", "ssd_conv_w", "ssd_conv_b", "dt_bias", "A_log", "D_skip", "ssd_norm_g", "conf_dw_w",
           "conf_dw_b", "conf_ln_g", "conf_ln_b", "w_out", "mlp_norm_g", "w_up", "w_down", "ple_gate_norm_g",
           "w_ple_gate", "b_ple_gate", "w_ple", "ple_norm_g", "final_norm_g")


def kernel(x, p, mix_norm_g, w_in, ssd_conv_w, ssd_conv_b, dt_bias, A_log, D_skip, ssd_norm_g, conf_dw_w, conf_dw_b, conf_ln_g, conf_ln_b, w_out, mlp_norm_g, w_up, w_down, ple_gate_norm_g, w_ple_gate, b_ple_gate, w_ple, ple_norm_g, final_norm_g, loss_target, m_mix_norm_g, m_w_in, m_ssd_conv_w, m_ssd_conv_b, m_dt_bias, m_A_log, m_D_skip, m_ssd_norm_g, m_conf_dw_w, m_conf_dw_b, m_conf_ln_g, m_conf_ln_b, m_w_out, m_mlp_norm_g, m_w_up, m_w_down, m_ple_gate_norm_g, m_w_ple_gate, m_b_ple_gate, m_w_ple, m_ple_norm_g, m_final_norm_g, v_mix_norm_g, v_w_in, v_ssd_conv_w, v_ssd_conv_b, v_dt_bias, v_A_log, v_D_skip, v_ssd_norm_g, v_conf_dw_w, v_conf_dw_b, v_conf_ln_g, v_conf_ln_b, v_w_out, v_mlp_norm_g, v_w_up, v_w_down, v_ple_gate_norm_g, v_w_ple_gate, v_b_ple_gate, v_w_ple, v_ple_norm_g, v_final_norm_g):
    w = dict(mix_norm_g=mix_norm_g, w_in=w_in, ssd_conv_w=ssd_conv_w, ssd_conv_b=ssd_conv_b, dt_bias=dt_bias, A_log=A_log,
             D_skip=D_skip, ssd_norm_g=ssd_norm_g, conf_dw_w=conf_dw_w, conf_dw_b=conf_dw_b, conf_ln_g=conf_ln_g,
             conf_ln_b=conf_ln_b, w_out=w_out, mlp_norm_g=mlp_norm_g, w_up=w_up, w_down=w_down,
             ple_gate_norm_g=ple_gate_norm_g, w_ple_gate=w_ple_gate, b_ple_gate=b_ple_gate, w_ple=w_ple,
             ple_norm_g=ple_norm_g, final_norm_g=final_norm_g)
    m = dict(mix_norm_g=m_mix_norm_g, w_in=m_w_in, ssd_conv_w=m_ssd_conv_w, ssd_conv_b=m_ssd_conv_b, dt_bias=m_dt_bias,
             A_log=m_A_log, D_skip=m_D_skip, ssd_norm_g=m_ssd_norm_g, conf_dw_w=m_conf_dw_w, conf_dw_b=m_conf_dw_b,
             conf_ln_g=m_conf_ln_g, conf_ln_b=m_conf_ln_b, w_out=m_w_out, mlp_norm_g=m_mlp_norm_g, w_up=m_w_up,
             w_down=m_w_down, ple_gate_norm_g=m_ple_gate_norm_g, w_ple_gate=m_w_ple_gate, b_ple_gate=m_b_ple_gate,
             w_ple=m_w_ple, ple_norm_g=m_ple_norm_g, final_norm_g=m_final_norm_g)
    v = dict(mix_norm_g=v_mix_norm_g, w_in=v_w_in, ssd_conv_w=v_ssd_conv_w, ssd_conv_b=v_ssd_conv_b, dt_bias=v_dt_bias,
             A_log=v_A_log, D_skip=v_D_skip, ssd_norm_g=v_ssd_norm_g, conf_dw_w=v_conf_dw_w, conf_dw_b=v_conf_dw_b,
             conf_ln_g=v_conf_ln_g, conf_ln_b=v_conf_ln_b, w_out=v_w_out, mlp_norm_g=v_mlp_norm_g, w_up=v_w_up,
             w_down=v_w_down, ple_gate_norm_g=v_ple_gate_norm_g, w_ple_gate=v_w_ple_gate, b_ple_gate=v_b_ple_gate,
             w_ple=v_w_ple, ple_norm_g=v_ple_norm_g, final_norm_g=v_final_norm_g)
    xi, yi, ci = lax.axis_index("x"), lax.axis_index("y"), lax.axis_index("c")
    chip = 2 * xi + yi

    slab = jnp.concatenate([w_up[0], w_down[0], w_out[0], w_ple_gate[0], _rows(w_ple[0])], axis=0).astype(bf16)
    gath0 = lax.dynamic_update_slice(lax.empty((N_CHIPS, SLAB_A, D_MODEL), bf16), slab[None], (chip, 0, 0))
    wt_shard = jnp.swapaxes(w_in, 1, 2).astype(bf16)
    wt_shard = jnp.pad(wt_shard, ((0, 0), (0, W_IN_ROWS_PAD - W_IN_ROWS), (0, 0)))
    gin0 = lax.dynamic_update_slice(lax.empty((N_CHIPS, W_IN_ROWS_PAD, D_MODEL), bf16), wt_shard, (chip, 0, 0))
    convw = _pad_rows(jnp.concatenate([ssd_conv_w[0].reshape(-1), conf_dw_w[0].reshape(-1)]), CONVW_ROWS)
    gin, cwg = _gather_weights([gin0], convw)
    n_sc = SSD_CONV * (XBC_WIDTH // N_CHIPS)
    n_cf = CONF_KERNEL * (CONF_WIDTH // N_CHIPS)
    S = {n: w[n][0] for n in ("mix_norm_g", "ssd_conv_b", "dt_bias", "A_log", "D_skip", "ssd_norm_g", "conf_dw_b",
                              "conf_ln_g", "conf_ln_b", "mlp_norm_g", "ple_gate_norm_g", "b_ple_gate", "ple_norm_g")}
    S = {n: a.reshape(1, -1) for n, a in S.items()}
    S["final_norm_g"] = final_norm_g.reshape(1, -1)
    S["ssd_conv_w"] = jnp.concatenate(
        [cwg[b].reshape(-1)[:n_sc].reshape(SSD_CONV, XBC_WIDTH // N_CHIPS) for b in range(N_CHIPS)], axis=1)
    S["conf_dw_w"] = jnp.concatenate(
        [cwg[b].reshape(-1)[n_sc:n_sc + n_cf].reshape(CONF_KERNEL, CONF_WIDTH // N_CHIPS) for b in range(N_CHIPS)], axis=1)

    cidx = jnp.stack([ci, chip]).astype(jnp.int32)
    grad_x, (ga, recv_a, ici_a), (gb, recv_b, ici_b), gsmall = _local_step(
        x[0], p[0, 0], loss_target[0], gath0, cidx, gin, S)

    all_small = _gather_small(_pack_small(gsmall))
    ra = _final_sum(cidx, ga, recv_a, ici_a, "final_sum_a")
    rb = _final_sum(cidx, gb, recv_b, ici_b, "final_sum_b")
    ra, rb = _join_halves(ra, rb)
    tot_small = _sum_small(all_small)

    loss = tot_small[ROW_HEADS, HEAD_LANES["loss"] * LANES]

    two_d = lambda a: a.reshape(a.shape[-2:]) if a.ndim > 1 else a.reshape(1, -1)
    small_w, small_m, small_v = ({n: two_d(d[n]) for n in SMALL_ORDER} for d in (w, m, v))
    grads, delta, new_m, new_v = _adamw_small(cidx, tot_small, small_w, small_m, small_v)
    g_in_t = rb[:W_IN_ROWS]
    grads["w_ple"] = _ple_of_slab(ra)
    grads["w_in"] = jnp.swapaxes(g_in_t, 0, 1)
    for n, off in BIG_A:
        grads[n] = ra[off:off + w[n].shape[1]]
        delta[n], new_m[n], new_v[n] = _adamw(w[n][0], ra, m[n][0], v[n][0], "adamw_" + n, g_off=off)
    delta["w_ple"], new_m["w_ple"], new_v["w_ple"] = _adamw(w_ple[0], grads["w_ple"], m_w_ple[0], v_w_ple[0], "adamw_w_ple")
    tr_ = lambda a: jnp.swapaxes(a[0], 0, 1)
    d_, m_, v_ = _adamw(tr_(w_in), g_in_t, tr_(m_w_in), tr_(v_w_in), "adamw_w_in", by_columns=True)
    delta["w_in"], new_m["w_in"], new_v["w_in"] = (jnp.swapaxes(a, 0, 1) for a in (d_, m_, v_))

    shaped = lambda d: [d[n].reshape(w[n].shape) for n in WEIGHTS]
    return (loss, grad_x[None], *shaped(grads), *shaped(delta), *shaped(new_m), *shaped(new_v))
```

```python
import jax
import jax.numpy as jnp
from jax import lax
from jax.experimental import pallas as pl
from jax.experimental.pallas import tpu as pltpu

f32 = jnp.float32
bf16 = jnp.bfloat16

D_MODEL = 1024
SSD_WIDTH = 1024
SSD_HEADS = 16
HEAD_DIM = 64
SSD_STATE = 128
XBC_WIDTH = 1536
SSD_CONV = 4
CHUNK = 128
CONF_WIDTH = 1024
CONF_KERNEL = 31
D_FF = 4096
PLE_DIM = 256
IN_WIDTH = 4624
EPS = 1e-6
N_CHIPS = 4
N_DEV = 8

ADAM_LR = 0.001
ADAM_B1 = 0.9
ADAM_B2 = 0.999
ADAM_EPS = 1e-08
ADAM_WD = 0.01
ADAM_STEP = 10

LANES = 128
VMEM_BIG = 56 * 1024 * 1024
VMEM_MID = 40 * 1024 * 1024

UP_OFF, DOWN_OFF, OUT_OFF, PG_OFF, PLE_OFF = 0, 1024, 2048, 2560, 2816
PLE_ROWS = 64
SLAB_A = PLE_OFF + PLE_ROWS
GATHER_ROWS = (416, 720)
EXCHANGE_FIRST_ROWS = 960
W_IN_ROWS = 1156
W_IN_ROWS_PAD = 1184
CONVW_ROWS = 16
SMALL_ROWS = 56

MESH = pl.DeviceIdType.MESH
ANY = pl.BlockSpec(memory_space=pl.ANY)


PIN_SMALL = 256 * 1024


def _pallas(body, pin_bytes=None, **kw):
    call = pl.pallas_call(body, **kw)

    def pin(a):
        wanted = pin_bytes is None or a.size * a.dtype.itemsize <= pin_bytes
        return pltpu.with_memory_space_constraint(a, pltpu.HBM) if wanted and a.dtype != jnp.int32 else a

    def run(*args):
        return call(*[pin(a) for a in args])

    return run


def _cparams(sem=None, vmem=None):
    return pltpu.CompilerParams(dimension_semantics=sem, vmem_limit_bytes=vmem)


def _full(shape):
    n = len(shape)
    return pl.BlockSpec(shape, lambda *_: (0,) * n)


class _Rider:
    def __init__(self, inputs, out_shapes, aliases, n_sems, start, finish):
        self.inputs, self.out_shapes, self.aliases = list(inputs), list(out_shapes), dict(aliases)
        self.n_sems, self.start, self.finish = n_sems, start, finish


def _call(body, args, *, name, grid, in_specs, out_specs, out_shape, scratch_shapes=(), params=None, rider=None):
    if rider is None:
        return _pallas(body, name=name, grid=grid, in_specs=in_specs, out_specs=out_specs, out_shape=out_shape,
                              scratch_shapes=list(scratch_shapes), compiler_params=params)(*args)
    ni, no, ns = len(in_specs), len(out_specs), len(scratch_shapes)
    ri, ro = len(rider.inputs), len(rider.out_shapes)
    (steps,) = grid

    def with_rider(*refs):
        ins, refs = refs[:ni], refs[ni:]
        rins, refs = refs[:ri], refs[ri:]
        outs, refs = refs[:no], refs[no:]
        routs, refs = refs[:ro], refs[ro:]
        scratch, (ssem, rsem) = refs[:ns], refs[ns:]
        step = pl.program_id(0)

        @pl.when(step == 0)
        def _():
            rider.start(rins, routs, ssem, rsem)

        body(*ins, *outs, *scratch)

        @pl.when(step == steps - 1)
        def _():
            rider.finish(rins, routs, ssem, rsem)

    sems = [pltpu.SemaphoreType.DMA((rider.n_sems,)), pltpu.SemaphoreType.DMA((rider.n_sems,))]
    return _pallas(
        with_rider, name=name, grid=grid, in_specs=list(in_specs) + [ANY] * ri, out_specs=list(out_specs) + [ANY] * ro,
        out_shape=list(out_shape) + rider.out_shapes, scratch_shapes=list(scratch_shapes) + sems,
        input_output_aliases={ni + a: no + b for a, b in rider.aliases.items()}, compiler_params=params,
    )(*args, *rider.inputs)


def _dot(a, b):
    return jnp.dot(a, b, preferred_element_type=f32)


def _dot_nt(a, b):
    return lax.dot_general(a, b, (((1,), (1,)), ((), ())), preferred_element_type=f32)


def _dot_tn(a, b):
    return lax.dot_general(a, b, (((0,), (0,)), ((), ())), preferred_element_type=f32)


def _sigmoid(x):
    return jax.nn.sigmoid(x)


def _rms(x, g):
    r = lax.rsqrt(jnp.mean(x * x, axis=-1, keepdims=True) + EPS)
    return x * r * g


def _rms_bwd(dy, x, g):
    r = lax.rsqrt(jnp.mean(x * x, axis=-1, keepdims=True) + EPS)
    xh = x * r
    dg = jnp.sum(dy * xh, axis=0, keepdims=True)
    dxh = dy * g
    dx = r * (dxh - xh * jnp.mean(dxh * xh, axis=-1, keepdims=True))
    return dx, dg


def _dsilu(x):
    s = _sigmoid(x)
    return s * (1.0 + x * (1.0 - s))


def _split3(x):
    hi = x.astype(bf16)
    r1 = x - hi.astype(f32)
    mid = r1.astype(bf16)
    lo = (r1 - mid.astype(f32)).astype(bf16)
    return hi, mid, lo


def _head_matrix():
    row = lax.broadcasted_iota(jnp.int32, (LANES, SSD_WIDTH), 0)
    col = lax.broadcasted_iota(jnp.int32, (LANES, SSD_WIDTH), 1)
    lo = row * HEAD_DIM
    return ((col >= lo) & (col < lo + HEAD_DIM)).astype(bf16)


def _expand(x, e):
    hi, mid, lo = _split3(x)
    return _dot(hi, e) + _dot(mid, e) + _dot(lo, e)


def _contract(x, e):
    hi = x.astype(bf16)
    mid = (x - hi.astype(f32)).astype(bf16)
    return _dot_nt(hi, e) + _dot_nt(mid, e)


O_XBC = SSD_WIDTH
O_DT = O_XBC + XBC_WIDTH
O_CV = O_DT + SSD_HEADS
O_CG = O_CV + CONF_WIDTH


def _assemble_w_in_t(gin_ref, wt_ref):
    for b in range(N_CHIPS):
        wt_ref[b * W_IN_ROWS:(b + 1) * W_IN_ROWS, :] = gin_ref[b, 0:W_IN_ROWS, :]


def _in_proj_fwd(x, g, gin, rider=None):
    T = x.shape[0]
    tm = min(256, T)

    def body(x_ref, g_ref, gin_ref, u_ref, z_ref, xbc_ref, cv_ref, cg_ref, dt_ref, v_ref, wt_ref):
        @pl.when(pl.program_id(0) == 0)
        def _():
            _assemble_w_in_t(gin_ref, wt_ref)

        ub = _rms(x_ref[...], g_ref[...]).astype(bf16)
        u_ref[...] = ub
        z_ref[...] = _dot_nt(ub, wt_ref[0:O_XBC, :])
        xbc_ref[...] = _dot_nt(ub, wt_ref[O_XBC:O_DT, :])
        cv = _dot_nt(ub, wt_ref[O_CV:O_CG, :])
        cg = _dot_nt(ub, wt_ref[O_CG:IN_WIDTH, :])
        cv_ref[...] = cv
        cg_ref[...] = cg
        v_ref[...] = cv * _sigmoid(cg)
        dt_ref[...] = _dot_nt(ub, wt_ref[O_DT:O_DT + LANES, :])

    row = lambda n: pl.BlockSpec((tm, n), lambda i: (i, 0))
    return _call(
        body, (x, g, gin), name="in_proj_fwd", grid=(T // tm,),
        in_specs=[row(D_MODEL), _full((1, D_MODEL)), _full(gin.shape)],
        out_specs=[row(D_MODEL), row(SSD_WIDTH), row(XBC_WIDTH), row(CONF_WIDTH), row(CONF_WIDTH), row(LANES),
                   row(CONF_WIDTH)],
        out_shape=[jax.ShapeDtypeStruct((T, D_MODEL), bf16), jax.ShapeDtypeStruct((T, SSD_WIDTH), f32),
                   jax.ShapeDtypeStruct((T, XBC_WIDTH), f32), jax.ShapeDtypeStruct((T, CONF_WIDTH), f32),
                   jax.ShapeDtypeStruct((T, CONF_WIDTH), f32), jax.ShapeDtypeStruct((T, LANES), f32),
                   jax.ShapeDtypeStruct((T, CONF_WIDTH), f32)],
        scratch_shapes=[pltpu.VMEM((IN_WIDTH, D_MODEL), bf16)],
        params=_cparams(("arbitrary",), VMEM_BIG), rider=rider)


SUBLANES = 8


def _phases(offsets):
    return sorted({o % SUBLANES for o in offsets} - {0})


def _phase_shape(offsets, tm, C):
    a_max = max([o // SUBLANES for o in offsets if o % SUBLANES] or [0])
    return (max(len(_phases(offsets)), 1), tm + SUBLANES * a_max, C)


def _make_phases(buf_ref, ph_ref, offsets, tm):
    for idx, b in enumerate(_phases(offsets)):
        n = tm + SUBLANES * max(o // SUBLANES for o in offsets if o % SUBLANES == b)
        ph_ref[idx, 0:n, :] = buf_ref[pl.ds(b, n), :]


def _window(buf_ref, ph_ref, offsets, o, r0, rb):
    a, b = divmod(o, SUBLANES)
    if b == 0:
        return buf_ref[pl.ds(r0 + SUBLANES * a, rb), :]
    return ph_ref[_phases(offsets).index(b), pl.ds(r0 + SUBLANES * a, rb), :]


def _conv_rows(wb_ref, buf_ref, ph_ref, offsets, r0, rb):
    nsub = rb // SUBLANES
    accs = [None] * nsub
    for k, o in enumerate(offsets):
        wk = wb_ref[pl.ds(SUBLANES * k, SUBLANES), :]
        for s in range(nsub):
            term = wk * _window(buf_ref, ph_ref, offsets, o, r0 + SUBLANES * s, SUBLANES)
            accs[s] = term if accs[s] is None else accs[s] + term
    return accs[0] if nsub == 1 else jnp.concatenate(accs, axis=0)


def _sublane_rows(w):
    return jnp.repeat(w, SUBLANES, axis=0)


def _fwd_offsets(K, hb):
    return [hb - (K - 1) + k for k in range(K)]


def _prev_halo_spec(hb, tm, C):
    return pl.BlockSpec((hb, C), lambda i: (jnp.maximum(i * (tm // hb) - 1, 0), 0))


CONV_RB = 16


def _ssd_conv_fwd(xbc, w, b):
    T, C = xbc.shape
    K, hb = SSD_CONV, 8
    tm = min(256, T)
    offs = _fwd_offsets(K, hb)

    def body(cur_ref, halo_ref, w_ref, b_ref, pre_ref, buf_ref, ph_ref):
        keep = jnp.where(pl.program_id(0) > 0, 1.0, 0.0)
        buf_ref[0:hb, :] = halo_ref[...] * keep
        buf_ref[hb:hb + tm, :] = cur_ref[...]
        _make_phases(buf_ref, ph_ref, offs, tm)

        def chunk(i, carry):
            r0 = pl.multiple_of(i * CONV_RB, CONV_RB)
            pre_ref[pl.ds(r0, CONV_RB), :] = _conv_rows(w_ref, buf_ref, ph_ref, offs, r0, CONV_RB) + b_ref[...]
            return carry

        lax.fori_loop(0, tm // CONV_RB, chunk, 0)

    return _pallas(
        body, name="ssd_conv_fwd", grid=(T // tm,),
        in_specs=[pl.BlockSpec((tm, C), lambda i: (i, 0)), _prev_halo_spec(hb, tm, C), _full((SUBLANES * K, C)),
                  _full((1, C))],
        out_specs=pl.BlockSpec((tm, C), lambda i: (i, 0)),
        out_shape=jax.ShapeDtypeStruct((T, C), f32),
        scratch_shapes=[pltpu.VMEM((hb + tm, C), f32), pltpu.VMEM(_phase_shape(offs, tm, C), f32)],
        compiler_params=_cparams(("parallel",), VMEM_MID),
    )(xbc, xbc, _sublane_rows(w), b)


def _conf_fwd(v, w, b, ln_g, ln_b, rider=None):
    T, C = v.shape
    K, hb = CONF_KERNEL, 32
    tm = min(256, T)
    offs = _fwd_offsets(K, hb)
    rb = 2 * CONV_RB

    def body(cur_ref, halo_ref, w_ref, b_ref, g_ref, bb_ref, co_ref, y_ref, buf_ref, ph_ref):
        keep = jnp.where(pl.program_id(0) > 0, 1.0, 0.0)
        buf_ref[0:hb, :] = halo_ref[...] * keep
        buf_ref[hb:hb + tm, :] = cur_ref[...]
        _make_phases(buf_ref, ph_ref, offs, tm)

        def chunk(i, carry):
            r0 = pl.multiple_of(i * rb, rb)
            co = _conv_rows(w_ref, buf_ref, ph_ref, offs, r0, rb) + b_ref[...]
            co_ref[pl.ds(r0, rb), :] = co
            mu = jnp.mean(co, axis=-1, keepdims=True)
            xc = co - mu
            yn = xc * lax.rsqrt(jnp.mean(xc * xc, axis=-1, keepdims=True) + EPS) * g_ref[...] + bb_ref[...]
            y_ref[pl.ds(r0, rb), :] = (yn * _sigmoid(yn)).astype(bf16)
            return carry

        lax.fori_loop(0, tm // rb, chunk, 0)

    return _call(
        body, (v, v, _sublane_rows(w), b, ln_g, ln_b), name="conf_fwd", grid=(T // tm,),
        in_specs=[pl.BlockSpec((tm, C), lambda i: (i, 0)), _prev_halo_spec(hb, tm, C), _full((SUBLANES * K, C)),
                  _full((1, C)), _full((1, C)), _full((1, C))],
        out_specs=[pl.BlockSpec((tm, C), lambda i: (i, 0)), pl.BlockSpec((tm, C), lambda i: (i, 0))],
        out_shape=[jax.ShapeDtypeStruct((T, C), f32), jax.ShapeDtypeStruct((T, C), bf16)],
        scratch_shapes=[pltpu.VMEM((hb + tm, C), f32), pltpu.VMEM(_phase_shape(offs, tm, C), f32)],
        params=_cparams(("arbitrary",), VMEM_MID), rider=rider)


def _ssd_chunk_common(pre, dtr, dtb, alog, e):
    act = pre * _sigmoid(pre)
    xs = act[:, :SSD_WIDTH]
    bm = act[:, SSD_WIDTH:SSD_WIDTH + 2 * SSD_STATE]
    cm = act[:, SSD_WIDTH + 2 * SSD_STATE:]
    row = lax.broadcasted_iota(jnp.int32, (CHUNK, CHUNK), 0)
    col = lax.broadcasted_iota(jnp.int32, (CHUNK, CHUNK), 1)
    tri = row >= col
    dt = jax.nn.softplus(dtr + dtb)
    a_neg = -jnp.exp(alog)
    a = dt * a_neg
    cs = jnp.dot(tri.astype(f32), a, precision=lax.Precision.HIGHEST, preferred_element_type=f32)
    cs_e = _expand(cs, e)
    dt_e = _expand(dt, e)
    csl_e = cs_e[CHUNK - 1:CHUNK, :]
    ecs_e = jnp.exp(cs_e)
    dte_e = jnp.exp(csl_e - cs_e)
    cd_e = jnp.exp(csl_e)
    xc = xs * dt_e
    xd = xc * dte_e
    return dict(xs=xs, bm=bm, cm=cm, tri=tri, dt=dt, a_neg=a_neg, cs=cs, ecs_e=ecs_e, dte_e=dte_e, cd_e=cd_e,
                dt_e=dt_e, xc=xc, xd=xd)


def _group(v, g, width):
    return v[:, g * width:(g + 1) * width]


def _ssd_fwd(pre, dtr, z, dtb, alog, dskip_e, gn, rider=None):
    T = pre.shape[0]
    nc = T // CHUNK
    GW = SSD_WIDTH // 2

    def body(pre_ref, dtr_ref, z_ref, dtb_ref, alog_ref, de_ref, gn_ref, y_ref, ys_ref, sp_ref, st_ref):
        @pl.when(pl.program_id(0) == 0)
        def _():
            st_ref[...] = jnp.zeros_like(st_ref)

        e = _head_matrix()
        q = _ssd_chunk_common(pre_ref[...], dtr_ref[...], dtb_ref[...], alog_ref[...], e)
        cs, tri, xc, xd = q["cs"], q["tri"], q["xc"], q["xd"]
        cs_t = cs.T
        st = st_ref[...]
        sp_ref[0] = st
        lane = lax.broadcasted_iota(jnp.int32, (1, LANES), 1)
        halves = (lane < HEAD_DIM, lane >= HEAD_DIM)

        g_mat, y_off, s_new = [], [], []
        for g in range(2):
            bg = _group(q["bm"], g, SSD_STATE)
            cg = _group(q["cm"], g, SSD_STATE)
            bgb, cgb = bg.astype(bf16), cg.astype(bf16)
            g_mat.append(_dot_nt(cgb, bgb))
            y_off.append(_dot(cgb, _group(st, g, GW).astype(bf16)))
            s_new.append(_dot(bg.T.astype(bf16), _group(xd, g, GW).astype(bf16)))
        y_off = jnp.concatenate(y_off, axis=1) * q["ecs_e"]
        st_ref[...] = st * q["cd_e"] + jnp.concatenate(s_new, axis=1)

        pairs = []
        for j in range(SSD_HEADS // 2):
            xp = xc[:, j * LANES:(j + 1) * LANES]
            acc = jnp.zeros((CHUNK, LANES), f32)
            for hh in range(2):
                h = 2 * j + hh
                seg = cs[:, h:h + 1] - cs_t[h:h + 1, :]
                lm = jnp.exp(jnp.where(tri, seg, -1e30))
                m = (g_mat[h // 8] * lm).astype(bf16)
                acc = acc + _dot(m, jnp.where(halves[hh], xp, 0.0).astype(bf16))
            pairs.append(acc)
        y = jnp.concatenate(pairs, axis=1) + y_off + q["xs"] * de_ref[...]
        y_ref[...] = y

        zz = z_ref[...]
        v = y * (zz * _sigmoid(zz))
        outs = []
        for g in range(2):
            vg = _group(v, g, GW)
            outs.append(vg * lax.rsqrt(jnp.mean(vg * vg, axis=-1, keepdims=True) + EPS))
        ys_ref[...] = (jnp.concatenate(outs, axis=1) * gn_ref[...]).astype(bf16)

    ch = lambda n: pl.BlockSpec((CHUNK, n), lambda c: (c, 0))
    return _call(
        body, (pre, dtr, z, dtb, alog, dskip_e, gn), name="ssd_fwd", grid=(nc,),
        in_specs=[ch(XBC_WIDTH), ch(LANES), ch(SSD_WIDTH), _full((1, LANES)), _full((1, LANES)), _full((1, SSD_WIDTH)),
                  _full((1, SSD_WIDTH))],
        out_specs=[ch(SSD_WIDTH), ch(SSD_WIDTH), pl.BlockSpec((1, SSD_STATE, SSD_WIDTH), lambda c: (c, 0, 0))],
        out_shape=[jax.ShapeDtypeStruct((T, SSD_WIDTH), f32), jax.ShapeDtypeStruct((T, SSD_WIDTH), bf16),
                   jax.ShapeDtypeStruct((nc, SSD_STATE, SSD_WIDTH), f32)],
        scratch_shapes=[pltpu.VMEM((SSD_STATE, SSD_WIDTH), f32)],
        params=_cparams(("arbitrary",), VMEM_MID), rider=rider)


def _w_out_spec():
    n = 2 * SSD_WIDTH // N_CHIPS
    return pl.BlockSpec((N_CHIPS, n, D_MODEL), lambda *_: (0, OUT_OFF // n, 0))


def _out_proj_fwd(x, ys, yc, gath, g):
    T = x.shape[0]
    tm = min(512, T)
    n = 2 * SSD_WIDTH // N_CHIPS

    def body(x_ref, ys_ref, yc_ref, w_ref, g_ref, h_ref, u_ref):
        h = (x_ref[...] + _dot(ys_ref[:, 0:n], w_ref[0]) + _dot(ys_ref[:, n:], w_ref[1])
             + _dot(yc_ref[:, 0:n], w_ref[2]) + _dot(yc_ref[:, n:], w_ref[3]))
        h_ref[...] = h
        u_ref[...] = _rms(h, g_ref[...]).astype(bf16)

    row = pl.BlockSpec((tm, D_MODEL), lambda i: (i, 0))
    return _pallas(
        body, name="out_proj_fwd", grid=(T // tm,),
        in_specs=[row, row, row, _w_out_spec(), _full((1, D_MODEL))],
        out_specs=[row, row],
        out_shape=[jax.ShapeDtypeStruct((T, D_MODEL), f32), jax.ShapeDtypeStruct((T, D_MODEL), bf16)],
        compiler_params=_cparams(("parallel",), VMEM_MID),
    )(x, ys, yc, gath, g)


def _w_up_spec():
    return pl.BlockSpec((1, D_MODEL, D_MODEL), lambda i, b: (b, UP_OFF // D_MODEL, 0))


def _w_down_spec():
    return pl.BlockSpec((1, D_MODEL, D_MODEL), lambda i, b: (b, DOWN_OFF // D_MODEL, 0))


def _mlp_fwd(h1, u1, gath, g_next):
    T = h1.shape[0]
    tm = min(512, T)
    nb = D_FF // D_MODEL

    def body(h_ref, u_ref, wu_ref, wd_ref, g_ref, r_ref, h2_ref, u2_ref, acc_ref):
        b = pl.program_id(1)

        @pl.when(b == 0)
        def _():
            acc_ref[...] = jnp.zeros_like(acc_ref)

        r = jnp.maximum(_dot(u_ref[...], wu_ref[0]), 0.0)
        r_ref[...] = r.astype(bf16)
        acc_ref[...] += _dot((r * r).astype(bf16), wd_ref[0])

        @pl.when(b == nb - 1)
        def _():
            h2 = h_ref[...] + acc_ref[...]
            h2_ref[...] = h2
            u2_ref[...] = _rms(h2, g_ref[...]).astype(bf16)

    row = pl.BlockSpec((tm, D_MODEL), lambda i, b: (i, 0))
    return _pallas(
        body, name="mlp_fwd", grid=(T // tm, nb),
        in_specs=[row, row, _w_up_spec(), _w_down_spec(), _full((1, D_MODEL))],
        out_specs=[pl.BlockSpec((tm, D_MODEL), lambda i, b: (i, b)), row, row],
        out_shape=[jax.ShapeDtypeStruct((T, D_FF), bf16), jax.ShapeDtypeStruct((T, D_MODEL), f32),
                   jax.ShapeDtypeStruct((T, D_MODEL), bf16)],
        scratch_shapes=[pltpu.VMEM((tm, D_MODEL), f32)],
        compiler_params=_cparams(("parallel", "arbitrary"), VMEM_MID),
    )(h1, u1, gath, gath, g_next)


def _ple_loss(h2, u2, p, tgt, gath, b_pg, w_ple, g_ple, g_fin, g_pg):
    T = h2.shape[0]
    tm = min(256, T)
    npg = D_MODEL // N_CHIPS

    def body(h2_ref, u2_ref, p_ref, t_ref, wpg_ref, bpg_ref, wple_ref, gple_ref, gfin_ref, gpg_ref,
             loss_ref, dh2_ref, dh2b_ref, dgp_ref, dep_ref, dgfin_ref, dgple_ref, dbpg_ref, dgpg_ref):
        @pl.when(pl.program_id(0) == 0)
        def _():
            loss_ref[...] = jnp.zeros_like(loss_ref)
            dgfin_ref[...] = jnp.zeros_like(dgfin_ref)
            dgple_ref[...] = jnp.zeros_like(dgple_ref)
            dbpg_ref[...] = jnp.zeros_like(dbpg_ref)
            dgpg_ref[...] = jnp.zeros_like(dgpg_ref)

        h2 = h2_ref[...]
        gate_pre = bpg_ref[...]
        for b in range(N_CHIPS):
            gate_pre = gate_pre + _dot(u2_ref[:, b * npg:(b + 1) * npg], wpg_ref[b])
        gate = _sigmoid(gate_pre)
        e_pre = _dot(p_ref[...].astype(bf16), wple_ref[...])
        emb = _rms(e_pre, gple_ref[...])
        h3 = h2 + gate * emb
        diff = _rms(h3, gfin_ref[...]) - t_ref[...]
        sq = jnp.sum(jnp.sum(diff * diff, axis=1, keepdims=True), axis=0, keepdims=True)
        loss_ref[...] += (0.5 / D_MODEL) * sq
        dh3, dgfin = _rms_bwd(diff * (1.0 / D_MODEL), h3, gfin_ref[...])
        dgfin_ref[...] += dgfin
        dgp = dh3 * emb * gate * (1.0 - gate)
        dbpg_ref[...] += jnp.sum(dgp, axis=0, keepdims=True)
        dep, dgple = _rms_bwd(dh3 * gate, e_pre, gple_ref[...])
        dgple_ref[...] += dgple
        dgpb = dgp.astype(bf16)
        dgp_ref[...] = dgpb
        dep_ref[...] = dep.astype(bf16)
        du2 = jnp.concatenate([_dot_nt(dgpb, wpg_ref[b]) for b in range(N_CHIPS)], axis=1)
        dx, dgpg = _rms_bwd(du2, h2, gpg_ref[...])
        dgpg_ref[...] += dgpg
        dh2 = dh3 + dx
        dh2_ref[...] = dh2
        dh2b_ref[...] = dh2.astype(bf16)

    row = pl.BlockSpec((tm, D_MODEL), lambda i: (i, 0))
    vec = _full((1, D_MODEL))
    vshape = jax.ShapeDtypeStruct((1, D_MODEL), f32)
    return _pallas(
        body, name="ple_loss", grid=(T // tm,),
        in_specs=[row, row, pl.BlockSpec((tm, PLE_DIM), lambda i: (i, 0)), row,
                  pl.BlockSpec((N_CHIPS, npg, D_MODEL), lambda i: (0, PG_OFF // npg, 0)), vec, _full(w_ple.shape),
                  vec, vec, vec],
        out_specs=[_full((8, LANES)), row, row, row, row, vec, vec, vec, vec],
        out_shape=[jax.ShapeDtypeStruct((8, LANES), f32), jax.ShapeDtypeStruct((T, D_MODEL), f32),
                   jax.ShapeDtypeStruct((T, D_MODEL), bf16), jax.ShapeDtypeStruct((T, D_MODEL), bf16),
                   jax.ShapeDtypeStruct((T, D_MODEL), bf16), vshape, vshape, vshape, vshape],
        compiler_params=_cparams(("arbitrary",), VMEM_MID),
    )(h2, u2, p, tgt, gath, b_pg, w_ple, g_ple, g_fin, g_pg)


def _mlp_bwd(dh2, r, gath, h1, g):
    T = dh2.shape[0]
    tm = min(512, T)
    nb = D_FF // D_MODEL

    def body(dh2_ref, r_ref, wd_ref, wu_ref, h1_ref, g_ref, dhp_ref, dh1_ref, dh1b_ref, dg_ref, acc_ref):
        i, b = pl.program_id(0), pl.program_id(1)

        @pl.when(b == 0)
        def _():
            acc_ref[...] = jnp.zeros_like(acc_ref)

        @pl.when((b == 0) & (i == 0))
        def _():
            dg_ref[...] = jnp.zeros_like(dg_ref)

        dact = _dot_nt(dh2_ref[...].astype(bf16), wd_ref[0])
        dhp = (dact * 2.0 * r_ref[...].astype(f32)).astype(bf16)
        dhp_ref[...] = dhp
        acc_ref[...] += _dot_nt(dhp, wu_ref[0])

        @pl.when(b == nb - 1)
        def _():
            dx, dg = _rms_bwd(acc_ref[...], h1_ref[...], g_ref[...])
            dg_ref[...] += dg
            dh1 = dh2_ref[...] + dx
            dh1_ref[...] = dh1
            dh1b_ref[...] = dh1.astype(bf16)

    row = pl.BlockSpec((tm, D_MODEL), lambda i, b: (i, 0))
    return _pallas(
        body, name="mlp_bwd", grid=(T // tm, nb),
        in_specs=[row, pl.BlockSpec((tm, D_MODEL), lambda i, b: (i, b)), _w_down_spec(), _w_up_spec(), row,
                  _full((1, D_MODEL))],
        out_specs=[pl.BlockSpec((tm, D_MODEL), lambda i, b: (i, b)), row, row, _full((1, D_MODEL))],
        out_shape=[jax.ShapeDtypeStruct((T, D_FF), bf16), jax.ShapeDtypeStruct((T, D_MODEL), f32),
                   jax.ShapeDtypeStruct((T, D_MODEL), bf16), jax.ShapeDtypeStruct((1, D_MODEL), f32)],
        scratch_shapes=[pltpu.VMEM((tm, D_MODEL), f32)],
        compiler_params=_cparams(("arbitrary", "arbitrary"), VMEM_MID),
    )(dh2, r, gath, gath, h1, g)


def _out_proj_bwd(dh1, gath, co, ln_g, ln_b, rider=None):
    T = dh1.shape[0]
    tm = min(512, T)

    def body(dh_ref, w_ref, co_ref, g_ref, b_ref, dys_ref, dco_ref, dg_ref, db_ref):
        @pl.when(pl.program_id(0) == 0)
        def _():
            dg_ref[...] = jnp.zeros_like(dg_ref)
            db_ref[...] = jnp.zeros_like(db_ref)

        dhb = dh_ref[...].astype(bf16)
        dys_ref[...] = jnp.concatenate([_dot_nt(dhb, w_ref[0]), _dot_nt(dhb, w_ref[1])], axis=1)
        dyc = jnp.concatenate([_dot_nt(dhb, w_ref[2]), _dot_nt(dhb, w_ref[3])], axis=1)
        co = co_ref[...]
        mu = jnp.mean(co, axis=-1, keepdims=True)
        xc = co - mu
        rstd = lax.rsqrt(jnp.mean(xc * xc, axis=-1, keepdims=True) + EPS)
        xh = xc * rstd
        yn = xh * g_ref[...] + b_ref[...]
        dyn = dyc * _dsilu(yn)
        dg_ref[...] += jnp.sum(dyn * xh, axis=0, keepdims=True)
        db_ref[...] += jnp.sum(dyn, axis=0, keepdims=True)
        dxh = dyn * g_ref[...]
        dco_ref[...] = rstd * (dxh - jnp.mean(dxh, axis=-1, keepdims=True)
                               - xh * jnp.mean(dxh * xh, axis=-1, keepdims=True))

    row = pl.BlockSpec((tm, D_MODEL), lambda i: (i, 0))
    vec = _full((1, CONF_WIDTH))
    vshape = jax.ShapeDtypeStruct((1, CONF_WIDTH), f32)
    return _call(
        body, (dh1, gath, co, ln_g, ln_b), name="out_proj_bwd", grid=(T // tm,),
        in_specs=[row, _w_out_spec(), row, vec, vec],
        out_specs=[row, row, vec, vec],
        out_shape=[jax.ShapeDtypeStruct((T, SSD_WIDTH), f32), jax.ShapeDtypeStruct((T, CONF_WIDTH), f32), vshape, vshape],
        params=_cparams(("arbitrary",), VMEM_MID), rider=rider)


def _bwd_offsets(K):
    return [K - 1 - k for k in range(K)]


def _next_halo_spec(hb, tm, C, T):
    return pl.BlockSpec((hb, C), lambda i: (jnp.minimum((i + 1) * (tm // hb), T // hb - 1), 0))


DW_RB = 8
DW_UNROLL = 16
DW_ACC_VREGS = 32


def _conv_dw(dw_ref, bufd_ref, bufx_ref, phx_ref, offs_x, tm, C):
    K = len(offs_x)
    group = max(1, DW_ACC_VREGS // (C // LANES))
    for k0 in range(0, K, group):
        ks = list(range(k0, min(k0 + group, K)))

        def step(i, accs, ks=ks):
            for u in range(DW_UNROLL):
                r0 = pl.multiple_of((i * DW_UNROLL + u) * DW_RB, DW_RB)
                d = bufd_ref[pl.ds(r0, DW_RB), :]
                accs = tuple(acc + _window(bufx_ref, phx_ref, offs_x, offs_x[k], r0, DW_RB) * d
                             for k, acc in zip(ks, accs))
            return accs

        accs = lax.fori_loop(0, tm // (DW_RB * DW_UNROLL), step, tuple(jnp.zeros((DW_RB, C), f32) for _ in ks))
        for k, acc in zip(ks, accs):
            dw_ref[k:k + 1, :] += jnp.sum(acc, axis=0, keepdims=True)


def _fill_bwd_buffers(dcur_ref, dnext_ref, xcur_ref, xprev_ref, bufd_ref, bufx_ref, phd_ref, phx_ref, offs_d, offs_x,
                      hb, tm, first, last):
    bufd_ref[0:tm, :] = dcur_ref[...]
    bufd_ref[tm:tm + hb, :] = dnext_ref[...] * jnp.where(last, 0.0, 1.0)
    bufx_ref[0:hb, :] = xprev_ref[...] * jnp.where(first, 0.0, 1.0)
    bufx_ref[hb:hb + tm, :] = xcur_ref[...]
    _make_phases(bufd_ref, phd_ref, offs_d, tm)
    _make_phases(bufx_ref, phx_ref, offs_x, tm)


def _ssd_conv_bwd(dpre, xbc, w):
    T, C = xbc.shape
    K, hb = SSD_CONV, 8
    tm = min(256, T)
    nt = T // tm
    offs_d, offs_x = _bwd_offsets(K), _fwd_offsets(K, hb)

    def body(dcur_ref, dnext_ref, xcur_ref, xprev_ref, w_ref, dx_ref, dw_ref, db_ref, bufd_ref, bufx_ref, phd_ref, phx_ref):
        i = pl.program_id(0)

        @pl.when(i == 0)
        def _():
            dw_ref[...] = jnp.zeros_like(dw_ref)
            db_ref[...] = jnp.zeros_like(db_ref)

        _fill_bwd_buffers(dcur_ref, dnext_ref, xcur_ref, xprev_ref, bufd_ref, bufx_ref, phd_ref, phx_ref, offs_d, offs_x,
                          hb, tm, i == 0, i == nt - 1)

        def chunk(j, carry):
            r0 = pl.multiple_of(j * CONV_RB, CONV_RB)
            dx_ref[pl.ds(r0, CONV_RB), :] = _conv_rows(w_ref, bufd_ref, phd_ref, offs_d, r0, CONV_RB).astype(bf16)
            return carry

        lax.fori_loop(0, tm // CONV_RB, chunk, 0)
        _conv_dw(dw_ref, bufd_ref, bufx_ref, phx_ref, offs_x, tm, C)
        db_ref[...] += jnp.sum(dcur_ref[...], axis=0, keepdims=True)

    row = pl.BlockSpec((tm, C), lambda i: (i, 0))
    return _pallas(
        body, name="ssd_conv_bwd", grid=(nt,),
        in_specs=[row, _next_halo_spec(hb, tm, C, T), row, _prev_halo_spec(hb, tm, C), _full((SUBLANES * K, C))],
        out_specs=[row, _full((8, C)), _full((1, C))],
        out_shape=[jax.ShapeDtypeStruct((T, C), bf16), jax.ShapeDtypeStruct((8, C), f32), jax.ShapeDtypeStruct((1, C), f32)],
        scratch_shapes=[pltpu.VMEM((tm + hb, C), f32), pltpu.VMEM((hb + tm, C), f32),
                        pltpu.VMEM(_phase_shape(offs_d, tm, C), f32),
                        pltpu.VMEM(_phase_shape(offs_x, tm, C), f32)],
        compiler_params=_cparams(("arbitrary",), VMEM_BIG),
    )(dpre, dpre, xbc, xbc, _sublane_rows(w))


def _conf_conv_bwd(dco, v, w, cv, cg, rider=None):
    T, C = v.shape
    K, hb = CONF_KERNEL, 32
    tm = min(256, T)
    nt = T // tm
    offs_d, offs_x = _bwd_offsets(K), _fwd_offsets(K, hb)

    def body(dcur_ref, dnext_ref, vcur_ref, vprev_ref, w_ref, cv_ref, cg_ref, dcv_ref, dcg_ref, dw_ref, db_ref,
             bufd_ref, bufx_ref, phd_ref, phx_ref):
        i = pl.program_id(0)

        @pl.when(i == 0)
        def _():
            dw_ref[...] = jnp.zeros_like(dw_ref)
            db_ref[...] = jnp.zeros_like(db_ref)

        _fill_bwd_buffers(dcur_ref, dnext_ref, vcur_ref, vprev_ref, bufd_ref, bufx_ref, phd_ref, phx_ref, offs_d, offs_x,
                          hb, tm, i == 0, i == nt - 1)

        def chunk(j, carry):
            r0 = pl.multiple_of(j * CONV_RB, CONV_RB)
            rows = pl.ds(r0, CONV_RB)
            dv = _conv_rows(w_ref, bufd_ref, phd_ref, offs_d, r0, CONV_RB)
            s = _sigmoid(cg_ref[rows, :])
            dcv_ref[rows, :] = (dv * s).astype(bf16)
            dcg_ref[rows, :] = (dv * cv_ref[rows, :] * s * (1.0 - s)).astype(bf16)
            return carry

        lax.fori_loop(0, tm // CONV_RB, chunk, 0)
        _conv_dw(dw_ref, bufd_ref, bufx_ref, phx_ref, offs_x, tm, C)
        db_ref[...] += jnp.sum(dcur_ref[...], axis=0, keepdims=True)

    row = pl.BlockSpec((tm, C), lambda i: (i, 0))
    return _call(
        body, (dco, dco, v, v, _sublane_rows(w), cv, cg), name="conf_conv_bwd", grid=(nt,),
        in_specs=[row, _next_halo_spec(hb, tm, C, T), row, _prev_halo_spec(hb, tm, C), _full((SUBLANES * K, C)), row, row],
        out_specs=[row, row, _full((32, C)), _full((1, C))],
        out_shape=[jax.ShapeDtypeStruct((T, C), bf16), jax.ShapeDtypeStruct((T, C), bf16),
                   jax.ShapeDtypeStruct((32, C), f32), jax.ShapeDtypeStruct((1, C), f32)],
        scratch_shapes=[pltpu.VMEM((tm + hb, C), f32), pltpu.VMEM((hb + tm, C), f32),
                        pltpu.VMEM(_phase_shape(offs_d, tm, C), f32),
                        pltpu.VMEM(_phase_shape(offs_x, tm, C), f32)],
        params=_cparams(("arbitrary",), VMEM_BIG), rider=rider)


def _ssd_bwd(dys, y, z, pre, dtr, sprev, dtb, alog, dskip_e, gn, rider=None):
    T = pre.shape[0]
    nc = T // CHUNK
    GW = SSD_WIDTH // 2

    def body(dys_ref, y_ref, z_ref, pre_ref, dtr_ref, sp_ref, dtb_ref, alog_ref, de_ref, gn_ref,
             dz_ref, dpre_ref, ddtr_ref, dgn_ref, dd_ref, dal_ref, ddtb_ref, ds_ref):
        @pl.when(pl.program_id(0) == 0)
        def _():
            ds_ref[...] = jnp.zeros_like(ds_ref)
            dgn_ref[...] = jnp.zeros_like(dgn_ref)
            dd_ref[...] = jnp.zeros_like(dd_ref)
            dal_ref[...] = jnp.zeros_like(dal_ref)
            ddtb_ref[...] = jnp.zeros_like(ddtb_ref)

        e = _head_matrix()
        pre = pre_ref[...]
        dtr_b = dtr_ref[...] + dtb_ref[...]
        q = _ssd_chunk_common(pre, dtr_ref[...], dtb_ref[...], alog_ref[...], e)
        cs, tri, xc, xd, xs, dt = q["cs"], q["tri"], q["xc"], q["xd"], q["xs"], q["dt"]
        cs_t = cs.T
        st = sp_ref[0]
        dsn = ds_ref[...]
        lane = lax.broadcasted_iota(jnp.int32, (1, LANES), 1)
        halves = (lane < HEAD_DIM, lane >= HEAD_DIM)
        row_i = lax.broadcasted_iota(jnp.int32, (CHUNK, CHUNK), 0)
        col_i = lax.broadcasted_iota(jnp.int32, (CHUNK, CHUNK), 1)
        tri_t = col_i >= row_i

        y = y_ref[...]
        zz = z_ref[...]
        sz = _sigmoid(zz)
        silu_z = zz * sz
        v = y * silu_z
        dout = dys_ref[...]
        gn_v = gn_ref[...]
        dv, vh = [], []
        for g in range(2):
            vg = _group(v, g, GW)
            rstd = lax.rsqrt(jnp.mean(vg * vg, axis=-1, keepdims=True) + EPS)
            vhg = vg * rstd
            dvh = _group(dout, g, GW) * _group(gn_v, g, GW)
            dv.append(rstd * (dvh - vhg * jnp.mean(dvh * vhg, axis=-1, keepdims=True)))
            vh.append(vhg)
        dv = jnp.concatenate(dv, axis=1)
        dgn_ref[...] += jnp.sum(dout * jnp.concatenate(vh, axis=1), axis=0, keepdims=True)
        dy = dv * silu_z
        dz_ref[...] = (dv * y * (sz * (1.0 + zz * (1.0 - sz)))).astype(bf16)

        dd_row = jnp.sum(dy * xs, axis=0, keepdims=True)
        dd_ref[...] += _contract(jnp.broadcast_to(dd_row, (8, SSD_WIDTH)), e)[0:1, :]
        dxs = dy * de_ref[...]

        dz_in = dy * q["ecs_e"]
        g_mat, gt_mat, dcm, dbm, dsp, dxd, y_off = [], [], [], [], [], [], []
        bgs, cgs = [], []
        for g in range(2):
            bg = _group(q["bm"], g, SSD_STATE)
            cg = _group(q["cm"], g, SSD_STATE)
            bgb, cgb = bg.astype(bf16), cg.astype(bf16)
            bgs.append(bgb)
            cgs.append(cgb)
            stg = _group(st, g, GW).astype(bf16)
            dsng = _group(dsn, g, GW).astype(bf16)
            dzg = _group(dz_in, g, GW).astype(bf16)
            g_mat.append(_dot_nt(cgb, bgb))
            gt_mat.append(_dot_nt(bgb, cgb))
            y_off.append(_dot(cgb, stg))
            dcm.append(_dot_nt(dzg, stg))
            dsp.append(_dot(cg.T.astype(bf16), dzg))
            dbm.append(_dot_nt(_group(xd, g, GW).astype(bf16), dsng))
            dxd.append(_dot(bgb, dsng))
        y_off = jnp.concatenate(y_off, axis=1) * q["ecs_e"]
        dxd = jnp.concatenate(dxd, axis=1)
        ds_ref[...] = dsn * q["cd_e"] + jnp.concatenate(dsp, axis=1)
        dcd_row = jnp.sum(dsn * st, axis=0, keepdims=True) * q["cd_e"]
        t_e = dxd * xd
        dcs = _contract(dy * y_off - t_e, e)
        last_row = _contract(jnp.broadcast_to(dcd_row + jnp.sum(t_e, axis=0, keepdims=True), (8, SSD_WIDTH)), e)[0:1, :]
        dxc_state = dxd * q["dte_e"]

        dg_acc = [jnp.zeros((CHUNK, CHUNK), f32), jnp.zeros((CHUNK, CHUNK), f32)]
        dgt_acc = [jnp.zeros((CHUNK, CHUNK), f32), jnp.zeros((CHUNK, CHUNK), f32)]
        dxc_pairs = []
        for j in range(SSD_HEADS // 2):
            dyp_f = dy[:, j * LANES:(j + 1) * LANES]
            xcp_f = xc[:, j * LANES:(j + 1) * LANES]
            acc = jnp.zeros((CHUNK, LANES), f32)
            for hh in range(2):
                h = 2 * j + hh
                g = h // 8
                dyp = jnp.where(halves[hh], dyp_f, 0.0).astype(bf16)
                xcp = jnp.where(halves[hh], xcp_f, 0.0).astype(bf16)
                lm = jnp.exp(jnp.where(tri, cs[:, h:h + 1] - cs_t[h:h + 1, :], -1e30))
                lm_t = jnp.exp(jnp.where(tri_t, cs_t[h:h + 1, :] - cs[:, h:h + 1], -1e30))
                dm = _dot_nt(dyp, xcp) * lm
                dm_t = _dot_nt(xcp, dyp) * lm_t
                acc = acc + _dot((gt_mat[g] * lm_t).astype(bf16), dyp)
                dg_acc[g] = dg_acc[g] + dm
                dgt_acc[g] = dgt_acc[g] + dm_t
                qd = jnp.sum(dm * g_mat[g] - dm_t * gt_mat[g], axis=1, keepdims=True)
                dcs = dcs + qd * (lane == h).astype(f32)
            dxc_pairs.append(acc)
        dxc = jnp.concatenate(dxc_pairs, axis=1) + dxc_state
        for g in range(2):
            dcm[g] = dcm[g] + _dot(dg_acc[g].astype(bf16), bgs[g])
            dbm[g] = dbm[g] + _dot(dgt_acc[g].astype(bf16), cgs[g])

        dxs = dxs + dxc * q["dt_e"]
        ddt = _contract(dxc * xs, e)
        dcs = dcs + jnp.where(row_i == CHUNK - 1, jnp.broadcast_to(last_row, (CHUNK, LANES)), 0.0)
        da = jnp.dot(tri_t.astype(f32), dcs, precision=lax.Precision.HIGHEST, preferred_element_type=f32)
        ddt = ddt + da * q["a_neg"]
        dal_ref[...] += jnp.sum(da * dt, axis=0, keepdims=True) * q["a_neg"]
        ddtr = ddt * _sigmoid(dtr_b) * (lane < SSD_HEADS).astype(f32)
        ddtb_ref[...] += jnp.sum(ddtr, axis=0, keepdims=True)
        ddtr_ref[...] = ddtr.astype(bf16)

        dact = jnp.concatenate([dxs, dbm[0], dbm[1], dcm[0], dcm[1]], axis=1)
        dpre_ref[...] = dact * _dsilu(pre)

    rev = lambda n: pl.BlockSpec((CHUNK, n), lambda c: (nc - 1 - c, 0))
    vec = _full((1, LANES))
    vshape = jax.ShapeDtypeStruct((1, LANES), f32)
    return _call(
        body, (dys, y, z, pre, dtr, sprev, dtb, alog, dskip_e, gn), name="ssd_bwd", grid=(nc,),
        in_specs=[rev(SSD_WIDTH), rev(SSD_WIDTH), rev(SSD_WIDTH), rev(XBC_WIDTH), rev(LANES),
                  pl.BlockSpec((1, SSD_STATE, SSD_WIDTH), lambda c: (nc - 1 - c, 0, 0)),
                  vec, vec, _full((1, SSD_WIDTH)), _full((1, SSD_WIDTH))],
        out_specs=[rev(SSD_WIDTH), rev(XBC_WIDTH), rev(LANES), _full((1, SSD_WIDTH)), vec, vec, vec],
        out_shape=[jax.ShapeDtypeStruct((T, SSD_WIDTH), bf16), jax.ShapeDtypeStruct((T, XBC_WIDTH), f32),
                   jax.ShapeDtypeStruct((T, LANES), bf16), jax.ShapeDtypeStruct((1, SSD_WIDTH), f32),
                   vshape, vshape, vshape],
        scratch_shapes=[pltpu.VMEM((SSD_STATE, SSD_WIDTH), f32)],
        params=_cparams(("arbitrary",), VMEM_MID), rider=rider)


def _in_proj_bwd(dz, dxbc, dcv, dcg, ddt, gin, x, dh1, g, rider=None):
    T = x.shape[0]
    tm = min(256, T)

    def body(dz_ref, dx_ref, dcv_ref, dcg_ref, ddt_ref, gin_ref, x_ref, dh_ref, g_ref, gx_ref, dg_ref, wt_ref):
        @pl.when(pl.program_id(0) == 0)
        def _():
            dg_ref[...] = jnp.zeros_like(dg_ref)
            _assemble_w_in_t(gin_ref, wt_ref)

        du = (_dot(dz_ref[...], wt_ref[0:O_XBC, :]) + _dot(dx_ref[...], wt_ref[O_XBC:O_DT, :])
              + _dot(dcv_ref[...], wt_ref[O_CV:O_CG, :]) + _dot(dcg_ref[...], wt_ref[O_CG:IN_WIDTH, :])
              + _dot(ddt_ref[...], wt_ref[O_DT:O_DT + LANES, :]))
        dx, dg = _rms_bwd(du, x_ref[...], g_ref[...])
        dg_ref[...] += dg
        gx_ref[...] = dh_ref[...] + dx

    row = lambda n: pl.BlockSpec((tm, n), lambda i: (i, 0))
    return _call(
        body, (dz, dxbc, dcv, dcg, ddt, gin, x, dh1, g), name="in_proj_bwd", grid=(T // tm,),
        in_specs=[row(SSD_WIDTH), row(XBC_WIDTH), row(CONF_WIDTH), row(CONF_WIDTH), row(LANES), _full(gin.shape),
                  row(D_MODEL), row(D_MODEL), _full((1, D_MODEL))],
        out_specs=[row(D_MODEL), _full((1, D_MODEL))],
        out_shape=[jax.ShapeDtypeStruct((T, D_MODEL), f32), jax.ShapeDtypeStruct((1, D_MODEL), f32)],
        scratch_shapes=[pltpu.VMEM((IN_WIDTH, D_MODEL), bf16)],
        params=_cparams(("arbitrary",), VMEM_BIG), rider=rider)


def _weight_grad(a, g, name, square=False, slab=None, place=None, tk=512):
    T, K = a.shape
    N = g.shape[1]
    tk = min(tk, K)
    tn = 1024 if N % 1024 == 0 else min(512, N)
    tt = min(2048, T)

    def body(a_ref, g_ref, *rest):
        o_ref = rest[-1]
        acc = _dot_tn(_operand(a_ref[...]), g_ref[...].astype(bf16))
        t = pl.program_id(2)
        shaped = acc if slab is None else acc[None]

        @pl.when(t == 0)
        def _():
            o_ref[...] = shaped

        @pl.when(t > 0)
        def _():
            o_ref[...] += shaped

    def _operand(av):
        if square:
            av = av.astype(f32)
            av = av * av
        return av.astype(bf16)

    in_specs = [pl.BlockSpec((tt, tk), lambda i, j, t: (t, i)), pl.BlockSpec((tt, tn), lambda i, j, t: (t, j))]
    grid = (K // tk, N // tn, T // tt)
    params = _cparams(("parallel", "parallel", "arbitrary"), VMEM_MID)
    if slab is None:
        return _pallas(
            body, pin_bytes=PIN_SMALL, name=name, grid=grid, in_specs=in_specs,
            out_specs=pl.BlockSpec((tk, tn), lambda i, j, t: (i, j)),
            out_shape=jax.ShapeDtypeStruct((K, N), f32), compiler_params=params,
        )(a, g)
    return _pallas(
        body, pin_bytes=PIN_SMALL, name=name, grid=grid, in_specs=in_specs + [ANY],
        out_specs=pl.BlockSpec((1, tk, tn), lambda i, j, t: place(i, j)),
        out_shape=jax.ShapeDtypeStruct(slab.shape, f32), input_output_aliases={2: 0}, compiler_params=params,
    )(a, g, slab)


def _place():
    return lax.axis_index("x"), lax.axis_index("y"), lax.axis_index("c")


def _other_chips(x, y):
    return [(1 - x, y), (x, 1 - y), (1 - x, 1 - y)]


def _remote(src, dst, ssem, rsem, dev):
    return pltpu.make_async_remote_copy(src_ref=src, dst_ref=dst, send_sem=ssem, recv_sem=rsem, device_id=dev,
                                        device_id_type=MESH)


def _gather_weights(arrays, convw):
    n = len(arrays)
    halves = tuple(a.shape[1] // 2 for a in arrays)

    def body(*refs):
        cw_ref, cwo_ref = refs[n], refs[2 * n + 1]
        ssem, rsem, lsem = refs[2 * n + 2:]
        triples = tuple(zip(refs[:n], refs[n + 1:2 * n + 1], halves))
        x, y, c = _place()
        me_b = 2 * x + y
        sib = (x, y, 1 - c)
        chips = _other_chips(x, y)
        loc = pltpu.make_async_copy(cw_ref, cwo_ref.at[me_b], lsem)
        loc.start()
        sends = []
        for j, (src, dst, h) in enumerate(triples):
            mine = pl.ds(c * h, h)
            for k, (px, py) in enumerate(chips):
                s = 6 * j + k
                sends.append(_remote(src.at[me_b, mine], dst.at[me_b, mine], ssem.at[s], rsem.at[s], (px, py, c)))
        for k, (px, py) in enumerate(chips):
            sends.append(_remote(cw_ref, cwo_ref.at[me_b], ssem.at[6 * n + k], rsem.at[6 * n + k], (px, py, c)))
        for cp in sends:
            cp.start()
        for j, (src, dst, h) in enumerate(triples):
            mine = pl.ds(c * h, h)
            for k, (px, py) in enumerate(chips):
                b = 2 * px + py
                s = 6 * j + k
                _remote(src.at[b, mine], dst.at[b, mine], ssem.at[s], rsem.at[s], (px, py, c)).wait_recv()
                fw = _remote(dst.at[b, mine], dst.at[b, mine], ssem.at[s + 3], rsem.at[s + 3], sib)
                fw.start()
                sends.append(fw)
        for k, (px, py) in enumerate(chips):
            b = 2 * px + py
            _remote(cw_ref, cwo_ref.at[b], ssem.at[6 * n + k], rsem.at[6 * n + k], (px, py, c)).wait_recv()
        for j, (src, dst, h) in enumerate(triples):
            theirs = pl.ds((1 - c) * h, h)
            for k, (px, py) in enumerate(chips):
                b = 2 * px + py
                s = 6 * j + k + 3
                _remote(src.at[b, theirs], dst.at[b, theirs], ssem.at[s], rsem.at[s], sib).wait_recv()
        for cp in sends:
            cp.wait_send()
        loc.wait()

    return _pallas(
        body, name="gather_weights", in_specs=[ANY] * (n + 1), out_specs=[ANY] * (n + 1),
        out_shape=[jax.ShapeDtypeStruct(a.shape, bf16) for a in arrays]
        + [jax.ShapeDtypeStruct((N_CHIPS, CONVW_ROWS, D_MODEL), f32)],
        input_output_aliases={j: j for j in range(n)},
        scratch_shapes=[pltpu.SemaphoreType.DMA((6 * n + 3,)), pltpu.SemaphoreType.DMA((6 * n + 3,)),
                        pltpu.SemaphoreType.DMA(())],
    )(*arrays, convw)


def _gather_rider(gath0, lo, n):
    h = gath0.shape[1] // 2

    def copies(rins, routs, ssem, rsem, sending):
        (g_ref,), (o_ref,) = rins, routs
        x, y, c = _place()
        mine = pl.ds(c * h + lo, n)
        for k, (px, py) in enumerate(_other_chips(x, y)):
            b = 2 * x + y if sending else 2 * px + py
            yield _remote(g_ref.at[b, mine], o_ref.at[b, mine], ssem.at[k], rsem.at[k], (px, py, c))

    def start(*refs):
        for cp in copies(*refs, sending=True):
            cp.start()

    def finish(*refs):
        for cp in copies(*refs, sending=False):
            cp.wait()

    return _Rider([gath0], [jax.ShapeDtypeStruct(gath0.shape, gath0.dtype)], {0: 0}, 3, start, finish)


def _forward_to_sibling(gath):
    h = gath.shape[1] // 2

    def body(g_ref, o_ref, ssem, rsem):
        x, y, c = _place()
        sib = (x, y, 1 - c)
        mine, theirs = pl.ds(c * h, h), pl.ds((1 - c) * h, h)
        blocks = [2 * px + py for px, py in _other_chips(x, y)]
        sends = [_remote(g_ref.at[b, mine], o_ref.at[b, mine], ssem.at[k], rsem.at[k], sib) for k, b in enumerate(blocks)]
        for cp in sends:
            cp.start()
        for k, b in enumerate(blocks):
            _remote(g_ref.at[b, theirs], o_ref.at[b, theirs], ssem.at[k], rsem.at[k], sib).wait_recv()
        for cp in sends:
            cp.wait_send()

    return _pallas(
        body, name="forward_to_sibling", in_specs=[ANY], out_specs=ANY,
        out_shape=jax.ShapeDtypeStruct(gath.shape, gath.dtype), input_output_aliases={0: 0},
        scratch_shapes=[pltpu.SemaphoreType.DMA((3,)), pltpu.SemaphoreType.DMA((3,))],
    )(gath)


def _swap_copy(g_ref, r_ref, ssem, rsem):
    x, y, c = _place()
    h = r_ref.shape[1]
    return _remote(g_ref.at[:, pl.ds((1 - c) * h, h), :], r_ref, ssem.at[0], rsem.at[0], (x, y, 1 - c))


def _swap_rider(g):
    def start(rins, routs, ssem, rsem):
        _swap_copy(rins[0], routs[0], ssem, rsem).start()

    def finish(rins, routs, ssem, rsem):
        _swap_copy(rins[0], routs[0], ssem, rsem).wait()

    return _Rider([g], [jax.ShapeDtypeStruct((N_CHIPS, g.shape[1] // 2, g.shape[2]), g.dtype)], {}, 1, start, finish)


def _swap_halves(g):
    def body(g_ref, r_ref, ssem, rsem):
        cp = _swap_copy(g_ref, r_ref, ssem, rsem)
        cp.start()
        cp.wait()

    return _pallas(
        body, name="swap_halves", in_specs=[ANY], out_specs=ANY,
        out_shape=jax.ShapeDtypeStruct((N_CHIPS, g.shape[1] // 2, g.shape[2]), g.dtype),
        scratch_shapes=[pltpu.SemaphoreType.DMA((1,)), pltpu.SemaphoreType.DMA((1,))],
    )(g)


def _chip_sum(cidx, gslab, recv, name):
    half, C = recv.shape[1:]
    tr = half // 2 if (half // 2) % 16 == 0 else half

    def body(c_ref, g_ref, r_ref, o_ref):
        o_ref[...] = (g_ref[...] + r_ref[...]).astype(bf16)

    return _pallas(
        body, name=name,
        grid_spec=pltpu.PrefetchScalarGridSpec(
            num_scalar_prefetch=1, grid=(N_CHIPS, half // tr),
            in_specs=[pl.BlockSpec((1, tr, C), lambda b, i, c_ref: (b, c_ref[0] * (half // tr) + i, 0)),
                      pl.BlockSpec((1, tr, C), lambda b, i, c_ref: (b, i, 0))],
            out_specs=pl.BlockSpec((1, tr, C), lambda b, i, c_ref: (b, i, 0))),
        out_shape=jax.ShapeDtypeStruct((N_CHIPS, half, C), bf16),
        compiler_params=_cparams(("parallel", "parallel"), VMEM_MID),
    )(cidx, gslab, recv)


def _exchange_rider(h, lo=0, n=None, recv=None):
    n = h.shape[1] - lo if n is None else n

    def copies(rins, routs, ssem, rsem):
        x, y, c = _place()
        rows = pl.ds(lo, n)
        for k, (px, py) in enumerate(_other_chips(x, y)):
            yield _remote(rins[0].at[2 * px + py, rows], routs[0].at[k, rows], ssem.at[k], rsem.at[k], (px, py, c))

    def start(*refs):
        for cp in copies(*refs):
            cp.start()

    def finish(*refs):
        for cp in copies(*refs):
            cp.wait()

    out = jax.ShapeDtypeStruct((3,) + h.shape[1:], h.dtype)
    if recv is None:
        return _Rider([h], [out], {}, 3, start, finish)
    return _Rider([h, recv], [out], {1: 0}, 3, start, finish)


def _gather_small(small):
    def body(sm_ref, all_ref, ssem, rsem, lsem):
        x, y, c = _place()
        me = 4 * x + 2 * y + c
        loc = pltpu.make_async_copy(sm_ref, all_ref.at[me], lsem)
        loc.start()
        sends, peers = [], []
        for r in range(1, N_DEV):
            peer = ((1 - x) if r & 4 else x, (1 - y) if r & 2 else y, (1 - c) if r & 1 else c)
            peers.append(peer)
            sends.append(_remote(sm_ref, all_ref.at[me], ssem.at[r - 1], rsem.at[r - 1], peer))
        for cp in sends:
            cp.start()
        for r, peer in zip(range(1, N_DEV), peers):
            pid = 4 * peer[0] + 2 * peer[1] + peer[2]
            _remote(sm_ref, all_ref.at[pid], ssem.at[r - 1], rsem.at[r - 1], peer).wait_recv()
        for cp in sends:
            cp.wait_send()
        loc.wait()

    return _pallas(
        body, name="gather_small", in_specs=[ANY], out_specs=ANY,
        out_shape=jax.ShapeDtypeStruct((N_DEV, SMALL_ROWS, D_MODEL), f32),
        scratch_shapes=[pltpu.SemaphoreType.DMA((7,)), pltpu.SemaphoreType.DMA((7,)), pltpu.SemaphoreType.DMA(())],
    )(small)


def _final_sum(idx, gslab, recv_sib, recv_ici, name):
    half, C = recv_sib.shape[1:]
    tr = half // 2 if (half // 2) % 16 == 0 else half

    def body(i_ref, g_ref, r_ref, p_ref, o_ref):
        acc = g_ref[0] + r_ref[0]
        for k in range(3):
            acc = acc + p_ref[k].astype(f32)
        o_ref[...] = acc

    return _pallas(
        body, name=name,
        grid_spec=pltpu.PrefetchScalarGridSpec(
            num_scalar_prefetch=1, grid=(half // tr,),
            in_specs=[pl.BlockSpec((1, tr, C), lambda i, s: (s[1], s[0] * (half // tr) + i, 0)),
                      pl.BlockSpec((1, tr, C), lambda i, s: (s[1], i, 0)),
                      pl.BlockSpec((3, tr, C), lambda i, s: (0, i, 0))],
            out_specs=pl.BlockSpec((tr, C), lambda i, s: (s[0] * (half // tr) + i, 0))),
        out_shape=jax.ShapeDtypeStruct((2 * half, C), f32),
        compiler_params=_cparams(("parallel",), VMEM_MID),
    )(idx, gslab, recv_sib, recv_ici)


def _join_halves(ra, rb):
    ha, hb = ra.shape[0] // 2, rb.shape[0] // 2

    def body(a_ref, b_ref, ao_ref, bo_ref, ssem, rsem):
        x, y, c = _place()
        sib = (x, y, 1 - c)
        mine_a, theirs_a = pl.ds(c * ha, ha), pl.ds((1 - c) * ha, ha)
        mine_b, theirs_b = pl.ds(c * hb, hb), pl.ds((1 - c) * hb, hb)
        ca = _remote(a_ref.at[mine_a], ao_ref.at[mine_a], ssem.at[0], rsem.at[0], sib)
        cb = _remote(b_ref.at[mine_b], bo_ref.at[mine_b], ssem.at[1], rsem.at[1], sib)
        ca.start()
        cb.start()
        _remote(a_ref.at[theirs_a], ao_ref.at[theirs_a], ssem.at[0], rsem.at[0], sib).wait_recv()
        _remote(b_ref.at[theirs_b], bo_ref.at[theirs_b], ssem.at[1], rsem.at[1], sib).wait_recv()
        ca.wait_send()
        cb.wait_send()

    return _pallas(
        body, name="join_halves", in_specs=[ANY, ANY], out_specs=[ANY, ANY],
        out_shape=[jax.ShapeDtypeStruct(ra.shape, f32), jax.ShapeDtypeStruct(rb.shape, f32)],
        input_output_aliases={0: 0, 1: 1},
        scratch_shapes=[pltpu.SemaphoreType.DMA((2,)), pltpu.SemaphoreType.DMA((2,))],
    )(ra, rb)


def _shard_rows(gt):
    def body(g_ref, o_ref):
        for b in range(N_CHIPS):
            o_ref[b, 0:W_IN_ROWS, :] = g_ref[b * W_IN_ROWS:(b + 1) * W_IN_ROWS, :]
            o_ref[b, W_IN_ROWS:W_IN_ROWS_PAD, :] = jnp.zeros((W_IN_ROWS_PAD - W_IN_ROWS, LANES), f32)

    return _pallas(
        body, name="shard_rows", grid=(D_MODEL // LANES,),
        in_specs=[pl.BlockSpec((IN_WIDTH, LANES), lambda i: (0, i))],
        out_specs=pl.BlockSpec((N_CHIPS, W_IN_ROWS_PAD, LANES), lambda i: (0, 0, i)),
        out_shape=jax.ShapeDtypeStruct((N_CHIPS, W_IN_ROWS_PAD, D_MODEL), f32),
        compiler_params=_cparams(("parallel",), VMEM_MID),
    )(gt)


def _sum_small(all_small):
    def body(a_ref, o_ref):
        acc = a_ref[0]
        for d in range(1, N_DEV):
            acc = acc + a_ref[d]
        o_ref[...] = acc

    return _pallas(
        body, name="sum_small", out_shape=jax.ShapeDtypeStruct((SMALL_ROWS, D_MODEL), f32),
    )(all_small)


def _adamw(w, g, m, v, name, g_off=0, by_columns=False):
    R, C = w.shape
    tr = 256 if R % 256 == 0 else R
    assert g_off % tr == 0 and not (by_columns and g_off)
    c1 = 1.0 - ADAM_B1 ** ADAM_STEP
    c2 = 1.0 - ADAM_B2 ** ADAM_STEP

    def body(w_ref, g_ref, m_ref, v_ref, d_ref, mo_ref, vo_ref):
        gg = g_ref[...]
        m2 = ADAM_B1 * m_ref[...] + (1.0 - ADAM_B1) * gg
        v2 = ADAM_B2 * v_ref[...] + (1.0 - ADAM_B2) * (gg * gg)
        mo_ref[...] = m2
        vo_ref[...] = v2
        d_ref[...] = -ADAM_LR * ((m2 / c1) / (jnp.sqrt(v2 / c2) + ADAM_EPS) + ADAM_WD * w_ref[...])

    if by_columns:
        blk = gblk = pl.BlockSpec((R, LANES), lambda i: (0, i))
        grid = (C // LANES,)
    else:
        blk = pl.BlockSpec((tr, C), lambda i: (i, 0))
        gblk = pl.BlockSpec((tr, C), lambda i: (g_off // tr + i, 0))
        grid = (R // tr,)
    shp = jax.ShapeDtypeStruct((R, C), f32)
    return _pallas(
        body, pin_bytes=PIN_SMALL, name=name, grid=grid, in_specs=[blk, gblk, blk, blk], out_specs=[blk] * 3,
        out_shape=[shp] * 3,
        compiler_params=_cparams(("parallel",), VMEM_MID),
    )(w, g, m, v)


def _pad_lanes(v):
    return jnp.pad(v, ((0, 0), (0, LANES - v.shape[1])))


def _local_step(x, p, tgt, gath0, cidx, gin, S):
    dtb = _pad_lanes(S["dt_bias"])
    alog = _pad_lanes(S["A_log"])
    dskip_e = jnp.repeat(S["D_skip"], HEAD_DIM, axis=1)

    r1, r2 = GATHER_ROWS[0], GATHER_ROWS[0] + GATHER_ROWS[1]
    u0, z, xbc, cv, cg, dtr, v, gath1 = _in_proj_fwd(x, S["mix_norm_g"], gin, rider=_gather_rider(gath0, 0, r1))
    co, yc, gath2 = _conf_fwd(v, S["conf_dw_w"], S["conf_dw_b"], S["conf_ln_g"], S["conf_ln_b"],
                              rider=_gather_rider(gath1, r1, r2 - r1))
    pre = _ssd_conv_fwd(xbc, S["ssd_conv_w"], S["ssd_conv_b"])
    y, ys, sprev, gath = _ssd_fwd(pre, dtr, z, dtb, alog, dskip_e, S["ssd_norm_g"],
                                  rider=_gather_rider(gath2, r2, SLAB_A // 2 - r2))
    gath = _forward_to_sibling(gath)
    w_ple = jnp.concatenate([_ple_of_slab(gath[b]) for b in range(N_CHIPS)], axis=1)
    h1, u1 = _out_proj_fwd(x, ys, yc, gath, S["mlp_norm_g"])
    r, h2, u2 = _mlp_fwd(h1, u1, gath, S["ple_gate_norm_g"])
    loss, dh2, dh2b, dgp, dep, dg_fin, dg_ple, db_pg, dg_pg = _ple_loss(
        h2, u2, p, tgt, gath, S["b_ple_gate"], w_ple, S["ple_norm_g"], S["final_norm_g"], S["ple_gate_norm_g"])

    npg = D_MODEL // N_CHIPS
    ga = lax.empty((N_CHIPS, SLAB_A, D_MODEL), f32)
    ga = _weight_grad(u2, dgp, "dw_ple_gate", slab=ga, tk=npg, place=lambda i, j: (i, PG_OFF // npg, j))
    ga = _weight_grad(r, dh2b, "dw_down", square=True, slab=ga, place=lambda i, j: (i // 2, DOWN_OFF // 512 + i % 2, j))
    gw_ple = _weight_grad(p, dep, "dw_ple")
    dhp, dh1, dh1b, dg_mlp = _mlp_bwd(dh2, r, gath, h1, S["mlp_norm_g"])
    ga = _weight_grad(u1, dhp, "dw_up", slab=ga, place=lambda i, j: (j, UP_OFF // 512 + i, 0))
    ga = _weight_grad(ys, dh1b, "dw_out_ssd", slab=ga, place=lambda i, j: (i, OUT_OFF // 512, j))
    ga = _weight_grad(yc, dh1b, "dw_out_conf", slab=ga, place=lambda i, j: (2 + i, OUT_OFF // 512, j))
    n_ple = D_MODEL // N_CHIPS
    ple_rows = jnp.stack([_rows(gw_ple[:, b * n_ple:(b + 1) * n_ple]) for b in range(N_CHIPS)], axis=0)
    ga = lax.dynamic_update_slice(ga, ple_rows, (0, PLE_OFF, 0))
    dys, dco, dg_ln, db_ln, recv_a = _out_proj_bwd(dh1, gath, co, S["conf_ln_g"], S["conf_ln_b"], rider=_swap_rider(ga))
    ha = _chip_sum(cidx, ga, recv_a, "chip_sum_a")
    first = EXCHANGE_FIRST_ROWS
    dcv, dcg, dw_conf, db_conf, ici_a = _conf_conv_bwd(dco, v, S["conf_dw_w"], cv, cg,
                                                       rider=_exchange_rider(ha, 0, first))
    dz, dpre, ddtr, dg_ssdn, dd, dal, ddtb, ici_a = _ssd_bwd(
        dys, y, z, pre, dtr, sprev, dtb, alog, dskip_e, S["ssd_norm_g"],
        rider=_exchange_rider(ha, first, SLAB_A // 2 - first, recv=ici_a))
    dxbc, dw_sconv, db_sconv = _ssd_conv_bwd(dpre, xbc, S["ssd_conv_w"])
    gw_in = jnp.concatenate([
        _weight_grad(dz, u0, "dw_in_z"), _weight_grad(dxbc, u0, "dw_in_xbc"),
        _weight_grad(ddtr, u0, "dw_in_dt")[:SSD_HEADS],
        _weight_grad(dcv, u0, "dw_in_cv"), _weight_grad(dcg, u0, "dw_in_cg")], axis=0)
    gb = _shard_rows(gw_in)
    recv_b = _swap_halves(gb)
    hb = _chip_sum(cidx, gb, recv_b, "chip_sum_b")
    gx, dg_mix, ici_b = _in_proj_bwd(dz, dxbc, dcv, dcg, ddtr, gin, x, dh1, S["mix_norm_g"], rider=_exchange_rider(hb))
    small = {
        "mix_norm_g": dg_mix, "ssd_conv_w": dw_sconv, "ssd_conv_b": db_sconv, "dt_bias": ddtb, "A_log": dal, "D_skip": dd,
        "ssd_norm_g": dg_ssdn, "conf_dw_w": dw_conf, "conf_dw_b": db_conf, "conf_ln_g": dg_ln, "conf_ln_b": db_ln,
        "mlp_norm_g": dg_mlp, "ple_gate_norm_g": dg_pg, "b_ple_gate": db_pg, "ple_norm_g": dg_ple,
        "final_norm_g": dg_fin, "loss": loss,
    }
    return gx, (ga, recv_a, ici_a), (gb, recv_b, ici_b), small


def _rows(a):
    return a.reshape(-1, D_MODEL)


def _pad_rows(a, n):
    flat = a.reshape(-1)
    return jnp.pad(flat, (0, n * D_MODEL - flat.shape[0])).reshape(n, D_MODEL)


def _ple_of_slab(slab):
    return slab[PLE_OFF:PLE_OFF + PLE_ROWS].reshape(PLE_DIM, D_MODEL // N_CHIPS)


ROW_VEC = {"mix_norm_g": 0, "ssd_norm_g": 1, "conf_dw_b": 2, "conf_ln_g": 3, "conf_ln_b": 4, "mlp_norm_g": 5,
           "ple_gate_norm_g": 6, "b_ple_gate": 7, "ple_norm_g": 8, "final_norm_g": 9}
ROW_CONV_B = 10
ROW_HEADS = 12
ROW_CONV_W = 16
ROW_DW = 24
HEAD_LANES = {"dt_bias": 0, "A_log": 1, "D_skip": 2, "loss": 3}
SMALL_ORDER = ("mix_norm_g", "ssd_conv_w", "ssd_conv_b", "dt_bias", "A_log", "D_skip", "ssd_norm_g", "conf_dw_w",
               "conf_dw_b", "conf_ln_g", "conf_ln_b", "mlp_norm_g", "ple_gate_norm_g", "b_ple_gate", "ple_norm_g",
               "final_norm_g")
SPLIT = XBC_WIDTH - D_MODEL


def _pack_small(raw):
    names = list(ROW_VEC) + ["ssd_conv_b", "dt_bias", "A_log", "D_skip", "loss", "ssd_conv_w", "conf_dw_w"]

    def body(*refs):
        r = dict(zip(names, refs[:-1]))
        o_ref = refs[-1]
        o_ref[...] = jnp.zeros_like(o_ref)
        for n, row in ROW_VEC.items():
            o_ref[row:row + 1, :] = r[n][...]
        o_ref[ROW_CONV_B:ROW_CONV_B + 1, :] = r["ssd_conv_b"][:, 0:D_MODEL]
        o_ref[ROW_CONV_B + 1:ROW_CONV_B + 2, 0:SPLIT] = r["ssd_conv_b"][:, D_MODEL:]
        for n, j in HEAD_LANES.items():
            o_ref[ROW_HEADS:ROW_HEADS + 1, j * LANES:(j + 1) * LANES] = r[n][0:1, :]
        for k in range(SSD_CONV):
            o_ref[ROW_CONV_W + 2 * k:ROW_CONV_W + 2 * k + 1, :] = r["ssd_conv_w"][k:k + 1, 0:D_MODEL]
            o_ref[ROW_CONV_W + 2 * k + 1:ROW_CONV_W + 2 * k + 2, 0:SPLIT] = r["ssd_conv_w"][k:k + 1, D_MODEL:]
        o_ref[ROW_DW:ROW_DW + 32, :] = r["conf_dw_w"][...]

    return _pallas(
        body, name="pack_small", out_shape=jax.ShapeDtypeStruct((SMALL_ROWS, D_MODEL), f32),
    )(*[raw[n] for n in names])


def _adamw_small(cidx, tot, w, m, v):
    c1 = 1.0 - ADAM_B1 ** ADAM_STEP
    c2 = 1.0 - ADAM_B2 ** ADAM_STEP
    n_par = len(SMALL_ORDER)

    def shard(full, chip, width):
        out = full[:, 0:width]
        for b in range(1, N_CHIPS):
            out = jnp.where(chip == b, full[:, b * width:(b + 1) * width], out)
        return out

    def grad_of(n, t_ref, chip):
        if n in ROW_VEC:
            return t_ref[ROW_VEC[n]:ROW_VEC[n] + 1, :]
        if n == "ssd_conv_b":
            return jnp.concatenate([t_ref[ROW_CONV_B:ROW_CONV_B + 1, :], t_ref[ROW_CONV_B + 1:ROW_CONV_B + 2, 0:SPLIT]], axis=1)
        if n in HEAD_LANES:
            j = HEAD_LANES[n]
            return t_ref[ROW_HEADS:ROW_HEADS + 1, j * LANES:j * LANES + SSD_HEADS]
        if n == "ssd_conv_w":
            rows = [jnp.concatenate([t_ref[ROW_CONV_W + 2 * k:ROW_CONV_W + 2 * k + 1, :],
                                     t_ref[ROW_CONV_W + 2 * k + 1:ROW_CONV_W + 2 * k + 2, 0:SPLIT]], axis=1)
                    for k in range(SSD_CONV)]
            return shard(jnp.concatenate(rows, axis=0), chip, XBC_WIDTH // N_CHIPS)
        return shard(t_ref[ROW_DW:ROW_DW + CONF_KERNEL, :], chip, CONF_WIDTH // N_CHIPS)

    def body(c_ref, t_ref, *refs):
        ins, outs = refs[:3 * n_par], refs[3 * n_par:]
        chip = c_ref[1]
        for i, n in enumerate(SMALL_ORDER):
            w_ref, m_ref, v_ref = ins[3 * i:3 * i + 3]
            g_ref, d_ref, mo_ref, vo_ref = outs[4 * i:4 * i + 4]
            g = grad_of(n, t_ref, chip)
            m2 = ADAM_B1 * m_ref[...] + (1.0 - ADAM_B1) * g
            v2 = ADAM_B2 * v_ref[...] + (1.0 - ADAM_B2) * (g * g)
            g_ref[...] = g
            mo_ref[...] = m2
            vo_ref[...] = v2
            d_ref[...] = -ADAM_LR * ((m2 / c1) / (jnp.sqrt(v2 / c2) + ADAM_EPS) + ADAM_WD * w_ref[...])

    args, in_specs, out_specs, out_shape = [], [], [], []
    for n in SMALL_ORDER:
        shp = w[n].shape
        spec = pl.BlockSpec(shp, lambda i, c_ref: (0, 0))
        args += [w[n], m[n], v[n]]
        in_specs += [spec] * 3
        out_specs += [spec] * 4
        out_shape += [jax.ShapeDtypeStruct(shp, f32)] * 4
    outs = _pallas(
        body, name="adamw_small",
        grid_spec=pltpu.PrefetchScalarGridSpec(
            num_scalar_prefetch=1, grid=(1,),
            in_specs=[pl.BlockSpec(tot.shape, lambda i, c_ref: (0, 0))] + in_specs, out_specs=out_specs),
        out_shape=out_shape,
    )(cidx, tot, *args)
    grad, delta, new_m, new_v = {}, {}, {}, {}
    for i, n in enumerate(SMALL_ORDER):
        grad[n], delta[n], new_m[n], new_v[n] = outs[4 * i:4 * i + 4]
    return grad, delta, new_m, new_v


BIG = ("w_in", "w_out", "w_up", "w_down", "w_ple_gate", "w_ple")
BIG_A = (("w_up", UP_OFF), ("w_down", DOWN_OFF), ("w_out", OUT_OFF), ("w_ple_gate", PG_OFF))
WEIGHTS = ("mix_norm_g", "w_in", "ssd_conv_w", "ssd_conv_b", "dt_bias", "A_log", "D_skip", "ssd_norm_g", "conf_dw_w",
           "conf_dw_b", "conf_ln_g", "conf_ln_b", "w_out", "mlp_norm_g", "w_up", "w_down", "ple_gate_norm_g",
           "w_ple_gate", "b_ple_gate", "w_ple", "ple_norm_g", "final_norm_g")


def kernel(x, p, mix_norm_g, w_in, ssd_conv_w, ssd_conv_b, dt_bias, A_log, D_skip, ssd_norm_g, conf_dw_w, conf_dw_b, conf_ln_g, conf_ln_b, w_out, mlp_norm_g, w_up, w_down, ple_gate_norm_g, w_ple_gate, b_ple_gate, w_ple, ple_norm_g, final_norm_g, loss_target, m_mix_norm_g, m_w_in, m_ssd_conv_w, m_ssd_conv_b, m_dt_bias, m_A_log, m_D_skip, m_ssd_norm_g, m_conf_dw_w, m_conf_dw_b, m_conf_ln_g, m_conf_ln_b, m_w_out, m_mlp_norm_g, m_w_up, m_w_down, m_ple_gate_norm_g, m_w_ple_gate, m_b_ple_gate, m_w_ple, m_ple_norm_g, m_final_norm_g, v_mix_norm_g, v_w_in, v_ssd_conv_w, v_ssd_conv_b, v_dt_bias, v_A_log, v_D_skip, v_ssd_norm_g, v_conf_dw_w, v_conf_dw_b, v_conf_ln_g, v_conf_ln_b, v_w_out, v_mlp_norm_g, v_w_up, v_w_down, v_ple_gate_norm_g, v_w_ple_gate, v_b_ple_gate, v_w_ple, v_ple_norm_g, v_final_norm_g):
    w = dict(mix_norm_g=mix_norm_g, w_in=w_in, ssd_conv_w=ssd_conv_w, ssd_conv_b=ssd_conv_b, dt_bias=dt_bias, A_log=A_log,
             D_skip=D_skip, ssd_norm_g=ssd_norm_g, conf_dw_w=conf_dw_w, conf_dw_b=conf_dw_b, conf_ln_g=conf_ln_g,
             conf_ln_b=conf_ln_b, w_out=w_out, mlp_norm_g=mlp_norm_g, w_up=w_up, w_down=w_down,
             ple_gate_norm_g=ple_gate_norm_g, w_ple_gate=w_ple_gate, b_ple_gate=b_ple_gate, w_ple=w_ple,
             ple_norm_g=ple_norm_g, final_norm_g=final_norm_g)
    m = dict(mix_norm_g=m_mix_norm_g, w_in=m_w_in, ssd_conv_w=m_ssd_conv_w, ssd_conv_b=m_ssd_conv_b, dt_bias=m_dt_bias,
             A_log=m_A_log, D_skip=m_D_skip, ssd_norm_g=m_ssd_norm_g, conf_dw_w=m_conf_dw_w, conf_dw_b=m_conf_dw_b,
             conf_ln_g=m_conf_ln_g, conf_ln_b=m_conf_ln_b, w_out=m_w_out, mlp_norm_g=m_mlp_norm_g, w_up=m_w_up,
             w_down=m_w_down, ple_gate_norm_g=m_ple_gate_norm_g, w_ple_gate=m_w_ple_gate, b_ple_gate=m_b_ple_gate,
             w_ple=m_w_ple, ple_norm_g=m_ple_norm_g, final_norm_g=m_final_norm_g)
    v = dict(mix_norm_g=v_mix_norm_g, w_in=v_w_in, ssd_conv_w=v_ssd_conv_w, ssd_conv_b=v_ssd_conv_b, dt_bias=v_dt_bias,
             A_log=v_A_log, D_skip=v_D_skip, ssd_norm_g=v_ssd_norm_g, conf_dw_w=v_conf_dw_w, conf_dw_b=v_conf_dw_b,
             conf_ln_g=v_conf_ln_g, conf_ln_b=v_conf_ln_b, w_out=v_w_out, mlp_norm_g=v_mlp_norm_g, w_up=v_w_up,
             w_down=v_w_down, ple_gate_norm_g=v_ple_gate_norm_g, w_ple_gate=v_w_ple_gate, b_ple_gate=v_b_ple_gate,
             w_ple=v_w_ple, ple_norm_g=v_ple_norm_g, final_norm_g=v_final_norm_g)
    xi, yi, ci = lax.axis_index("x"), lax.axis_index("y"), lax.axis_index("c")
    chip = 2 * xi + yi

    slab = jnp.concatenate([w_up[0], w_down[0], w_out[0], w_ple_gate[0], _rows(w_ple[0])], axis=0).astype(bf16)
    gath0 = lax.dynamic_update_slice(lax.empty((N_CHIPS, SLAB_A, D_MODEL), bf16), slab[None], (chip, 0, 0))
    wt_shard = jnp.swapaxes(w_in, 1, 2).astype(bf16)
    wt_shard = jnp.pad(wt_shard, ((0, 0), (0, W_IN_ROWS_PAD - W_IN_ROWS), (0, 0)))
    gin0 = lax.dynamic_update_slice(lax.empty((N_CHIPS, W_IN_ROWS_PAD, D_MODEL), bf16), wt_shard, (chip, 0, 0))
    convw = _pad_rows(jnp.concatenate([ssd_conv_w[0].reshape(-1), conf_dw_w[0].reshape(-1)]), CONVW_ROWS)
    gin, cwg = _gather_weights([gin0], convw)
    n_sc = SSD_CONV * (XBC_WIDTH // N_CHIPS)
    n_cf = CONF_KERNEL * (CONF_WIDTH // N_CHIPS)
    S = {n: w[n][0] for n in ("mix_norm_g", "ssd_conv_b", "dt_bias", "A_log", "D_skip", "ssd_norm_g", "conf_dw_b",
                              "conf_ln_g", "conf_ln_b", "mlp_norm_g", "ple_gate_norm_g", "b_ple_gate", "ple_norm_g")}
    S = {n: a.reshape(1, -1) for n, a in S.items()}
    S["final_norm_g"] = final_norm_g.reshape(1, -1)
    S["ssd_conv_w"] = jnp.concatenate(
        [cwg[b].reshape(-1)[:n_sc].reshape(SSD_CONV, XBC_WIDTH // N_CHIPS) for b in range(N_CHIPS)], axis=1)
    S["conf_dw_w"] = jnp.concatenate(
        [cwg[b].reshape(-1)[n_sc:n_sc + n_cf].reshape(CONF_KERNEL, CONF_WIDTH // N_CHIPS) for b in range(N_CHIPS)], axis=1)

    cidx = jnp.stack([ci, chip]).astype(jnp.int32)
    grad_x, (ga, recv_a, ici_a), (gb, recv_b, ici_b), gsmall = _local_step(
        x[0], p[0, 0], loss_target[0], gath0, cidx, gin, S)

    all_small = _gather_small(_pack_small(gsmall))
    ra = _final_sum(cidx, ga, recv_a, ici_a, "final_sum_a")
    rb = _final_sum(cidx, gb, recv_b, ici_b, "final_sum_b")
    ra, rb = _join_halves(ra, rb)
    tot_small = _sum_small(all_small)

    loss = tot_small[ROW_HEADS, HEAD_LANES["loss"] * LANES]

    two_d = lambda a: a.reshape(a.shape[-2:]) if a.ndim > 1 else a.reshape(1, -1)
    small_w, small_m, small_v = ({n: two_d(d[n]) for n in SMALL_ORDER} for d in (w, m, v))
    grads, delta, new_m, new_v = _adamw_small(cidx, tot_small, small_w, small_m, small_v)
    g_in_t = rb[:W_IN_ROWS]
    grads["w_ple"] = _ple_of_slab(ra)
    grads["w_in"] = jnp.swapaxes(g_in_t, 0, 1)
    for n, off in BIG_A:
        grads[n] = ra[off:off + w[n].shape[1]]
        delta[n], new_m[n], new_v[n] = _adamw(w[n][0], ra, m[n][0], v[n][0], "adamw_" + n, g_off=off)
    delta["w_ple"], new_m["w_ple"], new_v["w_ple"] = _adamw(w_ple[0], grads["w_ple"], m_w_ple[0], v_w_ple[0], "adamw_w_ple")
    tr_ = lambda a: jnp.swapaxes(a[0], 0, 1)
    d_, m_, v_ = _adamw(tr_(w_in), g_in_t, tr_(m_w_in), tr_(v_w_in), "adamw_w_in", by_columns=True)
    delta["w_in"], new_m["w_in"], new_v["w_in"] = (jnp.swapaxes(a, 0, 1) for a in (d_, m_, v_))

    shaped = lambda d: [d[n].reshape(w[n].shape) for n in WEIGHTS]
    return (loss, grad_x[None], *shaped(grads), *shaped(delta), *shaped(new_m), *shaped(new_v))
```

```python
import jax
import jax.numpy as jnp
from jax import lax
from jax.experimental import pallas as pl
from jax.experimental.pallas import tpu as pltpu

f32 = jnp.float32
bf16 = jnp.bfloat16

D_MODEL = 1024
SSD_WIDTH = 1024
SSD_HEADS = 16
HEAD_DIM = 64
SSD_STATE = 128
XBC_WIDTH = 1536
SSD_CONV = 4
CHUNK = 128
CONF_WIDTH = 1024
CONF_KERNEL = 31
D_FF = 4096
PLE_DIM = 256
IN_WIDTH = 4624
EPS = 1e-6
N_CHIPS = 4
N_DEV = 8

ADAM_LR = 0.001
ADAM_B1 = 0.9
ADAM_B2 = 0.999
ADAM_EPS = 1e-08
ADAM_WD = 0.01
ADAM_STEP = 10

LANES = 128
VMEM_BIG = 56 * 1024 * 1024
VMEM_MID = 40 * 1024 * 1024

UP_OFF, DOWN_OFF, OUT_OFF, PG_OFF, PLE_OFF = 0, 1024, 2048, 2560, 2816
PLE_ROWS = 64
SLAB_A = PLE_OFF + PLE_ROWS
GATHER_ROWS = (416, 720)
EXCHANGE_FIRST_ROWS = 960
W_IN_ROWS = 1156
W_IN_ROWS_PAD = 1184
CONVW_ROWS = 16
SMALL_ROWS = 56

MESH = pl.DeviceIdType.MESH
ANY = pl.BlockSpec(memory_space=pl.ANY)


PIN_SMALL = 256 * 1024


def _pallas(body, pin_bytes=None, **kw):
    call = pl.pallas_call(body, **kw)

    def pin(a):
        wanted = pin_bytes is None or a.size * a.dtype.itemsize <= pin_bytes
        return pltpu.with_memory_space_constraint(a, pltpu.HBM) if wanted and a.dtype != jnp.int32 else a

    def run(*args):
        return call(*[pin(a) for a in args])

    return run


def _cparams(sem=None, vmem=None):
    return pltpu.CompilerParams(dimension_semantics=sem, vmem_limit_bytes=vmem)


def _full(shape, single=False):
    n = len(shape)
    if single:
        return pl.BlockSpec(shape, lambda *_: (0,) * n, pipeline_mode=pl.Buffered(1))
    return pl.BlockSpec(shape, lambda *_: (0,) * n)


class _Rider:
    def __init__(self, inputs, out_shapes, aliases, n_sems, start, finish):
        self.inputs, self.out_shapes, self.aliases = list(inputs), list(out_shapes), dict(aliases)
        self.n_sems, self.start, self.finish = n_sems, start, finish


def _call(body, args, *, name, grid, in_specs, out_specs, out_shape, scratch_shapes=(), params=None, rider=None):
    if rider is None:
        return _pallas(body, name=name, grid=grid, in_specs=in_specs, out_specs=out_specs, out_shape=out_shape,
                              scratch_shapes=list(scratch_shapes), compiler_params=params)(*args)
    ni, no, ns = len(in_specs), len(out_specs), len(scratch_shapes)
    ri, ro = len(rider.inputs), len(rider.out_shapes)
    (steps,) = grid

    def with_rider(*refs):
        ins, refs = refs[:ni], refs[ni:]
        rins, refs = refs[:ri], refs[ri:]
        outs, refs = refs[:no], refs[no:]
        routs, refs = refs[:ro], refs[ro:]
        scratch, (ssem, rsem) = refs[:ns], refs[ns:]
        step = pl.program_id(0)

        @pl.when(step == 0)
        def _():
            rider.start(rins, routs, ssem, rsem)

        body(*ins, *outs, *scratch)

        @pl.when(step == steps - 1)
        def _():
            rider.finish(rins, routs, ssem, rsem)

    sems = [pltpu.SemaphoreType.DMA((rider.n_sems,)), pltpu.SemaphoreType.DMA((rider.n_sems,))]
    return _pallas(
        with_rider, name=name, grid=grid, in_specs=list(in_specs) + [ANY] * ri, out_specs=list(out_specs) + [ANY] * ro,
        out_shape=list(out_shape) + rider.out_shapes, scratch_shapes=list(scratch_shapes) + sems,
        input_output_aliases={ni + a: no + b for a, b in rider.aliases.items()}, compiler_params=params,
    )(*args, *rider.inputs)


def _dot(a, b):
    return jnp.dot(a, b, preferred_element_type=f32)


def _dot_nt(a, b):
    return lax.dot_general(a, b, (((1,), (1,)), ((), ())), preferred_element_type=f32)


def _dot_tn(a, b):
    return lax.dot_general(a, b, (((0,), (0,)), ((), ())), preferred_element_type=f32)


def _sigmoid(x):
    return jax.nn.sigmoid(x)


def _rms(x, g):
    r = lax.rsqrt(jnp.mean(x * x, axis=-1, keepdims=True) + EPS)
    return x * r * g


def _rms_bwd(dy, x, g):
    r = lax.rsqrt(jnp.mean(x * x, axis=-1, keepdims=True) + EPS)
    xh = x * r
    dg = jnp.sum(dy * xh, axis=0, keepdims=True)
    dxh = dy * g
    dx = r * (dxh - xh * jnp.mean(dxh * xh, axis=-1, keepdims=True))
    return dx, dg


def _dsilu(x):
    s = _sigmoid(x)
    return s * (1.0 + x * (1.0 - s))


def _split3(x):
    hi = x.astype(bf16)
    r1 = x - hi.astype(f32)
    mid = r1.astype(bf16)
    lo = (r1 - mid.astype(f32)).astype(bf16)
    return hi, mid, lo


def _head_matrix():
    row = lax.broadcasted_iota(jnp.int32, (LANES, SSD_WIDTH), 0)
    col = lax.broadcasted_iota(jnp.int32, (LANES, SSD_WIDTH), 1)
    lo = row * HEAD_DIM
    return ((col >= lo) & (col < lo + HEAD_DIM)).astype(bf16)


def _expand(x, e):
    hi, mid, lo = _split3(x)
    return _dot(hi, e) + _dot(mid, e) + _dot(lo, e)


def _contract(x, e):
    hi = x.astype(bf16)
    mid = (x - hi.astype(f32)).astype(bf16)
    return _dot_nt(hi, e) + _dot_nt(mid, e)


O_XBC = SSD_WIDTH
O_DT = O_XBC + XBC_WIDTH
O_CV = O_DT + SSD_HEADS
O_CG = O_CV + CONF_WIDTH


def _assemble_w_in_t(gin_ref, wt_ref):
    for b in range(N_CHIPS):
        wt_ref[b * W_IN_ROWS:(b + 1) * W_IN_ROWS, :] = gin_ref[b, 0:W_IN_ROWS, :]


def _in_proj_fwd(x, g, gin, rider=None):
    T = x.shape[0]
    tm = min(256, T)

    def body(x_ref, g_ref, gin_ref, u_ref, z_ref, xbc_ref, cv_ref, cg_ref, dt_ref, v_ref, wt_ref):
        @pl.when(pl.program_id(0) == 0)
        def _():
            _assemble_w_in_t(gin_ref, wt_ref)

        ub = _rms(x_ref[...], g_ref[...]).astype(bf16)
        u_ref[...] = ub
        z_ref[...] = _dot_nt(ub, wt_ref[0:O_XBC, :])
        xbc_ref[...] = _dot_nt(ub, wt_ref[O_XBC:O_DT, :])
        cv = _dot_nt(ub, wt_ref[O_CV:O_CG, :])
        cg = _dot_nt(ub, wt_ref[O_CG:IN_WIDTH, :])
        cv_ref[...] = cv
        cg_ref[...] = cg
        v_ref[...] = cv * _sigmoid(cg)
        dt_ref[...] = _dot_nt(ub, wt_ref[O_DT:O_DT + LANES, :])

    row = lambda n: pl.BlockSpec((tm, n), lambda i: (i, 0))
    return _call(
        body, (x, g, gin), name="in_proj_fwd", grid=(T // tm,),
        in_specs=[row(D_MODEL), _full((1, D_MODEL)), _full(gin.shape, single=True)],
        out_specs=[row(D_MODEL), row(SSD_WIDTH), row(XBC_WIDTH), row(CONF_WIDTH), row(CONF_WIDTH), row(LANES),
                   row(CONF_WIDTH)],
        out_shape=[jax.ShapeDtypeStruct((T, D_MODEL), bf16), jax.ShapeDtypeStruct((T, SSD_WIDTH), f32),
                   jax.ShapeDtypeStruct((T, XBC_WIDTH), f32), jax.ShapeDtypeStruct((T, CONF_WIDTH), f32),
                   jax.ShapeDtypeStruct((T, CONF_WIDTH), f32), jax.ShapeDtypeStruct((T, LANES), f32),
                   jax.ShapeDtypeStruct((T, CONF_WIDTH), f32)],
        scratch_shapes=[pltpu.VMEM((IN_WIDTH, D_MODEL), bf16)],
        params=_cparams(("arbitrary",), VMEM_BIG), rider=rider)


SUBLANES = 8


def _phases(offsets):
    return sorted({o % SUBLANES for o in offsets} - {0})


def _phase_shape(offsets, tm, C):
    a_max = max([o // SUBLANES for o in offsets if o % SUBLANES] or [0])
    return (max(len(_phases(offsets)), 1), tm + SUBLANES * a_max, C)


def _make_phases(buf_ref, ph_ref, offsets, tm):
    for idx, b in enumerate(_phases(offsets)):
        n = tm + SUBLANES * max(o // SUBLANES for o in offsets if o % SUBLANES == b)
        ph_ref[idx, 0:n, :] = buf_ref[pl.ds(b, n), :]


def _window(buf_ref, ph_ref, offsets, o, r0, rb):
    a, b = divmod(o, SUBLANES)
    if b == 0:
        return buf_ref[pl.ds(r0 + SUBLANES * a, rb), :]
    return ph_ref[_phases(offsets).index(b), pl.ds(r0 + SUBLANES * a, rb), :]


def _conv_rows(wb_ref, buf_ref, ph_ref, offsets, r0, rb):
    nsub = rb // SUBLANES
    accs = [None] * nsub
    for k, o in enumerate(offsets):
        wk = wb_ref[pl.ds(SUBLANES * k, SUBLANES), :]
        for s in range(nsub):
            term = wk * _window(buf_ref, ph_ref, offsets, o, r0 + SUBLANES * s, SUBLANES)
            accs[s] = term if accs[s] is None else accs[s] + term
    return accs[0] if nsub == 1 else jnp.concatenate(accs, axis=0)


def _sublane_rows(w):
    return jnp.repeat(w, SUBLANES, axis=0)


def _fwd_offsets(K, hb):
    return [hb - (K - 1) + k for k in range(K)]


def _prev_halo_spec(hb, tm, C):
    return pl.BlockSpec((hb, C), lambda i: (jnp.maximum(i * (tm // hb) - 1, 0), 0))


CONV_RB = 16


def _ssd_conv_fwd(xbc, w, b):
    T, C = xbc.shape
    K, hb = SSD_CONV, 8
    tm = min(256, T)
    offs = _fwd_offsets(K, hb)

    def body(cur_ref, halo_ref, w_ref, b_ref, pre_ref, buf_ref, ph_ref):
        keep = jnp.where(pl.program_id(0) > 0, 1.0, 0.0)
        buf_ref[0:hb, :] = halo_ref[...] * keep
        buf_ref[hb:hb + tm, :] = cur_ref[...]
        _make_phases(buf_ref, ph_ref, offs, tm)

        def chunk(i, carry):
            r0 = pl.multiple_of(i * CONV_RB, CONV_RB)
            pre_ref[pl.ds(r0, CONV_RB), :] = _conv_rows(w_ref, buf_ref, ph_ref, offs, r0, CONV_RB) + b_ref[...]
            return carry

        lax.fori_loop(0, tm // CONV_RB, chunk, 0)

    return _pallas(
        body, name="ssd_conv_fwd", grid=(T // tm,),
        in_specs=[pl.BlockSpec((tm, C), lambda i: (i, 0)), _prev_halo_spec(hb, tm, C), _full((SUBLANES * K, C)),
                  _full((1, C))],
        out_specs=pl.BlockSpec((tm, C), lambda i: (i, 0)),
        out_shape=jax.ShapeDtypeStruct((T, C), f32),
        scratch_shapes=[pltpu.VMEM((hb + tm, C), f32), pltpu.VMEM(_phase_shape(offs, tm, C), f32)],
        compiler_params=_cparams(("parallel",), VMEM_MID),
    )(xbc, xbc, _sublane_rows(w), b)


def _conf_fwd(v, w, b, ln_g, ln_b, rider=None):
    T, C = v.shape
    K, hb = CONF_KERNEL, 32
    tm = min(256, T)
    offs = _fwd_offsets(K, hb)
    rb = 2 * CONV_RB

    def body(cur_ref, halo_ref, w_ref, b_ref, g_ref, bb_ref, co_ref, y_ref, buf_ref, ph_ref):
        keep = jnp.where(pl.program_id(0) > 0, 1.0, 0.0)
        buf_ref[0:hb, :] = halo_ref[...] * keep
        buf_ref[hb:hb + tm, :] = cur_ref[...]
        _make_phases(buf_ref, ph_ref, offs, tm)

        def chunk(i, carry):
            r0 = pl.multiple_of(i * rb, rb)
            co = _conv_rows(w_ref, buf_ref, ph_ref, offs, r0, rb) + b_ref[...]
            co_ref[pl.ds(r0, rb), :] = co
            mu = jnp.mean(co, axis=-1, keepdims=True)
            xc = co - mu
            yn = xc * lax.rsqrt(jnp.mean(xc * xc, axis=-1, keepdims=True) + EPS) * g_ref[...] + bb_ref[...]
            y_ref[pl.ds(r0, rb), :] = (yn * _sigmoid(yn)).astype(bf16)
            return carry

        lax.fori_loop(0, tm // rb, chunk, 0)

    return _call(
        body, (v, v, _sublane_rows(w), b, ln_g, ln_b), name="conf_fwd", grid=(T // tm,),
        in_specs=[pl.BlockSpec((tm, C), lambda i: (i, 0)), _prev_halo_spec(hb, tm, C), _full((SUBLANES * K, C)),
                  _full((1, C)), _full((1, C)), _full((1, C))],
        out_specs=[pl.BlockSpec((tm, C), lambda i: (i, 0)), pl.BlockSpec((tm, C), lambda i: (i, 0))],
        out_shape=[jax.ShapeDtypeStruct((T, C), f32), jax.ShapeDtypeStruct((T, C), bf16)],
        scratch_shapes=[pltpu.VMEM((hb + tm, C), f32), pltpu.VMEM(_phase_shape(offs, tm, C), f32)],
        params=_cparams(("arbitrary",), VMEM_MID), rider=rider)


def _ssd_chunk_common(pre, dtr, dtb, alog, e):
    act = pre * _sigmoid(pre)
    xs = act[:, :SSD_WIDTH]
    bm = act[:, SSD_WIDTH:SSD_WIDTH + 2 * SSD_STATE]
    cm = act[:, SSD_WIDTH + 2 * SSD_STATE:]
    row = lax.broadcasted_iota(jnp.int32, (CHUNK, CHUNK), 0)
    col = lax.broadcasted_iota(jnp.int32, (CHUNK, CHUNK), 1)
    tri = row >= col
    dt = jax.nn.softplus(dtr + dtb)
    a_neg = -jnp.exp(alog)
    a = dt * a_neg
    cs = jnp.dot(tri.astype(f32), a, precision=lax.Precision.HIGHEST, preferred_element_type=f32)
    cs_e = _expand(cs, e)
    dt_e = _expand(dt, e)
    csl_e = cs_e[CHUNK - 1:CHUNK, :]
    ecs_e = jnp.exp(cs_e)
    dte_e = jnp.exp(csl_e - cs_e)
    cd_e = jnp.exp(csl_e)
    xc = xs * dt_e
    xd = xc * dte_e
    return dict(xs=xs, bm=bm, cm=cm, tri=tri, dt=dt, a_neg=a_neg, cs=cs, ecs_e=ecs_e, dte_e=dte_e, cd_e=cd_e,
                dt_e=dt_e, xc=xc, xd=xd)


def _group(v, g, width):
    return v[:, g * width:(g + 1) * width]


def _ssd_fwd(pre, dtr, z, dtb, alog, dskip_e, gn, rider=None):
    T = pre.shape[0]
    nc = T // CHUNK
    GW = SSD_WIDTH // 2

    def body(pre_ref, dtr_ref, z_ref, dtb_ref, alog_ref, de_ref, gn_ref, y_ref, ys_ref, sp_ref, st_ref):
        @pl.when(pl.program_id(0) == 0)
        def _():
            st_ref[...] = jnp.zeros_like(st_ref)

        e = _head_matrix()
        q = _ssd_chunk_common(pre_ref[...], dtr_ref[...], dtb_ref[...], alog_ref[...], e)
        cs, tri, xc, xd = q["cs"], q["tri"], q["xc"], q["xd"]
        cs_t = cs.T
        st = st_ref[...]
        sp_ref[0] = st
        lane = lax.broadcasted_iota(jnp.int32, (1, LANES), 1)
        halves = (lane < HEAD_DIM, lane >= HEAD_DIM)

        g_mat, y_off, s_new = [], [], []
        for g in range(2):
            bg = _group(q["bm"], g, SSD_STATE)
            cg = _group(q["cm"], g, SSD_STATE)
            bgb, cgb = bg.astype(bf16), cg.astype(bf16)
            g_mat.append(_dot_nt(cgb, bgb))
            y_off.append(_dot(cgb, _group(st, g, GW).astype(bf16)))
            s_new.append(_dot(bg.T.astype(bf16), _group(xd, g, GW).astype(bf16)))
        y_off = jnp.concatenate(y_off, axis=1) * q["ecs_e"]
        st_ref[...] = st * q["cd_e"] + jnp.concatenate(s_new, axis=1)

        pairs = []
        for j in range(SSD_HEADS // 2):
            xp = xc[:, j * LANES:(j + 1) * LANES]
            acc = jnp.zeros((CHUNK, LANES), f32)
            for hh in range(2):
                h = 2 * j + hh
                seg = cs[:, h:h + 1] - cs_t[h:h + 1, :]
                lm = jnp.exp(jnp.where(tri, seg, -1e30))
                m = (g_mat[h // 8] * lm).astype(bf16)
                acc = acc + _dot(m, jnp.where(halves[hh], xp, 0.0).astype(bf16))
            pairs.append(acc)
        y = jnp.concatenate(pairs, axis=1) + y_off + q["xs"] * de_ref[...]
        y_ref[...] = y

        zz = z_ref[...]
        v = y * (zz * _sigmoid(zz))
        outs = []
        for g in range(2):
            vg = _group(v, g, GW)
            outs.append(vg * lax.rsqrt(jnp.mean(vg * vg, axis=-1, keepdims=True) + EPS))
        ys_ref[...] = (jnp.concatenate(outs, axis=1) * gn_ref[...]).astype(bf16)

    ch = lambda n: pl.BlockSpec((CHUNK, n), lambda c: (c, 0))
    return _call(
        body, (pre, dtr, z, dtb, alog, dskip_e, gn), name="ssd_fwd", grid=(nc,),
        in_specs=[ch(XBC_WIDTH), ch(LANES), ch(SSD_WIDTH), _full((1, LANES)), _full((1, LANES)), _full((1, SSD_WIDTH)),
                  _full((1, SSD_WIDTH))],
        out_specs=[ch(SSD_WIDTH), ch(SSD_WIDTH), pl.BlockSpec((1, SSD_STATE, SSD_WIDTH), lambda c: (c, 0, 0))],
        out_shape=[jax.ShapeDtypeStruct((T, SSD_WIDTH), f32), jax.ShapeDtypeStruct((T, SSD_WIDTH), bf16),
                   jax.ShapeDtypeStruct((nc, SSD_STATE, SSD_WIDTH), f32)],
        scratch_shapes=[pltpu.VMEM((SSD_STATE, SSD_WIDTH), f32)],
        params=_cparams(("arbitrary",), VMEM_MID), rider=rider)


def _w_out_spec():
    n = 2 * SSD_WIDTH // N_CHIPS
    return pl.BlockSpec((N_CHIPS, n, D_MODEL), lambda *_: (0, OUT_OFF // n, 0))


def _out_proj_fwd(x, ys, yc, gath, g):
    T = x.shape[0]
    tm = min(512, T)
    n = 2 * SSD_WIDTH // N_CHIPS

    def body(x_ref, ys_ref, yc_ref, w_ref, g_ref, h_ref, u_ref):
        h = (x_ref[...] + _dot(ys_ref[:, 0:n], w_ref[0]) + _dot(ys_ref[:, n:], w_ref[1])
             + _dot(yc_ref[:, 0:n], w_ref[2]) + _dot(yc_ref[:, n:], w_ref[3]))
        h_ref[...] = h
        u_ref[...] = _rms(h, g_ref[...]).astype(bf16)

    row = pl.BlockSpec((tm, D_MODEL), lambda i: (i, 0))
    return _pallas(
        body, name="out_proj_fwd", grid=(T // tm,),
        in_specs=[row, row, row, _w_out_spec(), _full((1, D_MODEL))],
        out_specs=[row, row],
        out_shape=[jax.ShapeDtypeStruct((T, D_MODEL), f32), jax.ShapeDtypeStruct((T, D_MODEL), bf16)],
        compiler_params=_cparams(("parallel",), VMEM_MID),
    )(x, ys, yc, gath, g)


def _w_up_spec():
    return pl.BlockSpec((1, D_MODEL, D_MODEL), lambda i, b: (b, UP_OFF // D_MODEL, 0))


def _w_down_spec():
    return pl.BlockSpec((1, D_MODEL, D_MODEL), lambda i, b: (b, DOWN_OFF // D_MODEL, 0))


def _mlp_fwd(h1, u1, gath, g_next):
    T = h1.shape[0]
    tm = min(512, T)
    nb = D_FF // D_MODEL

    def body(h_ref, u_ref, wu_ref, wd_ref, g_ref, r_ref, h2_ref, u2_ref, acc_ref):
        b = pl.program_id(1)

        @pl.when(b == 0)
        def _():
            acc_ref[...] = jnp.zeros_like(acc_ref)

        r = jnp.maximum(_dot(u_ref[...], wu_ref[0]), 0.0)
        r_ref[...] = r.astype(bf16)
        acc_ref[...] += _dot((r * r).astype(bf16), wd_ref[0])

        @pl.when(b == nb - 1)
        def _():
            h2 = h_ref[...] + acc_ref[...]
            h2_ref[...] = h2
            u2_ref[...] = _rms(h2, g_ref[...]).astype(bf16)

    row = pl.BlockSpec((tm, D_MODEL), lambda i, b: (i, 0))
    return _pallas(
        body, name="mlp_fwd", grid=(T // tm, nb),
        in_specs=[row, row, _w_up_spec(), _w_down_spec(), _full((1, D_MODEL))],
        out_specs=[pl.BlockSpec((tm, D_MODEL), lambda i, b: (i, b)), row, row],
        out_shape=[jax.ShapeDtypeStruct((T, D_FF), bf16), jax.ShapeDtypeStruct((T, D_MODEL), f32),
                   jax.ShapeDtypeStruct((T, D_MODEL), bf16)],
        scratch_shapes=[pltpu.VMEM((tm, D_MODEL), f32)],
        compiler_params=_cparams(("parallel", "arbitrary"), VMEM_MID),
    )(h1, u1, gath, gath, g_next)


def _ple_loss(h2, u2, p, tgt, gath, b_pg, w_ple, g_ple, g_fin, g_pg):
    T = h2.shape[0]
    tm = min(256, T)
    npg = D_MODEL // N_CHIPS

    def body(h2_ref, u2_ref, p_ref, t_ref, wpg_ref, bpg_ref, wple_ref, gple_ref, gfin_ref, gpg_ref,
             loss_ref, dh2_ref, dh2b_ref, dgp_ref, dep_ref, dgfin_ref, dgple_ref, dbpg_ref, dgpg_ref):
        @pl.when(pl.program_id(0) == 0)
        def _():
            loss_ref[...] = jnp.zeros_like(loss_ref)
            dgfin_ref[...] = jnp.zeros_like(dgfin_ref)
            dgple_ref[...] = jnp.zeros_like(dgple_ref)
            dbpg_ref[...] = jnp.zeros_like(dbpg_ref)
            dgpg_ref[...] = jnp.zeros_like(dgpg_ref)

        h2 = h2_ref[...]
        gate_pre = bpg_ref[...]
        for b in range(N_CHIPS):
            gate_pre = gate_pre + _dot(u2_ref[:, b * npg:(b + 1) * npg], wpg_ref[b])
        gate = _sigmoid(gate_pre)
        e_pre = _dot(p_ref[...].astype(bf16), wple_ref[...])
        emb = _rms(e_pre, gple_ref[...])
        h3 = h2 + gate * emb
        diff = _rms(h3, gfin_ref[...]) - t_ref[...]
        sq = jnp.sum(jnp.sum(diff * diff, axis=1, keepdims=True), axis=0, keepdims=True)
        loss_ref[...] += (0.5 / D_MODEL) * sq
        dh3, dgfin = _rms_bwd(diff * (1.0 / D_MODEL), h3, gfin_ref[...])
        dgfin_ref[...] += dgfin
        dgp = dh3 * emb * gate * (1.0 - gate)
        dbpg_ref[...] += jnp.sum(dgp, axis=0, keepdims=True)
        dep, dgple = _rms_bwd(dh3 * gate, e_pre, gple_ref[...])
        dgple_ref[...] += dgple
        dgpb = dgp.astype(bf16)
        dgp_ref[...] = dgpb
        dep_ref[...] = dep.astype(bf16)
        du2 = jnp.concatenate([_dot_nt(dgpb, wpg_ref[b]) for b in range(N_CHIPS)], axis=1)
        dx, dgpg = _rms_bwd(du2, h2, gpg_ref[...])
        dgpg_ref[...] += dgpg
        dh2 = dh3 + dx
        dh2_ref[...] = dh2
        dh2b_ref[...] = dh2.astype(bf16)

    row = pl.BlockSpec((tm, D_MODEL), lambda i: (i, 0))
    vec = _full((1, D_MODEL))
    vshape = jax.ShapeDtypeStruct((1, D_MODEL), f32)
    return _pallas(
        body, name="ple_loss", grid=(T // tm,),
        in_specs=[row, row, pl.BlockSpec((tm, PLE_DIM), lambda i: (i, 0)), row,
                  pl.BlockSpec((N_CHIPS, npg, D_MODEL), lambda i: (0, PG_OFF // npg, 0)), vec, _full(w_ple.shape),
                  vec, vec, vec],
        out_specs=[_full((8, LANES)), row, row, row, row, vec, vec, vec, vec],
        out_shape=[jax.ShapeDtypeStruct((8, LANES), f32), jax.ShapeDtypeStruct((T, D_MODEL), f32),
                   jax.ShapeDtypeStruct((T, D_MODEL), bf16), jax.ShapeDtypeStruct((T, D_MODEL), bf16),
                   jax.ShapeDtypeStruct((T, D_MODEL), bf16), vshape, vshape, vshape, vshape],
        compiler_params=_cparams(("arbitrary",), VMEM_MID),
    )(h2, u2, p, tgt, gath, b_pg, w_ple, g_ple, g_fin, g_pg)


def _mlp_bwd(dh2, r, gath, h1, g):
    T = dh2.shape[0]
    tm = min(512, T)
    nb = D_FF // D_MODEL

    def body(dh2_ref, r_ref, wd_ref, wu_ref, h1_ref, g_ref, dhp_ref, dh1_ref, dh1b_ref, dg_ref, acc_ref):
        i, b = pl.program_id(0), pl.program_id(1)

        @pl.when(b == 0)
        def _():
            acc_ref[...] = jnp.zeros_like(acc_ref)

        @pl.when((b == 0) & (i == 0))
        def _():
            dg_ref[...] = jnp.zeros_like(dg_ref)

        dact = _dot_nt(dh2_ref[...].astype(bf16), wd_ref[0])
        dhp = (dact * 2.0 * r_ref[...].astype(f32)).astype(bf16)
        dhp_ref[...] = dhp
        acc_ref[...] += _dot_nt(dhp, wu_ref[0])

        @pl.when(b == nb - 1)
        def _():
            dx, dg = _rms_bwd(acc_ref[...], h1_ref[...], g_ref[...])
            dg_ref[...] += dg
            dh1 = dh2_ref[...] + dx
            dh1_ref[...] = dh1
            dh1b_ref[...] = dh1.astype(bf16)

    row = pl.BlockSpec((tm, D_MODEL), lambda i, b: (i, 0))
    return _pallas(
        body, name="mlp_bwd", grid=(T // tm, nb),
        in_specs=[row, pl.BlockSpec((tm, D_MODEL), lambda i, b: (i, b)), _w_down_spec(), _w_up_spec(), row,
                  _full((1, D_MODEL))],
        out_specs=[pl.BlockSpec((tm, D_MODEL), lambda i, b: (i, b)), row, row, _full((1, D_MODEL))],
        out_shape=[jax.ShapeDtypeStruct((T, D_FF), bf16), jax.ShapeDtypeStruct((T, D_MODEL), f32),
                   jax.ShapeDtypeStruct((T, D_MODEL), bf16), jax.ShapeDtypeStruct((1, D_MODEL), f32)],
        scratch_shapes=[pltpu.VMEM((tm, D_MODEL), f32)],
        compiler_params=_cparams(("arbitrary", "arbitrary"), VMEM_MID),
    )(dh2, r, gath, gath, h1, g)


def _out_proj_bwd(dh1, gath, co, ln_g, ln_b, rider=None):
    T = dh1.shape[0]
    tm = min(512, T)

    def body(dh_ref, w_ref, co_ref, g_ref, b_ref, dys_ref, dco_ref, dg_ref, db_ref):
        @pl.when(pl.program_id(0) == 0)
        def _():
            dg_ref[...] = jnp.zeros_like(dg_ref)
            db_ref[...] = jnp.zeros_like(db_ref)

        dhb = dh_ref[...].astype(bf16)
        dys_ref[...] = jnp.concatenate([_dot_nt(dhb, w_ref[0]), _dot_nt(dhb, w_ref[1])], axis=1)
        dyc = jnp.concatenate([_dot_nt(dhb, w_ref[2]), _dot_nt(dhb, w_ref[3])], axis=1)
        co = co_ref[...]
        mu = jnp.mean(co, axis=-1, keepdims=True)
        xc = co - mu
        rstd = lax.rsqrt(jnp.mean(xc * xc, axis=-1, keepdims=True) + EPS)
        xh = xc * rstd
        yn = xh * g_ref[...] + b_ref[...]
        dyn = dyc * _dsilu(yn)
        dg_ref[...] += jnp.sum(dyn * xh, axis=0, keepdims=True)
        db_ref[...] += jnp.sum(dyn, axis=0, keepdims=True)
        dxh = dyn * g_ref[...]
        dco_ref[...] = rstd * (dxh - jnp.mean(dxh, axis=-1, keepdims=True)
                               - xh * jnp.mean(dxh * xh, axis=-1, keepdims=True))

    row = pl.BlockSpec((tm, D_MODEL), lambda i: (i, 0))
    vec = _full((1, CONF_WIDTH))
    vshape = jax.ShapeDtypeStruct((1, CONF_WIDTH), f32)
    return _call(
        body, (dh1, gath, co, ln_g, ln_b), name="out_proj_bwd", grid=(T // tm,),
        in_specs=[row, _w_out_spec(), row, vec, vec],
        out_specs=[row, row, vec, vec],
        out_shape=[jax.ShapeDtypeStruct((T, SSD_WIDTH), f32), jax.ShapeDtypeStruct((T, CONF_WIDTH), f32), vshape, vshape],
        params=_cparams(("arbitrary",), VMEM_MID), rider=rider)


def _bwd_offsets(K):
    return [K - 1 - k for k in range(K)]


def _next_halo_spec(hb, tm, C, T):
    return pl.BlockSpec((hb, C), lambda i: (jnp.minimum((i + 1) * (tm // hb), T // hb - 1), 0))


DW_RB = 8
DW_UNROLL = 16
DW_ACC_VREGS = 32


def _conv_dw(dw_ref, bufd_ref, bufx_ref, phx_ref, offs_x, tm, C):
    K = len(offs_x)
    group = max(1, DW_ACC_VREGS // (C // LANES))
    for k0 in range(0, K, group):
        ks = list(range(k0, min(k0 + group, K)))

        def step(i, accs, ks=ks):
            for u in range(DW_UNROLL):
                r0 = pl.multiple_of((i * DW_UNROLL + u) * DW_RB, DW_RB)
                d = bufd_ref[pl.ds(r0, DW_RB), :]
                accs = tuple(acc + _window(bufx_ref, phx_ref, offs_x, offs_x[k], r0, DW_RB) * d
                             for k, acc in zip(ks, accs))
            return accs

        accs = lax.fori_loop(0, tm // (DW_RB * DW_UNROLL), step, tuple(jnp.zeros((DW_RB, C), f32) for _ in ks))
        for k, acc in zip(ks, accs):
            dw_ref[k:k + 1, :] += jnp.sum(acc, axis=0, keepdims=True)


def _fill_bwd_buffers(dcur_ref, dnext_ref, xcur_ref, xprev_ref, bufd_ref, bufx_ref, phd_ref, phx_ref, offs_d, offs_x,
                      hb, tm, first, last):
    bufd_ref[0:tm, :] = dcur_ref[...]
    bufd_ref[tm:tm + hb, :] = dnext_ref[...] * jnp.where(last, 0.0, 1.0)
    bufx_ref[0:hb, :] = xprev_ref[...] * jnp.where(first, 0.0, 1.0)
    bufx_ref[hb:hb + tm, :] = xcur_ref[...]
    _make_phases(bufd_ref, phd_ref, offs_d, tm)
    _make_phases(bufx_ref, phx_ref, offs_x, tm)


def _ssd_conv_bwd(dpre, xbc, w):
    T, C = xbc.shape
    K, hb = SSD_CONV, 8
    tm = min(256, T)
    nt = T // tm
    offs_d, offs_x = _bwd_offsets(K), _fwd_offsets(K, hb)

    def body(dcur_ref, dnext_ref, xcur_ref, xprev_ref, w_ref, dx_ref, dw_ref, db_ref, bufd_ref, bufx_ref, phd_ref, phx_ref):
        i = pl.program_id(0)

        @pl.when(i == 0)
        def _():
            dw_ref[...] = jnp.zeros_like(dw_ref)
            db_ref[...] = jnp.zeros_like(db_ref)

        _fill_bwd_buffers(dcur_ref, dnext_ref, xcur_ref, xprev_ref, bufd_ref, bufx_ref, phd_ref, phx_ref, offs_d, offs_x,
                          hb, tm, i == 0, i == nt - 1)

        def chunk(j, carry):
            r0 = pl.multiple_of(j * CONV_RB, CONV_RB)
            dx_ref[pl.ds(r0, CONV_RB), :] = _conv_rows(w_ref, bufd_ref, phd_ref, offs_d, r0, CONV_RB).astype(bf16)
            return carry

        lax.fori_loop(0, tm // CONV_RB, chunk, 0)
        _conv_dw(dw_ref, bufd_ref, bufx_ref, phx_ref, offs_x, tm, C)
        db_ref[...] += jnp.sum(dcur_ref[...], axis=0, keepdims=True)

    row = pl.BlockSpec((tm, C), lambda i: (i, 0))
    return _pallas(
        body, name="ssd_conv_bwd", grid=(nt,),
        in_specs=[row, _next_halo_spec(hb, tm, C, T), row, _prev_halo_spec(hb, tm, C), _full((SUBLANES * K, C))],
        out_specs=[row, _full((8, C)), _full((1, C))],
        out_shape=[jax.ShapeDtypeStruct((T, C), bf16), jax.ShapeDtypeStruct((8, C), f32), jax.ShapeDtypeStruct((1, C), f32)],
        scratch_shapes=[pltpu.VMEM((tm + hb, C), f32), pltpu.VMEM((hb + tm, C), f32),
                        pltpu.VMEM(_phase_shape(offs_d, tm, C), f32),
                        pltpu.VMEM(_phase_shape(offs_x, tm, C), f32)],
        compiler_params=_cparams(("arbitrary",), VMEM_BIG),
    )(dpre, dpre, xbc, xbc, _sublane_rows(w))


def _conf_conv_bwd(dco, v, w, cv, cg, rider=None):
    T, C = v.shape
    K, hb = CONF_KERNEL, 32
    tm = min(256, T)
    nt = T // tm
    offs_d, offs_x = _bwd_offsets(K), _fwd_offsets(K, hb)

    def body(dcur_ref, dnext_ref, vcur_ref, vprev_ref, w_ref, cv_ref, cg_ref, dcv_ref, dcg_ref, dw_ref, db_ref,
             bufd_ref, bufx_ref, phd_ref, phx_ref):
        i = pl.program_id(0)

        @pl.when(i == 0)
        def _():
            dw_ref[...] = jnp.zeros_like(dw_ref)
            db_ref[...] = jnp.zeros_like(db_ref)

        _fill_bwd_buffers(dcur_ref, dnext_ref, vcur_ref, vprev_ref, bufd_ref, bufx_ref, phd_ref, phx_ref, offs_d, offs_x,
                          hb, tm, i == 0, i == nt - 1)

        def chunk(j, carry):
            r0 = pl.multiple_of(j * CONV_RB, CONV_RB)
            rows = pl.ds(r0, CONV_RB)
            dv = _conv_rows(w_ref, bufd_ref, phd_ref, offs_d, r0, CONV_RB)
            s = _sigmoid(cg_ref[rows, :])
            dcv_ref[rows, :] = (dv * s).astype(bf16)
            dcg_ref[rows, :] = (dv * cv_ref[rows, :] * s * (1.0 - s)).astype(bf16)
            return carry

        lax.fori_loop(0, tm // CONV_RB, chunk, 0)
        _conv_dw(dw_ref, bufd_ref, bufx_ref, phx_ref, offs_x, tm, C)
        db_ref[...] += jnp.sum(dcur_ref[...], axis=0, keepdims=True)

    row = pl.BlockSpec((tm, C), lambda i: (i, 0))
    return _call(
        body, (dco, dco, v, v, _sublane_rows(w), cv, cg), name="conf_conv_bwd", grid=(nt,),
        in_specs=[row, _next_halo_spec(hb, tm, C, T), row, _prev_halo_spec(hb, tm, C), _full((SUBLANES * K, C)), row, row],
        out_specs=[row, row, _full((32, C)), _full((1, C))],
        out_shape=[jax.ShapeDtypeStruct((T, C), bf16), jax.ShapeDtypeStruct((T, C), bf16),
                   jax.ShapeDtypeStruct((32, C), f32), jax.ShapeDtypeStruct((1, C), f32)],
        scratch_shapes=[pltpu.VMEM((tm + hb, C), f32), pltpu.VMEM((hb + tm, C), f32),
                        pltpu.VMEM(_phase_shape(offs_d, tm, C), f32),
                        pltpu.VMEM(_phase_shape(offs_x, tm, C), f32)],
        params=_cparams(("arbitrary",), VMEM_BIG), rider=rider)


def _ssd_bwd(dys, y, z, pre, dtr, sprev, dtb, alog, dskip_e, gn, rider=None):
    T = pre.shape[0]
    nc = T // CHUNK
    GW = SSD_WIDTH // 2

    def body(dys_ref, y_ref, z_ref, pre_ref, dtr_ref, sp_ref, dtb_ref, alog_ref, de_ref, gn_ref,
             dz_ref, dpre_ref, ddtr_ref, dgn_ref, dd_ref, dal_ref, ddtb_ref, ds_ref):
        @pl.when(pl.program_id(0) == 0)
        def _():
            ds_ref[...] = jnp.zeros_like(ds_ref)
            dgn_ref[...] = jnp.zeros_like(dgn_ref)
            dd_ref[...] = jnp.zeros_like(dd_ref)
            dal_ref[...] = jnp.zeros_like(dal_ref)
            ddtb_ref[...] = jnp.zeros_like(ddtb_ref)

        e = _head_matrix()
        pre = pre_ref[...]
        dtr_b = dtr_ref[...] + dtb_ref[...]
        q = _ssd_chunk_common(pre, dtr_ref[...], dtb_ref[...], alog_ref[...], e)
        cs, tri, xc, xd, xs, dt = q["cs"], q["tri"], q["xc"], q["xd"], q["xs"], q["dt"]
        cs_t = cs.T
        st = sp_ref[0]
        dsn = ds_ref[...]
        lane = lax.broadcasted_iota(jnp.int32, (1, LANES), 1)
        halves = (lane < HEAD_DIM, lane >= HEAD_DIM)
        row_i = lax.broadcasted_iota(jnp.int32, (CHUNK, CHUNK), 0)
        col_i = lax.broadcasted_iota(jnp.int32, (CHUNK, CHUNK), 1)
        tri_t = col_i >= row_i

        y = y_ref[...]
        zz = z_ref[...]
        sz = _sigmoid(zz)
        silu_z = zz * sz
        v = y * silu_z
        dout = dys_ref[...]
        gn_v = gn_ref[...]
        dv, vh = [], []
        for g in range(2):
            vg = _group(v, g, GW)
            rstd = lax.rsqrt(jnp.mean(vg * vg, axis=-1, keepdims=True) + EPS)
            vhg = vg * rstd
            dvh = _group(dout, g, GW) * _group(gn_v, g, GW)
            dv.append(rstd * (dvh - vhg * jnp.mean(dvh * vhg, axis=-1, keepdims=True)))
            vh.append(vhg)
        dv = jnp.concatenate(dv, axis=1)
        dgn_ref[...] += jnp.sum(dout * jnp.concatenate(vh, axis=1), axis=0, keepdims=True)
        dy = dv * silu_z
        dz_ref[...] = (dv * y * (sz * (1.0 + zz * (1.0 - sz)))).astype(bf16)

        dd_row = jnp.sum(dy * xs, axis=0, keepdims=True)
        dd_ref[...] += _contract(jnp.broadcast_to(dd_row, (8, SSD_WIDTH)), e)[0:1, :]
        dxs = dy * de_ref[...]

        dz_in = dy * q["ecs_e"]
        g_mat, gt_mat, dcm, dbm, dsp, dxd, y_off = [], [], [], [], [], [], []
        bgs, cgs = [], []
        for g in range(2):
            bg = _group(q["bm"], g, SSD_STATE)
            cg = _group(q["cm"], g, SSD_STATE)
            bgb, cgb = bg.astype(bf16), cg.astype(bf16)
            bgs.append(bgb)
            cgs.append(cgb)
            stg = _group(st, g, GW).astype(bf16)
            dsng = _group(dsn, g, GW).astype(bf16)
            dzg = _group(dz_in, g, GW).astype(bf16)
            g_mat.append(_dot_nt(cgb, bgb))
            gt_mat.append(_dot_nt(bgb, cgb))
            y_off.append(_dot(cgb, stg))
            dcm.append(_dot_nt(dzg, stg))
            dsp.append(_dot(cg.T.astype(bf16), dzg))
            dbm.append(_dot_nt(_group(xd, g, GW).astype(bf16), dsng))
            dxd.append(_dot(bgb, dsng))
        y_off = jnp.concatenate(y_off, axis=1) * q["ecs_e"]
        dxd = jnp.concatenate(dxd, axis=1)
        ds_ref[...] = dsn * q["cd_e"] + jnp.concatenate(dsp, axis=1)
        dcd_row = jnp.sum(dsn * st, axis=0, keepdims=True) * q["cd_e"]
        t_e = dxd * xd
        dcs = _contract(dy * y_off - t_e, e)
        last_row = _contract(jnp.broadcast_to(dcd_row + jnp.sum(t_e, axis=0, keepdims=True), (8, SSD_WIDTH)), e)[0:1, :]
        dxc_state = dxd * q["dte_e"]

        dg_acc = [jnp.zeros((CHUNK, CHUNK), f32), jnp.zeros((CHUNK, CHUNK), f32)]
        dgt_acc = [jnp.zeros((CHUNK, CHUNK), f32), jnp.zeros((CHUNK, CHUNK), f32)]
        dxc_pairs = []
        for j in range(SSD_HEADS // 2):
            dyp_f = dy[:, j * LANES:(j + 1) * LANES]
            xcp_f = xc[:, j * LANES:(j + 1) * LANES]
            acc = jnp.zeros((CHUNK, LANES), f32)
            for hh in range(2):
                h = 2 * j + hh
                g = h // 8
                dyp = jnp.where(halves[hh], dyp_f, 0.0).astype(bf16)
                xcp = jnp.where(halves[hh], xcp_f, 0.0).astype(bf16)
                lm = jnp.exp(jnp.where(tri, cs[:, h:h + 1] - cs_t[h:h + 1, :], -1e30))
                lm_t = jnp.exp(jnp.where(tri_t, cs_t[h:h + 1, :] - cs[:, h:h + 1], -1e30))
                dm = _dot_nt(dyp, xcp) * lm
                dm_t = _dot_nt(xcp, dyp) * lm_t
                acc = acc + _dot((gt_mat[g] * lm_t).astype(bf16), dyp)
                dg_acc[g] = dg_acc[g] + dm
                dgt_acc[g] = dgt_acc[g] + dm_t
                qd = jnp.sum(dm * g_mat[g] - dm_t * gt_mat[g], axis=1, keepdims=True)
                dcs = dcs + qd * (lane == h).astype(f32)
            dxc_pairs.append(acc)
        dxc = jnp.concatenate(dxc_pairs, axis=1) + dxc_state
        for g in range(2):
            dcm[g] = dcm[g] + _dot(dg_acc[g].astype(bf16), bgs[g])
            dbm[g] = dbm[g] + _dot(dgt_acc[g].astype(bf16), cgs[g])

        dxs = dxs + dxc * q["dt_e"]
        ddt = _contract(dxc * xs, e)
        dcs = dcs + jnp.where(row_i == CHUNK - 1, jnp.broadcast_to(last_row, (CHUNK, LANES)), 0.0)
        da = jnp.dot(tri_t.astype(f32), dcs, precision=lax.Precision.HIGHEST, preferred_element_type=f32)
        ddt = ddt + da * q["a_neg"]
        dal_ref[...] += jnp.sum(da * dt, axis=0, keepdims=True) * q["a_neg"]
        ddtr = ddt * _sigmoid(dtr_b) * (lane < SSD_HEADS).astype(f32)
        ddtb_ref[...] += jnp.sum(ddtr, axis=0, keepdims=True)
        ddtr_ref[...] = ddtr.astype(bf16)

        dact = jnp.concatenate([dxs, dbm[0], dbm[1], dcm[0], dcm[1]], axis=1)
        dpre_ref[...] = dact * _dsilu(pre)

    rev = lambda n: pl.BlockSpec((CHUNK, n), lambda c: (nc - 1 - c, 0))
    vec = _full((1, LANES))
    vshape = jax.ShapeDtypeStruct((1, LANES), f32)
    return _call(
        body, (dys, y, z, pre, dtr, sprev, dtb, alog, dskip_e, gn), name="ssd_bwd", grid=(nc,),
        in_specs=[rev(SSD_WIDTH), rev(SSD_WIDTH), rev(SSD_WIDTH), rev(XBC_WIDTH), rev(LANES),
                  pl.BlockSpec((1, SSD_STATE, SSD_WIDTH), lambda c: (nc - 1 - c, 0, 0)),
                  vec, vec, _full((1, SSD_WIDTH)), _full((1, SSD_WIDTH))],
        out_specs=[rev(SSD_WIDTH), rev(XBC_WIDTH), rev(LANES), _full((1, SSD_WIDTH)), vec, vec, vec],
        out_shape=[jax.ShapeDtypeStruct((T, SSD_WIDTH), bf16), jax.ShapeDtypeStruct((T, XBC_WIDTH), f32),
                   jax.ShapeDtypeStruct((T, LANES), bf16), jax.ShapeDtypeStruct((1, SSD_WIDTH), f32),
                   vshape, vshape, vshape],
        scratch_shapes=[pltpu.VMEM((SSD_STATE, SSD_WIDTH), f32)],
        params=_cparams(("arbitrary",), VMEM_MID), rider=rider)


def _in_proj_bwd(dz, dxbc, dcv, dcg, ddt, gin, x, dh1, g, rider=None):
    T = x.shape[0]
    tm = min(512, T)

    def body(dz_ref, dx_ref, dcv_ref, dcg_ref, ddt_ref, gin_ref, x_ref, dh_ref, g_ref, gx_ref, dg_ref, wt_ref):
        @pl.when(pl.program_id(0) == 0)
        def _():
            dg_ref[...] = jnp.zeros_like(dg_ref)
            _assemble_w_in_t(gin_ref, wt_ref)

        du = (_dot(dz_ref[...], wt_ref[0:O_XBC, :]) + _dot(dx_ref[...], wt_ref[O_XBC:O_DT, :])
              + _dot(dcv_ref[...], wt_ref[O_CV:O_CG, :]) + _dot(dcg_ref[...], wt_ref[O_CG:IN_WIDTH, :])
              + _dot(ddt_ref[...], wt_ref[O_DT:O_DT + LANES, :]))
        dx, dg = _rms_bwd(du, x_ref[...], g_ref[...])
        dg_ref[...] += dg
        gx_ref[...] = dh_ref[...] + dx

    row = lambda n: pl.BlockSpec((tm, n), lambda i: (i, 0))
    return _call(
        body, (dz, dxbc, dcv, dcg, ddt, gin, x, dh1, g), name="in_proj_bwd", grid=(T // tm,),
        in_specs=[row(SSD_WIDTH), row(XBC_WIDTH), row(CONF_WIDTH), row(CONF_WIDTH), row(LANES), _full(gin.shape, single=True),
                  row(D_MODEL), row(D_MODEL), _full((1, D_MODEL))],
        out_specs=[row(D_MODEL), _full((1, D_MODEL))],
        out_shape=[jax.ShapeDtypeStruct((T, D_MODEL), f32), jax.ShapeDtypeStruct((1, D_MODEL), f32)],
        scratch_shapes=[pltpu.VMEM((IN_WIDTH, D_MODEL), bf16)],
        params=_cparams(("arbitrary",), VMEM_BIG), rider=rider)


def _weight_grad(a, g, name, square=False, slab=None, place=None, tk=512):
    T, K = a.shape
    N = g.shape[1]
    tk = min(tk, K)
    tn = 1024 if N % 1024 == 0 else min(512, N)
    tt = min(2048, T)

    def body(a_ref, g_ref, *rest):
        o_ref = rest[-1]
        acc = _dot_tn(_operand(a_ref[...]), g_ref[...].astype(bf16))
        t = pl.program_id(2)
        shaped = acc if slab is None else acc[None]

        @pl.when(t == 0)
        def _():
            o_ref[...] = shaped

        @pl.when(t > 0)
        def _():
            o_ref[...] += shaped

    def _operand(av):
        if square:
            av = av.astype(f32)
            av = av * av
        return av.astype(bf16)

    in_specs = [pl.BlockSpec((tt, tk), lambda i, j, t: (t, i)), pl.BlockSpec((tt, tn), lambda i, j, t: (t, j))]
    grid = (K // tk, N // tn, T // tt)
    params = _cparams(("parallel", "parallel", "arbitrary"), VMEM_MID)
    if slab is None:
        return _pallas(
            body, pin_bytes=PIN_SMALL, name=name, grid=grid, in_specs=in_specs,
            out_specs=pl.BlockSpec((tk, tn), lambda i, j, t: (i, j)),
            out_shape=jax.ShapeDtypeStruct((K, N), f32), compiler_params=params,
        )(a, g)
    return _pallas(
        body, pin_bytes=PIN_SMALL, name=name, grid=grid, in_specs=in_specs + [ANY],
        out_specs=pl.BlockSpec((1, tk, tn), lambda i, j, t: place(i, j)),
        out_shape=jax.ShapeDtypeStruct(slab.shape, f32), input_output_aliases={2: 0}, compiler_params=params,
    )(a, g, slab)


def _place():
    return lax.axis_index("x"), lax.axis_index("y"), lax.axis_index("c")


def _other_chips(x, y):
    return [(1 - x, y), (x, 1 - y), (1 - x, 1 - y)]


def _remote(src, dst, ssem, rsem, dev):
    return pltpu.make_async_remote_copy(src_ref=src, dst_ref=dst, send_sem=ssem, recv_sem=rsem, device_id=dev,
                                        device_id_type=MESH)


def _gather_weights(arrays, convw):
    n = len(arrays)
    halves = tuple(a.shape[1] // 2 for a in arrays)

    def body(*refs):
        cw_ref, cwo_ref = refs[n], refs[2 * n + 1]
        ssem, rsem, lsem = refs[2 * n + 2:]
        triples = tuple(zip(refs[:n], refs[n + 1:2 * n + 1], halves))
        x, y, c = _place()
        me_b = 2 * x + y
        sib = (x, y, 1 - c)
        chips = _other_chips(x, y)
        loc = pltpu.make_async_copy(cw_ref, cwo_ref.at[me_b], lsem)
        loc.start()
        sends = []
        for j, (src, dst, h) in enumerate(triples):
            mine = pl.ds(c * h, h)
            for k, (px, py) in enumerate(chips):
                s = 6 * j + k
                sends.append(_remote(src.at[me_b, mine], dst.at[me_b, mine], ssem.at[s], rsem.at[s], (px, py, c)))
        for k, (px, py) in enumerate(chips):
            sends.append(_remote(cw_ref, cwo_ref.at[me_b], ssem.at[6 * n + k], rsem.at[6 * n + k], (px, py, c)))
        for cp in sends:
            cp.start()
        for j, (src, dst, h) in enumerate(triples):
            mine = pl.ds(c * h, h)
            for k, (px, py) in enumerate(chips):
                b = 2 * px + py
                s = 6 * j + k
                _remote(src.at[b, mine], dst.at[b, mine], ssem.at[s], rsem.at[s], (px, py, c)).wait_recv()
                fw = _remote(dst.at[b, mine], dst.at[b, mine], ssem.at[s + 3], rsem.at[s + 3], sib)
                fw.start()
                sends.append(fw)
        for k, (px, py) in enumerate(chips):
            b = 2 * px + py
            _remote(cw_ref, cwo_ref.at[b], ssem.at[6 * n + k], rsem.at[6 * n + k], (px, py, c)).wait_recv()
        for j, (src, dst, h) in enumerate(triples):
            theirs = pl.ds((1 - c) * h, h)
            for k, (px, py) in enumerate(chips):
                b = 2 * px + py
                s = 6 * j + k + 3
                _remote(src.at[b, theirs], dst.at[b, theirs], ssem.at[s], rsem.at[s], sib).wait_recv()
        for cp in sends:
            cp.wait_send()
        loc.wait()

    return _pallas(
        body, name="gather_weights", in_specs=[ANY] * (n + 1), out_specs=[ANY] * (n + 1),
        out_shape=[jax.ShapeDtypeStruct(a.shape, bf16) for a in arrays]
        + [jax.ShapeDtypeStruct((N_CHIPS, CONVW_ROWS, D_MODEL), f32)],
        input_output_aliases={j: j for j in range(n)},
        scratch_shapes=[pltpu.SemaphoreType.DMA((6 * n + 3,)), pltpu.SemaphoreType.DMA((6 * n + 3,)),
                        pltpu.SemaphoreType.DMA(())],
    )(*arrays, convw)


def _gather_rider(gath0, lo, n):
    h = gath0.shape[1] // 2

    def copies(rins, routs, ssem, rsem, sending):
        (g_ref,), (o_ref,) = rins, routs
        x, y, c = _place()
        mine = pl.ds(c * h + lo, n)
        for k, (px, py) in enumerate(_other_chips(x, y)):
            b = 2 * x + y if sending else 2 * px + py
            yield _remote(g_ref.at[b, mine], o_ref.at[b, mine], ssem.at[k], rsem.at[k], (px, py, c))

    def start(*refs):
        for cp in copies(*refs, sending=True):
            cp.start()

    def finish(*refs):
        for cp in copies(*refs, sending=False):
            cp.wait()

    return _Rider([gath0], [jax.ShapeDtypeStruct(gath0.shape, gath0.dtype)], {0: 0}, 3, start, finish)


def _forward_to_sibling(gath):
    h = gath.shape[1] // 2

    def body(g_ref, o_ref, ssem, rsem):
        x, y, c = _place()
        sib = (x, y, 1 - c)
        mine, theirs = pl.ds(c * h, h), pl.ds((1 - c) * h, h)
        blocks = [2 * px + py for px, py in _other_chips(x, y)]
        sends = [_remote(g_ref.at[b, mine], o_ref.at[b, mine], ssem.at[k], rsem.at[k], sib) for k, b in enumerate(blocks)]
        for cp in sends:
            cp.start()
        for k, b in enumerate(blocks):
            _remote(g_ref.at[b, theirs], o_ref.at[b, theirs], ssem.at[k], rsem.at[k], sib).wait_recv()
        for cp in sends:
            cp.wait_send()

    return _pallas(
        body, name="forward_to_sibling", in_specs=[ANY], out_specs=ANY,
        out_shape=jax.ShapeDtypeStruct(gath.shape, gath.dtype), input_output_aliases={0: 0},
        scratch_shapes=[pltpu.SemaphoreType.DMA((3,)), pltpu.SemaphoreType.DMA((3,))],
    )(gath)


def _swap_copy(g_ref, r_ref, ssem, rsem):
    x, y, c = _place()
    h = r_ref.shape[1]
    return _remote(g_ref.at[:, pl.ds((1 - c) * h, h), :], r_ref, ssem.at[0], rsem.at[0], (x, y, 1 - c))


def _swap_rider(g):
    def start(rins, routs, ssem, rsem):
        _swap_copy(rins[0], routs[0], ssem, rsem).start()

    def finish(rins, routs, ssem, rsem):
        _swap_copy(rins[0], routs[0], ssem, rsem).wait()

    return _Rider([g], [jax.ShapeDtypeStruct((N_CHIPS, g.shape[1] // 2, g.shape[2]), g.dtype)], {}, 1, start, finish)


def _swap_halves(g):
    def body(g_ref, r_ref, ssem, rsem):
        cp = _swap_copy(g_ref, r_ref, ssem, rsem)
        cp.start()
        cp.wait()

    return _pallas(
        body, name="swap_halves", in_specs=[ANY], out_specs=ANY,
        out_shape=jax.ShapeDtypeStruct((N_CHIPS, g.shape[1] // 2, g.shape[2]), g.dtype),
        scratch_shapes=[pltpu.SemaphoreType.DMA((1,)), pltpu.SemaphoreType.DMA((1,))],
    )(g)


def _chip_sum(cidx, gslab, recv, name):
    half, C = recv.shape[1:]
    tr = half // 2 if (half // 2) % 16 == 0 else half

    def body(c_ref, g_ref, r_ref, o_ref):
        o_ref[...] = (g_ref[...] + r_ref[...]).astype(bf16)

    return _pallas(
        body, name=name,
        grid_spec=pltpu.PrefetchScalarGridSpec(
            num_scalar_prefetch=1, grid=(N_CHIPS, half // tr),
            in_specs=[pl.BlockSpec((1, tr, C), lambda b, i, c_ref: (b, c_ref[0] * (half // tr) + i, 0)),
                      pl.BlockSpec((1, tr, C), lambda b, i, c_ref: (b, i, 0))],
            out_specs=pl.BlockSpec((1, tr, C), lambda b, i, c_ref: (b, i, 0))),
        out_shape=jax.ShapeDtypeStruct((N_CHIPS, half, C), bf16),
        compiler_params=_cparams(("parallel", "parallel"), VMEM_MID),
    )(cidx, gslab, recv)


def _exchange_rider(h, lo=0, n=None, recv=None):
    n = h.shape[1] - lo if n is None else n

    def copies(rins, routs, ssem, rsem):
        x, y, c = _place()
        rows = pl.ds(lo, n)
        for k, (px, py) in enumerate(_other_chips(x, y)):
            yield _remote(rins[0].at[2 * px + py, rows], routs[0].at[k, rows], ssem.at[k], rsem.at[k], (px, py, c))

    def start(*refs):
        for cp in copies(*refs):
            cp.start()

    def finish(*refs):
        for cp in copies(*refs):
            cp.wait()

    out = jax.ShapeDtypeStruct((3,) + h.shape[1:], h.dtype)
    if recv is None:
        return _Rider([h], [out], {}, 3, start, finish)
    return _Rider([h, recv], [out], {1: 0}, 3, start, finish)


def _gather_small(small):
    def body(sm_ref, all_ref, ssem, rsem, lsem):
        x, y, c = _place()
        me = 4 * x + 2 * y + c
        loc = pltpu.make_async_copy(sm_ref, all_ref.at[me], lsem)
        loc.start()
        sends, peers = [], []
        for r in range(1, N_DEV):
            peer = ((1 - x) if r & 4 else x, (1 - y) if r & 2 else y, (1 - c) if r & 1 else c)
            peers.append(peer)
            sends.append(_remote(sm_ref, all_ref.at[me], ssem.at[r - 1], rsem.at[r - 1], peer))
        for cp in sends:
            cp.start()
        for r, peer in zip(range(1, N_DEV), peers):
            pid = 4 * peer[0] + 2 * peer[1] + peer[2]
            _remote(sm_ref, all_ref.at[pid], ssem.at[r - 1], rsem.at[r - 1], peer).wait_recv()
        for cp in sends:
            cp.wait_send()
        loc.wait()

    return _pallas(
        body, name="gather_small", in_specs=[ANY], out_specs=ANY,
        out_shape=jax.ShapeDtypeStruct((N_DEV, SMALL_ROWS, D_MODEL), f32),
        scratch_shapes=[pltpu.SemaphoreType.DMA((7,)), pltpu.SemaphoreType.DMA((7,)), pltpu.SemaphoreType.DMA(())],
    )(small)


def _final_sum(idx, gslab, recv_sib, recv_ici, name):
    half, C = recv_sib.shape[1:]
    tr = half // 2 if (half // 2) % 16 == 0 else half

    def body(i_ref, g_ref, r_ref, p_ref, o_ref):
        acc = g_ref[0] + r_ref[0]
        for k in range(3):
            acc = acc + p_ref[k].astype(f32)
        o_ref[...] = acc

    return _pallas(
        body, name=name,
        grid_spec=pltpu.PrefetchScalarGridSpec(
            num_scalar_prefetch=1, grid=(half // tr,),
            in_specs=[pl.BlockSpec((1, tr, C), lambda i, s: (s[1], s[0] * (half // tr) + i, 0)),
                      pl.BlockSpec((1, tr, C), lambda i, s: (s[1], i, 0)),
                      pl.BlockSpec((3, tr, C), lambda i, s: (0, i, 0))],
            out_specs=pl.BlockSpec((tr, C), lambda i, s: (s[0] * (half // tr) + i, 0))),
        out_shape=jax.ShapeDtypeStruct((2 * half, C), f32),
        compiler_params=_cparams(("parallel",), VMEM_MID),
    )(idx, gslab, recv_sib, recv_ici)


def _join_halves(ra, rb):
    ha, hb = ra.shape[0] // 2, rb.shape[0] // 2

    def body(a_ref, b_ref, ao_ref, bo_ref, ssem, rsem):
        x, y, c = _place()
        sib = (x, y, 1 - c)
        mine_a, theirs_a = pl.ds(c * ha, ha), pl.ds((1 - c) * ha, ha)
        mine_b, theirs_b = pl.ds(c * hb, hb), pl.ds((1 - c) * hb, hb)
        ca = _remote(a_ref.at[mine_a], ao_ref.at[mine_a], ssem.at[0], rsem.at[0], sib)
        cb = _remote(b_ref.at[mine_b], bo_ref.at[mine_b], ssem.at[1], rsem.at[1], sib)
        ca.start()
        cb.start()
        _remote(a_ref.at[theirs_a], ao_ref.at[theirs_a], ssem.at[0], rsem.at[0], sib).wait_recv()
        _remote(b_ref.at[theirs_b], bo_ref.at[theirs_b], ssem.at[1], rsem.at[1], sib).wait_recv()
        ca.wait_send()
        cb.wait_send()

    return _pallas(
        body, name="join_halves", in_specs=[ANY, ANY], out_specs=[ANY, ANY],
        out_shape=[jax.ShapeDtypeStruct(ra.shape, f32), jax.ShapeDtypeStruct(rb.shape, f32)],
        input_output_aliases={0: 0, 1: 1},
        scratch_shapes=[pltpu.SemaphoreType.DMA((2,)), pltpu.SemaphoreType.DMA((2,))],
    )(ra, rb)


def _shard_rows(gt):
    def body(g_ref, o_ref):
        for b in range(N_CHIPS):
            o_ref[b, 0:W_IN_ROWS, :] = g_ref[b * W_IN_ROWS:(b + 1) * W_IN_ROWS, :]
            o_ref[b, W_IN_ROWS:W_IN_ROWS_PAD, :] = jnp.zeros((W_IN_ROWS_PAD - W_IN_ROWS, LANES), f32)

    return _pallas(
        body, name="shard_rows", grid=(D_MODEL // LANES,),
        in_specs=[pl.BlockSpec((IN_WIDTH, LANES), lambda i: (0, i))],
        out_specs=pl.BlockSpec((N_CHIPS, W_IN_ROWS_PAD, LANES), lambda i: (0, 0, i)),
        out_shape=jax.ShapeDtypeStruct((N_CHIPS, W_IN_ROWS_PAD, D_MODEL), f32),
        compiler_params=_cparams(("parallel",), VMEM_MID),
    )(gt)


def _sum_small(all_small):
    def body(a_ref, o_ref):
        acc = a_ref[0]
        for d in range(1, N_DEV):
            acc = acc + a_ref[d]
        o_ref[...] = acc

    return _pallas(
        body, name="sum_small", out_shape=jax.ShapeDtypeStruct((SMALL_ROWS, D_MODEL), f32),
    )(all_small)


def _adamw(w, g, m, v, name, g_off=0, by_columns=False):
    R, C = w.shape
    tr = 256 if R % 256 == 0 else R
    assert g_off % tr == 0 and not (by_columns and g_off)
    c1 = 1.0 - ADAM_B1 ** ADAM_STEP
    c2 = 1.0 - ADAM_B2 ** ADAM_STEP

    def body(w_ref, g_ref, m_ref, v_ref, d_ref, mo_ref, vo_ref):
        gg = g_ref[...]
        m2 = ADAM_B1 * m_ref[...] + (1.0 - ADAM_B1) * gg
        v2 = ADAM_B2 * v_ref[...] + (1.0 - ADAM_B2) * (gg * gg)
        mo_ref[...] = m2
        vo_ref[...] = v2
        d_ref[...] = -ADAM_LR * ((m2 / c1) / (jnp.sqrt(v2 / c2) + ADAM_EPS) + ADAM_WD * w_ref[...])

    if by_columns:
        blk = gblk = pl.BlockSpec((R, LANES), lambda i: (0, i))
        grid = (C // LANES,)
    else:
        blk = pl.BlockSpec((tr, C), lambda i: (i, 0))
        gblk = pl.BlockSpec((tr, C), lambda i: (g_off // tr + i, 0))
        grid = (R // tr,)
    shp = jax.ShapeDtypeStruct((R, C), f32)
    return _pallas(
        body, pin_bytes=PIN_SMALL, name=name, grid=grid, in_specs=[blk, gblk, blk, blk], out_specs=[blk] * 3,
        out_shape=[shp] * 3,
        compiler_params=_cparams(("parallel",), VMEM_MID),
    )(w, g, m, v)


def _pad_lanes(v):
    return jnp.pad(v, ((0, 0), (0, LANES - v.shape[1])))


def _local_step(x, p, tgt, gath0, cidx, gin, S):
    dtb = _pad_lanes(S["dt_bias"])
    alog = _pad_lanes(S["A_log"])
    dskip_e = jnp.repeat(S["D_skip"], HEAD_DIM, axis=1)

    r1, r2 = GATHER_ROWS[0], GATHER_ROWS[0] + GATHER_ROWS[1]
    u0, z, xbc, cv, cg, dtr, v, gath1 = _in_proj_fwd(x, S["mix_norm_g"], gin, rider=_gather_rider(gath0, 0, r1))
    co, yc, gath2 = _conf_fwd(v, S["conf_dw_w"], S["conf_dw_b"], S["conf_ln_g"], S["conf_ln_b"],
                              rider=_gather_rider(gath1, r1, r2 - r1))
    pre = _ssd_conv_fwd(xbc, S["ssd_conv_w"], S["ssd_conv_b"])
    y, ys, sprev, gath = _ssd_fwd(pre, dtr, z, dtb, alog, dskip_e, S["ssd_norm_g"],
                                  rider=_gather_rider(gath2, r2, SLAB_A // 2 - r2))
    gath = _forward_to_sibling(gath)
    w_ple = jnp.concatenate([_ple_of_slab(gath[b]) for b in range(N_CHIPS)], axis=1)
    h1, u1 = _out_proj_fwd(x, ys, yc, gath, S["mlp_norm_g"])
    r, h2, u2 = _mlp_fwd(h1, u1, gath, S["ple_gate_norm_g"])
    loss, dh2, dh2b, dgp, dep, dg_fin, dg_ple, db_pg, dg_pg = _ple_loss(
        h2, u2, p, tgt, gath, S["b_ple_gate"], w_ple, S["ple_norm_g"], S["final_norm_g"], S["ple_gate_norm_g"])

    npg = D_MODEL // N_CHIPS
    ga = lax.empty((N_CHIPS, SLAB_A, D_MODEL), f32)
    ga = _weight_grad(u2, dgp, "dw_ple_gate", slab=ga, tk=npg, place=lambda i, j: (i, PG_OFF // npg, j))
    ga = _weight_grad(r, dh2b, "dw_down", square=True, slab=ga, place=lambda i, j: (i // 2, DOWN_OFF // 512 + i % 2, j))
    gw_ple = _weight_grad(p, dep, "dw_ple")
    dhp, dh1, dh1b, dg_mlp = _mlp_bwd(dh2, r, gath, h1, S["mlp_norm_g"])
    ga = _weight_grad(u1, dhp, "dw_up", slab=ga, place=lambda i, j: (j, UP_OFF // 512 + i, 0))
    ga = _weight_grad(ys, dh1b, "dw_out_ssd", slab=ga, place=lambda i, j: (i, OUT_OFF // 512, j))
    ga = _weight_grad(yc, dh1b, "dw_out_conf", slab=ga, place=lambda i, j: (2 + i, OUT_OFF // 512, j))
    n_ple = D_MODEL // N_CHIPS
    ple_rows = jnp.stack([_rows(gw_ple[:, b * n_ple:(b + 1) * n_ple]) for b in range(N_CHIPS)], axis=0)
    ga = lax.dynamic_update_slice(ga, ple_rows, (0, PLE_OFF, 0))
    dys, dco, dg_ln, db_ln, recv_a = _out_proj_bwd(dh1, gath, co, S["conf_ln_g"], S["conf_ln_b"], rider=_swap_rider(ga))
    ha = _chip_sum(cidx, ga, recv_a, "chip_sum_a")
    first = EXCHANGE_FIRST_ROWS
    dcv, dcg, dw_conf, db_conf, ici_a = _conf_conv_bwd(dco, v, S["conf_dw_w"], cv, cg,
                                                       rider=_exchange_rider(ha, 0, first))
    dz, dpre, ddtr, dg_ssdn, dd, dal, ddtb, ici_a = _ssd_bwd(
        dys, y, z, pre, dtr, sprev, dtb, alog, dskip_e, S["ssd_norm_g"],
        rider=_exchange_rider(ha, first, SLAB_A // 2 - first, recv=ici_a))
    dxbc, dw_sconv, db_sconv = _ssd_conv_bwd(dpre, xbc, S["ssd_conv_w"])
    gw_in = jnp.concatenate([
        _weight_grad(dz, u0, "dw_in_z"), _weight_grad(dxbc, u0, "dw_in_xbc"),
        _weight_grad(ddtr, u0, "dw_in_dt")[:SSD_HEADS],
        _weight_grad(dcv, u0, "dw_in_cv"), _weight_grad(dcg, u0, "dw_in_cg")], axis=0)
    gb = _shard_rows(gw_in)
    recv_b = _swap_halves(gb)
    hb = _chip_sum(cidx, gb, recv_b, "chip_sum_b")
    gx, dg_mix, ici_b = _in_proj_bwd(dz, dxbc, dcv, dcg, ddtr, gin, x, dh1, S["mix_norm_g"], rider=_exchange_rider(hb))
    small = {
        "mix_norm_g": dg_mix, "ssd_conv_w": dw_sconv, "ssd_conv_b": db_sconv, "dt_bias": ddtb, "A_log": dal, "D_skip": dd,
        "ssd_norm_g": dg_ssdn, "conf_dw_w": dw_conf, "conf_dw_b": db_conf, "conf_ln_g": dg_ln, "conf_ln_b": db_ln,
        "mlp_norm_g": dg_mlp, "ple_gate_norm_g": dg_pg, "b_ple_gate": db_pg, "ple_norm_g": dg_ple,
        "final_norm_g": dg_fin, "loss": loss,
    }
    return gx, (ga, recv_a, ici_a), (gb, recv_b, ici_b), small


def _rows(a):
    return a.reshape(-1, D_MODEL)


def _pad_rows(a, n):
    flat = a.reshape(-1)
    return jnp.pad(flat, (0, n * D_MODEL - flat.shape[0])).reshape(n, D_MODEL)


def _ple_of_slab(slab):
    return slab[PLE_OFF:PLE_OFF + PLE_ROWS].reshape(PLE_DIM, D_MODEL // N_CHIPS)


ROW_VEC = {"mix_norm_g": 0, "ssd_norm_g": 1, "conf_dw_b": 2, "conf_ln_g": 3, "conf_ln_b": 4, "mlp_norm_g": 5,
           "ple_gate_norm_g": 6, "b_ple_gate": 7, "ple_norm_g": 8, "final_norm_g": 9}
ROW_CONV_B = 10
ROW_HEADS = 12
ROW_CONV_W = 16
ROW_DW = 24
HEAD_LANES = {"dt_bias": 0, "A_log": 1, "D_skip": 2, "loss": 3}
SMALL_ORDER = ("mix_norm_g", "ssd_conv_w", "ssd_conv_b", "dt_bias", "A_log", "D_skip", "ssd_norm_g", "conf_dw_w",
               "conf_dw_b", "conf_ln_g", "conf_ln_b", "mlp_norm_g", "ple_gate_norm_g", "b_ple_gate", "ple_norm_g",
               "final_norm_g")
SPLIT = XBC_WIDTH - D_MODEL


def _pack_small(raw):
    names = list(ROW_VEC) + ["ssd_conv_b", "dt_bias", "A_log", "D_skip", "loss", "ssd_conv_w", "conf_dw_w"]

    def body(*refs):
        r = dict(zip(names, refs[:-1]))
        o_ref = refs[-1]
        o_ref[...] = jnp.zeros_like(o_ref)
        for n, row in ROW_VEC.items():
            o_ref[row:row + 1, :] = r[n][...]
        o_ref[ROW_CONV_B:ROW_CONV_B + 1, :] = r["ssd_conv_b"][:, 0:D_MODEL]
        o_ref[ROW_CONV_B + 1:ROW_CONV_B + 2, 0:SPLIT] = r["ssd_conv_b"][:, D_MODEL:]
        for n, j in HEAD_LANES.items():
            o_ref[ROW_HEADS:ROW_HEADS + 1, j * LANES:(j + 1) * LANES] = r[n][0:1, :]
        for k in range(SSD_CONV):
            o_ref[ROW_CONV_W + 2 * k:ROW_CONV_W + 2 * k + 1, :] = r["ssd_conv_w"][k:k + 1, 0:D_MODEL]
            o_ref[ROW_CONV_W + 2 * k + 1:ROW_CONV_W + 2 * k + 2, 0:SPLIT] = r["ssd_conv_w"][k:k + 1, D_MODEL:]
        o_ref[ROW_DW:ROW_DW + 32, :] = r["conf_dw_w"][...]

    return _pallas(
        body, name="pack_small", out_shape=jax.ShapeDtypeStruct((SMALL_ROWS, D_MODEL), f32),
    )(*[raw[n] for n in names])


def _adamw_small(cidx, tot, w, m, v):
    c1 = 1.0 - ADAM_B1 ** ADAM_STEP
    c2 = 1.0 - ADAM_B2 ** ADAM_STEP
    n_par = len(SMALL_ORDER)

    def shard(full, chip, width):
        out = full[:, 0:width]
        for b in range(1, N_CHIPS):
            out = jnp.where(chip == b, full[:, b * width:(b + 1) * width], out)
        return out

    def grad_of(n, t_ref, chip):
        if n in ROW_VEC:
            return t_ref[ROW_VEC[n]:ROW_VEC[n] + 1, :]
        if n == "ssd_conv_b":
            return jnp.concatenate([t_ref[ROW_CONV_B:ROW_CONV_B + 1, :], t_ref[ROW_CONV_B + 1:ROW_CONV_B + 2, 0:SPLIT]], axis=1)
        if n in HEAD_LANES:
            j = HEAD_LANES[n]
            return t_ref[ROW_HEADS:ROW_HEADS + 1, j * LANES:j * LANES + SSD_HEADS]
        if n == "ssd_conv_w":
            rows = [jnp.concatenate([t_ref[ROW_CONV_W + 2 * k:ROW_CONV_W + 2 * k + 1, :],
                                     t_ref[ROW_CONV_W + 2 * k + 1:ROW_CONV_W + 2 * k + 2, 0:SPLIT]], axis=1)
                    for k in range(SSD_CONV)]
            return shard(jnp.concatenate(rows, axis=0), chip, XBC_WIDTH // N_CHIPS)
        return shard(t_ref[ROW_DW:ROW_DW + CONF_KERNEL, :], chip, CONF_WIDTH // N_CHIPS)

    def body(c_ref, t_ref, *refs):
        ins, outs = refs[:3 * n_par], refs[3 * n_par:]
        chip = c_ref[1]
        for i, n in enumerate(SMALL_ORDER):
            w_ref, m_ref, v_ref = ins[3 * i:3 * i + 3]
            g_ref, d_ref, mo_ref, vo_ref = outs[4 * i:4 * i + 4]
            g = grad_of(n, t_ref, chip)
            m2 = ADAM_B1 * m_ref[...] + (1.0 - ADAM_B1) * g
            v2 = ADAM_B2 * v_ref[...] + (1.0 - ADAM_B2) * (g * g)
            g_ref[...] = g
            mo_ref[...] = m2
            vo_ref[...] = v2
            d_ref[...] = -ADAM_LR * ((m2 / c1) / (jnp.sqrt(v2 / c2) + ADAM_EPS) + ADAM_WD * w_ref[...])

    args, in_specs, out_specs, out_shape = [], [], [], []
    for n in SMALL_ORDER:
        shp = w[n].shape
        spec = pl.BlockSpec(shp, lambda i, c_ref: (0, 0))
        args += [w[n], m[n], v[n]]
        in_specs += [spec] * 3
        out_specs += [spec] * 4
        out_shape += [jax.ShapeDtypeStruct(shp, f32)] * 4
    outs = _pallas(
        body, name="adamw_small",
        grid_spec=pltpu.PrefetchScalarGridSpec(
            num_scalar_prefetch=1, grid=(1,),
            in_specs=[pl.BlockSpec(tot.shape, lambda i, c_ref: (0, 0))] + in_specs, out_specs=out_specs),
        out_shape=out_shape,
    )(cidx, tot, *args)
    grad, delta, new_m, new_v = {}, {}, {}, {}
    for i, n in enumerate(SMALL_ORDER):
        grad[n], delta[n], new_m[n], new_v[n] = outs[4 * i:4 * i + 4]
    return grad, delta, new_m, new_v


BIG = ("w_in", "w_out", "w_up", "w_down", "w_ple_gate", "w_ple")
BIG_A = (("w_up", UP_OFF), ("w_down", DOWN_OFF), ("w_out", OUT_OFF), ("w_ple_gate", PG_OFF))
WEIGHTS = ("mix_norm_g", "w_in", "ssd_conv_w", "ssd_conv_b", "dt_bias", "A_log", "D_skip", "ssd_norm_g", "conf_dw_w",
           "conf_dw_b", "conf_ln_g", "conf_ln_b", "w_out", "mlp_norm_g", "w_up", "w_down", "ple_gate_norm_g",
           "w_ple_gate", "b_ple_gate", "w_ple", "ple_norm_g", "final_norm_g")


def kernel(x, p, mix_norm_g, w_in, ssd_conv_w, ssd_conv_b, dt_bias, A_log, D_skip, ssd_norm_g, conf_dw_w, conf_dw_b, conf_ln_g, conf_ln_b, w_out, mlp_norm_g, w_up, w_down, ple_gate_norm_g, w_ple_gate, b_ple_gate, w_ple, ple_norm_g, final_norm_g, loss_target, m_mix_norm_g, m_w_in, m_ssd_conv_w, m_ssd_conv_b, m_dt_bias, m_A_log, m_D_skip, m_ssd_norm_g, m_conf_dw_w, m_conf_dw_b, m_conf_ln_g, m_conf_ln_b, m_w_out, m_mlp_norm_g, m_w_up, m_w_down, m_ple_gate_norm_g, m_w_ple_gate, m_b_ple_gate, m_w_ple, m_ple_norm_g, m_final_norm_g, v_mix_norm_g, v_w_in, v_ssd_conv_w, v_ssd_conv_b, v_dt_bias, v_A_log, v_D_skip, v_ssd_norm_g, v_conf_dw_w, v_conf_dw_b, v_conf_ln_g, v_conf_ln_b, v_w_out, v_mlp_norm_g, v_w_up, v_w_down, v_ple_gate_norm_g, v_w_ple_gate, v_b_ple_gate, v_w_ple, v_ple_norm_g, v_final_norm_g):
    w = dict(mix_norm_g=mix_norm_g, w_in=w_in, ssd_conv_w=ssd_conv_w, ssd_conv_b=ssd_conv_b, dt_bias=dt_bias, A_log=A_log,
             D_skip=D_skip, ssd_norm_g=ssd_norm_g, conf_dw_w=conf_dw_w, conf_dw_b=conf_dw_b, conf_ln_g=conf_ln_g,
             conf_ln_b=conf_ln_b, w_out=w_out, mlp_norm_g=mlp_norm_g, w_up=w_up, w_down=w_down,
             ple_gate_norm_g=ple_gate_norm_g, w_ple_gate=w_ple_gate, b_ple_gate=b_ple_gate, w_ple=w_ple,
             ple_norm_g=ple_norm_g, final_norm_g=final_norm_g)
    m = dict(mix_norm_g=m_mix_norm_g, w_in=m_w_in, ssd_conv_w=m_ssd_conv_w, ssd_conv_b=m_ssd_conv_b, dt_bias=m_dt_bias,
             A_log=m_A_log, D_skip=m_D_skip, ssd_norm_g=m_ssd_norm_g, conf_dw_w=m_conf_dw_w, conf_dw_b=m_conf_dw_b,
             conf_ln_g=m_conf_ln_g, conf_ln_b=m_conf_ln_b, w_out=m_w_out, mlp_norm_g=m_mlp_norm_g, w_up=m_w_up,
             w_down=m_w_down, ple_gate_norm_g=m_ple_gate_norm_g, w_ple_gate=m_w_ple_gate, b_ple_gate=m_b_ple_gate,
             w_ple=m_w_ple, ple_norm_g=m_ple_norm_g, final_norm_g=m_final_norm_g)
    v = dict(mix_norm_g=v_mix_norm_g, w_in=v_w_in, ssd_conv_w=v_ssd_conv_w, ssd_conv_b=v_ssd_conv_b, dt_bias=v_dt_bias,
             A_log=v_A_log, D_skip=v_D_skip, ssd_norm_g=v_ssd_norm_g, conf_dw_w=v_conf_dw_w, conf_dw_b=v_conf_dw_b,
             conf_ln_g=v_conf_ln_g, conf_ln_b=v_conf_ln_b, w_out=v_w_out, mlp_norm_g=v_mlp_norm_g, w_up=v_w_up,
             w_down=v_w_down, ple_gate_norm_g=v_ple_gate_norm_g, w_ple_gate=v_w_ple_gate, b_ple_gate=v_b_ple_gate,
             w_ple=v_w_ple, ple_norm_g=v_ple_norm_g, final_norm_g=v_final_norm_g)
    xi, yi, ci = lax.axis_index("x"), lax.axis_index("y"), lax.axis_index("c")
    chip = 2 * xi + yi

    slab = jnp.concatenate([w_up[0], w_down[0], w_out[0], w_ple_gate[0], _rows(w_ple[0])], axis=0).astype(bf16)
    gath0 = lax.dynamic_update_slice(lax.empty((N_CHIPS, SLAB_A, D_MODEL), bf16), slab[None], (chip, 0, 0))
    wt_shard = jnp.swapaxes(w_in, 1, 2).astype(bf16)
    wt_shard = jnp.pad(wt_shard, ((0, 0), (0, W_IN_ROWS_PAD - W_IN_ROWS), (0, 0)))
    gin0 = lax.dynamic_update_slice(lax.empty((N_CHIPS, W_IN_ROWS_PAD, D_MODEL), bf16), wt_shard, (chip, 0, 0))
    convw = _pad_rows(jnp.concatenate([ssd_conv_w[0].reshape(-1), conf_dw_w[0].reshape(-1)]), CONVW_ROWS)
    gin, cwg = _gather_weights([gin0], convw)
    n_sc = SSD_CONV * (XBC_WIDTH // N_CHIPS)
    n_cf = CONF_KERNEL * (CONF_WIDTH // N_CHIPS)
    S = {n: w[n][0] for n in ("mix_norm_g", "ssd_conv_b", "dt_bias", "A_log", "D_skip", "ssd_norm_g", "conf_dw_b",
                              "conf_ln_g", "conf_ln_b", "mlp_norm_g", "ple_gate_norm_g", "b_ple_gate", "ple_norm_g")}
    S = {n: a.reshape(1, -1) for n, a in S.items()}
    S["final_norm_g"] = final_norm_g.reshape(1, -1)
    S["ssd_conv_w"] = jnp.concatenate(
        [cwg[b].reshape(-1)[:n_sc].reshape(SSD_CONV, XBC_WIDTH // N_CHIPS) for b in range(N_CHIPS)], axis=1)
    S["conf_dw_w"] = jnp.concatenate(
        [cwg[b].reshape(-1)[n_sc:n_sc + n_cf].reshape(CONF_KERNEL, CONF_WIDTH // N_CHIPS) for b in range(N_CHIPS)], axis=1)

    cidx = jnp.stack([ci, chip]).astype(jnp.int32)
    grad_x, (ga, recv_a, ici_a), (gb, recv_b, ici_b), gsmall = _local_step(
        x[0], p[0, 0], loss_target[0], gath0, cidx, gin, S)

    all_small = _gather_small(_pack_small(gsmall))
    ra = _final_sum(cidx, ga, recv_a, ici_a, "final_sum_a")
    rb = _final_sum(cidx, gb, recv_b, ici_b, "final_sum_b")
    ra, rb = _join_halves(ra, rb)
    tot_small = _sum_small(all_small)

    loss = tot_small[ROW_HEADS, HEAD_LANES["loss"] * LANES]

    two_d = lambda a: a.reshape(a.shape[-2:]) if a.ndim > 1 else a.reshape(1, -1)
    small_w, small_m, small_v = ({n: two_d(d[n]) for n in SMALL_ORDER} for d in (w, m, v))
    grads, delta, new_m, new_v = _adamw_small(cidx, tot_small, small_w, small_m, small_v)
    g_in_t = rb[:W_IN_ROWS]
    grads["w_ple"] = _ple_of_slab(ra)
    grads["w_in"] = jnp.swapaxes(g_in_t, 0, 1)
    for n, off in BIG_A:
        grads[n] = ra[off:off + w[n].shape[1]]
        delta[n], new_m[n], new_v[n] = _adamw(w[n][0], ra, m[n][0], v[n][0], "adamw_" + n, g_off=off)
    delta["w_ple"], new_m["w_ple"], new_v["w_ple"] = _adamw(w_ple[0], grads["w_ple"], m_w_ple[0], v_w_ple[0], "adamw_w_ple")
    tr_ = lambda a: jnp.swapaxes(a[0], 0, 1)
    d_, m_, v_ = _adamw(tr_(w_in), g_in_t, tr_(m_w_in), tr_(v_w_in), "adamw_w_in", by_columns=True)
    delta["w_in"], new_m["w_in"], new_v["w_in"] = (jnp.swapaxes(a, 0, 1) for a in (d_, m_, v_))

    shaped = lambda d: [d[n].reshape(w[n].shape) for n in WEIGHTS]
    return (loss, grad_x[None], *shaped(grads), *shaped(delta), *shaped(new_m), *shaped(new_v))
```
